```python
import math
import jax, jax.numpy as jnp
from jax import lax
import numpy as np

D_MODEL = 1024
BATCH = 8
SEQ = 2048
DEPTH = 1

ATTN_HEADS = 8
ATTN_KV_HEADS = 2
HEAD_DIM = 64
ATTN_WIDTH = ATTN_HEADS * HEAD_DIM
KV_WIDTH = ATTN_KV_HEADS * HEAD_DIM
WINDOW = 128
BLOCK = 128
SSM_CH_PER_GROUP = 16
SSM_WIDTH = D_MODEL - ATTN_WIDTH
SSM_GROUPS = SSM_WIDTH // SSM_CH_PER_GROUP
SSM_STATE = 64
DT_MIN = 1e-3
DT_MAX = 1e-1
MIX_WIDTH = ATTN_WIDTH + SSM_WIDTH
IN_WIDTH = ATTN_WIDTH + 2 * KV_WIDTH + SSM_WIDTH
D_FF = 2816
EPS = 1e-6
NEG_INF = -1e30
LAMBDA_RE_MAX = -1e-4

kernel_name = "hymba_swa_s5_macaron_block"


def _alibi_slopes(n_heads):
    return jnp.asarray(2.0 ** (-8.0 * (np.arange(n_heads) + 1) / n_heads), dtype=jnp.float32)


def _rmsnorm(x, g):
    x32 = x.astype(jnp.float32)
    y = x32 * lax.rsqrt(jnp.mean(x32 * x32, axis=-1, keepdims=True) + EPS)
    return (y * g.astype(jnp.float32)).astype(x.dtype)


def _swiglu(x, w_gate, w_up, w_down):
    return (jax.nn.silu(x @ w_gate) * (x @ w_up)) @ w_down


def _window_attention(q, k, v, sinks):
    b, l = q.shape[0], q.shape[1]
    nb = l // BLOCK
    gq = ATTN_HEADS // ATTN_KV_HEADS
    qb = q.reshape(b, nb, BLOCK, ATTN_KV_HEADS, gq, HEAD_DIM)

    def band(t):
        tp = jnp.pad(t, ((0, 0), (BLOCK, BLOCK), (0, 0), (0, 0)))
        tp = tp.reshape(b, nb + 2, BLOCK, ATTN_KV_HEADS, HEAD_DIM)
        return jnp.concatenate([tp[:, :-2], tp[:, 1:-1], tp[:, 2:]], axis=2)

    kw, vw = band(k), band(v)
    scores = jnp.einsum('bnqkgd,bnskd->bnkgqs', qb, kw).astype(jnp.float32) * (HEAD_DIM ** -0.5)
    qi = jnp.arange(BLOCK)[:, None]
    kj = jnp.arange(3 * BLOCK)[None, :]
    rel = kj - BLOCK - qi
    key_pos = jnp.arange(nb)[:, None, None] * BLOCK - BLOCK + kj[None]
    valid = (jnp.abs(rel) <= WINDOW)[None] & (key_pos >= 0) & (key_pos < l)
    slopes = _alibi_slopes(ATTN_HEADS).reshape(ATTN_KV_HEADS, gq)
    alibi = -slopes[:, :, None, None] * jnp.abs(rel).astype(jnp.float32)
    scores = jnp.where(valid[None, :, None, None], scores + alibi, NEG_INF)
    sink = jnp.broadcast_to(sinks.astype(jnp.float32).reshape(1, 1, ATTN_KV_HEADS, gq, 1, 1),
                            scores.shape[:-1] + (1,))
    probs = jax.nn.softmax(jnp.concatenate([scores, sink], axis=-1), axis=-1)[..., :-1]
    out = jnp.einsum('bnkgqs,bnskd->bnqkgd', probs.astype(v.dtype), vw)
    return out.reshape(b, l, ATTN_WIDTH)


def _s5_direction(u, lam_re, lam_im, log_dt, b_re, b_im, c_re, c_im, reverse):
    f32 = jnp.float32
    lr = jnp.minimum(lam_re.astype(f32), LAMBDA_RE_MAX)
    li = lam_im.astype(f32)
    dt = jnp.exp(log_dt.astype(f32))[:, None]
    mag = jnp.exp(lr * dt)
    a_re = mag * jnp.cos(li * dt)
    a_im = mag * jnp.sin(li * dt)
    den = lr * lr + li * li
    coef_re = ((a_re - 1.0) * lr + a_im * li) / den
    coef_im = (a_im * lr - (a_re - 1.0) * li) / den
    br, bi = b_re.astype(f32), b_im.astype(f32)
    bb_re = coef_re[..., None] * br - coef_im[..., None] * bi
    bb_im = coef_re[..., None] * bi + coef_im[..., None] * br
    bu_re = jnp.einsum('blgh,gph->blgp', u, bb_re)
    bu_im = jnp.einsum('blgh,gph->blgp', u, bb_im)
    ar = jnp.broadcast_to(a_re, bu_re.shape)
    ai = jnp.broadcast_to(a_im, bu_re.shape)

    def combine(e1, e2):
        ar1, ai1, xr1, xi1 = e1
        ar2, ai2, xr2, xi2 = e2
        return (ar2 * ar1 - ai2 * ai1,
                ar2 * ai1 + ai2 * ar1,
                ar2 * xr1 - ai2 * xi1 + xr2,
                ar2 * xi1 + ai2 * xr1 + xi2)

    _, _, xr, xi = lax.associative_scan(combine, (ar, ai, bu_re, bu_im), reverse=reverse, axis=1)
    return (jnp.einsum('blgp,ghp->blgh', xr, c_re.astype(f32))
            - jnp.einsum('blgp,ghp->blgh', xi, c_im.astype(f32)))


def _s5_mixer(u, lam_re, lam_im, log_dt, b_re, b_im, c_re, c_im, d_skip, glu_w, glu_b):
    b, l, _ = u.shape
    ug = u.astype(jnp.float32).reshape(b, l, SSM_GROUPS, SSM_CH_PER_GROUP)
    y = d_skip.astype(jnp.float32) * ug
    for direction in range(2):
        y = y + _s5_direction(ug, lam_re[direction], lam_im[direction], log_dt[direction],
                              b_re[direction], b_im[direction], c_re[direction], c_im[direction],
                              reverse=(direction == 1))
    y = jax.nn.gelu(y.reshape(b, l, SSM_WIDTH)).astype(u.dtype)
    return y * jax.nn.sigmoid(y @ glu_w + glu_b)


def setup_inputs(seed: int = 0) -> dict:
    key = jax.random.key(seed)
    ks = iter(jax.random.split(key, 40))
    f32 = jnp.float32

    def nrm(shape, scale):
        return jax.random.normal(next(ks), shape, f32) * scale

    def gain(shape):
        return 1.0 + 0.01 * jax.random.normal(next(ks), shape, f32)

    L, G, P, Hc = DEPTH, SSM_GROUPS, SSM_STATE, SSM_CH_PER_GROUP
    lam_im_init = jnp.pi * jnp.arange(P, dtype=f32)
    return {
        "x": jax.random.normal(next(ks), (BATCH, SEQ, D_MODEL), f32),
        "norm_ffn1": gain((L, D_MODEL)),
        "ffn1_w_gate": nrm((L, D_MODEL, D_FF), D_MODEL ** -0.5),
        "ffn1_w_up": nrm((L, D_MODEL, D_FF), D_MODEL ** -0.5),
        "ffn1_w_down": nrm((L, D_FF, D_MODEL), D_FF ** -0.5),
        "norm_mix": gain((L, D_MODEL)),
        "w_in": nrm((L, D_MODEL, IN_WIDTH), D_MODEL ** -0.5),
        "attn_sinks": nrm((L, ATTN_HEADS), 0.5),
        "ssm_lambda_re": -0.5 + nrm((L, 2, G, P), 0.01),
        "ssm_lambda_im": lam_im_init + nrm((L, 2, G, P), 0.01),
        "ssm_log_dt": jax.random.uniform(next(ks), (L, 2, G), f32,
                                         minval=math.log(DT_MIN), maxval=math.log(DT_MAX)),
        "ssm_b_re": nrm((L, 2, G, P, Hc), (2.0 * Hc) ** -0.5),
        "ssm_b_im": nrm((L, 2, G, P, Hc), (2.0 * Hc) ** -0.5),
        "ssm_c_re": nrm((L, 2, G, Hc, P), (2.0 * P) ** -0.5),
        "ssm_c_im": nrm((L, 2, G, Hc, P), (2.0 * P) ** -0.5),
        "ssm_d": 1.0 + nrm((L, G, Hc), 0.1),
        "ssm_glu_w": nrm((L, SSM_WIDTH, SSM_WIDTH), SSM_WIDTH ** -0.5),
        "ssm_glu_b": nrm((L, SSM_WIDTH), 0.01),
        "attn_out_norm": gain((L, ATTN_WIDTH)),
        "ssm_out_norm": gain((L, SSM_WIDTH)),
        "w_out": nrm((L, MIX_WIDTH, D_MODEL), MIX_WIDTH ** -0.5),
        "norm_ffn2": gain((L, D_MODEL)),
        "ffn2_w_gate": nrm((L, D_MODEL, D_FF), D_MODEL ** -0.5),
        "ffn2_w_up": nrm((L, D_MODEL, D_FF), D_MODEL ** -0.5),
        "ffn2_w_down": nrm((L, D_FF, D_MODEL), D_FF ** -0.5),
        "final_norm": gain((D_MODEL,)),
    }


def reference(x, norm_ffn1, ffn1_w_gate, ffn1_w_up, ffn1_w_down, norm_mix, w_in, attn_sinks,
              ssm_lambda_re, ssm_lambda_im, ssm_log_dt, ssm_b_re, ssm_b_im, ssm_c_re, ssm_c_im,
              ssm_d, ssm_glu_w, ssm_glu_b, attn_out_norm, ssm_out_norm, w_out,
              norm_ffn2, ffn2_w_gate, ffn2_w_up, ffn2_w_down, final_norm):
    b, l, _ = x.shape
    for layer in range(DEPTH):
        x = x + 0.5 * _swiglu(_rmsnorm(x, norm_ffn1[layer]),
                              ffn1_w_gate[layer], ffn1_w_up[layer], ffn1_w_down[layer])
        h = _rmsnorm(x, norm_mix[layer])
        proj = h @ w_in[layer]
        q, k, v, u = jnp.split(proj, [ATTN_WIDTH, ATTN_WIDTH + KV_WIDTH,
                                      ATTN_WIDTH + 2 * KV_WIDTH], axis=-1)
        attn = _window_attention(q.reshape(b, l, ATTN_HEADS, HEAD_DIM),
                                 k.reshape(b, l, ATTN_KV_HEADS, HEAD_DIM),
                                 v.reshape(b, l, ATTN_KV_HEADS, HEAD_DIM),
                                 attn_sinks[layer])
        ssm = _s5_mixer(u, ssm_lambda_re[layer], ssm_lambda_im[layer], ssm_log_dt[layer],
                        ssm_b_re[layer], ssm_b_im[layer], ssm_c_re[layer], ssm_c_im[layer],
                        ssm_d[layer], ssm_glu_w[layer], ssm_glu_b[layer])
        mixed = jnp.concatenate([_rmsnorm(attn, attn_out_norm[layer]),
                                 _rmsnorm(ssm, ssm_out_norm[layer])], axis=-1)
        x = x + mixed @ w_out[layer]
        x = x + 0.5 * _swiglu(_rmsnorm(x, norm_ffn2[layer]),
                              ffn2_w_gate[layer], ffn2_w_up[layer], ffn2_w_down[layer])
    return _rmsnorm(x, final_norm)
```

```python
import functools

import jax
import jax.numpy as jnp
import numpy as np
from jax import lax
from jax.experimental import pallas as pl
from jax.experimental.pallas import tpu as pltpu

F32 = jnp.float32
BF16 = jnp.bfloat16

D_MODEL = 1024
ATTN_HEADS = 8
ATTN_KV_HEADS = 2
Q_PER_KV = ATTN_HEADS // ATTN_KV_HEADS
HEAD_DIM = 64
ATTN_WIDTH = ATTN_HEADS * HEAD_DIM
KV_WIDTH = ATTN_KV_HEADS * HEAD_DIM
WINDOW = 128
BLOCK = 128
SSM_CH = 16
SSM_WIDTH = D_MODEL - ATTN_WIDTH
SSM_GROUPS = SSM_WIDTH // SSM_CH
SSM_STATE = 64
IN_WIDTH = ATTN_WIDTH + 2 * KV_WIDTH + SSM_WIDTH
D_FF = 2816
EPS = 1e-6
NEG_INF = -1e30
LAMBDA_RE_MAX = -1e-4

VMEM_LIMIT_BYTES = 56 * 1024 * 1024

TOKEN_TILE = 1024
FF_TILE = 256
SSM_CHUNK = 16
SSM_GROUPS_PER_STEP = 4


def _rms(x):
    return x * lax.rsqrt(jnp.mean(x * x, axis=-1, keepdims=True) + EPS)


def _ffn_kernel(x_ref, gain_ref, wgu_ref, wd_ref, fgain_ref, o_ref, hn_ref, *, final_norm):
    j = pl.program_id(1)
    last = pl.num_programs(1) - 1

    @pl.when(j == 0)
    def _():
        hn_ref[...] = (_rms(x_ref[...]) * gain_ref[...]).astype(BF16)

    gu = jnp.dot(hn_ref[...], wgu_ref[0], preferred_element_type=F32)
    g = gu[:, :FF_TILE]
    u = gu[:, FF_TILE:]
    act = (g * jax.nn.sigmoid(g) * u).astype(BF16)
    d = jnp.dot(act, wd_ref[...], preferred_element_type=F32)

    @pl.when(j == 0)
    def _():
        o_ref[...] = d

    @pl.when(j > 0)
    def _():
        o_ref[...] += d

    @pl.when(j == last)
    def _():
        y = x_ref[...] + 0.5 * o_ref[...]
        if final_norm:
            y = _rms(y) * fgain_ref[...]
        o_ref[...] = y


def _ffn(x, gain, w_gate, w_up, w_down, final_gain, final_norm):
    t = x.shape[0]
    nj = D_FF // FF_TILE
    wgu = jnp.concatenate(
        [w_gate.reshape(D_MODEL, nj, FF_TILE), w_up.reshape(D_MODEL, nj, FF_TILE)], axis=-1
    ).transpose(1, 0, 2).astype(BF16)
    wd = w_down.astype(BF16)
    return pl.pallas_call(
        functools.partial(_ffn_kernel, final_norm=final_norm),
        grid=(t // TOKEN_TILE, nj),
        in_specs=[
            pl.BlockSpec((TOKEN_TILE, D_MODEL), lambda i, j: (i, 0)),
            pl.BlockSpec((1, D_MODEL), lambda i, j: (0, 0)),
            pl.BlockSpec((1, D_MODEL, 2 * FF_TILE), lambda i, j: (j, 0, 0)),
            pl.BlockSpec((FF_TILE, D_MODEL), lambda i, j: (j, 0)),
            pl.BlockSpec((1, D_MODEL), lambda i, j: (0, 0)),
        ],
        out_specs=pl.BlockSpec((TOKEN_TILE, D_MODEL), lambda i, j: (i, 0)),
        out_shape=jax.ShapeDtypeStruct((t, D_MODEL), F32),
        scratch_shapes=[pltpu.VMEM((TOKEN_TILE, D_MODEL), BF16)],
        compiler_params=pltpu.CompilerParams(
            dimension_semantics=("parallel", "arbitrary"), vmem_limit_bytes=VMEM_LIMIT_BYTES),
    )(x, gain.reshape(1, D_MODEL), wgu, wd, final_gain.reshape(1, D_MODEL))


def _in_proj_kernel(x_ref, gain_ref, w_ref, q_ref, k_ref, v_ref, u_ref):
    hn = (_rms(x_ref[...]) * gain_ref[...]).astype(BF16)
    proj = jnp.dot(hn, w_ref[...], preferred_element_type=F32)
    q_ref[...] = proj[:, :ATTN_WIDTH].astype(BF16)
    k_ref[...] = proj[:, ATTN_WIDTH:ATTN_WIDTH + KV_WIDTH].astype(BF16)
    v_ref[...] = proj[:, ATTN_WIDTH + KV_WIDTH:ATTN_WIDTH + 2 * KV_WIDTH].astype(BF16)
    u_ref[...] = proj[:, ATTN_WIDTH + 2 * KV_WIDTH:].astype(BF16)


def _in_proj(x, gain, w_in):
    t = x.shape[0]
    row = lambda width: pl.BlockSpec((TOKEN_TILE, width), lambda i: (i, 0))
    return pl.pallas_call(
        _in_proj_kernel,
        grid=(t // TOKEN_TILE,),
        in_specs=[
            row(D_MODEL),
            pl.BlockSpec((1, D_MODEL), lambda i: (0, 0)),
            pl.BlockSpec((D_MODEL, IN_WIDTH), lambda i: (0, 0)),
        ],
        out_specs=[row(ATTN_WIDTH), row(KV_WIDTH), row(KV_WIDTH), row(SSM_WIDTH)],
        out_shape=[
            jax.ShapeDtypeStruct((t, ATTN_WIDTH), BF16),
            jax.ShapeDtypeStruct((t, KV_WIDTH), BF16),
            jax.ShapeDtypeStruct((t, KV_WIDTH), BF16),
            jax.ShapeDtypeStruct((t, SSM_WIDTH), BF16),
        ],
        compiler_params=pltpu.CompilerParams(
            dimension_semantics=("parallel",), vmem_limit_bytes=VMEM_LIMIT_BYTES),
    )(x, gain.reshape(1, D_MODEL), w_in.astype(BF16))


def _attn_kernel(sink_ref, q_ref, kp_ref, kc_ref, kn_ref, vp_ref, vc_ref, vn_ref, o_ref, *, seq):
    n = pl.program_id(1)
    qi = lax.broadcasted_iota(jnp.int32, (BLOCK, 3 * BLOCK), 0)
    kj = lax.broadcasted_iota(jnp.int32, (BLOCK, 3 * BLOCK), 1)
    rel = jnp.abs(kj - BLOCK - qi)
    key_pos = kj + (n - 1) * BLOCK
    valid = (rel <= WINDOW) & (key_pos >= 0) & (key_pos < seq)
    dist = rel.astype(F32)

    outs = []
    for kh in range(ATTN_KV_HEADS):
        cols = slice(kh * HEAD_DIM, (kh + 1) * HEAD_DIM)
        kcat = jnp.concatenate([kp_ref[:, cols], kc_ref[:, cols], kn_ref[:, cols]], axis=0)
        vcat = jnp.concatenate([vp_ref[:, cols], vc_ref[:, cols], vn_ref[:, cols]], axis=0)
        heads = [kh * Q_PER_KV + g for g in range(Q_PER_KV)]
        qs = jnp.concatenate([q_ref[:, h * HEAD_DIM:(h + 1) * HEAD_DIM] for h in heads], axis=0)
        s_all = lax.dot_general(qs, kcat, (((1,), (1,)), ((), ())),
                                preferred_element_type=F32) * (HEAD_DIM ** -0.5)
        for g, h in enumerate(heads):
            slope = float(2.0 ** (-8.0 * (h + 1) / ATTN_HEADS))
            s = s_all[g * BLOCK:(g + 1) * BLOCK]
            s = jnp.where(valid, s - slope * dist, NEG_INF)
            sink = sink_ref[h]
            m = jnp.maximum(jnp.max(s, axis=-1, keepdims=True), sink)
            e = jnp.exp(s - m)
            den = jnp.sum(e, axis=-1, keepdims=True) + jnp.exp(sink - m)
            pv = jnp.dot(e.astype(BF16), vcat, preferred_element_type=F32)
            outs.append(pv / den)
    o_ref[...] = jnp.concatenate(outs, axis=-1).astype(o_ref.dtype)


def _attention(q, k, v, sinks):
    b, seq, _ = q.shape
    nb = seq // BLOCK
    kv_spec = lambda f: pl.BlockSpec((None, BLOCK, KV_WIDTH), f)
    prev = lambda bi, n: (bi, jnp.maximum(n - 1, 0), 0)
    cur = lambda bi, n: (bi, n, 0)
    nxt = lambda bi, n: (bi, jnp.minimum(n + 1, nb - 1), 0)
    return pl.pallas_call(
        functools.partial(_attn_kernel, seq=seq),
        grid=(b, nb),
        in_specs=[
            pl.BlockSpec(memory_space=pltpu.SMEM),
            pl.BlockSpec((None, BLOCK, ATTN_WIDTH), cur),
            kv_spec(prev), kv_spec(cur), kv_spec(nxt),
            kv_spec(prev), kv_spec(cur), kv_spec(nxt),
        ],
        out_specs=pl.BlockSpec((None, BLOCK, ATTN_WIDTH), cur),
        out_shape=jax.ShapeDtypeStruct((b, seq, ATTN_WIDTH), BF16),
        compiler_params=pltpu.CompilerParams(
            dimension_semantics=("parallel", "parallel"), vmem_limit_bytes=VMEM_LIMIT_BYTES),
    )(sinks.astype(F32), q, k, k, k, v, v, v)


def _ssm_kernel(u_ref, ein_ref, toep_ref, eout_ref, a_ref, y_ref, s_ref, x_ref, *, batch, n_chunks):
    gs = u_ref.shape[0]
    half = SSM_STATE
    for gi in range(gs):
        s_ref[gi] = jnp.dot(u_ref[gi], ein_ref[gi], preferred_element_type=F32)

    zeros = jnp.zeros((batch, half), F32)
    last_rows = pl.ds((n_chunks - 1) * batch, batch)
    for gi in range(gs):
        x_ref[gi, 0:batch, 0:half] = zeros
        x_ref[gi, 0:batch, 2 * half:3 * half] = zeros
        x_ref[gi, last_rows, half:2 * half] = zeros
        x_ref[gi, last_rows, 3 * half:4 * half] = zeros

    fwd_lane = lax.broadcasted_iota(jnp.int32, (batch, 2 * half), 1) < half

    def step(k, carry):
        rf = pl.multiple_of(k * batch, batch)
        rb = pl.multiple_of((n_chunks - 1 - k) * batch, batch)
        new = []
        for gi in range(gs):
            xr, xi = carry[gi]
            sre = jnp.where(fwd_lane, s_ref[gi, pl.ds(rf, batch), 0:2 * half],
                            s_ref[gi, pl.ds(rb, batch), 0:2 * half])
            sim = jnp.where(fwd_lane, s_ref[gi, pl.ds(rf, batch), 2 * half:4 * half],
                            s_ref[gi, pl.ds(rb, batch), 2 * half:4 * half])
            ar = a_ref[gi, 0:1, :]
            ai = a_ref[gi, 1:2, :]
            nr = ar * xr - ai * xi + sre
            ni = ar * xi + ai * xr + sim
            x_ref[gi, pl.ds(rf + batch, batch), 0:half] = nr[:, 0:half]
            x_ref[gi, pl.ds(rf + batch, batch), 2 * half:3 * half] = ni[:, 0:half]
            x_ref[gi, pl.ds(rb - batch, batch), half:2 * half] = nr[:, half:2 * half]
            x_ref[gi, pl.ds(rb - batch, batch), 3 * half:4 * half] = ni[:, half:2 * half]
            new.append((nr, ni))
        return tuple(new)

    init = tuple((jnp.zeros((batch, 2 * half), F32), jnp.zeros((batch, 2 * half), F32))
                 for _ in range(gs))
    lax.fori_loop(0, n_chunks - 1, step, init)

    for gi in range(gs):
        y = jnp.dot(u_ref[gi], toep_ref[gi], preferred_element_type=F32)
        y += jnp.dot(x_ref[gi].astype(BF16), eout_ref[gi], preferred_element_type=F32)
        y_ref[gi] = y.astype(y_ref.dtype)


def _ssm_operators(lam_re, lam_im, log_dt, b_re, b_im, c_re, c_im, d_skip):
    hi = lax.Precision.HIGHEST
    q = SSM_CHUNK
    g, p, hc = SSM_GROUPS, SSM_STATE, SSM_CH
    lr = jnp.minimum(lam_re.astype(F32), LAMBDA_RE_MAX)
    li = lam_im.astype(F32)
    dt = jnp.exp(log_dt.astype(F32))[..., None]
    mag = jnp.exp(lr * dt)
    a_r = mag * jnp.cos(li * dt)
    a_i = mag * jnp.sin(li * dt)
    den = lr * lr + li * li
    coef_r = ((a_r - 1.0) * lr + a_i * li) / den
    coef_i = (a_i * lr - (a_r - 1.0) * li) / den
    br, bi = b_re.astype(F32), b_im.astype(F32)
    bb_r = coef_r[..., None] * br - coef_i[..., None] * bi
    bb_i = coef_r[..., None] * bi + coef_i[..., None] * br
    cr, ci = c_re.astype(F32), c_im.astype(F32)

    pr, pi = [jnp.ones_like(a_r)], [jnp.zeros_like(a_r)]
    for _ in range(q):
        r, i = pr[-1], pi[-1]
        pr.append(r * a_r - i * a_i)
        pi.append(r * a_i + i * a_r)
    pr, pi = jnp.stack(pr), jnp.stack(pi)

    pb_r = pr[..., None] * bb_r - pi[..., None] * bb_i
    pb_i = pr[..., None] * bb_i + pi[..., None] * bb_r
    cp_r = cr * pr[:, :, :, None, :] - ci * pi[:, :, :, None, :]
    cp_i = cr * pi[:, :, :, None, :] + ci * pr[:, :, :, None, :]

    kern = (jnp.einsum('xghp,dxgpk->xgdhk', cr, pb_r, precision=hi)
            - jnp.einsum('xghp,dxgpk->xgdhk', ci, pb_i, precision=hi))
    k0 = kern[0, :, 0] + kern[1, :, 0] + jnp.eye(hc, dtype=F32) * d_skip.astype(F32)[:, :, None]
    kfull = jnp.concatenate([kern[1, :, 1:q][:, ::-1], k0[:, None], kern[0, :, 1:q]], axis=1)
    lag = np.arange(q)[None, :] - np.arange(q)[:, None] + q - 1
    toep = kfull[:, lag]
    toep = toep.transpose(0, 1, 4, 2, 3).reshape(g, q * hc, q * hc)

    def e_in(pb, direction):
        sel = pb[:q, 0][::-1] if direction == 0 else pb[:q, 1]
        return sel.transpose(1, 0, 3, 2).reshape(g, q * hc, p)

    ein = jnp.concatenate([e_in(pb_r, 0), e_in(pb_r, 1), e_in(pb_i, 0), e_in(pb_i, 1)], axis=-1)

    def e_out(cp, direction):
        sel = cp[1:q + 1, 0] if direction == 0 else cp[1:q + 1, 1][::-1]
        return sel.transpose(1, 3, 0, 2).reshape(g, p, q * hc)

    eout = jnp.concatenate([e_out(cp_r, 0), e_out(cp_r, 1), -e_out(cp_i, 0), -e_out(cp_i, 1)],
                           axis=1)
    a_q = jnp.zeros((g, 8, 2 * p), F32)
    a_q = a_q.at[:, 0].set(jnp.concatenate([pr[q, 0], pr[q, 1]], axis=-1))
    a_q = a_q.at[:, 1].set(jnp.concatenate([pi[q, 0], pi[q, 1]], axis=-1))
    return ein.astype(BF16), toep.astype(BF16), eout.astype(BF16), a_q


def _ssm(u, lam_re, lam_im, log_dt, b_re, b_im, c_re, c_im, d_skip):
    b, seq, _ = u.shape
    q, g, hc = SSM_CHUNK, SSM_GROUPS, SSM_CH
    nc = seq // q
    rows, width = nc * b, q * hc
    ein, toep, eout, a_q = _ssm_operators(lam_re, lam_im, log_dt, b_re, b_im, c_re, c_im, d_skip)
    ug = u.reshape(b, nc, q, g, hc).transpose(3, 1, 0, 2, 4).reshape(g, rows, width)
    gs = SSM_GROUPS_PER_STEP
    mat = lambda r, c: pl.BlockSpec((gs, r, c), lambda s: (s, 0, 0))
    yg = pl.pallas_call(
        functools.partial(_ssm_kernel, batch=b, n_chunks=nc),
        grid=(g // gs,),
        in_specs=[mat(rows, width), mat(width, 4 * SSM_STATE), mat(width, width),
                  mat(4 * SSM_STATE, width), mat(8, 2 * SSM_STATE)],
        out_specs=mat(rows, width),
        out_shape=jax.ShapeDtypeStruct((g, rows, width), BF16),
        scratch_shapes=[pltpu.VMEM((gs, rows, 4 * SSM_STATE), F32),
                        pltpu.VMEM((gs, rows, 4 * SSM_STATE), F32)],
        compiler_params=pltpu.CompilerParams(
            dimension_semantics=("parallel",), vmem_limit_bytes=VMEM_LIMIT_BYTES),
    )(ug, ein, toep, eout, a_q)
    return yg.reshape(g, nc, b, q, hc).transpose(2, 1, 3, 0, 4).reshape(b * seq, g * hc)


def _mix_kernel(x_ref, attn_ref, ssm_ref, gw_ref, gb_ref, ga_ref, gs_ref, wo_ref, o_ref):
    y = jax.nn.gelu(ssm_ref[...].astype(F32))
    z = jnp.dot(y.astype(BF16), gw_ref[...], preferred_element_type=F32) + gb_ref[...]
    s = y * jax.nn.sigmoid(z)
    sn = _rms(s) * gs_ref[...]
    an = _rms(attn_ref[...].astype(F32)) * ga_ref[...]
    mixed = jnp.concatenate([an, sn], axis=-1).astype(BF16)
    o_ref[...] = x_ref[...] + jnp.dot(mixed, wo_ref[...], preferred_element_type=F32)


def _mix_out(x, attn, ssm_pre, glu_w, glu_b, attn_gain, ssm_gain, w_out):
    t = x.shape[0]
    row = lambda width: pl.BlockSpec((TOKEN_TILE, width), lambda i: (i, 0))
    full = lambda r, c: pl.BlockSpec((r, c), lambda i: (0, 0))
    return pl.pallas_call(
        _mix_kernel,
        grid=(t // TOKEN_TILE,),
        in_specs=[row(D_MODEL), row(ATTN_WIDTH), row(SSM_WIDTH),
                  full(SSM_WIDTH, SSM_WIDTH), full(1, SSM_WIDTH),
                  full(1, ATTN_WIDTH), full(1, SSM_WIDTH), full(D_MODEL, D_MODEL)],
        out_specs=row(D_MODEL),
        out_shape=jax.ShapeDtypeStruct((t, D_MODEL), F32),
        compiler_params=pltpu.CompilerParams(
            dimension_semantics=("parallel",), vmem_limit_bytes=VMEM_LIMIT_BYTES),
    )(x, attn, ssm_pre, glu_w.astype(BF16), glu_b.reshape(1, -1).astype(F32),
      attn_gain.reshape(1, -1).astype(F32), ssm_gain.reshape(1, -1).astype(F32),
      w_out.astype(BF16))


def kernel(x, norm_ffn1, ffn1_w_gate, ffn1_w_up, ffn1_w_down, norm_mix, w_in, attn_sinks,
           ssm_lambda_re, ssm_lambda_im, ssm_log_dt, ssm_b_re, ssm_b_im, ssm_c_re, ssm_c_im,
           ssm_d, ssm_glu_w, ssm_glu_b, attn_out_norm, ssm_out_norm, w_out,
           norm_ffn2, ffn2_w_gate, ffn2_w_up, ffn2_w_down, final_norm):
    b, seq, d = x.shape
    depth = norm_ffn1.shape[0]
    assert d == D_MODEL and seq % BLOCK == 0 and (b * seq) % TOKEN_TILE == 0
    h = x.reshape(b * seq, d).astype(F32)
    for l in range(depth):
        h = _ffn(h, norm_ffn1[l], ffn1_w_gate[l], ffn1_w_up[l], ffn1_w_down[l],
                 final_norm, final_norm=False)
        q, k, v, u = _in_proj(h, norm_mix[l], w_in[l])
        attn = _attention(q.reshape(b, seq, -1), k.reshape(b, seq, -1), v.reshape(b, seq, -1),
                          attn_sinks[l])
        ssm_pre = _ssm(u.reshape(b, seq, -1), ssm_lambda_re[l], ssm_lambda_im[l], ssm_log_dt[l],
                       ssm_b_re[l], ssm_b_im[l], ssm_c_re[l], ssm_c_im[l], ssm_d[l])
        h = _mix_out(h, attn.reshape(b * seq, -1), ssm_pre, ssm_glu_w[l], ssm_glu_b[l],
                     attn_out_norm[l], ssm_out_norm[l], w_out[l])
        h = _ffn(h, norm_ffn2[l], ffn2_w_gate[l], ffn2_w_up[l], ffn2_w_down[l],
                 final_norm, final_norm=(l == depth - 1))
    return h.reshape(b, seq, d).astype(x.dtype)
```

```python
import functools

import jax
import jax.numpy as jnp
from jax import lax
from jax.experimental import pallas as pl
from jax.experimental.pallas import tpu as pltpu

F32 = jnp.float32
BF16 = jnp.bfloat16

D_MODEL = 1024
ATTN_HEADS = 8
ATTN_KV_HEADS = 2
Q_PER_KV = ATTN_HEADS // ATTN_KV_HEADS
HEAD_DIM = 64
ATTN_WIDTH = ATTN_HEADS * HEAD_DIM
KV_WIDTH = ATTN_KV_HEADS * HEAD_DIM
WINDOW = 128
BLOCK = 128
SSM_CH = 16
SSM_WIDTH = D_MODEL - ATTN_WIDTH
SSM_GROUPS = SSM_WIDTH // SSM_CH
SSM_STATE = 64
IN_WIDTH = ATTN_WIDTH + 2 * KV_WIDTH + SSM_WIDTH
D_FF = 2816
EPS = 1e-6
NEG_INF = -1e30
LAMBDA_RE_MAX = -1e-4

LANES = 128
SUBLANES = 8
VMEM_LIMIT_BYTES = 56 * 1024 * 1024

TOKEN_TILE = 1024
FF_TILE = 256
SSM_CHUNK = 16
CHUNK_WIDTH = SSM_CHUNK * SSM_CH
GROUPS_PER_BLOCK = LANES // SSM_CH
SSM_INTERLEAVE = 4
OPS_GROUPS_PER_STEP = 4

NT_DIMS = (((1,), (1,)), ((), ()))


def _rms(x):
    return x * lax.rsqrt(jnp.mean(x * x, axis=-1, keepdims=True) + EPS)


def _ffn_kernel(x_ref, gain_ref, wg_ref, wu_ref, wd_ref, fgain_ref, o_ref, hn_ref, *, final_norm):
    j = pl.program_id(1)
    last = pl.num_programs(1) - 1

    @pl.when(j == 0)
    def _():
        hn_ref[...] = (_rms(x_ref[...]) * gain_ref[...]).astype(BF16)

    hn = hn_ref[...]
    g = jnp.dot(hn, wg_ref[...], preferred_element_type=F32)
    u = jnp.dot(hn, wu_ref[...], preferred_element_type=F32)
    act = (g * jax.nn.sigmoid(g) * u).astype(BF16)
    d = jnp.dot(act, wd_ref[...], preferred_element_type=F32)

    @pl.when(j == 0)
    def _():
        o_ref[...] = d

    @pl.when(j > 0)
    def _():
        o_ref[...] += d

    @pl.when(j == last)
    def _():
        y = x_ref[...] + 0.5 * o_ref[...]
        if final_norm:
            y = _rms(y) * fgain_ref[...]
        o_ref[...] = y


def _ffn(x, gain, w_gate, w_up, w_down, final_gain, final_norm):
    t = x.shape[0]
    return pl.pallas_call(
        functools.partial(_ffn_kernel, final_norm=final_norm),
        grid=(t // TOKEN_TILE, D_FF // FF_TILE),
        in_specs=[
            pl.BlockSpec((TOKEN_TILE, D_MODEL), lambda i, j: (i, 0)),
            pl.BlockSpec((1, D_MODEL), lambda i, j: (0, 0)),
            pl.BlockSpec((D_MODEL, FF_TILE), lambda i, j: (0, j)),
            pl.BlockSpec((D_MODEL, FF_TILE), lambda i, j: (0, j)),
            pl.BlockSpec((FF_TILE, D_MODEL), lambda i, j: (j, 0)),
            pl.BlockSpec((1, D_MODEL), lambda i, j: (0, 0)),
        ],
        out_specs=pl.BlockSpec((TOKEN_TILE, D_MODEL), lambda i, j: (i, 0)),
        out_shape=jax.ShapeDtypeStruct((t, D_MODEL), F32),
        scratch_shapes=[pltpu.VMEM((TOKEN_TILE, D_MODEL), BF16)],
        compiler_params=pltpu.CompilerParams(
            dimension_semantics=("parallel", "arbitrary"), vmem_limit_bytes=VMEM_LIMIT_BYTES),
    )(x, gain.reshape(1, D_MODEL), w_gate.astype(BF16), w_up.astype(BF16), w_down.astype(BF16),
      final_gain.reshape(1, D_MODEL))


def _in_proj_kernel(x_ref, gain_ref, w_ref, q_ref, k_ref, v_ref, u_ref):
    hn = (_rms(x_ref[...]) * gain_ref[...]).astype(BF16)
    proj = jnp.dot(hn, w_ref[...], preferred_element_type=F32)
    q_ref[...] = proj[:, :ATTN_WIDTH].astype(BF16)
    k_ref[...] = proj[:, ATTN_WIDTH:ATTN_WIDTH + KV_WIDTH].astype(BF16)
    v_ref[...] = proj[:, ATTN_WIDTH + KV_WIDTH:ATTN_WIDTH + 2 * KV_WIDTH].astype(BF16)
    u_ref[...] = proj[:, ATTN_WIDTH + 2 * KV_WIDTH:]


def _in_proj(x, gain, w_in):
    t = x.shape[0]
    row = lambda width: pl.BlockSpec((TOKEN_TILE, width), lambda i: (i, 0))
    return pl.pallas_call(
        _in_proj_kernel,
        grid=(t // TOKEN_TILE,),
        in_specs=[
            row(D_MODEL),
            pl.BlockSpec((1, D_MODEL), lambda i: (0, 0)),
            pl.BlockSpec((D_MODEL, IN_WIDTH), lambda i: (0, 0)),
        ],
        out_specs=[row(ATTN_WIDTH), row(KV_WIDTH), row(KV_WIDTH), row(SSM_WIDTH)],
        out_shape=[
            jax.ShapeDtypeStruct((t, ATTN_WIDTH), BF16),
            jax.ShapeDtypeStruct((t, KV_WIDTH), BF16),
            jax.ShapeDtypeStruct((t, KV_WIDTH), BF16),
            jax.ShapeDtypeStruct((t, SSM_WIDTH), F32),
        ],
        compiler_params=pltpu.CompilerParams(
            dimension_semantics=("parallel",), vmem_limit_bytes=VMEM_LIMIT_BYTES),
    )(x, gain.reshape(1, D_MODEL), w_in.astype(BF16))


def _attn_kernel(sink_ref, q_ref, kp_ref, kc_ref, kn_ref, vp_ref, vc_ref, vn_ref, o_ref, *, seq):
    n = pl.program_id(1)
    qi = lax.broadcasted_iota(jnp.int32, (BLOCK, 3 * BLOCK), 0)
    kj = lax.broadcasted_iota(jnp.int32, (BLOCK, 3 * BLOCK), 1)
    rel = jnp.abs(kj - BLOCK - qi)
    key_pos = kj + (n - 1) * BLOCK
    valid = (rel <= WINDOW) & (key_pos >= 0) & (key_pos < seq)
    dist = rel.astype(F32)

    outs = []
    for kh in range(ATTN_KV_HEADS):
        cols = slice(kh * HEAD_DIM, (kh + 1) * HEAD_DIM)
        kcat = jnp.concatenate([kp_ref[:, cols], kc_ref[:, cols], kn_ref[:, cols]], axis=0)
        vcat = jnp.concatenate([vp_ref[:, cols], vc_ref[:, cols], vn_ref[:, cols]], axis=0)
        heads = [kh * Q_PER_KV + g for g in range(Q_PER_KV)]
        qs = jnp.concatenate([q_ref[:, h * HEAD_DIM:(h + 1) * HEAD_DIM] for h in heads], axis=0)
        s_all = lax.dot_general(qs, kcat, NT_DIMS, preferred_element_type=F32) * (HEAD_DIM ** -0.5)
        for g, h in enumerate(heads):
            slope = float(2.0 ** (-8.0 * (h + 1) / ATTN_HEADS))
            s = s_all[g * BLOCK:(g + 1) * BLOCK]
            s = jnp.where(valid, s - slope * dist, NEG_INF)
            sink = sink_ref[h]
            m = jnp.maximum(jnp.max(s, axis=-1, keepdims=True), sink)
            e = jnp.exp(s - m)
            den = jnp.sum(e, axis=-1, keepdims=True) + jnp.exp(sink - m)
            pv = jnp.dot(e.astype(BF16), vcat, preferred_element_type=F32)
            outs.append(pv / den)
    o_ref[...] = jnp.concatenate(outs, axis=-1).astype(o_ref.dtype)


def _attention(q, k, v, sinks):
    b, seq, _ = q.shape
    nb = seq // BLOCK
    kv_spec = lambda f: pl.BlockSpec((None, BLOCK, KV_WIDTH), f)
    prev = lambda bi, n: (bi, jnp.maximum(n - 1, 0), 0)
    cur = lambda bi, n: (bi, n, 0)
    nxt = lambda bi, n: (bi, jnp.minimum(n + 1, nb - 1), 0)
    return pl.pallas_call(
        functools.partial(_attn_kernel, seq=seq),
        grid=(b, nb),
        in_specs=[
            pl.BlockSpec(memory_space=pltpu.SMEM),
            pl.BlockSpec((None, BLOCK, ATTN_WIDTH), cur),
            kv_spec(prev), kv_spec(cur), kv_spec(nxt),
            kv_spec(prev), kv_spec(cur), kv_spec(nxt),
        ],
        out_specs=pl.BlockSpec((None, BLOCK, ATTN_WIDTH), cur),
        out_shape=jax.ShapeDtypeStruct((b, seq, ATTN_WIDTH), BF16),
        compiler_params=pltpu.CompilerParams(
            dimension_semantics=("parallel", "parallel"), vmem_limit_bytes=VMEM_LIMIT_BYTES),
    )(sinks.astype(F32), q, k, k, k, v, v, v)


def _ssm_ops_kernel(prm_ref, bc_ref, ein_ref, toep_ref, eout_ref, aq_ref):
    q, hc, p = SSM_CHUNK, SSM_CH, SSM_STATE
    hi = lax.Precision.HIGHEST
    fwd = lax.broadcasted_iota(jnp.int32, (1, 2 * p), 1) < p
    zero_row = jnp.zeros((1, 2 * p), F32)
    row_id = lax.broadcasted_iota(jnp.int32, (CHUNK_WIDTH, CHUNK_WIDTH), 0)
    col_id = lax.broadcasted_iota(jnp.int32, (CHUNK_WIDTH, CHUNK_WIDTH), 1)

    def table(select, n):
        picks = [select(m) for m in range(n)]
        re = jnp.concatenate([jnp.broadcast_to(r, (hc, 2 * p)) for r, _ in picks], axis=0)
        im = jnp.concatenate([jnp.broadcast_to(i, (hc, 2 * p)) for _, i in picks], axis=0)
        return re, im

    def tile_rows(x, n):
        return jnp.concatenate([x] * n, axis=0)

    for gi in range(prm_ref.shape[0]):
        lr = jnp.minimum(prm_ref[gi, 0:1, :], LAMBDA_RE_MAX)
        li = prm_ref[gi, 1:2, :]
        dt = jnp.exp(prm_ref[gi, 2:3, :])
        mag = jnp.exp(lr * dt)
        a_r = mag * jnp.cos(li * dt)
        a_i = mag * jnp.sin(li * dt)
        den = lr * lr + li * li
        coef_r = ((a_r - 1.0) * lr + a_i * li) / den
        coef_i = (a_i * lr - (a_r - 1.0) * li) / den
        b_r, b_i = bc_ref[gi, 0], bc_ref[gi, 1]
        c_r, c_i = bc_ref[gi, 2], bc_ref[gi, 3]
        bb_r = coef_r * b_r - coef_i * b_i
        bb_i = coef_r * b_i + coef_i * b_r

        pw = [(jnp.ones((1, 2 * p), F32), zero_row)]
        for _ in range(q):
            r, i = pw[-1]
            pw.append((r * a_r - i * a_i, r * a_i + i * a_r))

        def both(f_idx, b_idx):
            fr, fi = pw[f_idx] if f_idx is not None else (zero_row, zero_row)
            br, bi = pw[b_idx] if b_idx is not None else (zero_row, zero_row)
            return jnp.where(fwd, fr, br), jnp.where(fwd, fi, bi)

        p_r, p_i = table(lambda i: both(q - 1 - i, i), q)
        tb_r, tb_i = tile_rows(bb_r, q), tile_rows(bb_i, q)
        ein = jnp.concatenate([tb_r * p_r - tb_i * p_i, tb_r * p_i + tb_i * p_r], axis=1)
        ein_ref[gi] = ein.astype(BF16)

        p_r, p_i = table(lambda j: both(j + 1, q - j), q)
        tc_r, tc_i = tile_rows(c_r, q), tile_rows(c_i, q)
        eout = jnp.concatenate([tc_r * p_r - tc_i * p_i, -(tc_r * p_i + tc_i * p_r)], axis=1)
        eout_ref[gi] = eout.astype(BF16)

        def lag(m):
            return both(m - (q - 1) if q - 1 <= m <= 2 * q - 2 else None,
                        (q - 1) - m if m <= q - 1 else None)

        p_r, p_i = table(lag, 2 * q)
        tc_r, tc_i = tile_rows(c_r, 2 * q), tile_rows(c_i, 2 * q)
        cpt = jnp.concatenate([tc_r * p_r - tc_i * p_i, tc_r * p_i + tc_i * p_r], axis=1)
        bcat = jnp.concatenate([bb_r, -bb_i], axis=1)
        kern = lax.dot_general(bcat, cpt, NT_DIMS, precision=hi, preferred_element_type=F32)
        toep = jnp.concatenate(
            [kern[:, hc * (q - 1 - i):hc * (q - 1 - i) + CHUNK_WIDTH] for i in range(q)], axis=0)
        skip = jnp.concatenate([prm_ref[gi, 3:4, :]] * (CHUNK_WIDTH // (2 * p)), axis=1)
        toep_ref[gi] = (toep + jnp.where(row_id == col_id, skip, 0.0)).astype(BF16)

        aq_ref[gi] = jnp.concatenate(
            [pw[q][0], pw[q][1], jnp.zeros((SUBLANES - 2, 2 * p), F32)], axis=0)


def _ssm_operators(lam_re, lam_im, log_dt, b_re, b_im, c_re, c_im, d_skip):
    g, p, hc = SSM_GROUPS, SSM_STATE, SSM_CH
    lanes = lambda a: a.astype(F32).transpose(1, 0, 2).reshape(g, 2 * p)
    prm = jnp.stack(
        [lanes(lam_re), lanes(lam_im),
         jnp.repeat(log_dt.astype(F32).T, p, axis=1),
         jnp.tile(d_skip.astype(F32), (1, 2 * p // hc))]
        + [jnp.zeros((g, 2 * p), F32)] * (SUBLANES - 4), axis=1)
    bc = jnp.stack(
        [b_re.astype(F32).transpose(1, 3, 0, 2).reshape(g, hc, 2 * p),
         b_im.astype(F32).transpose(1, 3, 0, 2).reshape(g, hc, 2 * p),
         c_re.astype(F32).transpose(1, 2, 0, 3).reshape(g, hc, 2 * p),
         c_im.astype(F32).transpose(1, 2, 0, 3).reshape(g, hc, 2 * p)], axis=1)
    gs = OPS_GROUPS_PER_STEP
    mat = pl.BlockSpec((gs, CHUNK_WIDTH, CHUNK_WIDTH), lambda s: (s, 0, 0))
    mat_shape = jax.ShapeDtypeStruct((g, CHUNK_WIDTH, CHUNK_WIDTH), BF16)
    return pl.pallas_call(
        _ssm_ops_kernel,
        grid=(g // gs,),
        in_specs=[pl.BlockSpec((gs, SUBLANES, 2 * p), lambda s: (s, 0, 0)),
                  pl.BlockSpec((gs, 4, hc, 2 * p), lambda s: (s, 0, 0, 0))],
        out_specs=[mat, mat, mat, pl.BlockSpec((gs, SUBLANES, 2 * p), lambda s: (s, 0, 0))],
        out_shape=[mat_shape, mat_shape, mat_shape,
                   jax.ShapeDtypeStruct((g, SUBLANES, 2 * p), F32)],
        compiler_params=pltpu.CompilerParams(
            dimension_semantics=("parallel",), vmem_limit_bytes=VMEM_LIMIT_BYTES),
    )(prm, bc)


def _block_transpose(v, lane_block):
    v = list(v)
    n = len(v)
    d = n // 2
    while d >= 1:
        upper = (lane_block & d) != 0
        for i in range(n):
            if i & d == 0:
                a, b = v[i], v[i + d]
                v[i] = jnp.where(upper, pltpu.roll(b, SSM_CH * d, 1), a)
                v[i + d] = jnp.where(upper, b, pltpu.roll(a, LANES - SSM_CH * d, 1))
        d //= 2
    return v


def _ssm_kernel(u_ref, ein_ref, toep_ref, eout_ref, aq_ref, y_ref, ug_ref, s_ref, x_ref,
                *, batch, seq):
    q, half = SSM_CHUNK, SSM_STATE
    n_chunks = seq // q
    gpb = GROUPS_PER_BLOCK
    lane_block = lax.broadcasted_iota(jnp.int32, (batch, LANES), 1) // SSM_CH

    def gather(c, carry):
        rows = pl.ds(pl.multiple_of(c * batch, batch), batch)
        for part in range(q // gpb):
            t0 = c * q + part * gpb
            v = [u_ref[pl.ds(t0 + i, batch, stride=seq), :] for i in range(gpb)]
            w = _block_transpose(v, lane_block)
            for g in range(gpb):
                ug_ref[g, rows, part * LANES:(part + 1) * LANES] = w[g]
        return carry

    lax.fori_loop(0, n_chunks, gather, 0)

    fwd_lane = lax.broadcasted_iota(jnp.int32, (batch, 2 * half), 1) < half
    zeros = jnp.zeros((batch, half), F32)
    last_rows = pl.ds((n_chunks - 1) * batch, batch)
    ni = SSM_INTERLEAVE
    for g0 in range(0, gpb, ni):
        for gi in range(ni):
            s_ref[gi] = jnp.dot(ug_ref[g0 + gi].astype(BF16), ein_ref[g0 + gi],
                                preferred_element_type=F32)
            x_ref[gi, 0:batch, 0:half] = zeros
            x_ref[gi, 0:batch, 2 * half:3 * half] = zeros
            x_ref[gi, last_rows, half:2 * half] = zeros
            x_ref[gi, last_rows, 3 * half:4 * half] = zeros

        def step(k, carry):
            rf = pl.multiple_of(k * batch, batch)
            rb = pl.multiple_of((n_chunks - 1 - k) * batch, batch)
            new = []
            for gi in range(ni):
                xr, xi = carry[gi]
                sre = jnp.where(fwd_lane, s_ref[gi, pl.ds(rf, batch), 0:2 * half],
                                s_ref[gi, pl.ds(rb, batch), 0:2 * half])
                sim = jnp.where(fwd_lane, s_ref[gi, pl.ds(rf, batch), 2 * half:4 * half],
                                s_ref[gi, pl.ds(rb, batch), 2 * half:4 * half])
                ar = aq_ref[g0 + gi, 0:1, :]
                ai = aq_ref[g0 + gi, 1:2, :]
                nr = ar * xr - ai * xi + sre
                nim = ar * xi + ai * xr + sim
                x_ref[gi, pl.ds(rf + batch, batch), 0:half] = nr[:, 0:half]
                x_ref[gi, pl.ds(rf + batch, batch), 2 * half:3 * half] = nim[:, 0:half]
                x_ref[gi, pl.ds(rb - batch, batch), half:2 * half] = nr[:, half:2 * half]
                x_ref[gi, pl.ds(rb - batch, batch), 3 * half:4 * half] = nim[:, half:2 * half]
                new.append((nr, nim))
            return tuple(new)

        init = tuple((jnp.zeros((batch, 2 * half), F32), jnp.zeros((batch, 2 * half), F32))
                     for _ in range(ni))
        lax.fori_loop(0, n_chunks - 1, step, init)

        for gi in range(ni):
            g = g0 + gi
            y = jnp.dot(ug_ref[g].astype(BF16), toep_ref[g], preferred_element_type=F32)
            y += lax.dot_general(x_ref[gi].astype(BF16), eout_ref[g], NT_DIMS,
                                 preferred_element_type=F32)
            ug_ref[g] = y

    def scatter(c, carry):
        rows = pl.ds(pl.multiple_of(c * batch, batch), batch)
        for part in range(q // gpb):
            t0 = c * q + part * gpb
            v = [ug_ref[g, rows, part * LANES:(part + 1) * LANES] for g in range(gpb)]
            w = _block_transpose(v, lane_block)
            for j in range(gpb):
                y_ref[pl.ds(t0 + j, batch, stride=seq), :] = w[j]
        return carry

    lax.fori_loop(0, n_chunks, scatter, 0)


def _ssm(u, batch, seq, lam_re, lam_im, log_dt, b_re, b_im, c_re, c_im, d_skip):
    t = batch * seq
    rows = t // SSM_CHUNK
    ein, toep, eout, a_q = _ssm_operators(lam_re, lam_im, log_dt, b_re, b_im, c_re, c_im, d_skip)
    gpb = GROUPS_PER_BLOCK
    mat = pl.BlockSpec((gpb, CHUNK_WIDTH, CHUNK_WIDTH), lambda s: (s, 0, 0))
    col = pl.BlockSpec((t, LANES), lambda s: (0, s), pipeline_mode=pl.Buffered(1))
    return pl.pallas_call(
        functools.partial(_ssm_kernel, batch=batch, seq=seq),
        grid=(SSM_GROUPS // gpb,),
        in_specs=[col, mat, mat, mat,
                  pl.BlockSpec((gpb, SUBLANES, 2 * SSM_STATE), lambda s: (s, 0, 0))],
        out_specs=col,
        out_shape=jax.ShapeDtypeStruct((t, SSM_WIDTH), F32),
        scratch_shapes=[pltpu.VMEM((gpb, rows, CHUNK_WIDTH), F32),
                        pltpu.VMEM((SSM_INTERLEAVE, rows, 4 * SSM_STATE), F32),
                        pltpu.VMEM((SSM_INTERLEAVE, rows, 4 * SSM_STATE), F32)],
        compiler_params=pltpu.CompilerParams(
            dimension_semantics=("parallel",), vmem_limit_bytes=VMEM_LIMIT_BYTES),
    )(u, ein, toep, eout, a_q)


def _mix_kernel(x_ref, attn_ref, ssm_ref, gw_ref, gb_ref, ga_ref, gs_ref, wo_ref, o_ref):
    y = jax.nn.gelu(ssm_ref[...])
    z = jnp.dot(y.astype(BF16), gw_ref[...], preferred_element_type=F32) + gb_ref[...]
    s = y * jax.nn.sigmoid(z)
    sn = _rms(s) * gs_ref[...]
    an = _rms(attn_ref[...].astype(F32)) * ga_ref[...]
    mixed = jnp.concatenate([an, sn], axis=-1).astype(BF16)
    o_ref[...] = x_ref[...] + jnp.dot(mixed, wo_ref[...], preferred_element_type=F32)


def _mix_out(x, attn, ssm_pre, glu_w, glu_b, attn_gain, ssm_gain, w_out):
    t = x.shape[0]
    row = lambda width: pl.BlockSpec((TOKEN_TILE, width), lambda i: (i, 0))
    full = lambda r, c: pl.BlockSpec((r, c), lambda i: (0, 0))
    return pl.pallas_call(
        _mix_kernel,
        grid=(t // TOKEN_TILE,),
        in_specs=[row(D_MODEL), row(ATTN_WIDTH), row(SSM_WIDTH),
                  full(SSM_WIDTH, SSM_WIDTH), full(1, SSM_WIDTH),
                  full(1, ATTN_WIDTH), full(1, SSM_WIDTH), full(D_MODEL, D_MODEL)],
        out_specs=row(D_MODEL),
        out_shape=jax.ShapeDtypeStruct((t, D_MODEL), F32),
        compiler_params=pltpu.CompilerParams(
            dimension_semantics=("parallel",), vmem_limit_bytes=VMEM_LIMIT_BYTES),
    )(x, attn, ssm_pre, glu_w.astype(BF16), glu_b.reshape(1, -1).astype(F32),
      attn_gain.reshape(1, -1).astype(F32), ssm_gain.reshape(1, -1).astype(F32),
      w_out.astype(BF16))


def kernel(x, norm_ffn1, ffn1_w_gate, ffn1_w_up, ffn1_w_down, norm_mix, w_in, attn_sinks,
           ssm_lambda_re, ssm_lambda_im, ssm_log_dt, ssm_b_re, ssm_b_im, ssm_c_re, ssm_c_im,
           ssm_d, ssm_glu_w, ssm_glu_b, attn_out_norm, ssm_out_norm, w_out,
           norm_ffn2, ffn2_w_gate, ffn2_w_up, ffn2_w_down, final_norm):
    b, seq, d = x.shape
    depth = norm_ffn1.shape[0]
    assert d == D_MODEL and seq % BLOCK == 0 and (b * seq) % TOKEN_TILE == 0
    assert b == SUBLANES and seq % SSM_CHUNK == 0
    h = x.reshape(b * seq, d).astype(F32)
    for l in range(depth):
        h = _ffn(h, norm_ffn1[l], ffn1_w_gate[l], ffn1_w_up[l], ffn1_w_down[l],
                 final_norm, final_norm=False)
        q, k, v, u = _in_proj(h, norm_mix[l], w_in[l])
        attn = _attention(q.reshape(b, seq, -1), k.reshape(b, seq, -1), v.reshape(b, seq, -1),
                          attn_sinks[l])
        ssm_pre = _ssm(u, b, seq, ssm_lambda_re[l], ssm_lambda_im[l], ssm_log_dt[l],
                       ssm_b_re[l], ssm_b_im[l], ssm_c_re[l], ssm_c_im[l], ssm_d[l])
        h = _mix_out(h, attn.reshape(b * seq, -1), ssm_pre, ssm_glu_w[l], ssm_glu_b[l],
                     attn_out_norm[l], ssm_out_norm[l], w_out[l])
        h = _ffn(h, norm_ffn2[l], ffn2_w_gate[l], ffn2_w_up[l], ffn2_w_down[l],
                 final_norm, final_norm=(l == depth - 1))
    return h.reshape(b, seq, d).astype(x.dtype)
```

```python
import functools

import jax
import jax.numpy as jnp
from jax import lax
from jax.experimental import pallas as pl
from jax.experimental.pallas import tpu as pltpu

F32 = jnp.float32
BF16 = jnp.bfloat16

D_MODEL = 1024
ATTN_HEADS = 8
ATTN_KV_HEADS = 2
Q_PER_KV = ATTN_HEADS // ATTN_KV_HEADS
HEAD_DIM = 64
ATTN_WIDTH = ATTN_HEADS * HEAD_DIM
KV_WIDTH = ATTN_KV_HEADS * HEAD_DIM
WINDOW = 128
BLOCK = 128
SSM_CH = 16
SSM_WIDTH = D_MODEL - ATTN_WIDTH
SSM_GROUPS = SSM_WIDTH // SSM_CH
SSM_STATE = 64
IN_WIDTH = ATTN_WIDTH + 2 * KV_WIDTH + SSM_WIDTH
D_FF = 2816
EPS = 1e-6
NEG_INF = -1e30
LAMBDA_RE_MAX = -1e-4

LANES = 128
SUBLANES = 8
VMEM_LIMIT_BYTES = 56 * 1024 * 1024

TOKEN_TILE = 1024
FF_TILE = 256
SSM_CHUNK = 16
CHUNK_WIDTH = SSM_CHUNK * SSM_CH
GROUPS_PER_BLOCK = LANES // SSM_CH
SSM_INTERLEAVE = 4
SSM_RELAYOUT_UNROLL = 4
OPS_GROUPS_PER_STEP = 4

NT_DIMS = (((1,), (1,)), ((), ()))


def _rms(x):
    return x * lax.rsqrt(jnp.mean(x * x, axis=-1, keepdims=True) + EPS)


def _ffn_kernel(x_ref, gain_ref, wg_ref, wu_ref, wd_ref, fgain_ref, o_ref, hn_ref, *, final_norm):
    j = pl.program_id(1)
    last = pl.num_programs(1) - 1

    @pl.when(j == 0)
    def _():
        hn_ref[...] = (_rms(x_ref[...]) * gain_ref[...]).astype(BF16)

    hn = hn_ref[...]
    g = jnp.dot(hn, wg_ref[...], preferred_element_type=F32)
    u = jnp.dot(hn, wu_ref[...], preferred_element_type=F32)
    act = (g * jax.nn.sigmoid(g) * u).astype(BF16)
    d = jnp.dot(act, wd_ref[...], preferred_element_type=F32)

    @pl.when(j == 0)
    def _():
        o_ref[...] = d

    @pl.when(j > 0)
    def _():
        o_ref[...] += d

    @pl.when(j == last)
    def _():
        y = x_ref[...] + 0.5 * o_ref[...]
        if final_norm:
            y = _rms(y) * fgain_ref[...]
        o_ref[...] = y


def _ffn(x, gain, w_gate, w_up, w_down, final_gain, final_norm):
    t = x.shape[0]
    return pl.pallas_call(
        functools.partial(_ffn_kernel, final_norm=final_norm),
        grid=(t // TOKEN_TILE, D_FF // FF_TILE),
        in_specs=[
            pl.BlockSpec((TOKEN_TILE, D_MODEL), lambda i, j: (i, 0)),
            pl.BlockSpec((1, D_MODEL), lambda i, j: (0, 0)),
            pl.BlockSpec((D_MODEL, FF_TILE), lambda i, j: (0, j)),
            pl.BlockSpec((D_MODEL, FF_TILE), lambda i, j: (0, j)),
            pl.BlockSpec((FF_TILE, D_MODEL), lambda i, j: (j, 0)),
            pl.BlockSpec((1, D_MODEL), lambda i, j: (0, 0)),
        ],
        out_specs=pl.BlockSpec((TOKEN_TILE, D_MODEL), lambda i, j: (i, 0)),
        out_shape=jax.ShapeDtypeStruct((t, D_MODEL), F32),
        scratch_shapes=[pltpu.VMEM((TOKEN_TILE, D_MODEL), BF16)],
        compiler_params=pltpu.CompilerParams(
            dimension_semantics=("parallel", "arbitrary"), vmem_limit_bytes=VMEM_LIMIT_BYTES),
    )(x, gain.reshape(1, D_MODEL), w_gate.astype(BF16), w_up.astype(BF16), w_down.astype(BF16),
      final_gain.reshape(1, D_MODEL))


def _in_proj_kernel(x_ref, gain_ref, w_ref, q_ref, k_ref, v_ref, u_ref):
    hn = (_rms(x_ref[...]) * gain_ref[...]).astype(BF16)
    proj = jnp.dot(hn, w_ref[...], preferred_element_type=F32)
    q_ref[...] = proj[:, :ATTN_WIDTH].astype(BF16)
    k_ref[...] = proj[:, ATTN_WIDTH:ATTN_WIDTH + KV_WIDTH].astype(BF16)
    v_ref[...] = proj[:, ATTN_WIDTH + KV_WIDTH:ATTN_WIDTH + 2 * KV_WIDTH].astype(BF16)
    u_ref[...] = proj[:, ATTN_WIDTH + 2 * KV_WIDTH:]


def _padded_seq(seq):
    return seq + SUBLANES if seq % (2 * SUBLANES) == 0 else seq


def _batch_rows(seq, width):
    tiles = seq // TOKEN_TILE
    return pl.BlockSpec((None, TOKEN_TILE, width), lambda i: (i // tiles, i % tiles, 0))


def _in_proj(x, gain, w_in, batch, seq):
    t = x.shape[0]
    row = lambda width: pl.BlockSpec((TOKEN_TILE, width), lambda i: (i, 0))
    return pl.pallas_call(
        _in_proj_kernel,
        grid=(t // TOKEN_TILE,),
        in_specs=[
            row(D_MODEL),
            pl.BlockSpec((1, D_MODEL), lambda i: (0, 0)),
            pl.BlockSpec((D_MODEL, IN_WIDTH), lambda i: (0, 0)),
        ],
        out_specs=[row(ATTN_WIDTH), row(KV_WIDTH), row(KV_WIDTH), _batch_rows(seq, SSM_WIDTH)],
        out_shape=[
            jax.ShapeDtypeStruct((t, ATTN_WIDTH), BF16),
            jax.ShapeDtypeStruct((t, KV_WIDTH), BF16),
            jax.ShapeDtypeStruct((t, KV_WIDTH), BF16),
            jax.ShapeDtypeStruct((batch, _padded_seq(seq), SSM_WIDTH), F32),
        ],
        compiler_params=pltpu.CompilerParams(
            dimension_semantics=("parallel",), vmem_limit_bytes=VMEM_LIMIT_BYTES),
    )(x, gain.reshape(1, D_MODEL), w_in.astype(BF16))


def _attn_kernel(sink_ref, q_ref, kp_ref, kc_ref, kn_ref, vp_ref, vc_ref, vn_ref, o_ref, *, seq):
    n = pl.program_id(1)
    qi = lax.broadcasted_iota(jnp.int32, (BLOCK, 3 * BLOCK), 0)
    kj = lax.broadcasted_iota(jnp.int32, (BLOCK, 3 * BLOCK), 1)
    rel = jnp.abs(kj - BLOCK - qi)
    key_pos = kj + (n - 1) * BLOCK
    valid = (rel <= WINDOW) & (key_pos >= 0) & (key_pos < seq)
    dist = rel.astype(F32)

    outs = []
    for kh in range(ATTN_KV_HEADS):
        cols = slice(kh * HEAD_DIM, (kh + 1) * HEAD_DIM)
        kcat = jnp.concatenate([kp_ref[:, cols], kc_ref[:, cols], kn_ref[:, cols]], axis=0)
        vcat = jnp.concatenate([vp_ref[:, cols], vc_ref[:, cols], vn_ref[:, cols]], axis=0)
        heads = [kh * Q_PER_KV + g for g in range(Q_PER_KV)]
        qs = jnp.concatenate([q_ref[:, h * HEAD_DIM:(h + 1) * HEAD_DIM] for h in heads], axis=0)
        s_all = lax.dot_general(qs, kcat, NT_DIMS, preferred_element_type=F32) * (HEAD_DIM ** -0.5)
        for g, h in enumerate(heads):
            slope = float(2.0 ** (-8.0 * (h + 1) / ATTN_HEADS))
            s = s_all[g * BLOCK:(g + 1) * BLOCK]
            s = jnp.where(valid, s - slope * dist, NEG_INF)
            sink = sink_ref[h]
            m = jnp.maximum(jnp.max(s, axis=-1, keepdims=True), sink)
            e = jnp.exp(s - m)
            den = jnp.sum(e, axis=-1, keepdims=True) + jnp.exp(sink - m)
            pv = jnp.dot(e.astype(BF16), vcat, preferred_element_type=F32)
            outs.append(pv / den)
    o_ref[...] = jnp.concatenate(outs, axis=-1).astype(o_ref.dtype)


def _attention(q, k, v, sinks):
    b, seq, _ = q.shape
    nb = seq // BLOCK
    kv_spec = lambda f: pl.BlockSpec((None, BLOCK, KV_WIDTH), f)
    prev = lambda bi, n: (bi, jnp.maximum(n - 1, 0), 0)
    cur = lambda bi, n: (bi, n, 0)
    nxt = lambda bi, n: (bi, jnp.minimum(n + 1, nb - 1), 0)
    return pl.pallas_call(
        functools.partial(_attn_kernel, seq=seq),
        grid=(b, nb),
        in_specs=[
            pl.BlockSpec(memory_space=pltpu.SMEM),
            pl.BlockSpec((None, BLOCK, ATTN_WIDTH), cur),
            kv_spec(prev), kv_spec(cur), kv_spec(nxt),
            kv_spec(prev), kv_spec(cur), kv_spec(nxt),
        ],
        out_specs=pl.BlockSpec((None, BLOCK, ATTN_WIDTH), cur),
        out_shape=jax.ShapeDtypeStruct((b, seq, ATTN_WIDTH), BF16),
        compiler_params=pltpu.CompilerParams(
            dimension_semantics=("parallel", "parallel"), vmem_limit_bytes=VMEM_LIMIT_BYTES),
    )(sinks.astype(F32), q, k, k, k, v, v, v)


def _ssm_ops_kernel(prm_ref, bc_ref, ein_ref, toep_ref, eout_ref, aq_ref):
    q, hc, p = SSM_CHUNK, SSM_CH, SSM_STATE
    hi = lax.Precision.HIGHEST
    fwd = lax.broadcasted_iota(jnp.int32, (1, 2 * p), 1) < p
    zero_row = jnp.zeros((1, 2 * p), F32)
    row_id = lax.broadcasted_iota(jnp.int32, (CHUNK_WIDTH, CHUNK_WIDTH), 0)
    col_id = lax.broadcasted_iota(jnp.int32, (CHUNK_WIDTH, CHUNK_WIDTH), 1)

    def table(select, n):
        picks = [select(m) for m in range(n)]
        re = jnp.concatenate([jnp.broadcast_to(r, (hc, 2 * p)) for r, _ in picks], axis=0)
        im = jnp.concatenate([jnp.broadcast_to(i, (hc, 2 * p)) for _, i in picks], axis=0)
        return re, im

    def tile_rows(x, n):
        return jnp.concatenate([x] * n, axis=0)

    for gi in range(prm_ref.shape[0]):
        lr = jnp.minimum(prm_ref[gi, 0:1, :], LAMBDA_RE_MAX)
        li = prm_ref[gi, 1:2, :]
        dt = jnp.exp(prm_ref[gi, 2:3, :])
        mag = jnp.exp(lr * dt)
        a_r = mag * jnp.cos(li * dt)
        a_i = mag * jnp.sin(li * dt)
        den = lr * lr + li * li
        coef_r = ((a_r - 1.0) * lr + a_i * li) / den
        coef_i = (a_i * lr - (a_r - 1.0) * li) / den
        b_r, b_i = bc_ref[gi, 0], bc_ref[gi, 1]
        c_r, c_i = bc_ref[gi, 2], bc_ref[gi, 3]
        bb_r = coef_r * b_r - coef_i * b_i
        bb_i = coef_r * b_i + coef_i * b_r

        pw = [(jnp.ones((1, 2 * p), F32), zero_row)]
        for _ in range(q):
            r, i = pw[-1]
            pw.append((r * a_r - i * a_i, r * a_i + i * a_r))

        def both(f_idx, b_idx):
            fr, fi = pw[f_idx] if f_idx is not None else (zero_row, zero_row)
            br, bi = pw[b_idx] if b_idx is not None else (zero_row, zero_row)
            return jnp.where(fwd, fr, br), jnp.where(fwd, fi, bi)

        p_r, p_i = table(lambda i: both(q - 1 - i, i), q)
        tb_r, tb_i = tile_rows(bb_r, q), tile_rows(bb_i, q)
        ein = jnp.concatenate([tb_r * p_r - tb_i * p_i, tb_r * p_i + tb_i * p_r], axis=1)
        ein_ref[gi] = ein.astype(BF16)

        p_r, p_i = table(lambda j: both(j + 1, q - j), q)
        tc_r, tc_i = tile_rows(c_r, q), tile_rows(c_i, q)
        eout = jnp.concatenate([tc_r * p_r - tc_i * p_i, -(tc_r * p_i + tc_i * p_r)], axis=1)
        eout_ref[gi] = eout.astype(BF16)

        def lag(m):
            return both(m - (q - 1) if q - 1 <= m <= 2 * q - 2 else None,
                        (q - 1) - m if m <= q - 1 else None)

        p_r, p_i = table(lag, 2 * q)
        tc_r, tc_i = tile_rows(c_r, 2 * q), tile_rows(c_i, 2 * q)
        cpt = jnp.concatenate([tc_r * p_r - tc_i * p_i, tc_r * p_i + tc_i * p_r], axis=1)
        bcat = jnp.concatenate([bb_r, -bb_i], axis=1)
        kern = lax.dot_general(bcat, cpt, NT_DIMS, precision=hi, preferred_element_type=F32)
        toep = jnp.concatenate(
            [kern[:, hc * (q - 1 - i):hc * (q - 1 - i) + CHUNK_WIDTH] for i in range(q)], axis=0)
        skip = jnp.concatenate([prm_ref[gi, 3:4, :]] * (CHUNK_WIDTH // (2 * p)), axis=1)
        toep_ref[gi] = (toep + jnp.where(row_id == col_id, skip, 0.0)).astype(BF16)

        aq_ref[gi] = jnp.concatenate(
            [pw[q][0], pw[q][1], jnp.zeros((SUBLANES - 2, 2 * p), F32)], axis=0)


def _ssm_operators(lam_re, lam_im, log_dt, b_re, b_im, c_re, c_im, d_skip):
    g, p, hc = SSM_GROUPS, SSM_STATE, SSM_CH
    lanes = lambda a: a.astype(F32).transpose(1, 0, 2).reshape(g, 2 * p)
    prm = jnp.stack(
        [lanes(lam_re), lanes(lam_im),
         jnp.repeat(log_dt.astype(F32).T, p, axis=1),
         jnp.tile(d_skip.astype(F32), (1, 2 * p // hc))]
        + [jnp.zeros((g, 2 * p), F32)] * (SUBLANES - 4), axis=1)
    bc = jnp.stack(
        [b_re.astype(F32).transpose(1, 3, 0, 2).reshape(g, hc, 2 * p),
         b_im.astype(F32).transpose(1, 3, 0, 2).reshape(g, hc, 2 * p),
         c_re.astype(F32).transpose(1, 2, 0, 3).reshape(g, hc, 2 * p),
         c_im.astype(F32).transpose(1, 2, 0, 3).reshape(g, hc, 2 * p)], axis=1)
    gs = OPS_GROUPS_PER_STEP
    mat = pl.BlockSpec((gs, CHUNK_WIDTH, CHUNK_WIDTH), lambda s: (s, 0, 0))
    mat_shape = jax.ShapeDtypeStruct((g, CHUNK_WIDTH, CHUNK_WIDTH), BF16)
    return pl.pallas_call(
        _ssm_ops_kernel,
        grid=(g // gs,),
        in_specs=[pl.BlockSpec((gs, SUBLANES, 2 * p), lambda s: (s, 0, 0)),
                  pl.BlockSpec((gs, 4, hc, 2 * p), lambda s: (s, 0, 0, 0))],
        out_specs=[mat, mat, mat, pl.BlockSpec((gs, SUBLANES, 2 * p), lambda s: (s, 0, 0))],
        out_shape=[mat_shape, mat_shape, mat_shape,
                   jax.ShapeDtypeStruct((g, SUBLANES, 2 * p), F32)],
        compiler_params=pltpu.CompilerParams(
            dimension_semantics=("parallel",), vmem_limit_bytes=VMEM_LIMIT_BYTES),
    )(prm, bc)


def _lane_roll(x, shift):
    shift %= LANES
    return jnp.concatenate([x[:, LANES - shift:], x[:, :LANES - shift]], axis=1)


def _block_transpose(v, lane_block):
    v = list(v)
    n = len(v)
    d = n // 2
    while d >= 1:
        upper = (lane_block & d) != 0
        for i in range(n):
            if i & d == 0:
                a, b = v[i], v[i + d]
                v[i] = jnp.where(upper, _lane_roll(b, SSM_CH * d), a)
                v[i + d] = jnp.where(upper, b, _lane_roll(a, -SSM_CH * d))
        d //= 2
    return v


def _ssm_kernel(u_ref, ein_ref, toep_ref, eout_ref, aq_ref, y_ref, ug_ref, s_ref, x_ref,
                *, batch, seq, pitch):
    q, half = SSM_CHUNK, SSM_STATE
    n_chunks = seq // q
    gpb = GROUPS_PER_BLOCK
    slab = 2 * batch
    lane_block = lax.broadcasted_iota(jnp.int32, (slab, LANES), 1) // SSM_CH

    for b in range(batch if pitch > seq else 0):
        y_ref[b * pitch + seq:(b + 1) * pitch, :] = jnp.zeros((pitch - seq, LANES), F32)

    def gather(cp, carry):
        rows = pl.ds(pl.multiple_of(cp * slab, slab), slab)
        for part in range(q // gpb):
            t0 = cp * 2 * q + part * gpb
            v = [jnp.concatenate([u_ref[pl.ds(t0 + i, batch, stride=pitch), :],
                                  u_ref[pl.ds(t0 + q + i, batch, stride=pitch), :]],
                                 axis=0).astype(BF16) for i in range(gpb)]
            w = _block_transpose(v, lane_block)
            for g in range(gpb):
                ug_ref[g, rows, part * LANES:(part + 1) * LANES] = w[g]
        return carry

    lax.fori_loop(0, n_chunks // 2, gather, 0, unroll=SSM_RELAYOUT_UNROLL)

    fwd_lane = lax.broadcasted_iota(jnp.int32, (batch, 2 * half), 1) < half
    zeros = jnp.zeros((batch, half), F32)
    last_rows = pl.ds((n_chunks - 1) * batch, batch)
    ni = SSM_INTERLEAVE
    for g0 in range(0, gpb, ni):
        for gi in range(ni):
            s_ref[gi] = jnp.dot(ug_ref[g0 + gi], ein_ref[g0 + gi], preferred_element_type=F32)
            x_ref[gi, 0:batch, 0:half] = zeros
            x_ref[gi, 0:batch, 2 * half:3 * half] = zeros
            x_ref[gi, last_rows, half:2 * half] = zeros
            x_ref[gi, last_rows, 3 * half:4 * half] = zeros

        def step(k, carry):
            rf = pl.multiple_of(k * batch, batch)
            rb = pl.multiple_of((n_chunks - 1 - k) * batch, batch)
            new = []
            for gi in range(ni):
                xr, xi = carry[gi]
                sre = jnp.where(fwd_lane, s_ref[gi, pl.ds(rf, batch), 0:2 * half],
                                s_ref[gi, pl.ds(rb, batch), 0:2 * half])
                sim = jnp.where(fwd_lane, s_ref[gi, pl.ds(rf, batch), 2 * half:4 * half],
                                s_ref[gi, pl.ds(rb, batch), 2 * half:4 * half])
                ar = aq_ref[g0 + gi, 0:1, :]
                ai = aq_ref[g0 + gi, 1:2, :]
                nr = ar * xr - ai * xi + sre
                nim = ar * xi + ai * xr + sim
                x_ref[gi, pl.ds(rf + batch, batch), 0:half] = nr[:, 0:half]
                x_ref[gi, pl.ds(rf + batch, batch), 2 * half:3 * half] = nim[:, 0:half]
                x_ref[gi, pl.ds(rb - batch, batch), half:2 * half] = nr[:, half:2 * half]
                x_ref[gi, pl.ds(rb - batch, batch), 3 * half:4 * half] = nim[:, half:2 * half]
                new.append((nr, nim))
            return tuple(new)

        init = tuple((jnp.zeros((batch, 2 * half), F32), jnp.zeros((batch, 2 * half), F32))
                     for _ in range(ni))
        lax.fori_loop(0, n_chunks - 1, step, init)

        for gi in range(ni):
            g = g0 + gi
            y = jnp.dot(ug_ref[g], toep_ref[g], preferred_element_type=F32)
            y += lax.dot_general(x_ref[gi].astype(BF16), eout_ref[g], NT_DIMS,
                                 preferred_element_type=F32)
            ug_ref[g] = y.astype(BF16)

    def scatter(cp, carry):
        rows = pl.ds(pl.multiple_of(cp * slab, slab), slab)
        for part in range(q // gpb):
            t0 = cp * 2 * q + part * gpb
            v = [ug_ref[g, rows, part * LANES:(part + 1) * LANES] for g in range(gpb)]
            w = _block_transpose(v, lane_block)
            for j in range(gpb):
                wj = w[j].astype(F32)
                y_ref[pl.ds(t0 + j, batch, stride=pitch), :] = wj[0:batch]
                y_ref[pl.ds(t0 + q + j, batch, stride=pitch), :] = wj[batch:slab]
        return carry

    lax.fori_loop(0, n_chunks // 2, scatter, 0, unroll=SSM_RELAYOUT_UNROLL)


def _ssm(u, seq, lam_re, lam_im, log_dt, b_re, b_im, c_re, c_im, d_skip):
    batch, pitch, _ = u.shape
    rows = batch * seq // SSM_CHUNK
    ein, toep, eout, a_q = _ssm_operators(lam_re, lam_im, log_dt, b_re, b_im, c_re, c_im, d_skip)
    gpb = GROUPS_PER_BLOCK
    mat = pl.BlockSpec((gpb, CHUNK_WIDTH, CHUNK_WIDTH), lambda s: (s, 0, 0))
    col = pl.BlockSpec((batch * pitch, LANES), lambda s: (0, s), pipeline_mode=pl.Buffered(1))
    y = pl.pallas_call(
        functools.partial(_ssm_kernel, batch=batch, seq=seq, pitch=pitch),
        grid=(SSM_GROUPS // gpb,),
        in_specs=[col, mat, mat, mat,
                  pl.BlockSpec((gpb, SUBLANES, 2 * SSM_STATE), lambda s: (s, 0, 0))],
        out_specs=col,
        out_shape=jax.ShapeDtypeStruct((batch * pitch, SSM_WIDTH), F32),
        scratch_shapes=[pltpu.VMEM((gpb, rows, CHUNK_WIDTH), BF16),
                        pltpu.VMEM((SSM_INTERLEAVE, rows, 4 * SSM_STATE), F32),
                        pltpu.VMEM((SSM_INTERLEAVE, rows, 4 * SSM_STATE), F32)],
        compiler_params=pltpu.CompilerParams(
            dimension_semantics=("parallel",), vmem_limit_bytes=VMEM_LIMIT_BYTES),
    )(u.reshape(batch * pitch, SSM_WIDTH), ein, toep, eout, a_q)
    return y.reshape(batch, pitch, SSM_WIDTH)


def _mix_kernel(x_ref, attn_ref, ssm_ref, gw_ref, gb_ref, ga_ref, gs_ref, wo_ref, o_ref):
    y = jax.nn.gelu(ssm_ref[...])
    z = jnp.dot(y.astype(BF16), gw_ref[...], preferred_element_type=F32) + gb_ref[...]
    s = y * jax.nn.sigmoid(z)
    sn = _rms(s) * gs_ref[...]
    an = _rms(attn_ref[...].astype(F32)) * ga_ref[...]
    mixed = jnp.concatenate([an, sn], axis=-1).astype(BF16)
    o_ref[...] = x_ref[...] + jnp.dot(mixed, wo_ref[...], preferred_element_type=F32)


def _mix_out(x, attn, ssm_pre, seq, glu_w, glu_b, attn_gain, ssm_gain, w_out):
    t = x.shape[0]
    row = lambda width: pl.BlockSpec((TOKEN_TILE, width), lambda i: (i, 0))
    full = lambda r, c: pl.BlockSpec((r, c), lambda i: (0, 0))
    return pl.pallas_call(
        _mix_kernel,
        grid=(t // TOKEN_TILE,),
        in_specs=[row(D_MODEL), row(ATTN_WIDTH), _batch_rows(seq, SSM_WIDTH),
                  full(SSM_WIDTH, SSM_WIDTH), full(1, SSM_WIDTH),
                  full(1, ATTN_WIDTH), full(1, SSM_WIDTH), full(D_MODEL, D_MODEL)],
        out_specs=row(D_MODEL),
        out_shape=jax.ShapeDtypeStruct((t, D_MODEL), F32),
        compiler_params=pltpu.CompilerParams(
            dimension_semantics=("parallel",), vmem_limit_bytes=VMEM_LIMIT_BYTES),
    )(x, attn, ssm_pre, glu_w.astype(BF16), glu_b.reshape(1, -1).astype(F32),
      attn_gain.reshape(1, -1).astype(F32), ssm_gain.reshape(1, -1).astype(F32),
      w_out.astype(BF16))


def kernel(x, norm_ffn1, ffn1_w_gate, ffn1_w_up, ffn1_w_down, norm_mix, w_in, attn_sinks,
           ssm_lambda_re, ssm_lambda_im, ssm_log_dt, ssm_b_re, ssm_b_im, ssm_c_re, ssm_c_im,
           ssm_d, ssm_glu_w, ssm_glu_b, attn_out_norm, ssm_out_norm, w_out,
           norm_ffn2, ffn2_w_gate, ffn2_w_up, ffn2_w_down, final_norm):
    b, seq, d = x.shape
    depth = norm_ffn1.shape[0]
    assert d == D_MODEL and seq % BLOCK == 0 and seq % TOKEN_TILE == 0
    assert b == SUBLANES and seq % (2 * SSM_CHUNK * SSM_RELAYOUT_UNROLL) == 0
    h = x.reshape(b * seq, d).astype(F32)
    for l in range(depth):
        h = _ffn(h, norm_ffn1[l], ffn1_w_gate[l], ffn1_w_up[l], ffn1_w_down[l],
                 final_norm, final_norm=False)
        q, k, v, u = _in_proj(h, norm_mix[l], w_in[l], b, seq)
        attn = _attention(q.reshape(b, seq, -1), k.reshape(b, seq, -1), v.reshape(b, seq, -1),
                          attn_sinks[l])
        ssm_pre = _ssm(u, seq, ssm_lambda_re[l], ssm_lambda_im[l], ssm_log_dt[l],
                       ssm_b_re[l], ssm_b_im[l], ssm_c_re[l], ssm_c_im[l], ssm_d[l])
        h = _mix_out(h, attn.reshape(b * seq, -1), ssm_pre, seq, ssm_glu_w[l], ssm_glu_b[l],
                     attn_out_norm[l], ssm_out_norm[l], w_out[l])
        h = _ffn(h, norm_ffn2[l], ffn2_w_gate[l], ffn2_w_up[l], ffn2_w_down[l],
                 final_norm, final_norm=(l == depth - 1))
    return h.reshape(b, seq, d).astype(x.dtype)
```

```python
import functools

import jax
import jax.numpy as jnp
from jax import lax
from jax.experimental import pallas as pl
from jax.experimental.pallas import tpu as pltpu

F32 = jnp.float32
BF16 = jnp.bfloat16

D_MODEL = 1024
ATTN_HEADS = 8
ATTN_KV_HEADS = 2
Q_PER_KV = ATTN_HEADS // ATTN_KV_HEADS
HEAD_DIM = 64
ATTN_WIDTH = ATTN_HEADS * HEAD_DIM
KV_WIDTH = ATTN_KV_HEADS * HEAD_DIM
WINDOW = 128
BLOCK = 128
SSM_CH = 16
SSM_WIDTH = D_MODEL - ATTN_WIDTH
SSM_GROUPS = SSM_WIDTH // SSM_CH
SSM_STATE = 64
IN_WIDTH = ATTN_WIDTH + 2 * KV_WIDTH + SSM_WIDTH
D_FF = 2816
EPS = 1e-6
NEG_INF = -1e30
LAMBDA_RE_MAX = -1e-4

LANES = 128
SUBLANES = 8
VMEM_LIMIT_BYTES = 56 * 1024 * 1024

TOKEN_TILE = 1024
FFN_TOKEN_TILE = 512
FF_TILE = 256
SSM_CHUNK = 16
CHUNK_WIDTH = SSM_CHUNK * SSM_CH
GROUPS_PER_BLOCK = LANES // SSM_CH
SSM_INTERLEAVE = 4
SSM_RELAYOUT_UNROLL = 4
OPS_GROUPS_PER_STEP = 4

NT_DIMS = (((1,), (1,)), ((), ()))


def _rms(x):
    return x * lax.rsqrt(jnp.mean(x * x, axis=-1, keepdims=True) + EPS)


def _ffn_kernel(x_ref, gain_ref, wg_ref, wu_ref, wd_ref, fgain_ref, o_ref, act_ref, *, final_norm):
    x = x_ref[...]
    hn = (_rms(x) * gain_ref[...]).astype(BF16)
    for j in range(D_FF // FF_TILE):
        cols = slice(j * FF_TILE, (j + 1) * FF_TILE)
        g = jnp.dot(hn, wg_ref[:, cols], preferred_element_type=F32)
        u = jnp.dot(hn, wu_ref[:, cols], preferred_element_type=F32)
        act_ref[:, cols] = (g * jax.nn.sigmoid(g) * u).astype(BF16)
    y = x + 0.5 * jnp.dot(act_ref[...], wd_ref[...], preferred_element_type=F32)
    if final_norm:
        y = _rms(y) * fgain_ref[...]
    o_ref[...] = y


def _ffn(x, gain, w_gate, w_up, w_down, final_gain, final_norm):
    t = x.shape[0]
    resident = lambda r, c: pl.BlockSpec((r, c), lambda i: (0, 0), pipeline_mode=pl.Buffered(1))
    return pl.pallas_call(
        functools.partial(_ffn_kernel, final_norm=final_norm),
        grid=(t // FFN_TOKEN_TILE,),
        in_specs=[
            pl.BlockSpec((FFN_TOKEN_TILE, D_MODEL), lambda i: (i, 0)),
            resident(1, D_MODEL),
            resident(D_MODEL, D_FF),
            resident(D_MODEL, D_FF),
            resident(D_FF, D_MODEL),
            resident(1, D_MODEL),
        ],
        out_specs=pl.BlockSpec((FFN_TOKEN_TILE, D_MODEL), lambda i: (i, 0)),
        out_shape=jax.ShapeDtypeStruct((t, D_MODEL), F32),
        scratch_shapes=[pltpu.VMEM((FFN_TOKEN_TILE, D_FF), BF16)],
        compiler_params=pltpu.CompilerParams(
            dimension_semantics=("parallel",), vmem_limit_bytes=VMEM_LIMIT_BYTES),
    )(x, gain.reshape(1, D_MODEL), w_gate.astype(BF16), w_up.astype(BF16), w_down.astype(BF16),
      final_gain.reshape(1, D_MODEL))


def _in_proj_kernel(x_ref, gain_ref, w_ref, q_ref, k_ref, v_ref, u_ref):
    hn = (_rms(x_ref[...]) * gain_ref[...]).astype(BF16)
    proj = jnp.dot(hn, w_ref[...], preferred_element_type=F32)
    q_ref[...] = proj[:, :ATTN_WIDTH].astype(BF16)
    k_ref[...] = proj[:, ATTN_WIDTH:ATTN_WIDTH + KV_WIDTH].astype(BF16)
    v_ref[...] = proj[:, ATTN_WIDTH + KV_WIDTH:ATTN_WIDTH + 2 * KV_WIDTH].astype(BF16)
    u_ref[...] = proj[:, ATTN_WIDTH + 2 * KV_WIDTH:]


def _padded_seq(seq):
    return seq + SUBLANES if seq % (2 * SUBLANES) == 0 else seq


def _batch_rows(seq, width):
    tiles = seq // TOKEN_TILE
    return pl.BlockSpec((None, TOKEN_TILE, width), lambda i: (i // tiles, i % tiles, 0))


def _in_proj(x, gain, w_in, batch, seq):
    t = x.shape[0]
    row = lambda width: pl.BlockSpec((TOKEN_TILE, width), lambda i: (i, 0))
    return pl.pallas_call(
        _in_proj_kernel,
        grid=(t // TOKEN_TILE,),
        in_specs=[
            row(D_MODEL),
            pl.BlockSpec((1, D_MODEL), lambda i: (0, 0)),
            pl.BlockSpec((D_MODEL, IN_WIDTH), lambda i: (0, 0)),
        ],
        out_specs=[row(ATTN_WIDTH), row(KV_WIDTH), row(KV_WIDTH), _batch_rows(seq, SSM_WIDTH)],
        out_shape=[
            jax.ShapeDtypeStruct((t, ATTN_WIDTH), BF16),
            jax.ShapeDtypeStruct((t, KV_WIDTH), BF16),
            jax.ShapeDtypeStruct((t, KV_WIDTH), BF16),
            jax.ShapeDtypeStruct((batch, _padded_seq(seq), SSM_WIDTH), F32),
        ],
        compiler_params=pltpu.CompilerParams(
            dimension_semantics=("parallel",), vmem_limit_bytes=VMEM_LIMIT_BYTES),
    )(x, gain.reshape(1, D_MODEL), w_in.astype(BF16))


def _attn_kernel(sink_ref, q_ref, kp_ref, kc_ref, kn_ref, vp_ref, vc_ref, vn_ref, o_ref, *, seq):
    n = pl.program_id(1)
    qi = lax.broadcasted_iota(jnp.int32, (BLOCK, 3 * BLOCK), 0)
    kj = lax.broadcasted_iota(jnp.int32, (BLOCK, 3 * BLOCK), 1)
    rel = jnp.abs(kj - BLOCK - qi)
    key_pos = kj + (n - 1) * BLOCK
    valid = (rel <= WINDOW) & (key_pos >= 0) & (key_pos < seq)
    dist = rel.astype(F32)

    outs = []
    for kh in range(ATTN_KV_HEADS):
        cols = slice(kh * HEAD_DIM, (kh + 1) * HEAD_DIM)
        kcat = jnp.concatenate([kp_ref[:, cols], kc_ref[:, cols], kn_ref[:, cols]], axis=0)
        vcat = jnp.concatenate([vp_ref[:, cols], vc_ref[:, cols], vn_ref[:, cols]], axis=0)
        heads = [kh * Q_PER_KV + g for g in range(Q_PER_KV)]
        qs = jnp.concatenate([q_ref[:, h * HEAD_DIM:(h + 1) * HEAD_DIM] for h in heads], axis=0)
        s_all = lax.dot_general(qs, kcat, NT_DIMS, preferred_element_type=F32) * (HEAD_DIM ** -0.5)
        for g, h in enumerate(heads):
            slope = float(2.0 ** (-8.0 * (h + 1) / ATTN_HEADS))
            s = s_all[g * BLOCK:(g + 1) * BLOCK]
            s = jnp.where(valid, s - slope * dist, NEG_INF)
            sink = sink_ref[h]
            m = jnp.maximum(jnp.max(s, axis=-1, keepdims=True), sink)
            e = jnp.exp(s - m)
            den = jnp.sum(e, axis=-1, keepdims=True) + jnp.exp(sink - m)
            pv = jnp.dot(e.astype(BF16), vcat, preferred_element_type=F32)
            outs.append(pv / den)
    o_ref[...] = jnp.concatenate(outs, axis=-1).astype(o_ref.dtype)


def _attention(q, k, v, sinks):
    b, seq, _ = q.shape
    nb = seq // BLOCK
    kv_spec = lambda f: pl.BlockSpec((None, BLOCK, KV_WIDTH), f)
    prev = lambda bi, n: (bi, jnp.maximum(n - 1, 0), 0)
    cur = lambda bi, n: (bi, n, 0)
    nxt = lambda bi, n: (bi, jnp.minimum(n + 1, nb - 1), 0)
    return pl.pallas_call(
        functools.partial(_attn_kernel, seq=seq),
        grid=(b, nb),
        in_specs=[
            pl.BlockSpec(memory_space=pltpu.SMEM),
            pl.BlockSpec((None, BLOCK, ATTN_WIDTH), cur),
            kv_spec(prev), kv_spec(cur), kv_spec(nxt),
            kv_spec(prev), kv_spec(cur), kv_spec(nxt),
        ],
        out_specs=pl.BlockSpec((None, BLOCK, ATTN_WIDTH), cur),
        out_shape=jax.ShapeDtypeStruct((b, seq, ATTN_WIDTH), BF16),
        compiler_params=pltpu.CompilerParams(
            dimension_semantics=("parallel", "parallel"), vmem_limit_bytes=VMEM_LIMIT_BYTES),
    )(sinks.astype(F32), q, k, k, k, v, v, v)


def _ssm_ops_kernel(prm_ref, bc_ref, ein_ref, toep_ref, eout_ref, aq_ref):
    q, hc, p = SSM_CHUNK, SSM_CH, SSM_STATE
    hi = lax.Precision.HIGHEST
    fwd = lax.broadcasted_iota(jnp.int32, (1, 2 * p), 1) < p
    zero_row = jnp.zeros((1, 2 * p), F32)
    row_id = lax.broadcasted_iota(jnp.int32, (CHUNK_WIDTH, CHUNK_WIDTH), 0)
    col_id = lax.broadcasted_iota(jnp.int32, (CHUNK_WIDTH, CHUNK_WIDTH), 1)

    def table(select, n):
        picks = [select(m) for m in range(n)]
        re = jnp.concatenate([jnp.broadcast_to(r, (hc, 2 * p)) for r, _ in picks], axis=0)
        im = jnp.concatenate([jnp.broadcast_to(i, (hc, 2 * p)) for _, i in picks], axis=0)
        return re, im

    def tile_rows(x, n):
        return jnp.concatenate([x] * n, axis=0)

    for gi in range(prm_ref.shape[0]):
        lr = jnp.minimum(prm_ref[gi, 0:1, :], LAMBDA_RE_MAX)
        li = prm_ref[gi, 1:2, :]
        dt = jnp.exp(prm_ref[gi, 2:3, :])
        mag = jnp.exp(lr * dt)
        a_r = mag * jnp.cos(li * dt)
        a_i = mag * jnp.sin(li * dt)
        den = lr * lr + li * li
        coef_r = ((a_r - 1.0) * lr + a_i * li) / den
        coef_i = (a_i * lr - (a_r - 1.0) * li) / den
        b_r, b_i = bc_ref[gi, 0], bc_ref[gi, 1]
        c_r, c_i = bc_ref[gi, 2], bc_ref[gi, 3]
        bb_r = coef_r * b_r - coef_i * b_i
        bb_i = coef_r * b_i + coef_i * b_r

        pw = [(jnp.ones((1, 2 * p), F32), zero_row)]
        for _ in range(q):
            r, i = pw[-1]
            pw.append((r * a_r - i * a_i, r * a_i + i * a_r))

        def both(f_idx, b_idx):
            fr, fi = pw[f_idx] if f_idx is not None else (zero_row, zero_row)
            br, bi = pw[b_idx] if b_idx is not None else (zero_row, zero_row)
            return jnp.where(fwd, fr, br), jnp.where(fwd, fi, bi)

        p_r, p_i = table(lambda i: both(q - 1 - i, i), q)
        tb_r, tb_i = tile_rows(bb_r, q), tile_rows(bb_i, q)
        ein = jnp.concatenate([tb_r * p_r - tb_i * p_i, tb_r * p_i + tb_i * p_r], axis=1)
        ein_ref[gi] = ein.astype(BF16)

        p_r, p_i = table(lambda j: both(j + 1, q - j), q)
        tc_r, tc_i = tile_rows(c_r, q), tile_rows(c_i, q)
        eout = jnp.concatenate([tc_r * p_r - tc_i * p_i, -(tc_r * p_i + tc_i * p_r)], axis=1)
        eout_ref[gi] = eout.astype(BF16)

        def lag(m):
            return both(m - (q - 1) if q - 1 <= m <= 2 * q - 2 else None,
                        (q - 1) - m if m <= q - 1 else None)

        p_r, p_i = table(lag, 2 * q)
        tc_r, tc_i = tile_rows(c_r, 2 * q), tile_rows(c_i, 2 * q)
        cpt = jnp.concatenate([tc_r * p_r - tc_i * p_i, tc_r * p_i + tc_i * p_r], axis=1)
        bcat = jnp.concatenate([bb_r, -bb_i], axis=1)
        kern = lax.dot_general(bcat, cpt, NT_DIMS, precision=hi, preferred_element_type=F32)
        toep = jnp.concatenate(
            [kern[:, hc * (q - 1 - i):hc * (q - 1 - i) + CHUNK_WIDTH] for i in range(q)], axis=0)
        skip = jnp.concatenate([prm_ref[gi, 3:4, :]] * (CHUNK_WIDTH // (2 * p)), axis=1)
        toep_ref[gi] = (toep + jnp.where(row_id == col_id, skip, 0.0)).astype(BF16)

        aq_ref[gi] = jnp.concatenate(
            [pw[q][0], pw[q][1], jnp.zeros((SUBLANES - 2, 2 * p), F32)], axis=0)


def _ssm_operators(lam_re, lam_im, log_dt, b_re, b_im, c_re, c_im, d_skip):
    g, p, hc = SSM_GROUPS, SSM_STATE, SSM_CH
    lanes = lambda a: a.astype(F32).transpose(1, 0, 2).reshape(g, 2 * p)
    prm = jnp.stack(
        [lanes(lam_re), lanes(lam_im),
         jnp.repeat(log_dt.astype(F32).T, p, axis=1),
         jnp.tile(d_skip.astype(F32), (1, 2 * p // hc))]
        + [jnp.zeros((g, 2 * p), F32)] * (SUBLANES - 4), axis=1)
    bc = jnp.stack(
        [b_re.astype(F32).transpose(1, 3, 0, 2).reshape(g, hc, 2 * p),
         b_im.astype(F32).transpose(1, 3, 0, 2).reshape(g, hc, 2 * p),
         c_re.astype(F32).transpose(1, 2, 0, 3).reshape(g, hc, 2 * p),
         c_im.astype(F32).transpose(1, 2, 0, 3).reshape(g, hc, 2 * p)], axis=1)
    gs = OPS_GROUPS_PER_STEP
    mat = pl.BlockSpec((gs, CHUNK_WIDTH, CHUNK_WIDTH), lambda s: (s, 0, 0))
    mat_shape = jax.ShapeDtypeStruct((g, CHUNK_WIDTH, CHUNK_WIDTH), BF16)
    return pl.pallas_call(
        _ssm_ops_kernel,
        grid=(g // gs,),
        in_specs=[pl.BlockSpec((gs, SUBLANES, 2 * p), lambda s: (s, 0, 0)),
                  pl.BlockSpec((gs, 4, hc, 2 * p), lambda s: (s, 0, 0, 0))],
        out_specs=[mat, mat, mat, pl.BlockSpec((gs, SUBLANES, 2 * p), lambda s: (s, 0, 0))],
        out_shape=[mat_shape, mat_shape, mat_shape,
                   jax.ShapeDtypeStruct((g, SUBLANES, 2 * p), F32)],
        compiler_params=pltpu.CompilerParams(
            dimension_semantics=("parallel",), vmem_limit_bytes=VMEM_LIMIT_BYTES),
    )(prm, bc)


def _lane_roll(x, shift):
    shift %= LANES
    return jnp.concatenate([x[:, LANES - shift:], x[:, :LANES - shift]], axis=1)


def _block_transpose(v, lane_block):
    v = list(v)
    n = len(v)
    d = n // 2
    while d >= 1:
        upper = (lane_block & d) != 0
        for i in range(n):
            if i & d == 0:
                a, b = v[i], v[i + d]
                v[i] = jnp.where(upper, _lane_roll(b, SSM_CH * d), a)
                v[i + d] = jnp.where(upper, b, _lane_roll(a, -SSM_CH * d))
        d //= 2
    return v


def _ssm_kernel(u_ref, ein_ref, toep_ref, eout_ref, aq_ref, y_ref, ug_ref, s_ref, x_ref,
                *, batch, seq, pitch):
    q, half = SSM_CHUNK, SSM_STATE
    n_chunks = seq // q
    gpb = GROUPS_PER_BLOCK
    slab = 2 * batch
    lane_block = lax.broadcasted_iota(jnp.int32, (slab, LANES), 1) // SSM_CH

    for b in range(batch if pitch > seq else 0):
        y_ref[b * pitch + seq:(b + 1) * pitch, :] = jnp.zeros((pitch - seq, LANES), F32)

    def gather(cp, carry):
        rows = pl.ds(pl.multiple_of(cp * slab, slab), slab)
        for part in range(q // gpb):
            t0 = cp * 2 * q + part * gpb
            v = [jnp.concatenate([u_ref[pl.ds(t0 + i, batch, stride=pitch), :],
                                  u_ref[pl.ds(t0 + q + i, batch, stride=pitch), :]],
                                 axis=0).astype(BF16) for i in range(gpb)]
            w = _block_transpose(v, lane_block)
            for g in range(gpb):
                ug_ref[g, rows, part * LANES:(part + 1) * LANES] = w[g]
        return carry

    lax.fori_loop(0, n_chunks // 2, gather, 0, unroll=SSM_RELAYOUT_UNROLL)

    fwd_lane = lax.broadcasted_iota(jnp.int32, (batch, 2 * half), 1) < half
    zeros = jnp.zeros((batch, half), F32)
    last_rows = pl.ds((n_chunks - 1) * batch, batch)
    ni = SSM_INTERLEAVE
    for g0 in range(0, gpb, ni):
        for gi in range(ni):
            s_ref[gi] = jnp.dot(ug_ref[g0 + gi], ein_ref[g0 + gi], preferred_element_type=F32)
            x_ref[gi, 0:batch, 0:half] = zeros
            x_ref[gi, 0:batch, 2 * half:3 * half] = zeros
            x_ref[gi, last_rows, half:2 * half] = zeros
            x_ref[gi, last_rows, 3 * half:4 * half] = zeros

        def step(k, carry):
            rf = pl.multiple_of(k * batch, batch)
            rb = pl.multiple_of((n_chunks - 1 - k) * batch, batch)
            new = []
            for gi in range(ni):
                xr, xi = carry[gi]
                sre = jnp.where(fwd_lane, s_ref[gi, pl.ds(rf, batch), 0:2 * half],
                                s_ref[gi, pl.ds(rb, batch), 0:2 * half])
                sim = jnp.where(fwd_lane, s_ref[gi, pl.ds(rf, batch), 2 * half:4 * half],
                                s_ref[gi, pl.ds(rb, batch), 2 * half:4 * half])
                ar = aq_ref[g0 + gi, 0:1, :]
                ai = aq_ref[g0 + gi, 1:2, :]
                nr = ar * xr - ai * xi + sre
                nim = ar * xi + ai * xr + sim
                x_ref[gi, pl.ds(rf + batch, batch), 0:half] = nr[:, 0:half]
                x_ref[gi, pl.ds(rf + batch, batch), 2 * half:3 * half] = nim[:, 0:half]
                x_ref[gi, pl.ds(rb - batch, batch), half:2 * half] = nr[:, half:2 * half]
                x_ref[gi, pl.ds(rb - batch, batch), 3 * half:4 * half] = nim[:, half:2 * half]
                new.append((nr, nim))
            return tuple(new)

        init = tuple((jnp.zeros((batch, 2 * half), F32), jnp.zeros((batch, 2 * half), F32))
                     for _ in range(ni))
        lax.fori_loop(0, n_chunks - 1, step, init)

        for gi in range(ni):
            g = g0 + gi
            y = jnp.dot(ug_ref[g], toep_ref[g], preferred_element_type=F32)
            y += lax.dot_general(x_ref[gi].astype(BF16), eout_ref[g], NT_DIMS,
                                 preferred_element_type=F32)
            ug_ref[g] = y.astype(BF16)

    def scatter(cp, carry):
        rows = pl.ds(pl.multiple_of(cp * slab, slab), slab)
        for part in range(q // gpb):
            t0 = cp * 2 * q + part * gpb
            v = [ug_ref[g, rows, part * LANES:(part + 1) * LANES] for g in range(gpb)]
            w = _block_transpose(v, lane_block)
            for j in range(gpb):
                wj = w[j].astype(F32)
                y_ref[pl.ds(t0 + j, batch, stride=pitch), :] = wj[0:batch]
                y_ref[pl.ds(t0 + q + j, batch, stride=pitch), :] = wj[batch:slab]
        return carry

    lax.fori_loop(0, n_chunks // 2, scatter, 0, unroll=SSM_RELAYOUT_UNROLL)


def _ssm(u, seq, lam_re, lam_im, log_dt, b_re, b_im, c_re, c_im, d_skip):
    batch, pitch, _ = u.shape
    rows = batch * seq // SSM_CHUNK
    ein, toep, eout, a_q = _ssm_operators(lam_re, lam_im, log_dt, b_re, b_im, c_re, c_im, d_skip)
    gpb = GROUPS_PER_BLOCK
    mat = pl.BlockSpec((gpb, CHUNK_WIDTH, CHUNK_WIDTH), lambda s: (s, 0, 0))
    col = pl.BlockSpec((batch * pitch, LANES), lambda s: (0, s), pipeline_mode=pl.Buffered(1))
    y = pl.pallas_call(
        functools.partial(_ssm_kernel, batch=batch, seq=seq, pitch=pitch),
        grid=(SSM_GROUPS // gpb,),
        in_specs=[col, mat, mat, mat,
                  pl.BlockSpec((gpb, SUBLANES, 2 * SSM_STATE), lambda s: (s, 0, 0))],
        out_specs=col,
        out_shape=jax.ShapeDtypeStruct((batch * pitch, SSM_WIDTH), F32),
        scratch_shapes=[pltpu.VMEM((gpb, rows, CHUNK_WIDTH), BF16),
                        pltpu.VMEM((SSM_INTERLEAVE, rows, 4 * SSM_STATE), F32),
                        pltpu.VMEM((SSM_INTERLEAVE, rows, 4 * SSM_STATE), F32)],
        compiler_params=pltpu.CompilerParams(
            dimension_semantics=("parallel",), vmem_limit_bytes=VMEM_LIMIT_BYTES),
    )(u.reshape(batch * pitch, SSM_WIDTH), ein, toep, eout, a_q)
    return y.reshape(batch, pitch, SSM_WIDTH)


def _mix_kernel(x_ref, attn_ref, ssm_ref, gw_ref, gb_ref, ga_ref, gs_ref, wo_ref, o_ref):
    y = jax.nn.gelu(ssm_ref[...])
    z = jnp.dot(y.astype(BF16), gw_ref[...], preferred_element_type=F32) + gb_ref[...]
    s = y * jax.nn.sigmoid(z)
    sn = _rms(s) * gs_ref[...]
    an = _rms(attn_ref[...].astype(F32)) * ga_ref[...]
    mixed = jnp.concatenate([an, sn], axis=-1).astype(BF16)
    o_ref[...] = x_ref[...] + jnp.dot(mixed, wo_ref[...], preferred_element_type=F32)


def _mix_out(x, attn, ssm_pre, seq, glu_w, glu_b, attn_gain, ssm_gain, w_out):
    t = x.shape[0]
    row = lambda width: pl.BlockSpec((TOKEN_TILE, width), lambda i: (i, 0))
    full = lambda r, c: pl.BlockSpec((r, c), lambda i: (0, 0))
    return pl.pallas_call(
        _mix_kernel,
        grid=(t // TOKEN_TILE,),
        in_specs=[row(D_MODEL), row(ATTN_WIDTH), _batch_rows(seq, SSM_WIDTH),
                  full(SSM_WIDTH, SSM_WIDTH), full(1, SSM_WIDTH),
                  full(1, ATTN_WIDTH), full(1, SSM_WIDTH), full(D_MODEL, D_MODEL)],
        out_specs=row(D_MODEL),
        out_shape=jax.ShapeDtypeStruct((t, D_MODEL), F32),
        compiler_params=pltpu.CompilerParams(
            dimension_semantics=("parallel",), vmem_limit_bytes=VMEM_LIMIT_BYTES),
    )(x, attn, ssm_pre, glu_w.astype(BF16), glu_b.reshape(1, -1).astype(F32),
      attn_gain.reshape(1, -1).astype(F32), ssm_gain.reshape(1, -1).astype(F32),
      w_out.astype(BF16))


def kernel(x, norm_ffn1, ffn1_w_gate, ffn1_w_up, ffn1_w_down, norm_mix, w_in, attn_sinks,
           ssm_lambda_re, ssm_lambda_im, ssm_log_dt, ssm_b_re, ssm_b_im, ssm_c_re, ssm_c_im,
           ssm_d, ssm_glu_w, ssm_glu_b, attn_out_norm, ssm_out_norm, w_out,
           norm_ffn2, ffn2_w_gate, ffn2_w_up, ffn2_w_down, final_norm):
    b, seq, d = x.shape
    depth = norm_ffn1.shape[0]
    assert d == D_MODEL and seq % BLOCK == 0 and seq % TOKEN_TILE == 0
    assert b == SUBLANES and seq % (2 * SSM_CHUNK * SSM_RELAYOUT_UNROLL) == 0
    h = x.reshape(b * seq, d).astype(F32)
    for l in range(depth):
        h = _ffn(h, norm_ffn1[l], ffn1_w_gate[l], ffn1_w_up[l], ffn1_w_down[l],
                 final_norm, final_norm=False)
        q, k, v, u = _in_proj(h, norm_mix[l], w_in[l], b, seq)
        attn = _attention(q.reshape(b, seq, -1), k.reshape(b, seq, -1), v.reshape(b, seq, -1),
                          attn_sinks[l])
        ssm_pre = _ssm(u, seq, ssm_lambda_re[l], ssm_lambda_im[l], ssm_log_dt[l],
                       ssm_b_re[l], ssm_b_im[l], ssm_c_re[l], ssm_c_im[l], ssm_d[l])
        h = _mix_out(h, attn.reshape(b * seq, -1), ssm_pre, seq, ssm_glu_w[l], ssm_glu_b[l],
                     attn_out_norm[l], ssm_out_norm[l], w_out[l])
        h = _ffn(h, norm_ffn2[l], ffn2_w_gate[l], ffn2_w_up[l], ffn2_w_down[l],
                 final_norm, final_norm=(l == depth - 1))
    return h.reshape(b, seq, d).astype(x.dtype)
```

```python
import functools

import jax
import jax.numpy as jnp
from jax import lax
from jax.experimental import pallas as pl
from jax.experimental.pallas import tpu as pltpu

F32 = jnp.float32
BF16 = jnp.bfloat16

D_MODEL = 1024
ATTN_HEADS = 8
ATTN_KV_HEADS = 2
Q_PER_KV = ATTN_HEADS // ATTN_KV_HEADS
HEAD_DIM = 64
ATTN_WIDTH = ATTN_HEADS * HEAD_DIM
KV_WIDTH = ATTN_KV_HEADS * HEAD_DIM
WINDOW = 128
BLOCK = 128
SSM_CH = 16
SSM_WIDTH = D_MODEL - ATTN_WIDTH
SSM_GROUPS = SSM_WIDTH // SSM_CH
SSM_STATE = 64
IN_WIDTH = ATTN_WIDTH + 2 * KV_WIDTH + SSM_WIDTH
D_FF = 2816
EPS = 1e-6
NEG_INF = -1e30
LAMBDA_RE_MAX = -1e-4

LANES = 128
SUBLANES = 8
VMEM_LIMIT_BYTES = 56 * 1024 * 1024

TOKEN_TILE = 1024
FFN_TOKEN_TILE = 512
FF_TILE = 256
SSM_CHUNK = 16
CHUNK_WIDTH = SSM_CHUNK * SSM_CH
GROUPS_PER_BLOCK = LANES // SSM_CH
SSM_INTERLEAVE = 4
SSM_RELAYOUT_UNROLL = 4
OPS_GROUPS_PER_STEP = 4

NT_DIMS = (((1,), (1,)), ((), ()))


def _rms(x):
    return x * lax.rsqrt(jnp.mean(x * x, axis=-1, keepdims=True) + EPS)


def _ffn_kernel(x_ref, gain_ref, wg_ref, wu_ref, wd_ref, fgain_ref, o_ref, act_ref, *, final_norm):
    x = x_ref[...]
    hn = (_rms(x) * gain_ref[...]).astype(BF16)
    for j in range(D_FF // FF_TILE):
        cols = slice(j * FF_TILE, (j + 1) * FF_TILE)
        g = jnp.dot(hn, wg_ref[:, cols], preferred_element_type=F32)
        u = jnp.dot(hn, wu_ref[:, cols], preferred_element_type=F32)
        act_ref[:, cols] = (g * jax.nn.sigmoid(g) * u).astype(BF16)
    y = x + 0.5 * jnp.dot(act_ref[...], wd_ref[...], preferred_element_type=F32)
    if final_norm:
        y = _rms(y) * fgain_ref[...]
    o_ref[...] = y


def _ffn(x, gain, w_gate, w_up, w_down, final_gain, final_norm):
    t = x.shape[0]
    resident = lambda r, c: pl.BlockSpec((r, c), lambda i: (0, 0), pipeline_mode=pl.Buffered(1))
    return pl.pallas_call(
        functools.partial(_ffn_kernel, final_norm=final_norm),
        grid=(t // FFN_TOKEN_TILE,),
        in_specs=[
            pl.BlockSpec((FFN_TOKEN_TILE, D_MODEL), lambda i: (i, 0)),
            resident(1, D_MODEL),
            resident(D_MODEL, D_FF),
            resident(D_MODEL, D_FF),
            resident(D_FF, D_MODEL),
            resident(1, D_MODEL),
        ],
        out_specs=pl.BlockSpec((FFN_TOKEN_TILE, D_MODEL), lambda i: (i, 0)),
        out_shape=jax.ShapeDtypeStruct((t, D_MODEL), F32),
        scratch_shapes=[pltpu.VMEM((FFN_TOKEN_TILE, D_FF), BF16)],
        compiler_params=pltpu.CompilerParams(
            dimension_semantics=("parallel",), vmem_limit_bytes=VMEM_LIMIT_BYTES),
    )(x, gain.reshape(1, D_MODEL), w_gate.astype(BF16), w_up.astype(BF16), w_down.astype(BF16),
      final_gain.reshape(1, D_MODEL))


def _in_proj_kernel(x_ref, gain_ref, w_ref, q_ref, k_ref, v_ref, u_ref, *, seq):
    j = pl.program_id(1)
    hn = (_rms(x_ref[...]) * gain_ref[...]).astype(BF16)
    proj = jnp.dot(hn, w_ref[...], preferred_element_type=F32)
    q_ref[...] = proj[:, :ATTN_WIDTH].astype(BF16)
    k_ref[...] = proj[:, ATTN_WIDTH:ATTN_WIDTH + KV_WIDTH].astype(BF16)
    v_ref[...] = proj[:, ATTN_WIDTH + KV_WIDTH:ATTN_WIDTH + 2 * KV_WIDTH].astype(BF16)
    rows = pl.ds(pl.multiple_of(j * TOKEN_TILE, TOKEN_TILE), TOKEN_TILE)
    u_ref[rows, :] = proj[:, ATTN_WIDTH + 2 * KV_WIDTH:]
    pitch = u_ref.shape[0]
    if pitch > seq:
        @pl.when(j == 0)
        def _():
            u_ref[seq:pitch, :] = jnp.zeros((pitch - seq, SSM_WIDTH), F32)


def _padded_seq(seq):
    return seq + SUBLANES if seq % (2 * SUBLANES) == 0 else seq


def _batch_rows(seq, width):
    tiles = seq // TOKEN_TILE
    return pl.BlockSpec((None, TOKEN_TILE, width), lambda i: (i // tiles, i % tiles, 0))


def _in_proj(x, gain, w_in, batch, seq):
    t = x.shape[0]
    tiles = seq // TOKEN_TILE
    pitch = _padded_seq(seq)
    row = lambda width: pl.BlockSpec((TOKEN_TILE, width), lambda b, j: (b * tiles + j, 0))
    return pl.pallas_call(
        functools.partial(_in_proj_kernel, seq=seq),
        grid=(batch, tiles),
        in_specs=[
            row(D_MODEL),
            pl.BlockSpec((1, D_MODEL), lambda b, j: (0, 0)),
            pl.BlockSpec((D_MODEL, IN_WIDTH), lambda b, j: (0, 0)),
        ],
        out_specs=[row(ATTN_WIDTH), row(KV_WIDTH), row(KV_WIDTH),
                   pl.BlockSpec((None, pitch, SSM_WIDTH), lambda b, j: (b, 0, 0))],
        out_shape=[
            jax.ShapeDtypeStruct((t, ATTN_WIDTH), BF16),
            jax.ShapeDtypeStruct((t, KV_WIDTH), BF16),
            jax.ShapeDtypeStruct((t, KV_WIDTH), BF16),
            jax.ShapeDtypeStruct((batch, pitch, SSM_WIDTH), F32),
        ],
        compiler_params=pltpu.CompilerParams(
            dimension_semantics=("parallel", "arbitrary"), vmem_limit_bytes=VMEM_LIMIT_BYTES),
    )(x, gain.reshape(1, D_MODEL), w_in.astype(BF16))


def _attn_kernel(sink_ref, q_ref, kp_ref, kc_ref, kn_ref, vp_ref, vc_ref, vn_ref, o_ref, bias_ref):
    n = pl.program_id(1)
    last = pl.num_programs(1) - 1

    @pl.when(n == 0)
    def _():
        qi = lax.broadcasted_iota(jnp.int32, (BLOCK, 3 * BLOCK), 0)
        kj = lax.broadcasted_iota(jnp.int32, (BLOCK, 3 * BLOCK), 1)
        rel = jnp.abs(kj - BLOCK - qi)
        dist = rel.astype(F32)
        inside = rel <= WINDOW
        has_prev = kj >= BLOCK
        has_next = kj < 2 * BLOCK
        for variant, ok in enumerate((inside & has_prev, inside, inside & has_next)):
            for h in range(ATTN_HEADS):
                slope = float(2.0 ** (-8.0 * (h + 1) / ATTN_HEADS))
                bias_ref[variant, h] = jnp.where(ok, -slope * dist, NEG_INF)

    variant = jnp.where(n == 0, 0, jnp.where(n == last, 2, 1))
    outs = []
    for kh in range(ATTN_KV_HEADS):
        cols = slice(kh * HEAD_DIM, (kh + 1) * HEAD_DIM)
        kcat = jnp.concatenate([kp_ref[:, cols], kc_ref[:, cols], kn_ref[:, cols]], axis=0)
        vcat = jnp.concatenate([vp_ref[:, cols], vc_ref[:, cols], vn_ref[:, cols]], axis=0)
        heads = [kh * Q_PER_KV + g for g in range(Q_PER_KV)]
        qs = jnp.concatenate([q_ref[:, h * HEAD_DIM:(h + 1) * HEAD_DIM] for h in heads], axis=0)
        s_all = lax.dot_general(qs * (HEAD_DIM ** -0.5), kcat, NT_DIMS, preferred_element_type=F32)
        for g, h in enumerate(heads):
            s = s_all[g * BLOCK:(g + 1) * BLOCK] + bias_ref[variant, h]
            sink = sink_ref[h]
            m = jnp.maximum(jnp.max(s, axis=-1, keepdims=True), sink)
            e = jnp.exp(s - m)
            den = jnp.sum(e, axis=-1, keepdims=True) + jnp.exp(sink - m)
            pv = jnp.dot(e.astype(BF16), vcat, preferred_element_type=F32)
            outs.append(pv / den)
    o_ref[...] = jnp.concatenate(outs, axis=-1).astype(o_ref.dtype)


def _attention(q, k, v, sinks):
    b, seq, _ = q.shape
    nb = seq // BLOCK
    kv_spec = lambda f: pl.BlockSpec((None, BLOCK, KV_WIDTH), f)
    prev = lambda bi, n: (bi, jnp.maximum(n - 1, 0), 0)
    cur = lambda bi, n: (bi, n, 0)
    nxt = lambda bi, n: (bi, jnp.minimum(n + 1, nb - 1), 0)
    assert nb >= 2 and HEAD_DIM ** -0.5 == 2.0 ** -3
    return pl.pallas_call(
        _attn_kernel,
        grid=(b, nb),
        in_specs=[
            pl.BlockSpec(memory_space=pltpu.SMEM),
            pl.BlockSpec((None, BLOCK, ATTN_WIDTH), cur),
            kv_spec(prev), kv_spec(cur), kv_spec(nxt),
            kv_spec(prev), kv_spec(cur), kv_spec(nxt),
        ],
        out_specs=pl.BlockSpec((None, BLOCK, ATTN_WIDTH), cur),
        out_shape=jax.ShapeDtypeStruct((b, seq, ATTN_WIDTH), BF16),
        scratch_shapes=[pltpu.VMEM((3, ATTN_HEADS, BLOCK, 3 * BLOCK), F32)],
        compiler_params=pltpu.CompilerParams(
            dimension_semantics=("parallel", "arbitrary"), vmem_limit_bytes=VMEM_LIMIT_BYTES),
    )(sinks.astype(F32), q, k, k, k, v, v, v)


def _ssm_ops_kernel(prm_ref, bc_ref, ein_ref, toep_ref, eout_ref, aq_ref):
    q, hc, p = SSM_CHUNK, SSM_CH, SSM_STATE
    hi = lax.Precision.HIGHEST
    fwd = lax.broadcasted_iota(jnp.int32, (1, 2 * p), 1) < p
    zero_row = jnp.zeros((1, 2 * p), F32)
    row_id = lax.broadcasted_iota(jnp.int32, (CHUNK_WIDTH, CHUNK_WIDTH), 0)
    col_id = lax.broadcasted_iota(jnp.int32, (CHUNK_WIDTH, CHUNK_WIDTH), 1)

    def table(select, n):
        picks = [select(m) for m in range(n)]
        re = jnp.concatenate([jnp.broadcast_to(r, (hc, 2 * p)) for r, _ in picks], axis=0)
        im = jnp.concatenate([jnp.broadcast_to(i, (hc, 2 * p)) for _, i in picks], axis=0)
        return re, im

    def tile_rows(x, n):
        return jnp.concatenate([x] * n, axis=0)

    for gi in range(prm_ref.shape[0]):
        lr = jnp.minimum(prm_ref[gi, 0:1, :], LAMBDA_RE_MAX)
        li = prm_ref[gi, 1:2, :]
        dt = jnp.exp(prm_ref[gi, 2:3, :])
        mag = jnp.exp(lr * dt)
        a_r = mag * jnp.cos(li * dt)
        a_i = mag * jnp.sin(li * dt)
        den = lr * lr + li * li
        coef_r = ((a_r - 1.0) * lr + a_i * li) / den
        coef_i = (a_i * lr - (a_r - 1.0) * li) / den
        b_r, b_i = bc_ref[gi, 0], bc_ref[gi, 1]
        c_r, c_i = bc_ref[gi, 2], bc_ref[gi, 3]
        bb_r = coef_r * b_r - coef_i * b_i
        bb_i = coef_r * b_i + coef_i * b_r

        pw = [(jnp.ones((1, 2 * p), F32), zero_row)]
        for _ in range(q):
            r, i = pw[-1]
            pw.append((r * a_r - i * a_i, r * a_i + i * a_r))

        def both(f_idx, b_idx):
            fr, fi = pw[f_idx] if f_idx is not None else (zero_row, zero_row)
            br, bi = pw[b_idx] if b_idx is not None else (zero_row, zero_row)
            return jnp.where(fwd, fr, br), jnp.where(fwd, fi, bi)

        p_r, p_i = table(lambda i: both(q - 1 - i, i), q)
        tb_r, tb_i = tile_rows(bb_r, q), tile_rows(bb_i, q)
        ein = jnp.concatenate([tb_r * p_r - tb_i * p_i, tb_r * p_i + tb_i * p_r], axis=1)
        ein_ref[gi] = ein.astype(BF16)

        p_r, p_i = table(lambda j: both(j + 1, q - j), q)
        tc_r, tc_i = tile_rows(c_r, q), tile_rows(c_i, q)
        eout = jnp.concatenate([tc_r * p_r - tc_i * p_i, -(tc_r * p_i + tc_i * p_r)], axis=1)
        eout_ref[gi] = eout.astype(BF16)

        def lag(m):
            return both(m - (q - 1) if q - 1 <= m <= 2 * q - 2 else None,
                        (q - 1) - m if m <= q - 1 else None)

        p_r, p_i = table(lag, 2 * q)
        tc_r, tc_i = tile_rows(c_r, 2 * q), tile_rows(c_i, 2 * q)
        cpt = jnp.concatenate([tc_r * p_r - tc_i * p_i, tc_r * p_i + tc_i * p_r], axis=1)
        bcat = jnp.concatenate([bb_r, -bb_i], axis=1)
        kern = lax.dot_general(bcat, cpt, NT_DIMS, precision=hi, preferred_element_type=F32)
        toep = jnp.concatenate(
            [kern[:, hc * (q - 1 - i):hc * (q - 1 - i) + CHUNK_WIDTH] for i in range(q)], axis=0)
        skip = jnp.concatenate([prm_ref[gi, 3:4, :]] * (CHUNK_WIDTH // (2 * p)), axis=1)
        toep_ref[gi] = (toep + jnp.where(row_id == col_id, skip, 0.0)).astype(BF16)

        aq_ref[gi] = jnp.concatenate(
            [pw[q][0], pw[q][1], jnp.zeros((SUBLANES - 2, 2 * p), F32)], axis=0)


def _ssm_operators(lam_re, lam_im, log_dt, b_re, b_im, c_re, c_im, d_skip):
    g, p, hc = SSM_GROUPS, SSM_STATE, SSM_CH
    lanes = lambda a: a.astype(F32).transpose(1, 0, 2).reshape(g, 2 * p)
    prm = jnp.stack(
        [lanes(lam_re), lanes(lam_im),
         jnp.repeat(log_dt.astype(F32).T, p, axis=1),
         jnp.tile(d_skip.astype(F32), (1, 2 * p // hc))]
        + [jnp.zeros((g, 2 * p), F32)] * (SUBLANES - 4), axis=1)
    bc = jnp.stack(
        [b_re.astype(F32).transpose(1, 3, 0, 2).reshape(g, hc, 2 * p),
         b_im.astype(F32).transpose(1, 3, 0, 2).reshape(g, hc, 2 * p),
         c_re.astype(F32).transpose(1, 2, 0, 3).reshape(g, hc, 2 * p),
         c_im.astype(F32).transpose(1, 2, 0, 3).reshape(g, hc, 2 * p)], axis=1)
    gs = OPS_GROUPS_PER_STEP
    mat = pl.BlockSpec((gs, CHUNK_WIDTH, CHUNK_WIDTH), lambda s: (s, 0, 0))
    mat_shape = jax.ShapeDtypeStruct((g, CHUNK_WIDTH, CHUNK_WIDTH), BF16)
    return pl.pallas_call(
        _ssm_ops_kernel,
        grid=(g // gs,),
        in_specs=[pl.BlockSpec((gs, SUBLANES, 2 * p), lambda s: (s, 0, 0)),
                  pl.BlockSpec((gs, 4, hc, 2 * p), lambda s: (s, 0, 0, 0))],
        out_specs=[mat, mat, mat, pl.BlockSpec((gs, SUBLANES, 2 * p), lambda s: (s, 0, 0))],
        out_shape=[mat_shape, mat_shape, mat_shape,
                   jax.ShapeDtypeStruct((g, SUBLANES, 2 * p), F32)],
        compiler_params=pltpu.CompilerParams(
            dimension_semantics=("parallel",), vmem_limit_bytes=VMEM_LIMIT_BYTES),
    )(prm, bc)


def _lane_roll(x, shift):
    shift %= LANES
    return jnp.concatenate([x[:, LANES - shift:], x[:, :LANES - shift]], axis=1)


def _block_transpose(v, lane_block):
    v = list(v)
    n = len(v)
    d = n // 2
    while d >= 1:
        upper = (lane_block & d) != 0
        for i in range(n):
            if i & d == 0:
                a, b = v[i], v[i + d]
                v[i] = jnp.where(upper, _lane_roll(b, SSM_CH * d), a)
                v[i + d] = jnp.where(upper, b, _lane_roll(a, -SSM_CH * d))
        d //= 2
    return v


def _ssm_kernel(u_ref, ein_ref, toep_ref, eout_ref, aq_ref, y_ref, ug_ref, s_ref, x_ref,
                *, batch, seq, pitch):
    q, half = SSM_CHUNK, SSM_STATE
    n_chunks = seq // q
    gpb = GROUPS_PER_BLOCK
    slab = 2 * batch
    lane_block = lax.broadcasted_iota(jnp.int32, (slab, LANES), 1) // SSM_CH

    for b in range(batch if pitch > seq else 0):
        y_ref[b * pitch + seq:(b + 1) * pitch, :] = jnp.zeros((pitch - seq, LANES), F32)

    def gather(cp, carry):
        rows = pl.ds(pl.multiple_of(cp * slab, slab), slab)
        for part in range(q // gpb):
            t0 = cp * 2 * q + part * gpb
            v = [jnp.concatenate([u_ref[pl.ds(t0 + i, batch, stride=pitch), :],
                                  u_ref[pl.ds(t0 + q + i, batch, stride=pitch), :]],
                                 axis=0).astype(BF16) for i in range(gpb)]
            w = _block_transpose(v, lane_block)
            for g in range(gpb):
                ug_ref[g, rows, part * LANES:(part + 1) * LANES] = w[g]
        return carry

    lax.fori_loop(0, n_chunks // 2, gather, 0, unroll=SSM_RELAYOUT_UNROLL)

    fwd_lane = lax.broadcasted_iota(jnp.int32, (batch, 2 * half), 1) < half
    zeros = jnp.zeros((batch, half), F32)
    last_rows = pl.ds((n_chunks - 1) * batch, batch)
    ni = SSM_INTERLEAVE
    for g0 in range(0, gpb, ni):
        for gi in range(ni):
            s_ref[gi] = jnp.dot(ug_ref[g0 + gi], ein_ref[g0 + gi], preferred_element_type=F32)
            x_ref[gi, 0:batch, 0:half] = zeros
            x_ref[gi, 0:batch, 2 * half:3 * half] = zeros
            x_ref[gi, last_rows, half:2 * half] = zeros
            x_ref[gi, last_rows, 3 * half:4 * half] = zeros

        def step(k, carry):
            rf = pl.multiple_of(k * batch, batch)
            rb = pl.multiple_of((n_chunks - 1 - k) * batch, batch)
            new = []
            for gi in range(ni):
                xr, xi = carry[gi]
                sre = jnp.where(fwd_lane, s_ref[gi, pl.ds(rf, batch), 0:2 * half],
                                s_ref[gi, pl.ds(rb, batch), 0:2 * half])
                sim = jnp.where(fwd_lane, s_ref[gi, pl.ds(rf, batch), 2 * half:4 * half],
                                s_ref[gi, pl.ds(rb, batch), 2 * half:4 * half])
                ar = aq_ref[g0 + gi, 0:1, :]
                ai = aq_ref[g0 + gi, 1:2, :]
                nr = ar * xr - ai * xi + sre
                nim = ar * xi + ai * xr + sim
                x_ref[gi, pl.ds(rf + batch, batch), 0:half] = nr[:, 0:half]
                x_ref[gi, pl.ds(rf + batch, batch), 2 * half:3 * half] = nim[:, 0:half]
                x_ref[gi, pl.ds(rb - batch, batch), half:2 * half] = nr[:, half:2 * half]
                x_ref[gi, pl.ds(rb - batch, batch), 3 * half:4 * half] = nim[:, half:2 * half]
                new.append((nr, nim))
            return tuple(new)

        init = tuple((jnp.zeros((batch, 2 * half), F32), jnp.zeros((batch, 2 * half), F32))
                     for _ in range(ni))
        lax.fori_loop(0, n_chunks - 1, step, init)

        for gi in range(ni):
            g = g0 + gi
            y = jnp.dot(ug_ref[g], toep_ref[g], preferred_element_type=F32)
            y += lax.dot_general(x_ref[gi].astype(BF16), eout_ref[g], NT_DIMS,
                                 preferred_element_type=F32)
            ug_ref[g] = y.astype(BF16)

    def scatter(cp, carry):
        rows = pl.ds(pl.multiple_of(cp * slab, slab), slab)
        for part in range(q // gpb):
            t0 = cp * 2 * q + part * gpb
            v = [ug_ref[g, rows, part * LANES:(part + 1) * LANES] for g in range(gpb)]
            w = _block_transpose(v, lane_block)
            for j in range(gpb):
                wj = w[j].astype(F32)
                y_ref[pl.ds(t0 + j, batch, stride=pitch), :] = wj[0:batch]
                y_ref[pl.ds(t0 + q + j, batch, stride=pitch), :] = wj[batch:slab]
        return carry

    lax.fori_loop(0, n_chunks // 2, scatter, 0, unroll=SSM_RELAYOUT_UNROLL)


def _ssm(u, seq, lam_re, lam_im, log_dt, b_re, b_im, c_re, c_im, d_skip):
    batch, pitch, _ = u.shape
    rows = batch * seq // SSM_CHUNK
    ein, toep, eout, a_q = _ssm_operators(lam_re, lam_im, log_dt, b_re, b_im, c_re, c_im, d_skip)
    gpb = GROUPS_PER_BLOCK
    mat = pl.BlockSpec((gpb, CHUNK_WIDTH, CHUNK_WIDTH), lambda s: (s, 0, 0))
    col = pl.BlockSpec((batch * pitch, LANES), lambda s: (0, s), pipeline_mode=pl.Buffered(1))
    y = pl.pallas_call(
        functools.partial(_ssm_kernel, batch=batch, seq=seq, pitch=pitch),
        grid=(SSM_GROUPS // gpb,),
        in_specs=[col, mat, mat, mat,
                  pl.BlockSpec((gpb, SUBLANES, 2 * SSM_STATE), lambda s: (s, 0, 0))],
        out_specs=col,
        out_shape=jax.ShapeDtypeStruct((batch * pitch, SSM_WIDTH), F32),
        scratch_shapes=[pltpu.VMEM((gpb, rows, CHUNK_WIDTH), BF16),
                        pltpu.VMEM((SSM_INTERLEAVE, rows, 4 * SSM_STATE), F32),
                        pltpu.VMEM((SSM_INTERLEAVE, rows, 4 * SSM_STATE), F32)],
        compiler_params=pltpu.CompilerParams(
            dimension_semantics=("parallel",), vmem_limit_bytes=VMEM_LIMIT_BYTES),
    )(u.reshape(batch * pitch, SSM_WIDTH), ein, toep, eout, a_q)
    return y.reshape(batch, pitch, SSM_WIDTH)


def _mix_kernel(x_ref, attn_ref, ssm_ref, gw_ref, gb_ref, ga_ref, gs_ref, wo_ref, o_ref):
    y = jax.nn.gelu(ssm_ref[...])
    z = jnp.dot(y.astype(BF16), gw_ref[...], preferred_element_type=F32) + gb_ref[...]
    s = y * jax.nn.sigmoid(z)
    sn = _rms(s) * gs_ref[...]
    an = _rms(attn_ref[...].astype(F32)) * ga_ref[...]
    mixed = jnp.concatenate([an, sn], axis=-1).astype(BF16)
    o_ref[...] = x_ref[...] + jnp.dot(mixed, wo_ref[...], preferred_element_type=F32)


def _mix_out(x, attn, ssm_pre, seq, glu_w, glu_b, attn_gain, ssm_gain, w_out):
    t = x.shape[0]
    row = lambda width: pl.BlockSpec((TOKEN_TILE, width), lambda i: (i, 0))
    full = lambda r, c: pl.BlockSpec((r, c), lambda i: (0, 0))
    return pl.pallas_call(
        _mix_kernel,
        grid=(t // TOKEN_TILE,),
        in_specs=[row(D_MODEL), row(ATTN_WIDTH), _batch_rows(seq, SSM_WIDTH),
                  full(SSM_WIDTH, SSM_WIDTH), full(1, SSM_WIDTH),
                  full(1, ATTN_WIDTH), full(1, SSM_WIDTH), full(D_MODEL, D_MODEL)],
        out_specs=row(D_MODEL),
        out_shape=jax.ShapeDtypeStruct((t, D_MODEL), F32),
        compiler_params=pltpu.CompilerParams(
            dimension_semantics=("parallel",), vmem_limit_bytes=VMEM_LIMIT_BYTES),
    )(x, attn, ssm_pre, glu_w.astype(BF16), glu_b.reshape(1, -1).astype(F32),
      attn_gain.reshape(1, -1).astype(F32), ssm_gain.reshape(1, -1).astype(F32),
      w_out.astype(BF16))


def kernel(x, norm_ffn1, ffn1_w_gate, ffn1_w_up, ffn1_w_down, norm_mix, w_in, attn_sinks,
           ssm_lambda_re, ssm_lambda_im, ssm_log_dt, ssm_b_re, ssm_b_im, ssm_c_re, ssm_c_im,
           ssm_d, ssm_glu_w, ssm_glu_b, attn_out_norm, ssm_out_norm, w_out,
           norm_ffn2, ffn2_w_gate, ffn2_w_up, ffn2_w_down, final_norm):
    b, seq, d = x.shape
    depth = norm_ffn1.shape[0]
    assert d == D_MODEL and seq % BLOCK == 0 and seq % TOKEN_TILE == 0
    assert b == SUBLANES and seq % (2 * SSM_CHUNK * SSM_RELAYOUT_UNROLL) == 0
    h = x.reshape(b * seq, d).astype(F32)
    for l in range(depth):
        h = _ffn(h, norm_ffn1[l], ffn1_w_gate[l], ffn1_w_up[l], ffn1_w_down[l],
                 final_norm, final_norm=False)
        q, k, v, u = _in_proj(h, norm_mix[l], w_in[l], b, seq)
        attn = _attention(q.reshape(b, seq, -1), k.reshape(b, seq, -1), v.reshape(b, seq, -1),
                          attn_sinks[l])
        ssm_pre = _ssm(u, seq, ssm_lambda_re[l], ssm_lambda_im[l], ssm_log_dt[l],
                       ssm_b_re[l], ssm_b_im[l], ssm_c_re[l], ssm_c_im[l], ssm_d[l])
        h = _mix_out(h, attn.reshape(b * seq, -1), ssm_pre, seq, ssm_glu_w[l], ssm_glu_b[l],
                     attn_out_norm[l], ssm_out_norm[l], w_out[l])
        h = _ffn(h, norm_ffn2[l], ffn2_w_gate[l], ffn2_w_up[l], ffn2_w_down[l],
                 final_norm, final_norm=(l == depth - 1))
    return h.reshape(b, seq, d).astype(x.dtype)
```

```python
import functools

import jax
import jax.numpy as jnp
from jax import lax
from jax.experimental import pallas as pl
from jax.experimental.pallas import tpu as pltpu

F32 = jnp.float32
BF16 = jnp.bfloat16

D_MODEL = 1024
ATTN_HEADS = 8
ATTN_KV_HEADS = 2
Q_PER_KV = ATTN_HEADS // ATTN_KV_HEADS
HEAD_DIM = 64
ATTN_WIDTH = ATTN_HEADS * HEAD_DIM
KV_WIDTH = ATTN_KV_HEADS * HEAD_DIM
WINDOW = 128
BLOCK = 128
SSM_CH = 16
SSM_WIDTH = D_MODEL - ATTN_WIDTH
SSM_GROUPS = SSM_WIDTH // SSM_CH
SSM_STATE = 64
IN_WIDTH = ATTN_WIDTH + 2 * KV_WIDTH + SSM_WIDTH
D_FF = 2816
EPS = 1e-6
NEG_INF = -1e30
LAMBDA_RE_MAX = -1e-4

LANES = 128
SUBLANES = 8
VMEM_LIMIT_BYTES = 56 * 1024 * 1024

TOKEN_TILE = 1024
FFN_TOKEN_TILE = 512
FF_TILE = 256
SSM_CHUNK = 16
CHUNK_WIDTH = SSM_CHUNK * SSM_CH
GROUPS_PER_BLOCK = LANES // SSM_CH
SSM_INTERLEAVE = 4
SSM_RELAYOUT_UNROLL = 4
OPS_GROUPS_PER_STEP = 4

NT_DIMS = (((1,), (1,)), ((), ()))


def _rms(x):
    return x * lax.rsqrt(jnp.mean(x * x, axis=-1, keepdims=True) + EPS)


def _ffn_kernel(x_ref, gain_ref, wg_ref, wu_ref, wd_ref, fgain_ref, o_ref, act_ref, *, final_norm):
    x = x_ref[...]
    hn = (_rms(x) * gain_ref[...]).astype(BF16)
    for j in range(D_FF // FF_TILE):
        cols = slice(j * FF_TILE, (j + 1) * FF_TILE)
        g = jnp.dot(hn, wg_ref[:, cols].astype(BF16), preferred_element_type=F32)
        u = jnp.dot(hn, wu_ref[:, cols].astype(BF16), preferred_element_type=F32)
        act_ref[:, cols] = (g * jax.nn.sigmoid(g) * u).astype(BF16)
    y = x + 0.5 * jnp.dot(act_ref[...], wd_ref[...].astype(BF16), preferred_element_type=F32)
    if final_norm:
        y = _rms(y) * fgain_ref[...]
    o_ref[...] = y


def _ffn(x, gain, w_gate, w_up, w_down, final_gain, final_norm):
    t = x.shape[0]
    resident = lambda r, c: pl.BlockSpec((r, c), lambda i: (0, 0), pipeline_mode=pl.Buffered(1))
    return pl.pallas_call(
        functools.partial(_ffn_kernel, final_norm=final_norm),
        grid=(t // FFN_TOKEN_TILE,),
        in_specs=[
            pl.BlockSpec((FFN_TOKEN_TILE, D_MODEL), lambda i: (i, 0)),
            resident(1, D_MODEL),
            resident(D_MODEL, D_FF),
            resident(D_MODEL, D_FF),
            resident(D_FF, D_MODEL),
            resident(1, D_MODEL),
        ],
        out_specs=pl.BlockSpec((FFN_TOKEN_TILE, D_MODEL), lambda i: (i, 0)),
        out_shape=jax.ShapeDtypeStruct((t, D_MODEL), F32),
        scratch_shapes=[pltpu.VMEM((FFN_TOKEN_TILE, D_FF), BF16)],
        compiler_params=pltpu.CompilerParams(
            dimension_semantics=("parallel",), vmem_limit_bytes=VMEM_LIMIT_BYTES),
    )(x, gain.reshape(1, D_MODEL), w_gate, w_up, w_down, final_gain.reshape(1, D_MODEL))


def _in_proj_kernel(x_ref, gain_ref, w_ref, q_ref, k_ref, v_ref, u_ref, *, seq):
    j = pl.program_id(1)
    hn = (_rms(x_ref[...]) * gain_ref[...]).astype(BF16)
    proj = jnp.dot(hn, w_ref[...].astype(BF16), preferred_element_type=F32)
    q_ref[...] = proj[:, :ATTN_WIDTH].astype(BF16)
    k_ref[...] = proj[:, ATTN_WIDTH:ATTN_WIDTH + KV_WIDTH].astype(BF16)
    v_ref[...] = proj[:, ATTN_WIDTH + KV_WIDTH:ATTN_WIDTH + 2 * KV_WIDTH].astype(BF16)
    rows = pl.ds(pl.multiple_of(j * TOKEN_TILE, TOKEN_TILE), TOKEN_TILE)
    u_ref[rows, :] = proj[:, ATTN_WIDTH + 2 * KV_WIDTH:]
    pitch = u_ref.shape[0]
    if pitch > seq:
        @pl.when(j == 0)
        def _():
            u_ref[seq:pitch, :] = jnp.zeros((pitch - seq, SSM_WIDTH), F32)


def _padded_seq(seq):
    return seq + SUBLANES if seq % (2 * SUBLANES) == 0 else seq


def _batch_rows(seq, width):
    tiles = seq // TOKEN_TILE
    return pl.BlockSpec((None, TOKEN_TILE, width), lambda i: (i // tiles, i % tiles, 0))


def _in_proj(x, gain, w_in, batch, seq):
    t = x.shape[0]
    tiles = seq // TOKEN_TILE
    pitch = _padded_seq(seq)
    row = lambda width: pl.BlockSpec((TOKEN_TILE, width), lambda b, j: (b * tiles + j, 0))
    return pl.pallas_call(
        functools.partial(_in_proj_kernel, seq=seq),
        grid=(batch, tiles),
        in_specs=[
            row(D_MODEL),
            pl.BlockSpec((1, D_MODEL), lambda b, j: (0, 0)),
            pl.BlockSpec((D_MODEL, IN_WIDTH), lambda b, j: (0, 0)),
        ],
        out_specs=[row(ATTN_WIDTH), row(KV_WIDTH), row(KV_WIDTH),
                   pl.BlockSpec((None, pitch, SSM_WIDTH), lambda b, j: (b, 0, 0))],
        out_shape=[
            jax.ShapeDtypeStruct((t, ATTN_WIDTH), BF16),
            jax.ShapeDtypeStruct((t, KV_WIDTH), BF16),
            jax.ShapeDtypeStruct((t, KV_WIDTH), BF16),
            jax.ShapeDtypeStruct((batch, pitch, SSM_WIDTH), F32),
        ],
        compiler_params=pltpu.CompilerParams(
            dimension_semantics=("parallel", "arbitrary"), vmem_limit_bytes=VMEM_LIMIT_BYTES),
    )(x, gain.reshape(1, D_MODEL), w_in)


def _attn_kernel(sink_ref, q_ref, kp_ref, kc_ref, kn_ref, vp_ref, vc_ref, vn_ref, o_ref, bias_ref):
    n = pl.program_id(1)
    last = pl.num_programs(1) - 1

    @pl.when(n == 0)
    def _():
        qi = lax.broadcasted_iota(jnp.int32, (BLOCK, 3 * BLOCK), 0)
        kj = lax.broadcasted_iota(jnp.int32, (BLOCK, 3 * BLOCK), 1)
        rel = jnp.abs(kj - BLOCK - qi)
        dist = rel.astype(F32)
        inside = rel <= WINDOW
        has_prev = kj >= BLOCK
        has_next = kj < 2 * BLOCK
        for variant, ok in enumerate((inside & has_prev, inside, inside & has_next)):
            for h in range(ATTN_HEADS):
                slope = float(2.0 ** (-8.0 * (h + 1) / ATTN_HEADS))
                bias_ref[variant, h] = jnp.where(ok, -slope * dist, NEG_INF)

    variant = jnp.where(n == 0, 0, jnp.where(n == last, 2, 1))
    outs = []
    for kh in range(ATTN_KV_HEADS):
        cols = slice(kh * HEAD_DIM, (kh + 1) * HEAD_DIM)
        kcat = jnp.concatenate([kp_ref[:, cols], kc_ref[:, cols], kn_ref[:, cols]], axis=0)
        vcat = jnp.concatenate([vp_ref[:, cols], vc_ref[:, cols], vn_ref[:, cols]], axis=0)
        heads = [kh * Q_PER_KV + g for g in range(Q_PER_KV)]
        qs = jnp.concatenate([q_ref[:, h * HEAD_DIM:(h + 1) * HEAD_DIM] for h in heads], axis=0)
        s_all = lax.dot_general(qs * (HEAD_DIM ** -0.5), kcat, NT_DIMS, preferred_element_type=F32)
        for g, h in enumerate(heads):
            s = s_all[g * BLOCK:(g + 1) * BLOCK] + bias_ref[variant, h]
            sink = sink_ref[h]
            m = jnp.maximum(jnp.max(s, axis=-1, keepdims=True), sink)
            e = jnp.exp(s - m)
            den = jnp.sum(e, axis=-1, keepdims=True) + jnp.exp(sink - m)
            pv = jnp.dot(e.astype(BF16), vcat, preferred_element_type=F32)
            outs.append(pv / den)
    o_ref[...] = jnp.concatenate(outs, axis=-1).astype(o_ref.dtype)


def _attention(q, k, v, sinks):
    b, seq, _ = q.shape
    nb = seq // BLOCK
    kv_spec = lambda f: pl.BlockSpec((None, BLOCK, KV_WIDTH), f)
    prev = lambda bi, n: (bi, jnp.maximum(n - 1, 0), 0)
    cur = lambda bi, n: (bi, n, 0)
    nxt = lambda bi, n: (bi, jnp.minimum(n + 1, nb - 1), 0)
    assert nb >= 2 and HEAD_DIM ** -0.5 == 2.0 ** -3
    return pl.pallas_call(
        _attn_kernel,
        grid=(b, nb),
        in_specs=[
            pl.BlockSpec(memory_space=pltpu.SMEM),
            pl.BlockSpec((None, BLOCK, ATTN_WIDTH), cur),
            kv_spec(prev), kv_spec(cur), kv_spec(nxt),
            kv_spec(prev), kv_spec(cur), kv_spec(nxt),
        ],
        out_specs=pl.BlockSpec((None, BLOCK, ATTN_WIDTH), cur),
        out_shape=jax.ShapeDtypeStruct((b, seq, ATTN_WIDTH), BF16),
        scratch_shapes=[pltpu.VMEM((3, ATTN_HEADS, BLOCK, 3 * BLOCK), F32)],
        compiler_params=pltpu.CompilerParams(
            dimension_semantics=("parallel", "arbitrary"), vmem_limit_bytes=VMEM_LIMIT_BYTES),
    )(sinks.astype(F32), q, k, k, k, v, v, v)


def _ssm_ops_kernel(prm_ref, bc_ref, ein_ref, toep_ref, eout_ref, aq_ref):
    q, hc, p = SSM_CHUNK, SSM_CH, SSM_STATE
    hi = lax.Precision.HIGHEST
    fwd = lax.broadcasted_iota(jnp.int32, (1, 2 * p), 1) < p
    zero_row = jnp.zeros((1, 2 * p), F32)
    row_id = lax.broadcasted_iota(jnp.int32, (CHUNK_WIDTH, CHUNK_WIDTH), 0)
    col_id = lax.broadcasted_iota(jnp.int32, (CHUNK_WIDTH, CHUNK_WIDTH), 1)

    def table(select, n):
        picks = [select(m) for m in range(n)]
        re = jnp.concatenate([jnp.broadcast_to(r, (hc, 2 * p)) for r, _ in picks], axis=0)
        im = jnp.concatenate([jnp.broadcast_to(i, (hc, 2 * p)) for _, i in picks], axis=0)
        return re, im

    def tile_rows(x, n):
        return jnp.concatenate([x] * n, axis=0)

    for gi in range(prm_ref.shape[0]):
        lr = jnp.minimum(prm_ref[gi, 0:1, :], LAMBDA_RE_MAX)
        li = prm_ref[gi, 1:2, :]
        dt = jnp.exp(prm_ref[gi, 2:3, :])
        mag = jnp.exp(lr * dt)
        a_r = mag * jnp.cos(li * dt)
        a_i = mag * jnp.sin(li * dt)
        den = lr * lr + li * li
        coef_r = ((a_r - 1.0) * lr + a_i * li) / den
        coef_i = (a_i * lr - (a_r - 1.0) * li) / den
        b_r, b_i = bc_ref[gi, 0], bc_ref[gi, 1]
        c_r, c_i = bc_ref[gi, 2], bc_ref[gi, 3]
        bb_r = coef_r * b_r - coef_i * b_i
        bb_i = coef_r * b_i + coef_i * b_r

        pw = [(jnp.ones((1, 2 * p), F32), zero_row)]
        for _ in range(q):
            r, i = pw[-1]
            pw.append((r * a_r - i * a_i, r * a_i + i * a_r))

        def both(f_idx, b_idx):
            fr, fi = pw[f_idx] if f_idx is not None else (zero_row, zero_row)
            br, bi = pw[b_idx] if b_idx is not None else (zero_row, zero_row)
            return jnp.where(fwd, fr, br), jnp.where(fwd, fi, bi)

        p_r, p_i = table(lambda i: both(q - 1 - i, i), q)
        tb_r, tb_i = tile_rows(bb_r, q), tile_rows(bb_i, q)
        ein = jnp.concatenate([tb_r * p_r - tb_i * p_i, tb_r * p_i + tb_i * p_r], axis=1)
        ein_ref[gi] = ein.astype(BF16)

        p_r, p_i = table(lambda j: both(j + 1, q - j), q)
        tc_r, tc_i = tile_rows(c_r, q), tile_rows(c_i, q)
        eout = jnp.concatenate([tc_r * p_r - tc_i * p_i, -(tc_r * p_i + tc_i * p_r)], axis=1)
        eout_ref[gi] = eout.astype(BF16)

        def lag(m):
            return both(m - (q - 1) if q - 1 <= m <= 2 * q - 2 else None,
                        (q - 1) - m if m <= q - 1 else None)

        p_r, p_i = table(lag, 2 * q)
        tc_r, tc_i = tile_rows(c_r, 2 * q), tile_rows(c_i, 2 * q)
        cpt = jnp.concatenate([tc_r * p_r - tc_i * p_i, tc_r * p_i + tc_i * p_r], axis=1)
        bcat = jnp.concatenate([bb_r, -bb_i], axis=1)
        kern = lax.dot_general(bcat, cpt, NT_DIMS, precision=hi, preferred_element_type=F32)
        toep = jnp.concatenate(
            [kern[:, hc * (q - 1 - i):hc * (q - 1 - i) + CHUNK_WIDTH] for i in range(q)], axis=0)
        skip = jnp.concatenate([prm_ref[gi, 3:4, :]] * (CHUNK_WIDTH // (2 * p)), axis=1)
        toep_ref[gi] = (toep + jnp.where(row_id == col_id, skip, 0.0)).astype(BF16)

        aq_ref[gi] = jnp.concatenate(
            [pw[q][0], pw[q][1], jnp.zeros((SUBLANES - 2, 2 * p), F32)], axis=0)


def _ssm_operators(lam_re, lam_im, log_dt, b_re, b_im, c_re, c_im, d_skip):
    g, p, hc = SSM_GROUPS, SSM_STATE, SSM_CH
    lanes = lambda a: a.astype(F32).transpose(1, 0, 2).reshape(g, 2 * p)
    prm = jnp.stack(
        [lanes(lam_re), lanes(lam_im),
         jnp.repeat(log_dt.astype(F32).T, p, axis=1),
         jnp.tile(d_skip.astype(F32), (1, 2 * p // hc))]
        + [jnp.zeros((g, 2 * p), F32)] * (SUBLANES - 4), axis=1)
    bc = jnp.stack(
        [b_re.astype(F32).transpose(1, 3, 0, 2).reshape(g, hc, 2 * p),
         b_im.astype(F32).transpose(1, 3, 0, 2).reshape(g, hc, 2 * p),
         c_re.astype(F32).transpose(1, 2, 0, 3).reshape(g, hc, 2 * p),
         c_im.astype(F32).transpose(1, 2, 0, 3).reshape(g, hc, 2 * p)], axis=1)
    gs = OPS_GROUPS_PER_STEP
    mat = pl.BlockSpec((gs, CHUNK_WIDTH, CHUNK_WIDTH), lambda s: (s, 0, 0))
    mat_shape = jax.ShapeDtypeStruct((g, CHUNK_WIDTH, CHUNK_WIDTH), BF16)
    return pl.pallas_call(
        _ssm_ops_kernel,
        grid=(g // gs,),
        in_specs=[pl.BlockSpec((gs, SUBLANES, 2 * p), lambda s: (s, 0, 0)),
                  pl.BlockSpec((gs, 4, hc, 2 * p), lambda s: (s, 0, 0, 0))],
        out_specs=[mat, mat, mat, pl.BlockSpec((gs, SUBLANES, 2 * p), lambda s: (s, 0, 0))],
        out_shape=[mat_shape, mat_shape, mat_shape,
                   jax.ShapeDtypeStruct((g, SUBLANES, 2 * p), F32)],
        compiler_params=pltpu.CompilerParams(
            dimension_semantics=("parallel",), vmem_limit_bytes=VMEM_LIMIT_BYTES),
    )(prm, bc)


def _lane_roll(x, shift):
    shift %= LANES
    return jnp.concatenate([x[:, LANES - shift:], x[:, :LANES - shift]], axis=1)


def _block_transpose(v, lane_block):
    v = list(v)
    n = len(v)
    d = n // 2
    while d >= 1:
        upper = (lane_block & d) != 0
        for i in range(n):
            if i & d == 0:
                a, b = v[i], v[i + d]
                v[i] = jnp.where(upper, _lane_roll(b, SSM_CH * d), a)
                v[i + d] = jnp.where(upper, b, _lane_roll(a, -SSM_CH * d))
        d //= 2
    return v


def _ssm_kernel(u_ref, ein_ref, toep_ref, eout_ref, aq_ref, y_ref, ug_ref, s_ref, x_ref,
                *, batch, seq, pitch):
    q, half = SSM_CHUNK, SSM_STATE
    n_chunks = seq // q
    gpb = GROUPS_PER_BLOCK
    slab = 2 * batch
    lane_block = lax.broadcasted_iota(jnp.int32, (slab, LANES), 1) // SSM_CH

    for b in range(batch if pitch > seq else 0):
        y_ref[b * pitch + seq:(b + 1) * pitch, :] = jnp.zeros((pitch - seq, LANES), F32)

    def gather(cp, carry):
        rows = pl.ds(pl.multiple_of(cp * slab, slab), slab)
        for part in range(q // gpb):
            t0 = cp * 2 * q + part * gpb
            v = [jnp.concatenate([u_ref[pl.ds(t0 + i, batch, stride=pitch), :],
                                  u_ref[pl.ds(t0 + q + i, batch, stride=pitch), :]],
                                 axis=0).astype(BF16) for i in range(gpb)]
            w = _block_transpose(v, lane_block)
            for g in range(gpb):
                ug_ref[g, rows, part * LANES:(part + 1) * LANES] = w[g]
        return carry

    lax.fori_loop(0, n_chunks // 2, gather, 0, unroll=SSM_RELAYOUT_UNROLL)

    fwd_lane = lax.broadcasted_iota(jnp.int32, (batch, 2 * half), 1) < half
    zeros = jnp.zeros((batch, half), F32)
    last_rows = pl.ds((n_chunks - 1) * batch, batch)
    ni = SSM_INTERLEAVE
    for g0 in range(0, gpb, ni):
        for gi in range(ni):
            s_ref[gi] = jnp.dot(ug_ref[g0 + gi], ein_ref[g0 + gi], preferred_element_type=F32)
            x_ref[gi, 0:batch, 0:half] = zeros
            x_ref[gi, 0:batch, 2 * half:3 * half] = zeros
            x_ref[gi, last_rows, half:2 * half] = zeros
            x_ref[gi, last_rows, 3 * half:4 * half] = zeros

        def step(k, carry):
            rf = pl.multiple_of(k * batch, batch)
            rb = pl.multiple_of((n_chunks - 1 - k) * batch, batch)
            new = []
            for gi in range(ni):
                xr, xi = carry[gi]
                sre = jnp.where(fwd_lane, s_ref[gi, pl.ds(rf, batch), 0:2 * half],
                                s_ref[gi, pl.ds(rb, batch), 0:2 * half])
                sim = jnp.where(fwd_lane, s_ref[gi, pl.ds(rf, batch), 2 * half:4 * half],
                                s_ref[gi, pl.ds(rb, batch), 2 * half:4 * half])
                ar = aq_ref[g0 + gi, 0:1, :]
                ai = aq_ref[g0 + gi, 1:2, :]
                nr = ar * xr - ai * xi + sre
                nim = ar * xi + ai * xr + sim
                x_ref[gi, pl.ds(rf + batch, batch), 0:half] = nr[:, 0:half]
                x_ref[gi, pl.ds(rf + batch, batch), 2 * half:3 * half] = nim[:, 0:half]
                x_ref[gi, pl.ds(rb - batch, batch), half:2 * half] = nr[:, half:2 * half]
                x_ref[gi, pl.ds(rb - batch, batch), 3 * half:4 * half] = nim[:, half:2 * half]
                new.append((nr, nim))
            return tuple(new)

        init = tuple((jnp.zeros((batch, 2 * half), F32), jnp.zeros((batch, 2 * half), F32))
                     for _ in range(ni))
        lax.fori_loop(0, n_chunks - 1, step, init)

        for gi in range(ni):
            g = g0 + gi
            y = jnp.dot(ug_ref[g], toep_ref[g], preferred_element_type=F32)
            y += lax.dot_general(x_ref[gi].astype(BF16), eout_ref[g], NT_DIMS,
                                 preferred_element_type=F32)
            ug_ref[g] = y.astype(BF16)

    def scatter(cp, carry):
        rows = pl.ds(pl.multiple_of(cp * slab, slab), slab)
        for part in range(q // gpb):
            t0 = cp * 2 * q + part * gpb
            v = [ug_ref[g, rows, part * LANES:(part + 1) * LANES] for g in range(gpb)]
            w = _block_transpose(v, lane_block)
            for j in range(gpb):
                wj = w[j].astype(F32)
                y_ref[pl.ds(t0 + j, batch, stride=pitch), :] = wj[0:batch]
                y_ref[pl.ds(t0 + q + j, batch, stride=pitch), :] = wj[batch:slab]
        return carry

    lax.fori_loop(0, n_chunks // 2, scatter, 0, unroll=SSM_RELAYOUT_UNROLL)


def _ssm(u, seq, lam_re, lam_im, log_dt, b_re, b_im, c_re, c_im, d_skip):
    batch, pitch, _ = u.shape
    rows = batch * seq // SSM_CHUNK
    ein, toep, eout, a_q = _ssm_operators(lam_re, lam_im, log_dt, b_re, b_im, c_re, c_im, d_skip)
    gpb = GROUPS_PER_BLOCK
    mat = pl.BlockSpec((gpb, CHUNK_WIDTH, CHUNK_WIDTH), lambda s: (s, 0, 0))
    col = pl.BlockSpec((batch * pitch, LANES), lambda s: (0, s), pipeline_mode=pl.Buffered(1))
    y = pl.pallas_call(
        functools.partial(_ssm_kernel, batch=batch, seq=seq, pitch=pitch),
        grid=(SSM_GROUPS // gpb,),
        in_specs=[col, mat, mat, mat,
                  pl.BlockSpec((gpb, SUBLANES, 2 * SSM_STATE), lambda s: (s, 0, 0))],
        out_specs=col,
        out_shape=jax.ShapeDtypeStruct((batch * pitch, SSM_WIDTH), F32),
        scratch_shapes=[pltpu.VMEM((gpb, rows, CHUNK_WIDTH), BF16),
                        pltpu.VMEM((SSM_INTERLEAVE, rows, 4 * SSM_STATE), F32),
                        pltpu.VMEM((SSM_INTERLEAVE, rows, 4 * SSM_STATE), F32)],
        compiler_params=pltpu.CompilerParams(
            dimension_semantics=("parallel",), vmem_limit_bytes=VMEM_LIMIT_BYTES),
    )(u.reshape(batch * pitch, SSM_WIDTH), ein, toep, eout, a_q)
    return y.reshape(batch, pitch, SSM_WIDTH)


def _mix_kernel(x_ref, attn_ref, ssm_ref, gw_ref, gb_ref, ga_ref, gs_ref, wo_ref, o_ref):
    y = jax.nn.gelu(ssm_ref[...])
    z = jnp.dot(y.astype(BF16), gw_ref[...].astype(BF16), preferred_element_type=F32) + gb_ref[...]
    s = y * jax.nn.sigmoid(z)
    sn = _rms(s) * gs_ref[...]
    an = _rms(attn_ref[...].astype(F32)) * ga_ref[...]
    mixed = jnp.concatenate([an, sn], axis=-1).astype(BF16)
    o_ref[...] = x_ref[...] + jnp.dot(mixed, wo_ref[...].astype(BF16), preferred_element_type=F32)


def _mix_out(x, attn, ssm_pre, seq, glu_w, glu_b, attn_gain, ssm_gain, w_out):
    t = x.shape[0]
    row = lambda width: pl.BlockSpec((TOKEN_TILE, width), lambda i: (i, 0))
    full = lambda r, c: pl.BlockSpec((r, c), lambda i: (0, 0))
    return pl.pallas_call(
        _mix_kernel,
        grid=(t // TOKEN_TILE,),
        in_specs=[row(D_MODEL), row(ATTN_WIDTH), _batch_rows(seq, SSM_WIDTH),
                  full(SSM_WIDTH, SSM_WIDTH), full(1, SSM_WIDTH),
                  full(1, ATTN_WIDTH), full(1, SSM_WIDTH), full(D_MODEL, D_MODEL)],
        out_specs=row(D_MODEL),
        out_shape=jax.ShapeDtypeStruct((t, D_MODEL), F32),
        compiler_params=pltpu.CompilerParams(
            dimension_semantics=("parallel",), vmem_limit_bytes=VMEM_LIMIT_BYTES),
    )(x, attn, ssm_pre, glu_w, glu_b.reshape(1, -1).astype(F32),
      attn_gain.reshape(1, -1).astype(F32), ssm_gain.reshape(1, -1).astype(F32), w_out)


def kernel(x, norm_ffn1, ffn1_w_gate, ffn1_w_up, ffn1_w_down, norm_mix, w_in, attn_sinks,
           ssm_lambda_re, ssm_lambda_im, ssm_log_dt, ssm_b_re, ssm_b_im, ssm_c_re, ssm_c_im,
           ssm_d, ssm_glu_w, ssm_glu_b, attn_out_norm, ssm_out_norm, w_out,
           norm_ffn2, ffn2_w_gate, ffn2_w_up, ffn2_w_down, final_norm):
    b, seq, d = x.shape
    depth = norm_ffn1.shape[0]
    assert d == D_MODEL and seq % BLOCK == 0 and seq % TOKEN_TILE == 0
    assert b == SUBLANES and seq % (2 * SSM_CHUNK * SSM_RELAYOUT_UNROLL) == 0
    h = x.reshape(b * seq, d).astype(F32)
    for l in range(depth):
        h = _ffn(h, norm_ffn1[l], ffn1_w_gate[l], ffn1_w_up[l], ffn1_w_down[l],
                 final_norm, final_norm=False)
        q, k, v, u = _in_proj(h, norm_mix[l], w_in[l], b, seq)
        attn = _attention(q.reshape(b, seq, -1), k.reshape(b, seq, -1), v.reshape(b, seq, -1),
                          attn_sinks[l])
        ssm_pre = _ssm(u, seq, ssm_lambda_re[l], ssm_lambda_im[l], ssm_log_dt[l],
                       ssm_b_re[l], ssm_b_im[l], ssm_c_re[l], ssm_c_im[l], ssm_d[l])
        h = _mix_out(h, attn.reshape(b * seq, -1), ssm_pre, seq, ssm_glu_w[l], ssm_glu_b[l],
                     attn_out_norm[l], ssm_out_norm[l], w_out[l])
        h = _ffn(h, norm_ffn2[l], ffn2_w_gate[l], ffn2_w_up[l], ffn2_w_down[l],
                 final_norm, final_norm=(l == depth - 1))
    return h.reshape(b, seq, d).astype(x.dtype)
```

```python
import functools

import jax
import jax.numpy as jnp
from jax import lax
from jax.experimental import pallas as pl
from jax.experimental.pallas import tpu as pltpu

F32 = jnp.float32
BF16 = jnp.bfloat16

D_MODEL = 1024
ATTN_HEADS = 8
ATTN_KV_HEADS = 2
Q_PER_KV = ATTN_HEADS // ATTN_KV_HEADS
HEAD_DIM = 64
ATTN_WIDTH = ATTN_HEADS * HEAD_DIM
KV_WIDTH = ATTN_KV_HEADS * HEAD_DIM
WINDOW = 128
BLOCK = 128
SSM_CH = 16
SSM_WIDTH = D_MODEL - ATTN_WIDTH
SSM_GROUPS = SSM_WIDTH // SSM_CH
SSM_STATE = 64
IN_WIDTH = ATTN_WIDTH + 2 * KV_WIDTH + SSM_WIDTH
D_FF = 2816
EPS = 1e-6
NEG_INF = -1e30
LAMBDA_RE_MAX = -1e-4

LANES = 128
SUBLANES = 8
VMEM_LIMIT_BYTES = 56 * 1024 * 1024

TOKEN_TILE = 1024
FFN_TOKEN_TILE = 512
FF_TILE = 256
SSM_CHUNK = 16
CHUNK_WIDTH = SSM_CHUNK * SSM_CH
GROUPS_PER_BLOCK = LANES // SSM_CH
SSM_INTERLEAVE = 4
SSM_RELAYOUT_UNROLL = 4
OPS_GROUPS_PER_STEP = 4

NT_DIMS = (((1,), (1,)), ((), ()))
TN_DIMS = (((0,), (0,)), ((), ()))


def _rms(x):
    return x * lax.rsqrt(jnp.mean(x * x, axis=-1, keepdims=True) + EPS)


def _ffn_kernel(x_ref, gain_ref, wg_ref, wu_ref, wd_ref, fgain_ref, o_ref, act_ref, *, final_norm):
    x = x_ref[...]
    hn = (_rms(x) * gain_ref[...]).astype(BF16)
    for j in range(D_FF // FF_TILE):
        cols = slice(j * FF_TILE, (j + 1) * FF_TILE)
        g = jnp.dot(hn, wg_ref[:, cols].astype(BF16), preferred_element_type=F32)
        u = jnp.dot(hn, wu_ref[:, cols].astype(BF16), preferred_element_type=F32)
        act_ref[:, cols] = (g * jax.nn.sigmoid(g) * u).astype(BF16)
    y = x + 0.5 * jnp.dot(act_ref[...], wd_ref[...].astype(BF16), preferred_element_type=F32)
    if final_norm:
        y = _rms(y) * fgain_ref[...]
    o_ref[...] = y


def _ffn(x, gain, w_gate, w_up, w_down, final_gain, final_norm):
    t = x.shape[0]
    resident = lambda r, c: pl.BlockSpec((r, c), lambda i: (0, 0), pipeline_mode=pl.Buffered(1))
    return pl.pallas_call(
        functools.partial(_ffn_kernel, final_norm=final_norm),
        grid=(t // FFN_TOKEN_TILE,),
        in_specs=[
            pl.BlockSpec((FFN_TOKEN_TILE, D_MODEL), lambda i: (i, 0)),
            resident(1, D_MODEL),
            resident(D_MODEL, D_FF),
            resident(D_MODEL, D_FF),
            resident(D_FF, D_MODEL),
            resident(1, D_MODEL),
        ],
        out_specs=pl.BlockSpec((FFN_TOKEN_TILE, D_MODEL), lambda i: (i, 0)),
        out_shape=jax.ShapeDtypeStruct((t, D_MODEL), F32),
        scratch_shapes=[pltpu.VMEM((FFN_TOKEN_TILE, D_FF), BF16)],
        compiler_params=pltpu.CompilerParams(
            dimension_semantics=("parallel",), vmem_limit_bytes=VMEM_LIMIT_BYTES),
    )(x, gain.reshape(1, D_MODEL), w_gate, w_up, w_down, final_gain.reshape(1, D_MODEL))


def _in_proj_kernel(x_ref, gain_ref, w_ref, q_ref, k_ref, v_ref, u_ref, *, seq):
    j = pl.program_id(1)
    hn = (_rms(x_ref[...]) * gain_ref[...]).astype(BF16)
    proj = jnp.dot(hn, w_ref[...].astype(BF16), preferred_element_type=F32)
    q_ref[...] = proj[:, :ATTN_WIDTH].astype(BF16)
    k_ref[...] = proj[:, ATTN_WIDTH:ATTN_WIDTH + KV_WIDTH].astype(BF16)
    v_ref[...] = proj[:, ATTN_WIDTH + KV_WIDTH:ATTN_WIDTH + 2 * KV_WIDTH].astype(BF16)
    rows = pl.ds(pl.multiple_of(j * TOKEN_TILE, TOKEN_TILE), TOKEN_TILE)
    u_ref[rows, :] = proj[:, ATTN_WIDTH + 2 * KV_WIDTH:]
    pitch = u_ref.shape[0]
    if pitch > seq:
        @pl.when(j == 0)
        def _():
            u_ref[seq:pitch, :] = jnp.zeros((pitch - seq, SSM_WIDTH), F32)


def _padded_seq(seq):
    return seq + SUBLANES if seq % (2 * SUBLANES) == 0 else seq


def _batch_rows(seq, width):
    tiles = seq // TOKEN_TILE
    return pl.BlockSpec((None, TOKEN_TILE, width), lambda i: (i // tiles, i % tiles, 0))


def _in_proj(x, gain, w_in, batch, seq):
    t = x.shape[0]
    tiles = seq // TOKEN_TILE
    pitch = _padded_seq(seq)
    row = lambda width: pl.BlockSpec((TOKEN_TILE, width), lambda b, j: (b * tiles + j, 0))
    return pl.pallas_call(
        functools.partial(_in_proj_kernel, seq=seq),
        grid=(batch, tiles),
        in_specs=[
            row(D_MODEL),
            pl.BlockSpec((1, D_MODEL), lambda b, j: (0, 0)),
            pl.BlockSpec((D_MODEL, IN_WIDTH), lambda b, j: (0, 0)),
        ],
        out_specs=[row(ATTN_WIDTH), row(KV_WIDTH), row(KV_WIDTH),
                   pl.BlockSpec((None, pitch, SSM_WIDTH), lambda b, j: (b, 0, 0))],
        out_shape=[
            jax.ShapeDtypeStruct((t, ATTN_WIDTH), BF16),
            jax.ShapeDtypeStruct((t, KV_WIDTH), BF16),
            jax.ShapeDtypeStruct((t, KV_WIDTH), BF16),
            jax.ShapeDtypeStruct((batch, pitch, SSM_WIDTH), F32),
        ],
        compiler_params=pltpu.CompilerParams(
            dimension_semantics=("parallel", "arbitrary"), vmem_limit_bytes=VMEM_LIMIT_BYTES),
    )(x, gain.reshape(1, D_MODEL), w_in)


def _attn_kernel(sink_ref, q_ref, kp_ref, kc_ref, kn_ref, vp_ref, vc_ref, vn_ref, o_ref,
                 bias_ref, s_ref):
    n = pl.program_id(1)
    last = pl.num_programs(1) - 1

    @pl.when(n == 0)
    def _():
        kj = lax.broadcasted_iota(jnp.int32, (3 * BLOCK, BLOCK), 0)
        qi = lax.broadcasted_iota(jnp.int32, (3 * BLOCK, BLOCK), 1)
        rel = jnp.abs(kj - BLOCK - qi)
        dist = rel.astype(F32)
        inside = rel <= WINDOW
        has_prev = kj >= BLOCK
        has_next = kj < 2 * BLOCK
        for variant, ok in enumerate((inside & has_prev, inside, inside & has_next)):
            for h in range(ATTN_HEADS):
                slope = float(2.0 ** (-8.0 * (h + 1) / ATTN_HEADS))
                bias_ref[variant, h] = jnp.where(ok, -slope * dist, NEG_INF)

    variant = jnp.where(n == 0, 0, jnp.where(n == last, 2, 1))
    for kh in range(ATTN_KV_HEADS):
        cols = slice(kh * HEAD_DIM, (kh + 1) * HEAD_DIM)
        kcat = jnp.concatenate([kp_ref[:, cols], kc_ref[:, cols], kn_ref[:, cols]], axis=0)
        heads = [kh * Q_PER_KV + g for g in range(Q_PER_KV)]
        qs = jnp.concatenate([q_ref[:, h * HEAD_DIM:(h + 1) * HEAD_DIM] for h in heads], axis=0)
        s_ref[kh] = lax.dot_general(kcat, qs * (HEAD_DIM ** -0.5), NT_DIMS,
                                    preferred_element_type=F32)
    outs = []
    for kh in range(ATTN_KV_HEADS):
        cols = slice(kh * HEAD_DIM, (kh + 1) * HEAD_DIM)
        vcat = jnp.concatenate([vp_ref[:, cols], vc_ref[:, cols], vn_ref[:, cols]], axis=0)
        for g in range(Q_PER_KV):
            h = kh * Q_PER_KV + g
            s = s_ref[kh, :, g * BLOCK:(g + 1) * BLOCK] + bias_ref[variant, h]
            sink = sink_ref[h]
            m = jnp.maximum(jnp.max(s, axis=0, keepdims=True), sink)
            e = jnp.exp(s - m)
            den = jnp.sum(e, axis=0, keepdims=True) + jnp.exp(sink - m)
            pv = lax.dot_general(vcat, e.astype(BF16), TN_DIMS,
                                 preferred_element_type=F32)
            outs.append(pv / den)
    o_ref[...] = jnp.concatenate(outs, axis=0).T.astype(o_ref.dtype)


def _attention(q, k, v, sinks):
    b, seq, _ = q.shape
    nb = seq // BLOCK
    kv_spec = lambda f: pl.BlockSpec((None, BLOCK, KV_WIDTH), f)
    prev = lambda bi, n: (bi, jnp.maximum(n - 1, 0), 0)
    cur = lambda bi, n: (bi, n, 0)
    nxt = lambda bi, n: (bi, jnp.minimum(n + 1, nb - 1), 0)
    assert nb >= 2 and HEAD_DIM ** -0.5 == 2.0 ** -3
    return pl.pallas_call(
        _attn_kernel,
        grid=(b, nb),
        in_specs=[
            pl.BlockSpec(memory_space=pltpu.SMEM),
            pl.BlockSpec((None, BLOCK, ATTN_WIDTH), cur),
            kv_spec(prev), kv_spec(cur), kv_spec(nxt),
            kv_spec(prev), kv_spec(cur), kv_spec(nxt),
        ],
        out_specs=pl.BlockSpec((None, BLOCK, ATTN_WIDTH), cur),
        out_shape=jax.ShapeDtypeStruct((b, seq, ATTN_WIDTH), BF16),
        scratch_shapes=[pltpu.VMEM((3, ATTN_HEADS, 3 * BLOCK, BLOCK), F32),
                        pltpu.VMEM((ATTN_KV_HEADS, 3 * BLOCK, Q_PER_KV * BLOCK), F32)],
        compiler_params=pltpu.CompilerParams(
            dimension_semantics=("parallel", "arbitrary"), vmem_limit_bytes=VMEM_LIMIT_BYTES),
    )(sinks.astype(F32), q, k, k, k, v, v, v)


def _ssm_ops_kernel(prm_ref, bc_ref, ein_ref, toep_ref, eout_ref, aq_ref):
    q, hc, p = SSM_CHUNK, SSM_CH, SSM_STATE
    hi = lax.Precision.HIGHEST
    fwd = lax.broadcasted_iota(jnp.int32, (1, 2 * p), 1) < p
    zero_row = jnp.zeros((1, 2 * p), F32)
    row_id = lax.broadcasted_iota(jnp.int32, (CHUNK_WIDTH, CHUNK_WIDTH), 0)
    col_id = lax.broadcasted_iota(jnp.int32, (CHUNK_WIDTH, CHUNK_WIDTH), 1)

    def table(select, n):
        picks = [select(m) for m in range(n)]
        re = jnp.concatenate([jnp.broadcast_to(r, (hc, 2 * p)) for r, _ in picks], axis=0)
        im = jnp.concatenate([jnp.broadcast_to(i, (hc, 2 * p)) for _, i in picks], axis=0)
        return re, im

    def tile_rows(x, n):
        return jnp.concatenate([x] * n, axis=0)

    for gi in range(prm_ref.shape[0]):
        lr = jnp.minimum(prm_ref[gi, 0:1, :], LAMBDA_RE_MAX)
        li = prm_ref[gi, 1:2, :]
        dt = jnp.exp(prm_ref[gi, 2:3, :])
        mag = jnp.exp(lr * dt)
        a_r = mag * jnp.cos(li * dt)
        a_i = mag * jnp.sin(li * dt)
        den = lr * lr + li * li
        coef_r = ((a_r - 1.0) * lr + a_i * li) / den
        coef_i = (a_i * lr - (a_r - 1.0) * li) / den
        b_r, b_i = bc_ref[gi, 0], bc_ref[gi, 1]
        c_r, c_i = bc_ref[gi, 2], bc_ref[gi, 3]
        bb_r = coef_r * b_r - coef_i * b_i
        bb_i = coef_r * b_i + coef_i * b_r

        pw = [(jnp.ones((1, 2 * p), F32), zero_row)]
        for _ in range(q):
            r, i = pw[-1]
            pw.append((r * a_r - i * a_i, r * a_i + i * a_r))

        def both(f_idx, b_idx):
            fr, fi = pw[f_idx] if f_idx is not None else (zero_row, zero_row)
            br, bi = pw[b_idx] if b_idx is not None else (zero_row, zero_row)
            return jnp.where(fwd, fr, br), jnp.where(fwd, fi, bi)

        p_r, p_i = table(lambda i: both(q - 1 - i, i), q)
        tb_r, tb_i = tile_rows(bb_r, q), tile_rows(bb_i, q)
        ein = jnp.concatenate([tb_r * p_r - tb_i * p_i, tb_r * p_i + tb_i * p_r], axis=1)
        ein_ref[gi] = ein.astype(BF16)

        p_r, p_i = table(lambda j: both(j + 1, q - j), q)
        tc_r, tc_i = tile_rows(c_r, q), tile_rows(c_i, q)
        eout = jnp.concatenate([tc_r * p_r - tc_i * p_i, -(tc_r * p_i + tc_i * p_r)], axis=1)
        eout_ref[gi] = eout.astype(BF16)

        def lag(m):
            return both(m - (q - 1) if q - 1 <= m <= 2 * q - 2 else None,
                        (q - 1) - m if m <= q - 1 else None)

        p_r, p_i = table(lag, 2 * q)
        tc_r, tc_i = tile_rows(c_r, 2 * q), tile_rows(c_i, 2 * q)
        cpt = jnp.concatenate([tc_r * p_r - tc_i * p_i, tc_r * p_i + tc_i * p_r], axis=1)
        bcat = jnp.concatenate([bb_r, -bb_i], axis=1)
        kern = lax.dot_general(bcat, cpt, NT_DIMS, precision=hi, preferred_element_type=F32)
        toep = jnp.concatenate(
            [kern[:, hc * (q - 1 - i):hc * (q - 1 - i) + CHUNK_WIDTH] for i in range(q)], axis=0)
        skip = jnp.concatenate([prm_ref[gi, 3:4, :]] * (CHUNK_WIDTH // (2 * p)), axis=1)
        toep_ref[gi] = (toep + jnp.where(row_id == col_id, skip, 0.0)).astype(BF16)

        aq_ref[gi] = jnp.concatenate(
            [pw[q][0], pw[q][1], jnp.zeros((SUBLANES - 2, 2 * p), F32)], axis=0)


def _ssm_operators(lam_re, lam_im, log_dt, b_re, b_im, c_re, c_im, d_skip):
    g, p, hc = SSM_GROUPS, SSM_STATE, SSM_CH
    lanes = lambda a: a.astype(F32).transpose(1, 0, 2).reshape(g, 2 * p)
    prm = jnp.stack(
        [lanes(lam_re), lanes(lam_im),
         jnp.repeat(log_dt.astype(F32).T, p, axis=1),
         jnp.tile(d_skip.astype(F32), (1, 2 * p // hc))]
        + [jnp.zeros((g, 2 * p), F32)] * (SUBLANES - 4), axis=1)
    bc = jnp.stack(
        [b_re.astype(F32).transpose(1, 3, 0, 2).reshape(g, hc, 2 * p),
         b_im.astype(F32).transpose(1, 3, 0, 2).reshape(g, hc, 2 * p),
         c_re.astype(F32).transpose(1, 2, 0, 3).reshape(g, hc, 2 * p),
         c_im.astype(F32).transpose(1, 2, 0, 3).reshape(g, hc, 2 * p)], axis=1)
    gs = OPS_GROUPS_PER_STEP
    mat = pl.BlockSpec((gs, CHUNK_WIDTH, CHUNK_WIDTH), lambda s: (s, 0, 0))
    mat_shape = jax.ShapeDtypeStruct((g, CHUNK_WIDTH, CHUNK_WIDTH), BF16)
    return pl.pallas_call(
        _ssm_ops_kernel,
        grid=(g // gs,),
        in_specs=[pl.BlockSpec((gs, SUBLANES, 2 * p), lambda s: (s, 0, 0)),
                  pl.BlockSpec((gs, 4, hc, 2 * p), lambda s: (s, 0, 0, 0))],
        out_specs=[mat, mat, mat, pl.BlockSpec((gs, SUBLANES, 2 * p), lambda s: (s, 0, 0))],
        out_shape=[mat_shape, mat_shape, mat_shape,
                   jax.ShapeDtypeStruct((g, SUBLANES, 2 * p), F32)],
        compiler_params=pltpu.CompilerParams(
            dimension_semantics=("parallel",), vmem_limit_bytes=VMEM_LIMIT_BYTES),
    )(prm, bc)


def _lane_roll(x, shift):
    shift %= LANES
    return jnp.concatenate([x[:, LANES - shift:], x[:, :LANES - shift]], axis=1)


def _block_transpose(v, lane_block):
    v = list(v)
    n = len(v)
    d = n // 2
    while d >= 1:
        upper = (lane_block & d) != 0
        for i in range(n):
            if i & d == 0:
                a, b = v[i], v[i + d]
                v[i] = jnp.where(upper, _lane_roll(b, SSM_CH * d), a)
                v[i + d] = jnp.where(upper, b, _lane_roll(a, -SSM_CH * d))
        d //= 2
    return v


def _ssm_kernel(u_ref, ein_ref, toep_ref, eout_ref, aq_ref, y_ref, ug_ref, s_ref, x_ref,
                *, batch, seq, pitch):
    q, half = SSM_CHUNK, SSM_STATE
    n_chunks = seq // q
    gpb = GROUPS_PER_BLOCK
    slab = 2 * batch
    lane_block = lax.broadcasted_iota(jnp.int32, (slab, LANES), 1) // SSM_CH

    for b in range(batch if pitch > seq else 0):
        y_ref[b * pitch + seq:(b + 1) * pitch, :] = jnp.zeros((pitch - seq, LANES), F32)

    def gather(cp, carry):
        rows = pl.ds(pl.multiple_of(cp * slab, slab), slab)
        for part in range(q // gpb):
            t0 = cp * 2 * q + part * gpb
            v = [jnp.concatenate([u_ref[pl.ds(t0 + i, batch, stride=pitch), :],
                                  u_ref[pl.ds(t0 + q + i, batch, stride=pitch), :]],
                                 axis=0).astype(BF16) for i in range(gpb)]
            w = _block_transpose(v, lane_block)
            for g in range(gpb):
                ug_ref[g, rows, part * LANES:(part + 1) * LANES] = w[g]
        return carry

    lax.fori_loop(0, n_chunks // 2, gather, 0, unroll=SSM_RELAYOUT_UNROLL)

    fwd_lane = lax.broadcasted_iota(jnp.int32, (batch, 2 * half), 1) < half
    zeros = jnp.zeros((batch, half), F32)
    last_rows = pl.ds((n_chunks - 1) * batch, batch)
    ni = SSM_INTERLEAVE
    for g0 in range(0, gpb, ni):
        for gi in range(ni):
            s_ref[gi] = jnp.dot(ug_ref[g0 + gi], ein_ref[g0 + gi], preferred_element_type=F32)
            x_ref[gi, 0:batch, 0:half] = zeros
            x_ref[gi, 0:batch, 2 * half:3 * half] = zeros
            x_ref[gi, last_rows, half:2 * half] = zeros
            x_ref[gi, last_rows, 3 * half:4 * half] = zeros

        def step(k, carry):
            rf = pl.multiple_of(k * batch, batch)
            rb = pl.multiple_of((n_chunks - 1 - k) * batch, batch)
            new = []
            for gi in range(ni):
                xr, xi = carry[gi]
                sre = jnp.where(fwd_lane, s_ref[gi, pl.ds(rf, batch), 0:2 * half],
                                s_ref[gi, pl.ds(rb, batch), 0:2 * half])
                sim = jnp.where(fwd_lane, s_ref[gi, pl.ds(rf, batch), 2 * half:4 * half],
                                s_ref[gi, pl.ds(rb, batch), 2 * half:4 * half])
                ar = aq_ref[g0 + gi, 0:1, :]
                ai = aq_ref[g0 + gi, 1:2, :]
                nr = ar * xr - ai * xi + sre
                nim = ar * xi + ai * xr + sim
                x_ref[gi, pl.ds(rf + batch, batch), 0:half] = nr[:, 0:half]
                x_ref[gi, pl.ds(rf + batch, batch), 2 * half:3 * half] = nim[:, 0:half]
                x_ref[gi, pl.ds(rb - batch, batch), half:2 * half] = nr[:, half:2 * half]
                x_ref[gi, pl.ds(rb - batch, batch), 3 * half:4 * half] = nim[:, half:2 * half]
                new.append((nr, nim))
            return tuple(new)

        init = tuple((jnp.zeros((batch, 2 * half), F32), jnp.zeros((batch, 2 * half), F32))
                     for _ in range(ni))
        lax.fori_loop(0, n_chunks - 1, step, init)

        for gi in range(ni):
            g = g0 + gi
            y = jnp.dot(ug_ref[g], toep_ref[g], preferred_element_type=F32)
            y += lax.dot_general(x_ref[gi].astype(BF16), eout_ref[g], NT_DIMS,
                                 preferred_element_type=F32)
            ug_ref[g] = y.astype(BF16)

    def scatter(cp, carry):
        rows = pl.ds(pl.multiple_of(cp * slab, slab), slab)
        for part in range(q // gpb):
            t0 = cp * 2 * q + part * gpb
            v = [ug_ref[g, rows, part * LANES:(part + 1) * LANES] for g in range(gpb)]
            w = _block_transpose(v, lane_block)
            for j in range(gpb):
                wj = w[j].astype(F32)
                y_ref[pl.ds(t0 + j, batch, stride=pitch), :] = wj[0:batch]
                y_ref[pl.ds(t0 + q + j, batch, stride=pitch), :] = wj[batch:slab]
        return carry

    lax.fori_loop(0, n_chunks // 2, scatter, 0, unroll=SSM_RELAYOUT_UNROLL)


def _ssm(u, seq, lam_re, lam_im, log_dt, b_re, b_im, c_re, c_im, d_skip):
    batch, pitch, _ = u.shape
    rows = batch * seq // SSM_CHUNK
    ein, toep, eout, a_q = _ssm_operators(lam_re, lam_im, log_dt, b_re, b_im, c_re, c_im, d_skip)
    gpb = GROUPS_PER_BLOCK
    mat = pl.BlockSpec((gpb, CHUNK_WIDTH, CHUNK_WIDTH), lambda s: (s, 0, 0))
    col = pl.BlockSpec((batch * pitch, LANES), lambda s: (0, s), pipeline_mode=pl.Buffered(1))
    y = pl.pallas_call(
        functools.partial(_ssm_kernel, batch=batch, seq=seq, pitch=pitch),
        grid=(SSM_GROUPS // gpb,),
        in_specs=[col, mat, mat, mat,
                  pl.BlockSpec((gpb, SUBLANES, 2 * SSM_STATE), lambda s: (s, 0, 0))],
        out_specs=col,
        out_shape=jax.ShapeDtypeStruct((batch * pitch, SSM_WIDTH), F32),
        scratch_shapes=[pltpu.VMEM((gpb, rows, CHUNK_WIDTH), BF16),
                        pltpu.VMEM((SSM_INTERLEAVE, rows, 4 * SSM_STATE), F32),
                        pltpu.VMEM((SSM_INTERLEAVE, rows, 4 * SSM_STATE), F32)],
        compiler_params=pltpu.CompilerParams(
            dimension_semantics=("parallel",), vmem_limit_bytes=VMEM_LIMIT_BYTES),
    )(u.reshape(batch * pitch, SSM_WIDTH), ein, toep, eout, a_q)
    return y.reshape(batch, pitch, SSM_WIDTH)


def _mix_kernel(x_ref, attn_ref, ssm_ref, gw_ref, gb_ref, ga_ref, gs_ref, wo_ref, o_ref):
    y = jax.nn.gelu(ssm_ref[...])
    z = jnp.dot(y.astype(BF16), gw_ref[...].astype(BF16), preferred_element_type=F32) + gb_ref[...]
    s = y * jax.nn.sigmoid(z)
    sn = _rms(s) * gs_ref[...]
    an = _rms(attn_ref[...].astype(F32)) * ga_ref[...]
    mixed = jnp.concatenate([an, sn], axis=-1).astype(BF16)
    o_ref[...] = x_ref[...] + jnp.dot(mixed, wo_ref[...].astype(BF16), preferred_element_type=F32)


def _mix_out(x, attn, ssm_pre, seq, glu_w, glu_b, attn_gain, ssm_gain, w_out):
    t = x.shape[0]
    row = lambda width: pl.BlockSpec((TOKEN_TILE, width), lambda i: (i, 0))
    full = lambda r, c: pl.BlockSpec((r, c), lambda i: (0, 0))
    return pl.pallas_call(
        _mix_kernel,
        grid=(t // TOKEN_TILE,),
        in_specs=[row(D_MODEL), row(ATTN_WIDTH), _batch_rows(seq, SSM_WIDTH),
                  full(SSM_WIDTH, SSM_WIDTH), full(1, SSM_WIDTH),
                  full(1, ATTN_WIDTH), full(1, SSM_WIDTH), full(D_MODEL, D_MODEL)],
        out_specs=row(D_MODEL),
        out_shape=jax.ShapeDtypeStruct((t, D_MODEL), F32),
        compiler_params=pltpu.CompilerParams(
            dimension_semantics=("parallel",), vmem_limit_bytes=VMEM_LIMIT_BYTES),
    )(x, attn, ssm_pre, glu_w, glu_b.reshape(1, -1).astype(F32),
      attn_gain.reshape(1, -1).astype(F32), ssm_gain.reshape(1, -1).astype(F32), w_out)


def kernel(x, norm_ffn1, ffn1_w_gate, ffn1_w_up, ffn1_w_down, norm_mix, w_in, attn_sinks,
           ssm_lambda_re, ssm_lambda_im, ssm_log_dt, ssm_b_re, ssm_b_im, ssm_c_re, ssm_c_im,
           ssm_d, ssm_glu_w, ssm_glu_b, attn_out_norm, ssm_out_norm, w_out,
           norm_ffn2, ffn2_w_gate, ffn2_w_up, ffn2_w_down, final_norm):
    b, seq, d = x.shape
    depth = norm_ffn1.shape[0]
    assert d == D_MODEL and seq % BLOCK == 0 and seq % TOKEN_TILE == 0
    assert b == SUBLANES and seq % (2 * SSM_CHUNK * SSM_RELAYOUT_UNROLL) == 0
    h = x.reshape(b * seq, d).astype(F32)
    for l in range(depth):
        h = _ffn(h, norm_ffn1[l], ffn1_w_gate[l], ffn1_w_up[l], ffn1_w_down[l],
                 final_norm, final_norm=False)
        q, k, v, u = _in_proj(h, norm_mix[l], w_in[l], b, seq)
        attn = _attention(q.reshape(b, seq, -1), k.reshape(b, seq, -1), v.reshape(b, seq, -1),
                          attn_sinks[l])
        ssm_pre = _ssm(u, seq, ssm_lambda_re[l], ssm_lambda_im[l], ssm_log_dt[l],
                       ssm_b_re[l], ssm_b_im[l], ssm_c_re[l], ssm_c_im[l], ssm_d[l])
        h = _mix_out(h, attn.reshape(b * seq, -1), ssm_pre, seq, ssm_glu_w[l], ssm_glu_b[l],
                     attn_out_norm[l], ssm_out_norm[l], w_out[l])
        h = _ffn(h, norm_ffn2[l], ffn2_w_gate[l], ffn2_w_up[l], ffn2_w_down[l],
                 final_norm, final_norm=(l == depth - 1))
    return h.reshape(b, seq, d).astype(x.dtype)
```

```python
import functools

import jax
import jax.numpy as jnp
from jax import lax
from jax.experimental import pallas as pl
from jax.experimental.pallas import tpu as pltpu

F32 = jnp.float32
BF16 = jnp.bfloat16

D_MODEL = 1024
ATTN_HEADS = 8
ATTN_KV_HEADS = 2
Q_PER_KV = ATTN_HEADS // ATTN_KV_HEADS
HEAD_DIM = 64
ATTN_WIDTH = ATTN_HEADS * HEAD_DIM
KV_WIDTH = ATTN_KV_HEADS * HEAD_DIM
WINDOW = 128
BLOCK = 128
SSM_CH = 16
SSM_WIDTH = D_MODEL - ATTN_WIDTH
SSM_GROUPS = SSM_WIDTH // SSM_CH
SSM_STATE = 64
IN_WIDTH = ATTN_WIDTH + 2 * KV_WIDTH + SSM_WIDTH
D_FF = 2816
EPS = 1e-6
NEG_INF = -1e30
LAMBDA_RE_MAX = -1e-4

LANES = 128
SUBLANES = 8
VMEM_LIMIT_BYTES = 56 * 1024 * 1024

TOKEN_TILE = 1024
FFN_TOKEN_TILE = 512
FF_TILE = 256
ATTN_BLOCKS_PER_STEP = 4
SSM_CHUNK = 16
CHUNK_WIDTH = SSM_CHUNK * SSM_CH
GROUPS_PER_BLOCK = LANES // SSM_CH
SSM_INTERLEAVE = 4
SSM_RELAYOUT_UNROLL = 4
OPS_GROUPS_PER_STEP = 4

NT_DIMS = (((1,), (1,)), ((), ()))
TN_DIMS = (((0,), (0,)), ((), ()))


def _rms(x):
    return x * lax.rsqrt(jnp.mean(x * x, axis=-1, keepdims=True) + EPS)


def _ffn_kernel(x_ref, gain_ref, wg_ref, wu_ref, wd_ref, fgain_ref, o_ref, act_ref, *, final_norm):
    x = x_ref[...]
    hn = (_rms(x) * gain_ref[...]).astype(BF16)
    for j in range(D_FF // FF_TILE):
        cols = slice(j * FF_TILE, (j + 1) * FF_TILE)
        g = jnp.dot(hn, wg_ref[:, cols].astype(BF16), preferred_element_type=F32)
        u = jnp.dot(hn, wu_ref[:, cols].astype(BF16), preferred_element_type=F32)
        act_ref[:, cols] = (g * jax.nn.sigmoid(g) * u).astype(BF16)
    y = x + 0.5 * jnp.dot(act_ref[...], wd_ref[...].astype(BF16), preferred_element_type=F32)
    if final_norm:
        y = _rms(y) * fgain_ref[...]
    o_ref[...] = y


def _ffn(x, gain, w_gate, w_up, w_down, final_gain, final_norm):
    t = x.shape[0]
    resident = lambda r, c: pl.BlockSpec((r, c), lambda i: (0, 0), pipeline_mode=pl.Buffered(1))
    return pl.pallas_call(
        functools.partial(_ffn_kernel, final_norm=final_norm),
        grid=(t // FFN_TOKEN_TILE,),
        in_specs=[
            pl.BlockSpec((FFN_TOKEN_TILE, D_MODEL), lambda i: (i, 0)),
            resident(1, D_MODEL),
            resident(D_MODEL, D_FF),
            resident(D_MODEL, D_FF),
            resident(D_FF, D_MODEL),
            resident(1, D_MODEL),
        ],
        out_specs=pl.BlockSpec((FFN_TOKEN_TILE, D_MODEL), lambda i: (i, 0)),
        out_shape=jax.ShapeDtypeStruct((t, D_MODEL), F32),
        scratch_shapes=[pltpu.VMEM((FFN_TOKEN_TILE, D_FF), BF16)],
        compiler_params=pltpu.CompilerParams(
            dimension_semantics=("parallel",), vmem_limit_bytes=VMEM_LIMIT_BYTES),
    )(x, gain.reshape(1, D_MODEL), w_gate, w_up, w_down, final_gain.reshape(1, D_MODEL))


def _in_proj_kernel(x_ref, gain_ref, w_ref, q_ref, k_ref, v_ref, u_ref, *, seq):
    j = pl.program_id(1)
    hn = (_rms(x_ref[...]) * gain_ref[...]).astype(BF16)
    proj = jnp.dot(hn, w_ref[...].astype(BF16), preferred_element_type=F32)
    q_ref[...] = proj[:, :ATTN_WIDTH].astype(BF16)
    k_ref[...] = proj[:, ATTN_WIDTH:ATTN_WIDTH + KV_WIDTH].astype(BF16)
    v_ref[...] = proj[:, ATTN_WIDTH + KV_WIDTH:ATTN_WIDTH + 2 * KV_WIDTH].astype(BF16)
    rows = pl.ds(pl.multiple_of(j * TOKEN_TILE, TOKEN_TILE), TOKEN_TILE)
    u_ref[rows, :] = proj[:, ATTN_WIDTH + 2 * KV_WIDTH:]
    pitch = u_ref.shape[0]
    if pitch > seq:
        @pl.when(j == 0)
        def _():
            u_ref[seq:pitch, :] = jnp.zeros((pitch - seq, SSM_WIDTH), F32)


def _padded_seq(seq):
    return seq + SUBLANES if seq % (2 * SUBLANES) == 0 else seq


def _batch_rows(seq, width):
    tiles = seq // TOKEN_TILE
    return pl.BlockSpec((None, TOKEN_TILE, width), lambda i: (i // tiles, i % tiles, 0))


def _in_proj(x, gain, w_in, batch, seq):
    t = x.shape[0]
    tiles = seq // TOKEN_TILE
    pitch = _padded_seq(seq)
    row = lambda width: pl.BlockSpec((TOKEN_TILE, width), lambda b, j: (b * tiles + j, 0))
    return pl.pallas_call(
        functools.partial(_in_proj_kernel, seq=seq),
        grid=(batch, tiles),
        in_specs=[
            row(D_MODEL),
            pl.BlockSpec((1, D_MODEL), lambda b, j: (0, 0)),
            pl.BlockSpec((D_MODEL, IN_WIDTH), lambda b, j: (0, 0)),
        ],
        out_specs=[row(ATTN_WIDTH), row(KV_WIDTH), row(KV_WIDTH),
                   pl.BlockSpec((None, pitch, SSM_WIDTH), lambda b, j: (b, 0, 0))],
        out_shape=[
            jax.ShapeDtypeStruct((t, ATTN_WIDTH), BF16),
            jax.ShapeDtypeStruct((t, KV_WIDTH), BF16),
            jax.ShapeDtypeStruct((t, KV_WIDTH), BF16),
            jax.ShapeDtypeStruct((batch, pitch, SSM_WIDTH), F32),
        ],
        compiler_params=pltpu.CompilerParams(
            dimension_semantics=("parallel", "arbitrary"), vmem_limit_bytes=VMEM_LIMIT_BYTES),
    )(x, gain.reshape(1, D_MODEL), w_in)


def _attn_kernel(sink_ref, q_ref, kp_ref, kc_ref, kn_ref, vp_ref, vc_ref, vn_ref, o_ref,
                 bias_ref, k_ref, v_ref, s_ref):
    n = pl.program_id(1)
    last = pl.num_programs(1) - 1
    nblk = ATTN_BLOCKS_PER_STEP

    @pl.when(n == 0)
    def _():
        kj = lax.broadcasted_iota(jnp.int32, (3 * BLOCK, BLOCK), 0)
        qi = lax.broadcasted_iota(jnp.int32, (3 * BLOCK, BLOCK), 1)
        rel = jnp.abs(kj - BLOCK - qi)
        dist = rel.astype(F32)
        inside = rel <= WINDOW
        has_prev = kj >= BLOCK
        has_next = kj < 2 * BLOCK
        for variant, ok in enumerate((inside & has_prev, inside, inside & has_next)):
            for h in range(ATTN_HEADS):
                slope = float(2.0 ** (-8.0 * (h + 1) / ATTN_HEADS))
                bias_ref[variant, h] = jnp.where(ok, -slope * dist, NEG_INF)

    k_ref[0:BLOCK] = kp_ref[...]
    k_ref[BLOCK:(nblk + 1) * BLOCK] = kc_ref[...]
    k_ref[(nblk + 1) * BLOCK:(nblk + 2) * BLOCK] = kn_ref[...]
    v_ref[0:BLOCK] = vp_ref[...]
    v_ref[BLOCK:(nblk + 1) * BLOCK] = vc_ref[...]
    v_ref[(nblk + 1) * BLOCK:(nblk + 2) * BLOCK] = vn_ref[...]

    for j in range(nblk):
        for kh in range(ATTN_KV_HEADS):
            kcat = k_ref[j * BLOCK:(j + 3) * BLOCK, kh * HEAD_DIM:(kh + 1) * HEAD_DIM]
            heads = [kh * Q_PER_KV + g for g in range(Q_PER_KV)]
            qs = jnp.concatenate(
                [q_ref[j * BLOCK:(j + 1) * BLOCK, h * HEAD_DIM:(h + 1) * HEAD_DIM] for h in heads],
                axis=0)
            s_ref[j, kh] = lax.dot_general(kcat, qs * (HEAD_DIM ** -0.5), NT_DIMS,
                                           preferred_element_type=F32)

    for j in range(nblk):
        variant = 1
        if j == 0:
            variant = jnp.where(n == 0, 0, variant)
        if j == nblk - 1:
            variant = jnp.where(n == last, 2, variant)
        outs = []
        for kh in range(ATTN_KV_HEADS):
            vcat = v_ref[j * BLOCK:(j + 3) * BLOCK, kh * HEAD_DIM:(kh + 1) * HEAD_DIM]
            for g in range(Q_PER_KV):
                h = kh * Q_PER_KV + g
                s = s_ref[j, kh, :, g * BLOCK:(g + 1) * BLOCK] + bias_ref[variant, h]
                sink = sink_ref[h]
                m = jnp.maximum(jnp.max(s, axis=0, keepdims=True), sink)
                e = jnp.exp(s - m)
                den = jnp.sum(e, axis=0, keepdims=True) + jnp.exp(sink - m)
                pv = lax.dot_general(vcat, e.astype(BF16), TN_DIMS,
                                     preferred_element_type=F32)
                outs.append(pv / den)
        o_ref[j * BLOCK:(j + 1) * BLOCK, :] = jnp.concatenate(outs, axis=0).T.astype(o_ref.dtype)


def _attention(q, k, v, sinks):
    b, seq, _ = q.shape
    nblk = ATTN_BLOCKS_PER_STEP
    steps = seq // (nblk * BLOCK)
    nb = seq // BLOCK
    assert seq % (nblk * BLOCK) == 0 and nb >= 2 and HEAD_DIM ** -0.5 == 2.0 ** -3
    edge = lambda f: pl.BlockSpec((None, BLOCK, KV_WIDTH), f)
    body = pl.BlockSpec((None, nblk * BLOCK, KV_WIDTH), lambda bi, n: (bi, n, 0))
    prev = lambda bi, n: (bi, jnp.maximum(n * nblk - 1, 0), 0)
    nxt = lambda bi, n: (bi, jnp.minimum((n + 1) * nblk, nb - 1), 0)
    rows = pl.BlockSpec((None, nblk * BLOCK, ATTN_WIDTH), lambda bi, n: (bi, n, 0))
    return pl.pallas_call(
        _attn_kernel,
        grid=(b, steps),
        in_specs=[pl.BlockSpec(memory_space=pltpu.SMEM), rows,
                  edge(prev), body, edge(nxt), edge(prev), body, edge(nxt)],
        out_specs=rows,
        out_shape=jax.ShapeDtypeStruct((b, seq, ATTN_WIDTH), BF16),
        scratch_shapes=[pltpu.VMEM((3, ATTN_HEADS, 3 * BLOCK, BLOCK), F32),
                        pltpu.VMEM(((nblk + 2) * BLOCK, KV_WIDTH), BF16),
                        pltpu.VMEM(((nblk + 2) * BLOCK, KV_WIDTH), BF16),
                        pltpu.VMEM((nblk, ATTN_KV_HEADS, 3 * BLOCK, Q_PER_KV * BLOCK), F32)],
        compiler_params=pltpu.CompilerParams(
            dimension_semantics=("parallel", "arbitrary"), vmem_limit_bytes=VMEM_LIMIT_BYTES),
    )(sinks.astype(F32), q, k, k, k, v, v, v)


def _ssm_ops_kernel(prm_ref, bc_ref, ein_ref, toep_ref, eout_ref, aq_ref):
    q, hc, p = SSM_CHUNK, SSM_CH, SSM_STATE
    hi = lax.Precision.HIGHEST
    fwd = lax.broadcasted_iota(jnp.int32, (1, 2 * p), 1) < p
    zero_row = jnp.zeros((1, 2 * p), F32)
    row_id = lax.broadcasted_iota(jnp.int32, (CHUNK_WIDTH, CHUNK_WIDTH), 0)
    col_id = lax.broadcasted_iota(jnp.int32, (CHUNK_WIDTH, CHUNK_WIDTH), 1)

    def table(select, n):
        picks = [select(m) for m in range(n)]
        re = jnp.concatenate([jnp.broadcast_to(r, (hc, 2 * p)) for r, _ in picks], axis=0)
        im = jnp.concatenate([jnp.broadcast_to(i, (hc, 2 * p)) for _, i in picks], axis=0)
        return re, im

    def tile_rows(x, n):
        return jnp.concatenate([x] * n, axis=0)

    for gi in range(prm_ref.shape[0]):
        lr = jnp.minimum(prm_ref[gi, 0:1, :], LAMBDA_RE_MAX)
        li = prm_ref[gi, 1:2, :]
        dt = jnp.exp(prm_ref[gi, 2:3, :])
        mag = jnp.exp(lr * dt)
        a_r = mag * jnp.cos(li * dt)
        a_i = mag * jnp.sin(li * dt)
        den = lr * lr + li * li
        coef_r = ((a_r - 1.0) * lr + a_i * li) / den
        coef_i = (a_i * lr - (a_r - 1.0) * li) / den
        b_r, b_i = bc_ref[gi, 0], bc_ref[gi, 1]
        c_r, c_i = bc_ref[gi, 2], bc_ref[gi, 3]
        bb_r = coef_r * b_r - coef_i * b_i
        bb_i = coef_r * b_i + coef_i * b_r

        pw = [(jnp.ones((1, 2 * p), F32), zero_row)]
        for _ in range(q):
            r, i = pw[-1]
            pw.append((r * a_r - i * a_i, r * a_i + i * a_r))

        def both(f_idx, b_idx):
            fr, fi = pw[f_idx] if f_idx is not None else (zero_row, zero_row)
            br, bi = pw[b_idx] if b_idx is not None else (zero_row, zero_row)
            return jnp.where(fwd, fr, br), jnp.where(fwd, fi, bi)

        p_r, p_i = table(lambda i: both(q - 1 - i, i), q)
        tb_r, tb_i = tile_rows(bb_r, q), tile_rows(bb_i, q)
        ein = jnp.concatenate([tb_r * p_r - tb_i * p_i, tb_r * p_i + tb_i * p_r], axis=1)
        ein_ref[gi] = ein.astype(BF16)

        p_r, p_i = table(lambda j: both(j + 1, q - j), q)
        tc_r, tc_i = tile_rows(c_r, q), tile_rows(c_i, q)
        eout = jnp.concatenate([tc_r * p_r - tc_i * p_i, -(tc_r * p_i + tc_i * p_r)], axis=1)
        eout_ref[gi] = eout.astype(BF16)

        def lag(m):
            return both(m - (q - 1) if q - 1 <= m <= 2 * q - 2 else None,
                        (q - 1) - m if m <= q - 1 else None)

        p_r, p_i = table(lag, 2 * q)
        tc_r, tc_i = tile_rows(c_r, 2 * q), tile_rows(c_i, 2 * q)
        cpt = jnp.concatenate([tc_r * p_r - tc_i * p_i, tc_r * p_i + tc_i * p_r], axis=1)
        bcat = jnp.concatenate([bb_r, -bb_i], axis=1)
        kern = lax.dot_general(bcat, cpt, NT_DIMS, precision=hi, preferred_element_type=F32)
        toep = jnp.concatenate(
            [kern[:, hc * (q - 1 - i):hc * (q - 1 - i) + CHUNK_WIDTH] for i in range(q)], axis=0)
        skip = jnp.concatenate([prm_ref[gi, 3:4, :]] * (CHUNK_WIDTH // (2 * p)), axis=1)
        toep_ref[gi] = (toep + jnp.where(row_id == col_id, skip, 0.0)).astype(BF16)

        aq_ref[gi] = jnp.concatenate(
            [pw[q][0], pw[q][1], jnp.zeros((SUBLANES - 2, 2 * p), F32)], axis=0)


def _ssm_operators(lam_re, lam_im, log_dt, b_re, b_im, c_re, c_im, d_skip):
    g, p, hc = SSM_GROUPS, SSM_STATE, SSM_CH
    lanes = lambda a: a.astype(F32).transpose(1, 0, 2).reshape(g, 2 * p)
    prm = jnp.stack(
        [lanes(lam_re), lanes(lam_im),
         jnp.repeat(log_dt.astype(F32).T, p, axis=1),
         jnp.tile(d_skip.astype(F32), (1, 2 * p // hc))]
        + [jnp.zeros((g, 2 * p), F32)] * (SUBLANES - 4), axis=1)
    bc = jnp.stack(
        [b_re.astype(F32).transpose(1, 3, 0, 2).reshape(g, hc, 2 * p),
         b_im.astype(F32).transpose(1, 3, 0, 2).reshape(g, hc, 2 * p),
         c_re.astype(F32).transpose(1, 2, 0, 3).reshape(g, hc, 2 * p),
         c_im.astype(F32).transpose(1, 2, 0, 3).reshape(g, hc, 2 * p)], axis=1)
    gs = OPS_GROUPS_PER_STEP
    mat = pl.BlockSpec((gs, CHUNK_WIDTH, CHUNK_WIDTH), lambda s: (s, 0, 0))
    mat_shape = jax.ShapeDtypeStruct((g, CHUNK_WIDTH, CHUNK_WIDTH), BF16)
    return pl.pallas_call(
        _ssm_ops_kernel,
        grid=(g // gs,),
        in_specs=[pl.BlockSpec((gs, SUBLANES, 2 * p), lambda s: (s, 0, 0)),
                  pl.BlockSpec((gs, 4, hc, 2 * p), lambda s: (s, 0, 0, 0))],
        out_specs=[mat, mat, mat, pl.BlockSpec((gs, SUBLANES, 2 * p), lambda s: (s, 0, 0))],
        out_shape=[mat_shape, mat_shape, mat_shape,
                   jax.ShapeDtypeStruct((g, SUBLANES, 2 * p), F32)],
        compiler_params=pltpu.CompilerParams(
            dimension_semantics=("parallel",), vmem_limit_bytes=VMEM_LIMIT_BYTES),
    )(prm, bc)


def _lane_roll(x, shift):
    shift %= LANES
    return jnp.concatenate([x[:, LANES - shift:], x[:, :LANES - shift]], axis=1)


def _block_transpose(v, lane_block):
    v = list(v)
    n = len(v)
    d = n // 2
    while d >= 1:
        upper = (lane_block & d) != 0
        for i in range(n):
            if i & d == 0:
                a, b = v[i], v[i + d]
                v[i] = jnp.where(upper, _lane_roll(b, SSM_CH * d), a)
                v[i + d] = jnp.where(upper, b, _lane_roll(a, -SSM_CH * d))
        d //= 2
    return v


def _ssm_kernel(u_ref, ein_ref, toep_ref, eout_ref, aq_ref, y_ref, ug_ref, s_ref, x_ref,
                *, batch, seq, pitch):
    q, half = SSM_CHUNK, SSM_STATE
    n_chunks = seq // q
    gpb = GROUPS_PER_BLOCK
    slab = 2 * batch
    lane_block = lax.broadcasted_iota(jnp.int32, (slab, LANES), 1) // SSM_CH

    for b in range(batch if pitch > seq else 0):
        y_ref[b * pitch + seq:(b + 1) * pitch, :] = jnp.zeros((pitch - seq, LANES), F32)

    def gather(cp, carry):
        rows = pl.ds(pl.multiple_of(cp * slab, slab), slab)
        for part in range(q // gpb):
            t0 = cp * 2 * q + part * gpb
            v = [jnp.concatenate([u_ref[pl.ds(t0 + i, batch, stride=pitch), :],
                                  u_ref[pl.ds(t0 + q + i, batch, stride=pitch), :]],
                                 axis=0).astype(BF16) for i in range(gpb)]
            w = _block_transpose(v, lane_block)
            for g in range(gpb):
                ug_ref[g, rows, part * LANES:(part + 1) * LANES] = w[g]
        return carry

    lax.fori_loop(0, n_chunks // 2, gather, 0, unroll=SSM_RELAYOUT_UNROLL)

    fwd_lane = lax.broadcasted_iota(jnp.int32, (batch, 2 * half), 1) < half
    zeros = jnp.zeros((batch, half), F32)
    last_rows = pl.ds((n_chunks - 1) * batch, batch)
    ni = SSM_INTERLEAVE
    for g0 in range(0, gpb, ni):
        for gi in range(ni):
            s_ref[gi] = jnp.dot(ug_ref[g0 + gi], ein_ref[g0 + gi], preferred_element_type=F32)
            x_ref[gi, 0:batch, 0:half] = zeros
            x_ref[gi, 0:batch, 2 * half:3 * half] = zeros
            x_ref[gi, last_rows, half:2 * half] = zeros
            x_ref[gi, last_rows, 3 * half:4 * half] = zeros

        def step(k, carry):
            rf = pl.multiple_of(k * batch, batch)
            rb = pl.multiple_of((n_chunks - 1 - k) * batch, batch)
            new = []
            for gi in range(ni):
                xr, xi = carry[gi]
                sre = jnp.where(fwd_lane, s_ref[gi, pl.ds(rf, batch), 0:2 * half],
                                s_ref[gi, pl.ds(rb, batch), 0:2 * half])
                sim = jnp.where(fwd_lane, s_ref[gi, pl.ds(rf, batch), 2 * half:4 * half],
                                s_ref[gi, pl.ds(rb, batch), 2 * half:4 * half])
                ar = aq_ref[g0 + gi, 0:1, :]
                ai = aq_ref[g0 + gi, 1:2, :]
                nr = ar * xr - ai * xi + sre
                nim = ar * xi + ai * xr + sim
                x_ref[gi, pl.ds(rf + batch, batch), 0:half] = nr[:, 0:half]
                x_ref[gi, pl.ds(rf + batch, batch), 2 * half:3 * half] = nim[:, 0:half]
                x_ref[gi, pl.ds(rb - batch, batch), half:2 * half] = nr[:, half:2 * half]
                x_ref[gi, pl.ds(rb - batch, batch), 3 * half:4 * half] = nim[:, half:2 * half]
                new.append((nr, nim))
            return tuple(new)

        init = tuple((jnp.zeros((batch, 2 * half), F32), jnp.zeros((batch, 2 * half), F32))
                     for _ in range(ni))
        lax.fori_loop(0, n_chunks - 1, step, init)

        for gi in range(ni):
            g = g0 + gi
            y = jnp.dot(ug_ref[g], toep_ref[g], preferred_element_type=F32)
            y += lax.dot_general(x_ref[gi].astype(BF16), eout_ref[g], NT_DIMS,
                                 preferred_element_type=F32)
            ug_ref[g] = y.astype(BF16)

    def scatter(cp, carry):
        rows = pl.ds(pl.multiple_of(cp * slab, slab), slab)
        for part in range(q // gpb):
            t0 = cp * 2 * q + part * gpb
            v = [ug_ref[g, rows, part * LANES:(part + 1) * LANES] for g in range(gpb)]
            w = _block_transpose(v, lane_block)
            for j in range(gpb):
                wj = w[j].astype(F32)
                y_ref[pl.ds(t0 + j, batch, stride=pitch), :] = wj[0:batch]
                y_ref[pl.ds(t0 + q + j, batch, stride=pitch), :] = wj[batch:slab]
        return carry

    lax.fori_loop(0, n_chunks // 2, scatter, 0, unroll=SSM_RELAYOUT_UNROLL)


def _ssm(u, seq, lam_re, lam_im, log_dt, b_re, b_im, c_re, c_im, d_skip):
    batch, pitch, _ = u.shape
    rows = batch * seq // SSM_CHUNK
    ein, toep, eout, a_q = _ssm_operators(lam_re, lam_im, log_dt, b_re, b_im, c_re, c_im, d_skip)
    gpb = GROUPS_PER_BLOCK
    mat = pl.BlockSpec((gpb, CHUNK_WIDTH, CHUNK_WIDTH), lambda s: (s, 0, 0))
    col = pl.BlockSpec((batch * pitch, LANES), lambda s: (0, s), pipeline_mode=pl.Buffered(1))
    y = pl.pallas_call(
        functools.partial(_ssm_kernel, batch=batch, seq=seq, pitch=pitch),
        grid=(SSM_GROUPS // gpb,),
        in_specs=[col, mat, mat, mat,
                  pl.BlockSpec((gpb, SUBLANES, 2 * SSM_STATE), lambda s: (s, 0, 0))],
        out_specs=col,
        out_shape=jax.ShapeDtypeStruct((batch * pitch, SSM_WIDTH), F32),
        scratch_shapes=[pltpu.VMEM((gpb, rows, CHUNK_WIDTH), BF16),
                        pltpu.VMEM((SSM_INTERLEAVE, rows, 4 * SSM_STATE), F32),
                        pltpu.VMEM((SSM_INTERLEAVE, rows, 4 * SSM_STATE), F32)],
        compiler_params=pltpu.CompilerParams(
            dimension_semantics=("parallel",), vmem_limit_bytes=VMEM_LIMIT_BYTES),
    )(u.reshape(batch * pitch, SSM_WIDTH), ein, toep, eout, a_q)
    return y.reshape(batch, pitch, SSM_WIDTH)


def _mix_kernel(x_ref, attn_ref, ssm_ref, gw_ref, gb_ref, ga_ref, gs_ref, wo_ref, o_ref):
    y = jax.nn.gelu(ssm_ref[...])
    z = jnp.dot(y.astype(BF16), gw_ref[...].astype(BF16), preferred_element_type=F32) + gb_ref[...]
    s = y * jax.nn.sigmoid(z)
    sn = _rms(s) * gs_ref[...]
    an = _rms(attn_ref[...].astype(F32)) * ga_ref[...]
    mixed = jnp.concatenate([an, sn], axis=-1).astype(BF16)
    o_ref[...] = x_ref[...] + jnp.dot(mixed, wo_ref[...].astype(BF16), preferred_element_type=F32)


def _mix_out(x, attn, ssm_pre, seq, glu_w, glu_b, attn_gain, ssm_gain, w_out):
    t = x.shape[0]
    row = lambda width: pl.BlockSpec((TOKEN_TILE, width), lambda i: (i, 0))
    full = lambda r, c: pl.BlockSpec((r, c), lambda i: (0, 0))
    return pl.pallas_call(
        _mix_kernel,
        grid=(t // TOKEN_TILE,),
        in_specs=[row(D_MODEL), row(ATTN_WIDTH), _batch_rows(seq, SSM_WIDTH),
                  full(SSM_WIDTH, SSM_WIDTH), full(1, SSM_WIDTH),
                  full(1, ATTN_WIDTH), full(1, SSM_WIDTH), full(D_MODEL, D_MODEL)],
        out_specs=row(D_MODEL),
        out_shape=jax.ShapeDtypeStruct((t, D_MODEL), F32),
        compiler_params=pltpu.CompilerParams(
            dimension_semantics=("parallel",), vmem_limit_bytes=VMEM_LIMIT_BYTES),
    )(x, attn, ssm_pre, glu_w, glu_b.reshape(1, -1).astype(F32),
      attn_gain.reshape(1, -1).astype(F32), ssm_gain.reshape(1, -1).astype(F32), w_out)


def kernel(x, norm_ffn1, ffn1_w_gate, ffn1_w_up, ffn1_w_down, norm_mix, w_in, attn_sinks,
           ssm_lambda_re, ssm_lambda_im, ssm_log_dt, ssm_b_re, ssm_b_im, ssm_c_re, ssm_c_im,
           ssm_d, ssm_glu_w, ssm_glu_b, attn_out_norm, ssm_out_norm, w_out,
           norm_ffn2, ffn2_w_gate, ffn2_w_up, ffn2_w_down, final_norm):
    b, seq, d = x.shape
    depth = norm_ffn1.shape[0]
    assert d == D_MODEL and seq % BLOCK == 0 and seq % TOKEN_TILE == 0
    assert b == SUBLANES and seq % (2 * SSM_CHUNK * SSM_RELAYOUT_UNROLL) == 0
    h = x.reshape(b * seq, d).astype(F32)
    for l in range(depth):
        h = _ffn(h, norm_ffn1[l], ffn1_w_gate[l], ffn1_w_up[l], ffn1_w_down[l],
                 final_norm, final_norm=False)
        q, k, v, u = _in_proj(h, norm_mix[l], w_in[l], b, seq)
        attn = _attention(q.reshape(b, seq, -1), k.reshape(b, seq, -1), v.reshape(b, seq, -1),
                          attn_sinks[l])
        ssm_pre = _ssm(u, seq, ssm_lambda_re[l], ssm_lambda_im[l], ssm_log_dt[l],
                       ssm_b_re[l], ssm_b_im[l], ssm_c_re[l], ssm_c_im[l], ssm_d[l])
        h = _mix_out(h, attn.reshape(b * seq, -1), ssm_pre, seq, ssm_glu_w[l], ssm_glu_b[l],
                     attn_out_norm[l], ssm_out_norm[l], w_out[l])
        h = _ffn(h, norm_ffn2[l], ffn2_w_gate[l], ffn2_w_up[l], ffn2_w_down[l],
                 final_norm, final_norm=(l == depth - 1))
    return h.reshape(b, seq, d).astype(x.dtype)
```

```python
import functools

import jax
import jax.numpy as jnp
from jax import lax
from jax.experimental import pallas as pl
from jax.experimental.pallas import tpu as pltpu

F32 = jnp.float32
BF16 = jnp.bfloat16

D_MODEL = 1024
ATTN_HEADS = 8
ATTN_KV_HEADS = 2
Q_PER_KV = ATTN_HEADS // ATTN_KV_HEADS
HEAD_DIM = 64
ATTN_WIDTH = ATTN_HEADS * HEAD_DIM
KV_WIDTH = ATTN_KV_HEADS * HEAD_DIM
WINDOW = 128
BLOCK = 128
SSM_CH = 16
SSM_WIDTH = D_MODEL - ATTN_WIDTH
SSM_GROUPS = SSM_WIDTH // SSM_CH
SSM_STATE = 64
IN_WIDTH = ATTN_WIDTH + 2 * KV_WIDTH + SSM_WIDTH
D_FF = 2816
EPS = 1e-6
NEG_INF = -1e30
LAMBDA_RE_MAX = -1e-4

LANES = 128
SUBLANES = 8
VMEM_LIMIT_BYTES = 56 * 1024 * 1024

TOKEN_TILE = 1024
FFN_TOKEN_TILE = 512
FF_TILE = 256
ATTN_BLOCKS_PER_STEP = 4
SSM_CHUNK = 16
CHUNK_WIDTH = SSM_CHUNK * SSM_CH
GROUPS_PER_BLOCK = LANES // SSM_CH
SSM_INTERLEAVE = 4
SSM_RELAYOUT_UNROLL = 4
OPS_GROUPS_PER_STEP = 4

NT_DIMS = (((1,), (1,)), ((), ()))
TN_DIMS = (((0,), (0,)), ((), ()))


def _rms(x):
    return x * lax.rsqrt(jnp.mean(x * x, axis=-1, keepdims=True) + EPS)


def _ffn_kernel(x_ref, gain_ref, fgain_ref, wg_hbm, wu_hbm, wd_hbm, o_ref,
                wg_ref, wu_ref, wd_ref, act_ref, gu_stage, d_stage, sem, *, final_norm):
    nj = D_FF // FF_TILE

    def weight_copies(j, slot):
        span = pl.ds(j * FF_TILE, FF_TILE)
        return (pltpu.make_async_copy(wg_hbm.at[:, span], gu_stage.at[0, slot], sem.at[0, slot]),
                pltpu.make_async_copy(wu_hbm.at[:, span], gu_stage.at[1, slot], sem.at[1, slot]),
                pltpu.make_async_copy(wd_hbm.at[span, :], d_stage.at[slot], sem.at[2, slot]))

    def step(stage_weights):
        if stage_weights:
            for copy in weight_copies(0, 0):
                copy.start()
        x = x_ref[...]
        hn = (_rms(x) * gain_ref[...]).astype(BF16)
        for j in range(nj):
            cols = slice(j * FF_TILE, (j + 1) * FF_TILE)
            if stage_weights:
                slot = j % 2
                if j + 1 < nj:
                    for copy in weight_copies(j + 1, 1 - slot):
                        copy.start()
                for copy in weight_copies(j, slot):
                    copy.wait()
                wg_ref[:, cols] = gu_stage[0, slot].astype(BF16)
                wu_ref[:, cols] = gu_stage[1, slot].astype(BF16)
                wd_ref[cols, :] = d_stage[slot].astype(BF16)
            g = jnp.dot(hn, wg_ref[:, cols], preferred_element_type=F32)
            u = jnp.dot(hn, wu_ref[:, cols], preferred_element_type=F32)
            act_ref[:, cols] = (g * jax.nn.sigmoid(g) * u).astype(BF16)
        y = x + 0.5 * jnp.dot(act_ref[...], wd_ref[...], preferred_element_type=F32)
        if final_norm:
            y = _rms(y) * fgain_ref[...]
        o_ref[...] = y

    pl.when(pl.program_id(0) == 0)(functools.partial(step, True))
    pl.when(pl.program_id(0) != 0)(functools.partial(step, False))


def _ffn(x, gain, w_gate, w_up, w_down, final_gain, final_norm):
    t = x.shape[0]
    vec = pl.BlockSpec((1, D_MODEL), lambda i: (0, 0))
    hbm = pl.BlockSpec(memory_space=pl.ANY)
    return pl.pallas_call(
        functools.partial(_ffn_kernel, final_norm=final_norm),
        grid=(t // FFN_TOKEN_TILE,),
        in_specs=[pl.BlockSpec((FFN_TOKEN_TILE, D_MODEL), lambda i: (i, 0)), vec, vec,
                  hbm, hbm, hbm],
        out_specs=pl.BlockSpec((FFN_TOKEN_TILE, D_MODEL), lambda i: (i, 0)),
        out_shape=jax.ShapeDtypeStruct((t, D_MODEL), F32),
        scratch_shapes=[pltpu.VMEM((D_MODEL, D_FF), BF16),
                        pltpu.VMEM((D_MODEL, D_FF), BF16),
                        pltpu.VMEM((D_FF, D_MODEL), BF16),
                        pltpu.VMEM((FFN_TOKEN_TILE, D_FF), BF16),
                        pltpu.VMEM((2, 2, D_MODEL, FF_TILE), F32),
                        pltpu.VMEM((2, FF_TILE, D_MODEL), F32),
                        pltpu.SemaphoreType.DMA((3, 2))],
        compiler_params=pltpu.CompilerParams(
            dimension_semantics=("arbitrary",), vmem_limit_bytes=VMEM_LIMIT_BYTES),
    )(x, gain.reshape(1, D_MODEL), final_gain.reshape(1, D_MODEL), w_gate, w_up, w_down)


def _in_proj_kernel(x_ref, gain_ref, w_ref, q_ref, k_ref, v_ref, u_ref, *, seq):
    j = pl.program_id(1)
    hn = (_rms(x_ref[...]) * gain_ref[...]).astype(BF16)
    proj = jnp.dot(hn, w_ref[...].astype(BF16), preferred_element_type=F32)
    q_ref[...] = proj[:, :ATTN_WIDTH].astype(BF16)
    k_ref[...] = proj[:, ATTN_WIDTH:ATTN_WIDTH + KV_WIDTH].astype(BF16)
    v_ref[...] = proj[:, ATTN_WIDTH + KV_WIDTH:ATTN_WIDTH + 2 * KV_WIDTH].astype(BF16)
    rows = pl.ds(pl.multiple_of(j * TOKEN_TILE, TOKEN_TILE), TOKEN_TILE)
    u_ref[rows, :] = proj[:, ATTN_WIDTH + 2 * KV_WIDTH:]
    pitch = u_ref.shape[0]
    if pitch > seq:
        @pl.when(j == 0)
        def _():
            u_ref[seq:pitch, :] = jnp.zeros((pitch - seq, SSM_WIDTH), F32)


def _padded_seq(seq):
    return seq + SUBLANES if seq % (2 * SUBLANES) == 0 else seq


def _batch_rows(seq, width):
    tiles = seq // TOKEN_TILE
    return pl.BlockSpec((None, TOKEN_TILE, width), lambda i: (i // tiles, i % tiles, 0))


def _in_proj(x, gain, w_in, batch, seq):
    t = x.shape[0]
    tiles = seq // TOKEN_TILE
    pitch = _padded_seq(seq)
    row = lambda width: pl.BlockSpec((TOKEN_TILE, width), lambda b, j: (b * tiles + j, 0))
    return pl.pallas_call(
        functools.partial(_in_proj_kernel, seq=seq),
        grid=(batch, tiles),
        in_specs=[
            row(D_MODEL),
            pl.BlockSpec((1, D_MODEL), lambda b, j: (0, 0)),
            pl.BlockSpec((D_MODEL, IN_WIDTH), lambda b, j: (0, 0)),
        ],
        out_specs=[row(ATTN_WIDTH), row(KV_WIDTH), row(KV_WIDTH),
                   pl.BlockSpec((None, pitch, SSM_WIDTH), lambda b, j: (b, 0, 0))],
        out_shape=[
            jax.ShapeDtypeStruct((t, ATTN_WIDTH), BF16),
            jax.ShapeDtypeStruct((t, KV_WIDTH), BF16),
            jax.ShapeDtypeStruct((t, KV_WIDTH), BF16),
            jax.ShapeDtypeStruct((batch, pitch, SSM_WIDTH), F32),
        ],
        compiler_params=pltpu.CompilerParams(
            dimension_semantics=("parallel", "arbitrary"), vmem_limit_bytes=VMEM_LIMIT_BYTES),
    )(x, gain.reshape(1, D_MODEL), w_in)


def _attn_kernel(sink_ref, q_ref, kp_ref, kc_ref, kn_ref, vp_ref, vc_ref, vn_ref, o_ref,
                 bias_ref, k_ref, v_ref, s_ref):
    n = pl.program_id(1)
    last = pl.num_programs(1) - 1
    nblk = ATTN_BLOCKS_PER_STEP

    @pl.when(n == 0)
    def _():
        kj = lax.broadcasted_iota(jnp.int32, (3 * BLOCK, BLOCK), 0)
        qi = lax.broadcasted_iota(jnp.int32, (3 * BLOCK, BLOCK), 1)
        rel = jnp.abs(kj - BLOCK - qi)
        dist = rel.astype(F32)
        inside = rel <= WINDOW
        has_prev = kj >= BLOCK
        has_next = kj < 2 * BLOCK
        for variant, ok in enumerate((inside & has_prev, inside, inside & has_next)):
            for h in range(ATTN_HEADS):
                slope = float(2.0 ** (-8.0 * (h + 1) / ATTN_HEADS))
                bias_ref[variant, h] = jnp.where(ok, -slope * dist, NEG_INF)

    k_ref[0:BLOCK] = kp_ref[...]
    k_ref[BLOCK:(nblk + 1) * BLOCK] = kc_ref[...]
    k_ref[(nblk + 1) * BLOCK:(nblk + 2) * BLOCK] = kn_ref[...]
    v_ref[0:BLOCK] = vp_ref[...]
    v_ref[BLOCK:(nblk + 1) * BLOCK] = vc_ref[...]
    v_ref[(nblk + 1) * BLOCK:(nblk + 2) * BLOCK] = vn_ref[...]

    for j in range(nblk):
        for kh in range(ATTN_KV_HEADS):
            kcat = k_ref[j * BLOCK:(j + 3) * BLOCK, kh * HEAD_DIM:(kh + 1) * HEAD_DIM]
            heads = [kh * Q_PER_KV + g for g in range(Q_PER_KV)]
            qs = jnp.concatenate(
                [q_ref[j * BLOCK:(j + 1) * BLOCK, h * HEAD_DIM:(h + 1) * HEAD_DIM] for h in heads],
                axis=0)
            s_ref[j, kh] = lax.dot_general(kcat, qs * (HEAD_DIM ** -0.5), NT_DIMS,
                                           preferred_element_type=F32)

    for j in range(nblk):
        variant = 1
        if j == 0:
            variant = jnp.where(n == 0, 0, variant)
        if j == nblk - 1:
            variant = jnp.where(n == last, 2, variant)
        outs = []
        for kh in range(ATTN_KV_HEADS):
            vcat = v_ref[j * BLOCK:(j + 3) * BLOCK, kh * HEAD_DIM:(kh + 1) * HEAD_DIM]
            for g in range(Q_PER_KV):
                h = kh * Q_PER_KV + g
                s = s_ref[j, kh, :, g * BLOCK:(g + 1) * BLOCK] + bias_ref[variant, h]
                sink = sink_ref[h]
                m = jnp.maximum(jnp.max(s, axis=0, keepdims=True), sink)
                e = jnp.exp(s - m)
                den = jnp.sum(e, axis=0, keepdims=True) + jnp.exp(sink - m)
                pv = lax.dot_general(vcat, e.astype(BF16), TN_DIMS,
                                     preferred_element_type=F32)
                outs.append(pv / den)
        o_ref[j * BLOCK:(j + 1) * BLOCK, :] = jnp.concatenate(outs, axis=0).T.astype(o_ref.dtype)


def _attention(q, k, v, sinks):
    b, seq, _ = q.shape
    nblk = ATTN_BLOCKS_PER_STEP
    steps = seq // (nblk * BLOCK)
    nb = seq // BLOCK
    assert seq % (nblk * BLOCK) == 0 and nb >= 2 and HEAD_DIM ** -0.5 == 2.0 ** -3
    edge = lambda f: pl.BlockSpec((None, BLOCK, KV_WIDTH), f)
    body = pl.BlockSpec((None, nblk * BLOCK, KV_WIDTH), lambda bi, n: (bi, n, 0))
    prev = lambda bi, n: (bi, jnp.maximum(n * nblk - 1, 0), 0)
    nxt = lambda bi, n: (bi, jnp.minimum((n + 1) * nblk, nb - 1), 0)
    rows = pl.BlockSpec((None, nblk * BLOCK, ATTN_WIDTH), lambda bi, n: (bi, n, 0))
    return pl.pallas_call(
        _attn_kernel,
        grid=(b, steps),
        in_specs=[pl.BlockSpec(memory_space=pltpu.SMEM), rows,
                  edge(prev), body, edge(nxt), edge(prev), body, edge(nxt)],
        out_specs=rows,
        out_shape=jax.ShapeDtypeStruct((b, seq, ATTN_WIDTH), BF16),
        scratch_shapes=[pltpu.VMEM((3, ATTN_HEADS, 3 * BLOCK, BLOCK), F32),
                        pltpu.VMEM(((nblk + 2) * BLOCK, KV_WIDTH), BF16),
                        pltpu.VMEM(((nblk + 2) * BLOCK, KV_WIDTH), BF16),
                        pltpu.VMEM((nblk, ATTN_KV_HEADS, 3 * BLOCK, Q_PER_KV * BLOCK), F32)],
        compiler_params=pltpu.CompilerParams(
            dimension_semantics=("parallel", "arbitrary"), vmem_limit_bytes=VMEM_LIMIT_BYTES),
    )(sinks.astype(F32), q, k, k, k, v, v, v)


def _dot_nt_split(a, b):
    a_hi, b_hi = a.astype(BF16), b.astype(BF16)
    a_lo = (a - a_hi.astype(F32)).astype(BF16)
    b_lo = (b - b_hi.astype(F32)).astype(BF16)
    dot = functools.partial(lax.dot_general, dimension_numbers=NT_DIMS,
                            preferred_element_type=F32)
    return dot(a_hi, b_hi) + dot(a_hi, b_lo) + dot(a_lo, b_hi)


def _ssm_ops_kernel(prm_ref, bc_ref, ein_ref, toep_ref, eout_ref, aq_ref):
    q, hc, p = SSM_CHUNK, SSM_CH, SSM_STATE
    fwd = lax.broadcasted_iota(jnp.int32, (1, 2 * p), 1) < p
    zero_row = jnp.zeros((1, 2 * p), F32)
    row_id = lax.broadcasted_iota(jnp.int32, (CHUNK_WIDTH, CHUNK_WIDTH), 0)
    col_id = lax.broadcasted_iota(jnp.int32, (CHUNK_WIDTH, CHUNK_WIDTH), 1)

    def table(select, n):
        picks = [select(m) for m in range(n)]
        re = jnp.concatenate([jnp.broadcast_to(r, (hc, 2 * p)) for r, _ in picks], axis=0)
        im = jnp.concatenate([jnp.broadcast_to(i, (hc, 2 * p)) for _, i in picks], axis=0)
        return re, im

    def tile_rows(x, n):
        return jnp.concatenate([x] * n, axis=0)

    for gi in range(prm_ref.shape[0]):
        lr = jnp.minimum(prm_ref[gi, 0:1, :], LAMBDA_RE_MAX)
        li = prm_ref[gi, 1:2, :]
        dt = jnp.exp(prm_ref[gi, 2:3, :])
        mag = jnp.exp(lr * dt)
        a_r = mag * jnp.cos(li * dt)
        a_i = mag * jnp.sin(li * dt)
        den = lr * lr + li * li
        coef_r = ((a_r - 1.0) * lr + a_i * li) / den
        coef_i = (a_i * lr - (a_r - 1.0) * li) / den
        b_r, b_i = bc_ref[gi, 0], bc_ref[gi, 1]
        c_r, c_i = bc_ref[gi, 2], bc_ref[gi, 3]
        bb_r = coef_r * b_r - coef_i * b_i
        bb_i = coef_r * b_i + coef_i * b_r

        pw = [(jnp.ones((1, 2 * p), F32), zero_row)]
        for _ in range(q):
            r, i = pw[-1]
            pw.append((r * a_r - i * a_i, r * a_i + i * a_r))

        def both(f_idx, b_idx):
            fr, fi = pw[f_idx] if f_idx is not None else (zero_row, zero_row)
            br, bi = pw[b_idx] if b_idx is not None else (zero_row, zero_row)
            return jnp.where(fwd, fr, br), jnp.where(fwd, fi, bi)

        p_r, p_i = table(lambda i: both(q - 1 - i, i), q)
        tb_r, tb_i = tile_rows(bb_r, q), tile_rows(bb_i, q)
        ein = jnp.concatenate([tb_r * p_r - tb_i * p_i, tb_r * p_i + tb_i * p_r], axis=1)
        ein_ref[gi] = ein.astype(BF16)

        p_r, p_i = table(lambda j: both(j + 1, q - j), q)
        tc_r, tc_i = tile_rows(c_r, q), tile_rows(c_i, q)
        eout = jnp.concatenate([tc_r * p_r - tc_i * p_i, -(tc_r * p_i + tc_i * p_r)], axis=1)
        eout_ref[gi] = eout.astype(BF16)

        def lag(m):
            return both(m - (q - 1) if q - 1 <= m <= 2 * q - 2 else None,
                        (q - 1) - m if m <= q - 1 else None)

        p_r, p_i = table(lag, 2 * q)
        tc_r, tc_i = tile_rows(c_r, 2 * q), tile_rows(c_i, 2 * q)
        cpt = jnp.concatenate([tc_r * p_r - tc_i * p_i, tc_r * p_i + tc_i * p_r], axis=1)
        bcat = jnp.concatenate([bb_r, -bb_i], axis=1)
        kern = _dot_nt_split(bcat, cpt)
        toep = jnp.concatenate(
            [kern[:, hc * (q - 1 - i):hc * (q - 1 - i) + CHUNK_WIDTH] for i in range(q)], axis=0)
        skip = jnp.concatenate([prm_ref[gi, 3:4, :]] * (CHUNK_WIDTH // (2 * p)), axis=1)
        toep_ref[gi] = (toep + jnp.where(row_id == col_id, skip, 0.0)).astype(BF16)

        aq_ref[gi] = jnp.concatenate(
            [pw[q][0], pw[q][1], jnp.zeros((SUBLANES - 2, 2 * p), F32)], axis=0)


def _ssm_operators(lam_re, lam_im, log_dt, b_re, b_im, c_re, c_im, d_skip):
    g, p, hc = SSM_GROUPS, SSM_STATE, SSM_CH
    lanes = lambda a: a.astype(F32).transpose(1, 0, 2).reshape(g, 2 * p)
    prm = jnp.stack(
        [lanes(lam_re), lanes(lam_im),
         jnp.repeat(log_dt.astype(F32).T, p, axis=1),
         jnp.tile(d_skip.astype(F32), (1, 2 * p // hc))]
        + [jnp.zeros((g, 2 * p), F32)] * (SUBLANES - 4), axis=1)
    bc = jnp.stack(
        [b_re.astype(F32).transpose(1, 3, 0, 2).reshape(g, hc, 2 * p),
         b_im.astype(F32).transpose(1, 3, 0, 2).reshape(g, hc, 2 * p),
         c_re.astype(F32).transpose(1, 2, 0, 3).reshape(g, hc, 2 * p),
         c_im.astype(F32).transpose(1, 2, 0, 3).reshape(g, hc, 2 * p)], axis=1)
    gs = OPS_GROUPS_PER_STEP
    mat = pl.BlockSpec((gs, CHUNK_WIDTH, CHUNK_WIDTH), lambda s: (s, 0, 0))
    mat_shape = jax.ShapeDtypeStruct((g, CHUNK_WIDTH, CHUNK_WIDTH), BF16)
    return pl.pallas_call(
        _ssm_ops_kernel,
        grid=(g // gs,),
        in_specs=[pl.BlockSpec((gs, SUBLANES, 2 * p), lambda s: (s, 0, 0)),
                  pl.BlockSpec((gs, 4, hc, 2 * p), lambda s: (s, 0, 0, 0))],
        out_specs=[mat, mat, mat, pl.BlockSpec((gs, SUBLANES, 2 * p), lambda s: (s, 0, 0))],
        out_shape=[mat_shape, mat_shape, mat_shape,
                   jax.ShapeDtypeStruct((g, SUBLANES, 2 * p), F32)],
        compiler_params=pltpu.CompilerParams(
            dimension_semantics=("parallel",), vmem_limit_bytes=VMEM_LIMIT_BYTES),
    )(prm, bc)


def _lane_roll(x, shift):
    shift %= LANES
    return jnp.concatenate([x[:, LANES - shift:], x[:, :LANES - shift]], axis=1)


def _block_transpose(v, lane_block):
    v = list(v)
    n = len(v)
    d = n // 2
    while d >= 1:
        upper = (lane_block & d) != 0
        for i in range(n):
            if i & d == 0:
                a, b = v[i], v[i + d]
                v[i] = jnp.where(upper, _lane_roll(b, SSM_CH * d), a)
                v[i + d] = jnp.where(upper, b, _lane_roll(a, -SSM_CH * d))
        d //= 2
    return v


def _ssm_kernel(u_ref, ein_ref, toep_ref, eout_ref, aq_ref, y_ref, ug_ref, s_ref, x_ref,
                *, batch, seq, pitch):
    q, half = SSM_CHUNK, SSM_STATE
    n_chunks = seq // q
    gpb = GROUPS_PER_BLOCK
    slab = 2 * batch
    lane_block = lax.broadcasted_iota(jnp.int32, (slab, LANES), 1) // SSM_CH

    for b in range(batch if pitch > seq else 0):
        y_ref[b * pitch + seq:(b + 1) * pitch, :] = jnp.zeros((pitch - seq, LANES), F32)

    def gather(cp, carry):
        rows = pl.ds(pl.multiple_of(cp * slab, slab), slab)
        for part in range(q // gpb):
            t0 = cp * 2 * q + part * gpb
            v = [jnp.concatenate([u_ref[pl.ds(t0 + i, batch, stride=pitch), :],
                                  u_ref[pl.ds(t0 + q + i, batch, stride=pitch), :]],
                                 axis=0).astype(BF16) for i in range(gpb)]
            w = _block_transpose(v, lane_block)
            for g in range(gpb):
                ug_ref[g, rows, part * LANES:(part + 1) * LANES] = w[g]
        return carry

    lax.fori_loop(0, n_chunks // 2, gather, 0, unroll=SSM_RELAYOUT_UNROLL)

    fwd_lane = lax.broadcasted_iota(jnp.int32, (batch, 2 * half), 1) < half
    zeros = jnp.zeros((batch, half), F32)
    last_rows = pl.ds((n_chunks - 1) * batch, batch)
    ni = SSM_INTERLEAVE
    for g0 in range(0, gpb, ni):
        for gi in range(ni):
            s_ref[gi] = jnp.dot(ug_ref[g0 + gi], ein_ref[g0 + gi], preferred_element_type=F32)
            x_ref[gi, 0:batch, 0:half] = zeros
            x_ref[gi, 0:batch, 2 * half:3 * half] = zeros
            x_ref[gi, last_rows, half:2 * half] = zeros
            x_ref[gi, last_rows, 3 * half:4 * half] = zeros

        def step(k, carry):
            rf = pl.multiple_of(k * batch, batch)
            rb = pl.multiple_of((n_chunks - 1 - k) * batch, batch)
            new = []
            for gi in range(ni):
                xr, xi = carry[gi]
                sre = jnp.where(fwd_lane, s_ref[gi, pl.ds(rf, batch), 0:2 * half],
                                s_ref[gi, pl.ds(rb, batch), 0:2 * half])
                sim = jnp.where(fwd_lane, s_ref[gi, pl.ds(rf, batch), 2 * half:4 * half],
                                s_ref[gi, pl.ds(rb, batch), 2 * half:4 * half])
                ar = aq_ref[g0 + gi, 0:1, :]
                ai = aq_ref[g0 + gi, 1:2, :]
                nr = ar * xr - ai * xi + sre
                nim = ar * xi + ai * xr + sim
                x_ref[gi, pl.ds(rf + batch, batch), 0:half] = nr[:, 0:half]
                x_ref[gi, pl.ds(rf + batch, batch), 2 * half:3 * half] = nim[:, 0:half]
                x_ref[gi, pl.ds(rb - batch, batch), half:2 * half] = nr[:, half:2 * half]
                x_ref[gi, pl.ds(rb - batch, batch), 3 * half:4 * half] = nim[:, half:2 * half]
                new.append((nr, nim))
            return tuple(new)

        init = tuple((jnp.zeros((batch, 2 * half), F32), jnp.zeros((batch, 2 * half), F32))
                     for _ in range(ni))
        lax.fori_loop(0, n_chunks - 1, step, init)

        for gi in range(ni):
            g = g0 + gi
            y = jnp.dot(ug_ref[g], toep_ref[g], preferred_element_type=F32)
            y += lax.dot_general(x_ref[gi].astype(BF16), eout_ref[g], NT_DIMS,
                                 preferred_element_type=F32)
            ug_ref[g] = y.astype(BF16)

    def scatter(cp, carry):
        rows = pl.ds(pl.multiple_of(cp * slab, slab), slab)
        for part in range(q // gpb):
            t0 = cp * 2 * q + part * gpb
            v = [ug_ref[g, rows, part * LANES:(part + 1) * LANES] for g in range(gpb)]
            w = _block_transpose(v, lane_block)
            for j in range(gpb):
                wj = w[j].astype(F32)
                y_ref[pl.ds(t0 + j, batch, stride=pitch), :] = wj[0:batch]
                y_ref[pl.ds(t0 + q + j, batch, stride=pitch), :] = wj[batch:slab]
        return carry

    lax.fori_loop(0, n_chunks // 2, scatter, 0, unroll=SSM_RELAYOUT_UNROLL)


def _ssm(u, seq, lam_re, lam_im, log_dt, b_re, b_im, c_re, c_im, d_skip):
    batch, pitch, _ = u.shape
    rows = batch * seq // SSM_CHUNK
    ein, toep, eout, a_q = _ssm_operators(lam_re, lam_im, log_dt, b_re, b_im, c_re, c_im, d_skip)
    gpb = GROUPS_PER_BLOCK
    mat = pl.BlockSpec((gpb, CHUNK_WIDTH, CHUNK_WIDTH), lambda s: (s, 0, 0))
    col = pl.BlockSpec((batch * pitch, LANES), lambda s: (0, s), pipeline_mode=pl.Buffered(1))
    y = pl.pallas_call(
        functools.partial(_ssm_kernel, batch=batch, seq=seq, pitch=pitch),
        grid=(SSM_GROUPS // gpb,),
        in_specs=[col, mat, mat, mat,
                  pl.BlockSpec((gpb, SUBLANES, 2 * SSM_STATE), lambda s: (s, 0, 0))],
        out_specs=col,
        out_shape=jax.ShapeDtypeStruct((batch * pitch, SSM_WIDTH), F32),
        scratch_shapes=[pltpu.VMEM((gpb, rows, CHUNK_WIDTH), BF16),
                        pltpu.VMEM((SSM_INTERLEAVE, rows, 4 * SSM_STATE), F32),
                        pltpu.VMEM((SSM_INTERLEAVE, rows, 4 * SSM_STATE), F32)],
        compiler_params=pltpu.CompilerParams(
            dimension_semantics=("parallel",), vmem_limit_bytes=VMEM_LIMIT_BYTES),
    )(u.reshape(batch * pitch, SSM_WIDTH), ein, toep, eout, a_q)
    return y.reshape(batch, pitch, SSM_WIDTH)


def _mix_kernel(x_ref, attn_ref, ssm_ref, gw_ref, gb_ref, ga_ref, gs_ref, wo_ref, o_ref):
    y = jax.nn.gelu(ssm_ref[...])
    z = jnp.dot(y.astype(BF16), gw_ref[...].astype(BF16), preferred_element_type=F32) + gb_ref[...]
    s = y * jax.nn.sigmoid(z)
    sn = _rms(s) * gs_ref[...]
    an = _rms(attn_ref[...].astype(F32)) * ga_ref[...]
    mixed = jnp.concatenate([an, sn], axis=-1).astype(BF16)
    o_ref[...] = x_ref[...] + jnp.dot(mixed, wo_ref[...].astype(BF16), preferred_element_type=F32)


def _mix_out(x, attn, ssm_pre, seq, glu_w, glu_b, attn_gain, ssm_gain, w_out):
    t = x.shape[0]
    row = lambda width: pl.BlockSpec((TOKEN_TILE, width), lambda i: (i, 0))
    full = lambda r, c: pl.BlockSpec((r, c), lambda i: (0, 0))
    return pl.pallas_call(
        _mix_kernel,
        grid=(t // TOKEN_TILE,),
        in_specs=[row(D_MODEL), row(ATTN_WIDTH), _batch_rows(seq, SSM_WIDTH),
                  full(SSM_WIDTH, SSM_WIDTH), full(1, SSM_WIDTH),
                  full(1, ATTN_WIDTH), full(1, SSM_WIDTH), full(D_MODEL, D_MODEL)],
        out_specs=row(D_MODEL),
        out_shape=jax.ShapeDtypeStruct((t, D_MODEL), F32),
        compiler_params=pltpu.CompilerParams(
            dimension_semantics=("parallel",), vmem_limit_bytes=VMEM_LIMIT_BYTES),
    )(x, attn, ssm_pre, glu_w, glu_b.reshape(1, -1).astype(F32),
      attn_gain.reshape(1, -1).astype(F32), ssm_gain.reshape(1, -1).astype(F32), w_out)


def kernel(x, norm_ffn1, ffn1_w_gate, ffn1_w_up, ffn1_w_down, norm_mix, w_in, attn_sinks,
           ssm_lambda_re, ssm_lambda_im, ssm_log_dt, ssm_b_re, ssm_b_im, ssm_c_re, ssm_c_im,
           ssm_d, ssm_glu_w, ssm_glu_b, attn_out_norm, ssm_out_norm, w_out,
           norm_ffn2, ffn2_w_gate, ffn2_w_up, ffn2_w_down, final_norm):
    b, seq, d = x.shape
    depth = norm_ffn1.shape[0]
    assert d == D_MODEL and seq % BLOCK == 0 and seq % TOKEN_TILE == 0
    assert b == SUBLANES and seq % (2 * SSM_CHUNK * SSM_RELAYOUT_UNROLL) == 0
    h = x.reshape(b * seq, d).astype(F32)
    for l in range(depth):
        h = _ffn(h, norm_ffn1[l], ffn1_w_gate[l], ffn1_w_up[l], ffn1_w_down[l],
                 final_norm, final_norm=False)
        q, k, v, u = _in_proj(h, norm_mix[l], w_in[l], b, seq)
        attn = _attention(q.reshape(b, seq, -1), k.reshape(b, seq, -1), v.reshape(b, seq, -1),
                          attn_sinks[l])
        ssm_pre = _ssm(u, seq, ssm_lambda_re[l], ssm_lambda_im[l], ssm_log_dt[l],
                       ssm_b_re[l], ssm_b_im[l], ssm_c_re[l], ssm_c_im[l], ssm_d[l])
        h = _mix_out(h, attn.reshape(b * seq, -1), ssm_pre, seq, ssm_glu_w[l], ssm_glu_b[l],
                     attn_out_norm[l], ssm_out_norm[l], w_out[l])
        h = _ffn(h, norm_ffn2[l], ffn2_w_gate[l], ffn2_w_up[l], ffn2_w_down[l],
                 final_norm, final_norm=(l == depth - 1))
    return h.reshape(b, seq, d).astype(x.dtype)
```

```python
import functools

import jax
import jax.numpy as jnp
from jax import lax
from jax.experimental import pallas as pl
from jax.experimental.pallas import tpu as pltpu

F32 = jnp.float32
BF16 = jnp.bfloat16

D_MODEL = 1024
ATTN_HEADS = 8
ATTN_KV_HEADS = 2
Q_PER_KV = ATTN_HEADS // ATTN_KV_HEADS
HEAD_DIM = 64
ATTN_WIDTH = ATTN_HEADS * HEAD_DIM
KV_WIDTH = ATTN_KV_HEADS * HEAD_DIM
WINDOW = 128
BLOCK = 128
SSM_CH = 16
SSM_WIDTH = D_MODEL - ATTN_WIDTH
SSM_GROUPS = SSM_WIDTH // SSM_CH
SSM_STATE = 64
IN_WIDTH = ATTN_WIDTH + 2 * KV_WIDTH + SSM_WIDTH
D_FF = 2816
EPS = 1e-6
NEG_INF = -1e30
LAMBDA_RE_MAX = -1e-4

LANES = 128
SUBLANES = 8
VMEM_LIMIT_BYTES = 56 * 1024 * 1024

TOKEN_TILE = 1024
FFN_TOKEN_TILE = 1024
FF_TILE = 256
ATTN_BLOCKS_PER_STEP = 4
SSM_CHUNK = 16
CHUNK_WIDTH = SSM_CHUNK * SSM_CH
GROUPS_PER_BLOCK = LANES // SSM_CH
SSM_INTERLEAVE = 4
SSM_RELAYOUT_UNROLL = 4
OPS_GROUPS_PER_STEP = 4

NT_DIMS = (((1,), (1,)), ((), ()))
TN_DIMS = (((0,), (0,)), ((), ()))


def _rms(x):
    return x * lax.rsqrt(jnp.mean(x * x, axis=-1, keepdims=True) + EPS)


def _ffn_kernel(x_ref, gain_ref, fgain_ref, wg_hbm, wu_hbm, wd_hbm, o_ref,
                wg_ref, wu_ref, wd_ref, act_ref, gu_stage, d_stage, sem, *, final_norm):
    nj = D_FF // FF_TILE

    def weight_copies(j, slot):
        span = pl.ds(j * FF_TILE, FF_TILE)
        return (pltpu.make_async_copy(wg_hbm.at[:, span], gu_stage.at[0, slot], sem.at[0, slot]),
                pltpu.make_async_copy(wu_hbm.at[:, span], gu_stage.at[1, slot], sem.at[1, slot]),
                pltpu.make_async_copy(wd_hbm.at[span, :], d_stage.at[slot], sem.at[2, slot]))

    def step(stage_weights):
        if stage_weights:
            for copy in weight_copies(0, 0):
                copy.start()
        x = x_ref[...]
        hn = (_rms(x) * gain_ref[...]).astype(BF16)
        for j in range(nj):
            cols = slice(j * FF_TILE, (j + 1) * FF_TILE)
            if stage_weights:
                slot = j % 2
                if j + 1 < nj:
                    for copy in weight_copies(j + 1, 1 - slot):
                        copy.start()
                for copy in weight_copies(j, slot):
                    copy.wait()
                wg_ref[:, cols] = gu_stage[0, slot].astype(BF16)
                wu_ref[:, cols] = gu_stage[1, slot].astype(BF16)
                wd_ref[cols, :] = d_stage[slot].astype(BF16)
            g = jnp.dot(hn, wg_ref[:, cols], preferred_element_type=F32)
            u = jnp.dot(hn, wu_ref[:, cols], preferred_element_type=F32)
            act_ref[:, cols] = (g * jax.nn.sigmoid(g) * u).astype(BF16)
        y = x + 0.5 * jnp.dot(act_ref[...], wd_ref[...], preferred_element_type=F32)
        if final_norm:
            y = _rms(y) * fgain_ref[...]
        o_ref[...] = y

    pl.when(pl.program_id(0) == 0)(functools.partial(step, True))
    pl.when(pl.program_id(0) != 0)(functools.partial(step, False))


def _ffn(x, gain, w_gate, w_up, w_down, final_gain, final_norm):
    t = x.shape[0]
    vec = pl.BlockSpec((1, D_MODEL), lambda i: (0, 0))
    hbm = pl.BlockSpec(memory_space=pl.ANY)
    return pl.pallas_call(
        functools.partial(_ffn_kernel, final_norm=final_norm),
        grid=(t // FFN_TOKEN_TILE,),
        in_specs=[pl.BlockSpec((FFN_TOKEN_TILE, D_MODEL), lambda i: (i, 0)), vec, vec,
                  hbm, hbm, hbm],
        out_specs=pl.BlockSpec((FFN_TOKEN_TILE, D_MODEL), lambda i: (i, 0)),
        out_shape=jax.ShapeDtypeStruct((t, D_MODEL), F32),
        scratch_shapes=[pltpu.VMEM((D_MODEL, D_FF), BF16),
                        pltpu.VMEM((D_MODEL, D_FF), BF16),
                        pltpu.VMEM((D_FF, D_MODEL), BF16),
                        pltpu.VMEM((FFN_TOKEN_TILE, D_FF), BF16),
                        pltpu.VMEM((2, 2, D_MODEL, FF_TILE), F32),
                        pltpu.VMEM((2, FF_TILE, D_MODEL), F32),
                        pltpu.SemaphoreType.DMA((3, 2))],
        compiler_params=pltpu.CompilerParams(
            dimension_semantics=("arbitrary",), vmem_limit_bytes=VMEM_LIMIT_BYTES),
    )(x, gain.reshape(1, D_MODEL), final_gain.reshape(1, D_MODEL), w_gate, w_up, w_down)


def _in_proj_kernel(x_ref, gain_ref, w_ref, q_ref, k_ref, v_ref, u_ref, *, seq):
    j = pl.program_id(1)
    hn = (_rms(x_ref[...]) * gain_ref[...]).astype(BF16)
    proj = jnp.dot(hn, w_ref[...].astype(BF16), preferred_element_type=F32)
    q_ref[...] = proj[:, :ATTN_WIDTH].astype(BF16)
    k_ref[...] = proj[:, ATTN_WIDTH:ATTN_WIDTH + KV_WIDTH].astype(BF16)
    v_ref[...] = proj[:, ATTN_WIDTH + KV_WIDTH:ATTN_WIDTH + 2 * KV_WIDTH].astype(BF16)
    rows = pl.ds(pl.multiple_of(j * TOKEN_TILE, TOKEN_TILE), TOKEN_TILE)
    u_ref[rows, :] = proj[:, ATTN_WIDTH + 2 * KV_WIDTH:]
    pitch = u_ref.shape[0]
    if pitch > seq:
        @pl.when(j == 0)
        def _():
            u_ref[seq:pitch, :] = jnp.zeros((pitch - seq, SSM_WIDTH), F32)


def _padded_seq(seq):
    return seq + SUBLANES if seq % (2 * SUBLANES) == 0 else seq


def _batch_rows(seq, width):
    tiles = seq // TOKEN_TILE
    return pl.BlockSpec((None, TOKEN_TILE, width), lambda i: (i // tiles, i % tiles, 0))


def _in_proj(x, gain, w_in, batch, seq):
    t = x.shape[0]
    tiles = seq // TOKEN_TILE
    pitch = _padded_seq(seq)
    row = lambda width: pl.BlockSpec((TOKEN_TILE, width), lambda b, j: (b * tiles + j, 0))
    return pl.pallas_call(
        functools.partial(_in_proj_kernel, seq=seq),
        grid=(batch, tiles),
        in_specs=[
            row(D_MODEL),
            pl.BlockSpec((1, D_MODEL), lambda b, j: (0, 0)),
            pl.BlockSpec((D_MODEL, IN_WIDTH), lambda b, j: (0, 0)),
        ],
        out_specs=[row(ATTN_WIDTH), row(KV_WIDTH), row(KV_WIDTH),
                   pl.BlockSpec((None, pitch, SSM_WIDTH), lambda b, j: (b, 0, 0))],
        out_shape=[
            jax.ShapeDtypeStruct((t, ATTN_WIDTH), BF16),
            jax.ShapeDtypeStruct((t, KV_WIDTH), BF16),
            jax.ShapeDtypeStruct((t, KV_WIDTH), BF16),
            jax.ShapeDtypeStruct((batch, pitch, SSM_WIDTH), F32),
        ],
        compiler_params=pltpu.CompilerParams(
            dimension_semantics=("parallel", "arbitrary"), vmem_limit_bytes=VMEM_LIMIT_BYTES),
    )(x, gain.reshape(1, D_MODEL), w_in)


def _attn_kernel(sink_ref, q_ref, kp_ref, kc_ref, kn_ref, vp_ref, vc_ref, vn_ref, o_ref,
                 bias_ref, k_ref, v_ref, s_ref):
    n = pl.program_id(1)
    last = pl.num_programs(1) - 1
    nblk = ATTN_BLOCKS_PER_STEP

    @pl.when(n == 0)
    def _():
        kj = lax.broadcasted_iota(jnp.int32, (3 * BLOCK, BLOCK), 0)
        qi = lax.broadcasted_iota(jnp.int32, (3 * BLOCK, BLOCK), 1)
        rel = jnp.abs(kj - BLOCK - qi)
        dist = rel.astype(F32)
        inside = rel <= WINDOW
        has_prev = kj >= BLOCK
        has_next = kj < 2 * BLOCK
        for variant, ok in enumerate((inside & has_prev, inside, inside & has_next)):
            for h in range(ATTN_HEADS):
                slope = float(2.0 ** (-8.0 * (h + 1) / ATTN_HEADS))
                bias_ref[variant, h] = jnp.where(ok, -slope * dist, NEG_INF)

    k_ref[0:BLOCK] = kp_ref[...]
    k_ref[BLOCK:(nblk + 1) * BLOCK] = kc_ref[...]
    k_ref[(nblk + 1) * BLOCK:(nblk + 2) * BLOCK] = kn_ref[...]
    v_ref[0:BLOCK] = vp_ref[...]
    v_ref[BLOCK:(nblk + 1) * BLOCK] = vc_ref[...]
    v_ref[(nblk + 1) * BLOCK:(nblk + 2) * BLOCK] = vn_ref[...]

    for j in range(nblk):
        for kh in range(ATTN_KV_HEADS):
            kcat = k_ref[j * BLOCK:(j + 3) * BLOCK, kh * HEAD_DIM:(kh + 1) * HEAD_DIM]
            heads = [kh * Q_PER_KV + g for g in range(Q_PER_KV)]
            qs = jnp.concatenate(
                [q_ref[j * BLOCK:(j + 1) * BLOCK, h * HEAD_DIM:(h + 1) * HEAD_DIM] for h in heads],
                axis=0)
            s_ref[j, kh] = lax.dot_general(kcat, qs * (HEAD_DIM ** -0.5), NT_DIMS,
                                           preferred_element_type=F32)

    for j in range(nblk):
        variant = 1
        if j == 0:
            variant = jnp.where(n == 0, 0, variant)
        if j == nblk - 1:
            variant = jnp.where(n == last, 2, variant)
        outs = []
        for kh in range(ATTN_KV_HEADS):
            vcat = v_ref[j * BLOCK:(j + 3) * BLOCK, kh * HEAD_DIM:(kh + 1) * HEAD_DIM]
            for g in range(Q_PER_KV):
                h = kh * Q_PER_KV + g
                s = s_ref[j, kh, :, g * BLOCK:(g + 1) * BLOCK] + bias_ref[variant, h]
                sink = sink_ref[h]
                m = jnp.maximum(jnp.max(s, axis=0, keepdims=True), sink)
                e = jnp.exp(s - m)
                den = jnp.sum(e, axis=0, keepdims=True) + jnp.exp(sink - m)
                pv = lax.dot_general(vcat, e.astype(BF16), TN_DIMS,
                                     preferred_element_type=F32)
                outs.append(pv / den)
        o_ref[j * BLOCK:(j + 1) * BLOCK, :] = jnp.concatenate(outs, axis=0).T.astype(o_ref.dtype)


def _attention(q, k, v, sinks):
    b, seq, _ = q.shape
    nblk = ATTN_BLOCKS_PER_STEP
    steps = seq // (nblk * BLOCK)
    nb = seq // BLOCK
    assert seq % (nblk * BLOCK) == 0 and nb >= 2 and HEAD_DIM ** -0.5 == 2.0 ** -3
    edge = lambda f: pl.BlockSpec((None, BLOCK, KV_WIDTH), f)
    body = pl.BlockSpec((None, nblk * BLOCK, KV_WIDTH), lambda bi, n: (bi, n, 0))
    prev = lambda bi, n: (bi, jnp.maximum(n * nblk - 1, 0), 0)
    nxt = lambda bi, n: (bi, jnp.minimum((n + 1) * nblk, nb - 1), 0)
    rows = pl.BlockSpec((None, nblk * BLOCK, ATTN_WIDTH), lambda bi, n: (bi, n, 0))
    return pl.pallas_call(
        _attn_kernel,
        grid=(b, steps),
        in_specs=[pl.BlockSpec(memory_space=pltpu.SMEM), rows,
                  edge(prev), body, edge(nxt), edge(prev), body, edge(nxt)],
        out_specs=rows,
        out_shape=jax.ShapeDtypeStruct((b, seq, ATTN_WIDTH), BF16),
        scratch_shapes=[pltpu.VMEM((3, ATTN_HEADS, 3 * BLOCK, BLOCK), F32),
                        pltpu.VMEM(((nblk + 2) * BLOCK, KV_WIDTH), BF16),
                        pltpu.VMEM(((nblk + 2) * BLOCK, KV_WIDTH), BF16),
                        pltpu.VMEM((nblk, ATTN_KV_HEADS, 3 * BLOCK, Q_PER_KV * BLOCK), F32)],
        compiler_params=pltpu.CompilerParams(
            dimension_semantics=("parallel", "arbitrary"), vmem_limit_bytes=VMEM_LIMIT_BYTES),
    )(sinks.astype(F32), q, k, k, k, v, v, v)


def _dot_nt_split(a, b):
    a_hi, b_hi = a.astype(BF16), b.astype(BF16)
    a_lo = (a - a_hi.astype(F32)).astype(BF16)
    b_lo = (b - b_hi.astype(F32)).astype(BF16)
    dot = functools.partial(lax.dot_general, dimension_numbers=NT_DIMS,
                            preferred_element_type=F32)
    return dot(a_hi, b_hi) + dot(a_hi, b_lo) + dot(a_lo, b_hi)


def _ssm_ops_kernel(prm_ref, bc_ref, ein_ref, toep_ref, eout_ref, aq_ref):
    q, hc, p = SSM_CHUNK, SSM_CH, SSM_STATE
    fwd = lax.broadcasted_iota(jnp.int32, (1, 2 * p), 1) < p
    zero_row = jnp.zeros((1, 2 * p), F32)
    row_id = lax.broadcasted_iota(jnp.int32, (CHUNK_WIDTH, CHUNK_WIDTH), 0)
    col_id = lax.broadcasted_iota(jnp.int32, (CHUNK_WIDTH, CHUNK_WIDTH), 1)

    def table(select, n):
        picks = [select(m) for m in range(n)]
        re = jnp.concatenate([jnp.broadcast_to(r, (hc, 2 * p)) for r, _ in picks], axis=0)
        im = jnp.concatenate([jnp.broadcast_to(i, (hc, 2 * p)) for _, i in picks], axis=0)
        return re, im

    def tile_rows(x, n):
        return jnp.concatenate([x] * n, axis=0)

    for gi in range(prm_ref.shape[0]):
        lr = jnp.minimum(prm_ref[gi, 0:1, :], LAMBDA_RE_MAX)
        li = prm_ref[gi, 1:2, :]
        dt = jnp.exp(prm_ref[gi, 2:3, :])
        mag = jnp.exp(lr * dt)
        a_r = mag * jnp.cos(li * dt)
        a_i = mag * jnp.sin(li * dt)
        den = lr * lr + li * li
        coef_r = ((a_r - 1.0) * lr + a_i * li) / den
        coef_i = (a_i * lr - (a_r - 1.0) * li) / den
        b_r, b_i = bc_ref[gi, 0], bc_ref[gi, 1]
        c_r, c_i = bc_ref[gi, 2], bc_ref[gi, 3]
        bb_r = coef_r * b_r - coef_i * b_i
        bb_i = coef_r * b_i + coef_i * b_r

        pw = [(jnp.ones((1, 2 * p), F32), zero_row)]
        for _ in range(q):
            r, i = pw[-1]
            pw.append((r * a_r - i * a_i, r * a_i + i * a_r))

        def both(f_idx, b_idx):
            fr, fi = pw[f_idx] if f_idx is not None else (zero_row, zero_row)
            br, bi = pw[b_idx] if b_idx is not None else (zero_row, zero_row)
            return jnp.where(fwd, fr, br), jnp.where(fwd, fi, bi)

        p_r, p_i = table(lambda i: both(q - 1 - i, i), q)
        tb_r, tb_i = tile_rows(bb_r, q), tile_rows(bb_i, q)
        ein = jnp.concatenate([tb_r * p_r - tb_i * p_i, tb_r * p_i + tb_i * p_r], axis=1)
        ein_ref[gi] = ein.astype(BF16)

        p_r, p_i = table(lambda j: both(j + 1, q - j), q)
        tc_r, tc_i = tile_rows(c_r, q), tile_rows(c_i, q)
        eout = jnp.concatenate([tc_r * p_r - tc_i * p_i, -(tc_r * p_i + tc_i * p_r)], axis=1)
        eout_ref[gi] = eout.astype(BF16)

        def lag(m):
            return both(m - (q - 1) if q - 1 <= m <= 2 * q - 2 else None,
                        (q - 1) - m if m <= q - 1 else None)

        p_r, p_i = table(lag, 2 * q)
        tc_r, tc_i = tile_rows(c_r, 2 * q), tile_rows(c_i, 2 * q)
        cpt = jnp.concatenate([tc_r * p_r - tc_i * p_i, tc_r * p_i + tc_i * p_r], axis=1)
        bcat = jnp.concatenate([bb_r, -bb_i], axis=1)
        kern = _dot_nt_split(bcat, cpt)
        toep = jnp.concatenate(
            [kern[:, hc * (q - 1 - i):hc * (q - 1 - i) + CHUNK_WIDTH] for i in range(q)], axis=0)
        skip = jnp.concatenate([prm_ref[gi, 3:4, :]] * (CHUNK_WIDTH // (2 * p)), axis=1)
        toep_ref[gi] = (toep + jnp.where(row_id == col_id, skip, 0.0)).astype(BF16)

        aq_ref[gi] = jnp.concatenate(
            [pw[q][0], pw[q][1], jnp.zeros((SUBLANES - 2, 2 * p), F32)], axis=0)


def _ssm_operators(lam_re, lam_im, log_dt, b_re, b_im, c_re, c_im, d_skip):
    g, p, hc = SSM_GROUPS, SSM_STATE, SSM_CH
    lanes = lambda a: a.astype(F32).transpose(1, 0, 2).reshape(g, 2 * p)
    prm = jnp.stack(
        [lanes(lam_re), lanes(lam_im),
         jnp.repeat(log_dt.astype(F32).T, p, axis=1),
         jnp.tile(d_skip.astype(F32), (1, 2 * p // hc))]
        + [jnp.zeros((g, 2 * p), F32)] * (SUBLANES - 4), axis=1)
    bc = jnp.stack(
        [b_re.astype(F32).transpose(1, 3, 0, 2).reshape(g, hc, 2 * p),
         b_im.astype(F32).transpose(1, 3, 0, 2).reshape(g, hc, 2 * p),
         c_re.astype(F32).transpose(1, 2, 0, 3).reshape(g, hc, 2 * p),
         c_im.astype(F32).transpose(1, 2, 0, 3).reshape(g, hc, 2 * p)], axis=1)
    gs = OPS_GROUPS_PER_STEP
    mat = pl.BlockSpec((gs, CHUNK_WIDTH, CHUNK_WIDTH), lambda s: (s, 0, 0))
    mat_shape = jax.ShapeDtypeStruct((g, CHUNK_WIDTH, CHUNK_WIDTH), BF16)
    return pl.pallas_call(
        _ssm_ops_kernel,
        grid=(g // gs,),
        in_specs=[pl.BlockSpec((gs, SUBLANES, 2 * p), lambda s: (s, 0, 0)),
                  pl.BlockSpec((gs, 4, hc, 2 * p), lambda s: (s, 0, 0, 0))],
        out_specs=[mat, mat, mat, pl.BlockSpec((gs, SUBLANES, 2 * p), lambda s: (s, 0, 0))],
        out_shape=[mat_shape, mat_shape, mat_shape,
                   jax.ShapeDtypeStruct((g, SUBLANES, 2 * p), F32)],
        compiler_params=pltpu.CompilerParams(
            dimension_semantics=("parallel",), vmem_limit_bytes=VMEM_LIMIT_BYTES),
    )(prm, bc)


def _lane_roll(x, shift):
    shift %= LANES
    return jnp.concatenate([x[:, LANES - shift:], x[:, :LANES - shift]], axis=1)


def _block_transpose(v, lane_block):
    v = list(v)
    n = len(v)
    d = n // 2
    while d >= 1:
        upper = (lane_block & d) != 0
        for i in range(n):
            if i & d == 0:
                a, b = v[i], v[i + d]
                v[i] = jnp.where(upper, _lane_roll(b, SSM_CH * d), a)
                v[i + d] = jnp.where(upper, b, _lane_roll(a, -SSM_CH * d))
        d //= 2
    return v


def _ssm_kernel(u_ref, ein_ref, toep_ref, eout_ref, aq_ref, y_ref, ug_ref, s_ref, x_ref,
                *, batch, seq, pitch):
    q, half = SSM_CHUNK, SSM_STATE
    n_chunks = seq // q
    gpb = GROUPS_PER_BLOCK
    slab = 2 * batch
    lane_block = lax.broadcasted_iota(jnp.int32, (slab, LANES), 1) // SSM_CH

    for b in range(batch if pitch > seq else 0):
        y_ref[b * pitch + seq:(b + 1) * pitch, :] = jnp.zeros((pitch - seq, LANES), F32)

    def gather(cp, carry):
        rows = pl.ds(pl.multiple_of(cp * slab, slab), slab)
        for part in range(q // gpb):
            t0 = cp * 2 * q + part * gpb
            v = [jnp.concatenate([u_ref[pl.ds(t0 + i, batch, stride=pitch), :],
                                  u_ref[pl.ds(t0 + q + i, batch, stride=pitch), :]],
                                 axis=0).astype(BF16) for i in range(gpb)]
            w = _block_transpose(v, lane_block)
            for g in range(gpb):
                ug_ref[g, rows, part * LANES:(part + 1) * LANES] = w[g]
        return carry

    lax.fori_loop(0, n_chunks // 2, gather, 0, unroll=SSM_RELAYOUT_UNROLL)

    fwd_lane = lax.broadcasted_iota(jnp.int32, (batch, 2 * half), 1) < half
    zeros = jnp.zeros((batch, half), F32)
    last_rows = pl.ds((n_chunks - 1) * batch, batch)
    ni = SSM_INTERLEAVE
    for g0 in range(0, gpb, ni):
        for gi in range(ni):
            s_ref[gi] = jnp.dot(ug_ref[g0 + gi], ein_ref[g0 + gi], preferred_element_type=F32)
            x_ref[gi, 0:batch, 0:half] = zeros
            x_ref[gi, 0:batch, 2 * half:3 * half] = zeros
            x_ref[gi, last_rows, half:2 * half] = zeros
            x_ref[gi, last_rows, 3 * half:4 * half] = zeros

        def step(k, carry):
            rf = pl.multiple_of(k * batch, batch)
            rb = pl.multiple_of((n_chunks - 1 - k) * batch, batch)
            new = []
            for gi in range(ni):
                xr, xi = carry[gi]
                sre = jnp.where(fwd_lane, s_ref[gi, pl.ds(rf, batch), 0:2 * half],
                                s_ref[gi, pl.ds(rb, batch), 0:2 * half])
                sim = jnp.where(fwd_lane, s_ref[gi, pl.ds(rf, batch), 2 * half:4 * half],
                                s_ref[gi, pl.ds(rb, batch), 2 * half:4 * half])
                ar = aq_ref[g0 + gi, 0:1, :]
                ai = aq_ref[g0 + gi, 1:2, :]
                nr = ar * xr - ai * xi + sre
                nim = ar * xi + ai * xr + sim
                x_ref[gi, pl.ds(rf + batch, batch), 0:half] = nr[:, 0:half]
                x_ref[gi, pl.ds(rf + batch, batch), 2 * half:3 * half] = nim[:, 0:half]
                x_ref[gi, pl.ds(rb - batch, batch), half:2 * half] = nr[:, half:2 * half]
                x_ref[gi, pl.ds(rb - batch, batch), 3 * half:4 * half] = nim[:, half:2 * half]
                new.append((nr, nim))
            return tuple(new)

        init = tuple((jnp.zeros((batch, 2 * half), F32), jnp.zeros((batch, 2 * half), F32))
                     for _ in range(ni))
        lax.fori_loop(0, n_chunks - 1, step, init)

        for gi in range(ni):
            g = g0 + gi
            y = jnp.dot(ug_ref[g], toep_ref[g], preferred_element_type=F32)
            y += lax.dot_general(x_ref[gi].astype(BF16), eout_ref[g], NT_DIMS,
                                 preferred_element_type=F32)
            ug_ref[g] = y.astype(BF16)

    def scatter(cp, carry):
        rows = pl.ds(pl.multiple_of(cp * slab, slab), slab)
        for part in range(q // gpb):
            t0 = cp * 2 * q + part * gpb
            v = [ug_ref[g, rows, part * LANES:(part + 1) * LANES] for g in range(gpb)]
            w = _block_transpose(v, lane_block)
            for j in range(gpb):
                wj = w[j].astype(F32)
                y_ref[pl.ds(t0 + j, batch, stride=pitch), :] = wj[0:batch]
                y_ref[pl.ds(t0 + q + j, batch, stride=pitch), :] = wj[batch:slab]
        return carry

    lax.fori_loop(0, n_chunks // 2, scatter, 0, unroll=SSM_RELAYOUT_UNROLL)


def _ssm(u, seq, lam_re, lam_im, log_dt, b_re, b_im, c_re, c_im, d_skip):
    batch, pitch, _ = u.shape
    rows = batch * seq // SSM_CHUNK
    ein, toep, eout, a_q = _ssm_operators(lam_re, lam_im, log_dt, b_re, b_im, c_re, c_im, d_skip)
    gpb = GROUPS_PER_BLOCK
    mat = pl.BlockSpec((gpb, CHUNK_WIDTH, CHUNK_WIDTH), lambda s: (s, 0, 0))
    col = pl.BlockSpec((batch * pitch, LANES), lambda s: (0, s), pipeline_mode=pl.Buffered(1))
    y = pl.pallas_call(
        functools.partial(_ssm_kernel, batch=batch, seq=seq, pitch=pitch),
        grid=(SSM_GROUPS // gpb,),
        in_specs=[col, mat, mat, mat,
                  pl.BlockSpec((gpb, SUBLANES, 2 * SSM_STATE), lambda s: (s, 0, 0))],
        out_specs=col,
        out_shape=jax.ShapeDtypeStruct((batch * pitch, SSM_WIDTH), F32),
        scratch_shapes=[pltpu.VMEM((gpb, rows, CHUNK_WIDTH), BF16),
                        pltpu.VMEM((SSM_INTERLEAVE, rows, 4 * SSM_STATE), F32),
                        pltpu.VMEM((SSM_INTERLEAVE, rows, 4 * SSM_STATE), F32)],
        compiler_params=pltpu.CompilerParams(
            dimension_semantics=("parallel",), vmem_limit_bytes=VMEM_LIMIT_BYTES),
    )(u.reshape(batch * pitch, SSM_WIDTH), ein, toep, eout, a_q)
    return y.reshape(batch, pitch, SSM_WIDTH)


def _mix_kernel(x_ref, attn_ref, ssm_ref, gw_ref, gb_ref, ga_ref, gs_ref, wo_ref, o_ref):
    y = jax.nn.gelu(ssm_ref[...])
    z = jnp.dot(y.astype(BF16), gw_ref[...].astype(BF16), preferred_element_type=F32) + gb_ref[...]
    s = y * jax.nn.sigmoid(z)
    sn = _rms(s) * gs_ref[...]
    an = _rms(attn_ref[...].astype(F32)) * ga_ref[...]
    mixed = jnp.concatenate([an, sn], axis=-1).astype(BF16)
    o_ref[...] = x_ref[...] + jnp.dot(mixed, wo_ref[...].astype(BF16), preferred_element_type=F32)


def _mix_out(x, attn, ssm_pre, seq, glu_w, glu_b, attn_gain, ssm_gain, w_out):
    t = x.shape[0]
    row = lambda width: pl.BlockSpec((TOKEN_TILE, width), lambda i: (i, 0))
    full = lambda r, c: pl.BlockSpec((r, c), lambda i: (0, 0))
    return pl.pallas_call(
        _mix_kernel,
        grid=(t // TOKEN_TILE,),
        in_specs=[row(D_MODEL), row(ATTN_WIDTH), _batch_rows(seq, SSM_WIDTH),
                  full(SSM_WIDTH, SSM_WIDTH), full(1, SSM_WIDTH),
                  full(1, ATTN_WIDTH), full(1, SSM_WIDTH), full(D_MODEL, D_MODEL)],
        out_specs=row(D_MODEL),
        out_shape=jax.ShapeDtypeStruct((t, D_MODEL), F32),
        compiler_params=pltpu.CompilerParams(
            dimension_semantics=("parallel",), vmem_limit_bytes=VMEM_LIMIT_BYTES),
    )(x, attn, ssm_pre, glu_w, glu_b.reshape(1, -1).astype(F32),
      attn_gain.reshape(1, -1).astype(F32), ssm_gain.reshape(1, -1).astype(F32), w_out)


def kernel(x, norm_ffn1, ffn1_w_gate, ffn1_w_up, ffn1_w_down, norm_mix, w_in, attn_sinks,
           ssm_lambda_re, ssm_lambda_im, ssm_log_dt, ssm_b_re, ssm_b_im, ssm_c_re, ssm_c_im,
           ssm_d, ssm_glu_w, ssm_glu_b, attn_out_norm, ssm_out_norm, w_out,
           norm_ffn2, ffn2_w_gate, ffn2_w_up, ffn2_w_down, final_norm):
    b, seq, d = x.shape
    depth = norm_ffn1.shape[0]
    assert d == D_MODEL and seq % BLOCK == 0 and seq % TOKEN_TILE == 0
    assert b == SUBLANES and seq % (2 * SSM_CHUNK * SSM_RELAYOUT_UNROLL) == 0
    h = x.reshape(b * seq, d).astype(F32)
    for l in range(depth):
        h = _ffn(h, norm_ffn1[l], ffn1_w_gate[l], ffn1_w_up[l], ffn1_w_down[l],
                 final_norm, final_norm=False)
        q, k, v, u = _in_proj(h, norm_mix[l], w_in[l], b, seq)
        attn = _attention(q.reshape(b, seq, -1), k.reshape(b, seq, -1), v.reshape(b, seq, -1),
                          attn_sinks[l])
        ssm_pre = _ssm(u, seq, ssm_lambda_re[l], ssm_lambda_im[l], ssm_log_dt[l],
                       ssm_b_re[l], ssm_b_im[l], ssm_c_re[l], ssm_c_im[l], ssm_d[l])
        h = _mix_out(h, attn.reshape(b * seq, -1), ssm_pre, seq, ssm_glu_w[l], ssm_glu_b[l],
                     attn_out_norm[l], ssm_out_norm[l], w_out[l])
        h = _ffn(h, norm_ffn2[l], ffn2_w_gate[l], ffn2_w_up[l], ffn2_w_down[l],
                 final_norm, final_norm=(l == depth - 1))
    return h.reshape(b, seq, d).astype(x.dtype)
```

```python
import functools

import jax
import jax.numpy as jnp
from jax import lax
from jax.experimental import pallas as pl
from jax.experimental.pallas import tpu as pltpu

F32 = jnp.float32
BF16 = jnp.bfloat16

D_MODEL = 1024
ATTN_HEADS = 8
ATTN_KV_HEADS = 2
Q_PER_KV = ATTN_HEADS // ATTN_KV_HEADS
HEAD_DIM = 64
ATTN_WIDTH = ATTN_HEADS * HEAD_DIM
KV_WIDTH = ATTN_KV_HEADS * HEAD_DIM
WINDOW = 128
BLOCK = 128
SSM_CH = 16
SSM_WIDTH = D_MODEL - ATTN_WIDTH
SSM_GROUPS = SSM_WIDTH // SSM_CH
SSM_STATE = 64
IN_WIDTH = ATTN_WIDTH + 2 * KV_WIDTH + SSM_WIDTH
D_FF = 2816
EPS = 1e-6
NEG_INF = -1e30
LAMBDA_RE_MAX = -1e-4

LANES = 128
SUBLANES = 8
VMEM_LIMIT_BYTES = 56 * 1024 * 1024

TOKEN_TILE = 1024
FFN_TOKEN_TILE = 512
FF_TILE = 256
ATTN_BLOCKS_PER_STEP = 4
SSM_CHUNK = 16
CHUNK_WIDTH = SSM_CHUNK * SSM_CH
GROUPS_PER_BLOCK = LANES // SSM_CH
SSM_INTERLEAVE = 4
SSM_RELAYOUT_UNROLL = 4
OPS_GROUPS_PER_STEP = 4

NT_DIMS = (((1,), (1,)), ((), ()))
TN_DIMS = (((0,), (0,)), ((), ()))


def _rms(x):
    return x * lax.rsqrt(jnp.mean(x * x, axis=-1, keepdims=True) + EPS)


def _mixed_update(attn_ref, ssm_ref, gw_ref, gb_ref, ga_ref, gs_ref, wo_ref):
    y = jax.nn.gelu(ssm_ref[...])
    z = jnp.dot(y.astype(BF16), gw_ref[...].astype(BF16), preferred_element_type=F32) + gb_ref[...]
    s = y * jax.nn.sigmoid(z)
    sn = _rms(s) * gs_ref[...]
    an = _rms(attn_ref[...].astype(F32)) * ga_ref[...]
    mixed = jnp.concatenate([an, sn], axis=-1).astype(BF16)
    return jnp.dot(mixed, wo_ref[...].astype(BF16), preferred_element_type=F32)


def _ffn_body(read_x, gain_ref, fgain_ref, wg_hbm, wu_hbm, wd_hbm, o_ref,
              wg_ref, wu_ref, wd_ref, act_ref, gu_stage, d_stage, sem, *, final_norm):
    nj = D_FF // FF_TILE

    def weight_copies(j, slot):
        span = pl.ds(j * FF_TILE, FF_TILE)
        return (pltpu.make_async_copy(wg_hbm.at[:, span], gu_stage.at[0, slot], sem.at[0, slot]),
                pltpu.make_async_copy(wu_hbm.at[:, span], gu_stage.at[1, slot], sem.at[1, slot]),
                pltpu.make_async_copy(wd_hbm.at[span, :], d_stage.at[slot], sem.at[2, slot]))

    def step(stage_weights):
        if stage_weights:
            for copy in weight_copies(0, 0):
                copy.start()
        x = read_x()
        hn = (_rms(x) * gain_ref[...]).astype(BF16)
        for j in range(nj):
            cols = slice(j * FF_TILE, (j + 1) * FF_TILE)
            if stage_weights:
                slot = j % 2
                if j + 1 < nj:
                    for copy in weight_copies(j + 1, 1 - slot):
                        copy.start()
                for copy in weight_copies(j, slot):
                    copy.wait()
                wg_ref[:, cols] = gu_stage[0, slot].astype(BF16)
                wu_ref[:, cols] = gu_stage[1, slot].astype(BF16)
                wd_ref[cols, :] = d_stage[slot].astype(BF16)
            g = jnp.dot(hn, wg_ref[:, cols], preferred_element_type=F32)
            u = jnp.dot(hn, wu_ref[:, cols], preferred_element_type=F32)
            act_ref[:, cols] = (g * jax.nn.sigmoid(g) * u).astype(BF16)
        y = x + 0.5 * jnp.dot(act_ref[...], wd_ref[...], preferred_element_type=F32)
        if final_norm:
            y = _rms(y) * fgain_ref[...]
        o_ref[...] = y

    pl.when(pl.program_id(0) == 0)(functools.partial(step, True))
    pl.when(pl.program_id(0) != 0)(functools.partial(step, False))


def _ffn_kernel(x_ref, *ffn_refs, final_norm):
    _ffn_body(lambda: x_ref[...], *ffn_refs, final_norm=final_norm)


def _mix_ffn_kernel(x_ref, attn_ref, ssm_ref, gw_ref, gb_ref, ga_ref, gs_ref, wo_ref, *ffn_refs,
                    final_norm):
    read_x = lambda: x_ref[...] + _mixed_update(attn_ref, ssm_ref, gw_ref, gb_ref, ga_ref, gs_ref,
                                                wo_ref)
    _ffn_body(read_x, *ffn_refs, final_norm=final_norm)


def _ffn(x, gain, w_gate, w_up, w_down, final_gain, final_norm, mix=None):
    t = x.shape[0]
    tile = FFN_TOKEN_TILE
    vec = lambda width: pl.BlockSpec((1, width), lambda i: (0, 0))
    resident = lambda r, c: pl.BlockSpec((r, c), lambda i: (0, 0), pipeline_mode=pl.Buffered(1))
    row = lambda width: pl.BlockSpec((tile, width), lambda i: (i, 0))
    hbm = pl.BlockSpec(memory_space=pl.ANY)
    in_specs = [row(D_MODEL)]
    args = [x]
    body = _ffn_kernel
    if mix is not None:
        attn, ssm_pre, seq, glu_w, glu_b, attn_gain, ssm_gain, w_out = mix
        body = _mix_ffn_kernel
        in_specs += [row(ATTN_WIDTH), _batch_rows(seq, SSM_WIDTH, tile),
                     resident(SSM_WIDTH, SSM_WIDTH), vec(SSM_WIDTH), vec(ATTN_WIDTH),
                     vec(SSM_WIDTH), resident(D_MODEL, D_MODEL)]
        args += [attn, ssm_pre, glu_w, glu_b.reshape(1, -1).astype(F32),
                 attn_gain.reshape(1, -1).astype(F32), ssm_gain.reshape(1, -1).astype(F32), w_out]
    in_specs += [vec(D_MODEL), vec(D_MODEL), hbm, hbm, hbm]
    args += [gain.reshape(1, D_MODEL), final_gain.reshape(1, D_MODEL), w_gate, w_up, w_down]
    return pl.pallas_call(
        functools.partial(body, final_norm=final_norm),
        grid=(t // tile,),
        in_specs=in_specs,
        out_specs=row(D_MODEL),
        out_shape=jax.ShapeDtypeStruct((t, D_MODEL), F32),
        scratch_shapes=[pltpu.VMEM((D_MODEL, D_FF), BF16),
                        pltpu.VMEM((D_MODEL, D_FF), BF16),
                        pltpu.VMEM((D_FF, D_MODEL), BF16),
                        pltpu.VMEM((tile, D_FF), BF16),
                        pltpu.VMEM((2, 2, D_MODEL, FF_TILE), F32),
                        pltpu.VMEM((2, FF_TILE, D_MODEL), F32),
                        pltpu.SemaphoreType.DMA((3, 2))],
        compiler_params=pltpu.CompilerParams(
            dimension_semantics=("arbitrary",), vmem_limit_bytes=VMEM_LIMIT_BYTES),
    )(*args)


def _in_proj_kernel(x_ref, gain_ref, w_ref, q_ref, k_ref, v_ref, u_ref, *, seq):
    j = pl.program_id(1)
    hn = (_rms(x_ref[...]) * gain_ref[...]).astype(BF16)
    proj = jnp.dot(hn, w_ref[...].astype(BF16), preferred_element_type=F32)
    q_ref[...] = proj[:, :ATTN_WIDTH].astype(BF16)
    k_ref[...] = proj[:, ATTN_WIDTH:ATTN_WIDTH + KV_WIDTH].astype(BF16)
    v_ref[...] = proj[:, ATTN_WIDTH + KV_WIDTH:ATTN_WIDTH + 2 * KV_WIDTH].astype(BF16)
    rows = pl.ds(pl.multiple_of(j * TOKEN_TILE, TOKEN_TILE), TOKEN_TILE)
    u_ref[rows, :] = proj[:, ATTN_WIDTH + 2 * KV_WIDTH:]
    pitch = u_ref.shape[0]
    if pitch > seq:
        @pl.when(j == 0)
        def _():
            u_ref[seq:pitch, :] = jnp.zeros((pitch - seq, SSM_WIDTH), F32)


def _padded_seq(seq):
    return seq + SUBLANES if seq % (2 * SUBLANES) == 0 else seq


def _batch_rows(seq, width, tile):
    tiles = seq // tile
    return pl.BlockSpec((None, tile, width), lambda i: (i // tiles, i % tiles, 0))


def _in_proj(x, gain, w_in, batch, seq):
    t = x.shape[0]
    tiles = seq // TOKEN_TILE
    pitch = _padded_seq(seq)
    row = lambda width: pl.BlockSpec((TOKEN_TILE, width), lambda b, j: (b * tiles + j, 0))
    return pl.pallas_call(
        functools.partial(_in_proj_kernel, seq=seq),
        grid=(batch, tiles),
        in_specs=[
            row(D_MODEL),
            pl.BlockSpec((1, D_MODEL), lambda b, j: (0, 0)),
            pl.BlockSpec((D_MODEL, IN_WIDTH), lambda b, j: (0, 0)),
        ],
        out_specs=[row(ATTN_WIDTH), row(KV_WIDTH), row(KV_WIDTH),
                   pl.BlockSpec((None, pitch, SSM_WIDTH), lambda b, j: (b, 0, 0))],
        out_shape=[
            jax.ShapeDtypeStruct((t, ATTN_WIDTH), BF16),
            jax.ShapeDtypeStruct((t, KV_WIDTH), BF16),
            jax.ShapeDtypeStruct((t, KV_WIDTH), BF16),
            jax.ShapeDtypeStruct((batch, pitch, SSM_WIDTH), F32),
        ],
        compiler_params=pltpu.CompilerParams(
            dimension_semantics=("parallel", "arbitrary"), vmem_limit_bytes=VMEM_LIMIT_BYTES),
    )(x, gain.reshape(1, D_MODEL), w_in)


def _attn_kernel(sink_ref, q_ref, kp_ref, kc_ref, kn_ref, vp_ref, vc_ref, vn_ref, o_ref,
                 bias_ref, k_ref, v_ref, s_ref):
    n = pl.program_id(1)
    last = pl.num_programs(1) - 1
    nblk = ATTN_BLOCKS_PER_STEP

    @pl.when(n == 0)
    def _():
        kj = lax.broadcasted_iota(jnp.int32, (3 * BLOCK, BLOCK), 0)
        qi = lax.broadcasted_iota(jnp.int32, (3 * BLOCK, BLOCK), 1)
        rel = jnp.abs(kj - BLOCK - qi)
        dist = rel.astype(F32)
        inside = rel <= WINDOW
        has_prev = kj >= BLOCK
        has_next = kj < 2 * BLOCK
        for variant, ok in enumerate((inside & has_prev, inside, inside & has_next)):
            for h in range(ATTN_HEADS):
                slope = float(2.0 ** (-8.0 * (h + 1) / ATTN_HEADS))
                bias_ref[variant, h] = jnp.where(ok, -slope * dist, NEG_INF)

    k_ref[0:BLOCK] = kp_ref[...]
    k_ref[BLOCK:(nblk + 1) * BLOCK] = kc_ref[...]
    k_ref[(nblk + 1) * BLOCK:(nblk + 2) * BLOCK] = kn_ref[...]
    v_ref[0:BLOCK] = vp_ref[...]
    v_ref[BLOCK:(nblk + 1) * BLOCK] = vc_ref[...]
    v_ref[(nblk + 1) * BLOCK:(nblk + 2) * BLOCK] = vn_ref[...]

    for j in range(nblk):
        for kh in range(ATTN_KV_HEADS):
            kcat = k_ref[j * BLOCK:(j + 3) * BLOCK, kh * HEAD_DIM:(kh + 1) * HEAD_DIM]
            heads = [kh * Q_PER_KV + g for g in range(Q_PER_KV)]
            qs = jnp.concatenate(
                [q_ref[j * BLOCK:(j + 1) * BLOCK, h * HEAD_DIM:(h + 1) * HEAD_DIM] for h in heads],
                axis=0)
            s_ref[j, kh] = lax.dot_general(kcat, qs * (HEAD_DIM ** -0.5), NT_DIMS,
                                           preferred_element_type=F32)

    for j in range(nblk):
        variant = 1
        if j == 0:
            variant = jnp.where(n == 0, 0, variant)
        if j == nblk - 1:
            variant = jnp.where(n == last, 2, variant)
        outs = []
        for kh in range(ATTN_KV_HEADS):
            vcat = v_ref[j * BLOCK:(j + 3) * BLOCK, kh * HEAD_DIM:(kh + 1) * HEAD_DIM]
            for g in range(Q_PER_KV):
                h = kh * Q_PER_KV + g
                s = s_ref[j, kh, :, g * BLOCK:(g + 1) * BLOCK] + bias_ref[variant, h]
                sink = sink_ref[h]
                m = jnp.maximum(jnp.max(s, axis=0, keepdims=True), sink)
                e = jnp.exp(s - m)
                den = jnp.sum(e, axis=0, keepdims=True) + jnp.exp(sink - m)
                pv = lax.dot_general(vcat, e.astype(BF16), TN_DIMS,
                                     preferred_element_type=F32)
                outs.append(pv / den)
        o_ref[j * BLOCK:(j + 1) * BLOCK, :] = jnp.concatenate(outs, axis=0).T.astype(o_ref.dtype)


def _attention(q, k, v, sinks):
    b, seq, _ = q.shape
    nblk = ATTN_BLOCKS_PER_STEP
    steps = seq // (nblk * BLOCK)
    nb = seq // BLOCK
    assert seq % (nblk * BLOCK) == 0 and nb >= 2 and HEAD_DIM ** -0.5 == 2.0 ** -3
    edge = lambda f: pl.BlockSpec((None, BLOCK, KV_WIDTH), f)
    body = pl.BlockSpec((None, nblk * BLOCK, KV_WIDTH), lambda bi, n: (bi, n, 0))
    prev = lambda bi, n: (bi, jnp.maximum(n * nblk - 1, 0), 0)
    nxt = lambda bi, n: (bi, jnp.minimum((n + 1) * nblk, nb - 1), 0)
    rows = pl.BlockSpec((None, nblk * BLOCK, ATTN_WIDTH), lambda bi, n: (bi, n, 0))
    return pl.pallas_call(
        _attn_kernel,
        grid=(b, steps),
        in_specs=[pl.BlockSpec(memory_space=pltpu.SMEM), rows,
                  edge(prev), body, edge(nxt), edge(prev), body, edge(nxt)],
        out_specs=rows,
        out_shape=jax.ShapeDtypeStruct((b, seq, ATTN_WIDTH), BF16),
        scratch_shapes=[pltpu.VMEM((3, ATTN_HEADS, 3 * BLOCK, BLOCK), F32),
                        pltpu.VMEM(((nblk + 2) * BLOCK, KV_WIDTH), BF16),
                        pltpu.VMEM(((nblk + 2) * BLOCK, KV_WIDTH), BF16),
                        pltpu.VMEM((nblk, ATTN_KV_HEADS, 3 * BLOCK, Q_PER_KV * BLOCK), F32)],
        compiler_params=pltpu.CompilerParams(
            dimension_semantics=("parallel", "arbitrary"), vmem_limit_bytes=VMEM_LIMIT_BYTES),
    )(sinks.astype(F32), q, k, k, k, v, v, v)


def _dot_nt_split(a, b):
    a_hi, b_hi = a.astype(BF16), b.astype(BF16)
    a_lo = (a - a_hi.astype(F32)).astype(BF16)
    b_lo = (b - b_hi.astype(F32)).astype(BF16)
    dot = functools.partial(lax.dot_general, dimension_numbers=NT_DIMS,
                            preferred_element_type=F32)
    return dot(a_hi, b_hi) + dot(a_hi, b_lo) + dot(a_lo, b_hi)


def _ssm_ops_kernel(prm_ref, bc_ref, ein_ref, toep_ref, eout_ref, aq_ref):
    q, hc, p = SSM_CHUNK, SSM_CH, SSM_STATE
    fwd = lax.broadcasted_iota(jnp.int32, (1, 2 * p), 1) < p
    zero_row = jnp.zeros((1, 2 * p), F32)
    row_id = lax.broadcasted_iota(jnp.int32, (CHUNK_WIDTH, CHUNK_WIDTH), 0)
    col_id = lax.broadcasted_iota(jnp.int32, (CHUNK_WIDTH, CHUNK_WIDTH), 1)

    def table(select, n):
        picks = [select(m) for m in range(n)]
        re = jnp.concatenate([jnp.broadcast_to(r, (hc, 2 * p)) for r, _ in picks], axis=0)
        im = jnp.concatenate([jnp.broadcast_to(i, (hc, 2 * p)) for _, i in picks], axis=0)
        return re, im

    def tile_rows(x, n):
        return jnp.concatenate([x] * n, axis=0)

    for gi in range(prm_ref.shape[0]):
        lr = jnp.minimum(prm_ref[gi, 0:1, :], LAMBDA_RE_MAX)
        li = prm_ref[gi, 1:2, :]
        dt = jnp.exp(prm_ref[gi, 2:3, :])
        mag = jnp.exp(lr * dt)
        a_r = mag * jnp.cos(li * dt)
        a_i = mag * jnp.sin(li * dt)
        den = lr * lr + li * li
        coef_r = ((a_r - 1.0) * lr + a_i * li) / den
        coef_i = (a_i * lr - (a_r - 1.0) * li) / den
        b_r, b_i = bc_ref[gi, 0], bc_ref[gi, 1]
        c_r, c_i = bc_ref[gi, 2], bc_ref[gi, 3]
        bb_r = coef_r * b_r - coef_i * b_i
        bb_i = coef_r * b_i + coef_i * b_r

        pw = [(jnp.ones((1, 2 * p), F32), zero_row)]
        for _ in range(q):
            r, i = pw[-1]
            pw.append((r * a_r - i * a_i, r * a_i + i * a_r))

        def both(f_idx, b_idx):
            fr, fi = pw[f_idx] if f_idx is not None else (zero_row, zero_row)
            br, bi = pw[b_idx] if b_idx is not None else (zero_row, zero_row)
            return jnp.where(fwd, fr, br), jnp.where(fwd, fi, bi)

        p_r, p_i = table(lambda i: both(q - 1 - i, i), q)
        tb_r, tb_i = tile_rows(bb_r, q), tile_rows(bb_i, q)
        ein = jnp.concatenate([tb_r * p_r - tb_i * p_i, tb_r * p_i + tb_i * p_r], axis=1)
        ein_ref[gi] = ein.astype(BF16)

        p_r, p_i = table(lambda j: both(j + 1, q - j), q)
        tc_r, tc_i = tile_rows(c_r, q), tile_rows(c_i, q)
        eout = jnp.concatenate([tc_r * p_r - tc_i * p_i, -(tc_r * p_i + tc_i * p_r)], axis=1)
        eout_ref[gi] = eout.astype(BF16)

        def lag(m):
            return both(m - (q - 1) if q - 1 <= m <= 2 * q - 2 else None,
                        (q - 1) - m if m <= q - 1 else None)

        p_r, p_i = table(lag, 2 * q)
        tc_r, tc_i = tile_rows(c_r, 2 * q), tile_rows(c_i, 2 * q)
        cpt = jnp.concatenate([tc_r * p_r - tc_i * p_i, tc_r * p_i + tc_i * p_r], axis=1)
        bcat = jnp.concatenate([bb_r, -bb_i], axis=1)
        kern = _dot_nt_split(bcat, cpt)
        toep = jnp.concatenate(
            [kern[:, hc * (q - 1 - i):hc * (q - 1 - i) + CHUNK_WIDTH] for i in range(q)], axis=0)
        skip = jnp.concatenate([prm_ref[gi, 3:4, :]] * (CHUNK_WIDTH // (2 * p)), axis=1)
        toep_ref[gi] = (toep + jnp.where(row_id == col_id, skip, 0.0)).astype(BF16)

        aq_ref[gi] = jnp.concatenate(
            [pw[q][0], pw[q][1], jnp.zeros((SUBLANES - 2, 2 * p), F32)], axis=0)


def _ssm_operators(lam_re, lam_im, log_dt, b_re, b_im, c_re, c_im, d_skip):
    g, p, hc = SSM_GROUPS, SSM_STATE, SSM_CH
    lanes = lambda a: a.astype(F32).transpose(1, 0, 2).reshape(g, 2 * p)
    prm = jnp.stack(
        [lanes(lam_re), lanes(lam_im),
         jnp.repeat(log_dt.astype(F32).T, p, axis=1),
         jnp.tile(d_skip.astype(F32), (1, 2 * p // hc))]
        + [jnp.zeros((g, 2 * p), F32)] * (SUBLANES - 4), axis=1)
    bc = jnp.stack(
        [b_re.astype(F32).transpose(1, 3, 0, 2).reshape(g, hc, 2 * p),
         b_im.astype(F32).transpose(1, 3, 0, 2).reshape(g, hc, 2 * p),
         c_re.astype(F32).transpose(1, 2, 0, 3).reshape(g, hc, 2 * p),
         c_im.astype(F32).transpose(1, 2, 0, 3).reshape(g, hc, 2 * p)], axis=1)
    gs = OPS_GROUPS_PER_STEP
    mat = pl.BlockSpec((gs, CHUNK_WIDTH, CHUNK_WIDTH), lambda s: (s, 0, 0))
    mat_shape = jax.ShapeDtypeStruct((g, CHUNK_WIDTH, CHUNK_WIDTH), BF16)
    return pl.pallas_call(
        _ssm_ops_kernel,
        grid=(g // gs,),
        in_specs=[pl.BlockSpec((gs, SUBLANES, 2 * p), lambda s: (s, 0, 0)),
                  pl.BlockSpec((gs, 4, hc, 2 * p), lambda s: (s, 0, 0, 0))],
        out_specs=[mat, mat, mat, pl.BlockSpec((gs, SUBLANES, 2 * p), lambda s: (s, 0, 0))],
        out_shape=[mat_shape, mat_shape, mat_shape,
                   jax.ShapeDtypeStruct((g, SUBLANES, 2 * p), F32)],
        compiler_params=pltpu.CompilerParams(
            dimension_semantics=("parallel",), vmem_limit_bytes=VMEM_LIMIT_BYTES),
    )(prm, bc)


def _lane_roll(x, shift):
    shift %= LANES
    return jnp.concatenate([x[:, LANES - shift:], x[:, :LANES - shift]], axis=1)


def _block_transpose(v, lane_block):
    v = list(v)
    n = len(v)
    d = n // 2
    while d >= 1:
        upper = (lane_block & d) != 0
        for i in range(n):
            if i & d == 0:
                a, b = v[i], v[i + d]
                v[i] = jnp.where(upper, _lane_roll(b, SSM_CH * d), a)
                v[i + d] = jnp.where(upper, b, _lane_roll(a, -SSM_CH * d))
        d //= 2
    return v


def _ssm_kernel(u_ref, ein_ref, toep_ref, eout_ref, aq_ref, y_ref, ug_ref, s_ref, x_ref,
                *, batch, seq, pitch):
    q, half = SSM_CHUNK, SSM_STATE
    n_chunks = seq // q
    gpb = GROUPS_PER_BLOCK
    slab = 2 * batch
    lane_block = lax.broadcasted_iota(jnp.int32, (slab, LANES), 1) // SSM_CH

    for b in range(batch if pitch > seq else 0):
        y_ref[b * pitch + seq:(b + 1) * pitch, :] = jnp.zeros((pitch - seq, LANES), F32)

    def gather(cp, carry):
        rows = pl.ds(pl.multiple_of(cp * slab, slab), slab)
        for part in range(q // gpb):
            t0 = cp * 2 * q + part * gpb
            v = [jnp.concatenate([u_ref[pl.ds(t0 + i, batch, stride=pitch), :],
                                  u_ref[pl.ds(t0 + q + i, batch, stride=pitch), :]],
                                 axis=0).astype(BF16) for i in range(gpb)]
            w = _block_transpose(v, lane_block)
            for g in range(gpb):
                ug_ref[g, rows, part * LANES:(part + 1) * LANES] = w[g]
        return carry

    lax.fori_loop(0, n_chunks // 2, gather, 0, unroll=SSM_RELAYOUT_UNROLL)

    fwd_lane = lax.broadcasted_iota(jnp.int32, (batch, 2 * half), 1) < half
    zeros = jnp.zeros((batch, half), F32)
    last_rows = pl.ds((n_chunks - 1) * batch, batch)
    ni = SSM_INTERLEAVE
    for g0 in range(0, gpb, ni):
        for gi in range(ni):
            s_ref[gi] = jnp.dot(ug_ref[g0 + gi], ein_ref[g0 + gi], preferred_element_type=F32)
            x_ref[gi, 0:batch, 0:half] = zeros
            x_ref[gi, 0:batch, 2 * half:3 * half] = zeros
            x_ref[gi, last_rows, half:2 * half] = zeros
            x_ref[gi, last_rows, 3 * half:4 * half] = zeros

        def step(k, carry):
            rf = pl.multiple_of(k * batch, batch)
            rb = pl.multiple_of((n_chunks - 1 - k) * batch, batch)
            new = []
            for gi in range(ni):
                xr, xi = carry[gi]
                sre = jnp.where(fwd_lane, s_ref[gi, pl.ds(rf, batch), 0:2 * half],
                                s_ref[gi, pl.ds(rb, batch), 0:2 * half])
                sim = jnp.where(fwd_lane, s_ref[gi, pl.ds(rf, batch), 2 * half:4 * half],
                                s_ref[gi, pl.ds(rb, batch), 2 * half:4 * half])
                ar = aq_ref[g0 + gi, 0:1, :]
                ai = aq_ref[g0 + gi, 1:2, :]
                nr = ar * xr - ai * xi + sre
                nim = ar * xi + ai * xr + sim
                x_ref[gi, pl.ds(rf + batch, batch), 0:half] = nr[:, 0:half]
                x_ref[gi, pl.ds(rf + batch, batch), 2 * half:3 * half] = nim[:, 0:half]
                x_ref[gi, pl.ds(rb - batch, batch), half:2 * half] = nr[:, half:2 * half]
                x_ref[gi, pl.ds(rb - batch, batch), 3 * half:4 * half] = nim[:, half:2 * half]
                new.append((nr, nim))
            return tuple(new)

        init = tuple((jnp.zeros((batch, 2 * half), F32), jnp.zeros((batch, 2 * half), F32))
                     for _ in range(ni))
        lax.fori_loop(0, n_chunks - 1, step, init)

        for gi in range(ni):
            g = g0 + gi
            y = jnp.dot(ug_ref[g], toep_ref[g], preferred_element_type=F32)
            y += lax.dot_general(x_ref[gi].astype(BF16), eout_ref[g], NT_DIMS,
                                 preferred_element_type=F32)
            ug_ref[g] = y.astype(BF16)

    def scatter(cp, carry):
        rows = pl.ds(pl.multiple_of(cp * slab, slab), slab)
        for part in range(q // gpb):
            t0 = cp * 2 * q + part * gpb
            v = [ug_ref[g, rows, part * LANES:(part + 1) * LANES] for g in range(gpb)]
            w = _block_transpose(v, lane_block)
            for j in range(gpb):
                wj = w[j].astype(F32)
                y_ref[pl.ds(t0 + j, batch, stride=pitch), :] = wj[0:batch]
                y_ref[pl.ds(t0 + q + j, batch, stride=pitch), :] = wj[batch:slab]
        return carry

    lax.fori_loop(0, n_chunks // 2, scatter, 0, unroll=SSM_RELAYOUT_UNROLL)


def _ssm(u, seq, lam_re, lam_im, log_dt, b_re, b_im, c_re, c_im, d_skip):
    batch, pitch, _ = u.shape
    rows = batch * seq // SSM_CHUNK
    ein, toep, eout, a_q = _ssm_operators(lam_re, lam_im, log_dt, b_re, b_im, c_re, c_im, d_skip)
    gpb = GROUPS_PER_BLOCK
    mat = pl.BlockSpec((gpb, CHUNK_WIDTH, CHUNK_WIDTH), lambda s: (s, 0, 0))
    col = pl.BlockSpec((batch * pitch, LANES), lambda s: (0, s), pipeline_mode=pl.Buffered(1))
    y = pl.pallas_call(
        functools.partial(_ssm_kernel, batch=batch, seq=seq, pitch=pitch),
        grid=(SSM_GROUPS // gpb,),
        in_specs=[col, mat, mat, mat,
                  pl.BlockSpec((gpb, SUBLANES, 2 * SSM_STATE), lambda s: (s, 0, 0))],
        out_specs=col,
        out_shape=jax.ShapeDtypeStruct((batch * pitch, SSM_WIDTH), F32),
        scratch_shapes=[pltpu.VMEM((gpb, rows, CHUNK_WIDTH), BF16),
                        pltpu.VMEM((SSM_INTERLEAVE, rows, 4 * SSM_STATE), F32),
                        pltpu.VMEM((SSM_INTERLEAVE, rows, 4 * SSM_STATE), F32)],
        compiler_params=pltpu.CompilerParams(
            dimension_semantics=("parallel",), vmem_limit_bytes=VMEM_LIMIT_BYTES),
    )(u.reshape(batch * pitch, SSM_WIDTH), ein, toep, eout, a_q)
    return y.reshape(batch, pitch, SSM_WIDTH)


def kernel(x, norm_ffn1, ffn1_w_gate, ffn1_w_up, ffn1_w_down, norm_mix, w_in, attn_sinks,
           ssm_lambda_re, ssm_lambda_im, ssm_log_dt, ssm_b_re, ssm_b_im, ssm_c_re, ssm_c_im,
           ssm_d, ssm_glu_w, ssm_glu_b, attn_out_norm, ssm_out_norm, w_out,
           norm_ffn2, ffn2_w_gate, ffn2_w_up, ffn2_w_down, final_norm):
    b, seq, d = x.shape
    depth = norm_ffn1.shape[0]
    assert d == D_MODEL and seq % BLOCK == 0 and seq % TOKEN_TILE == 0 and seq % FFN_TOKEN_TILE == 0
    assert b == SUBLANES and seq % (2 * SSM_CHUNK * SSM_RELAYOUT_UNROLL) == 0
    h = x.reshape(b * seq, d).astype(F32)
    for l in range(depth):
        h = _ffn(h, norm_ffn1[l], ffn1_w_gate[l], ffn1_w_up[l], ffn1_w_down[l],
                 final_norm, final_norm=False)
        q, k, v, u = _in_proj(h, norm_mix[l], w_in[l], b, seq)
        attn = _attention(q.reshape(b, seq, -1), k.reshape(b, seq, -1), v.reshape(b, seq, -1),
                          attn_sinks[l])
        ssm_pre = _ssm(u, seq, ssm_lambda_re[l], ssm_lambda_im[l], ssm_log_dt[l],
                       ssm_b_re[l], ssm_b_im[l], ssm_c_re[l], ssm_c_im[l], ssm_d[l])
        mix = (attn.reshape(b * seq, -1), ssm_pre, seq, ssm_glu_w[l], ssm_glu_b[l],
               attn_out_norm[l], ssm_out_norm[l], w_out[l])
        h = _ffn(h, norm_ffn2[l], ffn2_w_gate[l], ffn2_w_up[l], ffn2_w_down[l],
                 final_norm, final_norm=(l == depth - 1), mix=mix)
    return h.reshape(b, seq, d).astype(x.dtype)
```

```python
import functools

import jax
import jax.numpy as jnp
from jax import lax
from jax.experimental import pallas as pl
from jax.experimental.pallas import tpu as pltpu

F32 = jnp.float32
BF16 = jnp.bfloat16

D_MODEL = 1024
ATTN_HEADS = 8
ATTN_KV_HEADS = 2
Q_PER_KV = ATTN_HEADS // ATTN_KV_HEADS
HEAD_DIM = 64
ATTN_WIDTH = ATTN_HEADS * HEAD_DIM
KV_WIDTH = ATTN_KV_HEADS * HEAD_DIM
WINDOW = 128
BLOCK = 128
SSM_CH = 16
SSM_WIDTH = D_MODEL - ATTN_WIDTH
SSM_GROUPS = SSM_WIDTH // SSM_CH
SSM_STATE = 64
IN_WIDTH = ATTN_WIDTH + 2 * KV_WIDTH + SSM_WIDTH
D_FF = 2816
EPS = 1e-6
NEG_INF = -1e30
LAMBDA_RE_MAX = -1e-4

LANES = 128
SUBLANES = 8
VMEM_LIMIT_BYTES = 56 * 1024 * 1024

FFN_TOKEN_TILE = 512
FF_TILE = 256
ATTN_BLOCKS_PER_STEP = 4
SSM_CHUNK = 16
CHUNK_WIDTH = SSM_CHUNK * SSM_CH
GROUPS_PER_BLOCK = LANES // SSM_CH
SSM_INTERLEAVE = 4
SSM_RELAYOUT_UNROLL = 4
OPS_GROUPS_PER_STEP = 4

NT_DIMS = (((1,), (1,)), ((), ()))
TN_DIMS = (((0,), (0,)), ((), ()))


def _rms(x):
    return x * lax.rsqrt(jnp.mean(x * x, axis=-1, keepdims=True) + EPS)


def _mixed_update(attn_ref, ssm_ref, gw_ref, gb_ref, ga_ref, gs_ref, wo_ref):
    y = jax.nn.gelu(ssm_ref[...])
    z = jnp.dot(y.astype(BF16), gw_ref[...].astype(BF16), preferred_element_type=F32) + gb_ref[...]
    s = y * jax.nn.sigmoid(z)
    sn = _rms(s) * gs_ref[...]
    an = _rms(attn_ref[...].astype(F32)) * ga_ref[...]
    mixed = jnp.concatenate([an, sn], axis=-1).astype(BF16)
    return jnp.dot(mixed, wo_ref[...].astype(BF16), preferred_element_type=F32)


def _mixer_inputs(y, gain_ref, w_ref, q_ref, k_ref, v_ref, u_ref):
    hn = (_rms(y) * gain_ref[...]).astype(BF16)
    proj = jnp.dot(hn, w_ref[...].astype(BF16), preferred_element_type=F32)
    q_ref[...] = proj[:, :ATTN_WIDTH].astype(BF16)
    k_ref[...] = proj[:, ATTN_WIDTH:ATTN_WIDTH + KV_WIDTH].astype(BF16)
    v_ref[...] = proj[:, ATTN_WIDTH + KV_WIDTH:ATTN_WIDTH + 2 * KV_WIDTH].astype(BF16)
    u_ref[...] = proj[:, ATTN_WIDTH + 2 * KV_WIDTH:]


def _ffn_body(read_x, gain_ref, fgain_ref, wg_hbm, wu_hbm, wd_hbm, o_ref,
              wg_ref, wu_ref, wd_ref, act_ref, gu_stage, d_stage, sem, *, final_norm,
              epilogue=None):
    nj = D_FF // FF_TILE

    def weight_copies(j, slot):
        span = pl.ds(j * FF_TILE, FF_TILE)
        return (pltpu.make_async_copy(wg_hbm.at[:, span], gu_stage.at[0, slot], sem.at[0, slot]),
                pltpu.make_async_copy(wu_hbm.at[:, span], gu_stage.at[1, slot], sem.at[1, slot]),
                pltpu.make_async_copy(wd_hbm.at[span, :], d_stage.at[slot], sem.at[2, slot]))

    def step(stage_weights):
        if stage_weights:
            for copy in weight_copies(0, 0):
                copy.start()
        x = read_x()
        hn = (_rms(x) * gain_ref[...]).astype(BF16)
        for j in range(nj):
            cols = slice(j * FF_TILE, (j + 1) * FF_TILE)
            if stage_weights:
                slot = j % 2
                if j + 1 < nj:
                    for copy in weight_copies(j + 1, 1 - slot):
                        copy.start()
                for copy in weight_copies(j, slot):
                    copy.wait()
                wg_ref[:, cols] = gu_stage[0, slot].astype(BF16)
                wu_ref[:, cols] = gu_stage[1, slot].astype(BF16)
                wd_ref[cols, :] = d_stage[slot].astype(BF16)
            g = jnp.dot(hn, wg_ref[:, cols], preferred_element_type=F32)
            u = jnp.dot(hn, wu_ref[:, cols], preferred_element_type=F32)
            act_ref[:, cols] = (g * jax.nn.sigmoid(g) * u).astype(BF16)
        y = x + 0.5 * jnp.dot(act_ref[...], wd_ref[...], preferred_element_type=F32)
        if final_norm:
            y = _rms(y) * fgain_ref[...]
        o_ref[...] = y
        if epilogue is not None:
            epilogue(y)

    pl.when(pl.program_id(0) == 0)(functools.partial(step, True))
    pl.when(pl.program_id(0) != 0)(functools.partial(step, False))


def _ffn_proj_kernel(x_ref, mgain_ref, win_ref, gain_ref, fgain_ref, wg_hbm, wu_hbm, wd_hbm,
                     o_ref, q_ref, k_ref, v_ref, u_ref, *scratch, final_norm):
    emit = lambda y: _mixer_inputs(y, mgain_ref, win_ref, q_ref, k_ref, v_ref, u_ref)
    _ffn_body(lambda: x_ref[...], gain_ref, fgain_ref, wg_hbm, wu_hbm, wd_hbm, o_ref, *scratch,
              final_norm=final_norm, epilogue=emit)


def _mix_ffn_kernel(x_ref, attn_ref, ssm_ref, gw_ref, gb_ref, ga_ref, gs_ref, wo_ref, *ffn_refs,
                    final_norm):
    read_x = lambda: x_ref[...] + _mixed_update(attn_ref, ssm_ref, gw_ref, gb_ref, ga_ref, gs_ref,
                                                wo_ref)
    _ffn_body(read_x, *ffn_refs, final_norm=final_norm)


def _ffn(x, gain, w_gate, w_up, w_down, final_gain, final_norm, mix=None, proj=None):
    assert (mix is None) != (proj is None)
    t = x.shape[0]
    tile = FFN_TOKEN_TILE
    vec = lambda width: pl.BlockSpec((1, width), lambda i: (0, 0))
    resident = lambda r, c: pl.BlockSpec((r, c), lambda i: (0, 0), pipeline_mode=pl.Buffered(1))
    row = lambda width: pl.BlockSpec((tile, width), lambda i: (i, 0))
    hbm = pl.BlockSpec(memory_space=pl.ANY)
    in_specs, args = [row(D_MODEL)], [x]
    out_specs, out_shape = [row(D_MODEL)], [jax.ShapeDtypeStruct((t, D_MODEL), F32)]
    if mix is not None:
        attn, ssm_pre, glu_w, glu_b, attn_gain, ssm_gain, w_out = mix
        body = _mix_ffn_kernel
        in_specs += [row(ATTN_WIDTH), row(SSM_WIDTH),
                     resident(SSM_WIDTH, SSM_WIDTH), vec(SSM_WIDTH), vec(ATTN_WIDTH),
                     vec(SSM_WIDTH), resident(D_MODEL, D_MODEL)]
        args += [attn, ssm_pre, glu_w, glu_b.reshape(1, -1).astype(F32),
                 attn_gain.reshape(1, -1).astype(F32), ssm_gain.reshape(1, -1).astype(F32), w_out]
    if proj is not None:
        mixer_gain, w_in = proj
        body = _ffn_proj_kernel
        in_specs += [vec(D_MODEL), resident(D_MODEL, IN_WIDTH)]
        args += [mixer_gain.reshape(1, D_MODEL), w_in]
        for width, dtype in ((ATTN_WIDTH, BF16), (KV_WIDTH, BF16), (KV_WIDTH, BF16),
                             (SSM_WIDTH, F32)):
            out_specs.append(row(width))
            out_shape.append(jax.ShapeDtypeStruct((t, width), dtype))
    in_specs += [vec(D_MODEL), vec(D_MODEL), hbm, hbm, hbm]
    args += [gain.reshape(1, D_MODEL), final_gain.reshape(1, D_MODEL), w_gate, w_up, w_down]
    return pl.pallas_call(
        functools.partial(body, final_norm=final_norm),
        grid=(t // tile,),
        in_specs=in_specs,
        out_specs=out_specs,
        out_shape=out_shape,
        scratch_shapes=[pltpu.VMEM((D_MODEL, D_FF), BF16),
                        pltpu.VMEM((D_MODEL, D_FF), BF16),
                        pltpu.VMEM((D_FF, D_MODEL), BF16),
                        pltpu.VMEM((tile, D_FF), BF16),
                        pltpu.VMEM((2, 2, D_MODEL, FF_TILE), F32),
                        pltpu.VMEM((2, FF_TILE, D_MODEL), F32),
                        pltpu.SemaphoreType.DMA((3, 2))],
        compiler_params=pltpu.CompilerParams(
            dimension_semantics=("arbitrary",), vmem_limit_bytes=VMEM_LIMIT_BYTES),
    )(*args)


def _attn_kernel(sink_ref, q_ref, kp_ref, kc_ref, kn_ref, vp_ref, vc_ref, vn_ref, o_ref,
                 bias_ref, k_ref, v_ref, s_ref):
    n = pl.program_id(1)
    last = pl.num_programs(1) - 1
    nblk = ATTN_BLOCKS_PER_STEP

    @pl.when(n == 0)
    def _():
        kj = lax.broadcasted_iota(jnp.int32, (3 * BLOCK, BLOCK), 0)
        qi = lax.broadcasted_iota(jnp.int32, (3 * BLOCK, BLOCK), 1)
        rel = jnp.abs(kj - BLOCK - qi)
        dist = rel.astype(F32)
        inside = rel <= WINDOW
        has_prev = kj >= BLOCK
        has_next = kj < 2 * BLOCK
        for variant, ok in enumerate((inside & has_prev, inside, inside & has_next)):
            for h in range(ATTN_HEADS):
                slope = float(2.0 ** (-8.0 * (h + 1) / ATTN_HEADS))
                bias_ref[variant, h] = jnp.where(ok, -slope * dist, NEG_INF)

    k_ref[0:BLOCK] = kp_ref[...]
    k_ref[BLOCK:(nblk + 1) * BLOCK] = kc_ref[...]
    k_ref[(nblk + 1) * BLOCK:(nblk + 2) * BLOCK] = kn_ref[...]
    v_ref[0:BLOCK] = vp_ref[...]
    v_ref[BLOCK:(nblk + 1) * BLOCK] = vc_ref[...]
    v_ref[(nblk + 1) * BLOCK:(nblk + 2) * BLOCK] = vn_ref[...]

    for j in range(nblk):
        for kh in range(ATTN_KV_HEADS):
            kcat = k_ref[j * BLOCK:(j + 3) * BLOCK, kh * HEAD_DIM:(kh + 1) * HEAD_DIM]
            heads = [kh * Q_PER_KV + g for g in range(Q_PER_KV)]
            qs = jnp.concatenate(
                [q_ref[j * BLOCK:(j + 1) * BLOCK, h * HEAD_DIM:(h + 1) * HEAD_DIM] for h in heads],
                axis=0)
            s_ref[j, kh] = lax.dot_general(kcat, qs * (HEAD_DIM ** -0.5), NT_DIMS,
                                           preferred_element_type=F32)

    for j in range(nblk):
        variant = 1
        if j == 0:
            variant = jnp.where(n == 0, 0, variant)
        if j == nblk - 1:
            variant = jnp.where(n == last, 2, variant)
        outs = []
        for kh in range(ATTN_KV_HEADS):
            vcat = v_ref[j * BLOCK:(j + 3) * BLOCK, kh * HEAD_DIM:(kh + 1) * HEAD_DIM]
            for g in range(Q_PER_KV):
                h = kh * Q_PER_KV + g
                s = s_ref[j, kh, :, g * BLOCK:(g + 1) * BLOCK] + bias_ref[variant, h]
                sink = sink_ref[h]
                m = jnp.maximum(jnp.max(s, axis=0, keepdims=True), sink)
                e = jnp.exp(s - m)
                den = jnp.sum(e, axis=0, keepdims=True) + jnp.exp(sink - m)
                pv = lax.dot_general(vcat, e.astype(BF16), TN_DIMS,
                                     preferred_element_type=F32)
                outs.append(pv / den)
        o_ref[j * BLOCK:(j + 1) * BLOCK, :] = jnp.concatenate(outs, axis=0).T.astype(o_ref.dtype)


def _attention(q, k, v, sinks):
    b, seq, _ = q.shape
    nblk = ATTN_BLOCKS_PER_STEP
    steps = seq // (nblk * BLOCK)
    nb = seq // BLOCK
    assert seq % (nblk * BLOCK) == 0 and nb >= 2 and HEAD_DIM ** -0.5 == 2.0 ** -3
    edge = lambda f: pl.BlockSpec((None, BLOCK, KV_WIDTH), f)
    body = pl.BlockSpec((None, nblk * BLOCK, KV_WIDTH), lambda bi, n: (bi, n, 0))
    prev = lambda bi, n: (bi, jnp.maximum(n * nblk - 1, 0), 0)
    nxt = lambda bi, n: (bi, jnp.minimum((n + 1) * nblk, nb - 1), 0)
    rows = pl.BlockSpec((None, nblk * BLOCK, ATTN_WIDTH), lambda bi, n: (bi, n, 0))
    return pl.pallas_call(
        _attn_kernel,
        grid=(b, steps),
        in_specs=[pl.BlockSpec(memory_space=pltpu.SMEM), rows,
                  edge(prev), body, edge(nxt), edge(prev), body, edge(nxt)],
        out_specs=rows,
        out_shape=jax.ShapeDtypeStruct((b, seq, ATTN_WIDTH), BF16),
        scratch_shapes=[pltpu.VMEM((3, ATTN_HEADS, 3 * BLOCK, BLOCK), F32),
                        pltpu.VMEM(((nblk + 2) * BLOCK, KV_WIDTH), BF16),
                        pltpu.VMEM(((nblk + 2) * BLOCK, KV_WIDTH), BF16),
                        pltpu.VMEM((nblk, ATTN_KV_HEADS, 3 * BLOCK, Q_PER_KV * BLOCK), F32)],
        compiler_params=pltpu.CompilerParams(
            dimension_semantics=("parallel", "arbitrary"), vmem_limit_bytes=VMEM_LIMIT_BYTES),
    )(sinks.astype(F32), q, k, k, k, v, v, v)


def _dot_nt_split(a, b):
    a_hi, b_hi = a.astype(BF16), b.astype(BF16)
    a_lo = (a - a_hi.astype(F32)).astype(BF16)
    b_lo = (b - b_hi.astype(F32)).astype(BF16)
    dot = functools.partial(lax.dot_general, dimension_numbers=NT_DIMS,
                            preferred_element_type=F32)
    return dot(a_hi, b_hi) + dot(a_hi, b_lo) + dot(a_lo, b_hi)


def _ssm_ops_kernel(prm_ref, bc_ref, ein_ref, toep_ref, eout_ref, aq_ref):
    q, hc, p = SSM_CHUNK, SSM_CH, SSM_STATE
    fwd = lax.broadcasted_iota(jnp.int32, (1, 2 * p), 1) < p
    zero_row = jnp.zeros((1, 2 * p), F32)
    row_id = lax.broadcasted_iota(jnp.int32, (CHUNK_WIDTH, CHUNK_WIDTH), 0)
    col_id = lax.broadcasted_iota(jnp.int32, (CHUNK_WIDTH, CHUNK_WIDTH), 1)

    def table(select, n):
        picks = [select(m) for m in range(n)]
        re = jnp.concatenate([jnp.broadcast_to(r, (hc, 2 * p)) for r, _ in picks], axis=0)
        im = jnp.concatenate([jnp.broadcast_to(i, (hc, 2 * p)) for _, i in picks], axis=0)
        return re, im

    def tile_rows(x, n):
        return jnp.concatenate([x] * n, axis=0)

    for gi in range(prm_ref.shape[0]):
        lr = jnp.minimum(prm_ref[gi, 0:1, :], LAMBDA_RE_MAX)
        li = prm_ref[gi, 1:2, :]
        dt = jnp.exp(prm_ref[gi, 2:3, :])
        mag = jnp.exp(lr * dt)
        a_r = mag * jnp.cos(li * dt)
        a_i = mag * jnp.sin(li * dt)
        den = lr * lr + li * li
        coef_r = ((a_r - 1.0) * lr + a_i * li) / den
        coef_i = (a_i * lr - (a_r - 1.0) * li) / den
        b_r, b_i = bc_ref[gi, 0], bc_ref[gi, 1]
        c_r, c_i = bc_ref[gi, 2], bc_ref[gi, 3]
        bb_r = coef_r * b_r - coef_i * b_i
        bb_i = coef_r * b_i + coef_i * b_r

        pw = [(jnp.ones((1, 2 * p), F32), zero_row)]
        for _ in range(q):
            r, i = pw[-1]
            pw.append((r * a_r - i * a_i, r * a_i + i * a_r))

        def both(f_idx, b_idx):
            fr, fi = pw[f_idx] if f_idx is not None else (zero_row, zero_row)
            br, bi = pw[b_idx] if b_idx is not None else (zero_row, zero_row)
            return jnp.where(fwd, fr, br), jnp.where(fwd, fi, bi)

        p_r, p_i = table(lambda i: both(q - 1 - i, i), q)
        tb_r, tb_i = tile_rows(bb_r, q), tile_rows(bb_i, q)
        ein = jnp.concatenate([tb_r * p_r - tb_i * p_i, tb_r * p_i + tb_i * p_r], axis=1)
        ein_ref[gi] = ein.astype(BF16)

        p_r, p_i = table(lambda j: both(j + 1, q - j), q)
        tc_r, tc_i = tile_rows(c_r, q), tile_rows(c_i, q)
        eout = jnp.concatenate([tc_r * p_r - tc_i * p_i, -(tc_r * p_i + tc_i * p_r)], axis=1)
        eout_ref[gi] = eout.astype(BF16)

        def lag(m):
            return both(m - (q - 1) if q - 1 <= m <= 2 * q - 2 else None,
                        (q - 1) - m if m <= q - 1 else None)

        p_r, p_i = table(lag, 2 * q)
        tc_r, tc_i = tile_rows(c_r, 2 * q), tile_rows(c_i, 2 * q)
        cpt = jnp.concatenate([tc_r * p_r - tc_i * p_i, tc_r * p_i + tc_i * p_r], axis=1)
        bcat = jnp.concatenate([bb_r, -bb_i], axis=1)
        kern = _dot_nt_split(bcat, cpt)
        toep = jnp.concatenate(
            [kern[:, hc * (q - 1 - i):hc * (q - 1 - i) + CHUNK_WIDTH] for i in range(q)], axis=0)
        skip = jnp.concatenate([prm_ref[gi, 3:4, :]] * (CHUNK_WIDTH // (2 * p)), axis=1)
        toep_ref[gi] = (toep + jnp.where(row_id == col_id, skip, 0.0)).astype(BF16)

        aq_ref[gi] = jnp.concatenate(
            [pw[q][0], pw[q][1], jnp.zeros((SUBLANES - 2, 2 * p), F32)], axis=0)


def _ssm_operators(lam_re, lam_im, log_dt, b_re, b_im, c_re, c_im, d_skip):
    g, p, hc = SSM_GROUPS, SSM_STATE, SSM_CH
    lanes = lambda a: a.astype(F32).transpose(1, 0, 2).reshape(g, 2 * p)
    prm = jnp.stack(
        [lanes(lam_re), lanes(lam_im),
         jnp.repeat(log_dt.astype(F32).T, p, axis=1),
         jnp.tile(d_skip.astype(F32), (1, 2 * p // hc))]
        + [jnp.zeros((g, 2 * p), F32)] * (SUBLANES - 4), axis=1)
    bc = jnp.stack(
        [b_re.astype(F32).transpose(1, 3, 0, 2).reshape(g, hc, 2 * p),
         b_im.astype(F32).transpose(1, 3, 0, 2).reshape(g, hc, 2 * p),
         c_re.astype(F32).transpose(1, 2, 0, 3).reshape(g, hc, 2 * p),
         c_im.astype(F32).transpose(1, 2, 0, 3).reshape(g, hc, 2 * p)], axis=1)
    gs = OPS_GROUPS_PER_STEP
    mat = pl.BlockSpec((gs, CHUNK_WIDTH, CHUNK_WIDTH), lambda s: (s, 0, 0))
    mat_shape = jax.ShapeDtypeStruct((g, CHUNK_WIDTH, CHUNK_WIDTH), BF16)
    return pl.pallas_call(
        _ssm_ops_kernel,
        grid=(g // gs,),
        in_specs=[pl.BlockSpec((gs, SUBLANES, 2 * p), lambda s: (s, 0, 0)),
                  pl.BlockSpec((gs, 4, hc, 2 * p), lambda s: (s, 0, 0, 0))],
        out_specs=[mat, mat, mat, pl.BlockSpec((gs, SUBLANES, 2 * p), lambda s: (s, 0, 0))],
        out_shape=[mat_shape, mat_shape, mat_shape,
                   jax.ShapeDtypeStruct((g, SUBLANES, 2 * p), F32)],
        compiler_params=pltpu.CompilerParams(
            dimension_semantics=("parallel",), vmem_limit_bytes=VMEM_LIMIT_BYTES),
    )(prm, bc)


def _lane_roll(x, shift):
    shift %= LANES
    return jnp.concatenate([x[:, LANES - shift:], x[:, :LANES - shift]], axis=1)


def _block_transpose(v, lane_block):
    v = list(v)
    n = len(v)
    d = n // 2
    while d >= 1:
        upper = (lane_block & d) != 0
        for i in range(n):
            if i & d == 0:
                a, b = v[i], v[i + d]
                v[i] = jnp.where(upper, _lane_roll(b, SSM_CH * d), a)
                v[i + d] = jnp.where(upper, b, _lane_roll(a, -SSM_CH * d))
        d //= 2
    return v


def _ssm_kernel(u_hbm, ein_ref, toep_ref, eout_ref, aq_ref, y_hbm,
                io_ref, ug_ref, s_ref, x_ref, in_sem, out_sem, *, batch, seq, pitch):
    q, half = SSM_CHUNK, SSM_STATE
    n_chunks = seq // q
    gpb = GROUPS_PER_BLOCK
    slab = 2 * batch
    lane_block = lax.broadcasted_iota(jnp.int32, (slab, LANES), 1) // SSM_CH
    step_id = pl.program_id(0)
    last_step = pl.num_programs(0) - 1
    slot = lax.rem(step_id, 2)

    def column_copies(block, slot_, to_vmem):
        lanes = pl.ds(pl.multiple_of(block * LANES, LANES), LANES)
        copies = []
        for b in range(batch):
            hbm = (u_hbm if to_vmem else y_hbm).at[b, :, lanes]
            vmem = io_ref.at[slot_, pl.ds(b * pitch, seq), :]
            if to_vmem:
                copies.append(pltpu.make_async_copy(hbm, vmem, in_sem.at[slot_, b]))
            else:
                copies.append(pltpu.make_async_copy(vmem, hbm, out_sem.at[slot_, b]))
        return copies

    def start(copies):
        for copy in copies:
            copy.start()

    def wait(copies):
        for copy in copies:
            copy.wait()

    @pl.when(step_id == 0)
    def _():
        start(column_copies(0, 0, True))

    @pl.when(step_id < last_step)
    def _():
        @pl.when(step_id >= 1)
        def _():
            wait(column_copies(step_id - 1, 1 - slot, False))
        start(column_copies(step_id + 1, 1 - slot, True))

    wait(column_copies(step_id, slot, True))
    buf = io_ref.at[slot]

    def gather(cp, carry):
        rows = pl.ds(pl.multiple_of(cp * slab, slab), slab)
        for part in range(q // gpb):
            t0 = cp * 2 * q + part * gpb
            v = [jnp.concatenate([buf[pl.ds(t0 + i, batch, stride=pitch), :],
                                  buf[pl.ds(t0 + q + i, batch, stride=pitch), :]],
                                 axis=0).astype(BF16) for i in range(gpb)]
            w = _block_transpose(v, lane_block)
            for g in range(gpb):
                ug_ref[g, rows, part * LANES:(part + 1) * LANES] = w[g]
        return carry

    lax.fori_loop(0, n_chunks // 2, gather, 0, unroll=SSM_RELAYOUT_UNROLL)

    fwd_lane = lax.broadcasted_iota(jnp.int32, (batch, 2 * half), 1) < half
    zeros = jnp.zeros((batch, half), F32)
    last_rows = pl.ds((n_chunks - 1) * batch, batch)
    ni = SSM_INTERLEAVE
    for g0 in range(0, gpb, ni):
        for gi in range(ni):
            s_ref[gi] = jnp.dot(ug_ref[g0 + gi], ein_ref[g0 + gi], preferred_element_type=F32)
            x_ref[gi, 0:batch, 0:half] = zeros
            x_ref[gi, 0:batch, 2 * half:3 * half] = zeros
            x_ref[gi, last_rows, half:2 * half] = zeros
            x_ref[gi, last_rows, 3 * half:4 * half] = zeros

        def step(k, carry):
            rf = pl.multiple_of(k * batch, batch)
            rb = pl.multiple_of((n_chunks - 1 - k) * batch, batch)
            new = []
            for gi in range(ni):
                xr, xi = carry[gi]
                sre = jnp.where(fwd_lane, s_ref[gi, pl.ds(rf, batch), 0:2 * half],
                                s_ref[gi, pl.ds(rb, batch), 0:2 * half])
                sim = jnp.where(fwd_lane, s_ref[gi, pl.ds(rf, batch), 2 * half:4 * half],
                                s_ref[gi, pl.ds(rb, batch), 2 * half:4 * half])
                ar = aq_ref[g0 + gi, 0:1, :]
                ai = aq_ref[g0 + gi, 1:2, :]
                nr = ar * xr - ai * xi + sre
                nim = ar * xi + ai * xr + sim
                x_ref[gi, pl.ds(rf + batch, batch), 0:half] = nr[:, 0:half]
                x_ref[gi, pl.ds(rf + batch, batch), 2 * half:3 * half] = nim[:, 0:half]
                x_ref[gi, pl.ds(rb - batch, batch), half:2 * half] = nr[:, half:2 * half]
                x_ref[gi, pl.ds(rb - batch, batch), 3 * half:4 * half] = nim[:, half:2 * half]
                new.append((nr, nim))
            return tuple(new)

        init = tuple((jnp.zeros((batch, 2 * half), F32), jnp.zeros((batch, 2 * half), F32))
                     for _ in range(ni))
        lax.fori_loop(0, n_chunks - 1, step, init)

        for gi in range(ni):
            g = g0 + gi
            y = jnp.dot(ug_ref[g], toep_ref[g], preferred_element_type=F32)
            y += lax.dot_general(x_ref[gi].astype(BF16), eout_ref[g], NT_DIMS,
                                 preferred_element_type=F32)
            ug_ref[g] = y.astype(BF16)

    def scatter(cp, carry):
        rows = pl.ds(pl.multiple_of(cp * slab, slab), slab)
        for part in range(q // gpb):
            t0 = cp * 2 * q + part * gpb
            v = [ug_ref[g, rows, part * LANES:(part + 1) * LANES] for g in range(gpb)]
            w = _block_transpose(v, lane_block)
            for j in range(gpb):
                wj = w[j].astype(F32)
                buf[pl.ds(t0 + j, batch, stride=pitch), :] = wj[0:batch]
                buf[pl.ds(t0 + q + j, batch, stride=pitch), :] = wj[batch:slab]
        return carry

    lax.fori_loop(0, n_chunks // 2, scatter, 0, unroll=SSM_RELAYOUT_UNROLL)

    start(column_copies(step_id, slot, False))

    @pl.when(step_id == last_step)
    def _():
        @pl.when(step_id >= 1)
        def _():
            wait(column_copies(step_id - 1, 1 - slot, False))
        wait(column_copies(step_id, slot, False))


def _padded_seq(seq):
    return seq + SUBLANES if seq % (2 * SUBLANES) == 0 else seq


def _ssm(u, lam_re, lam_im, log_dt, b_re, b_im, c_re, c_im, d_skip):
    batch, seq, _ = u.shape
    pitch = _padded_seq(seq)
    rows = batch * seq // SSM_CHUNK
    ein, toep, eout, a_q = _ssm_operators(lam_re, lam_im, log_dt, b_re, b_im, c_re, c_im, d_skip)
    gpb = GROUPS_PER_BLOCK
    mat = pl.BlockSpec((gpb, CHUNK_WIDTH, CHUNK_WIDTH), lambda s: (s, 0, 0))
    hbm = pl.BlockSpec(memory_space=pl.ANY)
    return pl.pallas_call(
        functools.partial(_ssm_kernel, batch=batch, seq=seq, pitch=pitch),
        grid=(SSM_GROUPS // gpb,),
        in_specs=[hbm, mat, mat, mat,
                  pl.BlockSpec((gpb, SUBLANES, 2 * SSM_STATE), lambda s: (s, 0, 0))],
        out_specs=hbm,
        out_shape=jax.ShapeDtypeStruct((batch, seq, SSM_WIDTH), F32),
        scratch_shapes=[pltpu.VMEM((2, batch * pitch, LANES), F32),
                        pltpu.VMEM((gpb, rows, CHUNK_WIDTH), BF16),
                        pltpu.VMEM((SSM_INTERLEAVE, rows, 4 * SSM_STATE), F32),
                        pltpu.VMEM((SSM_INTERLEAVE, rows, 4 * SSM_STATE), F32),
                        pltpu.SemaphoreType.DMA((2, batch)),
                        pltpu.SemaphoreType.DMA((2, batch))],
        compiler_params=pltpu.CompilerParams(
            dimension_semantics=("arbitrary",), vmem_limit_bytes=VMEM_LIMIT_BYTES),
    )(u, ein, toep, eout, a_q)


def kernel(x, norm_ffn1, ffn1_w_gate, ffn1_w_up, ffn1_w_down, norm_mix, w_in, attn_sinks,
           ssm_lambda_re, ssm_lambda_im, ssm_log_dt, ssm_b_re, ssm_b_im, ssm_c_re, ssm_c_im,
           ssm_d, ssm_glu_w, ssm_glu_b, attn_out_norm, ssm_out_norm, w_out,
           norm_ffn2, ffn2_w_gate, ffn2_w_up, ffn2_w_down, final_norm):
    b, seq, d = x.shape
    depth = norm_ffn1.shape[0]
    assert d == D_MODEL and seq % BLOCK == 0 and (b * seq) % FFN_TOKEN_TILE == 0
    assert b == SUBLANES and seq % (2 * SSM_CHUNK * SSM_RELAYOUT_UNROLL) == 0
    h = x.reshape(b * seq, d).astype(F32)
    for l in range(depth):
        h, q, k, v, u = _ffn(h, norm_ffn1[l], ffn1_w_gate[l], ffn1_w_up[l], ffn1_w_down[l],
                             final_norm, final_norm=False, proj=(norm_mix[l], w_in[l]))
        attn = _attention(q.reshape(b, seq, -1), k.reshape(b, seq, -1), v.reshape(b, seq, -1),
                          attn_sinks[l])
        ssm_pre = _ssm(u.reshape(b, seq, -1), ssm_lambda_re[l], ssm_lambda_im[l], ssm_log_dt[l],
                       ssm_b_re[l], ssm_b_im[l], ssm_c_re[l], ssm_c_im[l], ssm_d[l])
        mix = (attn.reshape(b * seq, -1), ssm_pre.reshape(b * seq, -1), ssm_glu_w[l],
               ssm_glu_b[l], attn_out_norm[l], ssm_out_norm[l], w_out[l])
        h, = _ffn(h, norm_ffn2[l], ffn2_w_gate[l], ffn2_w_up[l], ffn2_w_down[l],
                  final_norm, final_norm=(l == depth - 1), mix=mix)
    return h.reshape(b, seq, d).astype(x.dtype)
```

```python
import functools

import jax
import jax.numpy as jnp
from jax import lax
from jax.experimental import pallas as pl
from jax.experimental.pallas import tpu as pltpu

F32 = jnp.float32
BF16 = jnp.bfloat16

D_MODEL = 1024
ATTN_HEADS = 8
ATTN_KV_HEADS = 2
Q_PER_KV = ATTN_HEADS // ATTN_KV_HEADS
HEAD_DIM = 64
ATTN_WIDTH = ATTN_HEADS * HEAD_DIM
KV_WIDTH = ATTN_KV_HEADS * HEAD_DIM
WINDOW = 128
BLOCK = 128
SSM_CH = 16
SSM_WIDTH = D_MODEL - ATTN_WIDTH
SSM_GROUPS = SSM_WIDTH // SSM_CH
SSM_STATE = 64
IN_WIDTH = ATTN_WIDTH + 2 * KV_WIDTH + SSM_WIDTH
D_FF = 2816
EPS = 1e-6
NEG_INF = -1e30
LAMBDA_RE_MAX = -1e-4

LANES = 128
SUBLANES = 8
VMEM_LIMIT_BYTES = 56 * 1024 * 1024

FFN_TOKEN_TILE = 512
FFN_SUBTILES = 2
FF_TILE = 256
ATTN_BLOCKS_PER_STEP = 4
SSM_CHUNK = 16
CHUNK_WIDTH = SSM_CHUNK * SSM_CH
GROUPS_PER_BLOCK = LANES // SSM_CH
SSM_INTERLEAVE = 4
SSM_RELAYOUT_UNROLL = 4
OPS_GROUPS_PER_STEP = 4

NT_DIMS = (((1,), (1,)), ((), ()))
TN_DIMS = (((0,), (0,)), ((), ()))


def _rms(x):
    return x * lax.rsqrt(jnp.mean(x * x, axis=-1, keepdims=True) + EPS)


def _mixed_update(rows, attn_ref, ssm_ref, gw_ref, gb_ref, ga_ref, gs_ref, wo_ref):
    y = jax.nn.gelu(ssm_ref[rows, :])
    z = jnp.dot(y.astype(BF16), gw_ref[...].astype(BF16), preferred_element_type=F32) + gb_ref[...]
    s = y * jax.nn.sigmoid(z)
    sn = _rms(s) * gs_ref[...]
    an = _rms(attn_ref[rows, :].astype(F32)) * ga_ref[...]
    mixed = jnp.concatenate([an, sn], axis=-1).astype(BF16)
    return jnp.dot(mixed, wo_ref[...].astype(BF16), preferred_element_type=F32)


def _mixer_inputs(y, rows, gain_ref, w_ref, q_ref, k_ref, v_ref, u_ref):
    hn = (_rms(y) * gain_ref[...]).astype(BF16)
    proj = jnp.dot(hn, w_ref[...].astype(BF16), preferred_element_type=F32)
    q_ref[rows, :] = proj[:, :ATTN_WIDTH].astype(BF16)
    k_ref[rows, :] = proj[:, ATTN_WIDTH:ATTN_WIDTH + KV_WIDTH].astype(BF16)
    v_ref[rows, :] = proj[:, ATTN_WIDTH + KV_WIDTH:ATTN_WIDTH + 2 * KV_WIDTH].astype(BF16)
    u_ref[rows, :] = proj[:, ATTN_WIDTH + 2 * KV_WIDTH:]


def _ffn_body(read_x, gain_ref, fgain_ref, wg_hbm, wu_hbm, wd_hbm, o_ref,
              wg_ref, wu_ref, wd_ref, act_ref, gu_stage, d_stage, sem, *, final_norm,
              epilogue=None):
    nj = D_FF // FF_TILE

    def weight_copies(j, slot):
        span = pl.ds(j * FF_TILE, FF_TILE)
        return (pltpu.make_async_copy(wg_hbm.at[:, span], gu_stage.at[0, slot], sem.at[0, slot]),
                pltpu.make_async_copy(wu_hbm.at[:, span], gu_stage.at[1, slot], sem.at[1, slot]),
                pltpu.make_async_copy(wd_hbm.at[span, :], d_stage.at[slot], sem.at[2, slot]))

    def step(stage_weights):
        if stage_weights:
            for copy in weight_copies(0, 0):
                copy.start()
        sub = o_ref.shape[0] // FFN_SUBTILES
        halves = [slice(h * sub, (h + 1) * sub) for h in range(FFN_SUBTILES)]
        xs = [read_x(rows) for rows in halves]
        hns = [(_rms(x) * gain_ref[...]).astype(BF16) for x in xs]
        for j in range(nj):
            cols = slice(j * FF_TILE, (j + 1) * FF_TILE)
            if stage_weights:
                slot = j % 2
                if j + 1 < nj:
                    for copy in weight_copies(j + 1, 1 - slot):
                        copy.start()
                for copy in weight_copies(j, slot):
                    copy.wait()
                wg_ref[:, cols] = gu_stage[0, slot].astype(BF16)
                wu_ref[:, cols] = gu_stage[1, slot].astype(BF16)
                wd_ref[cols, :] = d_stage[slot].astype(BF16)
            for rows, hn in zip(halves, hns):
                g = jnp.dot(hn, wg_ref[:, cols], preferred_element_type=F32)
                u = jnp.dot(hn, wu_ref[:, cols], preferred_element_type=F32)
                act_ref[rows, cols] = (g * jax.nn.sigmoid(g) * u).astype(BF16)
        for rows, x in zip(halves, xs):
            y = x + 0.5 * jnp.dot(act_ref[rows, :], wd_ref[...], preferred_element_type=F32)
            if final_norm:
                y = _rms(y) * fgain_ref[...]
            o_ref[rows, :] = y
            if epilogue is not None:
                epilogue(y, rows)

    pl.when(pl.program_id(0) == 0)(functools.partial(step, True))
    pl.when(pl.program_id(0) != 0)(functools.partial(step, False))


def _ffn_proj_kernel(x_ref, mgain_ref, win_ref, gain_ref, fgain_ref, wg_hbm, wu_hbm, wd_hbm,
                     o_ref, q_ref, k_ref, v_ref, u_ref, *scratch, final_norm):
    emit = lambda y, rows: _mixer_inputs(y, rows, mgain_ref, win_ref, q_ref, k_ref, v_ref, u_ref)
    _ffn_body(lambda rows: x_ref[rows, :], gain_ref, fgain_ref, wg_hbm, wu_hbm, wd_hbm, o_ref,
              *scratch, final_norm=final_norm, epilogue=emit)


def _mix_ffn_kernel(x_ref, attn_ref, ssm_ref, gw_ref, gb_ref, ga_ref, gs_ref, wo_ref, *ffn_refs,
                    final_norm):
    read_x = lambda rows: x_ref[rows, :] + _mixed_update(rows, attn_ref, ssm_ref, gw_ref, gb_ref,
                                                         ga_ref, gs_ref, wo_ref)
    _ffn_body(read_x, *ffn_refs, final_norm=final_norm)


def _ffn(x, gain, w_gate, w_up, w_down, final_gain, final_norm, mix=None, proj=None):
    assert (mix is None) != (proj is None)
    t = x.shape[0]
    tile = FFN_TOKEN_TILE
    vec = lambda width: pl.BlockSpec((1, width), lambda i: (0, 0))
    resident = lambda r, c: pl.BlockSpec((r, c), lambda i: (0, 0), pipeline_mode=pl.Buffered(1))
    row = lambda width: pl.BlockSpec((tile, width), lambda i: (i, 0))
    hbm = pl.BlockSpec(memory_space=pl.ANY)
    in_specs, args = [row(D_MODEL)], [x]
    out_specs, out_shape = [row(D_MODEL)], [jax.ShapeDtypeStruct((t, D_MODEL), F32)]
    if mix is not None:
        attn, ssm_pre, glu_w, glu_b, attn_gain, ssm_gain, w_out = mix
        body = _mix_ffn_kernel
        in_specs += [row(ATTN_WIDTH), row(SSM_WIDTH),
                     resident(SSM_WIDTH, SSM_WIDTH), vec(SSM_WIDTH), vec(ATTN_WIDTH),
                     vec(SSM_WIDTH), resident(D_MODEL, D_MODEL)]
        args += [attn, ssm_pre, glu_w, glu_b.reshape(1, -1).astype(F32),
                 attn_gain.reshape(1, -1).astype(F32), ssm_gain.reshape(1, -1).astype(F32), w_out]
    if proj is not None:
        mixer_gain, w_in = proj
        body = _ffn_proj_kernel
        in_specs += [vec(D_MODEL), resident(D_MODEL, IN_WIDTH)]
        args += [mixer_gain.reshape(1, D_MODEL), w_in]
        for width, dtype in ((ATTN_WIDTH, BF16), (KV_WIDTH, BF16), (KV_WIDTH, BF16),
                             (SSM_WIDTH, F32)):
            out_specs.append(row(width))
            out_shape.append(jax.ShapeDtypeStruct((t, width), dtype))
    in_specs += [vec(D_MODEL), vec(D_MODEL), hbm, hbm, hbm]
    args += [gain.reshape(1, D_MODEL), final_gain.reshape(1, D_MODEL), w_gate, w_up, w_down]
    return pl.pallas_call(
        functools.partial(body, final_norm=final_norm),
        grid=(t // tile,),
        in_specs=in_specs,
        out_specs=out_specs,
        out_shape=out_shape,
        scratch_shapes=[pltpu.VMEM((D_MODEL, D_FF), BF16),
                        pltpu.VMEM((D_MODEL, D_FF), BF16),
                        pltpu.VMEM((D_FF, D_MODEL), BF16),
                        pltpu.VMEM((tile, D_FF), BF16),
                        pltpu.VMEM((2, 2, D_MODEL, FF_TILE), F32),
                        pltpu.VMEM((2, FF_TILE, D_MODEL), F32),
                        pltpu.SemaphoreType.DMA((3, 2))],
        compiler_params=pltpu.CompilerParams(
            dimension_semantics=("arbitrary",), vmem_limit_bytes=VMEM_LIMIT_BYTES),
    )(*args)


def _attn_kernel(sink_ref, q_ref, kp_ref, kc_ref, kn_ref, vp_ref, vc_ref, vn_ref, o_ref,
                 bias_ref, k_ref, v_ref, s_ref):
    n = pl.program_id(1)
    last = pl.num_programs(1) - 1
    nblk = ATTN_BLOCKS_PER_STEP

    @pl.when(n == 0)
    def _():
        kj = lax.broadcasted_iota(jnp.int32, (3 * BLOCK, BLOCK), 0)
        qi = lax.broadcasted_iota(jnp.int32, (3 * BLOCK, BLOCK), 1)
        rel = jnp.abs(kj - BLOCK - qi)
        dist = rel.astype(F32)
        inside = rel <= WINDOW
        has_prev = kj >= BLOCK
        has_next = kj < 2 * BLOCK
        for variant, ok in enumerate((inside & has_prev, inside, inside & has_next)):
            for h in range(ATTN_HEADS):
                slope = float(2.0 ** (-8.0 * (h + 1) / ATTN_HEADS))
                bias_ref[variant, h] = jnp.where(ok, -slope * dist, NEG_INF)

    k_ref[0:BLOCK] = kp_ref[...]
    k_ref[BLOCK:(nblk + 1) * BLOCK] = kc_ref[...]
    k_ref[(nblk + 1) * BLOCK:(nblk + 2) * BLOCK] = kn_ref[...]
    v_ref[0:BLOCK] = vp_ref[...]
    v_ref[BLOCK:(nblk + 1) * BLOCK] = vc_ref[...]
    v_ref[(nblk + 1) * BLOCK:(nblk + 2) * BLOCK] = vn_ref[...]

    for j in range(nblk):
        for kh in range(ATTN_KV_HEADS):
            kcat = k_ref[j * BLOCK:(j + 3) * BLOCK, kh * HEAD_DIM:(kh + 1) * HEAD_DIM]
            heads = [kh * Q_PER_KV + g for g in range(Q_PER_KV)]
            qs = jnp.concatenate(
                [q_ref[j * BLOCK:(j + 1) * BLOCK, h * HEAD_DIM:(h + 1) * HEAD_DIM] for h in heads],
                axis=0)
            s_ref[j, kh] = lax.dot_general(kcat, qs * (HEAD_DIM ** -0.5), NT_DIMS,
                                           preferred_element_type=F32)

    for j in range(nblk):
        variant = 1
        if j == 0:
            variant = jnp.where(n == 0, 0, variant)
        if j == nblk - 1:
            variant = jnp.where(n == last, 2, variant)
        outs = []
        for kh in range(ATTN_KV_HEADS):
            vcat = v_ref[j * BLOCK:(j + 3) * BLOCK, kh * HEAD_DIM:(kh + 1) * HEAD_DIM]
            for g in range(Q_PER_KV):
                h = kh * Q_PER_KV + g
                s = s_ref[j, kh, :, g * BLOCK:(g + 1) * BLOCK] + bias_ref[variant, h]
                sink = sink_ref[h]
                m = jnp.maximum(jnp.max(s, axis=0, keepdims=True), sink)
                e = jnp.exp(s - m)
                den = jnp.sum(e, axis=0, keepdims=True) + jnp.exp(sink - m)
                pv = lax.dot_general(vcat, e.astype(BF16), TN_DIMS,
                                     preferred_element_type=F32)
                outs.append(pv / den)
        o_ref[j * BLOCK:(j + 1) * BLOCK, :] = jnp.concatenate(outs, axis=0).T.astype(o_ref.dtype)


def _attention(q, k, v, sinks):
    b, seq, _ = q.shape
    nblk = ATTN_BLOCKS_PER_STEP
    steps = seq // (nblk * BLOCK)
    nb = seq // BLOCK
    assert seq % (nblk * BLOCK) == 0 and nb >= 2 and HEAD_DIM ** -0.5 == 2.0 ** -3
    edge = lambda f: pl.BlockSpec((None, BLOCK, KV_WIDTH), f)
    body = pl.BlockSpec((None, nblk * BLOCK, KV_WIDTH), lambda bi, n: (bi, n, 0))
    prev = lambda bi, n: (bi, jnp.maximum(n * nblk - 1, 0), 0)
    nxt = lambda bi, n: (bi, jnp.minimum((n + 1) * nblk, nb - 1), 0)
    rows = pl.BlockSpec((None, nblk * BLOCK, ATTN_WIDTH), lambda bi, n: (bi, n, 0))
    return pl.pallas_call(
        _attn_kernel,
        grid=(b, steps),
        in_specs=[pl.BlockSpec(memory_space=pltpu.SMEM), rows,
                  edge(prev), body, edge(nxt), edge(prev), body, edge(nxt)],
        out_specs=rows,
        out_shape=jax.ShapeDtypeStruct((b, seq, ATTN_WIDTH), BF16),
        scratch_shapes=[pltpu.VMEM((3, ATTN_HEADS, 3 * BLOCK, BLOCK), F32),
                        pltpu.VMEM(((nblk + 2) * BLOCK, KV_WIDTH), BF16),
                        pltpu.VMEM(((nblk + 2) * BLOCK, KV_WIDTH), BF16),
                        pltpu.VMEM((nblk, ATTN_KV_HEADS, 3 * BLOCK, Q_PER_KV * BLOCK), F32)],
        compiler_params=pltpu.CompilerParams(
            dimension_semantics=("parallel", "arbitrary"), vmem_limit_bytes=VMEM_LIMIT_BYTES),
    )(sinks.astype(F32), q, k, k, k, v, v, v)


def _dot_nt_split(a, b):
    a_hi, b_hi = a.astype(BF16), b.astype(BF16)
    a_lo = (a - a_hi.astype(F32)).astype(BF16)
    b_lo = (b - b_hi.astype(F32)).astype(BF16)
    dot = functools.partial(lax.dot_general, dimension_numbers=NT_DIMS,
                            preferred_element_type=F32)
    return dot(a_hi, b_hi) + dot(a_hi, b_lo) + dot(a_lo, b_hi)


def _ssm_ops_kernel(prm_ref, bc_ref, ein_ref, toep_ref, eout_ref, aq_ref):
    q, hc, p = SSM_CHUNK, SSM_CH, SSM_STATE
    fwd = lax.broadcasted_iota(jnp.int32, (1, 2 * p), 1) < p
    zero_row = jnp.zeros((1, 2 * p), F32)
    row_id = lax.broadcasted_iota(jnp.int32, (CHUNK_WIDTH, CHUNK_WIDTH), 0)
    col_id = lax.broadcasted_iota(jnp.int32, (CHUNK_WIDTH, CHUNK_WIDTH), 1)

    def table(select, n):
        picks = [select(m) for m in range(n)]
        re = jnp.concatenate([jnp.broadcast_to(r, (hc, 2 * p)) for r, _ in picks], axis=0)
        im = jnp.concatenate([jnp.broadcast_to(i, (hc, 2 * p)) for _, i in picks], axis=0)
        return re, im

    def tile_rows(x, n):
        return jnp.concatenate([x] * n, axis=0)

    for gi in range(prm_ref.shape[0]):
        lr = jnp.minimum(prm_ref[gi, 0:1, :], LAMBDA_RE_MAX)
        li = prm_ref[gi, 1:2, :]
        dt = jnp.exp(prm_ref[gi, 2:3, :])
        mag = jnp.exp(lr * dt)
        a_r = mag * jnp.cos(li * dt)
        a_i = mag * jnp.sin(li * dt)
        den = lr * lr + li * li
        coef_r = ((a_r - 1.0) * lr + a_i * li) / den
        coef_i = (a_i * lr - (a_r - 1.0) * li) / den
        b_r, b_i = bc_ref[gi, 0], bc_ref[gi, 1]
        c_r, c_i = bc_ref[gi, 2], bc_ref[gi, 3]
        bb_r = coef_r * b_r - coef_i * b_i
        bb_i = coef_r * b_i + coef_i * b_r

        pw = [(jnp.ones((1, 2 * p), F32), zero_row)]
        for _ in range(q):
            r, i = pw[-1]
            pw.append((r * a_r - i * a_i, r * a_i + i * a_r))

        def both(f_idx, b_idx):
            fr, fi = pw[f_idx] if f_idx is not None else (zero_row, zero_row)
            br, bi = pw[b_idx] if b_idx is not None else (zero_row, zero_row)
            return jnp.where(fwd, fr, br), jnp.where(fwd, fi, bi)

        p_r, p_i = table(lambda i: both(q - 1 - i, i), q)
        tb_r, tb_i = tile_rows(bb_r, q), tile_rows(bb_i, q)
        ein = jnp.concatenate([tb_r * p_r - tb_i * p_i, tb_r * p_i + tb_i * p_r], axis=1)
        ein_ref[gi] = ein.astype(BF16)

        p_r, p_i = table(lambda j: both(j + 1, q - j), q)
        tc_r, tc_i = tile_rows(c_r, q), tile_rows(c_i, q)
        eout = jnp.concatenate([tc_r * p_r - tc_i * p_i, -(tc_r * p_i + tc_i * p_r)], axis=1)
        eout_ref[gi] = eout.astype(BF16)

        def lag(m):
            return both(m - (q - 1) if q - 1 <= m <= 2 * q - 2 else None,
                        (q - 1) - m if m <= q - 1 else None)

        p_r, p_i = table(lag, 2 * q)
        tc_r, tc_i = tile_rows(c_r, 2 * q), tile_rows(c_i, 2 * q)
        cpt = jnp.concatenate([tc_r * p_r - tc_i * p_i, tc_r * p_i + tc_i * p_r], axis=1)
        bcat = jnp.concatenate([bb_r, -bb_i], axis=1)
        kern = _dot_nt_split(bcat, cpt)
        toep = jnp.concatenate(
            [kern[:, hc * (q - 1 - i):hc * (q - 1 - i) + CHUNK_WIDTH] for i in range(q)], axis=0)
        skip = jnp.concatenate([prm_ref[gi, 3:4, :]] * (CHUNK_WIDTH // (2 * p)), axis=1)
        toep_ref[gi] = (toep + jnp.where(row_id == col_id, skip, 0.0)).astype(BF16)

        aq_ref[gi] = jnp.concatenate(
            [pw[q][0], pw[q][1], jnp.zeros((SUBLANES - 2, 2 * p), F32)], axis=0)


def _ssm_operators(lam_re, lam_im, log_dt, b_re, b_im, c_re, c_im, d_skip):
    g, p, hc = SSM_GROUPS, SSM_STATE, SSM_CH
    lanes = lambda a: a.astype(F32).transpose(1, 0, 2).reshape(g, 2 * p)
    prm = jnp.stack(
        [lanes(lam_re), lanes(lam_im),
         jnp.repeat(log_dt.astype(F32).T, p, axis=1),
         jnp.tile(d_skip.astype(F32), (1, 2 * p // hc))]
        + [jnp.zeros((g, 2 * p), F32)] * (SUBLANES - 4), axis=1)
    bc = jnp.stack(
        [b_re.astype(F32).transpose(1, 3, 0, 2).reshape(g, hc, 2 * p),
         b_im.astype(F32).transpose(1, 3, 0, 2).reshape(g, hc, 2 * p),
         c_re.astype(F32).transpose(1, 2, 0, 3).reshape(g, hc, 2 * p),
         c_im.astype(F32).transpose(1, 2, 0, 3).reshape(g, hc, 2 * p)], axis=1)
    gs = OPS_GROUPS_PER_STEP
    mat = pl.BlockSpec((gs, CHUNK_WIDTH, CHUNK_WIDTH), lambda s: (s, 0, 0))
    mat_shape = jax.ShapeDtypeStruct((g, CHUNK_WIDTH, CHUNK_WIDTH), BF16)
    return pl.pallas_call(
        _ssm_ops_kernel,
        grid=(g // gs,),
        in_specs=[pl.BlockSpec((gs, SUBLANES, 2 * p), lambda s: (s, 0, 0)),
                  pl.BlockSpec((gs, 4, hc, 2 * p), lambda s: (s, 0, 0, 0))],
        out_specs=[mat, mat, mat, pl.BlockSpec((gs, SUBLANES, 2 * p), lambda s: (s, 0, 0))],
        out_shape=[mat_shape, mat_shape, mat_shape,
                   jax.ShapeDtypeStruct((g, SUBLANES, 2 * p), F32)],
        compiler_params=pltpu.CompilerParams(
            dimension_semantics=("parallel",), vmem_limit_bytes=VMEM_LIMIT_BYTES),
    )(prm, bc)


def _lane_roll(x, shift):
    shift %= LANES
    return jnp.concatenate([x[:, LANES - shift:], x[:, :LANES - shift]], axis=1)


def _block_transpose(v, lane_block):
    v = list(v)
    n = len(v)
    d = n // 2
    while d >= 1:
        upper = (lane_block & d) != 0
        for i in range(n):
            if i & d == 0:
                a, b = v[i], v[i + d]
                v[i] = jnp.where(upper, _lane_roll(b, SSM_CH * d), a)
                v[i + d] = jnp.where(upper, b, _lane_roll(a, -SSM_CH * d))
        d //= 2
    return v


def _ssm_kernel(u_hbm, ein_ref, toep_ref, eout_ref, aq_ref, y_hbm,
                io_ref, ug_ref, s_ref, x_ref, in_sem, out_sem, *, batch, seq, pitch):
    q, half = SSM_CHUNK, SSM_STATE
    n_chunks = seq // q
    gpb = GROUPS_PER_BLOCK
    slab = 2 * batch
    lane_block = lax.broadcasted_iota(jnp.int32, (slab, LANES), 1) // SSM_CH
    step_id = pl.program_id(0)
    last_step = pl.num_programs(0) - 1
    slot = lax.rem(step_id, 2)

    def column_copies(block, slot_, to_vmem):
        lanes = pl.ds(pl.multiple_of(block * LANES, LANES), LANES)
        copies = []
        for b in range(batch):
            hbm = (u_hbm if to_vmem else y_hbm).at[b, :, lanes]
            vmem = io_ref.at[slot_, pl.ds(b * pitch, seq), :]
            if to_vmem:
                copies.append(pltpu.make_async_copy(hbm, vmem, in_sem.at[slot_, b]))
            else:
                copies.append(pltpu.make_async_copy(vmem, hbm, out_sem.at[slot_, b]))
        return copies

    def start(copies):
        for copy in copies:
            copy.start()

    def wait(copies):
        for copy in copies:
            copy.wait()

    @pl.when(step_id == 0)
    def _():
        start(column_copies(0, 0, True))

    @pl.when(step_id < last_step)
    def _():
        @pl.when(step_id >= 1)
        def _():
            wait(column_copies(step_id - 1, 1 - slot, False))
        start(column_copies(step_id + 1, 1 - slot, True))

    wait(column_copies(step_id, slot, True))
    buf = io_ref.at[slot]

    def gather(cp, carry):
        rows = pl.ds(pl.multiple_of(cp * slab, slab), slab)
        for part in range(q // gpb):
            t0 = cp * 2 * q + part * gpb
            v = [jnp.concatenate([buf[pl.ds(t0 + i, batch, stride=pitch), :],
                                  buf[pl.ds(t0 + q + i, batch, stride=pitch), :]],
                                 axis=0).astype(BF16) for i in range(gpb)]
            w = _block_transpose(v, lane_block)
            for g in range(gpb):
                ug_ref[g, rows, part * LANES:(part + 1) * LANES] = w[g]
        return carry

    lax.fori_loop(0, n_chunks // 2, gather, 0, unroll=SSM_RELAYOUT_UNROLL)

    fwd_lane = lax.broadcasted_iota(jnp.int32, (batch, 2 * half), 1) < half
    zeros = jnp.zeros((batch, half), F32)
    last_rows = pl.ds((n_chunks - 1) * batch, batch)
    ni = SSM_INTERLEAVE
    for g0 in range(0, gpb, ni):
        for gi in range(ni):
            s_ref[gi] = jnp.dot(ug_ref[g0 + gi], ein_ref[g0 + gi], preferred_element_type=F32)
            x_ref[gi, 0:batch, 0:half] = zeros
            x_ref[gi, 0:batch, 2 * half:3 * half] = zeros
            x_ref[gi, last_rows, half:2 * half] = zeros
            x_ref[gi, last_rows, 3 * half:4 * half] = zeros

        def step(k, carry):
            rf = pl.multiple_of(k * batch, batch)
            rb = pl.multiple_of((n_chunks - 1 - k) * batch, batch)
            new = []
            for gi in range(ni):
                xr, xi = carry[gi]
                sre = jnp.where(fwd_lane, s_ref[gi, pl.ds(rf, batch), 0:2 * half],
                                s_ref[gi, pl.ds(rb, batch), 0:2 * half])
                sim = jnp.where(fwd_lane, s_ref[gi, pl.ds(rf, batch), 2 * half:4 * half],
                                s_ref[gi, pl.ds(rb, batch), 2 * half:4 * half])
                ar = aq_ref[g0 + gi, 0:1, :]
                ai = aq_ref[g0 + gi, 1:2, :]
                nr = ar * xr - ai * xi + sre
                nim = ar * xi + ai * xr + sim
                x_ref[gi, pl.ds(rf + batch, batch), 0:half] = nr[:, 0:half]
                x_ref[gi, pl.ds(rf + batch, batch), 2 * half:3 * half] = nim[:, 0:half]
                x_ref[gi, pl.ds(rb - batch, batch), half:2 * half] = nr[:, half:2 * half]
                x_ref[gi, pl.ds(rb - batch, batch), 3 * half:4 * half] = nim[:, half:2 * half]
                new.append((nr, nim))
            return tuple(new)

        init = tuple((jnp.zeros((batch, 2 * half), F32), jnp.zeros((batch, 2 * half), F32))
                     for _ in range(ni))
        lax.fori_loop(0, n_chunks - 1, step, init)

        for gi in range(ni):
            g = g0 + gi
            y = jnp.dot(ug_ref[g], toep_ref[g], preferred_element_type=F32)
            y += lax.dot_general(x_ref[gi].astype(BF16), eout_ref[g], NT_DIMS,
                                 preferred_element_type=F32)
            ug_ref[g] = y.astype(BF16)

    def scatter(cp, carry):
        rows = pl.ds(pl.multiple_of(cp * slab, slab), slab)
        for part in range(q // gpb):
            t0 = cp * 2 * q + part * gpb
            v = [ug_ref[g, rows, part * LANES:(part + 1) * LANES] for g in range(gpb)]
            w = _block_transpose(v, lane_block)
            for j in range(gpb):
                wj = w[j].astype(F32)
                buf[pl.ds(t0 + j, batch, stride=pitch), :] = wj[0:batch]
                buf[pl.ds(t0 + q + j, batch, stride=pitch), :] = wj[batch:slab]
        return carry

    lax.fori_loop(0, n_chunks // 2, scatter, 0, unroll=SSM_RELAYOUT_UNROLL)

    start(column_copies(step_id, slot, False))

    @pl.when(step_id == last_step)
    def _():
        @pl.when(step_id >= 1)
        def _():
            wait(column_copies(step_id - 1, 1 - slot, False))
        wait(column_copies(step_id, slot, False))


def _padded_seq(seq):
    return seq + SUBLANES if seq % (2 * SUBLANES) == 0 else seq


def _ssm(u, lam_re, lam_im, log_dt, b_re, b_im, c_re, c_im, d_skip):
    batch, seq, _ = u.shape
    pitch = _padded_seq(seq)
    rows = batch * seq // SSM_CHUNK
    ein, toep, eout, a_q = _ssm_operators(lam_re, lam_im, log_dt, b_re, b_im, c_re, c_im, d_skip)
    gpb = GROUPS_PER_BLOCK
    mat = pl.BlockSpec((gpb, CHUNK_WIDTH, CHUNK_WIDTH), lambda s: (s, 0, 0))
    hbm = pl.BlockSpec(memory_space=pl.ANY)
    return pl.pallas_call(
        functools.partial(_ssm_kernel, batch=batch, seq=seq, pitch=pitch),
        grid=(SSM_GROUPS // gpb,),
        in_specs=[hbm, mat, mat, mat,
                  pl.BlockSpec((gpb, SUBLANES, 2 * SSM_STATE), lambda s: (s, 0, 0))],
        out_specs=hbm,
        out_shape=jax.ShapeDtypeStruct((batch, seq, SSM_WIDTH), F32),
        scratch_shapes=[pltpu.VMEM((2, batch * pitch, LANES), F32),
                        pltpu.VMEM((gpb, rows, CHUNK_WIDTH), BF16),
                        pltpu.VMEM((SSM_INTERLEAVE, rows, 4 * SSM_STATE), F32),
                        pltpu.VMEM((SSM_INTERLEAVE, rows, 4 * SSM_STATE), F32),
                        pltpu.SemaphoreType.DMA((2, batch)),
                        pltpu.SemaphoreType.DMA((2, batch))],
        compiler_params=pltpu.CompilerParams(
            dimension_semantics=("arbitrary",), vmem_limit_bytes=VMEM_LIMIT_BYTES),
    )(u, ein, toep, eout, a_q)


def kernel(x, norm_ffn1, ffn1_w_gate, ffn1_w_up, ffn1_w_down, norm_mix, w_in, attn_sinks,
           ssm_lambda_re, ssm_lambda_im, ssm_log_dt, ssm_b_re, ssm_b_im, ssm_c_re, ssm_c_im,
           ssm_d, ssm_glu_w, ssm_glu_b, attn_out_norm, ssm_out_norm, w_out,
           norm_ffn2, ffn2_w_gate, ffn2_w_up, ffn2_w_down, final_norm):
    b, seq, d = x.shape
    depth = norm_ffn1.shape[0]
    assert d == D_MODEL and seq % BLOCK == 0 and (b * seq) % FFN_TOKEN_TILE == 0
    assert b == SUBLANES and seq % (2 * SSM_CHUNK * SSM_RELAYOUT_UNROLL) == 0
    h = x.reshape(b * seq, d).astype(F32)
    for l in range(depth):
        h, q, k, v, u = _ffn(h, norm_ffn1[l], ffn1_w_gate[l], ffn1_w_up[l], ffn1_w_down[l],
                             final_norm, final_norm=False, proj=(norm_mix[l], w_in[l]))
        attn = _attention(q.reshape(b, seq, -1), k.reshape(b, seq, -1), v.reshape(b, seq, -1),
                          attn_sinks[l])
        ssm_pre = _ssm(u.reshape(b, seq, -1), ssm_lambda_re[l], ssm_lambda_im[l], ssm_log_dt[l],
                       ssm_b_re[l], ssm_b_im[l], ssm_c_re[l], ssm_c_im[l], ssm_d[l])
        mix = (attn.reshape(b * seq, -1), ssm_pre.reshape(b * seq, -1), ssm_glu_w[l],
               ssm_glu_b[l], attn_out_norm[l], ssm_out_norm[l], w_out[l])
        h, = _ffn(h, norm_ffn2[l], ffn2_w_gate[l], ffn2_w_up[l], ffn2_w_down[l],
                  final_norm, final_norm=(l == depth - 1), mix=mix)
    return h.reshape(b, seq, d).astype(x.dtype)
```

```python
import functools
import math

import jax
import jax.numpy as jnp
from jax import lax
from jax.experimental import pallas as pl
from jax.experimental.pallas import tpu as pltpu

F32 = jnp.float32
BF16 = jnp.bfloat16

D_MODEL = 1024
ATTN_HEADS = 8
ATTN_KV_HEADS = 2
Q_PER_KV = ATTN_HEADS // ATTN_KV_HEADS
HEAD_DIM = 64
ATTN_WIDTH = ATTN_HEADS * HEAD_DIM
KV_WIDTH = ATTN_KV_HEADS * HEAD_DIM
WINDOW = 128
BLOCK = 128
SSM_CH = 16
SSM_WIDTH = D_MODEL - ATTN_WIDTH
SSM_GROUPS = SSM_WIDTH // SSM_CH
SSM_STATE = 64
IN_WIDTH = ATTN_WIDTH + 2 * KV_WIDTH + SSM_WIDTH
D_FF = 2816
EPS = 1e-6
NEG_INF = -1e30
LAMBDA_RE_MAX = -1e-4
LOG2_E = math.log2(math.e)
QUERY_SCALE = HEAD_DIM ** -0.5 * LOG2_E

LANES = 128
SUBLANES = 8
VMEM_LIMIT_BYTES = 56 * 1024 * 1024

FFN_TOKEN_TILE = 512
FFN_SUBTILES = 2
FF_TILE = 256
ATTN_BLOCKS_PER_STEP = 4
SSM_CHUNK = 16
CHUNK_WIDTH = SSM_CHUNK * SSM_CH
GROUPS_PER_BLOCK = LANES // SSM_CH
SSM_INTERLEAVE = 4
SSM_RELAYOUT_UNROLL = 4
OPS_GROUPS_PER_STEP = 4

NT_DIMS = (((1,), (1,)), ((), ()))
TN_DIMS = (((0,), (0,)), ((), ()))


def _rms(x):
    return x * lax.rsqrt(jnp.mean(x * x, axis=-1, keepdims=True) + EPS)


def _mixed_update(rows, attn_ref, ssm_ref, gw_ref, gb_ref, ga_ref, gs_ref, wo_ref):
    y = jax.nn.gelu(ssm_ref[rows, :])
    z = jnp.dot(y.astype(BF16), gw_ref[...].astype(BF16), preferred_element_type=F32) + gb_ref[...]
    s = y * jax.nn.sigmoid(z)
    sn = _rms(s) * gs_ref[...]
    an = _rms(attn_ref[rows, :].astype(F32)) * ga_ref[...]
    mixed = jnp.concatenate([an, sn], axis=-1).astype(BF16)
    return jnp.dot(mixed, wo_ref[...].astype(BF16), preferred_element_type=F32)


def _mixer_inputs(y, rows, gain_ref, w_ref, q_ref, k_ref, v_ref, u_ref):
    hn = (_rms(y) * gain_ref[...]).astype(BF16)
    proj = jnp.dot(hn, w_ref[...].astype(BF16), preferred_element_type=F32)
    q_ref[rows, :] = (proj[:, :ATTN_WIDTH] * QUERY_SCALE).astype(BF16)
    k_ref[rows, :] = proj[:, ATTN_WIDTH:ATTN_WIDTH + KV_WIDTH].astype(BF16)
    v_ref[rows, :] = proj[:, ATTN_WIDTH + KV_WIDTH:ATTN_WIDTH + 2 * KV_WIDTH].astype(BF16)
    u_ref[rows, :] = proj[:, ATTN_WIDTH + 2 * KV_WIDTH:]


def _ffn_body(read_x, gain_ref, fgain_ref, wg_hbm, wu_hbm, wd_hbm, o_ref,
              wg_ref, wu_ref, wd_ref, act_ref, gu_stage, d_stage, sem, *, final_norm,
              epilogue=None):
    nj = D_FF // FF_TILE

    def weight_copies(j, slot):
        span = pl.ds(j * FF_TILE, FF_TILE)
        return (pltpu.make_async_copy(wg_hbm.at[:, span], gu_stage.at[0, slot], sem.at[0, slot]),
                pltpu.make_async_copy(wu_hbm.at[:, span], gu_stage.at[1, slot], sem.at[1, slot]),
                pltpu.make_async_copy(wd_hbm.at[span, :], d_stage.at[slot], sem.at[2, slot]))

    def step(stage_weights):
        if stage_weights:
            for copy in weight_copies(0, 0):
                copy.start()
        sub = o_ref.shape[0] // FFN_SUBTILES
        halves = [slice(h * sub, (h + 1) * sub) for h in range(FFN_SUBTILES)]
        xs = [read_x(rows) for rows in halves]
        hns = [(_rms(x) * gain_ref[...]).astype(BF16) for x in xs]
        for j in range(nj):
            cols = slice(j * FF_TILE, (j + 1) * FF_TILE)
            if stage_weights:
                slot = j % 2
                if j + 1 < nj:
                    for copy in weight_copies(j + 1, 1 - slot):
                        copy.start()
                for copy in weight_copies(j, slot):
                    copy.wait()
                wg_ref[:, cols] = gu_stage[0, slot].astype(BF16)
                wu_ref[:, cols] = gu_stage[1, slot].astype(BF16)
                wd_ref[cols, :] = d_stage[slot].astype(BF16)
            for rows, hn in zip(halves, hns):
                g = jnp.dot(hn, wg_ref[:, cols], preferred_element_type=F32)
                u = jnp.dot(hn, wu_ref[:, cols], preferred_element_type=F32)
                act_ref[rows, cols] = (g * jax.nn.sigmoid(g) * u).astype(BF16)
        for rows, x in zip(halves, xs):
            y = x + 0.5 * jnp.dot(act_ref[rows, :], wd_ref[...], preferred_element_type=F32)
            if final_norm:
                y = _rms(y) * fgain_ref[...]
            o_ref[rows, :] = y
            if epilogue is not None:
                epilogue(y, rows)

    pl.when(pl.program_id(0) == 0)(functools.partial(step, True))
    pl.when(pl.program_id(0) != 0)(functools.partial(step, False))


def _ffn_proj_kernel(x_ref, mgain_ref, win_ref, gain_ref, fgain_ref, wg_hbm, wu_hbm, wd_hbm,
                     o_ref, q_ref, k_ref, v_ref, u_ref, *scratch, final_norm):
    emit = lambda y, rows: _mixer_inputs(y, rows, mgain_ref, win_ref, q_ref, k_ref, v_ref, u_ref)
    _ffn_body(lambda rows: x_ref[rows, :], gain_ref, fgain_ref, wg_hbm, wu_hbm, wd_hbm, o_ref,
              *scratch, final_norm=final_norm, epilogue=emit)


def _mix_ffn_kernel(x_ref, attn_ref, ssm_ref, gw_ref, gb_ref, ga_ref, gs_ref, wo_ref, *ffn_refs,
                    final_norm):
    read_x = lambda rows: x_ref[rows, :] + _mixed_update(rows, attn_ref, ssm_ref, gw_ref, gb_ref,
                                                         ga_ref, gs_ref, wo_ref)
    _ffn_body(read_x, *ffn_refs, final_norm=final_norm)


def _ffn(x, gain, w_gate, w_up, w_down, final_gain, final_norm, mix=None, proj=None):
    assert (mix is None) != (proj is None)
    t = x.shape[0]
    tile = FFN_TOKEN_TILE
    vec = lambda width: pl.BlockSpec((1, width), lambda i: (0, 0))
    resident = lambda r, c: pl.BlockSpec((r, c), lambda i: (0, 0), pipeline_mode=pl.Buffered(1))
    row = lambda width: pl.BlockSpec((tile, width), lambda i: (i, 0))
    hbm = pl.BlockSpec(memory_space=pl.ANY)
    in_specs, args = [row(D_MODEL)], [x]
    out_specs, out_shape = [row(D_MODEL)], [jax.ShapeDtypeStruct((t, D_MODEL), F32)]
    if mix is not None:
        attn, ssm_pre, glu_w, glu_b, attn_gain, ssm_gain, w_out = mix
        body = _mix_ffn_kernel
        in_specs += [row(ATTN_WIDTH), row(SSM_WIDTH),
                     resident(SSM_WIDTH, SSM_WIDTH), vec(SSM_WIDTH), vec(ATTN_WIDTH),
                     vec(SSM_WIDTH), resident(D_MODEL, D_MODEL)]
        args += [attn, ssm_pre, glu_w, glu_b.reshape(1, -1).astype(F32),
                 attn_gain.reshape(1, -1).astype(F32), ssm_gain.reshape(1, -1).astype(F32), w_out]
    if proj is not None:
        mixer_gain, w_in = proj
        body = _ffn_proj_kernel
        in_specs += [vec(D_MODEL), resident(D_MODEL, IN_WIDTH)]
        args += [mixer_gain.reshape(1, D_MODEL), w_in]
        for width, dtype in ((ATTN_WIDTH, BF16), (KV_WIDTH, BF16), (KV_WIDTH, BF16),
                             (SSM_WIDTH, F32)):
            out_specs.append(row(width))
            out_shape.append(jax.ShapeDtypeStruct((t, width), dtype))
    in_specs += [vec(D_MODEL), vec(D_MODEL), hbm, hbm, hbm]
    args += [gain.reshape(1, D_MODEL), final_gain.reshape(1, D_MODEL), w_gate, w_up, w_down]
    return pl.pallas_call(
        functools.partial(body, final_norm=final_norm),
        grid=(t // tile,),
        in_specs=in_specs,
        out_specs=out_specs,
        out_shape=out_shape,
        scratch_shapes=[pltpu.VMEM((D_MODEL, D_FF), BF16),
                        pltpu.VMEM((D_MODEL, D_FF), BF16),
                        pltpu.VMEM((D_FF, D_MODEL), BF16),
                        pltpu.VMEM((tile, D_FF), BF16),
                        pltpu.VMEM((2, 2, D_MODEL, FF_TILE), F32),
                        pltpu.VMEM((2, FF_TILE, D_MODEL), F32),
                        pltpu.SemaphoreType.DMA((3, 2))],
        compiler_params=pltpu.CompilerParams(
            dimension_semantics=("arbitrary",), vmem_limit_bytes=VMEM_LIMIT_BYTES),
    )(*args)


def _attn_kernel(sink_ref, q_ref, kp_ref, kc_ref, kn_ref, vp_ref, vc_ref, vn_ref, o_ref,
                 bias_ref, k_ref, v_ref, s_ref):
    n = pl.program_id(1)
    last = pl.num_programs(1) - 1
    nblk = ATTN_BLOCKS_PER_STEP

    @pl.when(n == 0)
    def _():
        kj = lax.broadcasted_iota(jnp.int32, (3 * BLOCK, BLOCK), 0)
        qi = lax.broadcasted_iota(jnp.int32, (3 * BLOCK, BLOCK), 1)
        rel = jnp.abs(kj - BLOCK - qi)
        dist = rel.astype(F32)
        inside = rel <= WINDOW
        has_prev = kj >= BLOCK
        has_next = kj < 2 * BLOCK
        for variant, ok in enumerate((inside & has_prev, inside, inside & has_next)):
            for h in range(ATTN_HEADS):
                slope = float(2.0 ** (-8.0 * (h + 1) / ATTN_HEADS))
                bias_ref[variant, h] = jnp.where(ok, (-slope * LOG2_E) * dist, NEG_INF)

    k_ref[0:BLOCK] = kp_ref[...]
    k_ref[BLOCK:(nblk + 1) * BLOCK] = kc_ref[...]
    k_ref[(nblk + 1) * BLOCK:(nblk + 2) * BLOCK] = kn_ref[...]
    v_ref[0:BLOCK] = vp_ref[...]
    v_ref[BLOCK:(nblk + 1) * BLOCK] = vc_ref[...]
    v_ref[(nblk + 1) * BLOCK:(nblk + 2) * BLOCK] = vn_ref[...]

    for j in range(nblk):
        for kh in range(ATTN_KV_HEADS):
            kcat = k_ref[j * BLOCK:(j + 3) * BLOCK, kh * HEAD_DIM:(kh + 1) * HEAD_DIM]
            heads = [kh * Q_PER_KV + g for g in range(Q_PER_KV)]
            qs = jnp.concatenate(
                [q_ref[j * BLOCK:(j + 1) * BLOCK, h * HEAD_DIM:(h + 1) * HEAD_DIM] for h in heads],
                axis=0)
            s_ref[j, kh] = lax.dot_general(kcat, qs, NT_DIMS,
                                           preferred_element_type=F32)

    for j in range(nblk):
        variant = 1
        if j == 0:
            variant = jnp.where(n == 0, 0, variant)
        if j == nblk - 1:
            variant = jnp.where(n == last, 2, variant)
        outs = []
        for kh in range(ATTN_KV_HEADS):
            vcat = v_ref[j * BLOCK:(j + 3) * BLOCK, kh * HEAD_DIM:(kh + 1) * HEAD_DIM]
            for g in range(Q_PER_KV):
                h = kh * Q_PER_KV + g
                s = s_ref[j, kh, :, g * BLOCK:(g + 1) * BLOCK] + bias_ref[variant, h]
                sink = sink_ref[h] * LOG2_E
                m = jnp.maximum(jnp.max(s, axis=0, keepdims=True), sink)
                e = jnp.exp2(s - m)
                den = jnp.sum(e, axis=0, keepdims=True) + jnp.exp2(sink - m)
                pv = lax.dot_general(vcat, e.astype(BF16), TN_DIMS,
                                     preferred_element_type=F32)
                outs.append(pv / den)
        o_ref[j * BLOCK:(j + 1) * BLOCK, :] = jnp.concatenate(outs, axis=0).T.astype(o_ref.dtype)


def _attention(q, k, v, sinks):
    b, seq, _ = q.shape
    nblk = ATTN_BLOCKS_PER_STEP
    steps = seq // (nblk * BLOCK)
    nb = seq // BLOCK
    assert seq % (nblk * BLOCK) == 0 and nb >= 2
    edge = lambda f: pl.BlockSpec((None, BLOCK, KV_WIDTH), f)
    body = pl.BlockSpec((None, nblk * BLOCK, KV_WIDTH), lambda bi, n: (bi, n, 0))
    prev = lambda bi, n: (bi, jnp.maximum(n * nblk - 1, 0), 0)
    nxt = lambda bi, n: (bi, jnp.minimum((n + 1) * nblk, nb - 1), 0)
    rows = pl.BlockSpec((None, nblk * BLOCK, ATTN_WIDTH), lambda bi, n: (bi, n, 0))
    return pl.pallas_call(
        _attn_kernel,
        grid=(b, steps),
        in_specs=[pl.BlockSpec(memory_space=pltpu.SMEM), rows,
                  edge(prev), body, edge(nxt), edge(prev), body, edge(nxt)],
        out_specs=rows,
        out_shape=jax.ShapeDtypeStruct((b, seq, ATTN_WIDTH), BF16),
        scratch_shapes=[pltpu.VMEM((3, ATTN_HEADS, 3 * BLOCK, BLOCK), F32),
                        pltpu.VMEM(((nblk + 2) * BLOCK, KV_WIDTH), BF16),
                        pltpu.VMEM(((nblk + 2) * BLOCK, KV_WIDTH), BF16),
                        pltpu.VMEM((nblk, ATTN_KV_HEADS, 3 * BLOCK, Q_PER_KV * BLOCK), F32)],
        compiler_params=pltpu.CompilerParams(
            dimension_semantics=("parallel", "arbitrary"), vmem_limit_bytes=VMEM_LIMIT_BYTES),
    )(sinks.astype(F32), q, k, k, k, v, v, v)


def _dot_nt_split(a, b):
    a_hi, b_hi = a.astype(BF16), b.astype(BF16)
    a_lo = (a - a_hi.astype(F32)).astype(BF16)
    b_lo = (b - b_hi.astype(F32)).astype(BF16)
    dot = functools.partial(lax.dot_general, dimension_numbers=NT_DIMS,
                            preferred_element_type=F32)
    return dot(a_hi, b_hi) + dot(a_hi, b_lo) + dot(a_lo, b_hi)


def _ssm_ops_kernel(prm_ref, bc_ref, ein_ref, toep_ref, eout_ref, aq_ref):
    q, hc, p = SSM_CHUNK, SSM_CH, SSM_STATE
    fwd = lax.broadcasted_iota(jnp.int32, (1, 2 * p), 1) < p
    zero_row = jnp.zeros((1, 2 * p), F32)
    row_id = lax.broadcasted_iota(jnp.int32, (CHUNK_WIDTH, CHUNK_WIDTH), 0)
    col_id = lax.broadcasted_iota(jnp.int32, (CHUNK_WIDTH, CHUNK_WIDTH), 1)

    def table(select, n):
        picks = [select(m) for m in range(n)]
        re = jnp.concatenate([jnp.broadcast_to(r, (hc, 2 * p)) for r, _ in picks], axis=0)
        im = jnp.concatenate([jnp.broadcast_to(i, (hc, 2 * p)) for _, i in picks], axis=0)
        return re, im

    def tile_rows(x, n):
        return jnp.concatenate([x] * n, axis=0)

    for gi in range(prm_ref.shape[0]):
        lr = jnp.minimum(prm_ref[gi, 0:1, :], LAMBDA_RE_MAX)
        li = prm_ref[gi, 1:2, :]
        dt = jnp.exp(prm_ref[gi, 2:3, :])
        mag = jnp.exp(lr * dt)
        a_r = mag * jnp.cos(li * dt)
        a_i = mag * jnp.sin(li * dt)
        den = lr * lr + li * li
        coef_r = ((a_r - 1.0) * lr + a_i * li) / den
        coef_i = (a_i * lr - (a_r - 1.0) * li) / den
        b_r, b_i = bc_ref[gi, 0], bc_ref[gi, 1]
        c_r, c_i = bc_ref[gi, 2], bc_ref[gi, 3]
        bb_r = coef_r * b_r - coef_i * b_i
        bb_i = coef_r * b_i + coef_i * b_r

        pw = [(jnp.ones((1, 2 * p), F32), zero_row)]
        for _ in range(q):
            r, i = pw[-1]
            pw.append((r * a_r - i * a_i, r * a_i + i * a_r))

        def both(f_idx, b_idx):
            fr, fi = pw[f_idx] if f_idx is not None else (zero_row, zero_row)
            br, bi = pw[b_idx] if b_idx is not None else (zero_row, zero_row)
            return jnp.where(fwd, fr, br), jnp.where(fwd, fi, bi)

        p_r, p_i = table(lambda i: both(q - 1 - i, i), q)
        tb_r, tb_i = tile_rows(bb_r, q), tile_rows(bb_i, q)
        ein = jnp.concatenate([tb_r * p_r - tb_i * p_i, tb_r * p_i + tb_i * p_r], axis=1)
        ein_ref[gi] = ein.astype(BF16)

        p_r, p_i = table(lambda j: both(j + 1, q - j), q)
        tc_r, tc_i = tile_rows(c_r, q), tile_rows(c_i, q)
        eout = jnp.concatenate([tc_r * p_r - tc_i * p_i, -(tc_r * p_i + tc_i * p_r)], axis=1)
        eout_ref[gi] = eout.astype(BF16)

        def lag(m):
            return both(m - (q - 1) if q - 1 <= m <= 2 * q - 2 else None,
                        (q - 1) - m if m <= q - 1 else None)

        p_r, p_i = table(lag, 2 * q)
        tc_r, tc_i = tile_rows(c_r, 2 * q), tile_rows(c_i, 2 * q)
        cpt = jnp.concatenate([tc_r * p_r - tc_i * p_i, tc_r * p_i + tc_i * p_r], axis=1)
        bcat = jnp.concatenate([bb_r, -bb_i], axis=1)
        kern = _dot_nt_split(bcat, cpt)
        toep = jnp.concatenate(
            [kern[:, hc * (q - 1 - i):hc * (q - 1 - i) + CHUNK_WIDTH] for i in range(q)], axis=0)
        skip = jnp.concatenate([prm_ref[gi, 3:4, :]] * (CHUNK_WIDTH // (2 * p)), axis=1)
        toep_ref[gi] = (toep + jnp.where(row_id == col_id, skip, 0.0)).astype(BF16)

        aq_ref[gi] = jnp.concatenate(
            [pw[q][0], pw[q][1], jnp.zeros((SUBLANES - 2, 2 * p), F32)], axis=0)


def _ssm_operators(lam_re, lam_im, log_dt, b_re, b_im, c_re, c_im, d_skip):
    g, p, hc = SSM_GROUPS, SSM_STATE, SSM_CH
    lanes = lambda a: a.astype(F32).transpose(1, 0, 2).reshape(g, 2 * p)
    prm = jnp.stack(
        [lanes(lam_re), lanes(lam_im),
         jnp.repeat(log_dt.astype(F32).T, p, axis=1),
         jnp.tile(d_skip.astype(F32), (1, 2 * p // hc))]
        + [jnp.zeros((g, 2 * p), F32)] * (SUBLANES - 4), axis=1)
    bc = jnp.stack(
        [b_re.astype(F32).transpose(1, 3, 0, 2).reshape(g, hc, 2 * p),
         b_im.astype(F32).transpose(1, 3, 0, 2).reshape(g, hc, 2 * p),
         c_re.astype(F32).transpose(1, 2, 0, 3).reshape(g, hc, 2 * p),
         c_im.astype(F32).transpose(1, 2, 0, 3).reshape(g, hc, 2 * p)], axis=1)
    gs = OPS_GROUPS_PER_STEP
    mat = pl.BlockSpec((gs, CHUNK_WIDTH, CHUNK_WIDTH), lambda s: (s, 0, 0))
    mat_shape = jax.ShapeDtypeStruct((g, CHUNK_WIDTH, CHUNK_WIDTH), BF16)
    return pl.pallas_call(
        _ssm_ops_kernel,
        grid=(g // gs,),
        in_specs=[pl.BlockSpec((gs, SUBLANES, 2 * p), lambda s: (s, 0, 0)),
                  pl.BlockSpec((gs, 4, hc, 2 * p), lambda s: (s, 0, 0, 0))],
        out_specs=[mat, mat, mat, pl.BlockSpec((gs, SUBLANES, 2 * p), lambda s: (s, 0, 0))],
        out_shape=[mat_shape, mat_shape, mat_shape,
                   jax.ShapeDtypeStruct((g, SUBLANES, 2 * p), F32)],
        compiler_params=pltpu.CompilerParams(
            dimension_semantics=("parallel",), vmem_limit_bytes=VMEM_LIMIT_BYTES),
    )(prm, bc)


def _lane_roll(x, shift):
    shift %= LANES
    return jnp.concatenate([x[:, LANES - shift:], x[:, :LANES - shift]], axis=1)


def _block_transpose(v, lane_block):
    v = list(v)
    n = len(v)
    d = n // 2
    while d >= 1:
        upper = (lane_block & d) != 0
        for i in range(n):
            if i & d == 0:
                a, b = v[i], v[i + d]
                v[i] = jnp.where(upper, _lane_roll(b, SSM_CH * d), a)
                v[i + d] = jnp.where(upper, b, _lane_roll(a, -SSM_CH * d))
        d //= 2
    return v


def _ssm_kernel(u_hbm, ein_ref, toep_ref, eout_ref, aq_ref, y_hbm,
                io_ref, ug_ref, s_ref, x_ref, in_sem, out_sem, *, batch, seq, pitch):
    q, half = SSM_CHUNK, SSM_STATE
    n_chunks = seq // q
    gpb = GROUPS_PER_BLOCK
    slab = 2 * batch
    lane_block = lax.broadcasted_iota(jnp.int32, (slab, LANES), 1) // SSM_CH
    step_id = pl.program_id(0)
    last_step = pl.num_programs(0) - 1
    slot = lax.rem(step_id, 2)

    def column_copies(block, slot_, to_vmem):
        lanes = pl.ds(pl.multiple_of(block * LANES, LANES), LANES)
        copies = []
        for b in range(batch):
            hbm = (u_hbm if to_vmem else y_hbm).at[b, :, lanes]
            vmem = io_ref.at[slot_, pl.ds(b * pitch, seq), :]
            if to_vmem:
                copies.append(pltpu.make_async_copy(hbm, vmem, in_sem.at[slot_, b]))
            else:
                copies.append(pltpu.make_async_copy(vmem, hbm, out_sem.at[slot_, b]))
        return copies

    def start(copies):
        for copy in copies:
            copy.start()

    def wait(copies):
        for copy in copies:
            copy.wait()

    @pl.when(step_id == 0)
    def _():
        start(column_copies(0, 0, True))

    @pl.when(step_id < last_step)
    def _():
        @pl.when(step_id >= 1)
        def _():
            wait(column_copies(step_id - 1, 1 - slot, False))
        start(column_copies(step_id + 1, 1 - slot, True))

    wait(column_copies(step_id, slot, True))
    buf = io_ref.at[slot]

    def gather(cp, carry):
        rows = pl.ds(pl.multiple_of(cp * slab, slab), slab)
        for part in range(q // gpb):
            t0 = cp * 2 * q + part * gpb
            v = [jnp.concatenate([buf[pl.ds(t0 + i, batch, stride=pitch), :],
                                  buf[pl.ds(t0 + q + i, batch, stride=pitch), :]],
                                 axis=0).astype(BF16) for i in range(gpb)]
            w = _block_transpose(v, lane_block)
            for g in range(gpb):
                ug_ref[g, rows, part * LANES:(part + 1) * LANES] = w[g]
        return carry

    lax.fori_loop(0, n_chunks // 2, gather, 0, unroll=SSM_RELAYOUT_UNROLL)

    fwd_lane = lax.broadcasted_iota(jnp.int32, (batch, 2 * half), 1) < half
    zeros = jnp.zeros((batch, half), F32)
    last_rows = pl.ds((n_chunks - 1) * batch, batch)
    ni = SSM_INTERLEAVE
    for g0 in range(0, gpb, ni):
        for gi in range(ni):
            s_ref[gi] = jnp.dot(ug_ref[g0 + gi], ein_ref[g0 + gi], preferred_element_type=F32)
            x_ref[gi, 0:batch, 0:half] = zeros
            x_ref[gi, 0:batch, 2 * half:3 * half] = zeros
            x_ref[gi, last_rows, half:2 * half] = zeros
            x_ref[gi, last_rows, 3 * half:4 * half] = zeros

        def step(k, carry):
            rf = pl.multiple_of(k * batch, batch)
            rb = pl.multiple_of((n_chunks - 1 - k) * batch, batch)
            new = []
            for gi in range(ni):
                xr, xi = carry[gi]
                sre = jnp.where(fwd_lane, s_ref[gi, pl.ds(rf, batch), 0:2 * half],
                                s_ref[gi, pl.ds(rb, batch), 0:2 * half])
                sim = jnp.where(fwd_lane, s_ref[gi, pl.ds(rf, batch), 2 * half:4 * half],
                                s_ref[gi, pl.ds(rb, batch), 2 * half:4 * half])
                ar = aq_ref[g0 + gi, 0:1, :]
                ai = aq_ref[g0 + gi, 1:2, :]
                nr = ar * xr - ai * xi + sre
                nim = ar * xi + ai * xr + sim
                x_ref[gi, pl.ds(rf + batch, batch), 0:half] = nr[:, 0:half]
                x_ref[gi, pl.ds(rf + batch, batch), 2 * half:3 * half] = nim[:, 0:half]
                x_ref[gi, pl.ds(rb - batch, batch), half:2 * half] = nr[:, half:2 * half]
                x_ref[gi, pl.ds(rb - batch, batch), 3 * half:4 * half] = nim[:, half:2 * half]
                new.append((nr, nim))
            return tuple(new)

        init = tuple((jnp.zeros((batch, 2 * half), F32), jnp.zeros((batch, 2 * half), F32))
                     for _ in range(ni))
        lax.fori_loop(0, n_chunks - 1, step, init)

        for gi in range(ni):
            g = g0 + gi
            y = jnp.dot(ug_ref[g], toep_ref[g], preferred_element_type=F32)
            y += lax.dot_general(x_ref[gi].astype(BF16), eout_ref[g], NT_DIMS,
                                 preferred_element_type=F32)
            ug_ref[g] = y.astype(BF16)

    def scatter(cp, carry):
        rows = pl.ds(pl.multiple_of(cp * slab, slab), slab)
        for part in range(q // gpb):
            t0 = cp * 2 * q + part * gpb
            v = [ug_ref[g, rows, part * LANES:(part + 1) * LANES] for g in range(gpb)]
            w = _block_transpose(v, lane_block)
            for j in range(gpb):
                wj = w[j].astype(F32)
                buf[pl.ds(t0 + j, batch, stride=pitch), :] = wj[0:batch]
                buf[pl.ds(t0 + q + j, batch, stride=pitch), :] = wj[batch:slab]
        return carry

    lax.fori_loop(0, n_chunks // 2, scatter, 0, unroll=SSM_RELAYOUT_UNROLL)

    start(column_copies(step_id, slot, False))

    @pl.when(step_id == last_step)
    def _():
        @pl.when(step_id >= 1)
        def _():
            wait(column_copies(step_id - 1, 1 - slot, False))
        wait(column_copies(step_id, slot, False))


def _padded_seq(seq):
    return seq + SUBLANES if seq % (2 * SUBLANES) == 0 else seq


def _ssm(u, lam_re, lam_im, log_dt, b_re, b_im, c_re, c_im, d_skip):
    batch, seq, _ = u.shape
    pitch = _padded_seq(seq)
    rows = batch * seq // SSM_CHUNK
    ein, toep, eout, a_q = _ssm_operators(lam_re, lam_im, log_dt, b_re, b_im, c_re, c_im, d_skip)
    gpb = GROUPS_PER_BLOCK
    mat = pl.BlockSpec((gpb, CHUNK_WIDTH, CHUNK_WIDTH), lambda s: (s, 0, 0))
    hbm = pl.BlockSpec(memory_space=pl.ANY)
    return pl.pallas_call(
        functools.partial(_ssm_kernel, batch=batch, seq=seq, pitch=pitch),
        grid=(SSM_GROUPS // gpb,),
        in_specs=[hbm, mat, mat, mat,
                  pl.BlockSpec((gpb, SUBLANES, 2 * SSM_STATE), lambda s: (s, 0, 0))],
        out_specs=hbm,
        out_shape=jax.ShapeDtypeStruct((batch, seq, SSM_WIDTH), F32),
        scratch_shapes=[pltpu.VMEM((2, batch * pitch, LANES), F32),
                        pltpu.VMEM((gpb, rows, CHUNK_WIDTH), BF16),
                        pltpu.VMEM((SSM_INTERLEAVE, rows, 4 * SSM_STATE), F32),
                        pltpu.VMEM((SSM_INTERLEAVE, rows, 4 * SSM_STATE), F32),
                        pltpu.SemaphoreType.DMA((2, batch)),
                        pltpu.SemaphoreType.DMA((2, batch))],
        compiler_params=pltpu.CompilerParams(
            dimension_semantics=("arbitrary",), vmem_limit_bytes=VMEM_LIMIT_BYTES),
    )(u, ein, toep, eout, a_q)


def kernel(x, norm_ffn1, ffn1_w_gate, ffn1_w_up, ffn1_w_down, norm_mix, w_in, attn_sinks,
           ssm_lambda_re, ssm_lambda_im, ssm_log_dt, ssm_b_re, ssm_b_im, ssm_c_re, ssm_c_im,
           ssm_d, ssm_glu_w, ssm_glu_b, attn_out_norm, ssm_out_norm, w_out,
           norm_ffn2, ffn2_w_gate, ffn2_w_up, ffn2_w_down, final_norm):
    b, seq, d = x.shape
    depth = norm_ffn1.shape[0]
    assert d == D_MODEL and seq % BLOCK == 0 and (b * seq) % FFN_TOKEN_TILE == 0
    assert b == SUBLANES and seq % (2 * SSM_CHUNK * SSM_RELAYOUT_UNROLL) == 0
    h = x.reshape(b * seq, d).astype(F32)
    for l in range(depth):
        h, q, k, v, u = _ffn(h, norm_ffn1[l], ffn1_w_gate[l], ffn1_w_up[l], ffn1_w_down[l],
                             final_norm, final_norm=False, proj=(norm_mix[l], w_in[l]))
        attn = _attention(q.reshape(b, seq, -1), k.reshape(b, seq, -1), v.reshape(b, seq, -1),
                          attn_sinks[l])
        ssm_pre = _ssm(u.reshape(b, seq, -1), ssm_lambda_re[l], ssm_lambda_im[l], ssm_log_dt[l],
                       ssm_b_re[l], ssm_b_im[l], ssm_c_re[l], ssm_c_im[l], ssm_d[l])
        mix = (attn.reshape(b * seq, -1), ssm_pre.reshape(b * seq, -1), ssm_glu_w[l],
               ssm_glu_b[l], attn_out_norm[l], ssm_out_norm[l], w_out[l])
        h, = _ffn(h, norm_ffn2[l], ffn2_w_gate[l], ffn2_w_up[l], ffn2_w_down[l],
                  final_norm, final_norm=(l == depth - 1), mix=mix)
    return h.reshape(b, seq, d).astype(x.dtype)
```

```python
import functools
import math

import jax
import jax.numpy as jnp
from jax import lax
from jax.experimental import pallas as pl
from jax.experimental.pallas import tpu as pltpu

F32 = jnp.float32
BF16 = jnp.bfloat16

D_MODEL = 1024
ATTN_HEADS = 8
ATTN_KV_HEADS = 2
Q_PER_KV = ATTN_HEADS // ATTN_KV_HEADS
HEAD_DIM = 64
ATTN_WIDTH = ATTN_HEADS * HEAD_DIM
KV_WIDTH = ATTN_KV_HEADS * HEAD_DIM
WINDOW = 128
BLOCK = 128
SSM_CH = 16
SSM_WIDTH = D_MODEL - ATTN_WIDTH
SSM_GROUPS = SSM_WIDTH // SSM_CH
SSM_STATE = 64
IN_WIDTH = ATTN_WIDTH + 2 * KV_WIDTH + SSM_WIDTH
D_FF = 2816
EPS = 1e-6
NEG_INF = -1e30
LAMBDA_RE_MAX = -1e-4
LOG2_E = math.log2(math.e)
QUERY_SCALE = HEAD_DIM ** -0.5 * LOG2_E

LANES = 128
SUBLANES = 8
VMEM_LIMIT_BYTES = 56 * 1024 * 1024

FFN_TOKEN_TILE = 512
FFN_SUBTILES = 2
FF_TILE = 256
ATTN_BLOCKS_PER_STEP = 4
SSM_CHUNK = 16
CHUNK_WIDTH = SSM_CHUNK * SSM_CH
GROUPS_PER_BLOCK = LANES // SSM_CH
SSM_INTERLEAVE = 4
SSM_RELAYOUT_UNROLL = 4
OPS_GROUPS_PER_STEP = 4

NT_DIMS = (((1,), (1,)), ((), ()))
TN_DIMS = (((0,), (0,)), ((), ()))


def _rms(x):
    return x * lax.rsqrt(jnp.mean(x * x, axis=-1, keepdims=True) + EPS)


def _mixed_update(rows, attn_ref, ssm_ref, gw_ref, gb_ref, ga_ref, gs_ref, wo_ref):
    y = jax.nn.gelu(ssm_ref[rows, :])
    z = jnp.dot(y.astype(BF16), gw_ref[...].astype(BF16), preferred_element_type=F32) + gb_ref[...]
    s = y * jax.nn.sigmoid(z)
    sn = _rms(s) * gs_ref[...]
    an = _rms(attn_ref[rows, :].astype(F32)) * ga_ref[...]
    mixed = jnp.concatenate([an, sn], axis=-1).astype(BF16)
    return jnp.dot(mixed, wo_ref[...].astype(BF16), preferred_element_type=F32)


def _mixer_inputs(y, rows, gain_ref, w_ref, q_ref, k_ref, v_ref, u_ref):
    hn = (_rms(y) * gain_ref[...]).astype(BF16)
    proj = jnp.dot(hn, w_ref[...].astype(BF16), preferred_element_type=F32)
    q_ref[rows, :] = (proj[:, :ATTN_WIDTH] * QUERY_SCALE).astype(BF16)
    k_ref[rows, :] = proj[:, ATTN_WIDTH:ATTN_WIDTH + KV_WIDTH].astype(BF16)
    v_ref[rows, :] = proj[:, ATTN_WIDTH + KV_WIDTH:ATTN_WIDTH + 2 * KV_WIDTH].astype(BF16)
    u_ref[rows, :] = proj[:, ATTN_WIDTH + 2 * KV_WIDTH:]


def _ffn_body(read_x, gain_ref, fgain_ref, wg_hbm, wu_hbm, wd_hbm, o_ref,
              wg_ref, wu_ref, wd_ref, act_ref, gu_stage, d_stage, sem, *, final_norm,
              epilogue=None):
    nj = D_FF // FF_TILE

    def weight_copies(j, slot):
        span = pl.ds(j * FF_TILE, FF_TILE)
        return (pltpu.make_async_copy(wg_hbm.at[:, span], gu_stage.at[0, slot], sem.at[0, slot]),
                pltpu.make_async_copy(wu_hbm.at[:, span], gu_stage.at[1, slot], sem.at[1, slot]),
                pltpu.make_async_copy(wd_hbm.at[span, :], d_stage.at[slot], sem.at[2, slot]))

    def step(stage_weights):
        if stage_weights:
            for copy in weight_copies(0, 0):
                copy.start()
        sub = o_ref.shape[0] // FFN_SUBTILES
        halves = [slice(h * sub, (h + 1) * sub) for h in range(FFN_SUBTILES)]
        xs = [read_x(rows) for rows in halves]
        hns = [(_rms(x) * gain_ref[...]).astype(BF16) for x in xs]
        for j in range(nj):
            cols = slice(j * FF_TILE, (j + 1) * FF_TILE)
            if stage_weights:
                slot = j % 2
                if j + 1 < nj:
                    for copy in weight_copies(j + 1, 1 - slot):
                        copy.start()
                for copy in weight_copies(j, slot):
                    copy.wait()
                wg_ref[:, cols] = gu_stage[0, slot].astype(BF16)
                wu_ref[:, cols] = gu_stage[1, slot].astype(BF16)
                wd_ref[cols, :] = d_stage[slot].astype(BF16)
            for rows, hn in zip(halves, hns):
                g = jnp.dot(hn, wg_ref[:, cols], preferred_element_type=F32)
                u = jnp.dot(hn, wu_ref[:, cols], preferred_element_type=F32)
                act_ref[rows, cols] = (g * jax.nn.sigmoid(g) * u).astype(BF16)
        for rows, x in zip(halves, xs):
            y = x + 0.5 * jnp.dot(act_ref[rows, :], wd_ref[...], preferred_element_type=F32)
            if final_norm:
                y = _rms(y) * fgain_ref[...]
            o_ref[rows, :] = y
            if epilogue is not None:
                epilogue(y, rows)

    pl.when(pl.program_id(0) == 0)(functools.partial(step, True))
    pl.when(pl.program_id(0) != 0)(functools.partial(step, False))


def _ffn_proj_kernel(x_ref, mgain_ref, win_ref, gain_ref, fgain_ref, wg_hbm, wu_hbm, wd_hbm,
                     o_ref, q_ref, k_ref, v_ref, u_ref, *scratch, final_norm):
    emit = lambda y, rows: _mixer_inputs(y, rows, mgain_ref, win_ref, q_ref, k_ref, v_ref, u_ref)
    _ffn_body(lambda rows: x_ref[rows, :], gain_ref, fgain_ref, wg_hbm, wu_hbm, wd_hbm, o_ref,
              *scratch, final_norm=final_norm, epilogue=emit)


def _mix_ffn_kernel(x_ref, attn_ref, ssm_ref, gw_ref, gb_ref, ga_ref, gs_ref, wo_ref, *ffn_refs,
                    final_norm):
    read_x = lambda rows: x_ref[rows, :] + _mixed_update(rows, attn_ref, ssm_ref, gw_ref, gb_ref,
                                                         ga_ref, gs_ref, wo_ref)
    _ffn_body(read_x, *ffn_refs, final_norm=final_norm)


def _ffn(x, gain, w_gate, w_up, w_down, final_gain, final_norm, mix=None, proj=None):
    assert (mix is None) != (proj is None)
    t = x.shape[0]
    tile = FFN_TOKEN_TILE
    vec = lambda width: pl.BlockSpec((1, width), lambda i: (0, 0))
    resident = lambda r, c: pl.BlockSpec((r, c), lambda i: (0, 0), pipeline_mode=pl.Buffered(1))
    row = lambda width: pl.BlockSpec((tile, width), lambda i: (i, 0))
    hbm = pl.BlockSpec(memory_space=pl.ANY)
    in_specs, args = [row(D_MODEL)], [x]
    out_specs, out_shape = [row(D_MODEL)], [jax.ShapeDtypeStruct((t, D_MODEL), F32)]
    if mix is not None:
        attn, ssm_pre, glu_w, glu_b, attn_gain, ssm_gain, w_out = mix
        body = _mix_ffn_kernel
        in_specs += [row(ATTN_WIDTH), row(SSM_WIDTH),
                     resident(SSM_WIDTH, SSM_WIDTH), vec(SSM_WIDTH), vec(ATTN_WIDTH),
                     vec(SSM_WIDTH), resident(D_MODEL, D_MODEL)]
        args += [attn, ssm_pre, glu_w, glu_b.reshape(1, -1).astype(F32),
                 attn_gain.reshape(1, -1).astype(F32), ssm_gain.reshape(1, -1).astype(F32), w_out]
    if proj is not None:
        mixer_gain, w_in = proj
        body = _ffn_proj_kernel
        in_specs += [vec(D_MODEL), resident(D_MODEL, IN_WIDTH)]
        args += [mixer_gain.reshape(1, D_MODEL), w_in]
        for width, dtype in ((ATTN_WIDTH, BF16), (KV_WIDTH, BF16), (KV_WIDTH, BF16),
                             (SSM_WIDTH, F32)):
            out_specs.append(row(width))
            out_shape.append(jax.ShapeDtypeStruct((t, width), dtype))
    in_specs += [vec(D_MODEL), vec(D_MODEL), hbm, hbm, hbm]
    args += [gain.reshape(1, D_MODEL), final_gain.reshape(1, D_MODEL), w_gate, w_up, w_down]
    return pl.pallas_call(
        functools.partial(body, final_norm=final_norm),
        grid=(t // tile,),
        in_specs=in_specs,
        out_specs=out_specs,
        out_shape=out_shape,
        scratch_shapes=[pltpu.VMEM((D_MODEL, D_FF), BF16),
                        pltpu.VMEM((D_MODEL, D_FF), BF16),
                        pltpu.VMEM((D_FF, D_MODEL), BF16),
                        pltpu.VMEM((tile, D_FF), BF16),
                        pltpu.VMEM((2, 2, D_MODEL, FF_TILE), F32),
                        pltpu.VMEM((2, FF_TILE, D_MODEL), F32),
                        pltpu.SemaphoreType.DMA((3, 2))],
        compiler_params=pltpu.CompilerParams(
            dimension_semantics=("arbitrary",), vmem_limit_bytes=VMEM_LIMIT_BYTES),
    )(*args)


def _attn_kernel(sink_ref, q_ref, kp_ref, kc_ref, kn_ref, vp_ref, vc_ref, vn_ref, o_ref,
                 bias_ref, k_ref, v_ref, s_ref):
    n = pl.program_id(1)
    last = pl.num_programs(1) - 1
    nblk = ATTN_BLOCKS_PER_STEP

    @pl.when(n == 0)
    def _():
        kj = lax.broadcasted_iota(jnp.int32, (3 * BLOCK, BLOCK), 0)
        qi = lax.broadcasted_iota(jnp.int32, (3 * BLOCK, BLOCK), 1)
        rel = jnp.abs(kj - BLOCK - qi)
        dist = rel.astype(F32)
        inside = rel <= WINDOW
        has_prev = kj >= BLOCK
        has_next = kj < 2 * BLOCK
        for variant, ok in enumerate((inside & has_prev, inside, inside & has_next)):
            for h in range(ATTN_HEADS):
                slope = float(2.0 ** (-8.0 * (h + 1) / ATTN_HEADS))
                bias_ref[variant, h] = jnp.where(ok, (-slope * LOG2_E) * dist, NEG_INF)

    k_ref[0:BLOCK] = kp_ref[...]
    k_ref[BLOCK:(nblk + 1) * BLOCK] = kc_ref[...]
    k_ref[(nblk + 1) * BLOCK:(nblk + 2) * BLOCK] = kn_ref[...]
    v_ref[0:BLOCK] = vp_ref[...]
    v_ref[BLOCK:(nblk + 1) * BLOCK] = vc_ref[...]
    v_ref[(nblk + 1) * BLOCK:(nblk + 2) * BLOCK] = vn_ref[...]

    for j in range(nblk):
        for kh in range(ATTN_KV_HEADS):
            kcat = k_ref[j * BLOCK:(j + 3) * BLOCK, kh * HEAD_DIM:(kh + 1) * HEAD_DIM]
            heads = [kh * Q_PER_KV + g for g in range(Q_PER_KV)]
            qs = jnp.concatenate(
                [q_ref[j * BLOCK:(j + 1) * BLOCK, h * HEAD_DIM:(h + 1) * HEAD_DIM] for h in heads],
                axis=0)
            s_ref[j, kh] = lax.dot_general(kcat, qs, NT_DIMS,
                                           preferred_element_type=F32)

    for j in range(nblk):
        variant = 1
        if j == 0:
            variant = jnp.where(n == 0, 0, variant)
        if j == nblk - 1:
            variant = jnp.where(n == last, 2, variant)
        outs = []
        for kh in range(ATTN_KV_HEADS):
            vcat = v_ref[j * BLOCK:(j + 3) * BLOCK, kh * HEAD_DIM:(kh + 1) * HEAD_DIM]
            for g in range(Q_PER_KV):
                h = kh * Q_PER_KV + g
                s = s_ref[j, kh, :, g * BLOCK:(g + 1) * BLOCK] + bias_ref[variant, h]
                sink = sink_ref[h] * LOG2_E
                m = jnp.maximum(jnp.max(s, axis=0, keepdims=True), sink)
                e = jnp.exp2(s - m)
                den = jnp.sum(e, axis=0, keepdims=True) + jnp.exp2(sink - m)
                pv = lax.dot_general(vcat, e.astype(BF16), TN_DIMS,
                                     preferred_element_type=F32)
                outs.append(pv / den)
        o_ref[j * BLOCK:(j + 1) * BLOCK, :] = jnp.concatenate(outs, axis=0).T.astype(o_ref.dtype)


def _attention(q, k, v, sinks):
    b, seq, _ = q.shape
    nblk = ATTN_BLOCKS_PER_STEP
    steps = seq // (nblk * BLOCK)
    nb = seq // BLOCK
    assert seq % (nblk * BLOCK) == 0 and nb >= 2
    edge = lambda f: pl.BlockSpec((None, BLOCK, KV_WIDTH), f)
    body = pl.BlockSpec((None, nblk * BLOCK, KV_WIDTH), lambda bi, n: (bi, n, 0))
    prev = lambda bi, n: (bi, jnp.maximum(n * nblk - 1, 0), 0)
    nxt = lambda bi, n: (bi, jnp.minimum((n + 1) * nblk, nb - 1), 0)
    rows = pl.BlockSpec((None, nblk * BLOCK, ATTN_WIDTH), lambda bi, n: (bi, n, 0))
    return pl.pallas_call(
        _attn_kernel,
        grid=(b, steps),
        in_specs=[pl.BlockSpec(memory_space=pltpu.SMEM), rows,
                  edge(prev), body, edge(nxt), edge(prev), body, edge(nxt)],
        out_specs=rows,
        out_shape=jax.ShapeDtypeStruct((b, seq, ATTN_WIDTH), BF16),
        scratch_shapes=[pltpu.VMEM((3, ATTN_HEADS, 3 * BLOCK, BLOCK), F32),
                        pltpu.VMEM(((nblk + 2) * BLOCK, KV_WIDTH), BF16),
                        pltpu.VMEM(((nblk + 2) * BLOCK, KV_WIDTH), BF16),
                        pltpu.VMEM((nblk, ATTN_KV_HEADS, 3 * BLOCK, Q_PER_KV * BLOCK), F32)],
        compiler_params=pltpu.CompilerParams(
            dimension_semantics=("parallel", "arbitrary"), vmem_limit_bytes=VMEM_LIMIT_BYTES),
    )(sinks.astype(F32), q, k, k, k, v, v, v)


def _dot_nt_split(a, b):
    a_hi, b_hi = a.astype(BF16), b.astype(BF16)
    a_lo = (a - a_hi.astype(F32)).astype(BF16)
    b_lo = (b - b_hi.astype(F32)).astype(BF16)
    dot = functools.partial(lax.dot_general, dimension_numbers=NT_DIMS,
                            preferred_element_type=F32)
    return dot(a_hi, b_hi) + dot(a_hi, b_lo) + dot(a_lo, b_hi)


def _ssm_ops_kernel(prm_ref, bc_ref, ein_ref, toep_ref, eout_ref, aq_ref):
    q, hc, p = SSM_CHUNK, SSM_CH, SSM_STATE
    fwd = lax.broadcasted_iota(jnp.int32, (1, 2 * p), 1) < p
    zero_row = jnp.zeros((1, 2 * p), F32)
    row_id = lax.broadcasted_iota(jnp.int32, (CHUNK_WIDTH, CHUNK_WIDTH), 0)
    col_id = lax.broadcasted_iota(jnp.int32, (CHUNK_WIDTH, CHUNK_WIDTH), 1)

    def table(select, n):
        picks = [select(m) for m in range(n)]
        re = jnp.concatenate([jnp.broadcast_to(r, (hc, 2 * p)) for r, _ in picks], axis=0)
        im = jnp.concatenate([jnp.broadcast_to(i, (hc, 2 * p)) for _, i in picks], axis=0)
        return re, im

    def tile_rows(x, n):
        return jnp.concatenate([x] * n, axis=0)

    for gi in range(prm_ref.shape[0]):
        lr = jnp.minimum(prm_ref[gi, 0:1, :], LAMBDA_RE_MAX)
        li = prm_ref[gi, 1:2, :]
        dt = jnp.exp(prm_ref[gi, 2:3, :])
        mag = jnp.exp(lr * dt)
        a_r = mag * jnp.cos(li * dt)
        a_i = mag * jnp.sin(li * dt)
        den = lr * lr + li * li
        coef_r = ((a_r - 1.0) * lr + a_i * li) / den
        coef_i = (a_i * lr - (a_r - 1.0) * li) / den
        b_r, b_i = bc_ref[gi, 0], bc_ref[gi, 1]
        c_r, c_i = bc_ref[gi, 2], bc_ref[gi, 3]
        bb_r = coef_r * b_r - coef_i * b_i
        bb_i = coef_r * b_i + coef_i * b_r

        pw = [(jnp.ones((1, 2 * p), F32), zero_row)]
        for _ in range(q):
            r, i = pw[-1]
            pw.append((r * a_r - i * a_i, r * a_i + i * a_r))

        def both(f_idx, b_idx):
            fr, fi = pw[f_idx] if f_idx is not None else (zero_row, zero_row)
            br, bi = pw[b_idx] if b_idx is not None else (zero_row, zero_row)
            return jnp.where(fwd, fr, br), jnp.where(fwd, fi, bi)

        p_r, p_i = table(lambda i: both(q - 1 - i, i), q)
        tb_r, tb_i = tile_rows(bb_r, q), tile_rows(bb_i, q)
        ein = jnp.concatenate([tb_r * p_r - tb_i * p_i, tb_r * p_i + tb_i * p_r], axis=1)
        ein_ref[gi] = ein.astype(BF16)

        p_r, p_i = table(lambda j: both(j + 1, q - j), q)
        tc_r, tc_i = tile_rows(c_r, q), tile_rows(c_i, q)
        eout = jnp.concatenate([tc_r * p_r - tc_i * p_i, -(tc_r * p_i + tc_i * p_r)], axis=1)
        eout_ref[gi] = eout.astype(BF16)

        def lag(m):
            return both(m - (q - 1) if q - 1 <= m <= 2 * q - 2 else None,
                        (q - 1) - m if m <= q - 1 else None)

        p_r, p_i = table(lag, 2 * q)
        tc_r, tc_i = tile_rows(c_r, 2 * q), tile_rows(c_i, 2 * q)
        cpt = jnp.concatenate([tc_r * p_r - tc_i * p_i, tc_r * p_i + tc_i * p_r], axis=1)
        bcat = jnp.concatenate([bb_r, -bb_i], axis=1)
        kern = _dot_nt_split(bcat, cpt)
        toep = jnp.concatenate(
            [kern[:, hc * (q - 1 - i):hc * (q - 1 - i) + CHUNK_WIDTH] for i in range(q)], axis=0)
        skip = jnp.concatenate([prm_ref[gi, 3:4, :]] * (CHUNK_WIDTH // (2 * p)), axis=1)
        toep_ref[gi] = (toep + jnp.where(row_id == col_id, skip, 0.0)).astype(BF16)

        aq_ref[gi] = jnp.concatenate(
            [pw[q][0], pw[q][1], jnp.zeros((SUBLANES - 2, 2 * p), F32)], axis=0)


def _ssm_operators(lam_re, lam_im, log_dt, b_re, b_im, c_re, c_im, d_skip):
    g, p, hc = SSM_GROUPS, SSM_STATE, SSM_CH
    lanes = lambda a: a.astype(F32).transpose(1, 0, 2).reshape(g, 2 * p)
    prm = jnp.stack(
        [lanes(lam_re), lanes(lam_im),
         jnp.repeat(log_dt.astype(F32).T, p, axis=1),
         jnp.tile(d_skip.astype(F32), (1, 2 * p // hc))]
        + [jnp.zeros((g, 2 * p), F32)] * (SUBLANES - 4), axis=1)
    bc = jnp.stack(
        [b_re.astype(F32).transpose(1, 3, 0, 2).reshape(g, hc, 2 * p),
         b_im.astype(F32).transpose(1, 3, 0, 2).reshape(g, hc, 2 * p),
         c_re.astype(F32).transpose(1, 2, 0, 3).reshape(g, hc, 2 * p),
         c_im.astype(F32).transpose(1, 2, 0, 3).reshape(g, hc, 2 * p)], axis=1)
    gs = OPS_GROUPS_PER_STEP
    mat = pl.BlockSpec((gs, CHUNK_WIDTH, CHUNK_WIDTH), lambda s: (s, 0, 0))
    mat_shape = jax.ShapeDtypeStruct((g, CHUNK_WIDTH, CHUNK_WIDTH), BF16)
    return pl.pallas_call(
        _ssm_ops_kernel,
        grid=(g // gs,),
        in_specs=[pl.BlockSpec((gs, SUBLANES, 2 * p), lambda s: (s, 0, 0)),
                  pl.BlockSpec((gs, 4, hc, 2 * p), lambda s: (s, 0, 0, 0))],
        out_specs=[mat, mat, mat, pl.BlockSpec((gs, SUBLANES, 2 * p), lambda s: (s, 0, 0))],
        out_shape=[mat_shape, mat_shape, mat_shape,
                   jax.ShapeDtypeStruct((g, SUBLANES, 2 * p), F32)],
        compiler_params=pltpu.CompilerParams(
            dimension_semantics=("parallel",), vmem_limit_bytes=VMEM_LIMIT_BYTES),
    )(prm, bc)


def _lane_roll(x, shift):
    shift %= LANES
    return jnp.concatenate([x[:, LANES - shift:], x[:, :LANES - shift]], axis=1)


def _block_transpose(v, lane_block):
    n = len(v)
    skewed = [v[i] if i == 0 else _lane_roll(v[i], SSM_CH * i) for i in range(n)]
    out = []
    for g in range(n):
        picked = skewed[(0 - g) % n]
        for j in range(1, n):
            picked = jnp.where(lane_block == j, skewed[(j - g) % n], picked)
        out.append(picked if g == 0 else _lane_roll(picked, -SSM_CH * g))
    return out


def _ssm_kernel(u_hbm, ein_ref, toep_ref, eout_ref, aq_ref, y_hbm,
                io_ref, ug_ref, s_ref, x_ref, in_sem, out_sem, *, batch, seq, pitch):
    q, half = SSM_CHUNK, SSM_STATE
    n_chunks = seq // q
    gpb = GROUPS_PER_BLOCK
    slab = 2 * batch
    lane_block = lax.broadcasted_iota(jnp.int32, (slab, LANES), 1) // SSM_CH
    step_id = pl.program_id(0)
    last_step = pl.num_programs(0) - 1
    slot = lax.rem(step_id, 2)

    def column_copies(block, slot_, to_vmem):
        lanes = pl.ds(pl.multiple_of(block * LANES, LANES), LANES)
        copies = []
        for b in range(batch):
            hbm = (u_hbm if to_vmem else y_hbm).at[b, :, lanes]
            vmem = io_ref.at[slot_, pl.ds(b * pitch, seq), :]
            if to_vmem:
                copies.append(pltpu.make_async_copy(hbm, vmem, in_sem.at[slot_, b]))
            else:
                copies.append(pltpu.make_async_copy(vmem, hbm, out_sem.at[slot_, b]))
        return copies

    def start(copies):
        for copy in copies:
            copy.start()

    def wait(copies):
        for copy in copies:
            copy.wait()

    @pl.when(step_id == 0)
    def _():
        start(column_copies(0, 0, True))

    @pl.when(step_id < last_step)
    def _():
        @pl.when(step_id >= 1)
        def _():
            wait(column_copies(step_id - 1, 1 - slot, False))
        start(column_copies(step_id + 1, 1 - slot, True))

    wait(column_copies(step_id, slot, True))
    buf = io_ref.at[slot]

    def gather(cp, carry):
        rows = pl.ds(pl.multiple_of(cp * slab, slab), slab)
        for part in range(q // gpb):
            t0 = cp * 2 * q + part * gpb
            v = [jnp.concatenate([buf[pl.ds(t0 + i, batch, stride=pitch), :],
                                  buf[pl.ds(t0 + q + i, batch, stride=pitch), :]],
                                 axis=0).astype(BF16) for i in range(gpb)]
            w = _block_transpose(v, lane_block)
            for g in range(gpb):
                ug_ref[g, rows, part * LANES:(part + 1) * LANES] = w[g]
        return carry

    lax.fori_loop(0, n_chunks // 2, gather, 0, unroll=SSM_RELAYOUT_UNROLL)

    fwd_lane = lax.broadcasted_iota(jnp.int32, (batch, 2 * half), 1) < half
    zeros = jnp.zeros((batch, half), F32)
    last_rows = pl.ds((n_chunks - 1) * batch, batch)
    ni = SSM_INTERLEAVE
    for g0 in range(0, gpb, ni):
        for gi in range(ni):
            s_ref[gi] = jnp.dot(ug_ref[g0 + gi], ein_ref[g0 + gi], preferred_element_type=F32)
            x_ref[gi, 0:batch, 0:half] = zeros
            x_ref[gi, 0:batch, 2 * half:3 * half] = zeros
            x_ref[gi, last_rows, half:2 * half] = zeros
            x_ref[gi, last_rows, 3 * half:4 * half] = zeros

        def step(k, carry):
            rf = pl.multiple_of(k * batch, batch)
            rb = pl.multiple_of((n_chunks - 1 - k) * batch, batch)
            new = []
            for gi in range(ni):
                xr, xi = carry[gi]
                sre = jnp.where(fwd_lane, s_ref[gi, pl.ds(rf, batch), 0:2 * half],
                                s_ref[gi, pl.ds(rb, batch), 0:2 * half])
                sim = jnp.where(fwd_lane, s_ref[gi, pl.ds(rf, batch), 2 * half:4 * half],
                                s_ref[gi, pl.ds(rb, batch), 2 * half:4 * half])
                ar = aq_ref[g0 + gi, 0:1, :]
                ai = aq_ref[g0 + gi, 1:2, :]
                nr = ar * xr - ai * xi + sre
                nim = ar * xi + ai * xr + sim
                x_ref[gi, pl.ds(rf + batch, batch), 0:half] = nr[:, 0:half]
                x_ref[gi, pl.ds(rf + batch, batch), 2 * half:3 * half] = nim[:, 0:half]
                x_ref[gi, pl.ds(rb - batch, batch), half:2 * half] = nr[:, half:2 * half]
                x_ref[gi, pl.ds(rb - batch, batch), 3 * half:4 * half] = nim[:, half:2 * half]
                new.append((nr, nim))
            return tuple(new)

        init = tuple((jnp.zeros((batch, 2 * half), F32), jnp.zeros((batch, 2 * half), F32))
                     for _ in range(ni))
        lax.fori_loop(0, n_chunks - 1, step, init)

        for gi in range(ni):
            g = g0 + gi
            y = jnp.dot(ug_ref[g], toep_ref[g], preferred_element_type=F32)
            y += lax.dot_general(x_ref[gi].astype(BF16), eout_ref[g], NT_DIMS,
                                 preferred_element_type=F32)
            ug_ref[g] = y.astype(BF16)

    def scatter(cp, carry):
        rows = pl.ds(pl.multiple_of(cp * slab, slab), slab)
        for part in range(q // gpb):
            t0 = cp * 2 * q + part * gpb
            v = [ug_ref[g, rows, part * LANES:(part + 1) * LANES] for g in range(gpb)]
            w = _block_transpose(v, lane_block)
            for j in range(gpb):
                wj = w[j].astype(F32)
                buf[pl.ds(t0 + j, batch, stride=pitch), :] = wj[0:batch]
                buf[pl.ds(t0 + q + j, batch, stride=pitch), :] = wj[batch:slab]
        return carry

    lax.fori_loop(0, n_chunks // 2, scatter, 0, unroll=SSM_RELAYOUT_UNROLL)

    start(column_copies(step_id, slot, False))

    @pl.when(step_id == last_step)
    def _():
        @pl.when(step_id >= 1)
        def _():
            wait(column_copies(step_id - 1, 1 - slot, False))
        wait(column_copies(step_id, slot, False))


def _padded_seq(seq):
    return seq + SUBLANES if seq % (2 * SUBLANES) == 0 else seq


def _ssm(u, lam_re, lam_im, log_dt, b_re, b_im, c_re, c_im, d_skip):
    batch, seq, _ = u.shape
    pitch = _padded_seq(seq)
    rows = batch * seq // SSM_CHUNK
    ein, toep, eout, a_q = _ssm_operators(lam_re, lam_im, log_dt, b_re, b_im, c_re, c_im, d_skip)
    gpb = GROUPS_PER_BLOCK
    mat = pl.BlockSpec((gpb, CHUNK_WIDTH, CHUNK_WIDTH), lambda s: (s, 0, 0))
    hbm = pl.BlockSpec(memory_space=pl.ANY)
    return pl.pallas_call(
        functools.partial(_ssm_kernel, batch=batch, seq=seq, pitch=pitch),
        grid=(SSM_GROUPS // gpb,),
        in_specs=[hbm, mat, mat, mat,
                  pl.BlockSpec((gpb, SUBLANES, 2 * SSM_STATE), lambda s: (s, 0, 0))],
        out_specs=hbm,
        out_shape=jax.ShapeDtypeStruct((batch, seq, SSM_WIDTH), F32),
        scratch_shapes=[pltpu.VMEM((2, batch * pitch, LANES), F32),
                        pltpu.VMEM((gpb, rows, CHUNK_WIDTH), BF16),
                        pltpu.VMEM((SSM_INTERLEAVE, rows, 4 * SSM_STATE), F32),
                        pltpu.VMEM((SSM_INTERLEAVE, rows, 4 * SSM_STATE), F32),
                        pltpu.SemaphoreType.DMA((2, batch)),
                        pltpu.SemaphoreType.DMA((2, batch))],
        compiler_params=pltpu.CompilerParams(
            dimension_semantics=("arbitrary",), vmem_limit_bytes=VMEM_LIMIT_BYTES),
    )(u, ein, toep, eout, a_q)


def kernel(x, norm_ffn1, ffn1_w_gate, ffn1_w_up, ffn1_w_down, norm_mix, w_in, attn_sinks,
           ssm_lambda_re, ssm_lambda_im, ssm_log_dt, ssm_b_re, ssm_b_im, ssm_c_re, ssm_c_im,
           ssm_d, ssm_glu_w, ssm_glu_b, attn_out_norm, ssm_out_norm, w_out,
           norm_ffn2, ffn2_w_gate, ffn2_w_up, ffn2_w_down, final_norm):
    b, seq, d = x.shape
    depth = norm_ffn1.shape[0]
    assert d == D_MODEL and seq % BLOCK == 0 and (b * seq) % FFN_TOKEN_TILE == 0
    assert b == SUBLANES and seq % (2 * SSM_CHUNK * SSM_RELAYOUT_UNROLL) == 0
    h = x.reshape(b * seq, d).astype(F32)
    for l in range(depth):
        h, q, k, v, u = _ffn(h, norm_ffn1[l], ffn1_w_gate[l], ffn1_w_up[l], ffn1_w_down[l],
                             final_norm, final_norm=False, proj=(norm_mix[l], w_in[l]))
        attn = _attention(q.reshape(b, seq, -1), k.reshape(b, seq, -1), v.reshape(b, seq, -1),
                          attn_sinks[l])
        ssm_pre = _ssm(u.reshape(b, seq, -1), ssm_lambda_re[l], ssm_lambda_im[l], ssm_log_dt[l],
                       ssm_b_re[l], ssm_b_im[l], ssm_c_re[l], ssm_c_im[l], ssm_d[l])
        mix = (attn.reshape(b * seq, -1), ssm_pre.reshape(b * seq, -1), ssm_glu_w[l],
               ssm_glu_b[l], attn_out_norm[l], ssm_out_norm[l], w_out[l])
        h, = _ffn(h, norm_ffn2[l], ffn2_w_gate[l], ffn2_w_up[l], ffn2_w_down[l],
                  final_norm, final_norm=(l == depth - 1), mix=mix)
    return h.reshape(b, seq, d).astype(x.dtype)
```

```python
import functools
import math

import jax
import jax.numpy as jnp
from jax import lax
from jax.experimental import pallas as pl
from jax.experimental.pallas import tpu as pltpu

F32 = jnp.float32
BF16 = jnp.bfloat16

D_MODEL = 1024
ATTN_HEADS = 8
ATTN_KV_HEADS = 2
Q_PER_KV = ATTN_HEADS // ATTN_KV_HEADS
HEAD_DIM = 64
ATTN_WIDTH = ATTN_HEADS * HEAD_DIM
KV_WIDTH = ATTN_KV_HEADS * HEAD_DIM
WINDOW = 128
BLOCK = 128
SSM_CH = 16
SSM_WIDTH = D_MODEL - ATTN_WIDTH
SSM_GROUPS = SSM_WIDTH // SSM_CH
SSM_STATE = 64
IN_WIDTH = ATTN_WIDTH + 2 * KV_WIDTH + SSM_WIDTH
D_FF = 2816
EPS = 1e-6
NEG_INF = -1e30
LAMBDA_RE_MAX = -1e-4
LOG2_E = math.log2(math.e)
QUERY_SCALE = HEAD_DIM ** -0.5 * LOG2_E

LANES = 128
SUBLANES = 8
VMEM_LIMIT_BYTES = 56 * 1024 * 1024

FFN_TOKEN_TILE = 512
FFN_SUBTILES = 2
FF_TILE = 256
ATTN_BLOCKS_PER_STEP = 8
SSM_CHUNK = 16
CHUNK_WIDTH = SSM_CHUNK * SSM_CH
GROUPS_PER_BLOCK = LANES // SSM_CH
SSM_INTERLEAVE = 8
SSM_RELAYOUT_UNROLL = 4
OPS_GROUPS_PER_STEP = 4

NT_DIMS = (((1,), (1,)), ((), ()))
TN_DIMS = (((0,), (0,)), ((), ()))


def _rms(x):
    return x * lax.rsqrt(jnp.mean(x * x, axis=-1, keepdims=True) + EPS)


def _mixed_update(rows, attn_ref, ssm_ref, gw_ref, gb_ref, ga_ref, gs_ref, wo_ref):
    y = jax.nn.gelu(ssm_ref[rows, :])
    z = jnp.dot(y.astype(BF16), gw_ref[...].astype(BF16), preferred_element_type=F32) + gb_ref[...]
    s = y * jax.nn.sigmoid(z)
    sn = _rms(s) * gs_ref[...]
    an = _rms(attn_ref[rows, :].astype(F32)) * ga_ref[...]
    mixed = jnp.concatenate([an, sn], axis=-1).astype(BF16)
    return jnp.dot(mixed, wo_ref[...].astype(BF16), preferred_element_type=F32)


def _mixer_inputs(y, rows, gain_ref, w_ref, q_ref, k_ref, v_ref, u_ref):
    hn = (_rms(y) * gain_ref[...]).astype(BF16)
    proj = jnp.dot(hn, w_ref[...].astype(BF16), preferred_element_type=F32)
    q_ref[rows, :] = (proj[:, :ATTN_WIDTH] * QUERY_SCALE).astype(BF16)
    k_ref[rows, :] = proj[:, ATTN_WIDTH:ATTN_WIDTH + KV_WIDTH].astype(BF16)
    v_ref[rows, :] = proj[:, ATTN_WIDTH + KV_WIDTH:ATTN_WIDTH + 2 * KV_WIDTH].astype(BF16)
    u_ref[rows, :] = proj[:, ATTN_WIDTH + 2 * KV_WIDTH:]


def _ffn_body(read_x, gain_ref, fgain_ref, wg_hbm, wu_hbm, wd_hbm, o_ref,
              wg_ref, wu_ref, wd_ref, act_ref, gu_stage, d_stage, sem, *, final_norm,
              epilogue=None):
    nj = D_FF // FF_TILE

    def weight_copies(j, slot):
        span = pl.ds(j * FF_TILE, FF_TILE)
        return (pltpu.make_async_copy(wg_hbm.at[:, span], gu_stage.at[0, slot], sem.at[0, slot]),
                pltpu.make_async_copy(wu_hbm.at[:, span], gu_stage.at[1, slot], sem.at[1, slot]),
                pltpu.make_async_copy(wd_hbm.at[span, :], d_stage.at[slot], sem.at[2, slot]))

    def step(stage_weights):
        if stage_weights:
            for copy in weight_copies(0, 0):
                copy.start()
        sub = o_ref.shape[0] // FFN_SUBTILES
        halves = [slice(h * sub, (h + 1) * sub) for h in range(FFN_SUBTILES)]
        xs = [read_x(rows) for rows in halves]
        hns = [(_rms(x) * gain_ref[...]).astype(BF16) for x in xs]
        for j in range(nj):
            cols = slice(j * FF_TILE, (j + 1) * FF_TILE)
            if stage_weights:
                slot = j % 2
                if j + 1 < nj:
                    for copy in weight_copies(j + 1, 1 - slot):
                        copy.start()
                for copy in weight_copies(j, slot):
                    copy.wait()
                wg_ref[:, cols] = gu_stage[0, slot].astype(BF16)
                wu_ref[:, cols] = gu_stage[1, slot].astype(BF16)
                wd_ref[cols, :] = d_stage[slot].astype(BF16)
            for rows, hn in zip(halves, hns):
                g = jnp.dot(hn, wg_ref[:, cols], preferred_element_type=F32)
                u = jnp.dot(hn, wu_ref[:, cols], preferred_element_type=F32)
                act_ref[rows, cols] = (g * jax.nn.sigmoid(g) * u).astype(BF16)
        for rows, x in zip(halves, xs):
            y = x + 0.5 * jnp.dot(act_ref[rows, :], wd_ref[...], preferred_element_type=F32)
            if final_norm:
                y = _rms(y) * fgain_ref[...]
            o_ref[rows, :] = y
            if epilogue is not None:
                epilogue(y, rows)

    pl.when(pl.program_id(0) == 0)(functools.partial(step, True))
    pl.when(pl.program_id(0) != 0)(functools.partial(step, False))


def _ffn_proj_kernel(x_ref, mgain_ref, win_ref, gain_ref, fgain_ref, wg_hbm, wu_hbm, wd_hbm,
                     o_ref, q_ref, k_ref, v_ref, u_ref, *scratch, final_norm):
    emit = lambda y, rows: _mixer_inputs(y, rows, mgain_ref, win_ref, q_ref, k_ref, v_ref, u_ref)
    _ffn_body(lambda rows: x_ref[rows, :], gain_ref, fgain_ref, wg_hbm, wu_hbm, wd_hbm, o_ref,
              *scratch, final_norm=final_norm, epilogue=emit)


def _mix_ffn_kernel(x_ref, attn_ref, ssm_ref, gw_ref, gb_ref, ga_ref, gs_ref, wo_ref, *ffn_refs,
                    final_norm):
    read_x = lambda rows: x_ref[rows, :] + _mixed_update(rows, attn_ref, ssm_ref, gw_ref, gb_ref,
                                                         ga_ref, gs_ref, wo_ref)
    _ffn_body(read_x, *ffn_refs, final_norm=final_norm)


def _ffn(x, gain, w_gate, w_up, w_down, final_gain, final_norm, mix=None, proj=None):
    assert (mix is None) != (proj is None)
    t = x.shape[0]
    tile = FFN_TOKEN_TILE
    vec = lambda width: pl.BlockSpec((1, width), lambda i: (0, 0))
    resident = lambda r, c: pl.BlockSpec((r, c), lambda i: (0, 0), pipeline_mode=pl.Buffered(1))
    row = lambda width: pl.BlockSpec((tile, width), lambda i: (i, 0))
    hbm = pl.BlockSpec(memory_space=pl.ANY)
    in_specs, args = [row(D_MODEL)], [x]
    out_specs, out_shape = [row(D_MODEL)], [jax.ShapeDtypeStruct((t, D_MODEL), F32)]
    if mix is not None:
        attn, ssm_pre, glu_w, glu_b, attn_gain, ssm_gain, w_out = mix
        body = _mix_ffn_kernel
        in_specs += [row(ATTN_WIDTH), row(SSM_WIDTH),
                     resident(SSM_WIDTH, SSM_WIDTH), vec(SSM_WIDTH), vec(ATTN_WIDTH),
                     vec(SSM_WIDTH), resident(D_MODEL, D_MODEL)]
        args += [attn, ssm_pre, glu_w, glu_b.reshape(1, -1).astype(F32),
                 attn_gain.reshape(1, -1).astype(F32), ssm_gain.reshape(1, -1).astype(F32), w_out]
    if proj is not None:
        mixer_gain, w_in = proj
        body = _ffn_proj_kernel
        in_specs += [vec(D_MODEL), resident(D_MODEL, IN_WIDTH)]
        args += [mixer_gain.reshape(1, D_MODEL), w_in]
        for width, dtype in ((ATTN_WIDTH, BF16), (KV_WIDTH, BF16), (KV_WIDTH, BF16),
                             (SSM_WIDTH, F32)):
            out_specs.append(row(width))
            out_shape.append(jax.ShapeDtypeStruct((t, width), dtype))
    in_specs += [vec(D_MODEL), vec(D_MODEL), hbm, hbm, hbm]
    args += [gain.reshape(1, D_MODEL), final_gain.reshape(1, D_MODEL), w_gate, w_up, w_down]
    return pl.pallas_call(
        functools.partial(body, final_norm=final_norm),
        grid=(t // tile,),
        in_specs=in_specs,
        out_specs=out_specs,
        out_shape=out_shape,
        scratch_shapes=[pltpu.VMEM((D_MODEL, D_FF), BF16),
                        pltpu.VMEM((D_MODEL, D_FF), BF16),
                        pltpu.VMEM((D_FF, D_MODEL), BF16),
                        pltpu.VMEM((tile, D_FF), BF16),
                        pltpu.VMEM((2, 2, D_MODEL, FF_TILE), F32),
                        pltpu.VMEM((2, FF_TILE, D_MODEL), F32),
                        pltpu.SemaphoreType.DMA((3, 2))],
        compiler_params=pltpu.CompilerParams(
            dimension_semantics=("arbitrary",), vmem_limit_bytes=VMEM_LIMIT_BYTES),
    )(*args)


def _attn_kernel(sink_ref, q_ref, kp_ref, kc_ref, kn_ref, vp_ref, vc_ref, vn_ref, o_ref,
                 bias_ref, k_ref, v_ref, s_ref):
    n = pl.program_id(1)
    last = pl.num_programs(1) - 1
    nblk = ATTN_BLOCKS_PER_STEP

    @pl.when(n == 0)
    def _():
        kj = lax.broadcasted_iota(jnp.int32, (3 * BLOCK, BLOCK), 0)
        qi = lax.broadcasted_iota(jnp.int32, (3 * BLOCK, BLOCK), 1)
        rel = jnp.abs(kj - BLOCK - qi)
        dist = rel.astype(F32)
        inside = rel <= WINDOW
        has_prev = kj >= BLOCK
        has_next = kj < 2 * BLOCK
        for variant, ok in enumerate((inside & has_prev, inside, inside & has_next)):
            for h in range(ATTN_HEADS):
                slope = float(2.0 ** (-8.0 * (h + 1) / ATTN_HEADS))
                bias_ref[variant, h] = jnp.where(ok, (-slope * LOG2_E) * dist, NEG_INF)

    k_ref[0:BLOCK] = kp_ref[...]
    k_ref[BLOCK:(nblk + 1) * BLOCK] = kc_ref[...]
    k_ref[(nblk + 1) * BLOCK:(nblk + 2) * BLOCK] = kn_ref[...]
    v_ref[0:BLOCK] = vp_ref[...]
    v_ref[BLOCK:(nblk + 1) * BLOCK] = vc_ref[...]
    v_ref[(nblk + 1) * BLOCK:(nblk + 2) * BLOCK] = vn_ref[...]

    for j in range(nblk):
        for kh in range(ATTN_KV_HEADS):
            kcat = k_ref[j * BLOCK:(j + 3) * BLOCK, kh * HEAD_DIM:(kh + 1) * HEAD_DIM]
            heads = [kh * Q_PER_KV + g for g in range(Q_PER_KV)]
            qs = jnp.concatenate(
                [q_ref[j * BLOCK:(j + 1) * BLOCK, h * HEAD_DIM:(h + 1) * HEAD_DIM] for h in heads],
                axis=0)
            s_ref[j, kh] = lax.dot_general(kcat, qs, NT_DIMS,
                                           preferred_element_type=F32)

    for j in range(nblk):
        variant = 1
        if j == 0:
            variant = jnp.where(n == 0, 0, variant)
        if j == nblk - 1:
            variant = jnp.where(n == last, 2, variant)
        outs = []
        for kh in range(ATTN_KV_HEADS):
            vcat = v_ref[j * BLOCK:(j + 3) * BLOCK, kh * HEAD_DIM:(kh + 1) * HEAD_DIM]
            for g in range(Q_PER_KV):
                h = kh * Q_PER_KV + g
                s = s_ref[j, kh, :, g * BLOCK:(g + 1) * BLOCK] + bias_ref[variant, h]
                sink = sink_ref[h] * LOG2_E
                m = jnp.maximum(jnp.max(s, axis=0, keepdims=True), sink)
                e = jnp.exp2(s - m)
                den = jnp.sum(e, axis=0, keepdims=True) + jnp.exp2(sink - m)
                pv = lax.dot_general(vcat, e.astype(BF16), TN_DIMS,
                                     preferred_element_type=F32)
                outs.append(pv / den)
        o_ref[j * BLOCK:(j + 1) * BLOCK, :] = jnp.concatenate(outs, axis=0).T.astype(o_ref.dtype)


def _attention(q, k, v, sinks):
    b, seq, _ = q.shape
    nblk = ATTN_BLOCKS_PER_STEP
    steps = seq // (nblk * BLOCK)
    nb = seq // BLOCK
    assert seq % (nblk * BLOCK) == 0 and nb >= 2
    edge = lambda f: pl.BlockSpec((None, BLOCK, KV_WIDTH), f)
    body = pl.BlockSpec((None, nblk * BLOCK, KV_WIDTH), lambda bi, n: (bi, n, 0))
    prev = lambda bi, n: (bi, jnp.maximum(n * nblk - 1, 0), 0)
    nxt = lambda bi, n: (bi, jnp.minimum((n + 1) * nblk, nb - 1), 0)
    rows = pl.BlockSpec((None, nblk * BLOCK, ATTN_WIDTH), lambda bi, n: (bi, n, 0))
    return pl.pallas_call(
        _attn_kernel,
        grid=(b, steps),
        in_specs=[pl.BlockSpec(memory_space=pltpu.SMEM), rows,
                  edge(prev), body, edge(nxt), edge(prev), body, edge(nxt)],
        out_specs=rows,
        out_shape=jax.ShapeDtypeStruct((b, seq, ATTN_WIDTH), BF16),
        scratch_shapes=[pltpu.VMEM((3, ATTN_HEADS, 3 * BLOCK, BLOCK), F32),
                        pltpu.VMEM(((nblk + 2) * BLOCK, KV_WIDTH), BF16),
                        pltpu.VMEM(((nblk + 2) * BLOCK, KV_WIDTH), BF16),
                        pltpu.VMEM((nblk, ATTN_KV_HEADS, 3 * BLOCK, Q_PER_KV * BLOCK), F32)],
        compiler_params=pltpu.CompilerParams(
            dimension_semantics=("parallel", "arbitrary"), vmem_limit_bytes=VMEM_LIMIT_BYTES),
    )(sinks.astype(F32), q, k, k, k, v, v, v)


def _dot_nt_split(a, b):
    a_hi, b_hi = a.astype(BF16), b.astype(BF16)
    a_lo = (a - a_hi.astype(F32)).astype(BF16)
    b_lo = (b - b_hi.astype(F32)).astype(BF16)
    dot = functools.partial(lax.dot_general, dimension_numbers=NT_DIMS,
                            preferred_element_type=F32)
    return dot(a_hi, b_hi) + dot(a_hi, b_lo) + dot(a_lo, b_hi)


def _ssm_ops_kernel(prm_ref, bc_ref, ein_ref, toep_ref, eout_ref, aq_ref):
    q, hc, p = SSM_CHUNK, SSM_CH, SSM_STATE
    fwd = lax.broadcasted_iota(jnp.int32, (1, 2 * p), 1) < p
    zero_row = jnp.zeros((1, 2 * p), F32)
    row_id = lax.broadcasted_iota(jnp.int32, (CHUNK_WIDTH, CHUNK_WIDTH), 0)
    col_id = lax.broadcasted_iota(jnp.int32, (CHUNK_WIDTH, CHUNK_WIDTH), 1)

    def table(select, n):
        picks = [select(m) for m in range(n)]
        re = jnp.concatenate([jnp.broadcast_to(r, (hc, 2 * p)) for r, _ in picks], axis=0)
        im = jnp.concatenate([jnp.broadcast_to(i, (hc, 2 * p)) for _, i in picks], axis=0)
        return re, im

    def tile_rows(x, n):
        return jnp.concatenate([x] * n, axis=0)

    for gi in range(prm_ref.shape[0]):
        lr = jnp.minimum(prm_ref[gi, 0:1, :], LAMBDA_RE_MAX)
        li = prm_ref[gi, 1:2, :]
        dt = jnp.exp(prm_ref[gi, 2:3, :])
        mag = jnp.exp(lr * dt)
        a_r = mag * jnp.cos(li * dt)
        a_i = mag * jnp.sin(li * dt)
        den = lr * lr + li * li
        coef_r = ((a_r - 1.0) * lr + a_i * li) / den
        coef_i = (a_i * lr - (a_r - 1.0) * li) / den
        b_r, b_i = bc_ref[gi, 0], bc_ref[gi, 1]
        c_r, c_i = bc_ref[gi, 2], bc_ref[gi, 3]
        bb_r = coef_r * b_r - coef_i * b_i
        bb_i = coef_r * b_i + coef_i * b_r

        pw = [(jnp.ones((1, 2 * p), F32), zero_row)]
        for _ in range(q):
            r, i = pw[-1]
            pw.append((r * a_r - i * a_i, r * a_i + i * a_r))

        def both(f_idx, b_idx):
            fr, fi = pw[f_idx] if f_idx is not None else (zero_row, zero_row)
            br, bi = pw[b_idx] if b_idx is not None else (zero_row, zero_row)
            return jnp.where(fwd, fr, br), jnp.where(fwd, fi, bi)

        p_r, p_i = table(lambda i: both(q - 1 - i, i), q)
        tb_r, tb_i = tile_rows(bb_r, q), tile_rows(bb_i, q)
        ein = jnp.concatenate([tb_r * p_r - tb_i * p_i, tb_r * p_i + tb_i * p_r], axis=1)
        ein_ref[gi] = ein.astype(BF16)

        p_r, p_i = table(lambda j: both(j + 1, q - j), q)
        tc_r, tc_i = tile_rows(c_r, q), tile_rows(c_i, q)
        eout = jnp.concatenate([tc_r * p_r - tc_i * p_i, -(tc_r * p_i + tc_i * p_r)], axis=1)
        eout_ref[gi] = eout.astype(BF16)

        def lag(m):
            return both(m - (q - 1) if q - 1 <= m <= 2 * q - 2 else None,
                        (q - 1) - m if m <= q - 1 else None)

        p_r, p_i = table(lag, 2 * q)
        tc_r, tc_i = tile_rows(c_r, 2 * q), tile_rows(c_i, 2 * q)
        cpt = jnp.concatenate([tc_r * p_r - tc_i * p_i, tc_r * p_i + tc_i * p_r], axis=1)
        bcat = jnp.concatenate([bb_r, -bb_i], axis=1)
        kern = _dot_nt_split(bcat, cpt)
        toep = jnp.concatenate(
            [kern[:, hc * (q - 1 - i):hc * (q - 1 - i) + CHUNK_WIDTH] for i in range(q)], axis=0)
        skip = jnp.concatenate([prm_ref[gi, 3:4, :]] * (CHUNK_WIDTH // (2 * p)), axis=1)
        toep_ref[gi] = (toep + jnp.where(row_id == col_id, skip, 0.0)).astype(BF16)

        aq_ref[gi] = jnp.concatenate(
            [pw[q][0], pw[q][1], jnp.zeros((SUBLANES - 2, 2 * p), F32)], axis=0)


def _ssm_operators(lam_re, lam_im, log_dt, b_re, b_im, c_re, c_im, d_skip):
    g, p, hc = SSM_GROUPS, SSM_STATE, SSM_CH
    lanes = lambda a: a.astype(F32).transpose(1, 0, 2).reshape(g, 2 * p)
    prm = jnp.stack(
        [lanes(lam_re), lanes(lam_im),
         jnp.repeat(log_dt.astype(F32).T, p, axis=1),
         jnp.tile(d_skip.astype(F32), (1, 2 * p // hc))]
        + [jnp.zeros((g, 2 * p), F32)] * (SUBLANES - 4), axis=1)
    bc = jnp.stack(
        [b_re.astype(F32).transpose(1, 3, 0, 2).reshape(g, hc, 2 * p),
         b_im.astype(F32).transpose(1, 3, 0, 2).reshape(g, hc, 2 * p),
         c_re.astype(F32).transpose(1, 2, 0, 3).reshape(g, hc, 2 * p),
         c_im.astype(F32).transpose(1, 2, 0, 3).reshape(g, hc, 2 * p)], axis=1)
    gs = OPS_GROUPS_PER_STEP
    mat = pl.BlockSpec((gs, CHUNK_WIDTH, CHUNK_WIDTH), lambda s: (s, 0, 0))
    mat_shape = jax.ShapeDtypeStruct((g, CHUNK_WIDTH, CHUNK_WIDTH), BF16)
    return pl.pallas_call(
        _ssm_ops_kernel,
        grid=(g // gs,),
        in_specs=[pl.BlockSpec((gs, SUBLANES, 2 * p), lambda s: (s, 0, 0)),
                  pl.BlockSpec((gs, 4, hc, 2 * p), lambda s: (s, 0, 0, 0))],
        out_specs=[mat, mat, mat, pl.BlockSpec((gs, SUBLANES, 2 * p), lambda s: (s, 0, 0))],
        out_shape=[mat_shape, mat_shape, mat_shape,
                   jax.ShapeDtypeStruct((g, SUBLANES, 2 * p), F32)],
        compiler_params=pltpu.CompilerParams(
            dimension_semantics=("parallel",), vmem_limit_bytes=VMEM_LIMIT_BYTES),
    )(prm, bc)


def _lane_roll(x, shift):
    shift %= LANES
    return jnp.concatenate([x[:, LANES - shift:], x[:, :LANES - shift]], axis=1)


def _block_transpose(v, lane_block):
    n = len(v)
    skewed = [v[i] if i == 0 else _lane_roll(v[i], SSM_CH * i) for i in range(n)]
    out = []
    for g in range(n):
        picked = skewed[(0 - g) % n]
        for j in range(1, n):
            picked = jnp.where(lane_block == j, skewed[(j - g) % n], picked)
        out.append(picked if g == 0 else _lane_roll(picked, -SSM_CH * g))
    return out


def _ssm_kernel(u_hbm, ein_ref, toep_ref, eout_ref, aq_ref, y_hbm,
                io_ref, ug_ref, s_ref, x_ref, in_sem, out_sem, *, batch, seq, pitch):
    q, half = SSM_CHUNK, SSM_STATE
    n_chunks = seq // q
    gpb = GROUPS_PER_BLOCK
    slab = 2 * batch
    lane_block = lax.broadcasted_iota(jnp.int32, (slab, LANES), 1) // SSM_CH
    step_id = pl.program_id(0)
    last_step = pl.num_programs(0) - 1
    slot = lax.rem(step_id, 2)

    def column_copies(block, slot_, to_vmem):
        lanes = pl.ds(pl.multiple_of(block * LANES, LANES), LANES)
        copies = []
        for b in range(batch):
            hbm = (u_hbm if to_vmem else y_hbm).at[b, :, lanes]
            vmem = io_ref.at[slot_, pl.ds(b * pitch, seq), :]
            if to_vmem:
                copies.append(pltpu.make_async_copy(hbm, vmem, in_sem.at[slot_, b]))
            else:
                copies.append(pltpu.make_async_copy(vmem, hbm, out_sem.at[slot_, b]))
        return copies

    def start(copies):
        for copy in copies:
            copy.start()

    def wait(copies):
        for copy in copies:
            copy.wait()

    @pl.when(step_id == 0)
    def _():
        start(column_copies(0, 0, True))

    @pl.when(step_id < last_step)
    def _():
        @pl.when(step_id >= 1)
        def _():
            wait(column_copies(step_id - 1, 1 - slot, False))
        start(column_copies(step_id + 1, 1 - slot, True))

    wait(column_copies(step_id, slot, True))
    buf = io_ref.at[slot]

    def gather(cp, carry):
        rows = pl.ds(pl.multiple_of(cp * slab, slab), slab)
        for part in range(q // gpb):
            t0 = cp * 2 * q + part * gpb
            v = [jnp.concatenate([buf[pl.ds(t0 + i, batch, stride=pitch), :],
                                  buf[pl.ds(t0 + q + i, batch, stride=pitch), :]],
                                 axis=0).astype(BF16) for i in range(gpb)]
            w = _block_transpose(v, lane_block)
            for g in range(gpb):
                ug_ref[g, rows, part * LANES:(part + 1) * LANES] = w[g]
        return carry

    lax.fori_loop(0, n_chunks // 2, gather, 0, unroll=SSM_RELAYOUT_UNROLL)

    fwd_lane = lax.broadcasted_iota(jnp.int32, (batch, 2 * half), 1) < half
    zeros = jnp.zeros((batch, half), F32)
    last_rows = pl.ds((n_chunks - 1) * batch, batch)
    ni = SSM_INTERLEAVE
    for g0 in range(0, gpb, ni):
        for gi in range(ni):
            s_ref[gi] = jnp.dot(ug_ref[g0 + gi], ein_ref[g0 + gi], preferred_element_type=F32)
            x_ref[gi, 0:batch, 0:half] = zeros
            x_ref[gi, 0:batch, 2 * half:3 * half] = zeros
            x_ref[gi, last_rows, half:2 * half] = zeros
            x_ref[gi, last_rows, 3 * half:4 * half] = zeros

        def step(k, carry):
            rf = pl.multiple_of(k * batch, batch)
            rb = pl.multiple_of((n_chunks - 1 - k) * batch, batch)
            new = []
            for gi in range(ni):
                xr, xi = carry[gi]
                sre = jnp.where(fwd_lane, s_ref[gi, pl.ds(rf, batch), 0:2 * half],
                                s_ref[gi, pl.ds(rb, batch), 0:2 * half])
                sim = jnp.where(fwd_lane, s_ref[gi, pl.ds(rf, batch), 2 * half:4 * half],
                                s_ref[gi, pl.ds(rb, batch), 2 * half:4 * half])
                ar = aq_ref[g0 + gi, 0:1, :]
                ai = aq_ref[g0 + gi, 1:2, :]
                nr = ar * xr - ai * xi + sre
                nim = ar * xi + ai * xr + sim
                x_ref[gi, pl.ds(rf + batch, batch), 0:half] = nr[:, 0:half]
                x_ref[gi, pl.ds(rf + batch, batch), 2 * half:3 * half] = nim[:, 0:half]
                x_ref[gi, pl.ds(rb - batch, batch), half:2 * half] = nr[:, half:2 * half]
                x_ref[gi, pl.ds(rb - batch, batch), 3 * half:4 * half] = nim[:, half:2 * half]
                new.append((nr, nim))
            return tuple(new)

        init = tuple((jnp.zeros((batch, 2 * half), F32), jnp.zeros((batch, 2 * half), F32))
                     for _ in range(ni))
        lax.fori_loop(0, n_chunks - 1, step, init)

        for gi in range(ni):
            g = g0 + gi
            y = jnp.dot(ug_ref[g], toep_ref[g], preferred_element_type=F32)
            y += lax.dot_general(x_ref[gi].astype(BF16), eout_ref[g], NT_DIMS,
                                 preferred_element_type=F32)
            ug_ref[g] = y.astype(BF16)

    def scatter(cp, carry):
        rows = pl.ds(pl.multiple_of(cp * slab, slab), slab)
        for part in range(q // gpb):
            t0 = cp * 2 * q + part * gpb
            v = [ug_ref[g, rows, part * LANES:(part + 1) * LANES] for g in range(gpb)]
            w = _block_transpose(v, lane_block)
            for j in range(gpb):
                wj = w[j].astype(F32)
                buf[pl.ds(t0 + j, batch, stride=pitch), :] = wj[0:batch]
                buf[pl.ds(t0 + q + j, batch, stride=pitch), :] = wj[batch:slab]
        return carry

    lax.fori_loop(0, n_chunks // 2, scatter, 0, unroll=SSM_RELAYOUT_UNROLL)

    start(column_copies(step_id, slot, False))

    @pl.when(step_id == last_step)
    def _():
        @pl.when(step_id >= 1)
        def _():
            wait(column_copies(step_id - 1, 1 - slot, False))
        wait(column_copies(step_id, slot, False))


def _padded_seq(seq):
    return seq + SUBLANES if seq % (2 * SUBLANES) == 0 else seq


def _ssm(u, lam_re, lam_im, log_dt, b_re, b_im, c_re, c_im, d_skip):
    batch, seq, _ = u.shape
    pitch = _padded_seq(seq)
    rows = batch * seq // SSM_CHUNK
    ein, toep, eout, a_q = _ssm_operators(lam_re, lam_im, log_dt, b_re, b_im, c_re, c_im, d_skip)
    gpb = GROUPS_PER_BLOCK
    mat = pl.BlockSpec((gpb, CHUNK_WIDTH, CHUNK_WIDTH), lambda s: (s, 0, 0))
    hbm = pl.BlockSpec(memory_space=pl.ANY)
    return pl.pallas_call(
        functools.partial(_ssm_kernel, batch=batch, seq=seq, pitch=pitch),
        grid=(SSM_GROUPS // gpb,),
        in_specs=[hbm, mat, mat, mat,
                  pl.BlockSpec((gpb, SUBLANES, 2 * SSM_STATE), lambda s: (s, 0, 0))],
        out_specs=hbm,
        out_shape=jax.ShapeDtypeStruct((batch, seq, SSM_WIDTH), F32),
        scratch_shapes=[pltpu.VMEM((2, batch * pitch, LANES), F32),
                        pltpu.VMEM((gpb, rows, CHUNK_WIDTH), BF16),
                        pltpu.VMEM((SSM_INTERLEAVE, rows, 4 * SSM_STATE), F32),
                        pltpu.VMEM((SSM_INTERLEAVE, rows, 4 * SSM_STATE), F32),
                        pltpu.SemaphoreType.DMA((2, batch)),
                        pltpu.SemaphoreType.DMA((2, batch))],
        compiler_params=pltpu.CompilerParams(
            dimension_semantics=("arbitrary",), vmem_limit_bytes=VMEM_LIMIT_BYTES),
    )(u, ein, toep, eout, a_q)


def kernel(x, norm_ffn1, ffn1_w_gate, ffn1_w_up, ffn1_w_down, norm_mix, w_in, attn_sinks,
           ssm_lambda_re, ssm_lambda_im, ssm_log_dt, ssm_b_re, ssm_b_im, ssm_c_re, ssm_c_im,
           ssm_d, ssm_glu_w, ssm_glu_b, attn_out_norm, ssm_out_norm, w_out,
           norm_ffn2, ffn2_w_gate, ffn2_w_up, ffn2_w_down, final_norm):
    b, seq, d = x.shape
    depth = norm_ffn1.shape[0]
    assert d == D_MODEL and seq % BLOCK == 0 and (b * seq) % FFN_TOKEN_TILE == 0
    assert b == SUBLANES and seq % (2 * SSM_CHUNK * SSM_RELAYOUT_UNROLL) == 0
    h = x.reshape(b * seq, d).astype(F32)
    for l in range(depth):
        h, q, k, v, u = _ffn(h, norm_ffn1[l], ffn1_w_gate[l], ffn1_w_up[l], ffn1_w_down[l],
                             final_norm, final_norm=False, proj=(norm_mix[l], w_in[l]))
        attn = _attention(q.reshape(b, seq, -1), k.reshape(b, seq, -1), v.reshape(b, seq, -1),
                          attn_sinks[l])
        ssm_pre = _ssm(u.reshape(b, seq, -1), ssm_lambda_re[l], ssm_lambda_im[l], ssm_log_dt[l],
                       ssm_b_re[l], ssm_b_im[l], ssm_c_re[l], ssm_c_im[l], ssm_d[l])
        mix = (attn.reshape(b * seq, -1), ssm_pre.reshape(b * seq, -1), ssm_glu_w[l],
               ssm_glu_b[l], attn_out_norm[l], ssm_out_norm[l], w_out[l])
        h, = _ffn(h, norm_ffn2[l], ffn2_w_gate[l], ffn2_w_up[l], ffn2_w_down[l],
                  final_norm, final_norm=(l == depth - 1), mix=mix)
    return h.reshape(b, seq, d).astype(x.dtype)
```

```python
import functools
import math

import jax
import jax.numpy as jnp
from jax import lax
from jax.experimental import pallas as pl
from jax.experimental.pallas import tpu as pltpu

F32 = jnp.float32
BF16 = jnp.bfloat16

D_MODEL = 1024
ATTN_HEADS = 8
ATTN_KV_HEADS = 2
Q_PER_KV = ATTN_HEADS // ATTN_KV_HEADS
HEAD_DIM = 64
ATTN_WIDTH = ATTN_HEADS * HEAD_DIM
KV_WIDTH = ATTN_KV_HEADS * HEAD_DIM
WINDOW = 128
BLOCK = 128
SSM_CH = 16
SSM_WIDTH = D_MODEL - ATTN_WIDTH
SSM_GROUPS = SSM_WIDTH // SSM_CH
SSM_STATE = 64
IN_WIDTH = ATTN_WIDTH + 2 * KV_WIDTH + SSM_WIDTH
D_FF = 2816
EPS = 1e-6
NEG_INF = -1e30
LAMBDA_RE_MAX = -1e-4
LOG2_E = math.log2(math.e)
QUERY_SCALE = HEAD_DIM ** -0.5 * LOG2_E

LANES = 128
SUBLANES = 8
VMEM_LIMIT_BYTES = 56 * 1024 * 1024

FFN_TOKEN_TILE = 512
FFN_SUBTILES = 2
FF_TILE = 256
ATTN_BLOCKS_PER_STEP = 8
SSM_CHUNK = 16
CHUNK_WIDTH = SSM_CHUNK * SSM_CH
GROUPS_PER_BLOCK = LANES // SSM_CH
SSM_INTERLEAVE = 8
SSM_RELAYOUT_UNROLL = 4
OPS_GROUPS_PER_STEP = 4

NT_DIMS = (((1,), (1,)), ((), ()))
TN_DIMS = (((0,), (0,)), ((), ()))


def _rms(x):
    return x * lax.rsqrt(jnp.mean(x * x, axis=-1, keepdims=True) + EPS)


def _mixed_update(rows, attn_ref, ssm_ref, gw_ref, gb_ref, ga_ref, gs_ref, wo_ref):
    y = jax.nn.gelu(ssm_ref[rows, :])
    z = jnp.dot(y.astype(BF16), gw_ref[...].astype(BF16), preferred_element_type=F32) + gb_ref[...]
    s = y * jax.nn.sigmoid(z)
    sn = _rms(s) * gs_ref[...]
    an = _rms(attn_ref[rows, :].astype(F32)) * ga_ref[...]
    mixed = jnp.concatenate([an, sn], axis=-1).astype(BF16)
    return jnp.dot(mixed, wo_ref[...].astype(BF16), preferred_element_type=F32)


def _mixer_inputs(y, rows, gain_ref, w_ref, q_ref, k_ref, v_ref, u_ref):
    hn = (_rms(y) * gain_ref[...]).astype(BF16)
    proj = jnp.dot(hn, w_ref[...].astype(BF16), preferred_element_type=F32)
    q_ref[rows, :] = (proj[:, :ATTN_WIDTH] * QUERY_SCALE).astype(BF16)
    k_ref[rows, :] = proj[:, ATTN_WIDTH:ATTN_WIDTH + KV_WIDTH].astype(BF16)
    v_ref[rows, :] = proj[:, ATTN_WIDTH + KV_WIDTH:ATTN_WIDTH + 2 * KV_WIDTH].astype(BF16)
    u_ref[rows, :] = proj[:, ATTN_WIDTH + 2 * KV_WIDTH:]


def _ffn_body(read_x, gain_ref, fgain_ref, wg_hbm, wu_hbm, wd_hbm, o_ref,
              wg_ref, wu_ref, wd_ref, act_ref, gu_stage, d_stage, sem, *, final_norm,
              epilogue=None):
    nj = D_FF // FF_TILE

    def weight_copies(j, slot):
        span = pl.ds(j * FF_TILE, FF_TILE)
        return (pltpu.make_async_copy(wg_hbm.at[:, span], gu_stage.at[0, slot], sem.at[0, slot]),
                pltpu.make_async_copy(wu_hbm.at[:, span], gu_stage.at[1, slot], sem.at[1, slot]),
                pltpu.make_async_copy(wd_hbm.at[span, :], d_stage.at[slot], sem.at[2, slot]))

    def step(stage_weights):
        if stage_weights:
            for copy in weight_copies(0, 0):
                copy.start()
        sub = o_ref.shape[0] // FFN_SUBTILES
        halves = [slice(h * sub, (h + 1) * sub) for h in range(FFN_SUBTILES)]
        xs = [read_x(rows) for rows in halves]
        hns = [(_rms(x) * gain_ref[...]).astype(BF16) for x in xs]
        for j in range(nj):
            cols = slice(j * FF_TILE, (j + 1) * FF_TILE)
            if stage_weights:
                slot = j % 2
                if j + 1 < nj:
                    for copy in weight_copies(j + 1, 1 - slot):
                        copy.start()
                for copy in weight_copies(j, slot):
                    copy.wait()
                wg_ref[:, cols] = gu_stage[0, slot].astype(BF16)
                wu_ref[:, cols] = gu_stage[1, slot].astype(BF16)
                wd_ref[cols, :] = d_stage[slot].astype(BF16)
            for rows, hn in zip(halves, hns):
                g = jnp.dot(hn, wg_ref[:, cols], preferred_element_type=F32)
                u = jnp.dot(hn, wu_ref[:, cols], preferred_element_type=F32)
                act_ref[rows, cols] = (g * jax.nn.sigmoid(g) * u).astype(BF16)
        for rows, x in zip(halves, xs):
            y = x + 0.5 * jnp.dot(act_ref[rows, :], wd_ref[...], preferred_element_type=F32)
            if final_norm:
                y = _rms(y) * fgain_ref[...]
            o_ref[rows, :] = y
            if epilogue is not None:
                epilogue(y, rows)

    pl.when(pl.program_id(0) == 0)(functools.partial(step, True))
    pl.when(pl.program_id(0) != 0)(functools.partial(step, False))


def _ffn_proj_kernel(x_ref, mgain_ref, win_ref, gain_ref, fgain_ref, wg_hbm, wu_hbm, wd_hbm,
                     o_ref, q_ref, k_ref, v_ref, u_ref, *scratch, final_norm):
    emit = lambda y, rows: _mixer_inputs(y, rows, mgain_ref, win_ref, q_ref, k_ref, v_ref, u_ref)
    _ffn_body(lambda rows: x_ref[rows, :], gain_ref, fgain_ref, wg_hbm, wu_hbm, wd_hbm, o_ref,
              *scratch, final_norm=final_norm, epilogue=emit)


def _mix_ffn_kernel(x_ref, attn_ref, ssm_ref, gw_ref, gb_ref, ga_ref, gs_ref, wo_ref, *ffn_refs,
                    final_norm):
    read_x = lambda rows: x_ref[rows, :] + _mixed_update(rows, attn_ref, ssm_ref, gw_ref, gb_ref,
                                                         ga_ref, gs_ref, wo_ref)
    _ffn_body(read_x, *ffn_refs, final_norm=final_norm)


def _ffn(x, gain, w_gate, w_up, w_down, final_gain, final_norm, mix=None, proj=None):
    assert (mix is None) != (proj is None)
    t = x.shape[0]
    tile = FFN_TOKEN_TILE
    vec = lambda width: pl.BlockSpec((1, width), lambda i: (0, 0))
    resident = lambda r, c: pl.BlockSpec((r, c), lambda i: (0, 0), pipeline_mode=pl.Buffered(1))
    row = lambda width: pl.BlockSpec((tile, width), lambda i: (i, 0))
    hbm = pl.BlockSpec(memory_space=pl.ANY)
    in_specs, args = [row(D_MODEL)], [x]
    out_specs, out_shape = [row(D_MODEL)], [jax.ShapeDtypeStruct((t, D_MODEL), F32)]
    if mix is not None:
        attn, ssm_pre, glu_w, glu_b, attn_gain, ssm_gain, w_out = mix
        body = _mix_ffn_kernel
        in_specs += [row(ATTN_WIDTH), row(SSM_WIDTH),
                     resident(SSM_WIDTH, SSM_WIDTH), vec(SSM_WIDTH), vec(ATTN_WIDTH),
                     vec(SSM_WIDTH), resident(D_MODEL, D_MODEL)]
        args += [attn, ssm_pre, glu_w, glu_b.reshape(1, -1).astype(F32),
                 attn_gain.reshape(1, -1).astype(F32), ssm_gain.reshape(1, -1).astype(F32), w_out]
    if proj is not None:
        mixer_gain, w_in = proj
        body = _ffn_proj_kernel
        in_specs += [vec(D_MODEL), resident(D_MODEL, IN_WIDTH)]
        args += [mixer_gain.reshape(1, D_MODEL), w_in]
        for width, dtype in ((ATTN_WIDTH, BF16), (KV_WIDTH, BF16), (KV_WIDTH, BF16),
                             (SSM_WIDTH, F32)):
            out_specs.append(row(width))
            out_shape.append(jax.ShapeDtypeStruct((t, width), dtype))
    in_specs += [vec(D_MODEL), vec(D_MODEL), hbm, hbm, hbm]
    args += [gain.reshape(1, D_MODEL), final_gain.reshape(1, D_MODEL), w_gate, w_up, w_down]
    return pl.pallas_call(
        functools.partial(body, final_norm=final_norm),
        grid=(t // tile,),
        in_specs=in_specs,
        out_specs=out_specs,
        out_shape=out_shape,
        scratch_shapes=[pltpu.VMEM((D_MODEL, D_FF), BF16),
                        pltpu.VMEM((D_MODEL, D_FF), BF16),
                        pltpu.VMEM((D_FF, D_MODEL), BF16),
                        pltpu.VMEM((tile, D_FF), BF16),
                        pltpu.VMEM((2, 2, D_MODEL, FF_TILE), F32),
                        pltpu.VMEM((2, FF_TILE, D_MODEL), F32),
                        pltpu.SemaphoreType.DMA((3, 2))],
        compiler_params=pltpu.CompilerParams(
            dimension_semantics=("arbitrary",), vmem_limit_bytes=VMEM_LIMIT_BYTES),
    )(*args)


def _attn_kernel(sink_ref, q_ref, kp_ref, kc_ref, kn_ref, vp_ref, vc_ref, vn_ref, o_ref,
                 bias_ref, k_ref, v_ref, s_ref):
    n = pl.program_id(1)
    last = pl.num_programs(1) - 1
    nblk = ATTN_BLOCKS_PER_STEP

    @pl.when(n == 0)
    def _():
        kj = lax.broadcasted_iota(jnp.int32, (3 * BLOCK, BLOCK), 0)
        qi = lax.broadcasted_iota(jnp.int32, (3 * BLOCK, BLOCK), 1)
        rel = jnp.abs(kj - BLOCK - qi)
        dist = rel.astype(F32)
        inside = rel <= WINDOW
        has_prev = kj >= BLOCK
        has_next = kj < 2 * BLOCK
        for variant, ok in enumerate((inside & has_prev, inside, inside & has_next)):
            for h in range(ATTN_HEADS):
                slope = float(2.0 ** (-8.0 * (h + 1) / ATTN_HEADS))
                bias_ref[variant, h] = jnp.where(ok, (-slope * LOG2_E) * dist, NEG_INF)
        for kh in range(ATTN_KV_HEADS):
            v_ref[:, (2 * kh + 1) * HEAD_DIM:(2 * kh + 2) * HEAD_DIM] = jnp.ones(
                (v_ref.shape[0], HEAD_DIM), BF16)

    spans = ((slice(0, BLOCK), kp_ref, vp_ref),
             (slice(BLOCK, (nblk + 1) * BLOCK), kc_ref, vc_ref),
             (slice((nblk + 1) * BLOCK, (nblk + 2) * BLOCK), kn_ref, vn_ref))
    for rows, k_in, v_in in spans:
        k_ref[rows, :] = k_in[...]
        for kh in range(ATTN_KV_HEADS):
            v_ref[rows, 2 * kh * HEAD_DIM:(2 * kh + 1) * HEAD_DIM] = (
                v_in[:, kh * HEAD_DIM:(kh + 1) * HEAD_DIM])

    def scores(j, kh):
        kcat = k_ref[j * BLOCK:(j + 3) * BLOCK, kh * HEAD_DIM:(kh + 1) * HEAD_DIM]
        heads = [kh * Q_PER_KV + g for g in range(Q_PER_KV)]
        qs = jnp.concatenate(
            [q_ref[j * BLOCK:(j + 1) * BLOCK, h * HEAD_DIM:(h + 1) * HEAD_DIM] for h in heads],
            axis=0)
        s_ref[j, kh] = lax.dot_general(kcat, qs, NT_DIMS,
                                       preferred_element_type=F32)

    for kh in range(ATTN_KV_HEADS):
        scores(0, kh)
    for j in range(nblk):
        variant = 1
        if j == 0:
            variant = jnp.where(n == 0, 0, variant)
        if j == nblk - 1:
            variant = jnp.where(n == last, 2, variant)
        outs = []
        for kh in range(ATTN_KV_HEADS):
            if j + 1 < nblk:
                scores(j + 1, kh)
            v_ones = v_ref[j * BLOCK:(j + 3) * BLOCK, 2 * kh * HEAD_DIM:(2 * kh + 2) * HEAD_DIM]
            for g in range(Q_PER_KV):
                h = kh * Q_PER_KV + g
                s = s_ref[j, kh, :, g * BLOCK:(g + 1) * BLOCK] + bias_ref[variant, h]
                sink = sink_ref[h] * LOG2_E
                m = jnp.maximum(jnp.max(s, axis=0, keepdims=True), sink)
                e = jnp.exp2(s - m).astype(BF16)
                pv = lax.dot_general(v_ones, e, TN_DIMS, preferred_element_type=F32)
                den = pv[HEAD_DIM:HEAD_DIM + 1] + jnp.exp2(sink - m)
                outs.append(pv[:HEAD_DIM] / den)
        o_ref[j * BLOCK:(j + 1) * BLOCK, :] = jnp.concatenate(outs, axis=0).T.astype(o_ref.dtype)


def _attention(q, k, v, sinks):
    b, seq, _ = q.shape
    nblk = ATTN_BLOCKS_PER_STEP
    steps = seq // (nblk * BLOCK)
    nb = seq // BLOCK
    assert seq % (nblk * BLOCK) == 0 and nb >= 2
    edge = lambda f: pl.BlockSpec((None, BLOCK, KV_WIDTH), f)
    body = pl.BlockSpec((None, nblk * BLOCK, KV_WIDTH), lambda bi, n: (bi, n, 0))
    prev = lambda bi, n: (bi, jnp.maximum(n * nblk - 1, 0), 0)
    nxt = lambda bi, n: (bi, jnp.minimum((n + 1) * nblk, nb - 1), 0)
    rows = pl.BlockSpec((None, nblk * BLOCK, ATTN_WIDTH), lambda bi, n: (bi, n, 0))
    return pl.pallas_call(
        _attn_kernel,
        grid=(b, steps),
        in_specs=[pl.BlockSpec(memory_space=pltpu.SMEM), rows,
                  edge(prev), body, edge(nxt), edge(prev), body, edge(nxt)],
        out_specs=rows,
        out_shape=jax.ShapeDtypeStruct((b, seq, ATTN_WIDTH), BF16),
        scratch_shapes=[pltpu.VMEM((3, ATTN_HEADS, 3 * BLOCK, BLOCK), F32),
                        pltpu.VMEM(((nblk + 2) * BLOCK, KV_WIDTH), BF16),
                        pltpu.VMEM(((nblk + 2) * BLOCK, 2 * KV_WIDTH), BF16),
                        pltpu.VMEM((nblk, ATTN_KV_HEADS, 3 * BLOCK, Q_PER_KV * BLOCK), F32)],
        compiler_params=pltpu.CompilerParams(
            dimension_semantics=("parallel", "arbitrary"), vmem_limit_bytes=VMEM_LIMIT_BYTES),
    )(sinks.astype(F32), q, k, k, k, v, v, v)


def _dot_nt_split(a, b):
    a_hi, b_hi = a.astype(BF16), b.astype(BF16)
    a_lo = (a - a_hi.astype(F32)).astype(BF16)
    b_lo = (b - b_hi.astype(F32)).astype(BF16)
    dot = functools.partial(lax.dot_general, dimension_numbers=NT_DIMS,
                            preferred_element_type=F32)
    return dot(a_hi, b_hi) + dot(a_hi, b_lo) + dot(a_lo, b_hi)


def _ssm_ops_kernel(prm_ref, bc_ref, ein_ref, toep_ref, eout_ref, aq_ref):
    q, hc, p = SSM_CHUNK, SSM_CH, SSM_STATE
    fwd = lax.broadcasted_iota(jnp.int32, (1, 2 * p), 1) < p
    zero_row = jnp.zeros((1, 2 * p), F32)
    row_id = lax.broadcasted_iota(jnp.int32, (CHUNK_WIDTH, CHUNK_WIDTH), 0)
    col_id = lax.broadcasted_iota(jnp.int32, (CHUNK_WIDTH, CHUNK_WIDTH), 1)

    def table(select, n):
        picks = [select(m) for m in range(n)]
        re = jnp.concatenate([jnp.broadcast_to(r, (hc, 2 * p)) for r, _ in picks], axis=0)
        im = jnp.concatenate([jnp.broadcast_to(i, (hc, 2 * p)) for _, i in picks], axis=0)
        return re, im

    def tile_rows(x, n):
        return jnp.concatenate([x] * n, axis=0)

    for gi in range(prm_ref.shape[0]):
        lr = jnp.minimum(prm_ref[gi, 0:1, :], LAMBDA_RE_MAX)
        li = prm_ref[gi, 1:2, :]
        dt = jnp.exp(prm_ref[gi, 2:3, :])
        mag = jnp.exp(lr * dt)
        a_r = mag * jnp.cos(li * dt)
        a_i = mag * jnp.sin(li * dt)
        den = lr * lr + li * li
        coef_r = ((a_r - 1.0) * lr + a_i * li) / den
        coef_i = (a_i * lr - (a_r - 1.0) * li) / den
        b_r, b_i = bc_ref[gi, 0], bc_ref[gi, 1]
        c_r, c_i = bc_ref[gi, 2], bc_ref[gi, 3]
        bb_r = coef_r * b_r - coef_i * b_i
        bb_i = coef_r * b_i + coef_i * b_r

        pw = [(jnp.ones((1, 2 * p), F32), zero_row)]
        for _ in range(q):
            r, i = pw[-1]
            pw.append((r * a_r - i * a_i, r * a_i + i * a_r))

        def both(f_idx, b_idx):
            fr, fi = pw[f_idx] if f_idx is not None else (zero_row, zero_row)
            br, bi = pw[b_idx] if b_idx is not None else (zero_row, zero_row)
            return jnp.where(fwd, fr, br), jnp.where(fwd, fi, bi)

        p_r, p_i = table(lambda i: both(q - 1 - i, i), q)
        tb_r, tb_i = tile_rows(bb_r, q), tile_rows(bb_i, q)
        ein = jnp.concatenate([tb_r * p_r - tb_i * p_i, tb_r * p_i + tb_i * p_r], axis=1)
        ein_ref[gi] = ein.astype(BF16)

        p_r, p_i = table(lambda j: both(j + 1, q - j), q)
        tc_r, tc_i = tile_rows(c_r, q), tile_rows(c_i, q)
        eout = jnp.concatenate([tc_r * p_r - tc_i * p_i, -(tc_r * p_i + tc_i * p_r)], axis=1)
        eout_ref[gi] = eout.astype(BF16)

        def lag(m):
            return both(m - (q - 1) if q - 1 <= m <= 2 * q - 2 else None,
                        (q - 1) - m if m <= q - 1 else None)

        p_r, p_i = table(lag, 2 * q)
        tc_r, tc_i = tile_rows(c_r, 2 * q), tile_rows(c_i, 2 * q)
        cpt = jnp.concatenate([tc_r * p_r - tc_i * p_i, tc_r * p_i + tc_i * p_r], axis=1)
        bcat = jnp.concatenate([bb_r, -bb_i], axis=1)
        kern = _dot_nt_split(bcat, cpt)
        toep = jnp.concatenate(
            [kern[:, hc * (q - 1 - i):hc * (q - 1 - i) + CHUNK_WIDTH] for i in range(q)], axis=0)
        skip = jnp.concatenate([prm_ref[gi, 3:4, :]] * (CHUNK_WIDTH // (2 * p)), axis=1)
        toep_ref[gi] = (toep + jnp.where(row_id == col_id, skip, 0.0)).astype(BF16)

        aq_ref[gi] = jnp.concatenate(
            [pw[q][0], pw[q][1], jnp.zeros((SUBLANES - 2, 2 * p), F32)], axis=0)


def _ssm_operators(lam_re, lam_im, log_dt, b_re, b_im, c_re, c_im, d_skip):
    g, p, hc = SSM_GROUPS, SSM_STATE, SSM_CH
    lanes = lambda a: a.astype(F32).transpose(1, 0, 2).reshape(g, 2 * p)
    prm = jnp.stack(
        [lanes(lam_re), lanes(lam_im),
         jnp.repeat(log_dt.astype(F32).T, p, axis=1),
         jnp.tile(d_skip.astype(F32), (1, 2 * p // hc))]
        + [jnp.zeros((g, 2 * p), F32)] * (SUBLANES - 4), axis=1)
    bc = jnp.stack(
        [b_re.astype(F32).transpose(1, 3, 0, 2).reshape(g, hc, 2 * p),
         b_im.astype(F32).transpose(1, 3, 0, 2).reshape(g, hc, 2 * p),
         c_re.astype(F32).transpose(1, 2, 0, 3).reshape(g, hc, 2 * p),
         c_im.astype(F32).transpose(1, 2, 0, 3).reshape(g, hc, 2 * p)], axis=1)
    gs = OPS_GROUPS_PER_STEP
    mat = pl.BlockSpec((gs, CHUNK_WIDTH, CHUNK_WIDTH), lambda s: (s, 0, 0))
    mat_shape = jax.ShapeDtypeStruct((g, CHUNK_WIDTH, CHUNK_WIDTH), BF16)
    return pl.pallas_call(
        _ssm_ops_kernel,
        grid=(g // gs,),
        in_specs=[pl.BlockSpec((gs, SUBLANES, 2 * p), lambda s: (s, 0, 0)),
                  pl.BlockSpec((gs, 4, hc, 2 * p), lambda s: (s, 0, 0, 0))],
        out_specs=[mat, mat, mat, pl.BlockSpec((gs, SUBLANES, 2 * p), lambda s: (s, 0, 0))],
        out_shape=[mat_shape, mat_shape, mat_shape,
                   jax.ShapeDtypeStruct((g, SUBLANES, 2 * p), F32)],
        compiler_params=pltpu.CompilerParams(
            dimension_semantics=("parallel",), vmem_limit_bytes=VMEM_LIMIT_BYTES),
    )(prm, bc)


def _lane_roll(x, shift):
    shift %= LANES
    return jnp.concatenate([x[:, LANES - shift:], x[:, :LANES - shift]], axis=1)


def _block_transpose(v, lane_block):
    n = len(v)
    skewed = [v[i] if i == 0 else _lane_roll(v[i], SSM_CH * i) for i in range(n)]
    out = []
    for g in range(n):
        picked = skewed[(0 - g) % n]
        for j in range(1, n):
            picked = jnp.where(lane_block == j, skewed[(j - g) % n], picked)
        out.append(picked if g == 0 else _lane_roll(picked, -SSM_CH * g))
    return out


def _ssm_kernel(u_hbm, ein_ref, toep_ref, eout_ref, aq_ref, y_hbm,
                io_ref, ug_ref, s_ref, x_ref, in_sem, out_sem, *, batch, seq, pitch):
    q, half = SSM_CHUNK, SSM_STATE
    n_chunks = seq // q
    gpb = GROUPS_PER_BLOCK
    slab = 2 * batch
    lane_block = lax.broadcasted_iota(jnp.int32, (slab, LANES), 1) // SSM_CH
    step_id = pl.program_id(0)
    last_step = pl.num_programs(0) - 1
    slot = lax.rem(step_id, 2)

    def column_copies(block, slot_, to_vmem):
        lanes = pl.ds(pl.multiple_of(block * LANES, LANES), LANES)
        copies = []
        for b in range(batch):
            hbm = (u_hbm if to_vmem else y_hbm).at[b, :, lanes]
            vmem = io_ref.at[slot_, pl.ds(b * pitch, seq), :]
            if to_vmem:
                copies.append(pltpu.make_async_copy(hbm, vmem, in_sem.at[slot_, b]))
            else:
                copies.append(pltpu.make_async_copy(vmem, hbm, out_sem.at[slot_, b]))
        return copies

    def start(copies):
        for copy in copies:
            copy.start()

    def wait(copies):
        for copy in copies:
            copy.wait()

    @pl.when(step_id == 0)
    def _():
        start(column_copies(0, 0, True))

    @pl.when(step_id < last_step)
    def _():
        @pl.when(step_id >= 1)
        def _():
            wait(column_copies(step_id - 1, 1 - slot, False))
        start(column_copies(step_id + 1, 1 - slot, True))

    wait(column_copies(step_id, slot, True))
    buf = io_ref.at[slot]

    def gather(cp, carry):
        rows = pl.ds(pl.multiple_of(cp * slab, slab), slab)
        for part in range(q // gpb):
            t0 = cp * 2 * q + part * gpb
            v = [jnp.concatenate([buf[pl.ds(t0 + i, batch, stride=pitch), :],
                                  buf[pl.ds(t0 + q + i, batch, stride=pitch), :]],
                                 axis=0).astype(BF16) for i in range(gpb)]
            w = _block_transpose(v, lane_block)
            for g in range(gpb):
                ug_ref[g, rows, part * LANES:(part + 1) * LANES] = w[g]
        return carry

    lax.fori_loop(0, n_chunks // 2, gather, 0, unroll=SSM_RELAYOUT_UNROLL)

    fwd_lane = lax.broadcasted_iota(jnp.int32, (batch, 2 * half), 1) < half
    zeros = jnp.zeros((batch, half), F32)
    last_rows = pl.ds((n_chunks - 1) * batch, batch)
    ni = SSM_INTERLEAVE
    for g0 in range(0, gpb, ni):
        for gi in range(ni):
            s_ref[gi] = jnp.dot(ug_ref[g0 + gi], ein_ref[g0 + gi], preferred_element_type=F32)
            x_ref[gi, 0:batch, 0:half] = zeros
            x_ref[gi, 0:batch, 2 * half:3 * half] = zeros
            x_ref[gi, last_rows, half:2 * half] = zeros
            x_ref[gi, last_rows, 3 * half:4 * half] = zeros

        def step(k, carry):
            rf = pl.multiple_of(k * batch, batch)
            rb = pl.multiple_of((n_chunks - 1 - k) * batch, batch)
            new = []
            for gi in range(ni):
                xr, xi = carry[gi]
                sre = jnp.where(fwd_lane, s_ref[gi, pl.ds(rf, batch), 0:2 * half],
                                s_ref[gi, pl.ds(rb, batch), 0:2 * half])
                sim = jnp.where(fwd_lane, s_ref[gi, pl.ds(rf, batch), 2 * half:4 * half],
                                s_ref[gi, pl.ds(rb, batch), 2 * half:4 * half])
                ar = aq_ref[g0 + gi, 0:1, :]
                ai = aq_ref[g0 + gi, 1:2, :]
                nr = ar * xr - ai * xi + sre
                nim = ar * xi + ai * xr + sim
                x_ref[gi, pl.ds(rf + batch, batch), 0:half] = nr[:, 0:half]
                x_ref[gi, pl.ds(rf + batch, batch), 2 * half:3 * half] = nim[:, 0:half]
                x_ref[gi, pl.ds(rb - batch, batch), half:2 * half] = nr[:, half:2 * half]
                x_ref[gi, pl.ds(rb - batch, batch), 3 * half:4 * half] = nim[:, half:2 * half]
                new.append((nr, nim))
            return tuple(new)

        init = tuple((jnp.zeros((batch, 2 * half), F32), jnp.zeros((batch, 2 * half), F32))
                     for _ in range(ni))
        lax.fori_loop(0, n_chunks - 1, step, init)

        for gi in range(ni):
            g = g0 + gi
            y = jnp.dot(ug_ref[g], toep_ref[g], preferred_element_type=F32)
            y += lax.dot_general(x_ref[gi].astype(BF16), eout_ref[g], NT_DIMS,
                                 preferred_element_type=F32)
            ug_ref[g] = y.astype(BF16)

    def scatter(cp, carry):
        rows = pl.ds(pl.multiple_of(cp * slab, slab), slab)
        for part in range(q // gpb):
            t0 = cp * 2 * q + part * gpb
            v = [ug_ref[g, rows, part * LANES:(part + 1) * LANES] for g in range(gpb)]
            w = _block_transpose(v, lane_block)
            for j in range(gpb):
                wj = w[j].astype(F32)
                buf[pl.ds(t0 + j, batch, stride=pitch), :] = wj[0:batch]
                buf[pl.ds(t0 + q + j, batch, stride=pitch), :] = wj[batch:slab]
        return carry

    lax.fori_loop(0, n_chunks // 2, scatter, 0, unroll=SSM_RELAYOUT_UNROLL)

    start(column_copies(step_id, slot, False))

    @pl.when(step_id == last_step)
    def _():
        @pl.when(step_id >= 1)
        def _():
            wait(column_copies(step_id - 1, 1 - slot, False))
        wait(column_copies(step_id, slot, False))


def _padded_seq(seq):
    return seq + SUBLANES if seq % (2 * SUBLANES) == 0 else seq


def _ssm(u, lam_re, lam_im, log_dt, b_re, b_im, c_re, c_im, d_skip):
    batch, seq, _ = u.shape
    pitch = _padded_seq(seq)
    rows = batch * seq // SSM_CHUNK
    ein, toep, eout, a_q = _ssm_operators(lam_re, lam_im, log_dt, b_re, b_im, c_re, c_im, d_skip)
    gpb = GROUPS_PER_BLOCK
    mat = pl.BlockSpec((gpb, CHUNK_WIDTH, CHUNK_WIDTH), lambda s: (s, 0, 0))
    hbm = pl.BlockSpec(memory_space=pl.ANY)
    return pl.pallas_call(
        functools.partial(_ssm_kernel, batch=batch, seq=seq, pitch=pitch),
        grid=(SSM_GROUPS // gpb,),
        in_specs=[hbm, mat, mat, mat,
                  pl.BlockSpec((gpb, SUBLANES, 2 * SSM_STATE), lambda s: (s, 0, 0))],
        out_specs=hbm,
        out_shape=jax.ShapeDtypeStruct((batch, seq, SSM_WIDTH), F32),
        scratch_shapes=[pltpu.VMEM((2, batch * pitch, LANES), F32),
                        pltpu.VMEM((gpb, rows, CHUNK_WIDTH), BF16),
                        pltpu.VMEM((SSM_INTERLEAVE, rows, 4 * SSM_STATE), F32),
                        pltpu.VMEM((SSM_INTERLEAVE, rows, 4 * SSM_STATE), F32),
                        pltpu.SemaphoreType.DMA((2, batch)),
                        pltpu.SemaphoreType.DMA((2, batch))],
        compiler_params=pltpu.CompilerParams(
            dimension_semantics=("arbitrary",), vmem_limit_bytes=VMEM_LIMIT_BYTES),
    )(u, ein, toep, eout, a_q)


def kernel(x, norm_ffn1, ffn1_w_gate, ffn1_w_up, ffn1_w_down, norm_mix, w_in, attn_sinks,
           ssm_lambda_re, ssm_lambda_im, ssm_log_dt, ssm_b_re, ssm_b_im, ssm_c_re, ssm_c_im,
           ssm_d, ssm_glu_w, ssm_glu_b, attn_out_norm, ssm_out_norm, w_out,
           norm_ffn2, ffn2_w_gate, ffn2_w_up, ffn2_w_down, final_norm):
    b, seq, d = x.shape
    depth = norm_ffn1.shape[0]
    assert d == D_MODEL and seq % BLOCK == 0 and (b * seq) % FFN_TOKEN_TILE == 0
    assert b == SUBLANES and seq % (2 * SSM_CHUNK * SSM_RELAYOUT_UNROLL) == 0
    h = x.reshape(b * seq, d).astype(F32)
    for l in range(depth):
        h, q, k, v, u = _ffn(h, norm_ffn1[l], ffn1_w_gate[l], ffn1_w_up[l], ffn1_w_down[l],
                             final_norm, final_norm=False, proj=(norm_mix[l], w_in[l]))
        attn = _attention(q.reshape(b, seq, -1), k.reshape(b, seq, -1), v.reshape(b, seq, -1),
                          attn_sinks[l])
        ssm_pre = _ssm(u.reshape(b, seq, -1), ssm_lambda_re[l], ssm_lambda_im[l], ssm_log_dt[l],
                       ssm_b_re[l], ssm_b_im[l], ssm_c_re[l], ssm_c_im[l], ssm_d[l])
        mix = (attn.reshape(b * seq, -1), ssm_pre.reshape(b * seq, -1), ssm_glu_w[l],
               ssm_glu_b[l], attn_out_norm[l], ssm_out_norm[l], w_out[l])
        h, = _ffn(h, norm_ffn2[l], ffn2_w_gate[l], ffn2_w_up[l], ffn2_w_down[l],
                  final_norm, final_norm=(l == depth - 1), mix=mix)
    return h.reshape(b, seq, d).astype(x.dtype)
```

```python
import functools
import math

import jax
import jax.numpy as jnp
from jax import lax
from jax.experimental import pallas as pl
from jax.experimental.pallas import tpu as pltpu

F32 = jnp.float32
BF16 = jnp.bfloat16

D_MODEL = 1024
ATTN_HEADS = 8
ATTN_KV_HEADS = 2
Q_PER_KV = ATTN_HEADS // ATTN_KV_HEADS
HEAD_DIM = 64
ATTN_WIDTH = ATTN_HEADS * HEAD_DIM
KV_WIDTH = ATTN_KV_HEADS * HEAD_DIM
WINDOW = 128
BLOCK = 128
SSM_CH = 16
SSM_WIDTH = D_MODEL - ATTN_WIDTH
SSM_GROUPS = SSM_WIDTH // SSM_CH
SSM_STATE = 64
IN_WIDTH = ATTN_WIDTH + 2 * KV_WIDTH + SSM_WIDTH
D_FF = 2816
EPS = 1e-6
NEG_INF = -1e30
LAMBDA_RE_MAX = -1e-4
LOG2_E = math.log2(math.e)
QUERY_SCALE = HEAD_DIM ** -0.5 * LOG2_E

LANES = 128
SUBLANES = 8
VMEM_LIMIT_BYTES = 56 * 1024 * 1024

FFN_TOKEN_TILE = 512
FFN_SUBTILES = 2
FF_TILE = 256
ATTN_BLOCKS_PER_STEP = 8
SSM_CHUNK = 16
CHUNK_WIDTH = SSM_CHUNK * SSM_CH
GROUPS_PER_BLOCK = LANES // SSM_CH
SSM_INTERLEAVE = 8
SSM_RELAYOUT_UNROLL = 4
OPS_GROUPS_PER_STEP = 8

NT_DIMS = (((1,), (1,)), ((), ()))
TN_DIMS = (((0,), (0,)), ((), ()))


def _rms(x):
    return x * lax.rsqrt(jnp.mean(x * x, axis=-1, keepdims=True) + EPS)


def _mixed_update(rows, attn_ref, ssm_ref, gw_ref, gb_ref, ga_ref, gs_ref, wo_ref):
    y = jax.nn.gelu(ssm_ref[rows, :])
    z = jnp.dot(y.astype(BF16), gw_ref[...].astype(BF16), preferred_element_type=F32) + gb_ref[...]
    s = y * jax.nn.sigmoid(z)
    sn = _rms(s) * gs_ref[...]
    an = _rms(attn_ref[rows, :].astype(F32)) * ga_ref[...]
    mixed = jnp.concatenate([an, sn], axis=-1).astype(BF16)
    return jnp.dot(mixed, wo_ref[...].astype(BF16), preferred_element_type=F32)


def _mixer_inputs(y, rows, gain_ref, w_ref, q_ref, k_ref, v_ref, u_ref):
    hn = (_rms(y) * gain_ref[...]).astype(BF16)
    proj = jnp.dot(hn, w_ref[...].astype(BF16), preferred_element_type=F32)
    q_ref[rows, :] = (proj[:, :ATTN_WIDTH] * QUERY_SCALE).astype(BF16)
    k_ref[rows, :] = proj[:, ATTN_WIDTH:ATTN_WIDTH + KV_WIDTH].astype(BF16)
    v_ref[rows, :] = proj[:, ATTN_WIDTH + KV_WIDTH:ATTN_WIDTH + 2 * KV_WIDTH].astype(BF16)
    u_ref[rows, :] = proj[:, ATTN_WIDTH + 2 * KV_WIDTH:]


def _ffn_body(read_x, gain_ref, fgain_ref, wg_hbm, wu_hbm, wd_hbm, o_ref,
              wg_ref, wu_ref, wd_ref, act_ref, gu_stage, d_stage, sem, *, final_norm,
              epilogue=None):
    nj = D_FF // FF_TILE

    def weight_copies(j, slot):
        span = pl.ds(j * FF_TILE, FF_TILE)
        return (pltpu.make_async_copy(wg_hbm.at[:, span], gu_stage.at[0, slot], sem.at[0, slot]),
                pltpu.make_async_copy(wu_hbm.at[:, span], gu_stage.at[1, slot], sem.at[1, slot]),
                pltpu.make_async_copy(wd_hbm.at[span, :], d_stage.at[slot], sem.at[2, slot]))

    def step(stage_weights):
        if stage_weights:
            for copy in weight_copies(0, 0):
                copy.start()
        sub = o_ref.shape[0] // FFN_SUBTILES
        row_groups = [slice(h * sub, (h + 1) * sub) for h in range(FFN_SUBTILES)]
        xs = [read_x(rows) for rows in row_groups]
        hns = [(_rms(x) * gain_ref[...]).astype(BF16) for x in xs]
        for j in range(nj):
            cols = slice(j * FF_TILE, (j + 1) * FF_TILE)
            if stage_weights:
                slot = j % 2
                if j + 1 < nj:
                    for copy in weight_copies(j + 1, 1 - slot):
                        copy.start()
                for copy in weight_copies(j, slot):
                    copy.wait()
                wg_ref[:, cols] = gu_stage[0, slot].astype(BF16)
                wu_ref[:, cols] = gu_stage[1, slot].astype(BF16)
                wd_ref[cols, :] = d_stage[slot].astype(BF16)
            for rows, hn in zip(row_groups, hns):
                g = jnp.dot(hn, wg_ref[:, cols], preferred_element_type=F32)
                u = jnp.dot(hn, wu_ref[:, cols], preferred_element_type=F32)
                act_ref[rows, cols] = (g * jax.nn.sigmoid(g) * u).astype(BF16)
        for rows, x in zip(row_groups, xs):
            y = x + 0.5 * jnp.dot(act_ref[rows, :], wd_ref[...], preferred_element_type=F32)
            if final_norm:
                y = _rms(y) * fgain_ref[...]
            o_ref[rows, :] = y
            if epilogue is not None:
                epilogue(y, rows)

    pl.when(pl.program_id(0) == 0)(functools.partial(step, True))
    pl.when(pl.program_id(0) != 0)(functools.partial(step, False))


def _ffn_proj_kernel(x_ref, mgain_ref, win_ref, gain_ref, fgain_ref, wg_hbm, wu_hbm, wd_hbm,
                     o_ref, q_ref, k_ref, v_ref, u_ref, *scratch, final_norm):
    emit = lambda y, rows: _mixer_inputs(y, rows, mgain_ref, win_ref, q_ref, k_ref, v_ref, u_ref)
    _ffn_body(lambda rows: x_ref[rows, :], gain_ref, fgain_ref, wg_hbm, wu_hbm, wd_hbm, o_ref,
              *scratch, final_norm=final_norm, epilogue=emit)


def _mix_ffn_kernel(x_ref, attn_ref, ssm_ref, gw_ref, gb_ref, ga_ref, gs_ref, wo_ref, *ffn_refs,
                    final_norm):
    read_x = lambda rows: x_ref[rows, :] + _mixed_update(rows, attn_ref, ssm_ref, gw_ref, gb_ref,
                                                         ga_ref, gs_ref, wo_ref)
    _ffn_body(read_x, *ffn_refs, final_norm=final_norm)


def _ffn(x, gain, w_gate, w_up, w_down, final_gain, final_norm, mix=None, proj=None):
    assert (mix is None) != (proj is None)
    t = x.shape[0]
    tile = FFN_TOKEN_TILE
    vec = lambda width: pl.BlockSpec((1, width), lambda i: (0, 0))
    resident = lambda r, c: pl.BlockSpec((r, c), lambda i: (0, 0), pipeline_mode=pl.Buffered(1))
    row = lambda width: pl.BlockSpec((tile, width), lambda i: (i, 0))
    hbm = pl.BlockSpec(memory_space=pl.ANY)
    in_specs, args = [row(D_MODEL)], [x]
    out_specs, out_shape = [row(D_MODEL)], [jax.ShapeDtypeStruct((t, D_MODEL), F32)]
    if mix is not None:
        attn, ssm_pre, glu_w, glu_b, attn_gain, ssm_gain, w_out = mix
        body = _mix_ffn_kernel
        in_specs += [row(ATTN_WIDTH), row(SSM_WIDTH),
                     resident(SSM_WIDTH, SSM_WIDTH), vec(SSM_WIDTH), vec(ATTN_WIDTH),
                     vec(SSM_WIDTH), resident(D_MODEL, D_MODEL)]
        args += [attn, ssm_pre, glu_w, glu_b.reshape(1, -1).astype(F32),
                 attn_gain.reshape(1, -1).astype(F32), ssm_gain.reshape(1, -1).astype(F32), w_out]
    if proj is not None:
        mixer_gain, w_in = proj
        body = _ffn_proj_kernel
        in_specs += [vec(D_MODEL), resident(D_MODEL, IN_WIDTH)]
        args += [mixer_gain.reshape(1, D_MODEL), w_in]
        for width, dtype in ((ATTN_WIDTH, BF16), (KV_WIDTH, BF16), (KV_WIDTH, BF16),
                             (SSM_WIDTH, F32)):
            out_specs.append(row(width))
            out_shape.append(jax.ShapeDtypeStruct((t, width), dtype))
    in_specs += [vec(D_MODEL), vec(D_MODEL), hbm, hbm, hbm]
    args += [gain.reshape(1, D_MODEL), final_gain.reshape(1, D_MODEL), w_gate, w_up, w_down]
    return pl.pallas_call(
        functools.partial(body, final_norm=final_norm),
        grid=(t // tile,),
        in_specs=in_specs,
        out_specs=out_specs,
        out_shape=out_shape,
        scratch_shapes=[pltpu.VMEM((D_MODEL, D_FF), BF16),
                        pltpu.VMEM((D_MODEL, D_FF), BF16),
                        pltpu.VMEM((D_FF, D_MODEL), BF16),
                        pltpu.VMEM((tile, D_FF), BF16),
                        pltpu.VMEM((2, 2, D_MODEL, FF_TILE), F32),
                        pltpu.VMEM((2, FF_TILE, D_MODEL), F32),
                        pltpu.SemaphoreType.DMA((3, 2))],
        compiler_params=pltpu.CompilerParams(
            dimension_semantics=("arbitrary",), vmem_limit_bytes=VMEM_LIMIT_BYTES),
    )(*args)


def _attn_kernel(sink_ref, q_ref, kp_ref, kc_ref, kn_ref, vp_ref, vc_ref, vn_ref, o_ref,
                 bias_ref, k_ref, v_ref, s_ref):
    n = pl.program_id(1)
    last = pl.num_programs(1) - 1
    nblk = ATTN_BLOCKS_PER_STEP

    @pl.when(n == 0)
    def _():
        kj = lax.broadcasted_iota(jnp.int32, (3 * BLOCK, BLOCK), 0)
        qi = lax.broadcasted_iota(jnp.int32, (3 * BLOCK, BLOCK), 1)
        rel = jnp.abs(kj - BLOCK - qi)
        dist = rel.astype(F32)
        inside = rel <= WINDOW
        has_prev = kj >= BLOCK
        has_next = kj < 2 * BLOCK
        for variant, ok in enumerate((inside & has_prev, inside, inside & has_next)):
            for h in range(ATTN_HEADS):
                slope = float(2.0 ** (-8.0 * (h + 1) / ATTN_HEADS))
                bias_ref[variant, h] = jnp.where(ok, (-slope * LOG2_E) * dist, NEG_INF)
        for kh in range(ATTN_KV_HEADS):
            v_ref[:, (2 * kh + 1) * HEAD_DIM:(2 * kh + 2) * HEAD_DIM] = jnp.ones(
                (v_ref.shape[0], HEAD_DIM), BF16)

    spans = ((slice(0, BLOCK), kp_ref, vp_ref),
             (slice(BLOCK, (nblk + 1) * BLOCK), kc_ref, vc_ref),
             (slice((nblk + 1) * BLOCK, (nblk + 2) * BLOCK), kn_ref, vn_ref))
    for rows, k_in, v_in in spans:
        k_ref[rows, :] = k_in[...]
        for kh in range(ATTN_KV_HEADS):
            v_ref[rows, 2 * kh * HEAD_DIM:(2 * kh + 1) * HEAD_DIM] = (
                v_in[:, kh * HEAD_DIM:(kh + 1) * HEAD_DIM])

    def scores(j, kh):
        kcat = k_ref[j * BLOCK:(j + 3) * BLOCK, kh * HEAD_DIM:(kh + 1) * HEAD_DIM]
        heads = [kh * Q_PER_KV + g for g in range(Q_PER_KV)]
        qs = jnp.concatenate(
            [q_ref[j * BLOCK:(j + 1) * BLOCK, h * HEAD_DIM:(h + 1) * HEAD_DIM] for h in heads],
            axis=0)
        s_ref[j, kh] = lax.dot_general(kcat, qs, NT_DIMS,
                                       preferred_element_type=F32)

    for kh in range(ATTN_KV_HEADS):
        scores(0, kh)
    for j in range(nblk):
        variant = 1
        if j == 0:
            variant = jnp.where(n == 0, 0, variant)
        if j == nblk - 1:
            variant = jnp.where(n == last, 2, variant)
        outs = []
        for kh in range(ATTN_KV_HEADS):
            if j + 1 < nblk:
                scores(j + 1, kh)
            v_ones = v_ref[j * BLOCK:(j + 3) * BLOCK, 2 * kh * HEAD_DIM:(2 * kh + 2) * HEAD_DIM]
            for g in range(Q_PER_KV):
                h = kh * Q_PER_KV + g
                s = s_ref[j, kh, :, g * BLOCK:(g + 1) * BLOCK] + bias_ref[variant, h]
                sink = sink_ref[h] * LOG2_E
                m = jnp.maximum(jnp.max(s, axis=0, keepdims=True), sink)
                e = jnp.exp2(s - m).astype(BF16)
                pv = lax.dot_general(v_ones, e, TN_DIMS, preferred_element_type=F32)
                den = pv[HEAD_DIM:HEAD_DIM + 1] + jnp.exp2(sink - m)
                outs.append(pv[:HEAD_DIM] / den)
        o_ref[j * BLOCK:(j + 1) * BLOCK, :] = jnp.concatenate(outs, axis=0).T.astype(o_ref.dtype)


def _attention(q, k, v, sinks):
    b, seq, _ = q.shape
    nblk = ATTN_BLOCKS_PER_STEP
    steps = seq // (nblk * BLOCK)
    nb = seq // BLOCK
    assert seq % (nblk * BLOCK) == 0 and nb >= 2
    edge = lambda f: pl.BlockSpec((None, BLOCK, KV_WIDTH), f)
    body = pl.BlockSpec((None, nblk * BLOCK, KV_WIDTH), lambda bi, n: (bi, n, 0))
    prev = lambda bi, n: (bi, jnp.maximum(n * nblk - 1, 0), 0)
    nxt = lambda bi, n: (bi, jnp.minimum((n + 1) * nblk, nb - 1), 0)
    rows = pl.BlockSpec((None, nblk * BLOCK, ATTN_WIDTH), lambda bi, n: (bi, n, 0))
    return pl.pallas_call(
        _attn_kernel,
        grid=(b, steps),
        in_specs=[pl.BlockSpec(memory_space=pltpu.SMEM), rows,
                  edge(prev), body, edge(nxt), edge(prev), body, edge(nxt)],
        out_specs=rows,
        out_shape=jax.ShapeDtypeStruct((b, seq, ATTN_WIDTH), BF16),
        scratch_shapes=[pltpu.VMEM((3, ATTN_HEADS, 3 * BLOCK, BLOCK), F32),
                        pltpu.VMEM(((nblk + 2) * BLOCK, KV_WIDTH), BF16),
                        pltpu.VMEM(((nblk + 2) * BLOCK, 2 * KV_WIDTH), BF16),
                        pltpu.VMEM((nblk, ATTN_KV_HEADS, 3 * BLOCK, Q_PER_KV * BLOCK), F32)],
        compiler_params=pltpu.CompilerParams(
            dimension_semantics=("parallel", "arbitrary"), vmem_limit_bytes=VMEM_LIMIT_BYTES),
    )(sinks.astype(F32), q, k, k, k, v, v, v)


def _dot_nt_split(a, b):
    a_hi, b_hi = a.astype(BF16), b.astype(BF16)
    a_lo = (a - a_hi.astype(F32)).astype(BF16)
    b_lo = (b - b_hi.astype(F32)).astype(BF16)
    dot = functools.partial(lax.dot_general, dimension_numbers=NT_DIMS,
                            preferred_element_type=F32)
    return dot(a_hi, b_hi) + dot(a_hi, b_lo) + dot(a_lo, b_hi)


def _ssm_ops_kernel(prm_ref, bc_ref, ein_ref, toep_ref, eout_ref, aq_ref):
    q, hc, p = SSM_CHUNK, SSM_CH, SSM_STATE
    fwd = lax.broadcasted_iota(jnp.int32, (1, 2 * p), 1) < p
    zero_row = jnp.zeros((1, 2 * p), F32)
    row_id = lax.broadcasted_iota(jnp.int32, (CHUNK_WIDTH, CHUNK_WIDTH), 0)
    col_id = lax.broadcasted_iota(jnp.int32, (CHUNK_WIDTH, CHUNK_WIDTH), 1)

    def table(select, n):
        picks = [select(m) for m in range(n)]
        re = jnp.concatenate([jnp.broadcast_to(r, (hc, 2 * p)) for r, _ in picks], axis=0)
        im = jnp.concatenate([jnp.broadcast_to(i, (hc, 2 * p)) for _, i in picks], axis=0)
        return re, im

    def tile_rows(x, n):
        return jnp.concatenate([x] * n, axis=0)

    for gi in range(prm_ref.shape[0]):
        lr = jnp.minimum(prm_ref[gi, 0:1, :], LAMBDA_RE_MAX)
        li = prm_ref[gi, 1:2, :]
        dt = jnp.exp(prm_ref[gi, 2:3, :])
        mag = jnp.exp(lr * dt)
        a_r = mag * jnp.cos(li * dt)
        a_i = mag * jnp.sin(li * dt)
        den = lr * lr + li * li
        coef_r = ((a_r - 1.0) * lr + a_i * li) / den
        coef_i = (a_i * lr - (a_r - 1.0) * li) / den
        b_r, b_i = bc_ref[gi, 0], bc_ref[gi, 1]
        c_r, c_i = bc_ref[gi, 2], bc_ref[gi, 3]
        bb_r = coef_r * b_r - coef_i * b_i
        bb_i = coef_r * b_i + coef_i * b_r

        pw = [(jnp.ones((1, 2 * p), F32), zero_row)]
        for _ in range(q):
            r, i = pw[-1]
            pw.append((r * a_r - i * a_i, r * a_i + i * a_r))

        def both(f_idx, b_idx):
            fr, fi = pw[f_idx] if f_idx is not None else (zero_row, zero_row)
            br, bi = pw[b_idx] if b_idx is not None else (zero_row, zero_row)
            return jnp.where(fwd, fr, br), jnp.where(fwd, fi, bi)

        p_r, p_i = table(lambda i: both(q - 1 - i, i), q)
        tb_r, tb_i = tile_rows(bb_r, q), tile_rows(bb_i, q)
        ein = jnp.concatenate([tb_r * p_r - tb_i * p_i, tb_r * p_i + tb_i * p_r], axis=1)
        ein_ref[gi] = ein.astype(BF16)

        p_r, p_i = table(lambda j: both(j + 1, q - j), q)
        tc_r, tc_i = tile_rows(c_r, q), tile_rows(c_i, q)
        eout = jnp.concatenate([tc_r * p_r - tc_i * p_i, -(tc_r * p_i + tc_i * p_r)], axis=1)
        eout_ref[gi] = eout.astype(BF16)

        def lag(m):
            return both(m - (q - 1) if q - 1 <= m <= 2 * q - 2 else None,
                        (q - 1) - m if m <= q - 1 else None)

        p_r, p_i = table(lag, 2 * q)
        tc_r, tc_i = tile_rows(c_r, 2 * q), tile_rows(c_i, 2 * q)
        cpt = jnp.concatenate([tc_r * p_r - tc_i * p_i, tc_r * p_i + tc_i * p_r], axis=1)
        bcat = jnp.concatenate([bb_r, -bb_i], axis=1)
        kern = _dot_nt_split(bcat, cpt)
        toep = jnp.concatenate(
            [kern[:, hc * (q - 1 - i):hc * (q - 1 - i) + CHUNK_WIDTH] for i in range(q)], axis=0)
        skip = jnp.concatenate([prm_ref[gi, 3:4, :]] * (CHUNK_WIDTH // (2 * p)), axis=1)
        toep_ref[gi] = (toep + jnp.where(row_id == col_id, skip, 0.0)).astype(BF16)

        aq_ref[gi] = jnp.concatenate(
            [pw[q][0], pw[q][1], jnp.zeros((SUBLANES - 2, 2 * p), F32)], axis=0)


def _ssm_operators(lam_re, lam_im, log_dt, b_re, b_im, c_re, c_im, d_skip):
    g, p, hc = SSM_GROUPS, SSM_STATE, SSM_CH
    lanes = lambda a: a.astype(F32).transpose(1, 0, 2).reshape(g, 2 * p)
    prm = jnp.stack(
        [lanes(lam_re), lanes(lam_im),
         jnp.repeat(log_dt.astype(F32).T, p, axis=1),
         jnp.tile(d_skip.astype(F32), (1, 2 * p // hc))]
        + [jnp.zeros((g, 2 * p), F32)] * (SUBLANES - 4), axis=1)
    bc = jnp.stack(
        [b_re.astype(F32).transpose(1, 3, 0, 2).reshape(g, hc, 2 * p),
         b_im.astype(F32).transpose(1, 3, 0, 2).reshape(g, hc, 2 * p),
         c_re.astype(F32).transpose(1, 2, 0, 3).reshape(g, hc, 2 * p),
         c_im.astype(F32).transpose(1, 2, 0, 3).reshape(g, hc, 2 * p)], axis=1)
    gs = OPS_GROUPS_PER_STEP
    mat = pl.BlockSpec((gs, CHUNK_WIDTH, CHUNK_WIDTH), lambda s: (s, 0, 0))
    mat_shape = jax.ShapeDtypeStruct((g, CHUNK_WIDTH, CHUNK_WIDTH), BF16)
    return pl.pallas_call(
        _ssm_ops_kernel,
        grid=(g // gs,),
        in_specs=[pl.BlockSpec((gs, SUBLANES, 2 * p), lambda s: (s, 0, 0)),
                  pl.BlockSpec((gs, 4, hc, 2 * p), lambda s: (s, 0, 0, 0))],
        out_specs=[mat, mat, mat, pl.BlockSpec((gs, SUBLANES, 2 * p), lambda s: (s, 0, 0))],
        out_shape=[mat_shape, mat_shape, mat_shape,
                   jax.ShapeDtypeStruct((g, SUBLANES, 2 * p), F32)],
        compiler_params=pltpu.CompilerParams(
            dimension_semantics=("parallel",), vmem_limit_bytes=VMEM_LIMIT_BYTES),
    )(prm, bc)


def _lane_roll(x, shift):
    shift %= LANES
    return jnp.concatenate([x[:, LANES - shift:], x[:, :LANES - shift]], axis=1)


def _block_transpose(v, lane_block):
    n = len(v)
    skewed = [v[i] if i == 0 else _lane_roll(v[i], SSM_CH * i) for i in range(n)]
    out = []
    for g in range(n):
        picked = skewed[(0 - g) % n]
        for j in range(1, n):
            picked = jnp.where(lane_block == j, skewed[(j - g) % n], picked)
        out.append(picked if g == 0 else _lane_roll(picked, -SSM_CH * g))
    return out


def _ssm_kernel(u_hbm, ein_ref, toep_ref, eout_ref, aq_ref, y_hbm,
                io_ref, ug_ref, s_ref, x_ref, in_sem, out_sem, *, batch, seq, pitch):
    q, half = SSM_CHUNK, SSM_STATE
    n_chunks = seq // q
    gpb = GROUPS_PER_BLOCK
    slab = 2 * batch
    lane_block = lax.broadcasted_iota(jnp.int32, (slab, LANES), 1) // SSM_CH
    step_id = pl.program_id(0)
    last_step = pl.num_programs(0) - 1
    slot = lax.rem(step_id, 2)

    def column_copies(block, slot_, to_vmem):
        lanes = pl.ds(pl.multiple_of(block * LANES, LANES), LANES)
        copies = []
        for b in range(batch):
            hbm = (u_hbm if to_vmem else y_hbm).at[b, :, lanes]
            vmem = io_ref.at[slot_, pl.ds(b * pitch, seq), :]
            if to_vmem:
                copies.append(pltpu.make_async_copy(hbm, vmem, in_sem.at[slot_, b]))
            else:
                copies.append(pltpu.make_async_copy(vmem, hbm, out_sem.at[slot_, b]))
        return copies

    def start(copies):
        for copy in copies:
            copy.start()

    def wait(copies):
        for copy in copies:
            copy.wait()

    @pl.when(step_id == 0)
    def _():
        start(column_copies(0, 0, True))

    @pl.when(step_id < last_step)
    def _():
        @pl.when(step_id >= 1)
        def _():
            wait(column_copies(step_id - 1, 1 - slot, False))
        start(column_copies(step_id + 1, 1 - slot, True))

    wait(column_copies(step_id, slot, True))
    buf = io_ref.at[slot]

    def gather(cp, carry):
        rows = pl.ds(pl.multiple_of(cp * slab, slab), slab)
        for part in range(q // gpb):
            t0 = cp * 2 * q + part * gpb
            v = [jnp.concatenate([buf[pl.ds(t0 + i, batch, stride=pitch), :],
                                  buf[pl.ds(t0 + q + i, batch, stride=pitch), :]],
                                 axis=0).astype(BF16) for i in range(gpb)]
            w = _block_transpose(v, lane_block)
            for g in range(gpb):
                ug_ref[g, rows, part * LANES:(part + 1) * LANES] = w[g]
        return carry

    lax.fori_loop(0, n_chunks // 2, gather, 0, unroll=SSM_RELAYOUT_UNROLL)

    fwd_lane = lax.broadcasted_iota(jnp.int32, (batch, 2 * half), 1) < half
    zeros = jnp.zeros((batch, half), F32)
    last_rows = pl.ds((n_chunks - 1) * batch, batch)
    ni = SSM_INTERLEAVE
    for g0 in range(0, gpb, ni):
        for gi in range(ni):
            s_ref[gi] = jnp.dot(ug_ref[g0 + gi], ein_ref[g0 + gi], preferred_element_type=F32)
            x_ref[gi, 0:batch, 0:half] = zeros
            x_ref[gi, 0:batch, 2 * half:3 * half] = zeros
            x_ref[gi, last_rows, half:2 * half] = zeros
            x_ref[gi, last_rows, 3 * half:4 * half] = zeros

        def step(k, carry):
            rf = pl.multiple_of(k * batch, batch)
            rb = pl.multiple_of((n_chunks - 1 - k) * batch, batch)
            new = []
            for gi in range(ni):
                xr, xi = carry[gi]
                sre = jnp.where(fwd_lane, s_ref[gi, pl.ds(rf, batch), 0:2 * half],
                                s_ref[gi, pl.ds(rb, batch), 0:2 * half])
                sim = jnp.where(fwd_lane, s_ref[gi, pl.ds(rf, batch), 2 * half:4 * half],
                                s_ref[gi, pl.ds(rb, batch), 2 * half:4 * half])
                ar = aq_ref[g0 + gi, 0:1, :]
                ai = aq_ref[g0 + gi, 1:2, :]
                nr = ar * xr - ai * xi + sre
                nim = ar * xi + ai * xr + sim
                x_ref[gi, pl.ds(rf + batch, batch), 0:half] = nr[:, 0:half]
                x_ref[gi, pl.ds(rf + batch, batch), 2 * half:3 * half] = nim[:, 0:half]
                x_ref[gi, pl.ds(rb - batch, batch), half:2 * half] = nr[:, half:2 * half]
                x_ref[gi, pl.ds(rb - batch, batch), 3 * half:4 * half] = nim[:, half:2 * half]
                new.append((nr, nim))
            return tuple(new)

        init = tuple((jnp.zeros((batch, 2 * half), F32), jnp.zeros((batch, 2 * half), F32))
                     for _ in range(ni))
        lax.fori_loop(0, n_chunks - 1, step, init)

        for gi in range(ni):
            g = g0 + gi
            y = jnp.dot(ug_ref[g], toep_ref[g], preferred_element_type=F32)
            y += lax.dot_general(x_ref[gi].astype(BF16), eout_ref[g], NT_DIMS,
                                 preferred_element_type=F32)
            ug_ref[g] = y.astype(BF16)

    def scatter(cp, carry):
        rows = pl.ds(pl.multiple_of(cp * slab, slab), slab)
        for part in range(q // gpb):
            t0 = cp * 2 * q + part * gpb
            v = [ug_ref[g, rows, part * LANES:(part + 1) * LANES] for g in range(gpb)]
            w = _block_transpose(v, lane_block)
            for j in range(gpb):
                wj = w[j].astype(F32)
                buf[pl.ds(t0 + j, batch, stride=pitch), :] = wj[0:batch]
                buf[pl.ds(t0 + q + j, batch, stride=pitch), :] = wj[batch:slab]
        return carry

    lax.fori_loop(0, n_chunks // 2, scatter, 0, unroll=SSM_RELAYOUT_UNROLL)

    start(column_copies(step_id, slot, False))

    @pl.when(step_id == last_step)
    def _():
        @pl.when(step_id >= 1)
        def _():
            wait(column_copies(step_id - 1, 1 - slot, False))
        wait(column_copies(step_id, slot, False))


def _padded_seq(seq):
    return seq + SUBLANES if seq % (2 * SUBLANES) == 0 else seq


def _ssm(u, lam_re, lam_im, log_dt, b_re, b_im, c_re, c_im, d_skip):
    batch, seq, _ = u.shape
    pitch = _padded_seq(seq)
    rows = batch * seq // SSM_CHUNK
    ein, toep, eout, a_q = _ssm_operators(lam_re, lam_im, log_dt, b_re, b_im, c_re, c_im, d_skip)
    gpb = GROUPS_PER_BLOCK
    mat = pl.BlockSpec((gpb, CHUNK_WIDTH, CHUNK_WIDTH), lambda s: (s, 0, 0))
    hbm = pl.BlockSpec(memory_space=pl.ANY)
    return pl.pallas_call(
        functools.partial(_ssm_kernel, batch=batch, seq=seq, pitch=pitch),
        grid=(SSM_GROUPS // gpb,),
        in_specs=[hbm, mat, mat, mat,
                  pl.BlockSpec((gpb, SUBLANES, 2 * SSM_STATE), lambda s: (s, 0, 0))],
        out_specs=hbm,
        out_shape=jax.ShapeDtypeStruct((batch, seq, SSM_WIDTH), F32),
        scratch_shapes=[pltpu.VMEM((2, batch * pitch, LANES), F32),
                        pltpu.VMEM((gpb, rows, CHUNK_WIDTH), BF16),
                        pltpu.VMEM((SSM_INTERLEAVE, rows, 4 * SSM_STATE), F32),
                        pltpu.VMEM((SSM_INTERLEAVE, rows, 4 * SSM_STATE), F32),
                        pltpu.SemaphoreType.DMA((2, batch)),
                        pltpu.SemaphoreType.DMA((2, batch))],
        compiler_params=pltpu.CompilerParams(
            dimension_semantics=("arbitrary",), vmem_limit_bytes=VMEM_LIMIT_BYTES),
    )(u, ein, toep, eout, a_q)


def kernel(x, norm_ffn1, ffn1_w_gate, ffn1_w_up, ffn1_w_down, norm_mix, w_in, attn_sinks,
           ssm_lambda_re, ssm_lambda_im, ssm_log_dt, ssm_b_re, ssm_b_im, ssm_c_re, ssm_c_im,
           ssm_d, ssm_glu_w, ssm_glu_b, attn_out_norm, ssm_out_norm, w_out,
           norm_ffn2, ffn2_w_gate, ffn2_w_up, ffn2_w_down, final_norm):
    b, seq, d = x.shape
    depth = norm_ffn1.shape[0]
    assert d == D_MODEL and seq % BLOCK == 0 and (b * seq) % FFN_TOKEN_TILE == 0
    assert b == SUBLANES and seq % (2 * SSM_CHUNK * SSM_RELAYOUT_UNROLL) == 0
    h = x.reshape(b * seq, d).astype(F32)
    for l in range(depth):
        h, q, k, v, u = _ffn(h, norm_ffn1[l], ffn1_w_gate[l], ffn1_w_up[l], ffn1_w_down[l],
                             final_norm, final_norm=False, proj=(norm_mix[l], w_in[l]))
        attn = _attention(q.reshape(b, seq, -1), k.reshape(b, seq, -1), v.reshape(b, seq, -1),
                          attn_sinks[l])
        ssm_pre = _ssm(u.reshape(b, seq, -1), ssm_lambda_re[l], ssm_lambda_im[l], ssm_log_dt[l],
                       ssm_b_re[l], ssm_b_im[l], ssm_c_re[l], ssm_c_im[l], ssm_d[l])
        mix = (attn.reshape(b * seq, -1), ssm_pre.reshape(b * seq, -1), ssm_glu_w[l],
               ssm_glu_b[l], attn_out_norm[l], ssm_out_norm[l], w_out[l])
        h, = _ffn(h, norm_ffn2[l], ffn2_w_gate[l], ffn2_w_up[l], ffn2_w_down[l],
                  final_norm, final_norm=(l == depth - 1), mix=mix)
    return h.reshape(b, seq, d).astype(x.dtype)
```

```python
import functools
import math

import jax
import jax.numpy as jnp
from jax import lax
from jax.experimental import pallas as pl
from jax.experimental.pallas import tpu as pltpu

F32 = jnp.float32
BF16 = jnp.bfloat16

D_MODEL = 1024
ATTN_HEADS = 8
ATTN_KV_HEADS = 2
Q_PER_KV = ATTN_HEADS // ATTN_KV_HEADS
HEAD_DIM = 64
ATTN_WIDTH = ATTN_HEADS * HEAD_DIM
KV_WIDTH = ATTN_KV_HEADS * HEAD_DIM
WINDOW = 128
BLOCK = 128
SSM_CH = 16
SSM_WIDTH = D_MODEL - ATTN_WIDTH
SSM_GROUPS = SSM_WIDTH // SSM_CH
SSM_STATE = 64
IN_WIDTH = ATTN_WIDTH + 2 * KV_WIDTH + SSM_WIDTH
D_FF = 2816
EPS = 1e-6
NEG_INF = -1e30
LAMBDA_RE_MAX = -1e-4
LOG2_E = math.log2(math.e)
QUERY_SCALE = HEAD_DIM ** -0.5 * LOG2_E

LANES = 128
SUBLANES = 8
VMEM_LIMIT_BYTES = 56 * 1024 * 1024

FFN_TOKEN_TILE = 512
FFN_SUBTILES = 2
FF_TILE = 256
ATTN_BLOCKS_PER_STEP = 8
SSM_CHUNK = 16
CHUNK_WIDTH = SSM_CHUNK * SSM_CH
GROUPS_PER_BLOCK = LANES // SSM_CH
SSM_INTERLEAVE = 8
SSM_RELAYOUT_UNROLL = 4
OPS_GROUPS_PER_STEP = 8

NT_DIMS = (((1,), (1,)), ((), ()))
TN_DIMS = (((0,), (0,)), ((), ()))


def _rms(x):
    return x * lax.rsqrt(jnp.mean(x * x, axis=-1, keepdims=True) + EPS)


def _mixed_update(rows, attn_ref, ssm_ref, gw_ref, gb_ref, ga_ref, gs_ref, wo_ref):
    y = jax.nn.gelu(ssm_ref[rows, :])
    z = jnp.dot(y.astype(BF16), gw_ref[...].astype(BF16), preferred_element_type=F32) + gb_ref[...]
    s = y * jax.nn.sigmoid(z)
    sn = _rms(s) * gs_ref[...]
    an = _rms(attn_ref[rows, :].astype(F32)) * ga_ref[...]
    mixed = jnp.concatenate([an, sn], axis=-1).astype(BF16)
    return jnp.dot(mixed, wo_ref[...].astype(BF16), preferred_element_type=F32)


def _mixer_inputs(y, rows, gain_ref, w_ref, q_ref, k_ref, v_ref, u_ref):
    hn = (_rms(y) * gain_ref[...]).astype(BF16)
    proj = jnp.dot(hn, w_ref[...].astype(BF16), preferred_element_type=F32)
    q_ref[rows, :] = (proj[:, :ATTN_WIDTH] * QUERY_SCALE).astype(BF16)
    k_ref[rows, :] = proj[:, ATTN_WIDTH:ATTN_WIDTH + KV_WIDTH].astype(BF16)
    v_ref[rows, :] = proj[:, ATTN_WIDTH + KV_WIDTH:ATTN_WIDTH + 2 * KV_WIDTH].astype(BF16)
    u_ref[rows, :] = proj[:, ATTN_WIDTH + 2 * KV_WIDTH:]


def _ffn_body(read_x, gain_ref, fgain_ref, wg_hbm, wu_hbm, wd_hbm, o_ref,
              wg_ref, wu_ref, wd_ref, act_ref, gu_stage, d_stage, sem, *, final_norm,
              epilogue=None):
    nj = D_FF // FF_TILE

    def weight_copies(j, slot):
        span = pl.ds(j * FF_TILE, FF_TILE)
        return (pltpu.make_async_copy(wg_hbm.at[:, span], gu_stage.at[0, slot], sem.at[0, slot]),
                pltpu.make_async_copy(wu_hbm.at[:, span], gu_stage.at[1, slot], sem.at[1, slot]),
                pltpu.make_async_copy(wd_hbm.at[span, :], d_stage.at[slot], sem.at[2, slot]))

    def step(stage_weights):
        if stage_weights:
            for copy in weight_copies(0, 0):
                copy.start()
        sub = o_ref.shape[0] // FFN_SUBTILES
        row_groups = [slice(h * sub, (h + 1) * sub) for h in range(FFN_SUBTILES)]
        xs = [read_x(rows) for rows in row_groups]
        hns = [(_rms(x) * gain_ref[...]).astype(BF16) for x in xs]
        for j in range(nj):
            cols = slice(j * FF_TILE, (j + 1) * FF_TILE)
            if stage_weights:
                slot = j % 2
                if j + 1 < nj:
                    for copy in weight_copies(j + 1, 1 - slot):
                        copy.start()
                for copy in weight_copies(j, slot):
                    copy.wait()
                wg_ref[:, cols] = gu_stage[0, slot].astype(BF16)
                wu_ref[:, cols] = gu_stage[1, slot].astype(BF16)
                wd_ref[cols, :] = d_stage[slot].astype(BF16)
            for rows, hn in zip(row_groups, hns):
                g = jnp.dot(hn, wg_ref[:, cols], preferred_element_type=F32)
                u = jnp.dot(hn, wu_ref[:, cols], preferred_element_type=F32)
                act_ref[rows, cols] = (g * jax.nn.sigmoid(g) * u).astype(BF16)
        for rows, x in zip(row_groups, xs):
            y = x + 0.5 * jnp.dot(act_ref[rows, :], wd_ref[...], preferred_element_type=F32)
            if final_norm:
                y = _rms(y) * fgain_ref[...]
            o_ref[rows, :] = y
            if epilogue is not None:
                epilogue(y, rows)

    pl.when(pl.program_id(0) == 0)(functools.partial(step, True))
    pl.when(pl.program_id(0) != 0)(functools.partial(step, False))


def _ffn_proj_kernel(x_ref, mgain_ref, win_ref, gain_ref, fgain_ref, wg_hbm, wu_hbm, wd_hbm,
                     o_ref, q_ref, k_ref, v_ref, u_ref, *scratch, final_norm):
    emit = lambda y, rows: _mixer_inputs(y, rows, mgain_ref, win_ref, q_ref, k_ref, v_ref, u_ref)
    _ffn_body(lambda rows: x_ref[rows, :], gain_ref, fgain_ref, wg_hbm, wu_hbm, wd_hbm, o_ref,
              *scratch, final_norm=final_norm, epilogue=emit)


def _mix_ffn_kernel(x_ref, attn_ref, ssm_ref, gw_ref, gb_ref, ga_ref, gs_ref, wo_ref, *ffn_refs,
                    final_norm):
    read_x = lambda rows: x_ref[rows, :] + _mixed_update(rows, attn_ref, ssm_ref, gw_ref, gb_ref,
                                                         ga_ref, gs_ref, wo_ref)
    _ffn_body(read_x, *ffn_refs, final_norm=final_norm)


def _ffn(x, gain, w_gate, w_up, w_down, final_gain, final_norm, mix=None, proj=None):
    assert (mix is None) != (proj is None)
    t = x.shape[0]
    tile = FFN_TOKEN_TILE
    vec = lambda width: pl.BlockSpec((1, width), lambda i: (0, 0))
    resident = lambda r, c: pl.BlockSpec((r, c), lambda i: (0, 0), pipeline_mode=pl.Buffered(1))
    row = lambda width: pl.BlockSpec((tile, width), lambda i: (i, 0))
    hbm = pl.BlockSpec(memory_space=pl.ANY)
    in_specs, args = [row(D_MODEL)], [x]
    out_specs, out_shape = [row(D_MODEL)], [jax.ShapeDtypeStruct((t, D_MODEL), F32)]
    if mix is not None:
        attn, ssm_pre, glu_w, glu_b, attn_gain, ssm_gain, w_out = mix
        body = _mix_ffn_kernel
        in_specs += [row(ATTN_WIDTH), row(SSM_WIDTH),
                     resident(SSM_WIDTH, SSM_WIDTH), vec(SSM_WIDTH), vec(ATTN_WIDTH),
                     vec(SSM_WIDTH), resident(D_MODEL, D_MODEL)]
        args += [attn, ssm_pre, glu_w, glu_b.reshape(1, -1).astype(F32),
                 attn_gain.reshape(1, -1).astype(F32), ssm_gain.reshape(1, -1).astype(F32), w_out]
    if proj is not None:
        mixer_gain, w_in = proj
        body = _ffn_proj_kernel
        in_specs += [vec(D_MODEL), resident(D_MODEL, IN_WIDTH)]
        args += [mixer_gain.reshape(1, D_MODEL), w_in]
        for width, dtype in ((ATTN_WIDTH, BF16), (KV_WIDTH, BF16), (KV_WIDTH, BF16),
                             (SSM_WIDTH, F32)):
            out_specs.append(row(width))
            out_shape.append(jax.ShapeDtypeStruct((t, width), dtype))
    in_specs += [vec(D_MODEL), vec(D_MODEL), hbm, hbm, hbm]
    args += [gain.reshape(1, D_MODEL), final_gain.reshape(1, D_MODEL), w_gate, w_up, w_down]
    return pl.pallas_call(
        functools.partial(body, final_norm=final_norm),
        grid=(t // tile,),
        in_specs=in_specs,
        out_specs=out_specs,
        out_shape=out_shape,
        scratch_shapes=[pltpu.VMEM((D_MODEL, D_FF), BF16),
                        pltpu.VMEM((D_MODEL, D_FF), BF16),
                        pltpu.VMEM((D_FF, D_MODEL), BF16),
                        pltpu.VMEM((tile, D_FF), BF16),
                        pltpu.VMEM((2, 2, D_MODEL, FF_TILE), F32),
                        pltpu.VMEM((2, FF_TILE, D_MODEL), F32),
                        pltpu.SemaphoreType.DMA((3, 2))],
        compiler_params=pltpu.CompilerParams(
            dimension_semantics=("arbitrary",), vmem_limit_bytes=VMEM_LIMIT_BYTES),
    )(*args)


def _attn_kernel(sink_ref, q_ref, kp_ref, kc_ref, kn_ref, vp_ref, vc_ref, vn_ref, o_ref,
                 bias_ref, k_ref, v_ref, s_ref):
    n = pl.program_id(1)
    last = pl.num_programs(1) - 1
    nblk = ATTN_BLOCKS_PER_STEP

    @pl.when(n == 0)
    def _():
        kj = lax.broadcasted_iota(jnp.int32, (3 * BLOCK, BLOCK), 0)
        qi = lax.broadcasted_iota(jnp.int32, (3 * BLOCK, BLOCK), 1)
        rel = jnp.abs(kj - BLOCK - qi)
        dist = rel.astype(F32)
        inside = rel <= WINDOW
        has_prev = kj >= BLOCK
        has_next = kj < 2 * BLOCK
        for variant, ok in enumerate((inside & has_prev, inside, inside & has_next)):
            for h in range(ATTN_HEADS):
                slope = float(2.0 ** (-8.0 * (h + 1) / ATTN_HEADS))
                bias_ref[variant, h] = jnp.where(ok, (-slope * LOG2_E) * dist, NEG_INF)
        for kh in range(ATTN_KV_HEADS):
            v_ref[:, (2 * kh + 1) * HEAD_DIM:(2 * kh + 2) * HEAD_DIM] = jnp.ones(
                (v_ref.shape[0], HEAD_DIM), BF16)

    spans = ((slice(0, BLOCK), kp_ref, vp_ref),
             (slice(BLOCK, (nblk + 1) * BLOCK), kc_ref, vc_ref),
             (slice((nblk + 1) * BLOCK, (nblk + 2) * BLOCK), kn_ref, vn_ref))
    for rows, k_in, v_in in spans:
        k_ref[rows, :] = k_in[...]
        for kh in range(ATTN_KV_HEADS):
            v_ref[rows, 2 * kh * HEAD_DIM:(2 * kh + 1) * HEAD_DIM] = (
                v_in[:, kh * HEAD_DIM:(kh + 1) * HEAD_DIM])

    def scores(j, kh):
        kcat = k_ref[j * BLOCK:(j + 3) * BLOCK, kh * HEAD_DIM:(kh + 1) * HEAD_DIM]
        heads = [kh * Q_PER_KV + g for g in range(Q_PER_KV)]
        qs = jnp.concatenate(
            [q_ref[j * BLOCK:(j + 1) * BLOCK, h * HEAD_DIM:(h + 1) * HEAD_DIM] for h in heads],
            axis=0)
        s_ref[j, kh] = lax.dot_general(kcat, qs, NT_DIMS,
                                       preferred_element_type=F32)

    for kh in range(ATTN_KV_HEADS):
        scores(0, kh)
    for j in range(nblk):
        variant = 1
        if j == 0:
            variant = jnp.where(n == 0, 0, variant)
        if j == nblk - 1:
            variant = jnp.where(n == last, 2, variant)
        outs = []
        for kh in range(ATTN_KV_HEADS):
            if j + 1 < nblk:
                scores(j + 1, kh)
            v_ones = v_ref[j * BLOCK:(j + 3) * BLOCK, 2 * kh * HEAD_DIM:(2 * kh + 2) * HEAD_DIM]
            for g in range(Q_PER_KV):
                h = kh * Q_PER_KV + g
                s = s_ref[j, kh, :, g * BLOCK:(g + 1) * BLOCK] + bias_ref[variant, h]
                sink = sink_ref[h] * LOG2_E
                m = jnp.maximum(jnp.max(s, axis=0, keepdims=True), sink)
                e = jnp.exp2(s - m).astype(BF16)
                pv = lax.dot_general(v_ones, e, TN_DIMS, preferred_element_type=F32)
                den = pv[HEAD_DIM:HEAD_DIM + 1] + jnp.exp2(sink - m)
                outs.append(pv[:HEAD_DIM] / den)
        o_ref[j * BLOCK:(j + 1) * BLOCK, :] = jnp.concatenate(outs, axis=0).T.astype(o_ref.dtype)


def _attention(q, k, v, sinks):
    b, seq, _ = q.shape
    nblk = ATTN_BLOCKS_PER_STEP
    steps = seq // (nblk * BLOCK)
    nb = seq // BLOCK
    assert seq % (nblk * BLOCK) == 0 and nb >= 2
    edge = lambda f: pl.BlockSpec((None, BLOCK, KV_WIDTH), f)
    body = pl.BlockSpec((None, nblk * BLOCK, KV_WIDTH), lambda bi, n: (bi, n, 0))
    prev = lambda bi, n: (bi, jnp.maximum(n * nblk - 1, 0), 0)
    nxt = lambda bi, n: (bi, jnp.minimum((n + 1) * nblk, nb - 1), 0)
    rows = pl.BlockSpec((None, nblk * BLOCK, ATTN_WIDTH), lambda bi, n: (bi, n, 0))
    return pl.pallas_call(
        _attn_kernel,
        grid=(b, steps),
        in_specs=[pl.BlockSpec(memory_space=pltpu.SMEM), rows,
                  edge(prev), body, edge(nxt), edge(prev), body, edge(nxt)],
        out_specs=rows,
        out_shape=jax.ShapeDtypeStruct((b, seq, ATTN_WIDTH), BF16),
        scratch_shapes=[pltpu.VMEM((3, ATTN_HEADS, 3 * BLOCK, BLOCK), F32),
                        pltpu.VMEM(((nblk + 2) * BLOCK, KV_WIDTH), BF16),
                        pltpu.VMEM(((nblk + 2) * BLOCK, 2 * KV_WIDTH), BF16),
                        pltpu.VMEM((nblk, ATTN_KV_HEADS, 3 * BLOCK, Q_PER_KV * BLOCK), F32)],
        compiler_params=pltpu.CompilerParams(
            dimension_semantics=("parallel", "arbitrary"), vmem_limit_bytes=VMEM_LIMIT_BYTES),
    )(sinks.astype(F32), q, k, k, k, v, v, v)


def _dot_nt_split(a, b):
    a_hi, b_hi = a.astype(BF16), b.astype(BF16)
    a_lo = (a - a_hi.astype(F32)).astype(BF16)
    b_lo = (b - b_hi.astype(F32)).astype(BF16)
    dot = functools.partial(lax.dot_general, dimension_numbers=NT_DIMS,
                            preferred_element_type=F32)
    return dot(a_hi, b_hi) + dot(a_hi, b_lo) + dot(a_lo, b_hi)


def _ssm_ops_kernel(prm_ref, bc_ref, ein_ref, toep_ref, eout_ref, aq_ref):
    q, hc, p = SSM_CHUNK, SSM_CH, SSM_STATE
    fwd = lax.broadcasted_iota(jnp.int32, (1, 2 * p), 1) < p
    zero_row = jnp.zeros((1, 2 * p), F32)
    row_id = lax.broadcasted_iota(jnp.int32, (CHUNK_WIDTH, CHUNK_WIDTH), 0)
    col_id = lax.broadcasted_iota(jnp.int32, (CHUNK_WIDTH, CHUNK_WIDTH), 1)

    def table(select, n):
        picks = [select(m) for m in range(n)]
        re = jnp.concatenate([jnp.broadcast_to(r, (hc, 2 * p)) for r, _ in picks], axis=0)
        im = jnp.concatenate([jnp.broadcast_to(i, (hc, 2 * p)) for _, i in picks], axis=0)
        return re, im

    def tile_rows(x, n):
        return jnp.concatenate([x] * n, axis=0)

    for gi in range(prm_ref.shape[0]):
        lr = jnp.minimum(prm_ref[gi, 0:1, :], LAMBDA_RE_MAX)
        li = prm_ref[gi, 1:2, :]
        dt = jnp.exp(prm_ref[gi, 2:3, :])
        mag = jnp.exp(lr * dt)
        a_r = mag * jnp.cos(li * dt)
        a_i = mag * jnp.sin(li * dt)
        den = lr * lr + li * li
        coef_r = ((a_r - 1.0) * lr + a_i * li) / den
        coef_i = (a_i * lr - (a_r - 1.0) * li) / den
        b_r, b_i = bc_ref[gi, 0], bc_ref[gi, 1]
        c_r, c_i = bc_ref[gi, 2], bc_ref[gi, 3]
        bb_r = coef_r * b_r - coef_i * b_i
        bb_i = coef_r * b_i + coef_i * b_r

        pw = [(jnp.ones((1, 2 * p), F32), zero_row)]
        for _ in range(q):
            r, i = pw[-1]
            pw.append((r * a_r - i * a_i, r * a_i + i * a_r))

        def both(f_idx, b_idx):
            fr, fi = pw[f_idx] if f_idx is not None else (zero_row, zero_row)
            br, bi = pw[b_idx] if b_idx is not None else (zero_row, zero_row)
            return jnp.where(fwd, fr, br), jnp.where(fwd, fi, bi)

        p_r, p_i = table(lambda i: both(q - 1 - i, i), q)
        tb_r, tb_i = tile_rows(bb_r, q), tile_rows(bb_i, q)
        ein = jnp.concatenate([tb_r * p_r - tb_i * p_i, tb_r * p_i + tb_i * p_r], axis=1)
        ein_ref[gi] = ein.astype(BF16)

        p_r, p_i = table(lambda j: both(j + 1, q - j), q)
        tc_r, tc_i = tile_rows(c_r, q), tile_rows(c_i, q)
        eout = jnp.concatenate([tc_r * p_r - tc_i * p_i, -(tc_r * p_i + tc_i * p_r)], axis=1)
        eout_ref[gi] = eout.astype(BF16)

        def lag(m):
            return both(m - (q - 1) if q - 1 <= m <= 2 * q - 2 else None,
                        (q - 1) - m if m <= q - 1 else None)

        p_r, p_i = table(lag, 2 * q)
        tc_r, tc_i = tile_rows(c_r, 2 * q), tile_rows(c_i, 2 * q)
        cpt = jnp.concatenate([tc_r * p_r - tc_i * p_i, tc_r * p_i + tc_i * p_r], axis=1)
        bcat = jnp.concatenate([bb_r, -bb_i], axis=1)
        kern = _dot_nt_split(bcat, cpt)
        toep = jnp.concatenate(
            [kern[:, hc * (q - 1 - i):hc * (q - 1 - i) + CHUNK_WIDTH] for i in range(q)], axis=0)
        skip = jnp.concatenate([prm_ref[gi, 3:4, :]] * (CHUNK_WIDTH // (2 * p)), axis=1)
        toep_ref[gi] = (toep + jnp.where(row_id == col_id, skip, 0.0)).astype(BF16)

        aq_ref[gi] = jnp.concatenate(
            [pw[q][0], pw[q][1], jnp.zeros((SUBLANES - 2, 2 * p), F32)], axis=0)


def _ssm_operators(lam_re, lam_im, log_dt, b_re, b_im, c_re, c_im, d_skip):
    g, p, hc = SSM_GROUPS, SSM_STATE, SSM_CH
    lanes = lambda a: a.astype(F32).transpose(1, 0, 2).reshape(g, 2 * p)
    prm = jnp.stack(
        [lanes(lam_re), lanes(lam_im),
         jnp.repeat(log_dt.astype(F32).T, p, axis=1),
         jnp.tile(d_skip.astype(F32), (1, 2 * p // hc))]
        + [jnp.zeros((g, 2 * p), F32)] * (SUBLANES - 4), axis=1)
    bc = jnp.stack(
        [b_re.astype(F32).transpose(1, 3, 0, 2).reshape(g, hc, 2 * p),
         b_im.astype(F32).transpose(1, 3, 0, 2).reshape(g, hc, 2 * p),
         c_re.astype(F32).transpose(1, 2, 0, 3).reshape(g, hc, 2 * p),
         c_im.astype(F32).transpose(1, 2, 0, 3).reshape(g, hc, 2 * p)], axis=1)
    gs = OPS_GROUPS_PER_STEP
    mat = pl.BlockSpec((gs, CHUNK_WIDTH, CHUNK_WIDTH), lambda s: (s, 0, 0))
    mat_shape = jax.ShapeDtypeStruct((g, CHUNK_WIDTH, CHUNK_WIDTH), BF16)
    return pl.pallas_call(
        _ssm_ops_kernel,
        grid=(g // gs,),
        in_specs=[pl.BlockSpec((gs, SUBLANES, 2 * p), lambda s: (s, 0, 0)),
                  pl.BlockSpec((gs, 4, hc, 2 * p), lambda s: (s, 0, 0, 0))],
        out_specs=[mat, mat, mat, pl.BlockSpec((gs, SUBLANES, 2 * p), lambda s: (s, 0, 0))],
        out_shape=[mat_shape, mat_shape, mat_shape,
                   jax.ShapeDtypeStruct((g, SUBLANES, 2 * p), F32)],
        compiler_params=pltpu.CompilerParams(
            dimension_semantics=("parallel",), vmem_limit_bytes=VMEM_LIMIT_BYTES),
    )(prm, bc)


def _lane_roll(x, shift):
    shift %= LANES
    return jnp.concatenate([x[:, LANES - shift:], x[:, :LANES - shift]], axis=1)


def _block_transpose(v, lane_block):
    return _block_transposes([v], lane_block)[0]


def _block_transposes(groups, lane_block):
    n = len(groups[0])
    skewed = [[v[i] if i == 0 else _lane_roll(v[i], SSM_CH * i) for i in range(n)] for v in groups]
    picked = []
    for sk in skewed:
        rows = []
        for g in range(n):
            p = sk[(0 - g) % n]
            for j in range(1, n):
                p = jnp.where(lane_block == j, sk[(j - g) % n], p)
            rows.append(p)
        picked.append(rows)
    return [[p[g] if g == 0 else _lane_roll(p[g], -SSM_CH * g) for g in range(n)] for p in picked]


def _ssm_kernel(u_hbm, ein_ref, toep_ref, eout_ref, aq_ref, y_hbm,
                io_ref, ug_ref, s_ref, x_ref, in_sem, out_sem, *, batch, seq, pitch):
    q, half = SSM_CHUNK, SSM_STATE
    n_chunks = seq // q
    gpb = GROUPS_PER_BLOCK
    slab = 2 * batch
    lane_block = lax.broadcasted_iota(jnp.int32, (slab, LANES), 1) // SSM_CH
    step_id = pl.program_id(0)
    last_step = pl.num_programs(0) - 1
    slot = lax.rem(step_id, 2)

    def column_copies(block, slot_, to_vmem):
        lanes = pl.ds(pl.multiple_of(block * LANES, LANES), LANES)
        copies = []
        for b in range(batch):
            hbm = (u_hbm if to_vmem else y_hbm).at[b, :, lanes]
            vmem = io_ref.at[slot_, pl.ds(b * pitch, seq), :]
            if to_vmem:
                copies.append(pltpu.make_async_copy(hbm, vmem, in_sem.at[slot_, b]))
            else:
                copies.append(pltpu.make_async_copy(vmem, hbm, out_sem.at[slot_, b]))
        return copies

    def start(copies):
        for copy in copies:
            copy.start()

    def wait(copies):
        for copy in copies:
            copy.wait()

    @pl.when(step_id == 0)
    def _():
        start(column_copies(0, 0, True))

    @pl.when(step_id < last_step)
    def _():
        @pl.when(step_id >= 1)
        def _():
            wait(column_copies(step_id - 1, 1 - slot, False))
        start(column_copies(step_id + 1, 1 - slot, True))

    wait(column_copies(step_id, slot, True))
    buf = io_ref.at[slot]

    def gather(cp, carry):
        rows = pl.ds(pl.multiple_of(cp * slab, slab), slab)
        parts = []
        for part in range(q // gpb):
            t0 = cp * 2 * q + part * gpb
            parts.append([jnp.concatenate([buf[pl.ds(t0 + i, batch, stride=pitch), :],
                                           buf[pl.ds(t0 + q + i, batch, stride=pitch), :]],
                                          axis=0).astype(BF16) for i in range(gpb)])
        for part, w in enumerate(_block_transposes(parts, lane_block)):
            for g in range(gpb):
                ug_ref[g, rows, part * LANES:(part + 1) * LANES] = w[g]
        return carry

    lax.fori_loop(0, n_chunks // 2, gather, 0, unroll=SSM_RELAYOUT_UNROLL)

    fwd_lane = lax.broadcasted_iota(jnp.int32, (batch, 2 * half), 1) < half
    zeros = jnp.zeros((batch, half), F32)
    last_rows = pl.ds((n_chunks - 1) * batch, batch)
    ni = SSM_INTERLEAVE
    for g0 in range(0, gpb, ni):
        for gi in range(ni):
            s_ref[gi] = jnp.dot(ug_ref[g0 + gi], ein_ref[g0 + gi], preferred_element_type=F32)
            x_ref[gi, 0:batch, 0:half] = zeros
            x_ref[gi, 0:batch, 2 * half:3 * half] = zeros
            x_ref[gi, last_rows, half:2 * half] = zeros
            x_ref[gi, last_rows, 3 * half:4 * half] = zeros

        def step(k, carry):
            rf = pl.multiple_of(k * batch, batch)
            rb = pl.multiple_of((n_chunks - 1 - k) * batch, batch)
            new = []
            for gi in range(ni):
                xr, xi = carry[gi]
                sre = jnp.where(fwd_lane, s_ref[gi, pl.ds(rf, batch), 0:2 * half],
                                s_ref[gi, pl.ds(rb, batch), 0:2 * half])
                sim = jnp.where(fwd_lane, s_ref[gi, pl.ds(rf, batch), 2 * half:4 * half],
                                s_ref[gi, pl.ds(rb, batch), 2 * half:4 * half])
                ar = aq_ref[g0 + gi, 0:1, :]
                ai = aq_ref[g0 + gi, 1:2, :]
                nr = ar * xr - ai * xi + sre
                nim = ar * xi + ai * xr + sim
                x_ref[gi, pl.ds(rf + batch, batch), 0:half] = nr[:, 0:half]
                x_ref[gi, pl.ds(rf + batch, batch), 2 * half:3 * half] = nim[:, 0:half]
                x_ref[gi, pl.ds(rb - batch, batch), half:2 * half] = nr[:, half:2 * half]
                x_ref[gi, pl.ds(rb - batch, batch), 3 * half:4 * half] = nim[:, half:2 * half]
                new.append((nr, nim))
            return tuple(new)

        init = tuple((jnp.zeros((batch, 2 * half), F32), jnp.zeros((batch, 2 * half), F32))
                     for _ in range(ni))
        lax.fori_loop(0, n_chunks - 1, step, init)

        for gi in range(ni):
            g = g0 + gi
            y = jnp.dot(ug_ref[g], toep_ref[g], preferred_element_type=F32)
            y += lax.dot_general(x_ref[gi].astype(BF16), eout_ref[g], NT_DIMS,
                                 preferred_element_type=F32)
            ug_ref[g] = y.astype(BF16)

    def scatter(cp, carry):
        rows = pl.ds(pl.multiple_of(cp * slab, slab), slab)
        parts = [[ug_ref[g, rows, part * LANES:(part + 1) * LANES] for g in range(gpb)]
                 for part in range(q // gpb)]
        for part, w in enumerate(_block_transposes(parts, lane_block)):
            t0 = cp * 2 * q + part * gpb
            for j in range(gpb):
                wj = w[j].astype(F32)
                buf[pl.ds(t0 + j, batch, stride=pitch), :] = wj[0:batch]
                buf[pl.ds(t0 + q + j, batch, stride=pitch), :] = wj[batch:slab]
        return carry

    lax.fori_loop(0, n_chunks // 2, scatter, 0, unroll=SSM_RELAYOUT_UNROLL)

    start(column_copies(step_id, slot, False))

    @pl.when(step_id == last_step)
    def _():
        @pl.when(step_id >= 1)
        def _():
            wait(column_copies(step_id - 1, 1 - slot, False))
        wait(column_copies(step_id, slot, False))


def _padded_seq(seq):
    return seq + SUBLANES if seq % (2 * SUBLANES) == 0 else seq


def _ssm(u, lam_re, lam_im, log_dt, b_re, b_im, c_re, c_im, d_skip):
    batch, seq, _ = u.shape
    pitch = _padded_seq(seq)
    rows = batch * seq // SSM_CHUNK
    ein, toep, eout, a_q = _ssm_operators(lam_re, lam_im, log_dt, b_re, b_im, c_re, c_im, d_skip)
    gpb = GROUPS_PER_BLOCK
    mat = pl.BlockSpec((gpb, CHUNK_WIDTH, CHUNK_WIDTH), lambda s: (s, 0, 0))
    hbm = pl.BlockSpec(memory_space=pl.ANY)
    return pl.pallas_call(
        functools.partial(_ssm_kernel, batch=batch, seq=seq, pitch=pitch),
        grid=(SSM_GROUPS // gpb,),
        in_specs=[hbm, mat, mat, mat,
                  pl.BlockSpec((gpb, SUBLANES, 2 * SSM_STATE), lambda s: (s, 0, 0))],
        out_specs=hbm,
        out_shape=jax.ShapeDtypeStruct((batch, seq, SSM_WIDTH), F32),
        scratch_shapes=[pltpu.VMEM((2, batch * pitch, LANES), F32),
                        pltpu.VMEM((gpb, rows, CHUNK_WIDTH), BF16),
                        pltpu.VMEM((SSM_INTERLEAVE, rows, 4 * SSM_STATE), F32),
                        pltpu.VMEM((SSM_INTERLEAVE, rows, 4 * SSM_STATE), F32),
                        pltpu.SemaphoreType.DMA((2, batch)),
                        pltpu.SemaphoreType.DMA((2, batch))],
        compiler_params=pltpu.CompilerParams(
            dimension_semantics=("arbitrary",), vmem_limit_bytes=VMEM_LIMIT_BYTES),
    )(u, ein, toep, eout, a_q)


def kernel(x, norm_ffn1, ffn1_w_gate, ffn1_w_up, ffn1_w_down, norm_mix, w_in, attn_sinks,
           ssm_lambda_re, ssm_lambda_im, ssm_log_dt, ssm_b_re, ssm_b_im, ssm_c_re, ssm_c_im,
           ssm_d, ssm_glu_w, ssm_glu_b, attn_out_norm, ssm_out_norm, w_out,
           norm_ffn2, ffn2_w_gate, ffn2_w_up, ffn2_w_down, final_norm):
    b, seq, d = x.shape
    depth = norm_ffn1.shape[0]
    assert d == D_MODEL and seq % BLOCK == 0 and (b * seq) % FFN_TOKEN_TILE == 0
    assert b == SUBLANES and seq % (2 * SSM_CHUNK * SSM_RELAYOUT_UNROLL) == 0
    h = x.reshape(b * seq, d).astype(F32)
    for l in range(depth):
        h, q, k, v, u = _ffn(h, norm_ffn1[l], ffn1_w_gate[l], ffn1_w_up[l], ffn1_w_down[l],
                             final_norm, final_norm=False, proj=(norm_mix[l], w_in[l]))
        attn = _attention(q.reshape(b, seq, -1), k.reshape(b, seq, -1), v.reshape(b, seq, -1),
                          attn_sinks[l])
        ssm_pre = _ssm(u.reshape(b, seq, -1), ssm_lambda_re[l], ssm_lambda_im[l], ssm_log_dt[l],
                       ssm_b_re[l], ssm_b_im[l], ssm_c_re[l], ssm_c_im[l], ssm_d[l])
        mix = (attn.reshape(b * seq, -1), ssm_pre.reshape(b * seq, -1), ssm_glu_w[l],
               ssm_glu_b[l], attn_out_norm[l], ssm_out_norm[l], w_out[l])
        h, = _ffn(h, norm_ffn2[l], ffn2_w_gate[l], ffn2_w_up[l], ffn2_w_down[l],
                  final_norm, final_norm=(l == depth - 1), mix=mix)
    return h.reshape(b, seq, d).astype(x.dtype)
```

```python
import functools
import math

import jax
import jax.numpy as jnp
from jax import lax
from jax.experimental import pallas as pl
from jax.experimental.pallas import tpu as pltpu

F32 = jnp.float32
BF16 = jnp.bfloat16

D_MODEL = 1024
ATTN_HEADS = 8
ATTN_KV_HEADS = 2
Q_PER_KV = ATTN_HEADS // ATTN_KV_HEADS
HEAD_DIM = 64
ATTN_WIDTH = ATTN_HEADS * HEAD_DIM
KV_WIDTH = ATTN_KV_HEADS * HEAD_DIM
WINDOW = 128
BLOCK = 128
SSM_CH = 16
SSM_WIDTH = D_MODEL - ATTN_WIDTH
SSM_GROUPS = SSM_WIDTH // SSM_CH
SSM_STATE = 64
IN_WIDTH = ATTN_WIDTH + 2 * KV_WIDTH + SSM_WIDTH
D_FF = 2816
EPS = 1e-6
NEG_INF = -1e30
LAMBDA_RE_MAX = -1e-4
LOG2_E = math.log2(math.e)
QUERY_SCALE = HEAD_DIM ** -0.5 * LOG2_E

LANES = 128
SUBLANES = 8
VMEM_LIMIT_BYTES = 56 * 1024 * 1024

FFN_TOKEN_TILE = 512
FFN_SUBTILES = 2
FF_TILE = 256
ATTN_BLOCKS_PER_STEP = 8
SSM_CHUNK = 16
CHUNK_WIDTH = SSM_CHUNK * SSM_CH
GROUPS_PER_BLOCK = LANES // SSM_CH
SSM_INTERLEAVE = 8
SSM_RELAYOUT_UNROLL = 4
OPS_GROUPS_PER_STEP = 8

NN_DIMS = (((1,), (0,)), ((), ()))
NT_DIMS = (((1,), (1,)), ((), ()))
TN_DIMS = (((0,), (0,)), ((), ()))


def _rms(x):
    return x * lax.rsqrt(jnp.mean(x * x, axis=-1, keepdims=True) + EPS)


def _mixed_update(rows, attn_ref, ssm_ref, gw_ref, gb_ref, ga_ref, gs_ref, wo_ref):
    y = jax.nn.gelu(ssm_ref[rows, :])
    z = jnp.dot(y.astype(BF16), gw_ref[...].astype(BF16), preferred_element_type=F32) + gb_ref[...]
    s = y * jax.nn.sigmoid(z)
    sn = _rms(s) * gs_ref[...]
    an = _rms(attn_ref[rows, :].astype(F32)) * ga_ref[...]
    mixed = jnp.concatenate([an, sn], axis=-1).astype(BF16)
    return jnp.dot(mixed, wo_ref[...].astype(BF16), preferred_element_type=F32)


def _mixer_inputs(y, rows, gain_ref, w_ref, q_ref, k_ref, v_ref, u_ref):
    hn = (_rms(y) * gain_ref[...]).astype(BF16)
    proj = jnp.dot(hn, w_ref[...].astype(BF16), preferred_element_type=F32)
    q_ref[rows, :] = (proj[:, :ATTN_WIDTH] * QUERY_SCALE).astype(BF16)
    k_ref[rows, :] = proj[:, ATTN_WIDTH:ATTN_WIDTH + KV_WIDTH].astype(BF16)
    v_ref[rows, :] = proj[:, ATTN_WIDTH + KV_WIDTH:ATTN_WIDTH + 2 * KV_WIDTH].astype(BF16)
    u_ref[rows, :] = proj[:, ATTN_WIDTH + 2 * KV_WIDTH:]


def _ffn_body(read_x, gain_ref, fgain_ref, wg_hbm, wu_hbm, wd_hbm, o_ref,
              wg_ref, wu_ref, wd_ref, act_ref, gu_stage, d_stage, sem, *, final_norm,
              epilogue=None):
    nj = D_FF // FF_TILE

    def weight_copies(j, slot):
        span = pl.ds(j * FF_TILE, FF_TILE)
        return (pltpu.make_async_copy(wg_hbm.at[:, span], gu_stage.at[0, slot], sem.at[0, slot]),
                pltpu.make_async_copy(wu_hbm.at[:, span], gu_stage.at[1, slot], sem.at[1, slot]),
                pltpu.make_async_copy(wd_hbm.at[span, :], d_stage.at[slot], sem.at[2, slot]))

    def step(stage_weights):
        if stage_weights:
            for copy in weight_copies(0, 0):
                copy.start()
        sub = o_ref.shape[0] // FFN_SUBTILES
        row_groups = [slice(h * sub, (h + 1) * sub) for h in range(FFN_SUBTILES)]
        xs = [read_x(rows) for rows in row_groups]
        hns = [(_rms(x) * gain_ref[...]).astype(BF16) for x in xs]
        for j in range(nj):
            cols = slice(j * FF_TILE, (j + 1) * FF_TILE)
            if stage_weights:
                slot = j % 2
                if j + 1 < nj:
                    for copy in weight_copies(j + 1, 1 - slot):
                        copy.start()
                for copy in weight_copies(j, slot):
                    copy.wait()
                wg_ref[:, cols] = gu_stage[0, slot].astype(BF16)
                wu_ref[:, cols] = gu_stage[1, slot].astype(BF16)
                wd_ref[cols, :] = d_stage[slot].astype(BF16)
            for rows, hn in zip(row_groups, hns):
                g = jnp.dot(hn, wg_ref[:, cols], preferred_element_type=F32)
                u = jnp.dot(hn, wu_ref[:, cols], preferred_element_type=F32)
                act_ref[rows, cols] = (g * jax.nn.sigmoid(g) * u).astype(BF16)
        for rows, x in zip(row_groups, xs):
            y = x + 0.5 * jnp.dot(act_ref[rows, :], wd_ref[...], preferred_element_type=F32)
            if final_norm:
                y = _rms(y) * fgain_ref[...]
            o_ref[rows, :] = y
            if epilogue is not None:
                epilogue(y, rows)

    pl.when(pl.program_id(0) == 0)(functools.partial(step, True))
    pl.when(pl.program_id(0) != 0)(functools.partial(step, False))


def _ffn_proj_kernel(x_ref, mgain_ref, win_ref, gain_ref, fgain_ref, wg_hbm, wu_hbm, wd_hbm,
                     o_ref, q_ref, k_ref, v_ref, u_ref, *scratch, final_norm):
    emit = lambda y, rows: _mixer_inputs(y, rows, mgain_ref, win_ref, q_ref, k_ref, v_ref, u_ref)
    _ffn_body(lambda rows: x_ref[rows, :], gain_ref, fgain_ref, wg_hbm, wu_hbm, wd_hbm, o_ref,
              *scratch, final_norm=final_norm, epilogue=emit)


def _mix_ffn_kernel(x_ref, attn_ref, ssm_ref, gw_ref, gb_ref, ga_ref, gs_ref, wo_ref, *ffn_refs,
                    final_norm):
    read_x = lambda rows: x_ref[rows, :] + _mixed_update(rows, attn_ref, ssm_ref, gw_ref, gb_ref,
                                                         ga_ref, gs_ref, wo_ref)
    _ffn_body(read_x, *ffn_refs, final_norm=final_norm)


def _ffn(x, gain, w_gate, w_up, w_down, final_gain, final_norm, mix=None, proj=None):
    assert (mix is None) != (proj is None)
    t = x.shape[0]
    tile = FFN_TOKEN_TILE
    vec = lambda width: pl.BlockSpec((1, width), lambda i: (0, 0))
    resident = lambda r, c: pl.BlockSpec((r, c), lambda i: (0, 0), pipeline_mode=pl.Buffered(1))
    row = lambda width: pl.BlockSpec((tile, width), lambda i: (i, 0))
    hbm = pl.BlockSpec(memory_space=pl.ANY)
    in_specs, args = [row(D_MODEL)], [x]
    out_specs, out_shape = [row(D_MODEL)], [jax.ShapeDtypeStruct((t, D_MODEL), F32)]
    if mix is not None:
        attn, ssm_pre, glu_w, glu_b, attn_gain, ssm_gain, w_out = mix
        body = _mix_ffn_kernel
        in_specs += [row(ATTN_WIDTH), row(SSM_WIDTH),
                     resident(SSM_WIDTH, SSM_WIDTH), vec(SSM_WIDTH), vec(ATTN_WIDTH),
                     vec(SSM_WIDTH), resident(D_MODEL, D_MODEL)]
        args += [attn, ssm_pre, glu_w, glu_b.reshape(1, -1).astype(F32),
                 attn_gain.reshape(1, -1).astype(F32), ssm_gain.reshape(1, -1).astype(F32), w_out]
    if proj is not None:
        mixer_gain, w_in = proj
        body = _ffn_proj_kernel
        in_specs += [vec(D_MODEL), resident(D_MODEL, IN_WIDTH)]
        args += [mixer_gain.reshape(1, D_MODEL), w_in]
        for width, dtype in ((ATTN_WIDTH, BF16), (KV_WIDTH, BF16), (KV_WIDTH, BF16),
                             (SSM_WIDTH, F32)):
            out_specs.append(row(width))
            out_shape.append(jax.ShapeDtypeStruct((t, width), dtype))
    in_specs += [vec(D_MODEL), vec(D_MODEL), hbm, hbm, hbm]
    args += [gain.reshape(1, D_MODEL), final_gain.reshape(1, D_MODEL), w_gate, w_up, w_down]
    return pl.pallas_call(
        functools.partial(body, final_norm=final_norm),
        grid=(t // tile,),
        in_specs=in_specs,
        out_specs=out_specs,
        out_shape=out_shape,
        scratch_shapes=[pltpu.VMEM((D_MODEL, D_FF), BF16),
                        pltpu.VMEM((D_MODEL, D_FF), BF16),
                        pltpu.VMEM((D_FF, D_MODEL), BF16),
                        pltpu.VMEM((tile, D_FF), BF16),
                        pltpu.VMEM((2, 2, D_MODEL, FF_TILE), F32),
                        pltpu.VMEM((2, FF_TILE, D_MODEL), F32),
                        pltpu.SemaphoreType.DMA((3, 2))],
        compiler_params=pltpu.CompilerParams(
            dimension_semantics=("arbitrary",), vmem_limit_bytes=VMEM_LIMIT_BYTES),
    )(*args)


def _attn_kernel(sink_ref, q_ref, kp_ref, kc_ref, kn_ref, vp_ref, vc_ref, vn_ref, o_ref,
                 bias_ref, k_ref, v_ref, s_ref):
    n = pl.program_id(1)
    last = pl.num_programs(1) - 1
    nblk = ATTN_BLOCKS_PER_STEP

    @pl.when(n == 0)
    def _():
        kj = lax.broadcasted_iota(jnp.int32, (3 * BLOCK, BLOCK), 0)
        qi = lax.broadcasted_iota(jnp.int32, (3 * BLOCK, BLOCK), 1)
        rel = jnp.abs(kj - BLOCK - qi)
        dist = rel.astype(F32)
        inside = rel <= WINDOW
        has_prev = kj >= BLOCK
        has_next = kj < 2 * BLOCK
        for variant, ok in enumerate((inside & has_prev, inside, inside & has_next)):
            for h in range(ATTN_HEADS):
                slope = float(2.0 ** (-8.0 * (h + 1) / ATTN_HEADS))
                bias_ref[variant, h] = jnp.where(ok, (-slope * LOG2_E) * dist, NEG_INF)
        for kh in range(ATTN_KV_HEADS):
            v_ref[:, (2 * kh + 1) * HEAD_DIM:(2 * kh + 2) * HEAD_DIM] = jnp.ones(
                (v_ref.shape[0], HEAD_DIM), BF16)

    spans = ((slice(0, BLOCK), kp_ref, vp_ref),
             (slice(BLOCK, (nblk + 1) * BLOCK), kc_ref, vc_ref),
             (slice((nblk + 1) * BLOCK, (nblk + 2) * BLOCK), kn_ref, vn_ref))
    for rows, k_in, v_in in spans:
        k_ref[rows, :] = k_in[...]
        for kh in range(ATTN_KV_HEADS):
            v_ref[rows, 2 * kh * HEAD_DIM:(2 * kh + 1) * HEAD_DIM] = (
                v_in[:, kh * HEAD_DIM:(kh + 1) * HEAD_DIM])

    def scores(j, kh):
        kcat = k_ref[j * BLOCK:(j + 3) * BLOCK, kh * HEAD_DIM:(kh + 1) * HEAD_DIM]
        heads = [kh * Q_PER_KV + g for g in range(Q_PER_KV)]
        qs = jnp.concatenate(
            [q_ref[j * BLOCK:(j + 1) * BLOCK, h * HEAD_DIM:(h + 1) * HEAD_DIM] for h in heads],
            axis=0)
        s_ref[j, kh] = lax.dot_general(kcat, qs, NT_DIMS,
                                       preferred_element_type=F32)

    for kh in range(ATTN_KV_HEADS):
        scores(0, kh)
    for j in range(nblk):
        variant = 1
        if j == 0:
            variant = jnp.where(n == 0, 0, variant)
        if j == nblk - 1:
            variant = jnp.where(n == last, 2, variant)
        outs = []
        for kh in range(ATTN_KV_HEADS):
            if j + 1 < nblk:
                scores(j + 1, kh)
            v_ones = v_ref[j * BLOCK:(j + 3) * BLOCK, 2 * kh * HEAD_DIM:(2 * kh + 2) * HEAD_DIM]
            for g in range(Q_PER_KV):
                h = kh * Q_PER_KV + g
                s = s_ref[j, kh, :, g * BLOCK:(g + 1) * BLOCK] + bias_ref[variant, h]
                sink = sink_ref[h] * LOG2_E
                m = jnp.maximum(jnp.max(s, axis=0, keepdims=True), sink)
                e = jnp.exp2(s - m).astype(BF16)
                pv = lax.dot_general(v_ones, e, TN_DIMS, preferred_element_type=F32)
                den = pv[HEAD_DIM:HEAD_DIM + 1] + jnp.exp2(sink - m)
                outs.append(pv[:HEAD_DIM] / den)
        o_ref[j * BLOCK:(j + 1) * BLOCK, :] = jnp.concatenate(outs, axis=0).T.astype(o_ref.dtype)


def _attention(q, k, v, sinks):
    b, seq, _ = q.shape
    nblk = ATTN_BLOCKS_PER_STEP
    steps = seq // (nblk * BLOCK)
    nb = seq // BLOCK
    assert seq % (nblk * BLOCK) == 0 and nb >= 2
    edge = lambda f: pl.BlockSpec((None, BLOCK, KV_WIDTH), f)
    body = pl.BlockSpec((None, nblk * BLOCK, KV_WIDTH), lambda bi, n: (bi, n, 0))
    prev = lambda bi, n: (bi, jnp.maximum(n * nblk - 1, 0), 0)
    nxt = lambda bi, n: (bi, jnp.minimum((n + 1) * nblk, nb - 1), 0)
    rows = pl.BlockSpec((None, nblk * BLOCK, ATTN_WIDTH), lambda bi, n: (bi, n, 0))
    return pl.pallas_call(
        _attn_kernel,
        grid=(b, steps),
        in_specs=[pl.BlockSpec(memory_space=pltpu.SMEM), rows,
                  edge(prev), body, edge(nxt), edge(prev), body, edge(nxt)],
        out_specs=rows,
        out_shape=jax.ShapeDtypeStruct((b, seq, ATTN_WIDTH), BF16),
        scratch_shapes=[pltpu.VMEM((3, ATTN_HEADS, 3 * BLOCK, BLOCK), F32),
                        pltpu.VMEM(((nblk + 2) * BLOCK, KV_WIDTH), BF16),
                        pltpu.VMEM(((nblk + 2) * BLOCK, 2 * KV_WIDTH), BF16),
                        pltpu.VMEM((nblk, ATTN_KV_HEADS, 3 * BLOCK, Q_PER_KV * BLOCK), F32)],
        compiler_params=pltpu.CompilerParams(
            dimension_semantics=("parallel", "arbitrary"), vmem_limit_bytes=VMEM_LIMIT_BYTES),
    )(sinks.astype(F32), q, k, k, k, v, v, v)


def _dot_split(a, b, dims):
    a_hi, b_hi = a.astype(BF16), b.astype(BF16)
    a_lo = (a - a_hi.astype(F32)).astype(BF16)
    b_lo = (b - b_hi.astype(F32)).astype(BF16)
    dot = functools.partial(lax.dot_general, dimension_numbers=dims, preferred_element_type=F32)
    return dot(a_hi, b_hi) + dot(a_hi, b_lo) + dot(a_lo, b_hi)


def _ssm_ops_kernel(ldt_ref, lre_ref, lim_ref, d_ref, bre_ref, bim_ref, cre_ref, cim_ref,
                    ein_ref, toep_ref, eout_ref, aq_ref):
    q, hc, p = SSM_CHUNK, SSM_CH, SSM_STATE
    gs = lre_ref.shape[1]
    g_base = pl.program_id(0) * gs
    fwd = lax.broadcasted_iota(jnp.int32, (1, 2 * p), 1) < p
    zero_row = jnp.zeros((1, 2 * p), F32)
    row_id = lax.broadcasted_iota(jnp.int32, (CHUNK_WIDTH, CHUNK_WIDTH), 0)
    col_id = lax.broadcasted_iota(jnp.int32, (CHUNK_WIDTH, CHUNK_WIDTH), 1)
    eye = (lax.broadcasted_iota(jnp.int32, (p, p), 0)
           == lax.broadcasted_iota(jnp.int32, (p, p), 1)).astype(F32)
    repeat = (lax.broadcasted_iota(jnp.int32, (hc, CHUNK_WIDTH), 1) % hc
              == lax.broadcasted_iota(jnp.int32, (hc, CHUNK_WIDTH), 0)).astype(F32)

    def mode_lanes(ref, gi):
        return jnp.concatenate([ref[0, gi:gi + 1, :], ref[1, gi:gi + 1, :]], axis=1)

    def table(select, n):
        picks = [select(m) for m in range(n)]
        re = jnp.concatenate([jnp.broadcast_to(r, (hc, 2 * p)) for r, _ in picks], axis=0)
        im = jnp.concatenate([jnp.broadcast_to(i, (hc, 2 * p)) for _, i in picks], axis=0)
        return re, im

    def tile_rows(x, n):
        return jnp.concatenate([x] * n, axis=0)

    for gi in range(gs):
        lr = jnp.minimum(mode_lanes(lre_ref, gi), LAMBDA_RE_MAX)
        li = mode_lanes(lim_ref, gi)
        dt = jnp.exp(jnp.where(fwd, ldt_ref[0, g_base + gi], ldt_ref[1, g_base + gi]))
        mag = jnp.exp(lr * dt)
        a_r = mag * jnp.cos(li * dt)
        a_i = mag * jnp.sin(li * dt)
        den = lr * lr + li * li
        coef_r = ((a_r - 1.0) * lr + a_i * li) / den
        coef_i = (a_i * lr - (a_r - 1.0) * li) / den
        b_r = jnp.concatenate([_dot_split(bre_ref[x, gi], eye, TN_DIMS) for x in range(2)], axis=1)
        b_i = jnp.concatenate([_dot_split(bim_ref[x, gi], eye, TN_DIMS) for x in range(2)], axis=1)
        c_r = jnp.concatenate([cre_ref[0, gi], cre_ref[1, gi]], axis=1)
        c_i = jnp.concatenate([cim_ref[0, gi], cim_ref[1, gi]], axis=1)
        bb_r = coef_r * b_r - coef_i * b_i
        bb_i = coef_r * b_i + coef_i * b_r

        pw = [(jnp.ones((1, 2 * p), F32), zero_row)]
        for _ in range(q):
            r, i = pw[-1]
            pw.append((r * a_r - i * a_i, r * a_i + i * a_r))

        def both(f_idx, b_idx):
            fr, fi = pw[f_idx] if f_idx is not None else (zero_row, zero_row)
            br, bi = pw[b_idx] if b_idx is not None else (zero_row, zero_row)
            return jnp.where(fwd, fr, br), jnp.where(fwd, fi, bi)

        p_r, p_i = table(lambda i: both(q - 1 - i, i), q)
        tb_r, tb_i = tile_rows(bb_r, q), tile_rows(bb_i, q)
        ein = jnp.concatenate([tb_r * p_r - tb_i * p_i, tb_r * p_i + tb_i * p_r], axis=1)
        ein_ref[gi] = ein.astype(BF16)

        p_r, p_i = table(lambda j: both(j + 1, q - j), q)
        tc_r, tc_i = tile_rows(c_r, q), tile_rows(c_i, q)
        eout = jnp.concatenate([tc_r * p_r - tc_i * p_i, -(tc_r * p_i + tc_i * p_r)], axis=1)
        eout_ref[gi] = eout.astype(BF16)

        def lag(m):
            return both(m - (q - 1) if q - 1 <= m <= 2 * q - 2 else None,
                        (q - 1) - m if m <= q - 1 else None)

        p_r, p_i = table(lag, 2 * q)
        tc_r, tc_i = tile_rows(c_r, 2 * q), tile_rows(c_i, 2 * q)
        cpt = jnp.concatenate([tc_r * p_r - tc_i * p_i, tc_r * p_i + tc_i * p_r], axis=1)
        bcat = jnp.concatenate([bb_r, -bb_i], axis=1)
        kern = _dot_split(bcat, cpt, NT_DIMS)
        toep = jnp.concatenate(
            [kern[:, hc * (q - 1 - i):hc * (q - 1 - i) + CHUNK_WIDTH] for i in range(q)], axis=0)
        skip = _dot_split(d_ref[gi:gi + 1, :], repeat, NN_DIMS)
        toep_ref[gi] = (toep + jnp.where(row_id == col_id, skip, 0.0)).astype(BF16)

        aq_ref[gi] = jnp.concatenate(
            [pw[q][0], pw[q][1], jnp.zeros((SUBLANES - 2, 2 * p), F32)], axis=0)


def _ssm_operators(lam_re, lam_im, log_dt, b_re, b_im, c_re, c_im, d_skip):
    g, p, hc = SSM_GROUPS, SSM_STATE, SSM_CH
    gs = OPS_GROUPS_PER_STEP
    mat = pl.BlockSpec((gs, CHUNK_WIDTH, CHUNK_WIDTH), lambda s: (s, 0, 0))
    mat_shape = jax.ShapeDtypeStruct((g, CHUNK_WIDTH, CHUNK_WIDTH), BF16)
    per_dir = lambda *tail: pl.BlockSpec((2, gs) + tail, lambda s: (0, s) + (0,) * len(tail))
    f32 = lambda a: a.astype(F32)
    return pl.pallas_call(
        _ssm_ops_kernel,
        grid=(g // gs,),
        in_specs=[pl.BlockSpec(memory_space=pltpu.SMEM),
                  per_dir(p), per_dir(p), pl.BlockSpec((gs, hc), lambda s: (s, 0)),
                  per_dir(p, hc), per_dir(p, hc), per_dir(hc, p), per_dir(hc, p)],
        out_specs=[mat, mat, mat, pl.BlockSpec((gs, SUBLANES, 2 * p), lambda s: (s, 0, 0))],
        out_shape=[mat_shape, mat_shape, mat_shape,
                   jax.ShapeDtypeStruct((g, SUBLANES, 2 * p), F32)],
        compiler_params=pltpu.CompilerParams(
            dimension_semantics=("parallel",), vmem_limit_bytes=VMEM_LIMIT_BYTES),
    )(f32(log_dt), f32(lam_re), f32(lam_im), f32(d_skip), f32(b_re), f32(b_im), f32(c_re),
      f32(c_im))


def _lane_roll(x, shift):
    shift %= LANES
    return jnp.concatenate([x[:, LANES - shift:], x[:, :LANES - shift]], axis=1)


def _block_transpose(v, lane_block):
    return _block_transposes([v], lane_block)[0]


def _block_transposes(groups, lane_block):
    n = len(groups[0])
    skewed = [[v[i] if i == 0 else _lane_roll(v[i], SSM_CH * i) for i in range(n)] for v in groups]
    picked = []
    for sk in skewed:
        rows = []
        for g in range(n):
            p = sk[(0 - g) % n]
            for j in range(1, n):
                p = jnp.where(lane_block == j, sk[(j - g) % n], p)
            rows.append(p)
        picked.append(rows)
    return [[p[g] if g == 0 else _lane_roll(p[g], -SSM_CH * g) for g in range(n)] for p in picked]


def _ssm_kernel(u_hbm, ein_ref, toep_ref, eout_ref, aq_ref, y_hbm,
                io_ref, ug_ref, s_ref, x_ref, in_sem, out_sem, *, batch, seq, pitch):
    q, half = SSM_CHUNK, SSM_STATE
    n_chunks = seq // q
    gpb = GROUPS_PER_BLOCK
    slab = 2 * batch
    lane_block = lax.broadcasted_iota(jnp.int32, (slab, LANES), 1) // SSM_CH
    step_id = pl.program_id(0)
    last_step = pl.num_programs(0) - 1
    slot = lax.rem(step_id, 2)

    def column_copies(block, slot_, to_vmem):
        lanes = pl.ds(pl.multiple_of(block * LANES, LANES), LANES)
        copies = []
        for b in range(batch):
            hbm = (u_hbm if to_vmem else y_hbm).at[b, :, lanes]
            vmem = io_ref.at[slot_, pl.ds(b * pitch, seq), :]
            if to_vmem:
                copies.append(pltpu.make_async_copy(hbm, vmem, in_sem.at[slot_, b]))
            else:
                copies.append(pltpu.make_async_copy(vmem, hbm, out_sem.at[slot_, b]))
        return copies

    def start(copies):
        for copy in copies:
            copy.start()

    def wait(copies):
        for copy in copies:
            copy.wait()

    @pl.when(step_id == 0)
    def _():
        start(column_copies(0, 0, True))

    @pl.when(step_id < last_step)
    def _():
        @pl.when(step_id >= 1)
        def _():
            wait(column_copies(step_id - 1, 1 - slot, False))
        start(column_copies(step_id + 1, 1 - slot, True))

    wait(column_copies(step_id, slot, True))
    buf = io_ref.at[slot]

    def gather(cp, carry):
        rows = pl.ds(pl.multiple_of(cp * slab, slab), slab)
        parts = []
        for part in range(q // gpb):
            t0 = cp * 2 * q + part * gpb
            parts.append([jnp.concatenate([buf[pl.ds(t0 + i, batch, stride=pitch), :],
                                           buf[pl.ds(t0 + q + i, batch, stride=pitch), :]],
                                          axis=0).astype(BF16) for i in range(gpb)])
        for part, w in enumerate(_block_transposes(parts, lane_block)):
            for g in range(gpb):
                ug_ref[g, rows, part * LANES:(part + 1) * LANES] = w[g]
        return carry

    lax.fori_loop(0, n_chunks // 2, gather, 0, unroll=SSM_RELAYOUT_UNROLL)

    fwd_lane = lax.broadcasted_iota(jnp.int32, (batch, 2 * half), 1) < half
    zeros = jnp.zeros((batch, half), F32)
    last_rows = pl.ds((n_chunks - 1) * batch, batch)
    ni = SSM_INTERLEAVE
    for g0 in range(0, gpb, ni):
        for gi in range(ni):
            s_ref[gi] = jnp.dot(ug_ref[g0 + gi], ein_ref[g0 + gi], preferred_element_type=F32)
            x_ref[gi, 0:batch, 0:half] = zeros
            x_ref[gi, 0:batch, 2 * half:3 * half] = zeros
            x_ref[gi, last_rows, half:2 * half] = zeros
            x_ref[gi, last_rows, 3 * half:4 * half] = zeros

        def step(k, carry):
            rf = pl.multiple_of(k * batch, batch)
            rb = pl.multiple_of((n_chunks - 1 - k) * batch, batch)
            new = []
            for gi in range(ni):
                xr, xi = carry[gi]
                sre = jnp.where(fwd_lane, s_ref[gi, pl.ds(rf, batch), 0:2 * half],
                                s_ref[gi, pl.ds(rb, batch), 0:2 * half])
                sim = jnp.where(fwd_lane, s_ref[gi, pl.ds(rf, batch), 2 * half:4 * half],
                                s_ref[gi, pl.ds(rb, batch), 2 * half:4 * half])
                ar = aq_ref[g0 + gi, 0:1, :]
                ai = aq_ref[g0 + gi, 1:2, :]
                nr = ar * xr - ai * xi + sre
                nim = ar * xi + ai * xr + sim
                x_ref[gi, pl.ds(rf + batch, batch), 0:half] = nr[:, 0:half]
                x_ref[gi, pl.ds(rf + batch, batch), 2 * half:3 * half] = nim[:, 0:half]
                x_ref[gi, pl.ds(rb - batch, batch), half:2 * half] = nr[:, half:2 * half]
                x_ref[gi, pl.ds(rb - batch, batch), 3 * half:4 * half] = nim[:, half:2 * half]
                new.append((nr, nim))
            return tuple(new)

        init = tuple((jnp.zeros((batch, 2 * half), F32), jnp.zeros((batch, 2 * half), F32))
                     for _ in range(ni))
        lax.fori_loop(0, n_chunks - 1, step, init)

        for gi in range(ni):
            g = g0 + gi
            y = jnp.dot(ug_ref[g], toep_ref[g], preferred_element_type=F32)
            y += lax.dot_general(x_ref[gi].astype(BF16), eout_ref[g], NT_DIMS,
                                 preferred_element_type=F32)
            ug_ref[g] = y.astype(BF16)

    def scatter(cp, carry):
        rows = pl.ds(pl.multiple_of(cp * slab, slab), slab)
        parts = [[ug_ref[g, rows, part * LANES:(part + 1) * LANES] for g in range(gpb)]
                 for part in range(q // gpb)]
        for part, w in enumerate(_block_transposes(parts, lane_block)):
            t0 = cp * 2 * q + part * gpb
            for j in range(gpb):
                wj = w[j].astype(F32)
                buf[pl.ds(t0 + j, batch, stride=pitch), :] = wj[0:batch]
                buf[pl.ds(t0 + q + j, batch, stride=pitch), :] = wj[batch:slab]
        return carry

    lax.fori_loop(0, n_chunks // 2, scatter, 0, unroll=SSM_RELAYOUT_UNROLL)

    start(column_copies(step_id, slot, False))

    @pl.when(step_id == last_step)
    def _():
        @pl.when(step_id >= 1)
        def _():
            wait(column_copies(step_id - 1, 1 - slot, False))
        wait(column_copies(step_id, slot, False))


def _padded_seq(seq):
    return seq + SUBLANES if seq % (2 * SUBLANES) == 0 else seq


def _ssm(u, lam_re, lam_im, log_dt, b_re, b_im, c_re, c_im, d_skip):
    batch, seq, _ = u.shape
    pitch = _padded_seq(seq)
    rows = batch * seq // SSM_CHUNK
    ein, toep, eout, a_q = _ssm_operators(lam_re, lam_im, log_dt, b_re, b_im, c_re, c_im, d_skip)
    gpb = GROUPS_PER_BLOCK
    mat = pl.BlockSpec((gpb, CHUNK_WIDTH, CHUNK_WIDTH), lambda s: (s, 0, 0))
    hbm = pl.BlockSpec(memory_space=pl.ANY)
    return pl.pallas_call(
        functools.partial(_ssm_kernel, batch=batch, seq=seq, pitch=pitch),
        grid=(SSM_GROUPS // gpb,),
        in_specs=[hbm, mat, mat, mat,
                  pl.BlockSpec((gpb, SUBLANES, 2 * SSM_STATE), lambda s: (s, 0, 0))],
        out_specs=hbm,
        out_shape=jax.ShapeDtypeStruct((batch, seq, SSM_WIDTH), F32),
        scratch_shapes=[pltpu.VMEM((2, batch * pitch, LANES), F32),
                        pltpu.VMEM((gpb, rows, CHUNK_WIDTH), BF16),
                        pltpu.VMEM((SSM_INTERLEAVE, rows, 4 * SSM_STATE), F32),
                        pltpu.VMEM((SSM_INTERLEAVE, rows, 4 * SSM_STATE), F32),
                        pltpu.SemaphoreType.DMA((2, batch)),
                        pltpu.SemaphoreType.DMA((2, batch))],
        compiler_params=pltpu.CompilerParams(
            dimension_semantics=("arbitrary",), vmem_limit_bytes=VMEM_LIMIT_BYTES),
    )(u, ein, toep, eout, a_q)


def kernel(x, norm_ffn1, ffn1_w_gate, ffn1_w_up, ffn1_w_down, norm_mix, w_in, attn_sinks,
           ssm_lambda_re, ssm_lambda_im, ssm_log_dt, ssm_b_re, ssm_b_im, ssm_c_re, ssm_c_im,
           ssm_d, ssm_glu_w, ssm_glu_b, attn_out_norm, ssm_out_norm, w_out,
           norm_ffn2, ffn2_w_gate, ffn2_w_up, ffn2_w_down, final_norm):
    b, seq, d = x.shape
    depth = norm_ffn1.shape[0]
    assert d == D_MODEL and seq % BLOCK == 0 and (b * seq) % FFN_TOKEN_TILE == 0
    assert b == SUBLANES and seq % (2 * SSM_CHUNK * SSM_RELAYOUT_UNROLL) == 0
    h = x.reshape(b * seq, d).astype(F32)
    for l in range(depth):
        h, q, k, v, u = _ffn(h, norm_ffn1[l], ffn1_w_gate[l], ffn1_w_up[l], ffn1_w_down[l],
                             final_norm, final_norm=False, proj=(norm_mix[l], w_in[l]))
        attn = _attention(q.reshape(b, seq, -1), k.reshape(b, seq, -1), v.reshape(b, seq, -1),
                          attn_sinks[l])
        ssm_pre = _ssm(u.reshape(b, seq, -1), ssm_lambda_re[l], ssm_lambda_im[l], ssm_log_dt[l],
                       ssm_b_re[l], ssm_b_im[l], ssm_c_re[l], ssm_c_im[l], ssm_d[l])
        mix = (attn.reshape(b * seq, -1), ssm_pre.reshape(b * seq, -1), ssm_glu_w[l],
               ssm_glu_b[l], attn_out_norm[l], ssm_out_norm[l], w_out[l])
        h, = _ffn(h, norm_ffn2[l], ffn2_w_gate[l], ffn2_w_up[l], ffn2_w_down[l],
                  final_norm, final_norm=(l == depth - 1), mix=mix)
    return h.reshape(b, seq, d).astype(x.dtype)
```

```python
import functools
import math

import jax
import jax.numpy as jnp
from jax import lax
from jax.experimental import pallas as pl
from jax.experimental.pallas import tpu as pltpu

F32 = jnp.float32
BF16 = jnp.bfloat16

D_MODEL = 1024
ATTN_HEADS = 8
ATTN_KV_HEADS = 2
Q_PER_KV = ATTN_HEADS // ATTN_KV_HEADS
HEAD_DIM = 64
ATTN_WIDTH = ATTN_HEADS * HEAD_DIM
KV_WIDTH = ATTN_KV_HEADS * HEAD_DIM
WINDOW = 128
BLOCK = 128
SSM_CH = 16
SSM_WIDTH = D_MODEL - ATTN_WIDTH
SSM_GROUPS = SSM_WIDTH // SSM_CH
SSM_STATE = 64
IN_WIDTH = ATTN_WIDTH + 2 * KV_WIDTH + SSM_WIDTH
D_FF = 2816
EPS = 1e-6
NEG_INF = -1e30
LAMBDA_RE_MAX = -1e-4
LOG2_E = math.log2(math.e)
QUERY_SCALE = HEAD_DIM ** -0.5 * LOG2_E

LANES = 128
SUBLANES = 8
VMEM_LIMIT_BYTES = 56 * 1024 * 1024

FFN_TOKEN_TILE = 512
FFN_SUBTILES = 2
FF_TILE = 256
ATTN_BLOCKS_PER_STEP = 8
SSM_CHUNK = 16
CHUNK_WIDTH = SSM_CHUNK * SSM_CH
GROUPS_PER_BLOCK = LANES // SSM_CH
SSM_INTERLEAVE = 8
SSM_RELAYOUT_UNROLL = 4
SSM_DMA_SLICES = 4
OPS_GROUPS_PER_STEP = 8

NT_DIMS = (((1,), (1,)), ((), ()))
TN_DIMS = (((0,), (0,)), ((), ()))


def _rms(x):
    return x * lax.rsqrt(jnp.mean(x * x, axis=-1, keepdims=True) + EPS)


def _mixed_update(rows, attn_ref, ssm_ref, gw_ref, gb_ref, ga_ref, gs_ref, wo_ref):
    y = jax.nn.gelu(ssm_ref[rows, :])
    z = jnp.dot(y.astype(BF16), gw_ref[...].astype(BF16), preferred_element_type=F32) + gb_ref[...]
    s = y * jax.nn.sigmoid(z)
    sn = _rms(s) * gs_ref[...]
    an = _rms(attn_ref[rows, :].astype(F32)) * ga_ref[...]
    mixed = jnp.concatenate([an, sn], axis=-1).astype(BF16)
    return jnp.dot(mixed, wo_ref[...].astype(BF16), preferred_element_type=F32)


def _mixer_inputs(y, rows, gain_ref, w_ref, q_ref, k_ref, v_ref, u_ref):
    hn = (_rms(y) * gain_ref[...]).astype(BF16)
    proj = jnp.dot(hn, w_ref[...].astype(BF16), preferred_element_type=F32)
    q_ref[rows, :] = (proj[:, :ATTN_WIDTH] * QUERY_SCALE).astype(BF16)
    k_ref[rows, :] = proj[:, ATTN_WIDTH:ATTN_WIDTH + KV_WIDTH].astype(BF16)
    v_ref[rows, :] = proj[:, ATTN_WIDTH + KV_WIDTH:ATTN_WIDTH + 2 * KV_WIDTH].astype(BF16)
    u_ref[rows, :] = proj[:, ATTN_WIDTH + 2 * KV_WIDTH:]


def _ffn_body(read_x, gain_ref, fgain_ref, wg_hbm, wu_hbm, wd_hbm, o_ref,
              wg_ref, wu_ref, wd_ref, act_ref, gu_stage, d_stage, sem, *, final_norm,
              epilogue=None):
    nj = D_FF // FF_TILE

    def weight_copies(j, slot):
        span = pl.ds(j * FF_TILE, FF_TILE)
        return (pltpu.make_async_copy(wg_hbm.at[:, span], gu_stage.at[0, slot], sem.at[0, slot]),
                pltpu.make_async_copy(wu_hbm.at[:, span], gu_stage.at[1, slot], sem.at[1, slot]),
                pltpu.make_async_copy(wd_hbm.at[span, :], d_stage.at[slot], sem.at[2, slot]))

    def step(stage_weights):
        if stage_weights:
            for copy in weight_copies(0, 0):
                copy.start()
        sub = o_ref.shape[0] // FFN_SUBTILES
        row_groups = [slice(h * sub, (h + 1) * sub) for h in range(FFN_SUBTILES)]
        xs = [read_x(rows) for rows in row_groups]
        hns = [(_rms(x) * gain_ref[...]).astype(BF16) for x in xs]
        for j in range(nj):
            cols = slice(j * FF_TILE, (j + 1) * FF_TILE)
            if stage_weights:
                slot = j % 2
                if j + 1 < nj:
                    for copy in weight_copies(j + 1, 1 - slot):
                        copy.start()
                for copy in weight_copies(j, slot):
                    copy.wait()
                wg_ref[:, cols] = gu_stage[0, slot].astype(BF16)
                wu_ref[:, cols] = gu_stage[1, slot].astype(BF16)
                wd_ref[cols, :] = d_stage[slot].astype(BF16)
            for rows, hn in zip(row_groups, hns):
                g = jnp.dot(hn, wg_ref[:, cols], preferred_element_type=F32)
                u = jnp.dot(hn, wu_ref[:, cols], preferred_element_type=F32)
                act_ref[rows, cols] = (g * jax.nn.sigmoid(g) * u).astype(BF16)
        for rows, x in zip(row_groups, xs):
            y = x + 0.5 * jnp.dot(act_ref[rows, :], wd_ref[...], preferred_element_type=F32)
            if final_norm:
                y = _rms(y) * fgain_ref[...]
            o_ref[rows, :] = y
            if epilogue is not None:
                epilogue(y, rows)

    pl.when(pl.program_id(0) == 0)(functools.partial(step, True))
    pl.when(pl.program_id(0) != 0)(functools.partial(step, False))


def _ffn_proj_kernel(x_ref, mgain_ref, win_ref, gain_ref, fgain_ref, wg_hbm, wu_hbm, wd_hbm,
                     o_ref, q_ref, k_ref, v_ref, u_ref, *scratch, final_norm):
    emit = lambda y, rows: _mixer_inputs(y, rows, mgain_ref, win_ref, q_ref, k_ref, v_ref, u_ref)
    _ffn_body(lambda rows: x_ref[rows, :], gain_ref, fgain_ref, wg_hbm, wu_hbm, wd_hbm, o_ref,
              *scratch, final_norm=final_norm, epilogue=emit)


def _mix_ffn_kernel(x_ref, attn_ref, ssm_ref, gw_ref, gb_ref, ga_ref, gs_ref, wo_ref, *ffn_refs,
                    final_norm):
    read_x = lambda rows: x_ref[rows, :] + _mixed_update(rows, attn_ref, ssm_ref, gw_ref, gb_ref,
                                                         ga_ref, gs_ref, wo_ref)
    _ffn_body(read_x, *ffn_refs, final_norm=final_norm)


def _ffn(x, gain, w_gate, w_up, w_down, final_gain, final_norm, mix=None, proj=None):
    assert (mix is None) != (proj is None)
    t = x.shape[0]
    tile = FFN_TOKEN_TILE
    vec = lambda width: pl.BlockSpec((1, width), lambda i: (0, 0))
    resident = lambda r, c: pl.BlockSpec((r, c), lambda i: (0, 0), pipeline_mode=pl.Buffered(1))
    row = lambda width: pl.BlockSpec((tile, width), lambda i: (i, 0))
    hbm = pl.BlockSpec(memory_space=pl.ANY)
    in_specs, args = [row(D_MODEL)], [x]
    out_specs, out_shape = [row(D_MODEL)], [jax.ShapeDtypeStruct((t, D_MODEL), F32)]
    if mix is not None:
        attn, ssm_pre, glu_w, glu_b, attn_gain, ssm_gain, w_out = mix
        body = _mix_ffn_kernel
        in_specs += [row(ATTN_WIDTH), row(SSM_WIDTH),
                     resident(SSM_WIDTH, SSM_WIDTH), vec(SSM_WIDTH), vec(ATTN_WIDTH),
                     vec(SSM_WIDTH), resident(D_MODEL, D_MODEL)]
        args += [attn, ssm_pre, glu_w, glu_b.reshape(1, -1).astype(F32),
                 attn_gain.reshape(1, -1).astype(F32), ssm_gain.reshape(1, -1).astype(F32), w_out]
    if proj is not None:
        mixer_gain, w_in = proj
        body = _ffn_proj_kernel
        in_specs += [vec(D_MODEL), resident(D_MODEL, IN_WIDTH)]
        args += [mixer_gain.reshape(1, D_MODEL), w_in]
        for width, dtype in ((ATTN_WIDTH, BF16), (KV_WIDTH, BF16), (KV_WIDTH, BF16),
                             (SSM_WIDTH, F32)):
            out_specs.append(row(width))
            out_shape.append(jax.ShapeDtypeStruct((t, width), dtype))
    in_specs += [vec(D_MODEL), vec(D_MODEL), hbm, hbm, hbm]
    args += [gain.reshape(1, D_MODEL), final_gain.reshape(1, D_MODEL), w_gate, w_up, w_down]
    return pl.pallas_call(
        functools.partial(body, final_norm=final_norm),
        grid=(t // tile,),
        in_specs=in_specs,
        out_specs=out_specs,
        out_shape=out_shape,
        scratch_shapes=[pltpu.VMEM((D_MODEL, D_FF), BF16),
                        pltpu.VMEM((D_MODEL, D_FF), BF16),
                        pltpu.VMEM((D_FF, D_MODEL), BF16),
                        pltpu.VMEM((tile, D_FF), BF16),
                        pltpu.VMEM((2, 2, D_MODEL, FF_TILE), F32),
                        pltpu.VMEM((2, FF_TILE, D_MODEL), F32),
                        pltpu.SemaphoreType.DMA((3, 2))],
        compiler_params=pltpu.CompilerParams(
            dimension_semantics=("arbitrary",), vmem_limit_bytes=VMEM_LIMIT_BYTES),
    )(*args)


def _attn_kernel(sink_ref, q_ref, kp_ref, kc_ref, kn_ref, vp_ref, vc_ref, vn_ref, o_ref,
                 bias_ref, k_ref, v_ref, s_ref):
    n = pl.program_id(1)
    last = pl.num_programs(1) - 1
    nblk = ATTN_BLOCKS_PER_STEP

    @pl.when(n == 0)
    def _():
        kj = lax.broadcasted_iota(jnp.int32, (3 * BLOCK, BLOCK), 0)
        qi = lax.broadcasted_iota(jnp.int32, (3 * BLOCK, BLOCK), 1)
        rel = jnp.abs(kj - BLOCK - qi)
        dist = rel.astype(F32)
        inside = rel <= WINDOW
        has_prev = kj >= BLOCK
        has_next = kj < 2 * BLOCK
        for variant, ok in enumerate((inside & has_prev, inside, inside & has_next)):
            for h in range(ATTN_HEADS):
                slope = float(2.0 ** (-8.0 * (h + 1) / ATTN_HEADS))
                bias_ref[variant, h] = jnp.where(ok, (-slope * LOG2_E) * dist, NEG_INF)
        for kh in range(ATTN_KV_HEADS):
            v_ref[:, (2 * kh + 1) * HEAD_DIM:(2 * kh + 2) * HEAD_DIM] = jnp.ones(
                (v_ref.shape[0], HEAD_DIM), BF16)

    spans = ((slice(0, BLOCK), kp_ref, vp_ref),
             (slice(BLOCK, (nblk + 1) * BLOCK), kc_ref, vc_ref),
             (slice((nblk + 1) * BLOCK, (nblk + 2) * BLOCK), kn_ref, vn_ref))
    for rows, k_in, v_in in spans:
        k_ref[rows, :] = k_in[...]
        for kh in range(ATTN_KV_HEADS):
            v_ref[rows, 2 * kh * HEAD_DIM:(2 * kh + 1) * HEAD_DIM] = (
                v_in[:, kh * HEAD_DIM:(kh + 1) * HEAD_DIM])

    def scores(j, kh):
        kcat = k_ref[j * BLOCK:(j + 3) * BLOCK, kh * HEAD_DIM:(kh + 1) * HEAD_DIM]
        heads = [kh * Q_PER_KV + g for g in range(Q_PER_KV)]
        qs = jnp.concatenate(
            [q_ref[j * BLOCK:(j + 1) * BLOCK, h * HEAD_DIM:(h + 1) * HEAD_DIM] for h in heads],
            axis=0)
        s_ref[j, kh] = lax.dot_general(kcat, qs, NT_DIMS,
                                       preferred_element_type=F32)

    for kh in range(ATTN_KV_HEADS):
        scores(0, kh)
    for j in range(nblk):
        variant = 1
        if j == 0:
            variant = jnp.where(n == 0, 0, variant)
        if j == nblk - 1:
            variant = jnp.where(n == last, 2, variant)
        outs = []
        for kh in range(ATTN_KV_HEADS):
            if j + 1 < nblk:
                scores(j + 1, kh)
            v_ones = v_ref[j * BLOCK:(j + 3) * BLOCK, 2 * kh * HEAD_DIM:(2 * kh + 2) * HEAD_DIM]
            for g in range(Q_PER_KV):
                h = kh * Q_PER_KV + g
                s = s_ref[j, kh, :, g * BLOCK:(g + 1) * BLOCK] + bias_ref[variant, h]
                sink = sink_ref[h] * LOG2_E
                m = jnp.maximum(jnp.max(s, axis=0, keepdims=True), sink)
                e = jnp.exp2(s - m).astype(BF16)
                pv = lax.dot_general(v_ones, e, TN_DIMS, preferred_element_type=F32)
                den = pv[HEAD_DIM:HEAD_DIM + 1] + jnp.exp2(sink - m)
                outs.append(pv[:HEAD_DIM] / den)
        o_ref[j * BLOCK:(j + 1) * BLOCK, :] = jnp.concatenate(outs, axis=0).T.astype(o_ref.dtype)


def _attention(q, k, v, sinks):
    b, seq, _ = q.shape
    nblk = ATTN_BLOCKS_PER_STEP
    steps = seq // (nblk * BLOCK)
    nb = seq // BLOCK
    assert seq % (nblk * BLOCK) == 0 and nb >= 2
    edge = lambda f: pl.BlockSpec((None, BLOCK, KV_WIDTH), f)
    body = pl.BlockSpec((None, nblk * BLOCK, KV_WIDTH), lambda bi, n: (bi, n, 0))
    prev = lambda bi, n: (bi, jnp.maximum(n * nblk - 1, 0), 0)
    nxt = lambda bi, n: (bi, jnp.minimum((n + 1) * nblk, nb - 1), 0)
    rows = pl.BlockSpec((None, nblk * BLOCK, ATTN_WIDTH), lambda bi, n: (bi, n, 0))
    return pl.pallas_call(
        _attn_kernel,
        grid=(b, steps),
        in_specs=[pl.BlockSpec(memory_space=pltpu.SMEM), rows,
                  edge(prev), body, edge(nxt), edge(prev), body, edge(nxt)],
        out_specs=rows,
        out_shape=jax.ShapeDtypeStruct((b, seq, ATTN_WIDTH), BF16),
        scratch_shapes=[pltpu.VMEM((3, ATTN_HEADS, 3 * BLOCK, BLOCK), F32),
                        pltpu.VMEM(((nblk + 2) * BLOCK, KV_WIDTH), BF16),
                        pltpu.VMEM(((nblk + 2) * BLOCK, 2 * KV_WIDTH), BF16),
                        pltpu.VMEM((nblk, ATTN_KV_HEADS, 3 * BLOCK, Q_PER_KV * BLOCK), F32)],
        compiler_params=pltpu.CompilerParams(
            dimension_semantics=("parallel", "arbitrary"), vmem_limit_bytes=VMEM_LIMIT_BYTES),
    )(sinks.astype(F32), q, k, k, k, v, v, v)


def _dot_nt_split(a, b):
    a_hi, b_hi = a.astype(BF16), b.astype(BF16)
    a_lo = (a - a_hi.astype(F32)).astype(BF16)
    b_lo = (b - b_hi.astype(F32)).astype(BF16)
    dot = functools.partial(lax.dot_general, dimension_numbers=NT_DIMS,
                            preferred_element_type=F32)
    return dot(a_hi, b_hi) + dot(a_hi, b_lo) + dot(a_lo, b_hi)


def _ssm_ops_kernel(prm_ref, bc_ref, ein_ref, toep_ref, eout_ref, aq_ref):
    q, hc, p = SSM_CHUNK, SSM_CH, SSM_STATE
    fwd = lax.broadcasted_iota(jnp.int32, (1, 2 * p), 1) < p
    zero_row = jnp.zeros((1, 2 * p), F32)
    row_id = lax.broadcasted_iota(jnp.int32, (CHUNK_WIDTH, CHUNK_WIDTH), 0)
    col_id = lax.broadcasted_iota(jnp.int32, (CHUNK_WIDTH, CHUNK_WIDTH), 1)

    def table(select, n):
        picks = [select(m) for m in range(n)]
        re = jnp.concatenate([jnp.broadcast_to(r, (hc, 2 * p)) for r, _ in picks], axis=0)
        im = jnp.concatenate([jnp.broadcast_to(i, (hc, 2 * p)) for _, i in picks], axis=0)
        return re, im

    def tile_rows(x, n):
        return jnp.concatenate([x] * n, axis=0)

    for gi in range(prm_ref.shape[0]):
        lr = jnp.minimum(prm_ref[gi, 0:1, :], LAMBDA_RE_MAX)
        li = prm_ref[gi, 1:2, :]
        dt = jnp.exp(prm_ref[gi, 2:3, :])
        mag = jnp.exp(lr * dt)
        a_r = mag * jnp.cos(li * dt)
        a_i = mag * jnp.sin(li * dt)
        den = lr * lr + li * li
        coef_r = ((a_r - 1.0) * lr + a_i * li) / den
        coef_i = (a_i * lr - (a_r - 1.0) * li) / den
        b_r, b_i = bc_ref[gi, 0], bc_ref[gi, 1]
        c_r, c_i = bc_ref[gi, 2], bc_ref[gi, 3]
        bb_r = coef_r * b_r - coef_i * b_i
        bb_i = coef_r * b_i + coef_i * b_r

        pw = [(jnp.ones((1, 2 * p), F32), zero_row)]
        for _ in range(q):
            r, i = pw[-1]
            pw.append((r * a_r - i * a_i, r * a_i + i * a_r))

        def both(f_idx, b_idx):
            fr, fi = pw[f_idx] if f_idx is not None else (zero_row, zero_row)
            br, bi = pw[b_idx] if b_idx is not None else (zero_row, zero_row)
            return jnp.where(fwd, fr, br), jnp.where(fwd, fi, bi)

        p_r, p_i = table(lambda i: both(q - 1 - i, i), q)
        tb_r, tb_i = tile_rows(bb_r, q), tile_rows(bb_i, q)
        ein = jnp.concatenate([tb_r * p_r - tb_i * p_i, tb_r * p_i + tb_i * p_r], axis=1)
        ein_ref[gi] = ein.astype(BF16)

        p_r, p_i = table(lambda j: both(j + 1, q - j), q)
        tc_r, tc_i = tile_rows(c_r, q), tile_rows(c_i, q)
        eout = jnp.concatenate([tc_r * p_r - tc_i * p_i, -(tc_r * p_i + tc_i * p_r)], axis=1)
        eout_ref[gi] = eout.astype(BF16)

        def lag(m):
            return both(m - (q - 1) if q - 1 <= m <= 2 * q - 2 else None,
                        (q - 1) - m if m <= q - 1 else None)

        p_r, p_i = table(lag, 2 * q)
        tc_r, tc_i = tile_rows(c_r, 2 * q), tile_rows(c_i, 2 * q)
        cpt = jnp.concatenate([tc_r * p_r - tc_i * p_i, tc_r * p_i + tc_i * p_r], axis=1)
        bcat = jnp.concatenate([bb_r, -bb_i], axis=1)
        kern = _dot_nt_split(bcat, cpt)
        toep = jnp.concatenate(
            [kern[:, hc * (q - 1 - i):hc * (q - 1 - i) + CHUNK_WIDTH] for i in range(q)], axis=0)
        skip = jnp.concatenate([prm_ref[gi, 3:4, :]] * (CHUNK_WIDTH // (2 * p)), axis=1)
        toep_ref[gi] = (toep + jnp.where(row_id == col_id, skip, 0.0)).astype(BF16)

        aq_ref[gi] = jnp.concatenate(
            [pw[q][0], pw[q][1], jnp.zeros((SUBLANES - 2, 2 * p), F32)], axis=0)


def _ssm_operators(lam_re, lam_im, log_dt, b_re, b_im, c_re, c_im, d_skip):
    g, p, hc = SSM_GROUPS, SSM_STATE, SSM_CH
    lanes = lambda a: a.astype(F32).transpose(1, 0, 2).reshape(g, 2 * p)
    prm = jnp.stack(
        [lanes(lam_re), lanes(lam_im),
         jnp.repeat(log_dt.astype(F32).T, p, axis=1),
         jnp.tile(d_skip.astype(F32), (1, 2 * p // hc))]
        + [jnp.zeros((g, 2 * p), F32)] * (SUBLANES - 4), axis=1)
    bc = jnp.stack(
        [b_re.astype(F32).transpose(1, 3, 0, 2).reshape(g, hc, 2 * p),
         b_im.astype(F32).transpose(1, 3, 0, 2).reshape(g, hc, 2 * p),
         c_re.astype(F32).transpose(1, 2, 0, 3).reshape(g, hc, 2 * p),
         c_im.astype(F32).transpose(1, 2, 0, 3).reshape(g, hc, 2 * p)], axis=1)
    gs = OPS_GROUPS_PER_STEP
    mat = pl.BlockSpec((gs, CHUNK_WIDTH, CHUNK_WIDTH), lambda s: (s, 0, 0))
    mat_shape = jax.ShapeDtypeStruct((g, CHUNK_WIDTH, CHUNK_WIDTH), BF16)
    return pl.pallas_call(
        _ssm_ops_kernel,
        grid=(g // gs,),
        in_specs=[pl.BlockSpec((gs, SUBLANES, 2 * p), lambda s: (s, 0, 0)),
                  pl.BlockSpec((gs, 4, hc, 2 * p), lambda s: (s, 0, 0, 0))],
        out_specs=[mat, mat, mat, pl.BlockSpec((gs, SUBLANES, 2 * p), lambda s: (s, 0, 0))],
        out_shape=[mat_shape, mat_shape, mat_shape,
                   jax.ShapeDtypeStruct((g, SUBLANES, 2 * p), F32)],
        compiler_params=pltpu.CompilerParams(
            dimension_semantics=("parallel",), vmem_limit_bytes=VMEM_LIMIT_BYTES),
    )(prm, bc)


def _lane_roll(x, shift):
    shift %= LANES
    return jnp.concatenate([x[:, LANES - shift:], x[:, :LANES - shift]], axis=1)


def _block_transpose(v, lane_block):
    return _block_transposes([v], lane_block)[0]


def _block_transposes(groups, lane_block):
    n = len(groups[0])
    skewed = [[v[i] if i == 0 else _lane_roll(v[i], SSM_CH * i) for i in range(n)] for v in groups]
    picked = []
    for sk in skewed:
        rows = []
        for g in range(n):
            p = sk[(0 - g) % n]
            for j in range(1, n):
                p = jnp.where(lane_block == j, sk[(j - g) % n], p)
            rows.append(p)
        picked.append(rows)
    return [[p[g] if g == 0 else _lane_roll(p[g], -SSM_CH * g) for g in range(n)] for p in picked]


def _ssm_kernel(u_hbm, ein_ref, toep_ref, eout_ref, aq_ref, y_hbm,
                io_ref, ug_ref, s_ref, x_ref, in_sem, out_sem, *, batch, seq, pitch):
    q, half = SSM_CHUNK, SSM_STATE
    n_chunks = seq // q
    gpb = GROUPS_PER_BLOCK
    slab = 2 * batch
    lane_block = lax.broadcasted_iota(jnp.int32, (slab, LANES), 1) // SSM_CH
    step_id = pl.program_id(0)
    last_step = pl.num_programs(0) - 1
    slot = lax.rem(step_id, 2)

    n_slices = SSM_DMA_SLICES
    slice_rows = seq // n_slices
    slice_trips = n_chunks // (2 * n_slices)

    def column_copies(block, slot_, to_vmem, slices=range(SSM_DMA_SLICES)):
        lanes = pl.ds(pl.multiple_of(block * LANES, LANES), LANES)
        copies = []
        for k in slices:
            for b in range(batch):
                hbm = (u_hbm if to_vmem else y_hbm).at[b, pl.ds(k * slice_rows, slice_rows), lanes]
                vmem = io_ref.at[slot_, pl.ds(b * pitch + k * slice_rows, slice_rows), :]
                if to_vmem:
                    copies.append(pltpu.make_async_copy(hbm, vmem, in_sem.at[slot_, k, b]))
                else:
                    copies.append(pltpu.make_async_copy(vmem, hbm, out_sem.at[slot_, k, b]))
        return copies

    def start(copies):
        for copy in copies:
            copy.start()

    def wait(copies):
        for copy in copies:
            copy.wait()

    @pl.when(step_id == 0)
    def _():
        start(column_copies(0, 0, True))

    @pl.when(step_id < last_step)
    def _():
        @pl.when(step_id >= 1)
        def _():
            wait(column_copies(step_id - 1, 1 - slot, False))
        start(column_copies(step_id + 1, 1 - slot, True))

    buf = io_ref.at[slot]

    def gather(cp, carry):
        rows = pl.ds(pl.multiple_of(cp * slab, slab), slab)
        parts = []
        for part in range(q // gpb):
            t0 = cp * 2 * q + part * gpb
            parts.append([jnp.concatenate([buf[pl.ds(t0 + i, batch, stride=pitch), :],
                                           buf[pl.ds(t0 + q + i, batch, stride=pitch), :]],
                                          axis=0).astype(BF16) for i in range(gpb)])
        for part, w in enumerate(_block_transposes(parts, lane_block)):
            for g in range(gpb):
                ug_ref[g, rows, part * LANES:(part + 1) * LANES] = w[g]
        return carry

    for k in range(n_slices):
        wait(column_copies(step_id, slot, True, slices=(k,)))
        lax.fori_loop(k * slice_trips, (k + 1) * slice_trips, gather, 0,
                      unroll=SSM_RELAYOUT_UNROLL)

    fwd_lane = lax.broadcasted_iota(jnp.int32, (batch, 2 * half), 1) < half
    zeros = jnp.zeros((batch, half), F32)
    last_rows = pl.ds((n_chunks - 1) * batch, batch)
    ni = SSM_INTERLEAVE
    for g0 in range(0, gpb, ni):
        for gi in range(ni):
            s_ref[gi] = jnp.dot(ug_ref[g0 + gi], ein_ref[g0 + gi], preferred_element_type=F32)
            x_ref[gi, 0:batch, 0:half] = zeros
            x_ref[gi, 0:batch, 2 * half:3 * half] = zeros
            x_ref[gi, last_rows, half:2 * half] = zeros
            x_ref[gi, last_rows, 3 * half:4 * half] = zeros

        def step(k, carry):
            rf = pl.multiple_of(k * batch, batch)
            rb = pl.multiple_of((n_chunks - 1 - k) * batch, batch)
            new = []
            for gi in range(ni):
                xr, xi = carry[gi]
                sre = jnp.where(fwd_lane, s_ref[gi, pl.ds(rf, batch), 0:2 * half],
                                s_ref[gi, pl.ds(rb, batch), 0:2 * half])
                sim = jnp.where(fwd_lane, s_ref[gi, pl.ds(rf, batch), 2 * half:4 * half],
                                s_ref[gi, pl.ds(rb, batch), 2 * half:4 * half])
                ar = aq_ref[g0 + gi, 0:1, :]
                ai = aq_ref[g0 + gi, 1:2, :]
                nr = ar * xr - ai * xi + sre
                nim = ar * xi + ai * xr + sim
                x_ref[gi, pl.ds(rf + batch, batch), 0:half] = nr[:, 0:half]
                x_ref[gi, pl.ds(rf + batch, batch), 2 * half:3 * half] = nim[:, 0:half]
                x_ref[gi, pl.ds(rb - batch, batch), half:2 * half] = nr[:, half:2 * half]
                x_ref[gi, pl.ds(rb - batch, batch), 3 * half:4 * half] = nim[:, half:2 * half]
                new.append((nr, nim))
            return tuple(new)

        init = tuple((jnp.zeros((batch, 2 * half), F32), jnp.zeros((batch, 2 * half), F32))
                     for _ in range(ni))
        lax.fori_loop(0, n_chunks - 1, step, init)

        for gi in range(ni):
            g = g0 + gi
            y = jnp.dot(ug_ref[g], toep_ref[g], preferred_element_type=F32)
            y += lax.dot_general(x_ref[gi].astype(BF16), eout_ref[g], NT_DIMS,
                                 preferred_element_type=F32)
            ug_ref[g] = y.astype(BF16)

    def scatter(cp, carry):
        rows = pl.ds(pl.multiple_of(cp * slab, slab), slab)
        parts = [[ug_ref[g, rows, part * LANES:(part + 1) * LANES] for g in range(gpb)]
                 for part in range(q // gpb)]
        for part, w in enumerate(_block_transposes(parts, lane_block)):
            t0 = cp * 2 * q + part * gpb
            for j in range(gpb):
                wj = w[j].astype(F32)
                buf[pl.ds(t0 + j, batch, stride=pitch), :] = wj[0:batch]
                buf[pl.ds(t0 + q + j, batch, stride=pitch), :] = wj[batch:slab]
        return carry

    for k in range(n_slices):
        lax.fori_loop(k * slice_trips, (k + 1) * slice_trips, scatter, 0,
                      unroll=SSM_RELAYOUT_UNROLL)
        start(column_copies(step_id, slot, False, slices=(k,)))

    @pl.when(step_id == last_step)
    def _():
        @pl.when(step_id >= 1)
        def _():
            wait(column_copies(step_id - 1, 1 - slot, False))
        wait(column_copies(step_id, slot, False))


def _padded_seq(seq):
    return seq + SUBLANES if seq % (2 * SUBLANES) == 0 else seq


def _ssm(u, lam_re, lam_im, log_dt, b_re, b_im, c_re, c_im, d_skip):
    batch, seq, _ = u.shape
    pitch = _padded_seq(seq)
    rows = batch * seq // SSM_CHUNK
    ein, toep, eout, a_q = _ssm_operators(lam_re, lam_im, log_dt, b_re, b_im, c_re, c_im, d_skip)
    gpb = GROUPS_PER_BLOCK
    mat = pl.BlockSpec((gpb, CHUNK_WIDTH, CHUNK_WIDTH), lambda s: (s, 0, 0))
    hbm = pl.BlockSpec(memory_space=pl.ANY)
    return pl.pallas_call(
        functools.partial(_ssm_kernel, batch=batch, seq=seq, pitch=pitch),
        grid=(SSM_GROUPS // gpb,),
        in_specs=[hbm, mat, mat, mat,
                  pl.BlockSpec((gpb, SUBLANES, 2 * SSM_STATE), lambda s: (s, 0, 0))],
        out_specs=hbm,
        out_shape=jax.ShapeDtypeStruct((batch, seq, SSM_WIDTH), F32),
        scratch_shapes=[pltpu.VMEM((2, batch * pitch, LANES), F32),
                        pltpu.VMEM((gpb, rows, CHUNK_WIDTH), BF16),
                        pltpu.VMEM((SSM_INTERLEAVE, rows, 4 * SSM_STATE), F32),
                        pltpu.VMEM((SSM_INTERLEAVE, rows, 4 * SSM_STATE), F32),
                        pltpu.SemaphoreType.DMA((2, SSM_DMA_SLICES, batch)),
                        pltpu.SemaphoreType.DMA((2, SSM_DMA_SLICES, batch))],
        compiler_params=pltpu.CompilerParams(
            dimension_semantics=("arbitrary",), vmem_limit_bytes=VMEM_LIMIT_BYTES),
    )(u, ein, toep, eout, a_q)


def kernel(x, norm_ffn1, ffn1_w_gate, ffn1_w_up, ffn1_w_down, norm_mix, w_in, attn_sinks,
           ssm_lambda_re, ssm_lambda_im, ssm_log_dt, ssm_b_re, ssm_b_im, ssm_c_re, ssm_c_im,
           ssm_d, ssm_glu_w, ssm_glu_b, attn_out_norm, ssm_out_norm, w_out,
           norm_ffn2, ffn2_w_gate, ffn2_w_up, ffn2_w_down, final_norm):
    b, seq, d = x.shape
    depth = norm_ffn1.shape[0]
    assert d == D_MODEL and seq % BLOCK == 0 and (b * seq) % FFN_TOKEN_TILE == 0
    assert b == SUBLANES and seq % (2 * SSM_CHUNK * SSM_RELAYOUT_UNROLL * SSM_DMA_SLICES) == 0
    h = x.reshape(b * seq, d).astype(F32)
    for l in range(depth):
        h, q, k, v, u = _ffn(h, norm_ffn1[l], ffn1_w_gate[l], ffn1_w_up[l], ffn1_w_down[l],
                             final_norm, final_norm=False, proj=(norm_mix[l], w_in[l]))
        attn = _attention(q.reshape(b, seq, -1), k.reshape(b, seq, -1), v.reshape(b, seq, -1),
                          attn_sinks[l])
        ssm_pre = _ssm(u.reshape(b, seq, -1), ssm_lambda_re[l], ssm_lambda_im[l], ssm_log_dt[l],
                       ssm_b_re[l], ssm_b_im[l], ssm_c_re[l], ssm_c_im[l], ssm_d[l])
        mix = (attn.reshape(b * seq, -1), ssm_pre.reshape(b * seq, -1), ssm_glu_w[l],
               ssm_glu_b[l], attn_out_norm[l], ssm_out_norm[l], w_out[l])
        h, = _ffn(h, norm_ffn2[l], ffn2_w_gate[l], ffn2_w_up[l], ffn2_w_down[l],
                  final_norm, final_norm=(l == depth - 1), mix=mix)
    return h.reshape(b, seq, d).astype(x.dtype)
```

```python
import functools
import math

import jax
import jax.numpy as jnp
from jax import lax
from jax.experimental import pallas as pl
from jax.experimental.pallas import tpu as pltpu

F32 = jnp.float32
BF16 = jnp.bfloat16

D_MODEL = 1024
ATTN_HEADS = 8
ATTN_KV_HEADS = 2
Q_PER_KV = ATTN_HEADS // ATTN_KV_HEADS
HEAD_DIM = 64
ATTN_WIDTH = ATTN_HEADS * HEAD_DIM
KV_WIDTH = ATTN_KV_HEADS * HEAD_DIM
WINDOW = 128
BLOCK = 128
SSM_CH = 16
SSM_WIDTH = D_MODEL - ATTN_WIDTH
SSM_GROUPS = SSM_WIDTH // SSM_CH
SSM_STATE = 64
IN_WIDTH = ATTN_WIDTH + 2 * KV_WIDTH + SSM_WIDTH
D_FF = 2816
EPS = 1e-6
NEG_INF = -1e30
LAMBDA_RE_MAX = -1e-4
LOG2_E = math.log2(math.e)
QUERY_SCALE = HEAD_DIM ** -0.5 * LOG2_E

LANES = 128
SUBLANES = 8
VMEM_LIMIT_BYTES = 56 * 1024 * 1024

FFN_TOKEN_TILE = 512
FFN_SUBTILES = 2
FF_TILE = 256
ATTN_BLOCKS_PER_STEP = 8
SSM_CHUNK = 16
CHUNK_WIDTH = SSM_CHUNK * SSM_CH
GROUPS_PER_BLOCK = LANES // SSM_CH
SSM_INTERLEAVE = 8
SSM_RELAYOUT_UNROLL = 4
SSM_DMA_SLICES = 8
OPS_GROUPS_PER_STEP = 8

NT_DIMS = (((1,), (1,)), ((), ()))
TN_DIMS = (((0,), (0,)), ((), ()))


def _rms(x):
    return x * lax.rsqrt(jnp.mean(x * x, axis=-1, keepdims=True) + EPS)


def _mixed_update(rows, attn_ref, ssm_ref, gw_ref, gb_ref, ga_ref, gs_ref, wo_ref):
    y = jax.nn.gelu(ssm_ref[rows, :])
    z = jnp.dot(y.astype(BF16), gw_ref[...].astype(BF16), preferred_element_type=F32) + gb_ref[...]
    s = y * jax.nn.sigmoid(z)
    sn = _rms(s) * gs_ref[...]
    an = _rms(attn_ref[rows, :].astype(F32)) * ga_ref[...]
    mixed = jnp.concatenate([an, sn], axis=-1).astype(BF16)
    return jnp.dot(mixed, wo_ref[...].astype(BF16), preferred_element_type=F32)


def _mixer_inputs(y, rows, gain_ref, w_ref, q_ref, k_ref, v_ref, u_ref):
    hn = (_rms(y) * gain_ref[...]).astype(BF16)
    proj = jnp.dot(hn, w_ref[...].astype(BF16), preferred_element_type=F32)
    q_ref[rows, :] = (proj[:, :ATTN_WIDTH] * QUERY_SCALE).astype(BF16)
    k_ref[rows, :] = proj[:, ATTN_WIDTH:ATTN_WIDTH + KV_WIDTH].astype(BF16)
    v_ref[rows, :] = proj[:, ATTN_WIDTH + KV_WIDTH:ATTN_WIDTH + 2 * KV_WIDTH].astype(BF16)
    u_ref[rows, :] = proj[:, ATTN_WIDTH + 2 * KV_WIDTH:]


def _ffn_body(read_x, gain_ref, fgain_ref, wg_hbm, wu_hbm, wd_hbm, o_ref,
              wg_ref, wu_ref, wd_ref, act_ref, gu_stage, d_stage, sem, *, final_norm,
              epilogue=None):
    nj = D_FF // FF_TILE

    def weight_copies(j, slot):
        span = pl.ds(j * FF_TILE, FF_TILE)
        return (pltpu.make_async_copy(wg_hbm.at[:, span], gu_stage.at[0, slot], sem.at[0, slot]),
                pltpu.make_async_copy(wu_hbm.at[:, span], gu_stage.at[1, slot], sem.at[1, slot]),
                pltpu.make_async_copy(wd_hbm.at[span, :], d_stage.at[slot], sem.at[2, slot]))

    def step(stage_weights):
        if stage_weights:
            for copy in weight_copies(0, 0):
                copy.start()
        sub = o_ref.shape[0] // FFN_SUBTILES
        row_groups = [slice(h * sub, (h + 1) * sub) for h in range(FFN_SUBTILES)]
        xs = [read_x(rows) for rows in row_groups]
        hns = [(_rms(x) * gain_ref[...]).astype(BF16) for x in xs]
        for j in range(nj):
            cols = slice(j * FF_TILE, (j + 1) * FF_TILE)
            if stage_weights:
                slot = j % 2
                if j + 1 < nj:
                    for copy in weight_copies(j + 1, 1 - slot):
                        copy.start()
                for copy in weight_copies(j, slot):
                    copy.wait()
                wg_ref[:, cols] = gu_stage[0, slot].astype(BF16)
                wu_ref[:, cols] = gu_stage[1, slot].astype(BF16)
                wd_ref[cols, :] = d_stage[slot].astype(BF16)
            for rows, hn in zip(row_groups, hns):
                g = jnp.dot(hn, wg_ref[:, cols], preferred_element_type=F32)
                u = jnp.dot(hn, wu_ref[:, cols], preferred_element_type=F32)
                act_ref[rows, cols] = (g * jax.nn.sigmoid(g) * u).astype(BF16)
        for rows, x in zip(row_groups, xs):
            y = x + 0.5 * jnp.dot(act_ref[rows, :], wd_ref[...], preferred_element_type=F32)
            if final_norm:
                y = _rms(y) * fgain_ref[...]
            o_ref[rows, :] = y
            if epilogue is not None:
                epilogue(y, rows)

    pl.when(pl.program_id(0) == 0)(functools.partial(step, True))
    pl.when(pl.program_id(0) != 0)(functools.partial(step, False))


def _ffn_proj_kernel(x_ref, mgain_ref, win_ref, gain_ref, fgain_ref, wg_hbm, wu_hbm, wd_hbm,
                     o_ref, q_ref, k_ref, v_ref, u_ref, *scratch, final_norm):
    emit = lambda y, rows: _mixer_inputs(y, rows, mgain_ref, win_ref, q_ref, k_ref, v_ref, u_ref)
    _ffn_body(lambda rows: x_ref[rows, :], gain_ref, fgain_ref, wg_hbm, wu_hbm, wd_hbm, o_ref,
              *scratch, final_norm=final_norm, epilogue=emit)


def _mix_ffn_kernel(x_ref, attn_ref, ssm_ref, gw_ref, gb_ref, ga_ref, gs_ref, wo_ref, *ffn_refs,
                    final_norm):
    read_x = lambda rows: x_ref[rows, :] + _mixed_update(rows, attn_ref, ssm_ref, gw_ref, gb_ref,
                                                         ga_ref, gs_ref, wo_ref)
    _ffn_body(read_x, *ffn_refs, final_norm=final_norm)


def _ffn(x, gain, w_gate, w_up, w_down, final_gain, final_norm, mix=None, proj=None):
    assert (mix is None) != (proj is None)
    t = x.shape[0]
    tile = FFN_TOKEN_TILE
    vec = lambda width: pl.BlockSpec((1, width), lambda i: (0, 0))
    resident = lambda r, c: pl.BlockSpec((r, c), lambda i: (0, 0), pipeline_mode=pl.Buffered(1))
    row = lambda width: pl.BlockSpec((tile, width), lambda i: (i, 0))
    hbm = pl.BlockSpec(memory_space=pl.ANY)
    in_specs, args = [row(D_MODEL)], [x]
    out_specs, out_shape = [row(D_MODEL)], [jax.ShapeDtypeStruct((t, D_MODEL), F32)]
    if mix is not None:
        attn, ssm_pre, glu_w, glu_b, attn_gain, ssm_gain, w_out = mix
        body = _mix_ffn_kernel
        in_specs += [row(ATTN_WIDTH), row(SSM_WIDTH),
                     resident(SSM_WIDTH, SSM_WIDTH), vec(SSM_WIDTH), vec(ATTN_WIDTH),
                     vec(SSM_WIDTH), resident(D_MODEL, D_MODEL)]
        args += [attn, ssm_pre, glu_w, glu_b.reshape(1, -1).astype(F32),
                 attn_gain.reshape(1, -1).astype(F32), ssm_gain.reshape(1, -1).astype(F32), w_out]
    if proj is not None:
        mixer_gain, w_in = proj
        body = _ffn_proj_kernel
        in_specs += [vec(D_MODEL), resident(D_MODEL, IN_WIDTH)]
        args += [mixer_gain.reshape(1, D_MODEL), w_in]
        for width, dtype in ((ATTN_WIDTH, BF16), (KV_WIDTH, BF16), (KV_WIDTH, BF16),
                             (SSM_WIDTH, F32)):
            out_specs.append(row(width))
            out_shape.append(jax.ShapeDtypeStruct((t, width), dtype))
    in_specs += [vec(D_MODEL), vec(D_MODEL), hbm, hbm, hbm]
    args += [gain.reshape(1, D_MODEL), final_gain.reshape(1, D_MODEL), w_gate, w_up, w_down]
    return pl.pallas_call(
        functools.partial(body, final_norm=final_norm),
        grid=(t // tile,),
        in_specs=in_specs,
        out_specs=out_specs,
        out_shape=out_shape,
        scratch_shapes=[pltpu.VMEM((D_MODEL, D_FF), BF16),
                        pltpu.VMEM((D_MODEL, D_FF), BF16),
                        pltpu.VMEM((D_FF, D_MODEL), BF16),
                        pltpu.VMEM((tile, D_FF), BF16),
                        pltpu.VMEM((2, 2, D_MODEL, FF_TILE), F32),
                        pltpu.VMEM((2, FF_TILE, D_MODEL), F32),
                        pltpu.SemaphoreType.DMA((3, 2))],
        compiler_params=pltpu.CompilerParams(
            dimension_semantics=("arbitrary",), vmem_limit_bytes=VMEM_LIMIT_BYTES),
    )(*args)


def _attn_kernel(sink_ref, q_ref, kp_ref, kc_ref, kn_ref, vp_ref, vc_ref, vn_ref, o_ref,
                 bias_ref, k_ref, v_ref, s_ref):
    n = pl.program_id(1)
    last = pl.num_programs(1) - 1
    nblk = ATTN_BLOCKS_PER_STEP

    @pl.when((pl.program_id(0) == 0) & (n == 0))
    def _():
        kj = lax.broadcasted_iota(jnp.int32, (3 * BLOCK, BLOCK), 0)
        qi = lax.broadcasted_iota(jnp.int32, (3 * BLOCK, BLOCK), 1)
        rel = jnp.abs(kj - BLOCK - qi)
        dist = rel.astype(F32)
        inside = rel <= WINDOW
        has_prev = kj >= BLOCK
        has_next = kj < 2 * BLOCK
        for variant, ok in enumerate((inside & has_prev, inside, inside & has_next)):
            for h in range(ATTN_HEADS):
                slope = float(2.0 ** (-8.0 * (h + 1) / ATTN_HEADS))
                bias_ref[variant, h] = jnp.where(ok, (-slope * LOG2_E) * dist, NEG_INF)
        for kh in range(ATTN_KV_HEADS):
            v_ref[:, (2 * kh + 1) * HEAD_DIM:(2 * kh + 2) * HEAD_DIM] = jnp.ones(
                (v_ref.shape[0], HEAD_DIM), BF16)

    spans = ((slice(0, BLOCK), kp_ref, vp_ref),
             (slice(BLOCK, (nblk + 1) * BLOCK), kc_ref, vc_ref),
             (slice((nblk + 1) * BLOCK, (nblk + 2) * BLOCK), kn_ref, vn_ref))
    for rows, k_in, v_in in spans:
        k_ref[rows, :] = k_in[...]
        for kh in range(ATTN_KV_HEADS):
            v_ref[rows, 2 * kh * HEAD_DIM:(2 * kh + 1) * HEAD_DIM] = (
                v_in[:, kh * HEAD_DIM:(kh + 1) * HEAD_DIM])

    def scores(j, kh):
        kcat = k_ref[j * BLOCK:(j + 3) * BLOCK, kh * HEAD_DIM:(kh + 1) * HEAD_DIM]
        heads = [kh * Q_PER_KV + g for g in range(Q_PER_KV)]
        qs = jnp.concatenate(
            [q_ref[j * BLOCK:(j + 1) * BLOCK, h * HEAD_DIM:(h + 1) * HEAD_DIM] for h in heads],
            axis=0)
        s_ref[j, kh] = lax.dot_general(kcat, qs, NT_DIMS,
                                       preferred_element_type=F32)

    for kh in range(ATTN_KV_HEADS):
        scores(0, kh)
    for j in range(nblk):
        variant = 1
        if j == 0:
            variant = jnp.where(n == 0, 0, variant)
        if j == nblk - 1:
            variant = jnp.where(n == last, 2, variant)
        outs = []
        for kh in range(ATTN_KV_HEADS):
            if j + 1 < nblk:
                scores(j + 1, kh)
            v_ones = v_ref[j * BLOCK:(j + 3) * BLOCK, 2 * kh * HEAD_DIM:(2 * kh + 2) * HEAD_DIM]
            for g in range(Q_PER_KV):
                h = kh * Q_PER_KV + g
                s = s_ref[j, kh, :, g * BLOCK:(g + 1) * BLOCK] + bias_ref[variant, h]
                sink = sink_ref[h] * LOG2_E
                m = jnp.maximum(jnp.max(s, axis=0, keepdims=True), sink)
                e = jnp.exp2(s - m).astype(BF16)
                pv = lax.dot_general(v_ones, e, TN_DIMS, preferred_element_type=F32)
                den = pv[HEAD_DIM:HEAD_DIM + 1] + jnp.exp2(sink - m)
                outs.append(pv[:HEAD_DIM] / den)
        o_ref[j * BLOCK:(j + 1) * BLOCK, :] = jnp.concatenate(outs, axis=0).T.astype(o_ref.dtype)


def _attention(q, k, v, sinks):
    b, seq, _ = q.shape
    nblk = ATTN_BLOCKS_PER_STEP
    steps = seq // (nblk * BLOCK)
    nb = seq // BLOCK
    assert seq % (nblk * BLOCK) == 0 and nb >= 2
    edge = lambda f: pl.BlockSpec((None, BLOCK, KV_WIDTH), f)
    body = pl.BlockSpec((None, nblk * BLOCK, KV_WIDTH), lambda bi, n: (bi, n, 0))
    prev = lambda bi, n: (bi, jnp.maximum(n * nblk - 1, 0), 0)
    nxt = lambda bi, n: (bi, jnp.minimum((n + 1) * nblk, nb - 1), 0)
    rows = pl.BlockSpec((None, nblk * BLOCK, ATTN_WIDTH), lambda bi, n: (bi, n, 0))
    return pl.pallas_call(
        _attn_kernel,
        grid=(b, steps),
        in_specs=[pl.BlockSpec(memory_space=pltpu.SMEM), rows,
                  edge(prev), body, edge(nxt), edge(prev), body, edge(nxt)],
        out_specs=rows,
        out_shape=jax.ShapeDtypeStruct((b, seq, ATTN_WIDTH), BF16),
        scratch_shapes=[pltpu.VMEM((3, ATTN_HEADS, 3 * BLOCK, BLOCK), F32),
                        pltpu.VMEM(((nblk + 2) * BLOCK, KV_WIDTH), BF16),
                        pltpu.VMEM(((nblk + 2) * BLOCK, 2 * KV_WIDTH), BF16),
                        pltpu.VMEM((nblk, ATTN_KV_HEADS, 3 * BLOCK, Q_PER_KV * BLOCK), F32)],
        compiler_params=pltpu.CompilerParams(
            dimension_semantics=("arbitrary", "arbitrary"), vmem_limit_bytes=VMEM_LIMIT_BYTES),
    )(sinks.astype(F32), q, k, k, k, v, v, v)


def _dot_nt_split(a, b):
    a_hi, b_hi = a.astype(BF16), b.astype(BF16)
    a_lo = (a - a_hi.astype(F32)).astype(BF16)
    b_lo = (b - b_hi.astype(F32)).astype(BF16)
    dot = functools.partial(lax.dot_general, dimension_numbers=NT_DIMS,
                            preferred_element_type=F32)
    return dot(a_hi, b_hi) + dot(a_hi, b_lo) + dot(a_lo, b_hi)


def _ssm_ops_kernel(prm_ref, bc_ref, ein_ref, toep_ref, eout_ref, aq_ref):
    q, hc, p = SSM_CHUNK, SSM_CH, SSM_STATE
    fwd = lax.broadcasted_iota(jnp.int32, (1, 2 * p), 1) < p
    zero_row = jnp.zeros((1, 2 * p), F32)
    row_id = lax.broadcasted_iota(jnp.int32, (CHUNK_WIDTH, CHUNK_WIDTH), 0)
    col_id = lax.broadcasted_iota(jnp.int32, (CHUNK_WIDTH, CHUNK_WIDTH), 1)

    def table(select, n):
        picks = [select(m) for m in range(n)]
        re = jnp.concatenate([jnp.broadcast_to(r, (hc, 2 * p)) for r, _ in picks], axis=0)
        im = jnp.concatenate([jnp.broadcast_to(i, (hc, 2 * p)) for _, i in picks], axis=0)
        return re, im

    def tile_rows(x, n):
        return jnp.concatenate([x] * n, axis=0)

    for gi in range(prm_ref.shape[0]):
        lr = jnp.minimum(prm_ref[gi, 0:1, :], LAMBDA_RE_MAX)
        li = prm_ref[gi, 1:2, :]
        dt = jnp.exp(prm_ref[gi, 2:3, :])
        mag = jnp.exp(lr * dt)
        a_r = mag * jnp.cos(li * dt)
        a_i = mag * jnp.sin(li * dt)
        den = lr * lr + li * li
        coef_r = ((a_r - 1.0) * lr + a_i * li) / den
        coef_i = (a_i * lr - (a_r - 1.0) * li) / den
        b_r, b_i = bc_ref[gi, 0], bc_ref[gi, 1]
        c_r, c_i = bc_ref[gi, 2], bc_ref[gi, 3]
        bb_r = coef_r * b_r - coef_i * b_i
        bb_i = coef_r * b_i + coef_i * b_r

        pw = [(jnp.ones((1, 2 * p), F32), zero_row)]
        for _ in range(q):
            r, i = pw[-1]
            pw.append((r * a_r - i * a_i, r * a_i + i * a_r))

        def both(f_idx, b_idx):
            fr, fi = pw[f_idx] if f_idx is not None else (zero_row, zero_row)
            br, bi = pw[b_idx] if b_idx is not None else (zero_row, zero_row)
            return jnp.where(fwd, fr, br), jnp.where(fwd, fi, bi)

        p_r, p_i = table(lambda i: both(q - 1 - i, i), q)
        tb_r, tb_i = tile_rows(bb_r, q), tile_rows(bb_i, q)
        ein = jnp.concatenate([tb_r * p_r - tb_i * p_i, tb_r * p_i + tb_i * p_r], axis=1)
        ein_ref[gi] = ein.astype(BF16)

        p_r, p_i = table(lambda j: both(j + 1, q - j), q)
        tc_r, tc_i = tile_rows(c_r, q), tile_rows(c_i, q)
        eout = jnp.concatenate([tc_r * p_r - tc_i * p_i, -(tc_r * p_i + tc_i * p_r)], axis=1)
        eout_ref[gi] = eout.astype(BF16)

        def lag(m):
            return both(m - (q - 1) if q - 1 <= m <= 2 * q - 2 else None,
                        (q - 1) - m if m <= q - 1 else None)

        p_r, p_i = table(lag, 2 * q)
        tc_r, tc_i = tile_rows(c_r, 2 * q), tile_rows(c_i, 2 * q)
        cpt = jnp.concatenate([tc_r * p_r - tc_i * p_i, tc_r * p_i + tc_i * p_r], axis=1)
        bcat = jnp.concatenate([bb_r, -bb_i], axis=1)
        kern = _dot_nt_split(bcat, cpt)
        toep = jnp.concatenate(
            [kern[:, hc * (q - 1 - i):hc * (q - 1 - i) + CHUNK_WIDTH] for i in range(q)], axis=0)
        skip = jnp.concatenate([prm_ref[gi, 3:4, :]] * (CHUNK_WIDTH // (2 * p)), axis=1)
        toep_ref[gi] = (toep + jnp.where(row_id == col_id, skip, 0.0)).astype(BF16)

        aq_ref[gi] = jnp.concatenate(
            [pw[q][0], pw[q][1], jnp.zeros((SUBLANES - 2, 2 * p), F32)], axis=0)


def _ssm_operators(lam_re, lam_im, log_dt, b_re, b_im, c_re, c_im, d_skip):
    g, p, hc = SSM_GROUPS, SSM_STATE, SSM_CH
    lanes = lambda a: a.astype(F32).transpose(1, 0, 2).reshape(g, 2 * p)
    prm = jnp.stack(
        [lanes(lam_re), lanes(lam_im),
         jnp.repeat(log_dt.astype(F32).T, p, axis=1),
         jnp.tile(d_skip.astype(F32), (1, 2 * p // hc))]
        + [jnp.zeros((g, 2 * p), F32)] * (SUBLANES - 4), axis=1)
    bc = jnp.stack(
        [b_re.astype(F32).transpose(1, 3, 0, 2).reshape(g, hc, 2 * p),
         b_im.astype(F32).transpose(1, 3, 0, 2).reshape(g, hc, 2 * p),
         c_re.astype(F32).transpose(1, 2, 0, 3).reshape(g, hc, 2 * p),
         c_im.astype(F32).transpose(1, 2, 0, 3).reshape(g, hc, 2 * p)], axis=1)
    gs = OPS_GROUPS_PER_STEP
    mat = pl.BlockSpec((gs, CHUNK_WIDTH, CHUNK_WIDTH), lambda s: (s, 0, 0))
    mat_shape = jax.ShapeDtypeStruct((g, CHUNK_WIDTH, CHUNK_WIDTH), BF16)
    return pl.pallas_call(
        _ssm_ops_kernel,
        grid=(g // gs,),
        in_specs=[pl.BlockSpec((gs, SUBLANES, 2 * p), lambda s: (s, 0, 0)),
                  pl.BlockSpec((gs, 4, hc, 2 * p), lambda s: (s, 0, 0, 0))],
        out_specs=[mat, mat, mat, pl.BlockSpec((gs, SUBLANES, 2 * p), lambda s: (s, 0, 0))],
        out_shape=[mat_shape, mat_shape, mat_shape,
                   jax.ShapeDtypeStruct((g, SUBLANES, 2 * p), F32)],
        compiler_params=pltpu.CompilerParams(
            dimension_semantics=("parallel",), vmem_limit_bytes=VMEM_LIMIT_BYTES),
    )(prm, bc)


def _lane_roll(x, shift):
    shift %= LANES
    return jnp.concatenate([x[:, LANES - shift:], x[:, :LANES - shift]], axis=1)


def _block_transpose(v, lane_block):
    return _block_transposes([v], lane_block)[0]


def _block_transposes(groups, lane_block):
    n = len(groups[0])
    skewed = [[v[i] if i == 0 else _lane_roll(v[i], SSM_CH * i) for i in range(n)] for v in groups]
    picked = []
    for sk in skewed:
        rows = []
        for g in range(n):
            p = sk[(0 - g) % n]
            for j in range(1, n):
                p = jnp.where(lane_block == j, sk[(j - g) % n], p)
            rows.append(p)
        picked.append(rows)
    return [[p[g] if g == 0 else _lane_roll(p[g], -SSM_CH * g) for g in range(n)] for p in picked]


def _ssm_kernel(u_hbm, ein_ref, toep_ref, eout_ref, aq_ref, y_hbm,
                io_ref, ug_ref, s_ref, x_ref, in_sem, out_sem, *, batch, seq, pitch):
    q, half = SSM_CHUNK, SSM_STATE
    n_chunks = seq // q
    gpb = GROUPS_PER_BLOCK
    slab = 2 * batch
    lane_block = lax.broadcasted_iota(jnp.int32, (slab, LANES), 1) // SSM_CH
    step_id = pl.program_id(0)
    last_step = pl.num_programs(0) - 1
    slot = lax.rem(step_id, 2)

    n_slices = SSM_DMA_SLICES
    slice_rows = seq // n_slices
    slice_trips = n_chunks // (2 * n_slices)

    def column_copies(block, slot_, to_vmem, slices=range(SSM_DMA_SLICES)):
        lanes = pl.ds(pl.multiple_of(block * LANES, LANES), LANES)
        copies = []
        for k in slices:
            for b in range(batch):
                hbm = (u_hbm if to_vmem else y_hbm).at[b, pl.ds(k * slice_rows, slice_rows), lanes]
                vmem = io_ref.at[slot_, pl.ds(b * pitch + k * slice_rows, slice_rows), :]
                if to_vmem:
                    copies.append(pltpu.make_async_copy(hbm, vmem, in_sem.at[slot_, k, b]))
                else:
                    copies.append(pltpu.make_async_copy(vmem, hbm, out_sem.at[slot_, k, b]))
        return copies

    def start(copies):
        for copy in copies:
            copy.start()

    def wait(copies):
        for copy in copies:
            copy.wait()

    @pl.when(step_id == 0)
    def _():
        start(column_copies(0, 0, True))

    @pl.when(step_id < last_step)
    def _():
        @pl.when(step_id >= 1)
        def _():
            wait(column_copies(step_id - 1, 1 - slot, False))
        start(column_copies(step_id + 1, 1 - slot, True))

    buf = io_ref.at[slot]

    def gather(cp, carry):
        rows = pl.ds(pl.multiple_of(cp * slab, slab), slab)
        parts = []
        for part in range(q // gpb):
            t0 = cp * 2 * q + part * gpb
            parts.append([jnp.concatenate([buf[pl.ds(t0 + i, batch, stride=pitch), :],
                                           buf[pl.ds(t0 + q + i, batch, stride=pitch), :]],
                                          axis=0).astype(BF16) for i in range(gpb)])
        for part, w in enumerate(_block_transposes(parts, lane_block)):
            for g in range(gpb):
                ug_ref[g, rows, part * LANES:(part + 1) * LANES] = w[g]
        return carry

    for k in range(n_slices):
        wait(column_copies(step_id, slot, True, slices=(k,)))
        lax.fori_loop(k * slice_trips, (k + 1) * slice_trips, gather, 0,
                      unroll=SSM_RELAYOUT_UNROLL)

    fwd_lane = lax.broadcasted_iota(jnp.int32, (batch, 2 * half), 1) < half
    zeros = jnp.zeros((batch, half), F32)
    last_rows = pl.ds((n_chunks - 1) * batch, batch)
    ni = SSM_INTERLEAVE
    for g0 in range(0, gpb, ni):
        for gi in range(ni):
            s_ref[gi] = jnp.dot(ug_ref[g0 + gi], ein_ref[g0 + gi], preferred_element_type=F32)
            x_ref[gi, 0:batch, 0:half] = zeros
            x_ref[gi, 0:batch, 2 * half:3 * half] = zeros
            x_ref[gi, last_rows, half:2 * half] = zeros
            x_ref[gi, last_rows, 3 * half:4 * half] = zeros

        def step(k, carry):
            rf = pl.multiple_of(k * batch, batch)
            rb = pl.multiple_of((n_chunks - 1 - k) * batch, batch)
            new = []
            for gi in range(ni):
                xr, xi = carry[gi]
                sre = jnp.where(fwd_lane, s_ref[gi, pl.ds(rf, batch), 0:2 * half],
                                s_ref[gi, pl.ds(rb, batch), 0:2 * half])
                sim = jnp.where(fwd_lane, s_ref[gi, pl.ds(rf, batch), 2 * half:4 * half],
                                s_ref[gi, pl.ds(rb, batch), 2 * half:4 * half])
                ar = aq_ref[g0 + gi, 0:1, :]
                ai = aq_ref[g0 + gi, 1:2, :]
                nr = ar * xr - ai * xi + sre
                nim = ar * xi + ai * xr + sim
                x_ref[gi, pl.ds(rf + batch, batch), 0:half] = nr[:, 0:half]
                x_ref[gi, pl.ds(rf + batch, batch), 2 * half:3 * half] = nim[:, 0:half]
                x_ref[gi, pl.ds(rb - batch, batch), half:2 * half] = nr[:, half:2 * half]
                x_ref[gi, pl.ds(rb - batch, batch), 3 * half:4 * half] = nim[:, half:2 * half]
                new.append((nr, nim))
            return tuple(new)

        init = tuple((jnp.zeros((batch, 2 * half), F32), jnp.zeros((batch, 2 * half), F32))
                     for _ in range(ni))
        lax.fori_loop(0, n_chunks - 1, step, init)

        for gi in range(ni):
            g = g0 + gi
            y = jnp.dot(ug_ref[g], toep_ref[g], preferred_element_type=F32)
            y += lax.dot_general(x_ref[gi].astype(BF16), eout_ref[g], NT_DIMS,
                                 preferred_element_type=F32)
            ug_ref[g] = y.astype(BF16)

    def scatter(cp, carry):
        rows = pl.ds(pl.multiple_of(cp * slab, slab), slab)
        parts = [[ug_ref[g, rows, part * LANES:(part + 1) * LANES] for g in range(gpb)]
                 for part in range(q // gpb)]
        for part, w in enumerate(_block_transposes(parts, lane_block)):
            t0 = cp * 2 * q + part * gpb
            for j in range(gpb):
                wj = w[j].astype(F32)
                buf[pl.ds(t0 + j, batch, stride=pitch), :] = wj[0:batch]
                buf[pl.ds(t0 + q + j, batch, stride=pitch), :] = wj[batch:slab]
        return carry

    for k in range(n_slices):
        lax.fori_loop(k * slice_trips, (k + 1) * slice_trips, scatter, 0,
                      unroll=SSM_RELAYOUT_UNROLL)
        start(column_copies(step_id, slot, False, slices=(k,)))

    @pl.when(step_id == last_step)
    def _():
        @pl.when(step_id >= 1)
        def _():
            wait(column_copies(step_id - 1, 1 - slot, False))
        wait(column_copies(step_id, slot, False))


def _padded_seq(seq):
    return seq + SUBLANES if seq % (2 * SUBLANES) == 0 else seq


def _ssm(u, lam_re, lam_im, log_dt, b_re, b_im, c_re, c_im, d_skip):
    batch, seq, _ = u.shape
    pitch = _padded_seq(seq)
    rows = batch * seq // SSM_CHUNK
    ein, toep, eout, a_q = _ssm_operators(lam_re, lam_im, log_dt, b_re, b_im, c_re, c_im, d_skip)
    gpb = GROUPS_PER_BLOCK
    mat = pl.BlockSpec((gpb, CHUNK_WIDTH, CHUNK_WIDTH), lambda s: (s, 0, 0))
    hbm = pl.BlockSpec(memory_space=pl.ANY)
    return pl.pallas_call(
        functools.partial(_ssm_kernel, batch=batch, seq=seq, pitch=pitch),
        grid=(SSM_GROUPS // gpb,),
        in_specs=[hbm, mat, mat, mat,
                  pl.BlockSpec((gpb, SUBLANES, 2 * SSM_STATE), lambda s: (s, 0, 0))],
        out_specs=hbm,
        out_shape=jax.ShapeDtypeStruct((batch, seq, SSM_WIDTH), F32),
        scratch_shapes=[pltpu.VMEM((2, batch * pitch, LANES), F32),
                        pltpu.VMEM((gpb, rows, CHUNK_WIDTH), BF16),
                        pltpu.VMEM((SSM_INTERLEAVE, rows, 4 * SSM_STATE), F32),
                        pltpu.VMEM((SSM_INTERLEAVE, rows, 4 * SSM_STATE), F32),
                        pltpu.SemaphoreType.DMA((2, SSM_DMA_SLICES, batch)),
                        pltpu.SemaphoreType.DMA((2, SSM_DMA_SLICES, batch))],
        compiler_params=pltpu.CompilerParams(
            dimension_semantics=("arbitrary",), vmem_limit_bytes=VMEM_LIMIT_BYTES),
    )(u, ein, toep, eout, a_q)


def kernel(x, norm_ffn1, ffn1_w_gate, ffn1_w_up, ffn1_w_down, norm_mix, w_in, attn_sinks,
           ssm_lambda_re, ssm_lambda_im, ssm_log_dt, ssm_b_re, ssm_b_im, ssm_c_re, ssm_c_im,
           ssm_d, ssm_glu_w, ssm_glu_b, attn_out_norm, ssm_out_norm, w_out,
           norm_ffn2, ffn2_w_gate, ffn2_w_up, ffn2_w_down, final_norm):
    b, seq, d = x.shape
    depth = norm_ffn1.shape[0]
    assert d == D_MODEL and seq % BLOCK == 0 and (b * seq) % FFN_TOKEN_TILE == 0
    assert b == SUBLANES and seq % (2 * SSM_CHUNK * SSM_RELAYOUT_UNROLL * SSM_DMA_SLICES) == 0
    h = x.reshape(b * seq, d).astype(F32)
    for l in range(depth):
        h, q, k, v, u = _ffn(h, norm_ffn1[l], ffn1_w_gate[l], ffn1_w_up[l], ffn1_w_down[l],
                             final_norm, final_norm=False, proj=(norm_mix[l], w_in[l]))
        attn = _attention(q.reshape(b, seq, -1), k.reshape(b, seq, -1), v.reshape(b, seq, -1),
                          attn_sinks[l])
        ssm_pre = _ssm(u.reshape(b, seq, -1), ssm_lambda_re[l], ssm_lambda_im[l], ssm_log_dt[l],
                       ssm_b_re[l], ssm_b_im[l], ssm_c_re[l], ssm_c_im[l], ssm_d[l])
        mix = (attn.reshape(b * seq, -1), ssm_pre.reshape(b * seq, -1), ssm_glu_w[l],
               ssm_glu_b[l], attn_out_norm[l], ssm_out_norm[l], w_out[l])
        h, = _ffn(h, norm_ffn2[l], ffn2_w_gate[l], ffn2_w_up[l], ffn2_w_down[l],
                  final_norm, final_norm=(l == depth - 1), mix=mix)
    return h.reshape(b, seq, d).astype(x.dtype)
```

```python
import functools
import math

import jax
import jax.numpy as jnp
from jax import lax
from jax.experimental import pallas as pl
from jax.experimental.pallas import tpu as pltpu

F32 = jnp.float32
BF16 = jnp.bfloat16

D_MODEL = 1024
ATTN_HEADS = 8
ATTN_KV_HEADS = 2
Q_PER_KV = ATTN_HEADS // ATTN_KV_HEADS
HEAD_DIM = 64
ATTN_WIDTH = ATTN_HEADS * HEAD_DIM
KV_WIDTH = ATTN_KV_HEADS * HEAD_DIM
WINDOW = 128
BLOCK = 128
SSM_CH = 16
SSM_WIDTH = D_MODEL - ATTN_WIDTH
SSM_GROUPS = SSM_WIDTH // SSM_CH
SSM_STATE = 64
IN_WIDTH = ATTN_WIDTH + 2 * KV_WIDTH + SSM_WIDTH
D_FF = 2816
EPS = 1e-6
NEG_INF = -1e30
LAMBDA_RE_MAX = -1e-4
LOG2_E = math.log2(math.e)
QUERY_SCALE = HEAD_DIM ** -0.5 * LOG2_E

LANES = 128
SUBLANES = 8
VMEM_LIMIT_BYTES = 56 * 1024 * 1024

FFN_TOKEN_TILE = 512
FFN_SUBTILES = 2
FF_TILE = 256
ATTN_BLOCKS_PER_STEP = 8
ATTN_HEADS_PER_PV = 2
SSM_CHUNK = 16
CHUNK_WIDTH = SSM_CHUNK * SSM_CH
GROUPS_PER_BLOCK = LANES // SSM_CH
SSM_INTERLEAVE = 8
SSM_RELAYOUT_UNROLL = 4
SSM_DMA_SLICES = 8
OPS_GROUPS_PER_STEP = 8

NT_DIMS = (((1,), (1,)), ((), ()))
TN_DIMS = (((0,), (0,)), ((), ()))


def _rms(x):
    return x * lax.rsqrt(jnp.mean(x * x, axis=-1, keepdims=True) + EPS)


def _mixed_update(rows, attn_ref, ssm_ref, gw_ref, gb_ref, ga_ref, gs_ref, wo_ref):
    y = jax.nn.gelu(ssm_ref[rows, :])
    z = jnp.dot(y.astype(BF16), gw_ref[...].astype(BF16), preferred_element_type=F32) + gb_ref[...]
    s = y * jax.nn.sigmoid(z)
    sn = _rms(s) * gs_ref[...]
    an = _rms(attn_ref[rows, :].astype(F32)) * ga_ref[...]
    mixed = jnp.concatenate([an, sn], axis=-1).astype(BF16)
    return jnp.dot(mixed, wo_ref[...].astype(BF16), preferred_element_type=F32)


def _mixer_inputs(y, rows, gain_ref, w_ref, q_ref, k_ref, v_ref, u_ref):
    hn = (_rms(y) * gain_ref[...]).astype(BF16)
    proj = jnp.dot(hn, w_ref[...].astype(BF16), preferred_element_type=F32)
    q_ref[rows, :] = (proj[:, :ATTN_WIDTH] * QUERY_SCALE).astype(BF16)
    k_ref[rows, :] = proj[:, ATTN_WIDTH:ATTN_WIDTH + KV_WIDTH].astype(BF16)
    v_ref[rows, :] = proj[:, ATTN_WIDTH + KV_WIDTH:ATTN_WIDTH + 2 * KV_WIDTH].astype(BF16)
    u_ref[rows, :] = proj[:, ATTN_WIDTH + 2 * KV_WIDTH:]


def _ffn_body(read_x, gain_ref, fgain_ref, wg_hbm, wu_hbm, wd_hbm, o_ref,
              wg_ref, wu_ref, wd_ref, act_ref, gu_stage, d_stage, sem, *, final_norm,
              epilogue=None):
    nj = D_FF // FF_TILE

    def weight_copies(j, slot):
        span = pl.ds(j * FF_TILE, FF_TILE)
        return (pltpu.make_async_copy(wg_hbm.at[:, span], gu_stage.at[0, slot], sem.at[0, slot]),
                pltpu.make_async_copy(wu_hbm.at[:, span], gu_stage.at[1, slot], sem.at[1, slot]),
                pltpu.make_async_copy(wd_hbm.at[span, :], d_stage.at[slot], sem.at[2, slot]))

    def step(stage_weights):
        if stage_weights:
            for copy in weight_copies(0, 0):
                copy.start()
        sub = o_ref.shape[0] // FFN_SUBTILES
        row_groups = [slice(h * sub, (h + 1) * sub) for h in range(FFN_SUBTILES)]
        xs = [read_x(rows) for rows in row_groups]
        hns = [(_rms(x) * gain_ref[...]).astype(BF16) for x in xs]
        for j in range(nj):
            cols = slice(j * FF_TILE, (j + 1) * FF_TILE)
            if stage_weights:
                slot = j % 2
                if j + 1 < nj:
                    for copy in weight_copies(j + 1, 1 - slot):
                        copy.start()
                for copy in weight_copies(j, slot):
                    copy.wait()
                wg_ref[:, cols] = gu_stage[0, slot].astype(BF16)
                wu_ref[:, cols] = gu_stage[1, slot].astype(BF16)
                wd_ref[cols, :] = d_stage[slot].astype(BF16)
            for rows, hn in zip(row_groups, hns):
                g = jnp.dot(hn, wg_ref[:, cols], preferred_element_type=F32)
                u = jnp.dot(hn, wu_ref[:, cols], preferred_element_type=F32)
                act_ref[rows, cols] = (g * jax.nn.sigmoid(g) * u).astype(BF16)
        for rows, x in zip(row_groups, xs):
            y = x + 0.5 * jnp.dot(act_ref[rows, :], wd_ref[...], preferred_element_type=F32)
            if final_norm:
                y = _rms(y) * fgain_ref[...]
            o_ref[rows, :] = y
            if epilogue is not None:
                epilogue(y, rows)

    pl.when(pl.program_id(0) == 0)(functools.partial(step, True))
    pl.when(pl.program_id(0) != 0)(functools.partial(step, False))


def _ffn_proj_kernel(x_ref, mgain_ref, win_ref, gain_ref, fgain_ref, wg_hbm, wu_hbm, wd_hbm,
                     o_ref, q_ref, k_ref, v_ref, u_ref, *scratch, final_norm):
    emit = lambda y, rows: _mixer_inputs(y, rows, mgain_ref, win_ref, q_ref, k_ref, v_ref, u_ref)
    _ffn_body(lambda rows: x_ref[rows, :], gain_ref, fgain_ref, wg_hbm, wu_hbm, wd_hbm, o_ref,
              *scratch, final_norm=final_norm, epilogue=emit)


def _mix_ffn_kernel(x_ref, attn_ref, ssm_ref, gw_ref, gb_ref, ga_ref, gs_ref, wo_ref, *ffn_refs,
                    final_norm):
    read_x = lambda rows: x_ref[rows, :] + _mixed_update(rows, attn_ref, ssm_ref, gw_ref, gb_ref,
                                                         ga_ref, gs_ref, wo_ref)
    _ffn_body(read_x, *ffn_refs, final_norm=final_norm)


def _ffn(x, gain, w_gate, w_up, w_down, final_gain, final_norm, mix=None, proj=None):
    assert (mix is None) != (proj is None)
    t = x.shape[0]
    tile = FFN_TOKEN_TILE
    vec = lambda width: pl.BlockSpec((1, width), lambda i: (0, 0))
    resident = lambda r, c: pl.BlockSpec((r, c), lambda i: (0, 0), pipeline_mode=pl.Buffered(1))
    row = lambda width: pl.BlockSpec((tile, width), lambda i: (i, 0))
    hbm = pl.BlockSpec(memory_space=pl.ANY)
    in_specs, args = [row(D_MODEL)], [x]
    out_specs, out_shape = [row(D_MODEL)], [jax.ShapeDtypeStruct((t, D_MODEL), F32)]
    if mix is not None:
        attn, ssm_pre, glu_w, glu_b, attn_gain, ssm_gain, w_out = mix
        body = _mix_ffn_kernel
        in_specs += [row(ATTN_WIDTH), row(SSM_WIDTH),
                     resident(SSM_WIDTH, SSM_WIDTH), vec(SSM_WIDTH), vec(ATTN_WIDTH),
                     vec(SSM_WIDTH), resident(D_MODEL, D_MODEL)]
        args += [attn, ssm_pre, glu_w, glu_b.reshape(1, -1).astype(F32),
                 attn_gain.reshape(1, -1).astype(F32), ssm_gain.reshape(1, -1).astype(F32), w_out]
    if proj is not None:
        mixer_gain, w_in = proj
        body = _ffn_proj_kernel
        in_specs += [vec(D_MODEL), resident(D_MODEL, IN_WIDTH)]
        args += [mixer_gain.reshape(1, D_MODEL), w_in]
        for width, dtype in ((ATTN_WIDTH, BF16), (KV_WIDTH, BF16), (KV_WIDTH, BF16),
                             (SSM_WIDTH, F32)):
            out_specs.append(row(width))
            out_shape.append(jax.ShapeDtypeStruct((t, width), dtype))
    in_specs += [vec(D_MODEL), vec(D_MODEL), hbm, hbm, hbm]
    args += [gain.reshape(1, D_MODEL), final_gain.reshape(1, D_MODEL), w_gate, w_up, w_down]
    return pl.pallas_call(
        functools.partial(body, final_norm=final_norm),
        grid=(t // tile,),
        in_specs=in_specs,
        out_specs=out_specs,
        out_shape=out_shape,
        scratch_shapes=[pltpu.VMEM((D_MODEL, D_FF), BF16),
                        pltpu.VMEM((D_MODEL, D_FF), BF16),
                        pltpu.VMEM((D_FF, D_MODEL), BF16),
                        pltpu.VMEM((tile, D_FF), BF16),
                        pltpu.VMEM((2, 2, D_MODEL, FF_TILE), F32),
                        pltpu.VMEM((2, FF_TILE, D_MODEL), F32),
                        pltpu.SemaphoreType.DMA((3, 2))],
        compiler_params=pltpu.CompilerParams(
            dimension_semantics=("arbitrary",), vmem_limit_bytes=VMEM_LIMIT_BYTES),
    )(*args)


def _attn_kernel(sink_ref, q_ref, kp_ref, kc_ref, kn_ref, vp_ref, vc_ref, vn_ref, o_ref,
                 bias_ref, k_ref, v_ref, s_ref):
    n = pl.program_id(1)
    last = pl.num_programs(1) - 1
    nblk = ATTN_BLOCKS_PER_STEP

    @pl.when((pl.program_id(0) == 0) & (n == 0))
    def _():
        kj = lax.broadcasted_iota(jnp.int32, (3 * BLOCK, BLOCK), 0)
        qi = lax.broadcasted_iota(jnp.int32, (3 * BLOCK, BLOCK), 1)
        rel = jnp.abs(kj - BLOCK - qi)
        dist = rel.astype(F32)
        inside = rel <= WINDOW
        has_prev = kj >= BLOCK
        has_next = kj < 2 * BLOCK
        for variant, ok in enumerate((inside & has_prev, inside, inside & has_next)):
            for h in range(ATTN_HEADS):
                slope = float(2.0 ** (-8.0 * (h + 1) / ATTN_HEADS))
                bias_ref[variant, h] = jnp.where(ok, (-slope * LOG2_E) * dist, NEG_INF)
        for kh in range(ATTN_KV_HEADS):
            v_ref[:, (2 * kh + 1) * HEAD_DIM:(2 * kh + 2) * HEAD_DIM] = jnp.ones(
                (v_ref.shape[0], HEAD_DIM), BF16)

    spans = ((slice(0, BLOCK), kp_ref, vp_ref),
             (slice(BLOCK, (nblk + 1) * BLOCK), kc_ref, vc_ref),
             (slice((nblk + 1) * BLOCK, (nblk + 2) * BLOCK), kn_ref, vn_ref))
    for rows, k_in, v_in in spans:
        k_ref[rows, :] = k_in[...]
        for kh in range(ATTN_KV_HEADS):
            v_ref[rows, 2 * kh * HEAD_DIM:(2 * kh + 1) * HEAD_DIM] = (
                v_in[:, kh * HEAD_DIM:(kh + 1) * HEAD_DIM])

    def scores(j, kh):
        kcat = k_ref[j * BLOCK:(j + 3) * BLOCK, kh * HEAD_DIM:(kh + 1) * HEAD_DIM]
        heads = [kh * Q_PER_KV + g for g in range(Q_PER_KV)]
        qs = jnp.concatenate(
            [q_ref[j * BLOCK:(j + 1) * BLOCK, h * HEAD_DIM:(h + 1) * HEAD_DIM] for h in heads],
            axis=0)
        s_ref[j, kh] = lax.dot_general(kcat, qs, NT_DIMS,
                                       preferred_element_type=F32)

    for kh in range(ATTN_KV_HEADS):
        scores(0, kh)
    for j in range(nblk):
        variant = 1
        if j == 0:
            variant = jnp.where(n == 0, 0, variant)
        if j == nblk - 1:
            variant = jnp.where(n == last, 2, variant)
        outs = []
        for kh in range(ATTN_KV_HEADS):
            if j + 1 < nblk:
                scores(j + 1, kh)
            v_ones = v_ref[j * BLOCK:(j + 3) * BLOCK, 2 * kh * HEAD_DIM:(2 * kh + 2) * HEAD_DIM]
            probs, sink_terms = [], []
            for g in range(Q_PER_KV):
                h = kh * Q_PER_KV + g
                s = s_ref[j, kh, :, g * BLOCK:(g + 1) * BLOCK] + bias_ref[variant, h]
                sink = sink_ref[h] * LOG2_E
                m = jnp.maximum(jnp.max(s, axis=0, keepdims=True), sink)
                probs.append(jnp.exp2(s - m).astype(BF16))
                sink_terms.append(jnp.exp2(sink - m))
            hp = ATTN_HEADS_PER_PV
            for g in range(0, Q_PER_KV, hp):
                pv = lax.dot_general(v_ones, jnp.concatenate(probs[g:g + hp], axis=1), TN_DIMS,
                                     preferred_element_type=F32)
                for t in range(hp):
                    cols = slice(t * BLOCK, (t + 1) * BLOCK)
                    den = pv[HEAD_DIM:HEAD_DIM + 1, cols] + sink_terms[g + t]
                    outs.append(pv[:HEAD_DIM, cols] / den)
        o_ref[j * BLOCK:(j + 1) * BLOCK, :] = jnp.concatenate(outs, axis=0).T.astype(o_ref.dtype)


def _attention(q, k, v, sinks):
    b, seq, _ = q.shape
    nblk = ATTN_BLOCKS_PER_STEP
    steps = seq // (nblk * BLOCK)
    nb = seq // BLOCK
    assert seq % (nblk * BLOCK) == 0 and nb >= 2
    edge = lambda f: pl.BlockSpec((None, BLOCK, KV_WIDTH), f)
    body = pl.BlockSpec((None, nblk * BLOCK, KV_WIDTH), lambda bi, n: (bi, n, 0))
    prev = lambda bi, n: (bi, jnp.maximum(n * nblk - 1, 0), 0)
    nxt = lambda bi, n: (bi, jnp.minimum((n + 1) * nblk, nb - 1), 0)
    rows = pl.BlockSpec((None, nblk * BLOCK, ATTN_WIDTH), lambda bi, n: (bi, n, 0))
    return pl.pallas_call(
        _attn_kernel,
        grid=(b, steps),
        in_specs=[pl.BlockSpec(memory_space=pltpu.SMEM), rows,
                  edge(prev), body, edge(nxt), edge(prev), body, edge(nxt)],
        out_specs=rows,
        out_shape=jax.ShapeDtypeStruct((b, seq, ATTN_WIDTH), BF16),
        scratch_shapes=[pltpu.VMEM((3, ATTN_HEADS, 3 * BLOCK, BLOCK), F32),
                        pltpu.VMEM(((nblk + 2) * BLOCK, KV_WIDTH), BF16),
                        pltpu.VMEM(((nblk + 2) * BLOCK, 2 * KV_WIDTH), BF16),
                        pltpu.VMEM((nblk, ATTN_KV_HEADS, 3 * BLOCK, Q_PER_KV * BLOCK), F32)],
        compiler_params=pltpu.CompilerParams(
            dimension_semantics=("arbitrary", "arbitrary"), vmem_limit_bytes=VMEM_LIMIT_BYTES),
    )(sinks.astype(F32), q, k, k, k, v, v, v)


def _dot_nt_split(a, b):
    a_hi, b_hi = a.astype(BF16), b.astype(BF16)
    a_lo = (a - a_hi.astype(F32)).astype(BF16)
    b_lo = (b - b_hi.astype(F32)).astype(BF16)
    dot = functools.partial(lax.dot_general, dimension_numbers=NT_DIMS,
                            preferred_element_type=F32)
    return dot(a_hi, b_hi) + dot(a_hi, b_lo) + dot(a_lo, b_hi)


def _ssm_ops_kernel(prm_ref, bc_ref, ein_ref, toep_ref, eout_ref, aq_ref):
    q, hc, p = SSM_CHUNK, SSM_CH, SSM_STATE
    fwd = lax.broadcasted_iota(jnp.int32, (1, 2 * p), 1) < p
    zero_row = jnp.zeros((1, 2 * p), F32)
    row_id = lax.broadcasted_iota(jnp.int32, (CHUNK_WIDTH, CHUNK_WIDTH), 0)
    col_id = lax.broadcasted_iota(jnp.int32, (CHUNK_WIDTH, CHUNK_WIDTH), 1)

    def table(select, n):
        picks = [select(m) for m in range(n)]
        re = jnp.concatenate([jnp.broadcast_to(r, (hc, 2 * p)) for r, _ in picks], axis=0)
        im = jnp.concatenate([jnp.broadcast_to(i, (hc, 2 * p)) for _, i in picks], axis=0)
        return re, im

    def tile_rows(x, n):
        return jnp.concatenate([x] * n, axis=0)

    for gi in range(prm_ref.shape[0]):
        lr = jnp.minimum(prm_ref[gi, 0:1, :], LAMBDA_RE_MAX)
        li = prm_ref[gi, 1:2, :]
        dt = jnp.exp(prm_ref[gi, 2:3, :])
        mag = jnp.exp(lr * dt)
        a_r = mag * jnp.cos(li * dt)
        a_i = mag * jnp.sin(li * dt)
        den = lr * lr + li * li
        coef_r = ((a_r - 1.0) * lr + a_i * li) / den
        coef_i = (a_i * lr - (a_r - 1.0) * li) / den
        b_r, b_i = bc_ref[gi, 0], bc_ref[gi, 1]
        c_r, c_i = bc_ref[gi, 2], bc_ref[gi, 3]
        bb_r = coef_r * b_r - coef_i * b_i
        bb_i = coef_r * b_i + coef_i * b_r

        pw = [(jnp.ones((1, 2 * p), F32), zero_row)]
        for _ in range(q):
            r, i = pw[-1]
            pw.append((r * a_r - i * a_i, r * a_i + i * a_r))

        def both(f_idx, b_idx):
            fr, fi = pw[f_idx] if f_idx is not None else (zero_row, zero_row)
            br, bi = pw[b_idx] if b_idx is not None else (zero_row, zero_row)
            return jnp.where(fwd, fr, br), jnp.where(fwd, fi, bi)

        p_r, p_i = table(lambda i: both(q - 1 - i, i), q)
        tb_r, tb_i = tile_rows(bb_r, q), tile_rows(bb_i, q)
        ein = jnp.concatenate([tb_r * p_r - tb_i * p_i, tb_r * p_i + tb_i * p_r], axis=1)
        ein_ref[gi] = ein.astype(BF16)

        p_r, p_i = table(lambda j: both(j + 1, q - j), q)
        tc_r, tc_i = tile_rows(c_r, q), tile_rows(c_i, q)
        eout = jnp.concatenate([tc_r * p_r - tc_i * p_i, -(tc_r * p_i + tc_i * p_r)], axis=1)
        eout_ref[gi] = eout.astype(BF16)

        def lag(m):
            return both(m - (q - 1) if q - 1 <= m <= 2 * q - 2 else None,
                        (q - 1) - m if m <= q - 1 else None)

        p_r, p_i = table(lag, 2 * q)
        tc_r, tc_i = tile_rows(c_r, 2 * q), tile_rows(c_i, 2 * q)
        cpt = jnp.concatenate([tc_r * p_r - tc_i * p_i, tc_r * p_i + tc_i * p_r], axis=1)
        bcat = jnp.concatenate([bb_r, -bb_i], axis=1)
        kern = _dot_nt_split(bcat, cpt)
        toep = jnp.concatenate(
            [kern[:, hc * (q - 1 - i):hc * (q - 1 - i) + CHUNK_WIDTH] for i in range(q)], axis=0)
        skip = jnp.concatenate([prm_ref[gi, 3:4, :]] * (CHUNK_WIDTH // (2 * p)), axis=1)
        toep_ref[gi] = (toep + jnp.where(row_id == col_id, skip, 0.0)).astype(BF16)

        aq_ref[gi] = jnp.concatenate(
            [pw[q][0], pw[q][1], jnp.zeros((SUBLANES - 2, 2 * p), F32)], axis=0)


def _ssm_operators(lam_re, lam_im, log_dt, b_re, b_im, c_re, c_im, d_skip):
    g, p, hc = SSM_GROUPS, SSM_STATE, SSM_CH
    lanes = lambda a: a.astype(F32).transpose(1, 0, 2).reshape(g, 2 * p)
    prm = jnp.stack(
        [lanes(lam_re), lanes(lam_im),
         jnp.repeat(log_dt.astype(F32).T, p, axis=1),
         jnp.tile(d_skip.astype(F32), (1, 2 * p // hc))]
        + [jnp.zeros((g, 2 * p), F32)] * (SUBLANES - 4), axis=1)
    bc = jnp.stack(
        [b_re.astype(F32).transpose(1, 3, 0, 2).reshape(g, hc, 2 * p),
         b_im.astype(F32).transpose(1, 3, 0, 2).reshape(g, hc, 2 * p),
         c_re.astype(F32).transpose(1, 2, 0, 3).reshape(g, hc, 2 * p),
         c_im.astype(F32).transpose(1, 2, 0, 3).reshape(g, hc, 2 * p)], axis=1)
    gs = OPS_GROUPS_PER_STEP
    mat = pl.BlockSpec((gs, CHUNK_WIDTH, CHUNK_WIDTH), lambda s: (s, 0, 0))
    mat_shape = jax.ShapeDtypeStruct((g, CHUNK_WIDTH, CHUNK_WIDTH), BF16)
    return pl.pallas_call(
        _ssm_ops_kernel,
        grid=(g // gs,),
        in_specs=[pl.BlockSpec((gs, SUBLANES, 2 * p), lambda s: (s, 0, 0)),
                  pl.BlockSpec((gs, 4, hc, 2 * p), lambda s: (s, 0, 0, 0))],
        out_specs=[mat, mat, mat, pl.BlockSpec((gs, SUBLANES, 2 * p), lambda s: (s, 0, 0))],
        out_shape=[mat_shape, mat_shape, mat_shape,
                   jax.ShapeDtypeStruct((g, SUBLANES, 2 * p), F32)],
        compiler_params=pltpu.CompilerParams(
            dimension_semantics=("parallel",), vmem_limit_bytes=VMEM_LIMIT_BYTES),
    )(prm, bc)


def _lane_roll(x, shift):
    shift %= LANES
    return jnp.concatenate([x[:, LANES - shift:], x[:, :LANES - shift]], axis=1)


def _block_transpose(v, lane_block):
    return _block_transposes([v], lane_block)[0]


def _block_transposes(groups, lane_block):
    n = len(groups[0])
    skewed = [[v[i] if i == 0 else _lane_roll(v[i], SSM_CH * i) for i in range(n)] for v in groups]
    picked = []
    for sk in skewed:
        rows = []
        for g in range(n):
            p = sk[(0 - g) % n]
            for j in range(1, n):
                p = jnp.where(lane_block == j, sk[(j - g) % n], p)
            rows.append(p)
        picked.append(rows)
    return [[p[g] if g == 0 else _lane_roll(p[g], -SSM_CH * g) for g in range(n)] for p in picked]


def _ssm_kernel(u_hbm, ein_ref, toep_ref, eout_ref, aq_ref, y_hbm,
                io_ref, ug_ref, s_ref, x_ref, in_sem, out_sem, *, batch, seq, pitch):
    q, half = SSM_CHUNK, SSM_STATE
    n_chunks = seq // q
    gpb = GROUPS_PER_BLOCK
    slab = 2 * batch
    lane_block = lax.broadcasted_iota(jnp.int32, (slab, LANES), 1) // SSM_CH
    step_id = pl.program_id(0)
    last_step = pl.num_programs(0) - 1
    slot = lax.rem(step_id, 2)

    n_slices = SSM_DMA_SLICES
    slice_rows = seq // n_slices
    slice_trips = n_chunks // (2 * n_slices)

    def column_copies(block, slot_, to_vmem, slices=range(SSM_DMA_SLICES)):
        lanes = pl.ds(pl.multiple_of(block * LANES, LANES), LANES)
        copies = []
        for k in slices:
            for b in range(batch):
                hbm = (u_hbm if to_vmem else y_hbm).at[b, pl.ds(k * slice_rows, slice_rows), lanes]
                vmem = io_ref.at[slot_, pl.ds(b * pitch + k * slice_rows, slice_rows), :]
                if to_vmem:
                    copies.append(pltpu.make_async_copy(hbm, vmem, in_sem.at[slot_, k, b]))
                else:
                    copies.append(pltpu.make_async_copy(vmem, hbm, out_sem.at[slot_, k, b]))
        return copies

    def start(copies):
        for copy in copies:
            copy.start()

    def wait(copies):
        for copy in copies:
            copy.wait()

    @pl.when(step_id == 0)
    def _():
        start(column_copies(0, 0, True))

    @pl.when(step_id < last_step)
    def _():
        @pl.when(step_id >= 1)
        def _():
            wait(column_copies(step_id - 1, 1 - slot, False))
        start(column_copies(step_id + 1, 1 - slot, True))

    buf = io_ref.at[slot]

    def gather(cp, carry):
        rows = pl.ds(pl.multiple_of(cp * slab, slab), slab)
        parts = []
        for part in range(q // gpb):
            t0 = cp * 2 * q + part * gpb
            parts.append([jnp.concatenate([buf[pl.ds(t0 + i, batch, stride=pitch), :],
                                           buf[pl.ds(t0 + q + i, batch, stride=pitch), :]],
                                          axis=0).astype(BF16) for i in range(gpb)])
        for part, w in enumerate(_block_transposes(parts, lane_block)):
            for g in range(gpb):
                ug_ref[g, rows, part * LANES:(part + 1) * LANES] = w[g]
        return carry

    for k in range(n_slices):
        wait(column_copies(step_id, slot, True, slices=(k,)))
        lax.fori_loop(k * slice_trips, (k + 1) * slice_trips, gather, 0,
                      unroll=SSM_RELAYOUT_UNROLL)

    fwd_lane = lax.broadcasted_iota(jnp.int32, (batch, 2 * half), 1) < half
    zeros = jnp.zeros((batch, half), F32)
    last_rows = pl.ds((n_chunks - 1) * batch, batch)
    ni = SSM_INTERLEAVE
    for g0 in range(0, gpb, ni):
        for gi in range(ni):
            s_ref[gi] = jnp.dot(ug_ref[g0 + gi], ein_ref[g0 + gi], preferred_element_type=F32)
            x_ref[gi, 0:batch, 0:half] = zeros
            x_ref[gi, 0:batch, 2 * half:3 * half] = zeros
            x_ref[gi, last_rows, half:2 * half] = zeros
            x_ref[gi, last_rows, 3 * half:4 * half] = zeros

        def step(k, carry):
            rf = pl.multiple_of(k * batch, batch)
            rb = pl.multiple_of((n_chunks - 1 - k) * batch, batch)
            new = []
            for gi in range(ni):
                xr, xi = carry[gi]
                sre = jnp.where(fwd_lane, s_ref[gi, pl.ds(rf, batch), 0:2 * half],
                                s_ref[gi, pl.ds(rb, batch), 0:2 * half])
                sim = jnp.where(fwd_lane, s_ref[gi, pl.ds(rf, batch), 2 * half:4 * half],
                                s_ref[gi, pl.ds(rb, batch), 2 * half:4 * half])
                ar = aq_ref[g0 + gi, 0:1, :]
                ai = aq_ref[g0 + gi, 1:2, :]
                nr = ar * xr - ai * xi + sre
                nim = ar * xi + ai * xr + sim
                x_ref[gi, pl.ds(rf + batch, batch), 0:half] = nr[:, 0:half]
                x_ref[gi, pl.ds(rf + batch, batch), 2 * half:3 * half] = nim[:, 0:half]
                x_ref[gi, pl.ds(rb - batch, batch), half:2 * half] = nr[:, half:2 * half]
                x_ref[gi, pl.ds(rb - batch, batch), 3 * half:4 * half] = nim[:, half:2 * half]
                new.append((nr, nim))
            return tuple(new)

        init = tuple((jnp.zeros((batch, 2 * half), F32), jnp.zeros((batch, 2 * half), F32))
                     for _ in range(ni))
        lax.fori_loop(0, n_chunks - 1, step, init)

        for gi in range(ni):
            g = g0 + gi
            y = jnp.dot(ug_ref[g], toep_ref[g], preferred_element_type=F32)
            y += lax.dot_general(x_ref[gi].astype(BF16), eout_ref[g], NT_DIMS,
                                 preferred_element_type=F32)
            ug_ref[g] = y.astype(BF16)

    def scatter(cp, carry):
        rows = pl.ds(pl.multiple_of(cp * slab, slab), slab)
        parts = [[ug_ref[g, rows, part * LANES:(part + 1) * LANES] for g in range(gpb)]
                 for part in range(q // gpb)]
        for part, w in enumerate(_block_transposes(parts, lane_block)):
            t0 = cp * 2 * q + part * gpb
            for j in range(gpb):
                wj = w[j].astype(F32)
                buf[pl.ds(t0 + j, batch, stride=pitch), :] = wj[0:batch]
                buf[pl.ds(t0 + q + j, batch, stride=pitch), :] = wj[batch:slab]
        return carry

    for k in range(n_slices):
        lax.fori_loop(k * slice_trips, (k + 1) * slice_trips, scatter, 0,
                      unroll=SSM_RELAYOUT_UNROLL)
        start(column_copies(step_id, slot, False, slices=(k,)))

    @pl.when(step_id == last_step)
    def _():
        @pl.when(step_id >= 1)
        def _():
            wait(column_copies(step_id - 1, 1 - slot, False))
        wait(column_copies(step_id, slot, False))


def _padded_seq(seq):
    return seq + SUBLANES if seq % (2 * SUBLANES) == 0 else seq


def _ssm(u, lam_re, lam_im, log_dt, b_re, b_im, c_re, c_im, d_skip):
    batch, seq, _ = u.shape
    pitch = _padded_seq(seq)
    rows = batch * seq // SSM_CHUNK
    ein, toep, eout, a_q = _ssm_operators(lam_re, lam_im, log_dt, b_re, b_im, c_re, c_im, d_skip)
    gpb = GROUPS_PER_BLOCK
    mat = pl.BlockSpec((gpb, CHUNK_WIDTH, CHUNK_WIDTH), lambda s: (s, 0, 0))
    hbm = pl.BlockSpec(memory_space=pl.ANY)
    return pl.pallas_call(
        functools.partial(_ssm_kernel, batch=batch, seq=seq, pitch=pitch),
        grid=(SSM_GROUPS // gpb,),
        in_specs=[hbm, mat, mat, mat,
                  pl.BlockSpec((gpb, SUBLANES, 2 * SSM_STATE), lambda s: (s, 0, 0))],
        out_specs=hbm,
        out_shape=jax.ShapeDtypeStruct((batch, seq, SSM_WIDTH), F32),
        scratch_shapes=[pltpu.VMEM((2, batch * pitch, LANES), F32),
                        pltpu.VMEM((gpb, rows, CHUNK_WIDTH), BF16),
                        pltpu.VMEM((SSM_INTERLEAVE, rows, 4 * SSM_STATE), F32),
                        pltpu.VMEM((SSM_INTERLEAVE, rows, 4 * SSM_STATE), F32),
                        pltpu.SemaphoreType.DMA((2, SSM_DMA_SLICES, batch)),
                        pltpu.SemaphoreType.DMA((2, SSM_DMA_SLICES, batch))],
        compiler_params=pltpu.CompilerParams(
            dimension_semantics=("arbitrary",), vmem_limit_bytes=VMEM_LIMIT_BYTES),
    )(u, ein, toep, eout, a_q)


def kernel(x, norm_ffn1, ffn1_w_gate, ffn1_w_up, ffn1_w_down, norm_mix, w_in, attn_sinks,
           ssm_lambda_re, ssm_lambda_im, ssm_log_dt, ssm_b_re, ssm_b_im, ssm_c_re, ssm_c_im,
           ssm_d, ssm_glu_w, ssm_glu_b, attn_out_norm, ssm_out_norm, w_out,
           norm_ffn2, ffn2_w_gate, ffn2_w_up, ffn2_w_down, final_norm):
    b, seq, d = x.shape
    depth = norm_ffn1.shape[0]
    assert d == D_MODEL and seq % BLOCK == 0 and (b * seq) % FFN_TOKEN_TILE == 0
    assert b == SUBLANES and seq % (2 * SSM_CHUNK * SSM_RELAYOUT_UNROLL * SSM_DMA_SLICES) == 0
    h = x.reshape(b * seq, d).astype(F32)
    for l in range(depth):
        h, q, k, v, u = _ffn(h, norm_ffn1[l], ffn1_w_gate[l], ffn1_w_up[l], ffn1_w_down[l],
                             final_norm, final_norm=False, proj=(norm_mix[l], w_in[l]))
        attn = _attention(q.reshape(b, seq, -1), k.reshape(b, seq, -1), v.reshape(b, seq, -1),
                          attn_sinks[l])
        ssm_pre = _ssm(u.reshape(b, seq, -1), ssm_lambda_re[l], ssm_lambda_im[l], ssm_log_dt[l],
                       ssm_b_re[l], ssm_b_im[l], ssm_c_re[l], ssm_c_im[l], ssm_d[l])
        mix = (attn.reshape(b * seq, -1), ssm_pre.reshape(b * seq, -1), ssm_glu_w[l],
               ssm_glu_b[l], attn_out_norm[l], ssm_out_norm[l], w_out[l])
        h, = _ffn(h, norm_ffn2[l], ffn2_w_gate[l], ffn2_w_up[l], ffn2_w_down[l],
                  final_norm, final_norm=(l == depth - 1), mix=mix)
    return h.reshape(b, seq, d).astype(x.dtype)
```

```python
import functools
import math

import jax
import jax.numpy as jnp
from jax import lax
from jax.experimental import pallas as pl
from jax.experimental.pallas import tpu as pltpu

F32 = jnp.float32
BF16 = jnp.bfloat16

D_MODEL = 1024
ATTN_HEADS = 8
ATTN_KV_HEADS = 2
Q_PER_KV = ATTN_HEADS // ATTN_KV_HEADS
HEAD_DIM = 64
ATTN_WIDTH = ATTN_HEADS * HEAD_DIM
KV_WIDTH = ATTN_KV_HEADS * HEAD_DIM
WINDOW = 128
BLOCK = 128
SSM_CH = 16
SSM_WIDTH = D_MODEL - ATTN_WIDTH
SSM_GROUPS = SSM_WIDTH // SSM_CH
SSM_STATE = 64
IN_WIDTH = ATTN_WIDTH + 2 * KV_WIDTH + SSM_WIDTH
D_FF = 2816
EPS = 1e-6
NEG_INF = -1e30
LAMBDA_RE_MAX = -1e-4
LOG2_E = math.log2(math.e)
QUERY_SCALE = HEAD_DIM ** -0.5 * LOG2_E

LANES = 128
SUBLANES = 8
VMEM_LIMIT_BYTES = 56 * 1024 * 1024

FFN_TOKEN_TILE = 512
FFN_SUBTILES = 2
FF_TILE = 256
ATTN_BLOCKS_PER_STEP = 8
ATTN_HEADS_PER_PV = 2
SSM_CHUNK = 16
CHUNK_WIDTH = SSM_CHUNK * SSM_CH
GROUPS_PER_BLOCK = LANES // SSM_CH
SSM_INTERLEAVE = 8
SSM_RELAYOUT_UNROLL = 4
SSM_DMA_SLICES = 8
OPS_GROUPS_PER_STEP = 8

NT_DIMS = (((1,), (1,)), ((), ()))
TN_DIMS = (((0,), (0,)), ((), ()))


def _rms(x):
    return x * lax.rsqrt(jnp.mean(x * x, axis=-1, keepdims=True) + EPS)


def _mixed_update(rows, attn_ref, ssm_ref, gw_ref, gb_ref, ga_ref, gs_ref, wo_ref):
    y = jax.nn.gelu(ssm_ref[rows, :])
    z = jnp.dot(y.astype(BF16), gw_ref[...].astype(BF16), preferred_element_type=F32) + gb_ref[...]
    s = y * jax.nn.sigmoid(z)
    sn = _rms(s) * gs_ref[...]
    an = _rms(attn_ref[rows, :].astype(F32)) * ga_ref[...]
    mixed = jnp.concatenate([an, sn], axis=-1).astype(BF16)
    return jnp.dot(mixed, wo_ref[...].astype(BF16), preferred_element_type=F32)


def _mixer_inputs(y, rows, gain_ref, w_ref, q_ref, k_ref, v_ref, u_ref):
    hn = (_rms(y) * gain_ref[...]).astype(BF16)
    proj = jnp.dot(hn, w_ref[...].astype(BF16), preferred_element_type=F32)
    q_ref[rows, :] = (proj[:, :ATTN_WIDTH] * QUERY_SCALE).astype(BF16)
    k_ref[rows, :] = proj[:, ATTN_WIDTH:ATTN_WIDTH + KV_WIDTH].astype(BF16)
    v_ref[rows, :] = proj[:, ATTN_WIDTH + KV_WIDTH:ATTN_WIDTH + 2 * KV_WIDTH].astype(BF16)
    u_ref[rows, :] = proj[:, ATTN_WIDTH + 2 * KV_WIDTH:]


def _ffn_body(read_x, gain_ref, fgain_ref, wg_hbm, wu_hbm, wd_hbm, o_ref,
              wg_ref, wu_ref, wd_ref, act_ref, gu_stage, d_stage, sem, *, final_norm,
              epilogue=None):
    nj = D_FF // FF_TILE

    def weight_copies(j, slot):
        span = pl.ds(j * FF_TILE, FF_TILE)
        return (pltpu.make_async_copy(wg_hbm.at[:, span], gu_stage.at[0, slot], sem.at[0, slot]),
                pltpu.make_async_copy(wu_hbm.at[:, span], gu_stage.at[1, slot], sem.at[1, slot]),
                pltpu.make_async_copy(wd_hbm.at[span, :], d_stage.at[slot], sem.at[2, slot]))

    def step(stage_weights):
        if stage_weights:
            for copy in weight_copies(0, 0):
                copy.start()
        sub = o_ref.shape[0] // FFN_SUBTILES
        row_groups = [slice(h * sub, (h + 1) * sub) for h in range(FFN_SUBTILES)]
        xs = [read_x(rows) for rows in row_groups]
        hns = [(_rms(x) * gain_ref[...]).astype(BF16) for x in xs]
        for j in range(nj):
            cols = slice(j * FF_TILE, (j + 1) * FF_TILE)
            if stage_weights:
                slot = j % 2
                if j + 1 < nj:
                    for copy in weight_copies(j + 1, 1 - slot):
                        copy.start()
                for copy in weight_copies(j, slot):
                    copy.wait()
                wg_ref[:, cols] = gu_stage[0, slot].astype(BF16)
                wu_ref[:, cols] = gu_stage[1, slot].astype(BF16)
                wd_ref[cols, :] = d_stage[slot].astype(BF16)
            for rows, hn in zip(row_groups, hns):
                g = jnp.dot(hn, wg_ref[:, cols], preferred_element_type=F32)
                u = jnp.dot(hn, wu_ref[:, cols], preferred_element_type=F32)
                act_ref[rows, cols] = (g * jax.nn.sigmoid(g) * u).astype(BF16)
        for rows, x in zip(row_groups, xs):
            y = x + 0.5 * jnp.dot(act_ref[rows, :], wd_ref[...], preferred_element_type=F32)
            if final_norm:
                y = _rms(y) * fgain_ref[...]
            o_ref[rows, :] = y
            if epilogue is not None:
                epilogue(y, rows)

    pl.when(pl.program_id(0) == 0)(functools.partial(step, True))
    pl.when(pl.program_id(0) != 0)(functools.partial(step, False))


def _ffn_proj_kernel(x_ref, mgain_ref, win_ref, gain_ref, fgain_ref, wg_hbm, wu_hbm, wd_hbm,
                     o_ref, q_ref, k_ref, v_ref, u_ref, *scratch, final_norm):
    emit = lambda y, rows: _mixer_inputs(y, rows, mgain_ref, win_ref, q_ref, k_ref, v_ref, u_ref)
    _ffn_body(lambda rows: x_ref[rows, :], gain_ref, fgain_ref, wg_hbm, wu_hbm, wd_hbm, o_ref,
              *scratch, final_norm=final_norm, epilogue=emit)


def _mix_ffn_kernel(x_ref, attn_ref, ssm_ref, gw_ref, gb_ref, ga_ref, gs_ref, wo_ref, *ffn_refs,
                    final_norm):
    read_x = lambda rows: x_ref[rows, :] + _mixed_update(rows, attn_ref, ssm_ref, gw_ref, gb_ref,
                                                         ga_ref, gs_ref, wo_ref)
    _ffn_body(read_x, *ffn_refs, final_norm=final_norm)


def _ffn(x, gain, w_gate, w_up, w_down, final_gain, final_norm, mix=None, proj=None):
    assert (mix is None) != (proj is None)
    t = x.shape[0]
    tile = FFN_TOKEN_TILE
    vec = lambda width: pl.BlockSpec((1, width), lambda i: (0, 0))
    resident = lambda r, c: pl.BlockSpec((r, c), lambda i: (0, 0), pipeline_mode=pl.Buffered(1))
    row = lambda width: pl.BlockSpec((tile, width), lambda i: (i, 0))
    hbm = pl.BlockSpec(memory_space=pl.ANY)
    in_specs, args = [row(D_MODEL)], [x]
    out_specs, out_shape = [row(D_MODEL)], [jax.ShapeDtypeStruct((t, D_MODEL), F32)]
    if mix is not None:
        attn, ssm_pre, glu_w, glu_b, attn_gain, ssm_gain, w_out = mix
        body = _mix_ffn_kernel
        in_specs += [row(ATTN_WIDTH), row(SSM_WIDTH),
                     resident(SSM_WIDTH, SSM_WIDTH), vec(SSM_WIDTH), vec(ATTN_WIDTH),
                     vec(SSM_WIDTH), resident(D_MODEL, D_MODEL)]
        args += [attn, ssm_pre, glu_w, glu_b.reshape(1, -1).astype(F32),
                 attn_gain.reshape(1, -1).astype(F32), ssm_gain.reshape(1, -1).astype(F32), w_out]
    if proj is not None:
        mixer_gain, w_in = proj
        body = _ffn_proj_kernel
        in_specs += [vec(D_MODEL), resident(D_MODEL, IN_WIDTH)]
        args += [mixer_gain.reshape(1, D_MODEL), w_in]
        for width, dtype in ((ATTN_WIDTH, BF16), (KV_WIDTH, BF16), (KV_WIDTH, BF16),
                             (SSM_WIDTH, F32)):
            out_specs.append(row(width))
            out_shape.append(jax.ShapeDtypeStruct((t, width), dtype))
    in_specs += [vec(D_MODEL), vec(D_MODEL), hbm, hbm, hbm]
    args += [gain.reshape(1, D_MODEL), final_gain.reshape(1, D_MODEL), w_gate, w_up, w_down]
    return pl.pallas_call(
        functools.partial(body, final_norm=final_norm),
        grid=(t // tile,),
        in_specs=in_specs,
        out_specs=out_specs,
        out_shape=out_shape,
        scratch_shapes=[pltpu.VMEM((D_MODEL, D_FF), BF16),
                        pltpu.VMEM((D_MODEL, D_FF), BF16),
                        pltpu.VMEM((D_FF, D_MODEL), BF16),
                        pltpu.VMEM((tile, D_FF), BF16),
                        pltpu.VMEM((2, 2, D_MODEL, FF_TILE), F32),
                        pltpu.VMEM((2, FF_TILE, D_MODEL), F32),
                        pltpu.SemaphoreType.DMA((3, 2))],
        compiler_params=pltpu.CompilerParams(
            dimension_semantics=("arbitrary",), vmem_limit_bytes=VMEM_LIMIT_BYTES),
    )(*args)


def _attn_kernel(sink_ref, q_ref, kp_ref, kc_ref, kn_ref, vp_ref, vc_ref, vn_ref, o_ref,
                 bias_ref, k_ref, v_ref, s_ref):
    n = pl.program_id(1)
    last = pl.num_programs(1) - 1
    nblk = ATTN_BLOCKS_PER_STEP

    @pl.when((pl.program_id(0) == 0) & (n == 0))
    def _():
        kj = lax.broadcasted_iota(jnp.int32, (3 * BLOCK, BLOCK), 0)
        qi = lax.broadcasted_iota(jnp.int32, (3 * BLOCK, BLOCK), 1)
        rel = jnp.abs(kj - BLOCK - qi)
        dist = rel.astype(F32)
        inside = rel <= WINDOW
        has_prev = kj >= BLOCK
        has_next = kj < 2 * BLOCK
        for variant, ok in enumerate((inside & has_prev, inside, inside & has_next)):
            for h in range(ATTN_HEADS):
                slope = float(2.0 ** (-8.0 * (h + 1) / ATTN_HEADS))
                bias_ref[variant, h] = jnp.where(ok, (-slope * LOG2_E) * dist, NEG_INF)
        for kh in range(ATTN_KV_HEADS):
            v_ref[:, (2 * kh + 1) * HEAD_DIM:(2 * kh + 2) * HEAD_DIM] = jnp.ones(
                (v_ref.shape[0], HEAD_DIM), BF16)

    spans = ((slice(0, BLOCK), kp_ref, vp_ref),
             (slice(BLOCK, (nblk + 1) * BLOCK), kc_ref, vc_ref),
             (slice((nblk + 1) * BLOCK, (nblk + 2) * BLOCK), kn_ref, vn_ref))
    for rows, k_in, v_in in spans:
        k_ref[rows, :] = k_in[...]
        for kh in range(ATTN_KV_HEADS):
            v_ref[rows, 2 * kh * HEAD_DIM:(2 * kh + 1) * HEAD_DIM] = (
                v_in[:, kh * HEAD_DIM:(kh + 1) * HEAD_DIM])

    def scores(j, kh):
        kcat = k_ref[j * BLOCK:(j + 3) * BLOCK, kh * HEAD_DIM:(kh + 1) * HEAD_DIM]
        heads = [kh * Q_PER_KV + g for g in range(Q_PER_KV)]
        qs = jnp.concatenate(
            [q_ref[j * BLOCK:(j + 1) * BLOCK, h * HEAD_DIM:(h + 1) * HEAD_DIM] for h in heads],
            axis=0)
        s_ref[j, kh] = lax.dot_general(kcat, qs, NT_DIMS,
                                       preferred_element_type=F32)

    for kh in range(ATTN_KV_HEADS):
        scores(0, kh)
    for j in range(nblk):
        variant = 1
        if j == 0:
            variant = jnp.where(n == 0, 0, variant)
        if j == nblk - 1:
            variant = jnp.where(n == last, 2, variant)
        outs = []
        for kh in range(ATTN_KV_HEADS):
            if j + 1 < nblk:
                scores(j + 1, kh)
            v_ones = v_ref[j * BLOCK:(j + 3) * BLOCK, 2 * kh * HEAD_DIM:(2 * kh + 2) * HEAD_DIM]
            probs, sink_terms = [], []
            for g in range(Q_PER_KV):
                h = kh * Q_PER_KV + g
                s = s_ref[j, kh, :, g * BLOCK:(g + 1) * BLOCK] + bias_ref[variant, h]
                sink = sink_ref[h] * LOG2_E
                m = jnp.maximum(jnp.max(s, axis=0, keepdims=True), sink)
                probs.append(jnp.exp2(s - m).astype(BF16))
                sink_terms.append(jnp.exp2(sink - m))
            hp = ATTN_HEADS_PER_PV
            for g in range(0, Q_PER_KV, hp):
                pv = lax.dot_general(v_ones, jnp.concatenate(probs[g:g + hp], axis=1), TN_DIMS,
                                     preferred_element_type=F32)
                for t in range(hp):
                    cols = slice(t * BLOCK, (t + 1) * BLOCK)
                    den = pv[HEAD_DIM:HEAD_DIM + 1, cols] + sink_terms[g + t]
                    outs.append(pv[:HEAD_DIM, cols] / den)
        o_ref[j * BLOCK:(j + 1) * BLOCK, :] = jnp.concatenate(outs, axis=0).T.astype(o_ref.dtype)


def _attention(q, k, v, sinks):
    b, seq, _ = q.shape
    nblk = ATTN_BLOCKS_PER_STEP
    steps = seq // (nblk * BLOCK)
    nb = seq // BLOCK
    assert seq % (nblk * BLOCK) == 0 and nb >= 2
    edge = lambda f: pl.BlockSpec((None, BLOCK, KV_WIDTH), f)
    body = pl.BlockSpec((None, nblk * BLOCK, KV_WIDTH), lambda bi, n: (bi, n, 0))
    prev = lambda bi, n: (bi, jnp.maximum(n * nblk - 1, 0), 0)
    nxt = lambda bi, n: (bi, jnp.minimum((n + 1) * nblk, nb - 1), 0)
    rows = pl.BlockSpec((None, nblk * BLOCK, ATTN_WIDTH), lambda bi, n: (bi, n, 0))
    return pl.pallas_call(
        _attn_kernel,
        grid=(b, steps),
        in_specs=[pl.BlockSpec(memory_space=pltpu.SMEM), rows,
                  edge(prev), body, edge(nxt), edge(prev), body, edge(nxt)],
        out_specs=rows,
        out_shape=jax.ShapeDtypeStruct((b, seq, ATTN_WIDTH), BF16),
        scratch_shapes=[pltpu.VMEM((3, ATTN_HEADS, 3 * BLOCK, BLOCK), F32),
                        pltpu.VMEM(((nblk + 2) * BLOCK, KV_WIDTH), BF16),
                        pltpu.VMEM(((nblk + 2) * BLOCK, 2 * KV_WIDTH), BF16),
                        pltpu.VMEM((nblk, ATTN_KV_HEADS, 3 * BLOCK, Q_PER_KV * BLOCK), F32)],
        compiler_params=pltpu.CompilerParams(
            dimension_semantics=("arbitrary", "arbitrary"), vmem_limit_bytes=VMEM_LIMIT_BYTES),
    )(sinks.astype(F32), q, k, k, k, v, v, v)


def _dot_nt_split(a, b):
    a_hi, b_hi = a.astype(BF16), b.astype(BF16)
    a_lo = (a - a_hi.astype(F32)).astype(BF16)
    b_lo = (b - b_hi.astype(F32)).astype(BF16)
    dot = functools.partial(lax.dot_general, dimension_numbers=NT_DIMS,
                            preferred_element_type=F32)
    return dot(a_hi, b_hi) + dot(a_hi, b_lo) + dot(a_lo, b_hi)


def _ssm_ops_kernel(ldt_ref, lre_ref, lim_ref, d_ref, btr_ref, bti_ref, cre_ref, cim_ref,
                    ein_ref, toep_ref, eout_ref, aq_ref):
    q, hc, p = SSM_CHUNK, SSM_CH, SSM_STATE
    gs = lre_ref.shape[1]
    g_base = pl.program_id(0) * gs
    both_dirs = lambda ref, gi: jnp.concatenate([ref[0, gi], ref[1, gi]], axis=1)
    fwd = lax.broadcasted_iota(jnp.int32, (1, 2 * p), 1) < p
    zero_row = jnp.zeros((1, 2 * p), F32)
    row_id = lax.broadcasted_iota(jnp.int32, (CHUNK_WIDTH, CHUNK_WIDTH), 0)
    col_id = lax.broadcasted_iota(jnp.int32, (CHUNK_WIDTH, CHUNK_WIDTH), 1)

    def table(select, n):
        picks = [select(m) for m in range(n)]
        re = jnp.concatenate([jnp.broadcast_to(r, (hc, 2 * p)) for r, _ in picks], axis=0)
        im = jnp.concatenate([jnp.broadcast_to(i, (hc, 2 * p)) for _, i in picks], axis=0)
        return re, im

    def tile_rows(x, n):
        return jnp.concatenate([x] * n, axis=0)

    for gi in range(gs):
        lr = jnp.minimum(jnp.concatenate([lre_ref[0, gi:gi + 1, :], lre_ref[1, gi:gi + 1, :]],
                                         axis=1), LAMBDA_RE_MAX)
        li = jnp.concatenate([lim_ref[0, gi:gi + 1, :], lim_ref[1, gi:gi + 1, :]], axis=1)
        dt = jnp.exp(jnp.where(fwd, ldt_ref[0, g_base + gi], ldt_ref[1, g_base + gi]))
        mag = jnp.exp(lr * dt)
        a_r = mag * jnp.cos(li * dt)
        a_i = mag * jnp.sin(li * dt)
        den = lr * lr + li * li
        coef_r = ((a_r - 1.0) * lr + a_i * li) / den
        coef_i = (a_i * lr - (a_r - 1.0) * li) / den
        b_r, b_i = both_dirs(btr_ref, gi), both_dirs(bti_ref, gi)
        c_r, c_i = both_dirs(cre_ref, gi), both_dirs(cim_ref, gi)
        bb_r = coef_r * b_r - coef_i * b_i
        bb_i = coef_r * b_i + coef_i * b_r

        pw = [(jnp.ones((1, 2 * p), F32), zero_row)]
        for _ in range(q):
            r, i = pw[-1]
            pw.append((r * a_r - i * a_i, r * a_i + i * a_r))

        def both(f_idx, b_idx):
            fr, fi = pw[f_idx] if f_idx is not None else (zero_row, zero_row)
            br, bi = pw[b_idx] if b_idx is not None else (zero_row, zero_row)
            return jnp.where(fwd, fr, br), jnp.where(fwd, fi, bi)

        p_r, p_i = table(lambda i: both(q - 1 - i, i), q)
        tb_r, tb_i = tile_rows(bb_r, q), tile_rows(bb_i, q)
        ein = jnp.concatenate([tb_r * p_r - tb_i * p_i, tb_r * p_i + tb_i * p_r], axis=1)
        ein_ref[gi] = ein.astype(BF16)

        p_r, p_i = table(lambda j: both(j + 1, q - j), q)
        tc_r, tc_i = tile_rows(c_r, q), tile_rows(c_i, q)
        eout = jnp.concatenate([tc_r * p_r - tc_i * p_i, -(tc_r * p_i + tc_i * p_r)], axis=1)
        eout_ref[gi] = eout.astype(BF16)

        def lag(m):
            return both(m - (q - 1) if q - 1 <= m <= 2 * q - 2 else None,
                        (q - 1) - m if m <= q - 1 else None)

        p_r, p_i = table(lag, 2 * q)
        tc_r, tc_i = tile_rows(c_r, 2 * q), tile_rows(c_i, 2 * q)
        cpt = jnp.concatenate([tc_r * p_r - tc_i * p_i, tc_r * p_i + tc_i * p_r], axis=1)
        bcat = jnp.concatenate([bb_r, -bb_i], axis=1)
        kern = _dot_nt_split(bcat, cpt)
        toep = jnp.concatenate(
            [kern[:, hc * (q - 1 - i):hc * (q - 1 - i) + CHUNK_WIDTH] for i in range(q)], axis=0)
        skip = jnp.concatenate([d_ref[gi:gi + 1, :]] * q, axis=1)
        toep_ref[gi] = (toep + jnp.where(row_id == col_id, skip, 0.0)).astype(BF16)

        aq_ref[gi] = jnp.concatenate(
            [pw[q][0], pw[q][1], jnp.zeros((SUBLANES - 2, 2 * p), F32)], axis=0)


def _ssm_operators(lam_re, lam_im, log_dt, b_re, b_im, c_re, c_im, d_skip):
    g, p, hc = SSM_GROUPS, SSM_STATE, SSM_CH
    f32 = lambda a: a.astype(F32)
    channel_major = lambda b: f32(b).transpose(0, 1, 3, 2)
    gs = OPS_GROUPS_PER_STEP
    mat = pl.BlockSpec((gs, CHUNK_WIDTH, CHUNK_WIDTH), lambda s: (s, 0, 0))
    mat_shape = jax.ShapeDtypeStruct((g, CHUNK_WIDTH, CHUNK_WIDTH), BF16)
    per_dir = lambda *tail: pl.BlockSpec((2, gs) + tail, lambda s: (0, s) + (0,) * len(tail))
    return pl.pallas_call(
        _ssm_ops_kernel,
        grid=(g // gs,),
        in_specs=[pl.BlockSpec(memory_space=pltpu.SMEM), per_dir(p), per_dir(p),
                  pl.BlockSpec((gs, hc), lambda s: (s, 0)),
                  per_dir(hc, p), per_dir(hc, p), per_dir(hc, p), per_dir(hc, p)],
        out_specs=[mat, mat, mat, pl.BlockSpec((gs, SUBLANES, 2 * p), lambda s: (s, 0, 0))],
        out_shape=[mat_shape, mat_shape, mat_shape,
                   jax.ShapeDtypeStruct((g, SUBLANES, 2 * p), F32)],
        compiler_params=pltpu.CompilerParams(
            dimension_semantics=("parallel",), vmem_limit_bytes=VMEM_LIMIT_BYTES),
    )(f32(log_dt), f32(lam_re), f32(lam_im), f32(d_skip), channel_major(b_re), channel_major(b_im),
      f32(c_re), f32(c_im))


def _lane_roll(x, shift):
    shift %= LANES
    return jnp.concatenate([x[:, LANES - shift:], x[:, :LANES - shift]], axis=1)


def _block_transpose(v, lane_block):
    return _block_transposes([v], lane_block)[0]


def _block_transposes(groups, lane_block):
    n = len(groups[0])
    skewed = [[v[i] if i == 0 else _lane_roll(v[i], SSM_CH * i) for i in range(n)] for v in groups]
    picked = []
    for sk in skewed:
        rows = []
        for g in range(n):
            p = sk[(0 - g) % n]
            for j in range(1, n):
                p = jnp.where(lane_block == j, sk[(j - g) % n], p)
            rows.append(p)
        picked.append(rows)
    return [[p[g] if g == 0 else _lane_roll(p[g], -SSM_CH * g) for g in range(n)] for p in picked]


def _ssm_kernel(u_hbm, ein_ref, toep_ref, eout_ref, aq_ref, y_hbm,
                io_ref, ug_ref, s_ref, x_ref, in_sem, out_sem, *, batch, seq, pitch):
    q, half = SSM_CHUNK, SSM_STATE
    n_chunks = seq // q
    gpb = GROUPS_PER_BLOCK
    slab = 2 * batch
    lane_block = lax.broadcasted_iota(jnp.int32, (slab, LANES), 1) // SSM_CH
    step_id = pl.program_id(0)
    last_step = pl.num_programs(0) - 1
    slot = lax.rem(step_id, 2)

    n_slices = SSM_DMA_SLICES
    slice_rows = seq // n_slices
    slice_trips = n_chunks // (2 * n_slices)

    def column_copies(block, slot_, to_vmem, slices=range(SSM_DMA_SLICES)):
        lanes = pl.ds(pl.multiple_of(block * LANES, LANES), LANES)
        copies = []
        for k in slices:
            for b in range(batch):
                hbm = (u_hbm if to_vmem else y_hbm).at[b, pl.ds(k * slice_rows, slice_rows), lanes]
                vmem = io_ref.at[slot_, pl.ds(b * pitch + k * slice_rows, slice_rows), :]
                if to_vmem:
                    copies.append(pltpu.make_async_copy(hbm, vmem, in_sem.at[slot_, k, b]))
                else:
                    copies.append(pltpu.make_async_copy(vmem, hbm, out_sem.at[slot_, k, b]))
        return copies

    def start(copies):
        for copy in copies:
            copy.start()

    def wait(copies):
        for copy in copies:
            copy.wait()

    @pl.when(step_id == 0)
    def _():
        start(column_copies(0, 0, True))

    @pl.when(step_id < last_step)
    def _():
        @pl.when(step_id >= 1)
        def _():
            wait(column_copies(step_id - 1, 1 - slot, False))
        start(column_copies(step_id + 1, 1 - slot, True))

    buf = io_ref.at[slot]

    def gather(cp, carry):
        rows = pl.ds(pl.multiple_of(cp * slab, slab), slab)
        parts = []
        for part in range(q // gpb):
            t0 = cp * 2 * q + part * gpb
            parts.append([jnp.concatenate([buf[pl.ds(t0 + i, batch, stride=pitch), :],
                                           buf[pl.ds(t0 + q + i, batch, stride=pitch), :]],
                                          axis=0).astype(BF16) for i in range(gpb)])
        for part, w in enumerate(_block_transposes(parts, lane_block)):
            for g in range(gpb):
                ug_ref[g, rows, part * LANES:(part + 1) * LANES] = w[g]
        return carry

    for k in range(n_slices):
        wait(column_copies(step_id, slot, True, slices=(k,)))
        lax.fori_loop(k * slice_trips, (k + 1) * slice_trips, gather, 0,
                      unroll=SSM_RELAYOUT_UNROLL)

    fwd_lane = lax.broadcasted_iota(jnp.int32, (batch, 2 * half), 1) < half
    zeros = jnp.zeros((batch, half), F32)
    last_rows = pl.ds((n_chunks - 1) * batch, batch)
    ni = SSM_INTERLEAVE
    for g0 in range(0, gpb, ni):
        for gi in range(ni):
            s_ref[gi] = jnp.dot(ug_ref[g0 + gi], ein_ref[g0 + gi], preferred_element_type=F32)
            x_ref[gi, 0:batch, 0:half] = zeros
            x_ref[gi, 0:batch, 2 * half:3 * half] = zeros
            x_ref[gi, last_rows, half:2 * half] = zeros
            x_ref[gi, last_rows, 3 * half:4 * half] = zeros

        def step(k, carry):
            rf = pl.multiple_of(k * batch, batch)
            rb = pl.multiple_of((n_chunks - 1 - k) * batch, batch)
            new = []
            for gi in range(ni):
                xr, xi = carry[gi]
                sre = jnp.where(fwd_lane, s_ref[gi, pl.ds(rf, batch), 0:2 * half],
                                s_ref[gi, pl.ds(rb, batch), 0:2 * half])
                sim = jnp.where(fwd_lane, s_ref[gi, pl.ds(rf, batch), 2 * half:4 * half],
                                s_ref[gi, pl.ds(rb, batch), 2 * half:4 * half])
                ar = aq_ref[g0 + gi, 0:1, :]
                ai = aq_ref[g0 + gi, 1:2, :]
                nr = ar * xr - ai * xi + sre
                nim = ar * xi + ai * xr + sim
                x_ref[gi, pl.ds(rf + batch, batch), 0:half] = nr[:, 0:half]
                x_ref[gi, pl.ds(rf + batch, batch), 2 * half:3 * half] = nim[:, 0:half]
                x_ref[gi, pl.ds(rb - batch, batch), half:2 * half] = nr[:, half:2 * half]
                x_ref[gi, pl.ds(rb - batch, batch), 3 * half:4 * half] = nim[:, half:2 * half]
                new.append((nr, nim))
            return tuple(new)

        init = tuple((jnp.zeros((batch, 2 * half), F32), jnp.zeros((batch, 2 * half), F32))
                     for _ in range(ni))
        lax.fori_loop(0, n_chunks - 1, step, init)

        for gi in range(ni):
            g = g0 + gi
            y = jnp.dot(ug_ref[g], toep_ref[g], preferred_element_type=F32)
            y += lax.dot_general(x_ref[gi].astype(BF16), eout_ref[g], NT_DIMS,
                                 preferred_element_type=F32)
            ug_ref[g] = y.astype(BF16)

    def scatter(cp, carry):
        rows = pl.ds(pl.multiple_of(cp * slab, slab), slab)
        parts = [[ug_ref[g, rows, part * LANES:(part + 1) * LANES] for g in range(gpb)]
                 for part in range(q // gpb)]
        for part, w in enumerate(_block_transposes(parts, lane_block)):
            t0 = cp * 2 * q + part * gpb
            for j in range(gpb):
                wj = w[j].astype(F32)
                buf[pl.ds(t0 + j, batch, stride=pitch), :] = wj[0:batch]
                buf[pl.ds(t0 + q + j, batch, stride=pitch), :] = wj[batch:slab]
        return carry

    for k in range(n_slices):
        lax.fori_loop(k * slice_trips, (k + 1) * slice_trips, scatter, 0,
                      unroll=SSM_RELAYOUT_UNROLL)
        start(column_copies(step_id, slot, False, slices=(k,)))

    @pl.when(step_id == last_step)
    def _():
        @pl.when(step_id >= 1)
        def _():
            wait(column_copies(step_id - 1, 1 - slot, False))
        wait(column_copies(step_id, slot, False))


def _padded_seq(seq):
    return seq + SUBLANES if seq % (2 * SUBLANES) == 0 else seq


def _ssm(u, lam_re, lam_im, log_dt, b_re, b_im, c_re, c_im, d_skip):
    batch, seq, _ = u.shape
    pitch = _padded_seq(seq)
    rows = batch * seq // SSM_CHUNK
    ein, toep, eout, a_q = _ssm_operators(lam_re, lam_im, log_dt, b_re, b_im, c_re, c_im, d_skip)
    gpb = GROUPS_PER_BLOCK
    mat = pl.BlockSpec((gpb, CHUNK_WIDTH, CHUNK_WIDTH), lambda s: (s, 0, 0))
    hbm = pl.BlockSpec(memory_space=pl.ANY)
    return pl.pallas_call(
        functools.partial(_ssm_kernel, batch=batch, seq=seq, pitch=pitch),
        grid=(SSM_GROUPS // gpb,),
        in_specs=[hbm, mat, mat, mat,
                  pl.BlockSpec((gpb, SUBLANES, 2 * SSM_STATE), lambda s: (s, 0, 0))],
        out_specs=hbm,
        out_shape=jax.ShapeDtypeStruct((batch, seq, SSM_WIDTH), F32),
        scratch_shapes=[pltpu.VMEM((2, batch * pitch, LANES), F32),
                        pltpu.VMEM((gpb, rows, CHUNK_WIDTH), BF16),
                        pltpu.VMEM((SSM_INTERLEAVE, rows, 4 * SSM_STATE), F32),
                        pltpu.VMEM((SSM_INTERLEAVE, rows, 4 * SSM_STATE), F32),
                        pltpu.SemaphoreType.DMA((2, SSM_DMA_SLICES, batch)),
                        pltpu.SemaphoreType.DMA((2, SSM_DMA_SLICES, batch))],
        compiler_params=pltpu.CompilerParams(
            dimension_semantics=("arbitrary",), vmem_limit_bytes=VMEM_LIMIT_BYTES),
    )(u, ein, toep, eout, a_q)


def kernel(x, norm_ffn1, ffn1_w_gate, ffn1_w_up, ffn1_w_down, norm_mix, w_in, attn_sinks,
           ssm_lambda_re, ssm_lambda_im, ssm_log_dt, ssm_b_re, ssm_b_im, ssm_c_re, ssm_c_im,
           ssm_d, ssm_glu_w, ssm_glu_b, attn_out_norm, ssm_out_norm, w_out,
           norm_ffn2, ffn2_w_gate, ffn2_w_up, ffn2_w_down, final_norm):
    b, seq, d = x.shape
    depth = norm_ffn1.shape[0]
    assert d == D_MODEL and seq % BLOCK == 0 and (b * seq) % FFN_TOKEN_TILE == 0
    assert b == SUBLANES and seq % (2 * SSM_CHUNK * SSM_RELAYOUT_UNROLL * SSM_DMA_SLICES) == 0
    h = x.reshape(b * seq, d).astype(F32)
    for l in range(depth):
        h, q, k, v, u = _ffn(h, norm_ffn1[l], ffn1_w_gate[l], ffn1_w_up[l], ffn1_w_down[l],
                             final_norm, final_norm=False, proj=(norm_mix[l], w_in[l]))
        attn = _attention(q.reshape(b, seq, -1), k.reshape(b, seq, -1), v.reshape(b, seq, -1),
                          attn_sinks[l])
        ssm_pre = _ssm(u.reshape(b, seq, -1), ssm_lambda_re[l], ssm_lambda_im[l], ssm_log_dt[l],
                       ssm_b_re[l], ssm_b_im[l], ssm_c_re[l], ssm_c_im[l], ssm_d[l])
        mix = (attn.reshape(b * seq, -1), ssm_pre.reshape(b * seq, -1), ssm_glu_w[l],
               ssm_glu_b[l], attn_out_norm[l], ssm_out_norm[l], w_out[l])
        h, = _ffn(h, norm_ffn2[l], ffn2_w_gate[l], ffn2_w_up[l], ffn2_w_down[l],
                  final_norm, final_norm=(l == depth - 1), mix=mix)
    return h.reshape(b, seq, d).astype(x.dtype)
```

```python
import functools
import math

import jax
import jax.numpy as jnp
from jax import lax
from jax.experimental import pallas as pl
from jax.experimental.pallas import tpu as pltpu

F32 = jnp.float32
BF16 = jnp.bfloat16

D_MODEL = 1024
ATTN_HEADS = 8
ATTN_KV_HEADS = 2
Q_PER_KV = ATTN_HEADS // ATTN_KV_HEADS
HEAD_DIM = 64
ATTN_WIDTH = ATTN_HEADS * HEAD_DIM
KV_WIDTH = ATTN_KV_HEADS * HEAD_DIM
WINDOW = 128
BLOCK = 128
SSM_CH = 16
SSM_WIDTH = D_MODEL - ATTN_WIDTH
SSM_GROUPS = SSM_WIDTH // SSM_CH
SSM_STATE = 64
IN_WIDTH = ATTN_WIDTH + 2 * KV_WIDTH + SSM_WIDTH
D_FF = 2816
EPS = 1e-6
NEG_INF = -1e30
LAMBDA_RE_MAX = -1e-4
LOG2_E = math.log2(math.e)
QUERY_SCALE = HEAD_DIM ** -0.5 * LOG2_E

LANES = 128
SUBLANES = 8
VMEM_LIMIT_BYTES = 56 * 1024 * 1024

FFN_TOKEN_TILE = 512
FFN_SUBTILES = 2
FF_TILE = 256
ATTN_BLOCKS_PER_STEP = 8
ATTN_HEADS_PER_PV = 2
SSM_CHUNK = 16
CHUNK_WIDTH = SSM_CHUNK * SSM_CH
GROUPS_PER_BLOCK = LANES // SSM_CH
SSM_INTERLEAVE = 8
SSM_RELAYOUT_UNROLL = 4
SSM_DMA_SLICES = 8
OPS_GROUPS_PER_STEP = 8

NT_DIMS = (((1,), (1,)), ((), ()))
TN_DIMS = (((0,), (0,)), ((), ()))


def _rms(x):
    return x * lax.rsqrt(jnp.mean(x * x, axis=-1, keepdims=True) + EPS)


def _mixed_update(rows, attn_ref, ssm_ref, gw_ref, gb_ref, ga_ref, gs_ref, wo_ref):
    y = jax.nn.gelu(ssm_ref[rows, :])
    z = jnp.dot(y.astype(BF16), gw_ref[...].astype(BF16), preferred_element_type=F32) + gb_ref[...]
    s = y * jax.nn.sigmoid(z)
    sn = _rms(s) * gs_ref[...]
    an = _rms(attn_ref[rows, :].astype(F32)) * ga_ref[...]
    mixed = jnp.concatenate([an, sn], axis=-1).astype(BF16)
    return jnp.dot(mixed, wo_ref[...].astype(BF16), preferred_element_type=F32)


def _mixer_inputs(y, rows, gain_ref, w_ref, q_ref, k_ref, v_ref, u_ref):
    hn = (_rms(y) * gain_ref[...]).astype(BF16)
    proj = jnp.dot(hn, w_ref[...].astype(BF16), preferred_element_type=F32)
    q_ref[rows, :] = (proj[:, :ATTN_WIDTH] * QUERY_SCALE).astype(BF16)
    k_ref[rows, :] = proj[:, ATTN_WIDTH:ATTN_WIDTH + KV_WIDTH].astype(BF16)
    v_ref[rows, :] = proj[:, ATTN_WIDTH + KV_WIDTH:ATTN_WIDTH + 2 * KV_WIDTH].astype(BF16)
    u_ref[rows, :] = proj[:, ATTN_WIDTH + 2 * KV_WIDTH:]


def _ffn_body(read_x, gain_ref, fgain_ref, wg_hbm, wu_hbm, wd_hbm, o_ref,
              wg_ref, wu_ref, wd_ref, act_ref, gu_stage, d_stage, sem, *, final_norm,
              epilogue=None):
    nj = D_FF // FF_TILE

    def weight_copies(j, slot):
        span = pl.ds(j * FF_TILE, FF_TILE)
        return (pltpu.make_async_copy(wg_hbm.at[:, span], gu_stage.at[0, slot], sem.at[0, slot]),
                pltpu.make_async_copy(wu_hbm.at[:, span], gu_stage.at[1, slot], sem.at[1, slot]),
                pltpu.make_async_copy(wd_hbm.at[span, :], d_stage.at[slot], sem.at[2, slot]))

    def step(stage_weights):
        if stage_weights:
            for copy in weight_copies(0, 0):
                copy.start()
        sub = o_ref.shape[0] // FFN_SUBTILES
        row_groups = [slice(h * sub, (h + 1) * sub) for h in range(FFN_SUBTILES)]
        xs = [read_x(rows) for rows in row_groups]
        hns = [(_rms(x) * gain_ref[...]).astype(BF16) for x in xs]
        for j in range(nj):
            cols = slice(j * FF_TILE, (j + 1) * FF_TILE)
            if stage_weights:
                slot = j % 2
                if j + 1 < nj:
                    for copy in weight_copies(j + 1, 1 - slot):
                        copy.start()
                for copy in weight_copies(j, slot):
                    copy.wait()
                wg_ref[:, cols] = gu_stage[0, slot].astype(BF16)
                wu_ref[:, cols] = gu_stage[1, slot].astype(BF16)
                wd_ref[cols, :] = d_stage[slot].astype(BF16)
            for rows, hn in zip(row_groups, hns):
                g = jnp.dot(hn, wg_ref[:, cols], preferred_element_type=F32)
                u = jnp.dot(hn, wu_ref[:, cols], preferred_element_type=F32)
                act_ref[rows, cols] = (g * jax.nn.sigmoid(g) * u).astype(BF16)
        for rows, x in zip(row_groups, xs):
            y = x + 0.5 * jnp.dot(act_ref[rows, :], wd_ref[...], preferred_element_type=F32)
            if final_norm:
                y = _rms(y) * fgain_ref[...]
            o_ref[rows, :] = y
            if epilogue is not None:
                epilogue(y, rows)

    pl.when(pl.program_id(0) == 0)(functools.partial(step, True))
    pl.when(pl.program_id(0) != 0)(functools.partial(step, False))


def _ffn_proj_kernel(x_ref, mgain_ref, win_ref, gain_ref, fgain_ref, wg_hbm, wu_hbm, wd_hbm,
                     o_ref, q_ref, k_ref, v_ref, u_ref, *scratch, final_norm):
    emit = lambda y, rows: _mixer_inputs(y, rows, mgain_ref, win_ref, q_ref, k_ref, v_ref, u_ref)
    _ffn_body(lambda rows: x_ref[rows, :], gain_ref, fgain_ref, wg_hbm, wu_hbm, wd_hbm, o_ref,
              *scratch, final_norm=final_norm, epilogue=emit)


def _mix_ffn_kernel(x_ref, attn_ref, ssm_ref, gw_ref, gb_ref, ga_ref, gs_ref, wo_ref, *ffn_refs,
                    final_norm):
    read_x = lambda rows: x_ref[rows, :] + _mixed_update(rows, attn_ref, ssm_ref, gw_ref, gb_ref,
                                                         ga_ref, gs_ref, wo_ref)
    _ffn_body(read_x, *ffn_refs, final_norm=final_norm)


def _ffn(x, gain, w_gate, w_up, w_down, final_gain, final_norm, mix=None, proj=None):
    assert (mix is None) != (proj is None)
    t = x.shape[0]
    tile = FFN_TOKEN_TILE
    vec = lambda width: pl.BlockSpec((1, width), lambda i: (0, 0))
    resident = lambda r, c: pl.BlockSpec((r, c), lambda i: (0, 0), pipeline_mode=pl.Buffered(1))
    row = lambda width: pl.BlockSpec((tile, width), lambda i: (i, 0))
    hbm = pl.BlockSpec(memory_space=pl.ANY)
    in_specs, args = [row(D_MODEL)], [x]
    out_specs, out_shape = [row(D_MODEL)], [jax.ShapeDtypeStruct((t, D_MODEL), F32)]
    if mix is not None:
        attn, ssm_pre, glu_w, glu_b, attn_gain, ssm_gain, w_out = mix
        body = _mix_ffn_kernel
        in_specs += [row(ATTN_WIDTH), row(SSM_WIDTH),
                     resident(SSM_WIDTH, SSM_WIDTH), vec(SSM_WIDTH), vec(ATTN_WIDTH),
                     vec(SSM_WIDTH), resident(D_MODEL, D_MODEL)]
        args += [attn, ssm_pre, glu_w, glu_b.reshape(1, -1).astype(F32),
                 attn_gain.reshape(1, -1).astype(F32), ssm_gain.reshape(1, -1).astype(F32), w_out]
    if proj is not None:
        mixer_gain, w_in = proj
        body = _ffn_proj_kernel
        in_specs += [vec(D_MODEL), resident(D_MODEL, IN_WIDTH)]
        args += [mixer_gain.reshape(1, D_MODEL), w_in]
        for width, dtype in ((ATTN_WIDTH, BF16), (KV_WIDTH, BF16), (KV_WIDTH, BF16),
                             (SSM_WIDTH, F32)):
            out_specs.append(row(width))
            out_shape.append(jax.ShapeDtypeStruct((t, width), dtype))
    in_specs += [vec(D_MODEL), vec(D_MODEL), hbm, hbm, hbm]
    args += [gain.reshape(1, D_MODEL), final_gain.reshape(1, D_MODEL), w_gate, w_up, w_down]
    return pl.pallas_call(
        functools.partial(body, final_norm=final_norm),
        grid=(t // tile,),
        in_specs=in_specs,
        out_specs=out_specs,
        out_shape=out_shape,
        scratch_shapes=[pltpu.VMEM((D_MODEL, D_FF), BF16),
                        pltpu.VMEM((D_MODEL, D_FF), BF16),
                        pltpu.VMEM((D_FF, D_MODEL), BF16),
                        pltpu.VMEM((tile, D_FF), BF16),
                        pltpu.VMEM((2, 2, D_MODEL, FF_TILE), F32),
                        pltpu.VMEM((2, FF_TILE, D_MODEL), F32),
                        pltpu.SemaphoreType.DMA((3, 2))],
        compiler_params=pltpu.CompilerParams(
            dimension_semantics=("arbitrary",), vmem_limit_bytes=VMEM_LIMIT_BYTES),
    )(*args)


def _attn_kernel(sink_ref, q_ref, kp_ref, kc_ref, kn_ref, vp_ref, vc_ref, vn_ref, o_ref,
                 bias_ref, k_ref, v_ref, s_ref):
    n = pl.program_id(1)
    last = pl.num_programs(1) - 1
    nblk = ATTN_BLOCKS_PER_STEP

    @pl.when((pl.program_id(0) == 0) & (n == 0))
    def _():
        kj = lax.broadcasted_iota(jnp.int32, (3 * BLOCK, BLOCK), 0)
        qi = lax.broadcasted_iota(jnp.int32, (3 * BLOCK, BLOCK), 1)
        rel = jnp.abs(kj - BLOCK - qi)
        dist = rel.astype(F32)
        inside = rel <= WINDOW
        has_prev = kj >= BLOCK
        has_next = kj < 2 * BLOCK
        for variant, ok in enumerate((inside & has_prev, inside, inside & has_next)):
            for h in range(ATTN_HEADS):
                slope = float(2.0 ** (-8.0 * (h + 1) / ATTN_HEADS))
                bias_ref[variant, h] = jnp.where(ok, (-slope * LOG2_E) * dist, NEG_INF)
        for kh in range(ATTN_KV_HEADS):
            v_ref[:, (2 * kh + 1) * HEAD_DIM:(2 * kh + 2) * HEAD_DIM] = jnp.ones(
                (v_ref.shape[0], HEAD_DIM), BF16)

    spans = ((slice(0, BLOCK), kp_ref, vp_ref),
             (slice(BLOCK, (nblk + 1) * BLOCK), kc_ref, vc_ref),
             (slice((nblk + 1) * BLOCK, (nblk + 2) * BLOCK), kn_ref, vn_ref))
    for rows, k_in, v_in in spans:
        k_ref[rows, :] = k_in[...]
        for kh in range(ATTN_KV_HEADS):
            v_ref[rows, 2 * kh * HEAD_DIM:(2 * kh + 1) * HEAD_DIM] = (
                v_in[:, kh * HEAD_DIM:(kh + 1) * HEAD_DIM])

    def scores(j, kh):
        kcat = k_ref[j * BLOCK:(j + 3) * BLOCK, kh * HEAD_DIM:(kh + 1) * HEAD_DIM]
        heads = [kh * Q_PER_KV + g for g in range(Q_PER_KV)]
        qs = jnp.concatenate(
            [q_ref[j * BLOCK:(j + 1) * BLOCK, h * HEAD_DIM:(h + 1) * HEAD_DIM] for h in heads],
            axis=0)
        s_ref[j, kh] = lax.dot_general(kcat, qs, NT_DIMS,
                                       preferred_element_type=F32)

    for kh in range(ATTN_KV_HEADS):
        scores(0, kh)
    for j in range(nblk):
        variant = 1
        if j == 0:
            variant = jnp.where(n == 0, 0, variant)
        if j == nblk - 1:
            variant = jnp.where(n == last, 2, variant)
        outs = []
        for kh in range(ATTN_KV_HEADS):
            if j + 1 < nblk:
                scores(j + 1, kh)
            v_ones = v_ref[j * BLOCK:(j + 3) * BLOCK, 2 * kh * HEAD_DIM:(2 * kh + 2) * HEAD_DIM]
            probs, sink_terms = [], []
            for g in range(Q_PER_KV):
                h = kh * Q_PER_KV + g
                s = s_ref[j, kh, :, g * BLOCK:(g + 1) * BLOCK] + bias_ref[variant, h]
                sink = sink_ref[h] * LOG2_E
                m = jnp.maximum(jnp.max(s, axis=0, keepdims=True), sink)
                probs.append(jnp.exp2(s - m).astype(BF16))
                sink_terms.append(jnp.exp2(sink - m))
            hp = ATTN_HEADS_PER_PV
            for g in range(0, Q_PER_KV, hp):
                pv = lax.dot_general(v_ones, jnp.concatenate(probs[g:g + hp], axis=1), TN_DIMS,
                                     preferred_element_type=F32)
                for t in range(hp):
                    cols = slice(t * BLOCK, (t + 1) * BLOCK)
                    den = pv[HEAD_DIM:HEAD_DIM + 1, cols] + sink_terms[g + t]
                    outs.append(pv[:HEAD_DIM, cols] / den)
        o_ref[j * BLOCK:(j + 1) * BLOCK, :] = jnp.concatenate(outs, axis=0).T.astype(o_ref.dtype)


def _attention(q, k, v, sinks):
    b, seq, _ = q.shape
    nblk = ATTN_BLOCKS_PER_STEP
    steps = seq // (nblk * BLOCK)
    nb = seq // BLOCK
    assert seq % (nblk * BLOCK) == 0 and nb >= 2
    edge = lambda f: pl.BlockSpec((None, BLOCK, KV_WIDTH), f)
    body = pl.BlockSpec((None, nblk * BLOCK, KV_WIDTH), lambda bi, n: (bi, n, 0))
    prev = lambda bi, n: (bi, jnp.maximum(n * nblk - 1, 0), 0)
    nxt = lambda bi, n: (bi, jnp.minimum((n + 1) * nblk, nb - 1), 0)
    rows = pl.BlockSpec((None, nblk * BLOCK, ATTN_WIDTH), lambda bi, n: (bi, n, 0))
    return pl.pallas_call(
        _attn_kernel,
        grid=(b, steps),
        in_specs=[pl.BlockSpec(memory_space=pltpu.SMEM), rows,
                  edge(prev), body, edge(nxt), edge(prev), body, edge(nxt)],
        out_specs=rows,
        out_shape=jax.ShapeDtypeStruct((b, seq, ATTN_WIDTH), BF16),
        scratch_shapes=[pltpu.VMEM((3, ATTN_HEADS, 3 * BLOCK, BLOCK), F32),
                        pltpu.VMEM(((nblk + 2) * BLOCK, KV_WIDTH), BF16),
                        pltpu.VMEM(((nblk + 2) * BLOCK, 2 * KV_WIDTH), BF16),
                        pltpu.VMEM((nblk, ATTN_KV_HEADS, 3 * BLOCK, Q_PER_KV * BLOCK), F32)],
        compiler_params=pltpu.CompilerParams(
            dimension_semantics=("arbitrary", "arbitrary"), vmem_limit_bytes=VMEM_LIMIT_BYTES),
    )(sinks.astype(F32), q, k, k, k, v, v, v)


def _dot_nt_split(a, b):
    a_hi, b_hi = a.astype(BF16), b.astype(BF16)
    a_lo = (a - a_hi.astype(F32)).astype(BF16)
    b_lo = (b - b_hi.astype(F32)).astype(BF16)
    dot = functools.partial(lax.dot_general, dimension_numbers=NT_DIMS,
                            preferred_element_type=F32)
    return dot(a_hi, b_hi) + dot(a_hi, b_lo) + dot(a_lo, b_hi)


def _ssm_ops_kernel(ldt_ref, lre_ref, lim_ref, d_ref, btr_ref, bti_ref, cre_ref, cim_ref,
                    ein_ref, toep_ref, eout_ref, aq_ref):
    q, hc, p = SSM_CHUNK, SSM_CH, SSM_STATE
    gs = lre_ref.shape[1]
    g_base = pl.program_id(0) * gs
    both_dirs = lambda ref, gi: jnp.concatenate([ref[0, gi], ref[1, gi]], axis=1)
    fwd = lax.broadcasted_iota(jnp.int32, (1, 2 * p), 1) < p
    zero_row = jnp.zeros((1, 2 * p), F32)
    row_id = lax.broadcasted_iota(jnp.int32, (CHUNK_WIDTH, CHUNK_WIDTH), 0)
    col_id = lax.broadcasted_iota(jnp.int32, (CHUNK_WIDTH, CHUNK_WIDTH), 1)

    def table(select, n):
        picks = [select(m) for m in range(n)]
        re = jnp.concatenate([jnp.broadcast_to(r, (hc, 2 * p)) for r, _ in picks], axis=0)
        im = jnp.concatenate([jnp.broadcast_to(i, (hc, 2 * p)) for _, i in picks], axis=0)
        return re, im

    def tile_rows(x, n):
        return jnp.concatenate([x] * n, axis=0)

    for gi in range(gs):
        lr = jnp.minimum(jnp.concatenate([lre_ref[0, gi:gi + 1, :], lre_ref[1, gi:gi + 1, :]],
                                         axis=1), LAMBDA_RE_MAX)
        li = jnp.concatenate([lim_ref[0, gi:gi + 1, :], lim_ref[1, gi:gi + 1, :]], axis=1)
        dt = jnp.exp(jnp.where(fwd, ldt_ref[0, g_base + gi], ldt_ref[1, g_base + gi]))
        mag = jnp.exp(lr * dt)
        a_r = mag * jnp.cos(li * dt)
        a_i = mag * jnp.sin(li * dt)
        den = lr * lr + li * li
        coef_r = ((a_r - 1.0) * lr + a_i * li) / den
        coef_i = (a_i * lr - (a_r - 1.0) * li) / den
        b_r, b_i = both_dirs(btr_ref, gi), both_dirs(bti_ref, gi)
        c_r, c_i = both_dirs(cre_ref, gi), both_dirs(cim_ref, gi)
        bb_r = coef_r * b_r - coef_i * b_i
        bb_i = coef_r * b_i + coef_i * b_r

        pw = [(jnp.ones((1, 2 * p), F32), zero_row)]
        for _ in range(q):
            r, i = pw[-1]
            pw.append((r * a_r - i * a_i, r * a_i + i * a_r))

        def both(f_idx, b_idx):
            fr, fi = pw[f_idx] if f_idx is not None else (zero_row, zero_row)
            br, bi = pw[b_idx] if b_idx is not None else (zero_row, zero_row)
            return jnp.where(fwd, fr, br), jnp.where(fwd, fi, bi)

        p_r, p_i = table(lambda i: both(q - 1 - i, i), q)
        tb_r, tb_i = tile_rows(bb_r, q), tile_rows(bb_i, q)
        ein = jnp.concatenate([tb_r * p_r - tb_i * p_i, tb_r * p_i + tb_i * p_r], axis=1)
        ein_ref[gi] = ein.astype(BF16)

        p_r, p_i = table(lambda j: both(j + 1, q - j), q)
        tc_r, tc_i = tile_rows(c_r, q), tile_rows(c_i, q)
        eout = jnp.concatenate([tc_r * p_r - tc_i * p_i, -(tc_r * p_i + tc_i * p_r)], axis=1)
        eout_ref[gi] = eout.astype(BF16)

        def lag(m):
            return both(m - (q - 1) if q - 1 <= m <= 2 * q - 2 else None,
                        (q - 1) - m if m <= q - 1 else None)

        p_r, p_i = table(lag, 2 * q)
        tc_r, tc_i = tile_rows(c_r, 2 * q), tile_rows(c_i, 2 * q)
        cpt = jnp.concatenate([tc_r * p_r - tc_i * p_i, tc_r * p_i + tc_i * p_r], axis=1)
        bcat = jnp.concatenate([bb_r, -bb_i], axis=1)
        kern = _dot_nt_split(bcat, cpt)
        toep = jnp.concatenate(
            [kern[:, hc * (q - 1 - i):hc * (q - 1 - i) + CHUNK_WIDTH] for i in range(q)], axis=0)
        skip = jnp.concatenate([d_ref[gi:gi + 1, :]] * q, axis=1)
        toep_ref[gi] = (toep + jnp.where(row_id == col_id, skip, 0.0)).astype(BF16)

        aq_ref[gi] = jnp.concatenate(
            [pw[q][0], pw[q][1], jnp.zeros((SUBLANES - 2, 2 * p), F32)], axis=0)


def _ssm_operators(lam_re, lam_im, log_dt, b_re, b_im, c_re, c_im, d_skip):
    g, p, hc = SSM_GROUPS, SSM_STATE, SSM_CH
    f32 = lambda a: a.astype(F32)
    channel_major = lambda b: f32(b).transpose(0, 1, 3, 2)
    gs = OPS_GROUPS_PER_STEP
    mat = pl.BlockSpec((gs, CHUNK_WIDTH, CHUNK_WIDTH), lambda s: (s, 0, 0))
    mat_shape = jax.ShapeDtypeStruct((g, CHUNK_WIDTH, CHUNK_WIDTH), BF16)
    per_dir = lambda *tail: pl.BlockSpec((2, gs) + tail, lambda s: (0, s) + (0,) * len(tail))
    return pl.pallas_call(
        _ssm_ops_kernel,
        grid=(g // gs,),
        in_specs=[pl.BlockSpec(memory_space=pltpu.SMEM), per_dir(p), per_dir(p),
                  pl.BlockSpec((gs, hc), lambda s: (s, 0)),
                  per_dir(hc, p), per_dir(hc, p), per_dir(hc, p), per_dir(hc, p)],
        out_specs=[mat, mat, mat, pl.BlockSpec((gs, SUBLANES, 2 * p), lambda s: (s, 0, 0))],
        out_shape=[mat_shape, mat_shape, mat_shape,
                   jax.ShapeDtypeStruct((g, SUBLANES, 2 * p), F32)],
        compiler_params=pltpu.CompilerParams(
            dimension_semantics=("parallel",), vmem_limit_bytes=VMEM_LIMIT_BYTES),
    )(f32(log_dt), f32(lam_re), f32(lam_im), f32(d_skip), channel_major(b_re), channel_major(b_im),
      f32(c_re), f32(c_im))


def _lane_roll(x, shift):
    shift %= LANES
    return jnp.concatenate([x[:, LANES - shift:], x[:, :LANES - shift]], axis=1)


def _block_transpose(v, lane_block):
    return _block_transposes([v], lane_block)[0]


def _block_transposes(groups, lane_block):
    n = len(groups[0])
    skewed = [[v[i] if i == 0 else _lane_roll(v[i], SSM_CH * i) for i in range(n)] for v in groups]
    picked = []
    for sk in skewed:
        rows = []
        for g in range(n):
            p = sk[(0 - g) % n]
            for j in range(1, n):
                p = jnp.where(lane_block == j, sk[(j - g) % n], p)
            rows.append(p)
        picked.append(rows)
    return [[p[g] if g == 0 else _lane_roll(p[g], -SSM_CH * g) for g in range(n)] for p in picked]


def _ssm_kernel(u_hbm, ein_ref, toep_ref, eout_ref, aq_ref, y_hbm,
                io_ref, ug_ref, s_ref, x_ref, in_sem, out_sem, *, batch, seq, pitch):
    q, half = SSM_CHUNK, SSM_STATE
    n_chunks = seq // q
    gpb = GROUPS_PER_BLOCK
    slab = 2 * batch
    lane_block = lax.broadcasted_iota(jnp.int32, (slab, LANES), 1) // SSM_CH
    step_id = pl.program_id(0)
    last_step = pl.num_programs(0) - 1
    slot = lax.rem(step_id, 2)

    n_slices = SSM_DMA_SLICES
    slice_rows = seq // n_slices
    slice_trips = n_chunks // (2 * n_slices)

    def column_copies(block, slot_, to_vmem, slices=range(SSM_DMA_SLICES)):
        lanes = pl.ds(pl.multiple_of(block * LANES, LANES), LANES)
        copies = []
        for k in slices:
            for b in range(batch):
                hbm = (u_hbm if to_vmem else y_hbm).at[b, pl.ds(k * slice_rows, slice_rows), lanes]
                vmem = io_ref.at[slot_, pl.ds(b * pitch + k * slice_rows, slice_rows), :]
                if to_vmem:
                    copies.append(pltpu.make_async_copy(hbm, vmem, in_sem.at[slot_, k, b]))
                else:
                    copies.append(pltpu.make_async_copy(vmem, hbm, out_sem.at[slot_, k, b]))
        return copies

    def start(copies):
        for copy in copies:
            copy.start()

    def wait(copies):
        for copy in copies:
            copy.wait()

    @pl.when(step_id == 0)
    def _():
        start(column_copies(0, 0, True))

    @pl.when(step_id < last_step)
    def _():
        @pl.when(step_id >= 1)
        def _():
            wait(column_copies(step_id - 1, 1 - slot, False))
        start(column_copies(step_id + 1, 1 - slot, True))

    buf = io_ref.at[slot]

    def gather(cp, carry):
        rows = pl.ds(pl.multiple_of(cp * slab, slab), slab)
        parts = []
        for part in range(q // gpb):
            t0 = cp * 2 * q + part * gpb
            parts.append([jnp.concatenate([buf[pl.ds(t0 + i, batch, stride=pitch), :],
                                           buf[pl.ds(t0 + q + i, batch, stride=pitch), :]],
                                          axis=0).astype(BF16) for i in range(gpb)])
        for part, w in enumerate(_block_transposes(parts, lane_block)):
            for g in range(gpb):
                ug_ref[g, rows, part * LANES:(part + 1) * LANES] = w[g]
        return carry

    for k in range(n_slices):
        wait(column_copies(step_id, slot, True, slices=(k,)))
        lax.fori_loop(k * slice_trips, (k + 1) * slice_trips, gather, 0,
                      unroll=SSM_RELAYOUT_UNROLL)

    fwd_lane = lax.broadcasted_iota(jnp.int32, (batch, 2 * half), 1) < half
    zeros = jnp.zeros((batch, half), F32)
    last_rows = pl.ds((n_chunks - 1) * batch, batch)
    ni = SSM_INTERLEAVE
    for g0 in range(0, gpb, ni):
        for gi in range(ni):
            s_ref[gi] = jnp.dot(ug_ref[g0 + gi], ein_ref[g0 + gi], preferred_element_type=F32)
            x_ref[gi, 0:batch, 0:half] = zeros
            x_ref[gi, 0:batch, 2 * half:3 * half] = zeros
            x_ref[gi, last_rows, half:2 * half] = zeros
            x_ref[gi, last_rows, 3 * half:4 * half] = zeros

        def step(k, carry):
            rf = pl.multiple_of(k * batch, batch)
            rb = pl.multiple_of((n_chunks - 1 - k) * batch, batch)
            new = []
            for gi in range(ni):
                xr, xi = carry[gi]
                sre = jnp.where(fwd_lane, s_ref[gi, pl.ds(rf, batch), 0:2 * half],
                                s_ref[gi, pl.ds(rb, batch), 0:2 * half])
                sim = jnp.where(fwd_lane, s_ref[gi, pl.ds(rf, batch), 2 * half:4 * half],
                                s_ref[gi, pl.ds(rb, batch), 2 * half:4 * half])
                ar = aq_ref[g0 + gi, 0:1, :]
                ai = aq_ref[g0 + gi, 1:2, :]
                nr = ar * xr - ai * xi + sre
                nim = ar * xi + ai * xr + sim
                x_ref[gi, pl.ds(rf + batch, batch), 0:half] = nr[:, 0:half]
                x_ref[gi, pl.ds(rf + batch, batch), 2 * half:3 * half] = nim[:, 0:half]
                x_ref[gi, pl.ds(rb - batch, batch), half:2 * half] = nr[:, half:2 * half]
                x_ref[gi, pl.ds(rb - batch, batch), 3 * half:4 * half] = nim[:, half:2 * half]
                new.append((nr, nim))
            return tuple(new)

        init = tuple((jnp.zeros((batch, 2 * half), F32), jnp.zeros((batch, 2 * half), F32))
                     for _ in range(ni))
        lax.fori_loop(0, n_chunks - 1, step, init, unroll=True)

        for gi in range(ni):
            g = g0 + gi
            y = jnp.dot(ug_ref[g], toep_ref[g], preferred_element_type=F32)
            y += lax.dot_general(x_ref[gi].astype(BF16), eout_ref[g], NT_DIMS,
                                 preferred_element_type=F32)
            ug_ref[g] = y.astype(BF16)

    def scatter(cp, carry):
        rows = pl.ds(pl.multiple_of(cp * slab, slab), slab)
        parts = [[ug_ref[g, rows, part * LANES:(part + 1) * LANES] for g in range(gpb)]
                 for part in range(q // gpb)]
        for part, w in enumerate(_block_transposes(parts, lane_block)):
            t0 = cp * 2 * q + part * gpb
            for j in range(gpb):
                wj = w[j].astype(F32)
                buf[pl.ds(t0 + j, batch, stride=pitch), :] = wj[0:batch]
                buf[pl.ds(t0 + q + j, batch, stride=pitch), :] = wj[batch:slab]
        return carry

    for k in range(n_slices):
        lax.fori_loop(k * slice_trips, (k + 1) * slice_trips, scatter, 0,
                      unroll=SSM_RELAYOUT_UNROLL)
        start(column_copies(step_id, slot, False, slices=(k,)))

    @pl.when(step_id == last_step)
    def _():
        @pl.when(step_id >= 1)
        def _():
            wait(column_copies(step_id - 1, 1 - slot, False))
        wait(column_copies(step_id, slot, False))


def _padded_seq(seq):
    return seq + SUBLANES if seq % (2 * SUBLANES) == 0 else seq


def _ssm(u, lam_re, lam_im, log_dt, b_re, b_im, c_re, c_im, d_skip):
    batch, seq, _ = u.shape
    pitch = _padded_seq(seq)
    rows = batch * seq // SSM_CHUNK
    ein, toep, eout, a_q = _ssm_operators(lam_re, lam_im, log_dt, b_re, b_im, c_re, c_im, d_skip)
    gpb = GROUPS_PER_BLOCK
    mat = pl.BlockSpec((gpb, CHUNK_WIDTH, CHUNK_WIDTH), lambda s: (s, 0, 0))
    hbm = pl.BlockSpec(memory_space=pl.ANY)
    return pl.pallas_call(
        functools.partial(_ssm_kernel, batch=batch, seq=seq, pitch=pitch),
        grid=(SSM_GROUPS // gpb,),
        in_specs=[hbm, mat, mat, mat,
                  pl.BlockSpec((gpb, SUBLANES, 2 * SSM_STATE), lambda s: (s, 0, 0))],
        out_specs=hbm,
        out_shape=jax.ShapeDtypeStruct((batch, seq, SSM_WIDTH), F32),
        scratch_shapes=[pltpu.VMEM((2, batch * pitch, LANES), F32),
                        pltpu.VMEM((gpb, rows, CHUNK_WIDTH), BF16),
                        pltpu.VMEM((SSM_INTERLEAVE, rows, 4 * SSM_STATE), F32),
                        pltpu.VMEM((SSM_INTERLEAVE, rows, 4 * SSM_STATE), F32),
                        pltpu.SemaphoreType.DMA((2, SSM_DMA_SLICES, batch)),
                        pltpu.SemaphoreType.DMA((2, SSM_DMA_SLICES, batch))],
        compiler_params=pltpu.CompilerParams(
            dimension_semantics=("arbitrary",), vmem_limit_bytes=VMEM_LIMIT_BYTES),
    )(u, ein, toep, eout, a_q)


def kernel(x, norm_ffn1, ffn1_w_gate, ffn1_w_up, ffn1_w_down, norm_mix, w_in, attn_sinks,
           ssm_lambda_re, ssm_lambda_im, ssm_log_dt, ssm_b_re, ssm_b_im, ssm_c_re, ssm_c_im,
           ssm_d, ssm_glu_w, ssm_glu_b, attn_out_norm, ssm_out_norm, w_out,
           norm_ffn2, ffn2_w_gate, ffn2_w_up, ffn2_w_down, final_norm):
    b, seq, d = x.shape
    depth = norm_ffn1.shape[0]
    assert d == D_MODEL and seq % BLOCK == 0 and (b * seq) % FFN_TOKEN_TILE == 0
    assert b == SUBLANES and seq % (2 * SSM_CHUNK * SSM_RELAYOUT_UNROLL * SSM_DMA_SLICES) == 0
    h = x.reshape(b * seq, d).astype(F32)
    for l in range(depth):
        h, q, k, v, u = _ffn(h, norm_ffn1[l], ffn1_w_gate[l], ffn1_w_up[l], ffn1_w_down[l],
                             final_norm, final_norm=False, proj=(norm_mix[l], w_in[l]))
        attn = _attention(q.reshape(b, seq, -1), k.reshape(b, seq, -1), v.reshape(b, seq, -1),
                          attn_sinks[l])
        ssm_pre = _ssm(u.reshape(b, seq, -1), ssm_lambda_re[l], ssm_lambda_im[l], ssm_log_dt[l],
                       ssm_b_re[l], ssm_b_im[l], ssm_c_re[l], ssm_c_im[l], ssm_d[l])
        mix = (attn.reshape(b * seq, -1), ssm_pre.reshape(b * seq, -1), ssm_glu_w[l],
               ssm_glu_b[l], attn_out_norm[l], ssm_out_norm[l], w_out[l])
        h, = _ffn(h, norm_ffn2[l], ffn2_w_gate[l], ffn2_w_up[l], ffn2_w_down[l],
                  final_norm, final_norm=(l == depth - 1), mix=mix)
    return h.reshape(b, seq, d).astype(x.dtype)
```

```python
import functools
import math

import jax
import jax.numpy as jnp
from jax import lax
from jax.experimental import pallas as pl
from jax.experimental.pallas import tpu as pltpu

F32 = jnp.float32
BF16 = jnp.bfloat16

D_MODEL = 1024
ATTN_HEADS = 8
ATTN_KV_HEADS = 2
Q_PER_KV = ATTN_HEADS // ATTN_KV_HEADS
HEAD_DIM = 64
ATTN_WIDTH = ATTN_HEADS * HEAD_DIM
KV_WIDTH = ATTN_KV_HEADS * HEAD_DIM
WINDOW = 128
BLOCK = 128
SSM_CH = 16
SSM_WIDTH = D_MODEL - ATTN_WIDTH
SSM_GROUPS = SSM_WIDTH // SSM_CH
SSM_STATE = 64
IN_WIDTH = ATTN_WIDTH + 2 * KV_WIDTH + SSM_WIDTH
D_FF = 2816
EPS = 1e-6
NEG_INF = -1e30
LAMBDA_RE_MAX = -1e-4
LOG2_E = math.log2(math.e)
QUERY_SCALE = HEAD_DIM ** -0.5 * LOG2_E

LANES = 128
SUBLANES = 8
VMEM_LIMIT_BYTES = 56 * 1024 * 1024

FFN_TOKEN_TILE = 512
FFN_SUBTILES = 2
FF_TILE = 256
ATTN_BLOCKS_PER_STEP = 16
ATTN_HEADS_PER_PV = 2
SSM_CHUNK = 16
CHUNK_WIDTH = SSM_CHUNK * SSM_CH
GROUPS_PER_BLOCK = LANES // SSM_CH
SSM_INTERLEAVE = 8
SSM_RELAYOUT_UNROLL = 4
SSM_DMA_SLICES = 8
OPS_GROUPS_PER_STEP = 8

NT_DIMS = (((1,), (1,)), ((), ()))
TN_DIMS = (((0,), (0,)), ((), ()))


def _rms(x):
    return x * lax.rsqrt(jnp.mean(x * x, axis=-1, keepdims=True) + EPS)


def _mixed_update(rows, attn_ref, ssm_ref, gw_ref, gb_ref, ga_ref, gs_ref, wo_ref):
    y = jax.nn.gelu(ssm_ref[rows, :])
    z = jnp.dot(y.astype(BF16), gw_ref[...].astype(BF16), preferred_element_type=F32) + gb_ref[...]
    s = y * jax.nn.sigmoid(z)
    sn = _rms(s) * gs_ref[...]
    an = _rms(attn_ref[rows, :].astype(F32)) * ga_ref[...]
    mixed = jnp.concatenate([an, sn], axis=-1).astype(BF16)
    return jnp.dot(mixed, wo_ref[...].astype(BF16), preferred_element_type=F32)


def _mixer_inputs(y, rows, gain_ref, w_ref, q_ref, k_ref, v_ref, u_ref):
    hn = (_rms(y) * gain_ref[...]).astype(BF16)
    proj = jnp.dot(hn, w_ref[...].astype(BF16), preferred_element_type=F32)
    q_ref[rows, :] = (proj[:, :ATTN_WIDTH] * QUERY_SCALE).astype(BF16)
    k_ref[rows, :] = proj[:, ATTN_WIDTH:ATTN_WIDTH + KV_WIDTH].astype(BF16)
    v_ref[rows, :] = proj[:, ATTN_WIDTH + KV_WIDTH:ATTN_WIDTH + 2 * KV_WIDTH].astype(BF16)
    u_ref[rows, :] = proj[:, ATTN_WIDTH + 2 * KV_WIDTH:]


def _ffn_body(read_x, gain_ref, fgain_ref, wg_hbm, wu_hbm, wd_hbm, o_ref,
              wg_ref, wu_ref, wd_ref, act_ref, gu_stage, d_stage, sem, *, final_norm,
              epilogue=None):
    nj = D_FF // FF_TILE

    def weight_copies(j, slot):
        span = pl.ds(j * FF_TILE, FF_TILE)
        return (pltpu.make_async_copy(wg_hbm.at[:, span], gu_stage.at[0, slot], sem.at[0, slot]),
                pltpu.make_async_copy(wu_hbm.at[:, span], gu_stage.at[1, slot], sem.at[1, slot]),
                pltpu.make_async_copy(wd_hbm.at[span, :], d_stage.at[slot], sem.at[2, slot]))

    def step(stage_weights):
        if stage_weights:
            for copy in weight_copies(0, 0):
                copy.start()
        sub = o_ref.shape[0] // FFN_SUBTILES
        row_groups = [slice(h * sub, (h + 1) * sub) for h in range(FFN_SUBTILES)]
        xs = [read_x(rows) for rows in row_groups]
        hns = [(_rms(x) * gain_ref[...]).astype(BF16) for x in xs]
        for j in range(nj):
            cols = slice(j * FF_TILE, (j + 1) * FF_TILE)
            if stage_weights:
                slot = j % 2
                if j + 1 < nj:
                    for copy in weight_copies(j + 1, 1 - slot):
                        copy.start()
                for copy in weight_copies(j, slot):
                    copy.wait()
                wg_ref[:, cols] = gu_stage[0, slot].astype(BF16)
                wu_ref[:, cols] = gu_stage[1, slot].astype(BF16)
                wd_ref[cols, :] = d_stage[slot].astype(BF16)
            for rows, hn in zip(row_groups, hns):
                g = jnp.dot(hn, wg_ref[:, cols], preferred_element_type=F32)
                u = jnp.dot(hn, wu_ref[:, cols], preferred_element_type=F32)
                act_ref[rows, cols] = (g * jax.nn.sigmoid(g) * u).astype(BF16)
        for rows, x in zip(row_groups, xs):
            y = x + 0.5 * jnp.dot(act_ref[rows, :], wd_ref[...], preferred_element_type=F32)
            if final_norm:
                y = _rms(y) * fgain_ref[...]
            o_ref[rows, :] = y
            if epilogue is not None:
                epilogue(y, rows)

    pl.when(pl.program_id(0) == 0)(functools.partial(step, True))
    pl.when(pl.program_id(0) != 0)(functools.partial(step, False))


def _ffn_proj_kernel(x_ref, mgain_ref, win_ref, gain_ref, fgain_ref, wg_hbm, wu_hbm, wd_hbm,
                     o_ref, q_ref, k_ref, v_ref, u_ref, *scratch, final_norm):
    emit = lambda y, rows: _mixer_inputs(y, rows, mgain_ref, win_ref, q_ref, k_ref, v_ref, u_ref)
    _ffn_body(lambda rows: x_ref[rows, :], gain_ref, fgain_ref, wg_hbm, wu_hbm, wd_hbm, o_ref,
              *scratch, final_norm=final_norm, epilogue=emit)


def _mix_ffn_kernel(x_ref, attn_ref, ssm_ref, gw_ref, gb_ref, ga_ref, gs_ref, wo_ref, *ffn_refs,
                    final_norm):
    read_x = lambda rows: x_ref[rows, :] + _mixed_update(rows, attn_ref, ssm_ref, gw_ref, gb_ref,
                                                         ga_ref, gs_ref, wo_ref)
    _ffn_body(read_x, *ffn_refs, final_norm=final_norm)


def _ffn(x, gain, w_gate, w_up, w_down, final_gain, final_norm, mix=None, proj=None):
    assert (mix is None) != (proj is None)
    t = x.shape[0]
    tile = FFN_TOKEN_TILE
    vec = lambda width: pl.BlockSpec((1, width), lambda i: (0, 0))
    resident = lambda r, c: pl.BlockSpec((r, c), lambda i: (0, 0), pipeline_mode=pl.Buffered(1))
    row = lambda width: pl.BlockSpec((tile, width), lambda i: (i, 0))
    hbm = pl.BlockSpec(memory_space=pl.ANY)
    in_specs, args = [row(D_MODEL)], [x]
    out_specs, out_shape = [row(D_MODEL)], [jax.ShapeDtypeStruct((t, D_MODEL), F32)]
    if mix is not None:
        attn, ssm_pre, glu_w, glu_b, attn_gain, ssm_gain, w_out = mix
        body = _mix_ffn_kernel
        in_specs += [row(ATTN_WIDTH), row(SSM_WIDTH),
                     resident(SSM_WIDTH, SSM_WIDTH), vec(SSM_WIDTH), vec(ATTN_WIDTH),
                     vec(SSM_WIDTH), resident(D_MODEL, D_MODEL)]
        args += [attn, ssm_pre, glu_w, glu_b.reshape(1, -1).astype(F32),
                 attn_gain.reshape(1, -1).astype(F32), ssm_gain.reshape(1, -1).astype(F32), w_out]
    if proj is not None:
        mixer_gain, w_in = proj
        body = _ffn_proj_kernel
        in_specs += [vec(D_MODEL), resident(D_MODEL, IN_WIDTH)]
        args += [mixer_gain.reshape(1, D_MODEL), w_in]
        for width, dtype in ((ATTN_WIDTH, BF16), (KV_WIDTH, BF16), (KV_WIDTH, BF16),
                             (SSM_WIDTH, F32)):
            out_specs.append(row(width))
            out_shape.append(jax.ShapeDtypeStruct((t, width), dtype))
    in_specs += [vec(D_MODEL), vec(D_MODEL), hbm, hbm, hbm]
    args += [gain.reshape(1, D_MODEL), final_gain.reshape(1, D_MODEL), w_gate, w_up, w_down]
    return pl.pallas_call(
        functools.partial(body, final_norm=final_norm),
        grid=(t // tile,),
        in_specs=in_specs,
        out_specs=out_specs,
        out_shape=out_shape,
        scratch_shapes=[pltpu.VMEM((D_MODEL, D_FF), BF16),
                        pltpu.VMEM((D_MODEL, D_FF), BF16),
                        pltpu.VMEM((D_FF, D_MODEL), BF16),
                        pltpu.VMEM((tile, D_FF), BF16),
                        pltpu.VMEM((2, 2, D_MODEL, FF_TILE), F32),
                        pltpu.VMEM((2, FF_TILE, D_MODEL), F32),
                        pltpu.SemaphoreType.DMA((3, 2))],
        compiler_params=pltpu.CompilerParams(
            dimension_semantics=("arbitrary",), vmem_limit_bytes=VMEM_LIMIT_BYTES),
    )(*args)


def _attn_kernel(sink_ref, q_ref, kp_ref, kc_ref, kn_ref, vp_ref, vc_ref, vn_ref, o_ref,
                 bias_ref, k_ref, v_ref, s_ref):
    n = pl.program_id(1)
    last = pl.num_programs(1) - 1
    nblk = ATTN_BLOCKS_PER_STEP

    @pl.when((pl.program_id(0) == 0) & (n == 0))
    def _():
        kj = lax.broadcasted_iota(jnp.int32, (3 * BLOCK, BLOCK), 0)
        qi = lax.broadcasted_iota(jnp.int32, (3 * BLOCK, BLOCK), 1)
        rel = jnp.abs(kj - BLOCK - qi)
        dist = rel.astype(F32)
        inside = rel <= WINDOW
        has_prev = kj >= BLOCK
        has_next = kj < 2 * BLOCK
        for variant, ok in enumerate((inside & has_prev, inside, inside & has_next)):
            for h in range(ATTN_HEADS):
                slope = float(2.0 ** (-8.0 * (h + 1) / ATTN_HEADS))
                bias_ref[variant, h] = jnp.where(ok, (-slope * LOG2_E) * dist, NEG_INF)
        for kh in range(ATTN_KV_HEADS):
            v_ref[:, (2 * kh + 1) * HEAD_DIM:(2 * kh + 2) * HEAD_DIM] = jnp.ones(
                (v_ref.shape[0], HEAD_DIM), BF16)

    spans = ((slice(0, BLOCK), kp_ref, vp_ref),
             (slice(BLOCK, (nblk + 1) * BLOCK), kc_ref, vc_ref),
             (slice((nblk + 1) * BLOCK, (nblk + 2) * BLOCK), kn_ref, vn_ref))
    for rows, k_in, v_in in spans:
        k_ref[rows, :] = k_in[...]
        for kh in range(ATTN_KV_HEADS):
            v_ref[rows, 2 * kh * HEAD_DIM:(2 * kh + 1) * HEAD_DIM] = (
                v_in[:, kh * HEAD_DIM:(kh + 1) * HEAD_DIM])

    def scores(j, kh):
        kcat = k_ref[j * BLOCK:(j + 3) * BLOCK, kh * HEAD_DIM:(kh + 1) * HEAD_DIM]
        heads = [kh * Q_PER_KV + g for g in range(Q_PER_KV)]
        qs = jnp.concatenate(
            [q_ref[j * BLOCK:(j + 1) * BLOCK, h * HEAD_DIM:(h + 1) * HEAD_DIM] for h in heads],
            axis=0)
        s_ref[j, kh] = lax.dot_general(kcat, qs, NT_DIMS,
                                       preferred_element_type=F32)

    for kh in range(ATTN_KV_HEADS):
        scores(0, kh)
    for j in range(nblk):
        variant = 1
        if j == 0:
            variant = jnp.where(n == 0, 0, variant)
        if j == nblk - 1:
            variant = jnp.where(n == last, 2, variant)
        outs = []
        for kh in range(ATTN_KV_HEADS):
            if j + 1 < nblk:
                scores(j + 1, kh)
            v_ones = v_ref[j * BLOCK:(j + 3) * BLOCK, 2 * kh * HEAD_DIM:(2 * kh + 2) * HEAD_DIM]
            probs, sink_terms = [], []
            for g in range(Q_PER_KV):
                h = kh * Q_PER_KV + g
                s = s_ref[j, kh, :, g * BLOCK:(g + 1) * BLOCK] + bias_ref[variant, h]
                sink = sink_ref[h] * LOG2_E
                m = jnp.maximum(jnp.max(s, axis=0, keepdims=True), sink)
                probs.append(jnp.exp2(s - m).astype(BF16))
                sink_terms.append(jnp.exp2(sink - m))
            hp = ATTN_HEADS_PER_PV
            for g in range(0, Q_PER_KV, hp):
                pv = lax.dot_general(v_ones, jnp.concatenate(probs[g:g + hp], axis=1), TN_DIMS,
                                     preferred_element_type=F32)
                for t in range(hp):
                    cols = slice(t * BLOCK, (t + 1) * BLOCK)
                    den = pv[HEAD_DIM:HEAD_DIM + 1, cols] + sink_terms[g + t]
                    outs.append(pv[:HEAD_DIM, cols] / den)
        o_ref[j * BLOCK:(j + 1) * BLOCK, :] = jnp.concatenate(outs, axis=0).T.astype(o_ref.dtype)


def _attention(q, k, v, sinks):
    b, seq, _ = q.shape
    nblk = ATTN_BLOCKS_PER_STEP
    steps = seq // (nblk * BLOCK)
    nb = seq // BLOCK
    assert seq % (nblk * BLOCK) == 0 and nb >= 2
    edge = lambda f: pl.BlockSpec((None, BLOCK, KV_WIDTH), f)
    body = pl.BlockSpec((None, nblk * BLOCK, KV_WIDTH), lambda bi, n: (bi, n, 0))
    prev = lambda bi, n: (bi, jnp.maximum(n * nblk - 1, 0), 0)
    nxt = lambda bi, n: (bi, jnp.minimum((n + 1) * nblk, nb - 1), 0)
    rows = pl.BlockSpec((None, nblk * BLOCK, ATTN_WIDTH), lambda bi, n: (bi, n, 0))
    return pl.pallas_call(
        _attn_kernel,
        grid=(b, steps),
        in_specs=[pl.BlockSpec(memory_space=pltpu.SMEM), rows,
                  edge(prev), body, edge(nxt), edge(prev), body, edge(nxt)],
        out_specs=rows,
        out_shape=jax.ShapeDtypeStruct((b, seq, ATTN_WIDTH), BF16),
        scratch_shapes=[pltpu.VMEM((3, ATTN_HEADS, 3 * BLOCK, BLOCK), F32),
                        pltpu.VMEM(((nblk + 2) * BLOCK, KV_WIDTH), BF16),
                        pltpu.VMEM(((nblk + 2) * BLOCK, 2 * KV_WIDTH), BF16),
                        pltpu.VMEM((nblk, ATTN_KV_HEADS, 3 * BLOCK, Q_PER_KV * BLOCK), F32)],
        compiler_params=pltpu.CompilerParams(
            dimension_semantics=("arbitrary", "arbitrary"), vmem_limit_bytes=VMEM_LIMIT_BYTES),
    )(sinks.astype(F32), q, k, k, k, v, v, v)


def _dot_nt_split(a, b):
    a_hi, b_hi = a.astype(BF16), b.astype(BF16)
    a_lo = (a - a_hi.astype(F32)).astype(BF16)
    b_lo = (b - b_hi.astype(F32)).astype(BF16)
    dot = functools.partial(lax.dot_general, dimension_numbers=NT_DIMS,
                            preferred_element_type=F32)
    return dot(a_hi, b_hi) + dot(a_hi, b_lo) + dot(a_lo, b_hi)


def _ssm_ops_kernel(ldt_ref, lre_ref, lim_ref, d_ref, btr_ref, bti_ref, cre_ref, cim_ref,
                    ein_ref, toep_ref, eout_ref, aq_ref):
    q, hc, p = SSM_CHUNK, SSM_CH, SSM_STATE
    gs = lre_ref.shape[1]
    g_base = pl.program_id(0) * gs
    both_dirs = lambda ref, gi: jnp.concatenate([ref[0, gi], ref[1, gi]], axis=1)
    fwd = lax.broadcasted_iota(jnp.int32, (1, 2 * p), 1) < p
    zero_row = jnp.zeros((1, 2 * p), F32)
    row_id = lax.broadcasted_iota(jnp.int32, (CHUNK_WIDTH, CHUNK_WIDTH), 0)
    col_id = lax.broadcasted_iota(jnp.int32, (CHUNK_WIDTH, CHUNK_WIDTH), 1)

    def table(select, n):
        picks = [select(m) for m in range(n)]
        re = jnp.concatenate([jnp.broadcast_to(r, (hc, 2 * p)) for r, _ in picks], axis=0)
        im = jnp.concatenate([jnp.broadcast_to(i, (hc, 2 * p)) for _, i in picks], axis=0)
        return re, im

    def tile_rows(x, n):
        return jnp.concatenate([x] * n, axis=0)

    for gi in range(gs):
        lr = jnp.minimum(jnp.concatenate([lre_ref[0, gi:gi + 1, :], lre_ref[1, gi:gi + 1, :]],
                                         axis=1), LAMBDA_RE_MAX)
        li = jnp.concatenate([lim_ref[0, gi:gi + 1, :], lim_ref[1, gi:gi + 1, :]], axis=1)
        dt = jnp.exp(jnp.where(fwd, ldt_ref[0, g_base + gi], ldt_ref[1, g_base + gi]))
        mag = jnp.exp(lr * dt)
        a_r = mag * jnp.cos(li * dt)
        a_i = mag * jnp.sin(li * dt)
        den = lr * lr + li * li
        coef_r = ((a_r - 1.0) * lr + a_i * li) / den
        coef_i = (a_i * lr - (a_r - 1.0) * li) / den
        b_r, b_i = both_dirs(btr_ref, gi), both_dirs(bti_ref, gi)
        c_r, c_i = both_dirs(cre_ref, gi), both_dirs(cim_ref, gi)
        bb_r = coef_r * b_r - coef_i * b_i
        bb_i = coef_r * b_i + coef_i * b_r

        pw = [(jnp.ones((1, 2 * p), F32), zero_row)]
        for _ in range(q):
            r, i = pw[-1]
            pw.append((r * a_r - i * a_i, r * a_i + i * a_r))

        def both(f_idx, b_idx):
            fr, fi = pw[f_idx] if f_idx is not None else (zero_row, zero_row)
            br, bi = pw[b_idx] if b_idx is not None else (zero_row, zero_row)
            return jnp.where(fwd, fr, br), jnp.where(fwd, fi, bi)

        p_r, p_i = table(lambda i: both(q - 1 - i, i), q)
        tb_r, tb_i = tile_rows(bb_r, q), tile_rows(bb_i, q)
        ein = jnp.concatenate([tb_r * p_r - tb_i * p_i, tb_r * p_i + tb_i * p_r], axis=1)
        ein_ref[gi] = ein.astype(BF16)

        p_r, p_i = table(lambda j: both(j + 1, q - j), q)
        tc_r, tc_i = tile_rows(c_r, q), tile_rows(c_i, q)
        eout = jnp.concatenate([tc_r * p_r - tc_i * p_i, -(tc_r * p_i + tc_i * p_r)], axis=1)
        eout_ref[gi] = eout.astype(BF16)

        def lag(m):
            return both(m - (q - 1) if q - 1 <= m <= 2 * q - 2 else None,
                        (q - 1) - m if m <= q - 1 else None)

        p_r, p_i = table(lag, 2 * q)
        tc_r, tc_i = tile_rows(c_r, 2 * q), tile_rows(c_i, 2 * q)
        cpt = jnp.concatenate([tc_r * p_r - tc_i * p_i, tc_r * p_i + tc_i * p_r], axis=1)
        bcat = jnp.concatenate([bb_r, -bb_i], axis=1)
        kern = _dot_nt_split(bcat, cpt)
        toep = jnp.concatenate(
            [kern[:, hc * (q - 1 - i):hc * (q - 1 - i) + CHUNK_WIDTH] for i in range(q)], axis=0)
        skip = jnp.concatenate([d_ref[gi:gi + 1, :]] * q, axis=1)
        toep_ref[gi] = (toep + jnp.where(row_id == col_id, skip, 0.0)).astype(BF16)

        aq_ref[gi] = jnp.concatenate(
            [pw[q][0], pw[q][1], jnp.zeros((SUBLANES - 2, 2 * p), F32)], axis=0)


def _ssm_operators(lam_re, lam_im, log_dt, b_re, b_im, c_re, c_im, d_skip):
    g, p, hc = SSM_GROUPS, SSM_STATE, SSM_CH
    f32 = lambda a: a.astype(F32)
    channel_major = lambda b: f32(b).transpose(0, 1, 3, 2)
    gs = OPS_GROUPS_PER_STEP
    mat = pl.BlockSpec((gs, CHUNK_WIDTH, CHUNK_WIDTH), lambda s: (s, 0, 0))
    mat_shape = jax.ShapeDtypeStruct((g, CHUNK_WIDTH, CHUNK_WIDTH), BF16)
    per_dir = lambda *tail: pl.BlockSpec((2, gs) + tail, lambda s: (0, s) + (0,) * len(tail))
    return pl.pallas_call(
        _ssm_ops_kernel,
        grid=(g // gs,),
        in_specs=[pl.BlockSpec(memory_space=pltpu.SMEM), per_dir(p), per_dir(p),
                  pl.BlockSpec((gs, hc), lambda s: (s, 0)),
                  per_dir(hc, p), per_dir(hc, p), per_dir(hc, p), per_dir(hc, p)],
        out_specs=[mat, mat, mat, pl.BlockSpec((gs, SUBLANES, 2 * p), lambda s: (s, 0, 0))],
        out_shape=[mat_shape, mat_shape, mat_shape,
                   jax.ShapeDtypeStruct((g, SUBLANES, 2 * p), F32)],
        compiler_params=pltpu.CompilerParams(
            dimension_semantics=("parallel",), vmem_limit_bytes=VMEM_LIMIT_BYTES),
    )(f32(log_dt), f32(lam_re), f32(lam_im), f32(d_skip), channel_major(b_re), channel_major(b_im),
      f32(c_re), f32(c_im))


def _lane_roll(x, shift):
    shift %= LANES
    return jnp.concatenate([x[:, LANES - shift:], x[:, :LANES - shift]], axis=1)


def _block_transpose(v, lane_block):
    return _block_transposes([v], lane_block)[0]


def _block_transposes(groups, lane_block):
    n = len(groups[0])
    skewed = [[v[i] if i == 0 else _lane_roll(v[i], SSM_CH * i) for i in range(n)] for v in groups]
    picked = []
    for sk in skewed:
        rows = []
        for g in range(n):
            p = sk[(0 - g) % n]
            for j in range(1, n):
                p = jnp.where(lane_block == j, sk[(j - g) % n], p)
            rows.append(p)
        picked.append(rows)
    return [[p[g] if g == 0 else _lane_roll(p[g], -SSM_CH * g) for g in range(n)] for p in picked]


def _ssm_kernel(u_hbm, ein_ref, toep_ref, eout_ref, aq_ref, y_hbm,
                io_ref, ug_ref, s_ref, x_ref, in_sem, out_sem, *, batch, seq, pitch):
    q, half = SSM_CHUNK, SSM_STATE
    n_chunks = seq // q
    gpb = GROUPS_PER_BLOCK
    slab = 2 * batch
    lane_block = lax.broadcasted_iota(jnp.int32, (slab, LANES), 1) // SSM_CH
    step_id = pl.program_id(0)
    last_step = pl.num_programs(0) - 1
    slot = lax.rem(step_id, 2)

    n_slices = SSM_DMA_SLICES
    slice_rows = seq // n_slices
    slice_trips = n_chunks // (2 * n_slices)

    def column_copies(block, slot_, to_vmem, slices=range(SSM_DMA_SLICES)):
        lanes = pl.ds(pl.multiple_of(block * LANES, LANES), LANES)
        copies = []
        for k in slices:
            for b in range(batch):
                hbm = (u_hbm if to_vmem else y_hbm).at[b, pl.ds(k * slice_rows, slice_rows), lanes]
                vmem = io_ref.at[slot_, pl.ds(b * pitch + k * slice_rows, slice_rows), :]
                if to_vmem:
                    copies.append(pltpu.make_async_copy(hbm, vmem, in_sem.at[slot_, k, b]))
                else:
                    copies.append(pltpu.make_async_copy(vmem, hbm, out_sem.at[slot_, k, b]))
        return copies

    def start(copies):
        for copy in copies:
            copy.start()

    def wait(copies):
        for copy in copies:
            copy.wait()

    @pl.when(step_id == 0)
    def _():
        start(column_copies(0, 0, True))

    @pl.when(step_id < last_step)
    def _():
        @pl.when(step_id >= 1)
        def _():
            wait(column_copies(step_id - 1, 1 - slot, False))
        start(column_copies(step_id + 1, 1 - slot, True))

    buf = io_ref.at[slot]

    def gather(cp, carry):
        rows = pl.ds(pl.multiple_of(cp * slab, slab), slab)
        parts = []
        for part in range(q // gpb):
            t0 = cp * 2 * q + part * gpb
            parts.append([jnp.concatenate([buf[pl.ds(t0 + i, batch, stride=pitch), :],
                                           buf[pl.ds(t0 + q + i, batch, stride=pitch), :]],
                                          axis=0).astype(BF16) for i in range(gpb)])
        for part, w in enumerate(_block_transposes(parts, lane_block)):
            for g in range(gpb):
                ug_ref[g, rows, part * LANES:(part + 1) * LANES] = w[g]
        return carry

    for k in range(n_slices):
        wait(column_copies(step_id, slot, True, slices=(k,)))
        lax.fori_loop(k * slice_trips, (k + 1) * slice_trips, gather, 0,
                      unroll=SSM_RELAYOUT_UNROLL)

    fwd_lane = lax.broadcasted_iota(jnp.int32, (batch, 2 * half), 1) < half
    zeros = jnp.zeros((batch, half), F32)
    last_rows = pl.ds((n_chunks - 1) * batch, batch)
    ni = SSM_INTERLEAVE
    for g0 in range(0, gpb, ni):
        for gi in range(ni):
            s_ref[gi] = jnp.dot(ug_ref[g0 + gi], ein_ref[g0 + gi], preferred_element_type=F32)
            x_ref[gi, 0:batch, 0:half] = zeros
            x_ref[gi, 0:batch, 2 * half:3 * half] = zeros
            x_ref[gi, last_rows, half:2 * half] = zeros
            x_ref[gi, last_rows, 3 * half:4 * half] = zeros

        def step(k, carry):
            rf = pl.multiple_of(k * batch, batch)
            rb = pl.multiple_of((n_chunks - 1 - k) * batch, batch)
            new = []
            for gi in range(ni):
                xr, xi = carry[gi]
                sre = jnp.where(fwd_lane, s_ref[gi, pl.ds(rf, batch), 0:2 * half],
                                s_ref[gi, pl.ds(rb, batch), 0:2 * half])
                sim = jnp.where(fwd_lane, s_ref[gi, pl.ds(rf, batch), 2 * half:4 * half],
                                s_ref[gi, pl.ds(rb, batch), 2 * half:4 * half])
                ar = aq_ref[g0 + gi, 0:1, :]
                ai = aq_ref[g0 + gi, 1:2, :]
                nr = ar * xr - ai * xi + sre
                nim = ar * xi + ai * xr + sim
                x_ref[gi, pl.ds(rf + batch, batch), 0:half] = nr[:, 0:half]
                x_ref[gi, pl.ds(rf + batch, batch), 2 * half:3 * half] = nim[:, 0:half]
                x_ref[gi, pl.ds(rb - batch, batch), half:2 * half] = nr[:, half:2 * half]
                x_ref[gi, pl.ds(rb - batch, batch), 3 * half:4 * half] = nim[:, half:2 * half]
                new.append((nr, nim))
            return tuple(new)

        init = tuple((jnp.zeros((batch, 2 * half), F32), jnp.zeros((batch, 2 * half), F32))
                     for _ in range(ni))
        lax.fori_loop(0, n_chunks - 1, step, init, unroll=True)

        for gi in range(ni):
            g = g0 + gi
            y = jnp.dot(ug_ref[g], toep_ref[g], preferred_element_type=F32)
            y += lax.dot_general(x_ref[gi].astype(BF16), eout_ref[g], NT_DIMS,
                                 preferred_element_type=F32)
            ug_ref[g] = y.astype(BF16)

    def scatter(cp, carry):
        rows = pl.ds(pl.multiple_of(cp * slab, slab), slab)
        parts = [[ug_ref[g, rows, part * LANES:(part + 1) * LANES] for g in range(gpb)]
                 for part in range(q // gpb)]
        for part, w in enumerate(_block_transposes(parts, lane_block)):
            t0 = cp * 2 * q + part * gpb
            for j in range(gpb):
                wj = w[j].astype(F32)
                buf[pl.ds(t0 + j, batch, stride=pitch), :] = wj[0:batch]
                buf[pl.ds(t0 + q + j, batch, stride=pitch), :] = wj[batch:slab]
        return carry

    for k in range(n_slices):
        lax.fori_loop(k * slice_trips, (k + 1) * slice_trips, scatter, 0,
                      unroll=SSM_RELAYOUT_UNROLL)
        start(column_copies(step_id, slot, False, slices=(k,)))

    @pl.when(step_id == last_step)
    def _():
        @pl.when(step_id >= 1)
        def _():
            wait(column_copies(step_id - 1, 1 - slot, False))
        wait(column_copies(step_id, slot, False))


def _padded_seq(seq):
    return seq + SUBLANES if seq % (2 * SUBLANES) == 0 else seq


def _ssm(u, lam_re, lam_im, log_dt, b_re, b_im, c_re, c_im, d_skip):
    batch, seq, _ = u.shape
    pitch = _padded_seq(seq)
    rows = batch * seq // SSM_CHUNK
    ein, toep, eout, a_q = _ssm_operators(lam_re, lam_im, log_dt, b_re, b_im, c_re, c_im, d_skip)
    gpb = GROUPS_PER_BLOCK
    mat = pl.BlockSpec((gpb, CHUNK_WIDTH, CHUNK_WIDTH), lambda s: (s, 0, 0))
    hbm = pl.BlockSpec(memory_space=pl.ANY)
    return pl.pallas_call(
        functools.partial(_ssm_kernel, batch=batch, seq=seq, pitch=pitch),
        grid=(SSM_GROUPS // gpb,),
        in_specs=[hbm, mat, mat, mat,
                  pl.BlockSpec((gpb, SUBLANES, 2 * SSM_STATE), lambda s: (s, 0, 0))],
        out_specs=hbm,
        out_shape=jax.ShapeDtypeStruct((batch, seq, SSM_WIDTH), F32),
        scratch_shapes=[pltpu.VMEM((2, batch * pitch, LANES), F32),
                        pltpu.VMEM((gpb, rows, CHUNK_WIDTH), BF16),
                        pltpu.VMEM((SSM_INTERLEAVE, rows, 4 * SSM_STATE), F32),
                        pltpu.VMEM((SSM_INTERLEAVE, rows, 4 * SSM_STATE), F32),
                        pltpu.SemaphoreType.DMA((2, SSM_DMA_SLICES, batch)),
                        pltpu.SemaphoreType.DMA((2, SSM_DMA_SLICES, batch))],
        compiler_params=pltpu.CompilerParams(
            dimension_semantics=("arbitrary",), vmem_limit_bytes=VMEM_LIMIT_BYTES),
    )(u, ein, toep, eout, a_q)


def kernel(x, norm_ffn1, ffn1_w_gate, ffn1_w_up, ffn1_w_down, norm_mix, w_in, attn_sinks,
           ssm_lambda_re, ssm_lambda_im, ssm_log_dt, ssm_b_re, ssm_b_im, ssm_c_re, ssm_c_im,
           ssm_d, ssm_glu_w, ssm_glu_b, attn_out_norm, ssm_out_norm, w_out,
           norm_ffn2, ffn2_w_gate, ffn2_w_up, ffn2_w_down, final_norm):
    b, seq, d = x.shape
    depth = norm_ffn1.shape[0]
    assert d == D_MODEL and seq % BLOCK == 0 and (b * seq) % FFN_TOKEN_TILE == 0
    assert b == SUBLANES and seq % (2 * SSM_CHUNK * SSM_RELAYOUT_UNROLL * SSM_DMA_SLICES) == 0
    h = x.reshape(b * seq, d).astype(F32)
    for l in range(depth):
        h, q, k, v, u = _ffn(h, norm_ffn1[l], ffn1_w_gate[l], ffn1_w_up[l], ffn1_w_down[l],
                             final_norm, final_norm=False, proj=(norm_mix[l], w_in[l]))
        attn = _attention(q.reshape(b, seq, -1), k.reshape(b, seq, -1), v.reshape(b, seq, -1),
                          attn_sinks[l])
        ssm_pre = _ssm(u.reshape(b, seq, -1), ssm_lambda_re[l], ssm_lambda_im[l], ssm_log_dt[l],
                       ssm_b_re[l], ssm_b_im[l], ssm_c_re[l], ssm_c_im[l], ssm_d[l])
        mix = (attn.reshape(b * seq, -1), ssm_pre.reshape(b * seq, -1), ssm_glu_w[l],
               ssm_glu_b[l], attn_out_norm[l], ssm_out_norm[l], w_out[l])
        h, = _ffn(h, norm_ffn2[l], ffn2_w_gate[l], ffn2_w_up[l], ffn2_w_down[l],
                  final_norm, final_norm=(l == depth - 1), mix=mix)
    return h.reshape(b, seq, d).astype(x.dtype)
```

```python
import functools
import math

import jax
import jax.numpy as jnp
from jax import lax
from jax.experimental import pallas as pl
from jax.experimental.pallas import tpu as pltpu

F32 = jnp.float32
BF16 = jnp.bfloat16

D_MODEL = 1024
ATTN_HEADS = 8
ATTN_KV_HEADS = 2
Q_PER_KV = ATTN_HEADS // ATTN_KV_HEADS
HEAD_DIM = 64
ATTN_WIDTH = ATTN_HEADS * HEAD_DIM
KV_WIDTH = ATTN_KV_HEADS * HEAD_DIM
WINDOW = 128
BLOCK = 128
SSM_CH = 16
SSM_WIDTH = D_MODEL - ATTN_WIDTH
SSM_GROUPS = SSM_WIDTH // SSM_CH
SSM_STATE = 64
IN_WIDTH = ATTN_WIDTH + 2 * KV_WIDTH + SSM_WIDTH
D_FF = 2816
EPS = 1e-6
NEG_INF = -1e30
LAMBDA_RE_MAX = -1e-4
LOG2_E = math.log2(math.e)
QUERY_SCALE = HEAD_DIM ** -0.5 * LOG2_E

LANES = 128
SUBLANES = 8
VMEM_LIMIT_BYTES = 56 * 1024 * 1024

FFN_TOKEN_TILE = 512
FFN_SUBTILES = 2
FF_TILE = 256
ATTN_BLOCKS_PER_STEP = 16
ATTN_HEADS_PER_PV = 2
SSM_CHUNK = 16
CHUNK_WIDTH = SSM_CHUNK * SSM_CH
GROUPS_PER_BLOCK = LANES // SSM_CH
SSM_INTERLEAVE = 8
SSM_RELAYOUT_UNROLL = 4
SSM_DMA_SLICES = 8
OPS_GROUPS_PER_STEP = 8

NT_DIMS = (((1,), (1,)), ((), ()))
TN_DIMS = (((0,), (0,)), ((), ()))


def _rms(x):
    return x * lax.rsqrt(jnp.mean(x * x, axis=-1, keepdims=True) + EPS)


def _mixed_update(rows, attn_ref, ssm_ref, gw_ref, gb_ref, ga_ref, gs_ref, wo_ref):
    y = jax.nn.gelu(ssm_ref[rows, :])
    z = jnp.dot(y.astype(BF16), gw_ref[...].astype(BF16), preferred_element_type=F32) + gb_ref[...]
    s = y * jax.nn.sigmoid(z)
    sn = _rms(s) * gs_ref[...]
    an = _rms(attn_ref[rows, :].astype(F32)) * ga_ref[...]
    mixed = jnp.concatenate([an, sn], axis=-1).astype(BF16)
    return jnp.dot(mixed, wo_ref[...].astype(BF16), preferred_element_type=F32)


def _mixer_inputs(y, rows, gain_ref, w_ref, q_ref, k_ref, v_ref, u_ref):
    hn = (_rms(y) * gain_ref[...]).astype(BF16)
    proj = jnp.dot(hn, w_ref[...].astype(BF16), preferred_element_type=F32)
    q_ref[rows, :] = (proj[:, :ATTN_WIDTH] * QUERY_SCALE).astype(BF16)
    k_ref[rows, :] = proj[:, ATTN_WIDTH:ATTN_WIDTH + KV_WIDTH].astype(BF16)
    v_ref[rows, :] = proj[:, ATTN_WIDTH + KV_WIDTH:ATTN_WIDTH + 2 * KV_WIDTH].astype(BF16)
    u_ref[rows, :] = proj[:, ATTN_WIDTH + 2 * KV_WIDTH:]


def _ffn_body(read_x, gain_ref, fgain_ref, wg_hbm, wu_hbm, wd_hbm, o_ref,
              wg_ref, wu_ref, wd_ref, act_ref, gu_stage, d_stage, sem, *, final_norm,
              epilogue=None):
    nj = D_FF // FF_TILE

    def weight_copies(j, slot):
        span = pl.ds(j * FF_TILE, FF_TILE)
        return (pltpu.make_async_copy(wg_hbm.at[:, span], gu_stage.at[0, slot], sem.at[0, slot]),
                pltpu.make_async_copy(wu_hbm.at[:, span], gu_stage.at[1, slot], sem.at[1, slot]),
                pltpu.make_async_copy(wd_hbm.at[span, :], d_stage.at[slot], sem.at[2, slot]))

    def step(stage_weights):
        if stage_weights:
            for copy in weight_copies(0, 0):
                copy.start()
        sub = o_ref.shape[0] // FFN_SUBTILES
        row_groups = [slice(h * sub, (h + 1) * sub) for h in range(FFN_SUBTILES)]
        xs = [read_x(rows) for rows in row_groups]
        hns = [(_rms(x) * gain_ref[...]).astype(BF16) for x in xs]
        for j in range(nj):
            cols = slice(j * FF_TILE, (j + 1) * FF_TILE)
            if stage_weights:
                slot = j % 2
                if j + 1 < nj:
                    for copy in weight_copies(j + 1, 1 - slot):
                        copy.start()
                for copy in weight_copies(j, slot):
                    copy.wait()
                wg_ref[:, cols] = gu_stage[0, slot].astype(BF16)
                wu_ref[:, cols] = gu_stage[1, slot].astype(BF16)
                wd_ref[cols, :] = d_stage[slot].astype(BF16)
            for rows, hn in zip(row_groups, hns):
                g = jnp.dot(hn, wg_ref[:, cols], preferred_element_type=F32)
                u = jnp.dot(hn, wu_ref[:, cols], preferred_element_type=F32)
                act_ref[rows, cols] = (g * jax.nn.sigmoid(g) * u).astype(BF16)
        for rows, x in zip(row_groups, xs):
            y = x + 0.5 * jnp.dot(act_ref[rows, :], wd_ref[...], preferred_element_type=F32)
            if final_norm:
                y = _rms(y) * fgain_ref[...]
            o_ref[rows, :] = y
            if epilogue is not None:
                epilogue(y, rows)

    pl.when(pl.program_id(0) == 0)(functools.partial(step, True))
    pl.when(pl.program_id(0) != 0)(functools.partial(step, False))


def _ffn_proj_kernel(x_ref, mgain_ref, win_ref, gain_ref, fgain_ref, wg_hbm, wu_hbm, wd_hbm,
                     o_ref, q_ref, k_ref, v_ref, u_ref, *scratch, final_norm):
    emit = lambda y, rows: _mixer_inputs(y, rows, mgain_ref, win_ref, q_ref, k_ref, v_ref, u_ref)
    _ffn_body(lambda rows: x_ref[rows, :], gain_ref, fgain_ref, wg_hbm, wu_hbm, wd_hbm, o_ref,
              *scratch, final_norm=final_norm, epilogue=emit)


def _mix_ffn_kernel(x_ref, attn_ref, ssm_ref, gw_ref, gb_ref, ga_ref, gs_ref, wo_ref, *ffn_refs,
                    final_norm):
    read_x = lambda rows: x_ref[rows, :] + _mixed_update(rows, attn_ref, ssm_ref, gw_ref, gb_ref,
                                                         ga_ref, gs_ref, wo_ref)
    _ffn_body(read_x, *ffn_refs, final_norm=final_norm)


def _ffn(x, gain, w_gate, w_up, w_down, final_gain, final_norm, mix=None, proj=None):
    assert (mix is None) != (proj is None)
    t = x.shape[0]
    tile = FFN_TOKEN_TILE
    vec = lambda width: pl.BlockSpec((1, width), lambda i: (0, 0))
    resident = lambda r, c: pl.BlockSpec((r, c), lambda i: (0, 0), pipeline_mode=pl.Buffered(1))
    row = lambda width: pl.BlockSpec((tile, width), lambda i: (i, 0))
    hbm = pl.BlockSpec(memory_space=pl.ANY)
    in_specs, args = [row(D_MODEL)], [x]
    out_specs, out_shape = [row(D_MODEL)], [jax.ShapeDtypeStruct((t, D_MODEL), F32)]
    if mix is not None:
        attn, ssm_pre, glu_w, glu_b, attn_gain, ssm_gain, w_out = mix
        body = _mix_ffn_kernel
        in_specs += [row(ATTN_WIDTH), row(SSM_WIDTH),
                     resident(SSM_WIDTH, SSM_WIDTH), vec(SSM_WIDTH), vec(ATTN_WIDTH),
                     vec(SSM_WIDTH), resident(D_MODEL, D_MODEL)]
        args += [attn, ssm_pre, glu_w, glu_b.reshape(1, -1).astype(F32),
                 attn_gain.reshape(1, -1).astype(F32), ssm_gain.reshape(1, -1).astype(F32), w_out]
    if proj is not None:
        mixer_gain, w_in = proj
        body = _ffn_proj_kernel
        in_specs += [vec(D_MODEL), resident(D_MODEL, IN_WIDTH)]
        args += [mixer_gain.reshape(1, D_MODEL), w_in]
        for width, dtype in ((ATTN_WIDTH, BF16), (KV_WIDTH, BF16), (KV_WIDTH, BF16),
                             (SSM_WIDTH, F32)):
            out_specs.append(row(width))
            out_shape.append(jax.ShapeDtypeStruct((t, width), dtype))
    in_specs += [vec(D_MODEL), vec(D_MODEL), hbm, hbm, hbm]
    args += [gain.reshape(1, D_MODEL), final_gain.reshape(1, D_MODEL), w_gate, w_up, w_down]
    return pl.pallas_call(
        functools.partial(body, final_norm=final_norm),
        grid=(t // tile,),
        in_specs=in_specs,
        out_specs=out_specs,
        out_shape=out_shape,
        scratch_shapes=[pltpu.VMEM((D_MODEL, D_FF), BF16),
                        pltpu.VMEM((D_MODEL, D_FF), BF16),
                        pltpu.VMEM((D_FF, D_MODEL), BF16),
                        pltpu.VMEM((tile, D_FF), BF16),
                        pltpu.VMEM((2, 2, D_MODEL, FF_TILE), F32),
                        pltpu.VMEM((2, FF_TILE, D_MODEL), F32),
                        pltpu.SemaphoreType.DMA((3, 2))],
        compiler_params=pltpu.CompilerParams(
            dimension_semantics=("arbitrary",), vmem_limit_bytes=VMEM_LIMIT_BYTES),
    )(*args)


def _attn_kernel(sink_ref, q_ref, kp_ref, kc_ref, kn_ref, vp_ref, vc_ref, vn_ref, o_ref,
                 bias_ref, k_ref, v_ref, s_ref):
    n = pl.program_id(1)
    last = pl.num_programs(1) - 1
    nblk = ATTN_BLOCKS_PER_STEP

    @pl.when((pl.program_id(0) == 0) & (n == 0))
    def _():
        kj = lax.broadcasted_iota(jnp.int32, (3 * BLOCK, BLOCK), 0)
        qi = lax.broadcasted_iota(jnp.int32, (3 * BLOCK, BLOCK), 1)
        rel = jnp.abs(kj - BLOCK - qi)
        dist = rel.astype(F32)
        inside = rel <= WINDOW
        has_prev = kj >= BLOCK
        has_next = kj < 2 * BLOCK
        for variant, ok in enumerate((inside & has_prev, inside, inside & has_next)):
            for h in range(ATTN_HEADS):
                slope = float(2.0 ** (-8.0 * (h + 1) / ATTN_HEADS))
                bias_ref[variant, h] = jnp.where(ok, (-slope * LOG2_E) * dist, NEG_INF)
        for kh in range(ATTN_KV_HEADS):
            v_ref[:, (2 * kh + 1) * HEAD_DIM:(2 * kh + 2) * HEAD_DIM] = jnp.ones(
                (v_ref.shape[0], HEAD_DIM), BF16)

    spans = ((slice(0, BLOCK), kp_ref, vp_ref),
             (slice(BLOCK, (nblk + 1) * BLOCK), kc_ref, vc_ref),
             (slice((nblk + 1) * BLOCK, (nblk + 2) * BLOCK), kn_ref, vn_ref))
    for rows, k_in, v_in in spans:
        k_ref[rows, :] = k_in[...]
        for kh in range(ATTN_KV_HEADS):
            v_ref[rows, 2 * kh * HEAD_DIM:(2 * kh + 1) * HEAD_DIM] = (
                v_in[:, kh * HEAD_DIM:(kh + 1) * HEAD_DIM])

    def scores(j, kh):
        kcat = k_ref[j * BLOCK:(j + 3) * BLOCK, kh * HEAD_DIM:(kh + 1) * HEAD_DIM]
        heads = [kh * Q_PER_KV + g for g in range(Q_PER_KV)]
        qs = jnp.concatenate(
            [q_ref[j * BLOCK:(j + 1) * BLOCK, h * HEAD_DIM:(h + 1) * HEAD_DIM] for h in heads],
            axis=0)
        s_ref[j, kh] = lax.dot_general(kcat, qs, NT_DIMS,
                                       preferred_element_type=F32)

    for kh in range(ATTN_KV_HEADS):
        scores(0, kh)
    for j in range(nblk):
        variant = 1
        if j == 0:
            variant = jnp.where(n == 0, 0, variant)
        if j == nblk - 1:
            variant = jnp.where(n == last, 2, variant)
        outs = []
        for kh in range(ATTN_KV_HEADS):
            if j + 1 < nblk:
                scores(j + 1, kh)
            v_ones = v_ref[j * BLOCK:(j + 3) * BLOCK, 2 * kh * HEAD_DIM:(2 * kh + 2) * HEAD_DIM]
            probs, sink_terms = [], []
            for g in range(Q_PER_KV):
                h = kh * Q_PER_KV + g
                s = s_ref[j, kh, :, g * BLOCK:(g + 1) * BLOCK] + bias_ref[variant, h]
                sink = sink_ref[h] * LOG2_E
                m = jnp.maximum(jnp.max(s, axis=0, keepdims=True), sink)
                probs.append(jnp.exp2(s - m).astype(BF16))
                sink_terms.append(jnp.exp2(sink - m))
            hp = ATTN_HEADS_PER_PV
            for g in range(0, Q_PER_KV, hp):
                pv = lax.dot_general(v_ones, jnp.concatenate(probs[g:g + hp], axis=1), TN_DIMS,
                                     preferred_element_type=F32)
                for t in range(hp):
                    cols = slice(t * BLOCK, (t + 1) * BLOCK)
                    den = pv[HEAD_DIM:HEAD_DIM + 1, cols] + sink_terms[g + t]
                    outs.append(pv[:HEAD_DIM, cols] / den)
        o_ref[j * BLOCK:(j + 1) * BLOCK, :] = jnp.concatenate(outs, axis=0).T.astype(o_ref.dtype)


def _attention(q, k, v, sinks):
    b, seq, _ = q.shape
    nblk = ATTN_BLOCKS_PER_STEP
    steps = seq // (nblk * BLOCK)
    nb = seq // BLOCK
    assert seq % (nblk * BLOCK) == 0 and nb >= 2
    edge = lambda f: pl.BlockSpec((None, BLOCK, KV_WIDTH), f)
    body = pl.BlockSpec((None, nblk * BLOCK, KV_WIDTH), lambda bi, n: (bi, n, 0))
    prev = lambda bi, n: (bi, jnp.maximum(n * nblk - 1, 0), 0)
    nxt = lambda bi, n: (bi, jnp.minimum((n + 1) * nblk, nb - 1), 0)
    rows = pl.BlockSpec((None, nblk * BLOCK, ATTN_WIDTH), lambda bi, n: (bi, n, 0))
    return pl.pallas_call(
        _attn_kernel,
        grid=(b, steps),
        in_specs=[pl.BlockSpec(memory_space=pltpu.SMEM), rows,
                  edge(prev), body, edge(nxt), edge(prev), body, edge(nxt)],
        out_specs=rows,
        out_shape=jax.ShapeDtypeStruct((b, seq, ATTN_WIDTH), BF16),
        scratch_shapes=[pltpu.VMEM((3, ATTN_HEADS, 3 * BLOCK, BLOCK), F32),
                        pltpu.VMEM(((nblk + 2) * BLOCK, KV_WIDTH), BF16),
                        pltpu.VMEM(((nblk + 2) * BLOCK, 2 * KV_WIDTH), BF16),
                        pltpu.VMEM((nblk, ATTN_KV_HEADS, 3 * BLOCK, Q_PER_KV * BLOCK), F32)],
        compiler_params=pltpu.CompilerParams(
            dimension_semantics=("arbitrary", "arbitrary"), vmem_limit_bytes=VMEM_LIMIT_BYTES),
    )(sinks.astype(F32), q, k, k, k, v, v, v)


def _dot_nt_split(a, b):
    a_hi, b_hi = a.astype(BF16), b.astype(BF16)
    a_lo = (a - a_hi.astype(F32)).astype(BF16)
    b_lo = (b - b_hi.astype(F32)).astype(BF16)
    dot = functools.partial(lax.dot_general, dimension_numbers=NT_DIMS,
                            preferred_element_type=F32)
    return dot(a_hi, b_hi) + dot(a_hi, b_lo) + dot(a_lo, b_hi)


def _ssm_ops_kernel(ldt_ref, lre_ref, lim_ref, d_ref, btr_ref, bti_ref, cre_ref, cim_ref,
                    ein_ref, toep_ref, eout_ref, aq_ref):
    q, hc, p = SSM_CHUNK, SSM_CH, SSM_STATE
    gs = lre_ref.shape[1]
    g_base = pl.program_id(0) * gs
    both_dirs = lambda ref, gi: jnp.concatenate([ref[0, gi], ref[1, gi]], axis=1)
    fwd = lax.broadcasted_iota(jnp.int32, (1, 2 * p), 1) < p
    zero_row = jnp.zeros((1, 2 * p), F32)
    row_id = lax.broadcasted_iota(jnp.int32, (CHUNK_WIDTH, CHUNK_WIDTH), 0)
    col_id = lax.broadcasted_iota(jnp.int32, (CHUNK_WIDTH, CHUNK_WIDTH), 1)

    def table(select, n):
        picks = [select(m) for m in range(n)]
        re = jnp.concatenate([jnp.broadcast_to(r, (hc, 2 * p)) for r, _ in picks], axis=0)
        im = jnp.concatenate([jnp.broadcast_to(i, (hc, 2 * p)) for _, i in picks], axis=0)
        return re, im

    def tile_rows(x, n):
        return jnp.concatenate([x] * n, axis=0)

    for gi in range(gs):
        lr = jnp.minimum(jnp.concatenate([lre_ref[0, gi:gi + 1, :], lre_ref[1, gi:gi + 1, :]],
                                         axis=1), LAMBDA_RE_MAX)
        li = jnp.concatenate([lim_ref[0, gi:gi + 1, :], lim_ref[1, gi:gi + 1, :]], axis=1)
        dt = jnp.exp(jnp.where(fwd, ldt_ref[0, g_base + gi], ldt_ref[1, g_base + gi]))
        mag = jnp.exp(lr * dt)
        a_r = mag * jnp.cos(li * dt)
        a_i = mag * jnp.sin(li * dt)
        den = lr * lr + li * li
        coef_r = ((a_r - 1.0) * lr + a_i * li) / den
        coef_i = (a_i * lr - (a_r - 1.0) * li) / den
        b_r, b_i = both_dirs(btr_ref, gi), both_dirs(bti_ref, gi)
        c_r, c_i = both_dirs(cre_ref, gi), both_dirs(cim_ref, gi)
        bb_r = coef_r * b_r - coef_i * b_i
        bb_i = coef_r * b_i + coef_i * b_r

        pw = [(jnp.ones((1, 2 * p), F32), zero_row)]
        for _ in range(q):
            r, i = pw[-1]
            pw.append((r * a_r - i * a_i, r * a_i + i * a_r))

        def both(f_idx, b_idx):
            fr, fi = pw[f_idx] if f_idx is not None else (zero_row, zero_row)
            br, bi = pw[b_idx] if b_idx is not None else (zero_row, zero_row)
            return jnp.where(fwd, fr, br), jnp.where(fwd, fi, bi)

        p_r, p_i = table(lambda i: both(q - 1 - i, i), q)
        tb_r, tb_i = tile_rows(bb_r, q), tile_rows(bb_i, q)
        ein = jnp.concatenate([tb_r * p_r - tb_i * p_i, tb_r * p_i + tb_i * p_r], axis=1)
        ein_ref[gi] = ein.astype(BF16)

        p_r, p_i = table(lambda j: both(j + 1, q - j), q)
        tc_r, tc_i = tile_rows(c_r, q), tile_rows(c_i, q)
        eout = jnp.concatenate([tc_r * p_r - tc_i * p_i, -(tc_r * p_i + tc_i * p_r)], axis=1)
        eout_ref[gi] = eout.astype(BF16)

        def lag(m):
            return both(m - (q - 1) if q - 1 <= m <= 2 * q - 2 else None,
                        (q - 1) - m if m <= q - 1 else None)

        p_r, p_i = table(lag, 2 * q)
        tc_r, tc_i = tile_rows(c_r, 2 * q), tile_rows(c_i, 2 * q)
        cpt = jnp.concatenate([tc_r * p_r - tc_i * p_i, tc_r * p_i + tc_i * p_r], axis=1)
        bcat = jnp.concatenate([bb_r, -bb_i], axis=1)
        kern = _dot_nt_split(bcat, cpt)
        toep = jnp.concatenate(
            [kern[:, hc * (q - 1 - i):hc * (q - 1 - i) + CHUNK_WIDTH] for i in range(q)], axis=0)
        skip = jnp.concatenate([d_ref[gi:gi + 1, :]] * q, axis=1)
        toep_ref[gi] = (toep + jnp.where(row_id == col_id, skip, 0.0)).astype(BF16)

        aq_ref[gi] = jnp.concatenate(
            [pw[q][0], pw[q][1], jnp.zeros((SUBLANES - 2, 2 * p), F32)], axis=0)


def _ssm_operators(lam_re, lam_im, log_dt, b_re, b_im, c_re, c_im, d_skip):
    g, p, hc = SSM_GROUPS, SSM_STATE, SSM_CH
    f32 = lambda a: a.astype(F32)
    channel_major = lambda b: f32(b).transpose(0, 1, 3, 2)
    gs = OPS_GROUPS_PER_STEP
    mat = pl.BlockSpec((gs, CHUNK_WIDTH, CHUNK_WIDTH), lambda s: (s, 0, 0))
    mat_shape = jax.ShapeDtypeStruct((g, CHUNK_WIDTH, CHUNK_WIDTH), BF16)
    per_dir = lambda *tail: pl.BlockSpec((2, gs) + tail, lambda s: (0, s) + (0,) * len(tail))
    return pl.pallas_call(
        _ssm_ops_kernel,
        grid=(g // gs,),
        in_specs=[pl.BlockSpec(memory_space=pltpu.SMEM), per_dir(p), per_dir(p),
                  pl.BlockSpec((gs, hc), lambda s: (s, 0)),
                  per_dir(hc, p), per_dir(hc, p), per_dir(hc, p), per_dir(hc, p)],
        out_specs=[mat, mat, mat, pl.BlockSpec((gs, SUBLANES, 2 * p), lambda s: (s, 0, 0))],
        out_shape=[mat_shape, mat_shape, mat_shape,
                   jax.ShapeDtypeStruct((g, SUBLANES, 2 * p), F32)],
        compiler_params=pltpu.CompilerParams(
            dimension_semantics=("parallel",), vmem_limit_bytes=VMEM_LIMIT_BYTES),
    )(f32(log_dt), f32(lam_re), f32(lam_im), f32(d_skip), channel_major(b_re), channel_major(b_im),
      f32(c_re), f32(c_im))


def _lane_roll(x, shift):
    shift %= LANES
    return jnp.concatenate([x[:, LANES - shift:], x[:, :LANES - shift]], axis=1)


def _block_transposes(groups, lane_block):
    n = len(groups[0])
    skewed = [[v[i] if i == 0 else _lane_roll(v[i], SSM_CH * i) for i in range(n)] for v in groups]
    picked = []
    for sk in skewed:
        rows = []
        for g in range(n):
            p = sk[(0 - g) % n]
            for j in range(1, n):
                p = jnp.where(lane_block == j, sk[(j - g) % n], p)
            rows.append(p)
        picked.append(rows)
    return [[p[g] if g == 0 else _lane_roll(p[g], -SSM_CH * g) for g in range(n)] for p in picked]


def _ssm_kernel(u_hbm, ein_ref, toep_ref, eout_ref, aq_ref, y_hbm,
                io_ref, ug_ref, s_ref, x_ref, in_sem, out_sem, *, batch, seq, pitch):
    q, half = SSM_CHUNK, SSM_STATE
    n_chunks = seq // q
    gpb = GROUPS_PER_BLOCK
    slab = 2 * batch
    lane_block = lax.broadcasted_iota(jnp.int32, (slab, LANES), 1) // SSM_CH
    step_id = pl.program_id(0)
    last_step = pl.num_programs(0) - 1
    slot = lax.rem(step_id, 2)

    n_slices = SSM_DMA_SLICES
    slice_rows = seq // n_slices
    slice_trips = n_chunks // (2 * n_slices)

    def column_copies(block, slot_, to_vmem, slices=range(SSM_DMA_SLICES)):
        lanes = pl.ds(pl.multiple_of(block * LANES, LANES), LANES)
        copies = []
        for k in slices:
            for b in range(batch):
                hbm = (u_hbm if to_vmem else y_hbm).at[b, pl.ds(k * slice_rows, slice_rows), lanes]
                vmem = io_ref.at[slot_, pl.ds(b * pitch + k * slice_rows, slice_rows), :]
                if to_vmem:
                    copies.append(pltpu.make_async_copy(hbm, vmem, in_sem.at[slot_, k, b]))
                else:
                    copies.append(pltpu.make_async_copy(vmem, hbm, out_sem.at[slot_, k, b]))
        return copies

    def start(copies):
        for copy in copies:
            copy.start()

    def wait(copies):
        for copy in copies:
            copy.wait()

    @pl.when(step_id == 0)
    def _():
        start(column_copies(0, 0, True))

    @pl.when(step_id < last_step)
    def _():
        @pl.when(step_id >= 1)
        def _():
            wait(column_copies(step_id - 1, 1 - slot, False))
        start(column_copies(step_id + 1, 1 - slot, True))

    buf = io_ref.at[slot]

    def gather(cp, carry):
        rows = pl.ds(pl.multiple_of(cp * slab, slab), slab)
        parts = []
        for part in range(q // gpb):
            t0 = cp * 2 * q + part * gpb
            parts.append([jnp.concatenate([buf[pl.ds(t0 + i, batch, stride=pitch), :],
                                           buf[pl.ds(t0 + q + i, batch, stride=pitch), :]],
                                          axis=0).astype(BF16) for i in range(gpb)])
        for part, w in enumerate(_block_transposes(parts, lane_block)):
            for g in range(gpb):
                ug_ref[g, rows, part * LANES:(part + 1) * LANES] = w[g]
        return carry

    for k in range(n_slices):
        wait(column_copies(step_id, slot, True, slices=(k,)))
        lax.fori_loop(k * slice_trips, (k + 1) * slice_trips, gather, 0,
                      unroll=SSM_RELAYOUT_UNROLL)

    fwd_lane = lax.broadcasted_iota(jnp.int32, (batch, 2 * half), 1) < half
    zeros = jnp.zeros((batch, half), F32)
    last_rows = pl.ds((n_chunks - 1) * batch, batch)
    ni = SSM_INTERLEAVE
    for g0 in range(0, gpb, ni):
        for gi in range(ni):
            s_ref[gi] = jnp.dot(ug_ref[g0 + gi], ein_ref[g0 + gi], preferred_element_type=F32)
            x_ref[gi, 0:batch, 0:half] = zeros
            x_ref[gi, 0:batch, 2 * half:3 * half] = zeros
            x_ref[gi, last_rows, half:2 * half] = zeros
            x_ref[gi, last_rows, 3 * half:4 * half] = zeros

        def step(k, carry):
            rf = pl.multiple_of(k * batch, batch)
            rb = pl.multiple_of((n_chunks - 1 - k) * batch, batch)
            new = []
            for gi in range(ni):
                xr, xi = carry[gi]
                sre = jnp.where(fwd_lane, s_ref[gi, pl.ds(rf, batch), 0:2 * half],
                                s_ref[gi, pl.ds(rb, batch), 0:2 * half])
                sim = jnp.where(fwd_lane, s_ref[gi, pl.ds(rf, batch), 2 * half:4 * half],
                                s_ref[gi, pl.ds(rb, batch), 2 * half:4 * half])
                ar = aq_ref[g0 + gi, 0:1, :]
                ai = aq_ref[g0 + gi, 1:2, :]
                nr = ar * xr - ai * xi + sre
                nim = ar * xi + ai * xr + sim
                x_ref[gi, pl.ds(rf + batch, batch), 0:half] = nr[:, 0:half]
                x_ref[gi, pl.ds(rf + batch, batch), 2 * half:3 * half] = nim[:, 0:half]
                x_ref[gi, pl.ds(rb - batch, batch), half:2 * half] = nr[:, half:2 * half]
                x_ref[gi, pl.ds(rb - batch, batch), 3 * half:4 * half] = nim[:, half:2 * half]
                new.append((nr, nim))
            return tuple(new)

        init = tuple((jnp.zeros((batch, 2 * half), F32), jnp.zeros((batch, 2 * half), F32))
                     for _ in range(ni))
        lax.fori_loop(0, n_chunks - 1, step, init, unroll=True)

        for gi in range(ni):
            g = g0 + gi
            y = jnp.dot(ug_ref[g], toep_ref[g], preferred_element_type=F32)
            y += lax.dot_general(x_ref[gi].astype(BF16), eout_ref[g], NT_DIMS,
                                 preferred_element_type=F32)
            ug_ref[g] = y.astype(BF16)

    def scatter(cp, carry):
        rows = pl.ds(pl.multiple_of(cp * slab, slab), slab)
        parts = [[ug_ref[g, rows, part * LANES:(part + 1) * LANES] for g in range(gpb)]
                 for part in range(q // gpb)]
        for part, w in enumerate(_block_transposes(parts, lane_block)):
            t0 = cp * 2 * q + part * gpb
            for j in range(gpb):
                wj = w[j].astype(F32)
                buf[pl.ds(t0 + j, batch, stride=pitch), :] = wj[0:batch]
                buf[pl.ds(t0 + q + j, batch, stride=pitch), :] = wj[batch:slab]
        return carry

    for k in range(n_slices):
        lax.fori_loop(k * slice_trips, (k + 1) * slice_trips, scatter, 0,
                      unroll=SSM_RELAYOUT_UNROLL)
        start(column_copies(step_id, slot, False, slices=(k,)))

    @pl.when(step_id == last_step)
    def _():
        @pl.when(step_id >= 1)
        def _():
            wait(column_copies(step_id - 1, 1 - slot, False))
        wait(column_copies(step_id, slot, False))


def _padded_seq(seq):
    return seq + SUBLANES if seq % (2 * SUBLANES) == 0 else seq


def _ssm(u, lam_re, lam_im, log_dt, b_re, b_im, c_re, c_im, d_skip):
    batch, seq, _ = u.shape
    pitch = _padded_seq(seq)
    rows = batch * seq // SSM_CHUNK
    ein, toep, eout, a_q = _ssm_operators(lam_re, lam_im, log_dt, b_re, b_im, c_re, c_im, d_skip)
    gpb = GROUPS_PER_BLOCK
    mat = pl.BlockSpec((gpb, CHUNK_WIDTH, CHUNK_WIDTH), lambda s: (s, 0, 0))
    hbm = pl.BlockSpec(memory_space=pl.ANY)
    return pl.pallas_call(
        functools.partial(_ssm_kernel, batch=batch, seq=seq, pitch=pitch),
        grid=(SSM_GROUPS // gpb,),
        in_specs=[hbm, mat, mat, mat,
                  pl.BlockSpec((gpb, SUBLANES, 2 * SSM_STATE), lambda s: (s, 0, 0))],
        out_specs=hbm,
        out_shape=jax.ShapeDtypeStruct((batch, seq, SSM_WIDTH), F32),
        scratch_shapes=[pltpu.VMEM((2, batch * pitch, LANES), F32),
                        pltpu.VMEM((gpb, rows, CHUNK_WIDTH), BF16),
                        pltpu.VMEM((SSM_INTERLEAVE, rows, 4 * SSM_STATE), F32),
                        pltpu.VMEM((SSM_INTERLEAVE, rows, 4 * SSM_STATE), F32),
                        pltpu.SemaphoreType.DMA((2, SSM_DMA_SLICES, batch)),
                        pltpu.SemaphoreType.DMA((2, SSM_DMA_SLICES, batch))],
        compiler_params=pltpu.CompilerParams(
            dimension_semantics=("arbitrary",), vmem_limit_bytes=VMEM_LIMIT_BYTES),
    )(u, ein, toep, eout, a_q)


def kernel(x, norm_ffn1, ffn1_w_gate, ffn1_w_up, ffn1_w_down, norm_mix, w_in, attn_sinks,
           ssm_lambda_re, ssm_lambda_im, ssm_log_dt, ssm_b_re, ssm_b_im, ssm_c_re, ssm_c_im,
           ssm_d, ssm_glu_w, ssm_glu_b, attn_out_norm, ssm_out_norm, w_out,
           norm_ffn2, ffn2_w_gate, ffn2_w_up, ffn2_w_down, final_norm):
    b, seq, d = x.shape
    depth = norm_ffn1.shape[0]
    assert d == D_MODEL and seq % BLOCK == 0 and (b * seq) % FFN_TOKEN_TILE == 0
    assert b == SUBLANES and seq % (2 * SSM_CHUNK * SSM_RELAYOUT_UNROLL * SSM_DMA_SLICES) == 0
    h = x.reshape(b * seq, d).astype(F32)
    for l in range(depth):
        h, q, k, v, u = _ffn(h, norm_ffn1[l], ffn1_w_gate[l], ffn1_w_up[l], ffn1_w_down[l],
                             final_norm, final_norm=False, proj=(norm_mix[l], w_in[l]))
        attn = _attention(q.reshape(b, seq, -1), k.reshape(b, seq, -1), v.reshape(b, seq, -1),
                          attn_sinks[l])
        ssm_pre = _ssm(u.reshape(b, seq, -1), ssm_lambda_re[l], ssm_lambda_im[l], ssm_log_dt[l],
                       ssm_b_re[l], ssm_b_im[l], ssm_c_re[l], ssm_c_im[l], ssm_d[l])
        mix = (attn.reshape(b * seq, -1), ssm_pre.reshape(b * seq, -1), ssm_glu_w[l],
               ssm_glu_b[l], attn_out_norm[l], ssm_out_norm[l], w_out[l])
        h, = _ffn(h, norm_ffn2[l], ffn2_w_gate[l], ffn2_w_up[l], ffn2_w_down[l],
                  final_norm, final_norm=(l == depth - 1), mix=mix)
    return h.reshape(b, seq, d).astype(x.dtype)
```

```python
import functools
import math

import jax
import jax.numpy as jnp
from jax import lax
from jax.experimental import pallas as pl
from jax.experimental.pallas import tpu as pltpu

F32 = jnp.float32
BF16 = jnp.bfloat16

D_MODEL = 1024
ATTN_HEADS = 8
ATTN_KV_HEADS = 2
Q_PER_KV = ATTN_HEADS // ATTN_KV_HEADS
HEAD_DIM = 64
ATTN_WIDTH = ATTN_HEADS * HEAD_DIM
KV_WIDTH = ATTN_KV_HEADS * HEAD_DIM
WINDOW = 128
BLOCK = 128
SSM_CH = 16
SSM_WIDTH = D_MODEL - ATTN_WIDTH
SSM_GROUPS = SSM_WIDTH // SSM_CH
SSM_STATE = 64
IN_WIDTH = ATTN_WIDTH + 2 * KV_WIDTH + SSM_WIDTH
D_FF = 2816
EPS = 1e-6
NEG_INF = -1e30
LAMBDA_RE_MAX = -1e-4
LOG2_E = math.log2(math.e)
QUERY_SCALE = HEAD_DIM ** -0.5 * LOG2_E

LANES = 128
SUBLANES = 8
VMEM_LIMIT_BYTES = 56 * 1024 * 1024

FFN_TOKEN_TILE = 512
FFN_SUBTILES = 2
FF_TILE = 256
ATTN_BLOCKS_PER_STEP = 16
ATTN_HEADS_PER_PV = 2
SSM_CHUNK = 16
CHUNK_WIDTH = SSM_CHUNK * SSM_CH
GROUPS_PER_BLOCK = LANES // SSM_CH
SSM_INTERLEAVE = 8
SSM_RELAYOUT_UNROLL = 4
SSM_DMA_SLICES = 8

NT_DIMS = (((1,), (1,)), ((), ()))
TN_DIMS = (((0,), (0,)), ((), ()))


def _rms(x):
    return x * lax.rsqrt(jnp.mean(x * x, axis=-1, keepdims=True) + EPS)


def _mixed_update(rows, attn_ref, ssm_ref, gw_ref, gb_ref, ga_ref, gs_ref, wo_ref):
    y = jax.nn.gelu(ssm_ref[rows, :])
    z = jnp.dot(y.astype(BF16), gw_ref[...].astype(BF16), preferred_element_type=F32) + gb_ref[...]
    s = y * jax.nn.sigmoid(z)
    sn = _rms(s) * gs_ref[...]
    an = _rms(attn_ref[rows, :].astype(F32)) * ga_ref[...]
    mixed = jnp.concatenate([an, sn], axis=-1).astype(BF16)
    return jnp.dot(mixed, wo_ref[...].astype(BF16), preferred_element_type=F32)


def _mixer_inputs(y, rows, gain_ref, w_ref, q_ref, k_ref, v_ref, u_ref):
    hn = (_rms(y) * gain_ref[...]).astype(BF16)
    proj = jnp.dot(hn, w_ref[...].astype(BF16), preferred_element_type=F32)
    q_ref[rows, :] = (proj[:, :ATTN_WIDTH] * QUERY_SCALE).astype(BF16)
    k_ref[rows, :] = proj[:, ATTN_WIDTH:ATTN_WIDTH + KV_WIDTH].astype(BF16)
    v_ref[rows, :] = proj[:, ATTN_WIDTH + KV_WIDTH:ATTN_WIDTH + 2 * KV_WIDTH].astype(BF16)
    u_ref[rows, :] = proj[:, ATTN_WIDTH + 2 * KV_WIDTH:]


def _ffn_body(read_x, gain_ref, fgain_ref, wg_hbm, wu_hbm, wd_hbm, o_ref,
              wg_ref, wu_ref, wd_ref, act_ref, gu_stage, d_stage, sem, *, final_norm,
              epilogue=None):
    nj = D_FF // FF_TILE

    def weight_copies(j, slot):
        span = pl.ds(j * FF_TILE, FF_TILE)
        return (pltpu.make_async_copy(wg_hbm.at[:, span], gu_stage.at[0, slot], sem.at[0, slot]),
                pltpu.make_async_copy(wu_hbm.at[:, span], gu_stage.at[1, slot], sem.at[1, slot]),
                pltpu.make_async_copy(wd_hbm.at[span, :], d_stage.at[slot], sem.at[2, slot]))

    def step(stage_weights):
        if stage_weights:
            for copy in weight_copies(0, 0):
                copy.start()
        sub = o_ref.shape[0] // FFN_SUBTILES
        row_groups = [slice(h * sub, (h + 1) * sub) for h in range(FFN_SUBTILES)]
        xs = [read_x(rows) for rows in row_groups]
        hns = [(_rms(x) * gain_ref[...]).astype(BF16) for x in xs]
        for j in range(nj):
            cols = slice(j * FF_TILE, (j + 1) * FF_TILE)
            if stage_weights:
                slot = j % 2
                if j + 1 < nj:
                    for copy in weight_copies(j + 1, 1 - slot):
                        copy.start()
                for copy in weight_copies(j, slot):
                    copy.wait()
                wg_ref[:, cols] = gu_stage[0, slot].astype(BF16)
                wu_ref[:, cols] = gu_stage[1, slot].astype(BF16)
                wd_ref[cols, :] = d_stage[slot].astype(BF16)
            for rows, hn in zip(row_groups, hns):
                g = jnp.dot(hn, wg_ref[:, cols], preferred_element_type=F32)
                u = jnp.dot(hn, wu_ref[:, cols], preferred_element_type=F32)
                act_ref[rows, cols] = (g * jax.nn.sigmoid(g) * u).astype(BF16)
        for rows, x in zip(row_groups, xs):
            y = x + 0.5 * jnp.dot(act_ref[rows, :], wd_ref[...], preferred_element_type=F32)
            if final_norm:
                y = _rms(y) * fgain_ref[...]
            o_ref[rows, :] = y
            if epilogue is not None:
                epilogue(y, rows)

    pl.when(pl.program_id(0) == 0)(functools.partial(step, True))
    pl.when(pl.program_id(0) != 0)(functools.partial(step, False))


def _ffn_proj_kernel(x_ref, mgain_ref, win_ref, gain_ref, fgain_ref, wg_hbm, wu_hbm, wd_hbm,
                     o_ref, q_ref, k_ref, v_ref, u_ref, *scratch, final_norm):
    emit = lambda y, rows: _mixer_inputs(y, rows, mgain_ref, win_ref, q_ref, k_ref, v_ref, u_ref)
    _ffn_body(lambda rows: x_ref[rows, :], gain_ref, fgain_ref, wg_hbm, wu_hbm, wd_hbm, o_ref,
              *scratch, final_norm=final_norm, epilogue=emit)


def _mix_ffn_kernel(x_ref, attn_ref, ssm_ref, gw_ref, gb_ref, ga_ref, gs_ref, wo_ref, *ffn_refs,
                    final_norm):
    read_x = lambda rows: x_ref[rows, :] + _mixed_update(rows, attn_ref, ssm_ref, gw_ref, gb_ref,
                                                         ga_ref, gs_ref, wo_ref)
    _ffn_body(read_x, *ffn_refs, final_norm=final_norm)


def _ffn(x, gain, w_gate, w_up, w_down, final_gain, final_norm, mix=None, proj=None):
    assert (mix is None) != (proj is None)
    t = x.shape[0]
    tile = FFN_TOKEN_TILE
    vec = lambda width: pl.BlockSpec((1, width), lambda i: (0, 0))
    resident = lambda r, c: pl.BlockSpec((r, c), lambda i: (0, 0), pipeline_mode=pl.Buffered(1))
    row = lambda width: pl.BlockSpec((tile, width), lambda i: (i, 0))
    hbm = pl.BlockSpec(memory_space=pl.ANY)
    in_specs, args = [row(D_MODEL)], [x]
    out_specs, out_shape = [row(D_MODEL)], [jax.ShapeDtypeStruct((t, D_MODEL), F32)]
    if mix is not None:
        attn, ssm_pre, glu_w, glu_b, attn_gain, ssm_gain, w_out = mix
        body = _mix_ffn_kernel
        in_specs += [row(ATTN_WIDTH), row(SSM_WIDTH),
                     resident(SSM_WIDTH, SSM_WIDTH), vec(SSM_WIDTH), vec(ATTN_WIDTH),
                     vec(SSM_WIDTH), resident(D_MODEL, D_MODEL)]
        args += [attn, ssm_pre, glu_w, glu_b.reshape(1, -1).astype(F32),
                 attn_gain.reshape(1, -1).astype(F32), ssm_gain.reshape(1, -1).astype(F32), w_out]
    if proj is not None:
        mixer_gain, w_in = proj
        body = _ffn_proj_kernel
        in_specs += [vec(D_MODEL), resident(D_MODEL, IN_WIDTH)]
        args += [mixer_gain.reshape(1, D_MODEL), w_in]
        for width, dtype in ((ATTN_WIDTH, BF16), (KV_WIDTH, BF16), (KV_WIDTH, BF16),
                             (SSM_WIDTH, F32)):
            out_specs.append(row(width))
            out_shape.append(jax.ShapeDtypeStruct((t, width), dtype))
    in_specs += [vec(D_MODEL), vec(D_MODEL), hbm, hbm, hbm]
    args += [gain.reshape(1, D_MODEL), final_gain.reshape(1, D_MODEL), w_gate, w_up, w_down]
    return pl.pallas_call(
        functools.partial(body, final_norm=final_norm),
        grid=(t // tile,),
        in_specs=in_specs,
        out_specs=out_specs,
        out_shape=out_shape,
        scratch_shapes=[pltpu.VMEM((D_MODEL, D_FF), BF16),
                        pltpu.VMEM((D_MODEL, D_FF), BF16),
                        pltpu.VMEM((D_FF, D_MODEL), BF16),
                        pltpu.VMEM((tile, D_FF), BF16),
                        pltpu.VMEM((2, 2, D_MODEL, FF_TILE), F32),
                        pltpu.VMEM((2, FF_TILE, D_MODEL), F32),
                        pltpu.SemaphoreType.DMA((3, 2))],
        compiler_params=pltpu.CompilerParams(
            dimension_semantics=("arbitrary",), vmem_limit_bytes=VMEM_LIMIT_BYTES),
    )(*args)


def _attn_kernel(sink_ref, q_ref, kp_ref, kc_ref, kn_ref, vp_ref, vc_ref, vn_ref, o_ref,
                 bias_ref, k_ref, v_ref, s_ref):
    n = pl.program_id(1)
    last = pl.num_programs(1) - 1
    nblk = ATTN_BLOCKS_PER_STEP

    @pl.when((pl.program_id(0) == 0) & (n == 0))
    def _():
        kj = lax.broadcasted_iota(jnp.int32, (3 * BLOCK, BLOCK), 0)
        qi = lax.broadcasted_iota(jnp.int32, (3 * BLOCK, BLOCK), 1)
        rel = jnp.abs(kj - BLOCK - qi)
        dist = rel.astype(F32)
        inside = rel <= WINDOW
        has_prev = kj >= BLOCK
        has_next = kj < 2 * BLOCK
        for variant, ok in enumerate((inside & has_prev, inside, inside & has_next)):
            for h in range(ATTN_HEADS):
                slope = float(2.0 ** (-8.0 * (h + 1) / ATTN_HEADS))
                bias_ref[variant, h] = jnp.where(ok, (-slope * LOG2_E) * dist, NEG_INF)
        for kh in range(ATTN_KV_HEADS):
            v_ref[:, (2 * kh + 1) * HEAD_DIM:(2 * kh + 2) * HEAD_DIM] = jnp.ones(
                (v_ref.shape[0], HEAD_DIM), BF16)

    spans = ((slice(0, BLOCK), kp_ref, vp_ref),
             (slice(BLOCK, (nblk + 1) * BLOCK), kc_ref, vc_ref),
             (slice((nblk + 1) * BLOCK, (nblk + 2) * BLOCK), kn_ref, vn_ref))
    for rows, k_in, v_in in spans:
        k_ref[rows, :] = k_in[...]
        for kh in range(ATTN_KV_HEADS):
            v_ref[rows, 2 * kh * HEAD_DIM:(2 * kh + 1) * HEAD_DIM] = (
                v_in[:, kh * HEAD_DIM:(kh + 1) * HEAD_DIM])

    def scores(j, kh):
        kcat = k_ref[j * BLOCK:(j + 3) * BLOCK, kh * HEAD_DIM:(kh + 1) * HEAD_DIM]
        heads = [kh * Q_PER_KV + g for g in range(Q_PER_KV)]
        qs = jnp.concatenate(
            [q_ref[j * BLOCK:(j + 1) * BLOCK, h * HEAD_DIM:(h + 1) * HEAD_DIM] for h in heads],
            axis=0)
        s_ref[j, kh] = lax.dot_general(kcat, qs, NT_DIMS,
                                       preferred_element_type=F32)

    for kh in range(ATTN_KV_HEADS):
        scores(0, kh)
    for j in range(nblk):
        variant = 1
        if j == 0:
            variant = jnp.where(n == 0, 0, variant)
        if j == nblk - 1:
            variant = jnp.where(n == last, 2, variant)
        outs = []
        for kh in range(ATTN_KV_HEADS):
            if j + 1 < nblk:
                scores(j + 1, kh)
            v_ones = v_ref[j * BLOCK:(j + 3) * BLOCK, 2 * kh * HEAD_DIM:(2 * kh + 2) * HEAD_DIM]
            probs, sink_terms = [], []
            for g in range(Q_PER_KV):
                h = kh * Q_PER_KV + g
                s = s_ref[j, kh, :, g * BLOCK:(g + 1) * BLOCK] + bias_ref[variant, h]
                sink = sink_ref[h] * LOG2_E
                m = jnp.maximum(jnp.max(s, axis=0, keepdims=True), sink)
                probs.append(jnp.exp2(s - m).astype(BF16))
                sink_terms.append(jnp.exp2(sink - m))
            hp = ATTN_HEADS_PER_PV
            for g in range(0, Q_PER_KV, hp):
                pv = lax.dot_general(v_ones, jnp.concatenate(probs[g:g + hp], axis=1), TN_DIMS,
                                     preferred_element_type=F32)
                for t in range(hp):
                    cols = slice(t * BLOCK, (t + 1) * BLOCK)
                    den = pv[HEAD_DIM:HEAD_DIM + 1, cols] + sink_terms[g + t]
                    outs.append(pv[:HEAD_DIM, cols] / den)
        o_ref[j * BLOCK:(j + 1) * BLOCK, :] = jnp.concatenate(outs, axis=0).T.astype(o_ref.dtype)


def _attention(q, k, v, sinks):
    b, seq, _ = q.shape
    nblk = ATTN_BLOCKS_PER_STEP
    steps = seq // (nblk * BLOCK)
    nb = seq // BLOCK
    assert seq % (nblk * BLOCK) == 0 and nb >= 2
    edge = lambda f: pl.BlockSpec((None, BLOCK, KV_WIDTH), f)
    body = pl.BlockSpec((None, nblk * BLOCK, KV_WIDTH), lambda bi, n: (bi, n, 0))
    prev = lambda bi, n: (bi, jnp.maximum(n * nblk - 1, 0), 0)
    nxt = lambda bi, n: (bi, jnp.minimum((n + 1) * nblk, nb - 1), 0)
    rows = pl.BlockSpec((None, nblk * BLOCK, ATTN_WIDTH), lambda bi, n: (bi, n, 0))
    return pl.pallas_call(
        _attn_kernel,
        grid=(b, steps),
        in_specs=[pl.BlockSpec(memory_space=pltpu.SMEM), rows,
                  edge(prev), body, edge(nxt), edge(prev), body, edge(nxt)],
        out_specs=rows,
        out_shape=jax.ShapeDtypeStruct((b, seq, ATTN_WIDTH), BF16),
        scratch_shapes=[pltpu.VMEM((3, ATTN_HEADS, 3 * BLOCK, BLOCK), F32),
                        pltpu.VMEM(((nblk + 2) * BLOCK, KV_WIDTH), BF16),
                        pltpu.VMEM(((nblk + 2) * BLOCK, 2 * KV_WIDTH), BF16),
                        pltpu.VMEM((nblk, ATTN_KV_HEADS, 3 * BLOCK, Q_PER_KV * BLOCK), F32)],
        compiler_params=pltpu.CompilerParams(
            dimension_semantics=("arbitrary", "arbitrary"), vmem_limit_bytes=VMEM_LIMIT_BYTES),
    )(sinks.astype(F32), q, k, k, k, v, v, v)


def _dot_nt_split(a, b):
    a_hi, b_hi = a.astype(BF16), b.astype(BF16)
    a_lo = (a - a_hi.astype(F32)).astype(BF16)
    b_lo = (b - b_hi.astype(F32)).astype(BF16)
    dot = functools.partial(lax.dot_general, dimension_numbers=NT_DIMS,
                            preferred_element_type=F32)
    return dot(a_hi, b_hi) + dot(a_hi, b_lo) + dot(a_lo, b_hi)


def _build_operators(ldt_ref, lre_ref, lim_ref, d_ref, btr_ref, bti_ref, cre_ref, cim_ref,
                     ein_ref, toep_ref, eout_ref, aq_ref, g_base, gs):
    q, hc, p = SSM_CHUNK, SSM_CH, SSM_STATE
    both_dirs = lambda ref, g: jnp.concatenate([ref[0, g], ref[1, g]], axis=1)
    mode_row = lambda ref, g: jnp.concatenate([ref[0, pl.ds(g, 1), :], ref[1, pl.ds(g, 1), :]],
                                              axis=1)
    fwd = lax.broadcasted_iota(jnp.int32, (1, 2 * p), 1) < p
    zero_row = jnp.zeros((1, 2 * p), F32)
    row_id = lax.broadcasted_iota(jnp.int32, (CHUNK_WIDTH, CHUNK_WIDTH), 0)
    col_id = lax.broadcasted_iota(jnp.int32, (CHUNK_WIDTH, CHUNK_WIDTH), 1)

    def table(select, n):
        picks = [select(m) for m in range(n)]
        re = jnp.concatenate([jnp.broadcast_to(r, (hc, 2 * p)) for r, _ in picks], axis=0)
        im = jnp.concatenate([jnp.broadcast_to(i, (hc, 2 * p)) for _, i in picks], axis=0)
        return re, im

    def tile_rows(x, n):
        return jnp.concatenate([x] * n, axis=0)

    for gi in range(gs):
        g = g_base + gi
        lr = jnp.minimum(mode_row(lre_ref, g), LAMBDA_RE_MAX)
        li = mode_row(lim_ref, g)
        dt = jnp.exp(jnp.where(fwd, ldt_ref[0, g], ldt_ref[1, g]))
        mag = jnp.exp(lr * dt)
        a_r = mag * jnp.cos(li * dt)
        a_i = mag * jnp.sin(li * dt)
        den = lr * lr + li * li
        coef_r = ((a_r - 1.0) * lr + a_i * li) / den
        coef_i = (a_i * lr - (a_r - 1.0) * li) / den
        b_r, b_i = both_dirs(btr_ref, g), both_dirs(bti_ref, g)
        c_r, c_i = both_dirs(cre_ref, g), both_dirs(cim_ref, g)
        bb_r = coef_r * b_r - coef_i * b_i
        bb_i = coef_r * b_i + coef_i * b_r

        pw = [(jnp.ones((1, 2 * p), F32), zero_row)]
        for _ in range(q):
            r, i = pw[-1]
            pw.append((r * a_r - i * a_i, r * a_i + i * a_r))

        def both(f_idx, b_idx):
            fr, fi = pw[f_idx] if f_idx is not None else (zero_row, zero_row)
            br, bi = pw[b_idx] if b_idx is not None else (zero_row, zero_row)
            return jnp.where(fwd, fr, br), jnp.where(fwd, fi, bi)

        p_r, p_i = table(lambda i: both(q - 1 - i, i), q)
        tb_r, tb_i = tile_rows(bb_r, q), tile_rows(bb_i, q)
        ein = jnp.concatenate([tb_r * p_r - tb_i * p_i, tb_r * p_i + tb_i * p_r], axis=1)
        ein_ref[gi] = ein.astype(BF16)

        p_r, p_i = table(lambda j: both(j + 1, q - j), q)
        tc_r, tc_i = tile_rows(c_r, q), tile_rows(c_i, q)
        eout = jnp.concatenate([tc_r * p_r - tc_i * p_i, -(tc_r * p_i + tc_i * p_r)], axis=1)
        eout_ref[gi] = eout.astype(BF16)

        def lag(m):
            return both(m - (q - 1) if q - 1 <= m <= 2 * q - 2 else None,
                        (q - 1) - m if m <= q - 1 else None)

        p_r, p_i = table(lag, 2 * q)
        tc_r, tc_i = tile_rows(c_r, 2 * q), tile_rows(c_i, 2 * q)
        cpt = jnp.concatenate([tc_r * p_r - tc_i * p_i, tc_r * p_i + tc_i * p_r], axis=1)
        bcat = jnp.concatenate([bb_r, -bb_i], axis=1)
        kern = _dot_nt_split(bcat, cpt)
        toep = jnp.concatenate(
            [kern[:, hc * (q - 1 - i):hc * (q - 1 - i) + CHUNK_WIDTH] for i in range(q)], axis=0)
        skip = jnp.concatenate([d_ref[pl.ds(g, 1), :]] * q, axis=1)
        toep_ref[gi] = (toep + jnp.where(row_id == col_id, skip, 0.0)).astype(BF16)

        aq_ref[gi] = jnp.concatenate(
            [pw[q][0], pw[q][1], jnp.zeros((SUBLANES - 2, 2 * p), F32)], axis=0)


def _lane_roll(x, shift):
    shift %= LANES
    return jnp.concatenate([x[:, LANES - shift:], x[:, :LANES - shift]], axis=1)


def _block_transpose(v, lane_block):
    return _block_transposes([v], lane_block)[0]


def _block_transposes(groups, lane_block):
    n = len(groups[0])
    skewed = [[v[i] if i == 0 else _lane_roll(v[i], SSM_CH * i) for i in range(n)] for v in groups]
    picked = []
    for sk in skewed:
        rows = []
        for g in range(n):
            p = sk[(0 - g) % n]
            for j in range(1, n):
                p = jnp.where(lane_block == j, sk[(j - g) % n], p)
            rows.append(p)
        picked.append(rows)
    return [[p[g] if g == 0 else _lane_roll(p[g], -SSM_CH * g) for g in range(n)] for p in picked]


def _ssm_kernel(u_hbm, ldt_ref, lre_ref, lim_ref, d_ref, btr_ref, bti_ref, cre_ref, cim_ref, y_hbm,
                io_ref, ug_ref, s_ref, x_ref, ein_s, toep_s, eout_s, aq_s, in_sem, out_sem,
                *, batch, seq, pitch):
    q, half = SSM_CHUNK, SSM_STATE
    n_chunks = seq // q
    gpb = GROUPS_PER_BLOCK
    slab = 2 * batch
    lane_block = lax.broadcasted_iota(jnp.int32, (slab, LANES), 1) // SSM_CH
    step_id = pl.program_id(0)
    last_step = pl.num_programs(0) - 1
    slot = lax.rem(step_id, 2)
    params = (ldt_ref, lre_ref, lim_ref, d_ref, btr_ref, bti_ref, cre_ref, cim_ref)
    operators = lambda sl: (ein_s.at[sl], toep_s.at[sl], eout_s.at[sl], aq_s.at[sl])
    ein_ref, toep_ref, eout_ref, aq_ref = operators(slot)

    @pl.when(step_id == 0)
    def _():
        _build_operators(*params, *operators(0), 0, gpb)

    n_slices = SSM_DMA_SLICES
    slice_rows = seq // n_slices
    slice_trips = n_chunks // (2 * n_slices)

    def column_copies(block, slot_, to_vmem, slices=range(SSM_DMA_SLICES)):
        lanes = pl.ds(pl.multiple_of(block * LANES, LANES), LANES)
        copies = []
        for k in slices:
            for b in range(batch):
                hbm = (u_hbm if to_vmem else y_hbm).at[b, pl.ds(k * slice_rows, slice_rows), lanes]
                vmem = io_ref.at[slot_, pl.ds(b * pitch + k * slice_rows, slice_rows), :]
                if to_vmem:
                    copies.append(pltpu.make_async_copy(hbm, vmem, in_sem.at[slot_, k, b]))
                else:
                    copies.append(pltpu.make_async_copy(vmem, hbm, out_sem.at[slot_, k, b]))
        return copies

    def start(copies):
        for copy in copies:
            copy.start()

    def wait(copies):
        for copy in copies:
            copy.wait()

    @pl.when(step_id == 0)
    def _():
        start(column_copies(0, 0, True))

    @pl.when(step_id < last_step)
    def _():
        @pl.when(step_id >= 1)
        def _():
            wait(column_copies(step_id - 1, 1 - slot, False))
        start(column_copies(step_id + 1, 1 - slot, True))

    buf = io_ref.at[slot]

    def gather(cp, carry):
        rows = pl.ds(pl.multiple_of(cp * slab, slab), slab)
        parts = []
        for part in range(q // gpb):
            t0 = cp * 2 * q + part * gpb
            parts.append([jnp.concatenate([buf[pl.ds(t0 + i, batch, stride=pitch), :],
                                           buf[pl.ds(t0 + q + i, batch, stride=pitch), :]],
                                          axis=0).astype(BF16) for i in range(gpb)])
        for part, w in enumerate(_block_transposes(parts, lane_block)):
            for g in range(gpb):
                ug_ref[g, rows, part * LANES:(part + 1) * LANES] = w[g]
        return carry

    for k in range(n_slices):
        wait(column_copies(step_id, slot, True, slices=(k,)))
        lax.fori_loop(k * slice_trips, (k + 1) * slice_trips, gather, 0,
                      unroll=SSM_RELAYOUT_UNROLL)

    fwd_lane = lax.broadcasted_iota(jnp.int32, (batch, 2 * half), 1) < half
    zeros = jnp.zeros((batch, half), F32)
    last_rows = pl.ds((n_chunks - 1) * batch, batch)
    ni = SSM_INTERLEAVE
    for g0 in range(0, gpb, ni):
        for gi in range(ni):
            s_ref[gi] = jnp.dot(ug_ref[g0 + gi], ein_ref[g0 + gi], preferred_element_type=F32)
            x_ref[gi, 0:batch, 0:half] = zeros
            x_ref[gi, 0:batch, 2 * half:3 * half] = zeros
            x_ref[gi, last_rows, half:2 * half] = zeros
            x_ref[gi, last_rows, 3 * half:4 * half] = zeros

        if g0 + ni == gpb:
            _build_operators(*params, *operators(1 - slot),
                             jnp.minimum(step_id + 1, last_step) * gpb, gpb)

        def step(k, carry):
            rf = pl.multiple_of(k * batch, batch)
            rb = pl.multiple_of((n_chunks - 1 - k) * batch, batch)
            new = []
            for gi in range(ni):
                xr, xi = carry[gi]
                sre = jnp.where(fwd_lane, s_ref[gi, pl.ds(rf, batch), 0:2 * half],
                                s_ref[gi, pl.ds(rb, batch), 0:2 * half])
                sim = jnp.where(fwd_lane, s_ref[gi, pl.ds(rf, batch), 2 * half:4 * half],
                                s_ref[gi, pl.ds(rb, batch), 2 * half:4 * half])
                ar = aq_ref[g0 + gi, 0:1, :]
                ai = aq_ref[g0 + gi, 1:2, :]
                nr = ar * xr - ai * xi + sre
                nim = ar * xi + ai * xr + sim
                x_ref[gi, pl.ds(rf + batch, batch), 0:half] = nr[:, 0:half]
                x_ref[gi, pl.ds(rf + batch, batch), 2 * half:3 * half] = nim[:, 0:half]
                x_ref[gi, pl.ds(rb - batch, batch), half:2 * half] = nr[:, half:2 * half]
                x_ref[gi, pl.ds(rb - batch, batch), 3 * half:4 * half] = nim[:, half:2 * half]
                new.append((nr, nim))
            return tuple(new)

        init = tuple((jnp.zeros((batch, 2 * half), F32), jnp.zeros((batch, 2 * half), F32))
                     for _ in range(ni))
        lax.fori_loop(0, n_chunks - 1, step, init, unroll=True)

        for gi in range(ni):
            g = g0 + gi
            y = jnp.dot(ug_ref[g], toep_ref[g], preferred_element_type=F32)
            y += lax.dot_general(x_ref[gi].astype(BF16), eout_ref[g], NT_DIMS,
                                 preferred_element_type=F32)
            ug_ref[g] = y.astype(BF16)

    def scatter(cp, carry):
        rows = pl.ds(pl.multiple_of(cp * slab, slab), slab)
        parts = [[ug_ref[g, rows, part * LANES:(part + 1) * LANES] for g in range(gpb)]
                 for part in range(q // gpb)]
        for part, w in enumerate(_block_transposes(parts, lane_block)):
            t0 = cp * 2 * q + part * gpb
            for j in range(gpb):
                wj = w[j].astype(F32)
                buf[pl.ds(t0 + j, batch, stride=pitch), :] = wj[0:batch]
                buf[pl.ds(t0 + q + j, batch, stride=pitch), :] = wj[batch:slab]
        return carry

    for k in range(n_slices):
        lax.fori_loop(k * slice_trips, (k + 1) * slice_trips, scatter, 0,
                      unroll=SSM_RELAYOUT_UNROLL)
        start(column_copies(step_id, slot, False, slices=(k,)))

    @pl.when(step_id == last_step)
    def _():
        @pl.when(step_id >= 1)
        def _():
            wait(column_copies(step_id - 1, 1 - slot, False))
        wait(column_copies(step_id, slot, False))


def _padded_seq(seq):
    return seq + SUBLANES if seq % (2 * SUBLANES) == 0 else seq


def _ssm(u, lam_re, lam_im, log_dt, b_re, b_im, c_re, c_im, d_skip):
    batch, seq, _ = u.shape
    pitch = _padded_seq(seq)
    rows = batch * seq // SSM_CHUNK
    gpb = GROUPS_PER_BLOCK
    f32 = lambda a: a.astype(F32)
    channel_major = lambda b: f32(b).transpose(0, 1, 3, 2)
    params = [f32(lam_re), f32(lam_im), f32(d_skip), channel_major(b_re), channel_major(b_im),
              f32(c_re), f32(c_im)]
    whole = lambda a: pl.BlockSpec(a.shape, lambda s: (0,) * a.ndim)
    hbm = pl.BlockSpec(memory_space=pl.ANY)
    ops_scratch = pltpu.VMEM((2, gpb, CHUNK_WIDTH, CHUNK_WIDTH), BF16)
    return pl.pallas_call(
        functools.partial(_ssm_kernel, batch=batch, seq=seq, pitch=pitch),
        grid=(SSM_GROUPS // gpb,),
        in_specs=[hbm, pl.BlockSpec(memory_space=pltpu.SMEM)] + [whole(a) for a in params],
        out_specs=hbm,
        out_shape=jax.ShapeDtypeStruct((batch, seq, SSM_WIDTH), F32),
        scratch_shapes=[pltpu.VMEM((2, batch * pitch, LANES), F32),
                        pltpu.VMEM((gpb, rows, CHUNK_WIDTH), BF16),
                        pltpu.VMEM((SSM_INTERLEAVE, rows, 4 * SSM_STATE), F32),
                        pltpu.VMEM((SSM_INTERLEAVE, rows, 4 * SSM_STATE), F32),
                        ops_scratch, ops_scratch, ops_scratch,
                        pltpu.VMEM((2, gpb, SUBLANES, 2 * SSM_STATE), F32),
                        pltpu.SemaphoreType.DMA((2, SSM_DMA_SLICES, batch)),
                        pltpu.SemaphoreType.DMA((2, SSM_DMA_SLICES, batch))],
        compiler_params=pltpu.CompilerParams(
            dimension_semantics=("arbitrary",), vmem_limit_bytes=VMEM_LIMIT_BYTES),
    )(u, f32(log_dt), *params)


def kernel(x, norm_ffn1, ffn1_w_gate, ffn1_w_up, ffn1_w_down, norm_mix, w_in, attn_sinks,
           ssm_lambda_re, ssm_lambda_im, ssm_log_dt, ssm_b_re, ssm_b_im, ssm_c_re, ssm_c_im,
           ssm_d, ssm_glu_w, ssm_glu_b, attn_out_norm, ssm_out_norm, w_out,
           norm_ffn2, ffn2_w_gate, ffn2_w_up, ffn2_w_down, final_norm):
    b, seq, d = x.shape
    depth = norm_ffn1.shape[0]
    assert d == D_MODEL and seq % BLOCK == 0 and (b * seq) % FFN_TOKEN_TILE == 0
    assert b == SUBLANES and seq % (2 * SSM_CHUNK * SSM_RELAYOUT_UNROLL * SSM_DMA_SLICES) == 0
    h = x.reshape(b * seq, d).astype(F32)
    for l in range(depth):
        h, q, k, v, u = _ffn(h, norm_ffn1[l], ffn1_w_gate[l], ffn1_w_up[l], ffn1_w_down[l],
                             final_norm, final_norm=False, proj=(norm_mix[l], w_in[l]))
        attn = _attention(q.reshape(b, seq, -1), k.reshape(b, seq, -1), v.reshape(b, seq, -1),
                          attn_sinks[l])
        ssm_pre = _ssm(u.reshape(b, seq, -1), ssm_lambda_re[l], ssm_lambda_im[l], ssm_log_dt[l],
                       ssm_b_re[l], ssm_b_im[l], ssm_c_re[l], ssm_c_im[l], ssm_d[l])
        mix = (attn.reshape(b * seq, -1), ssm_pre.reshape(b * seq, -1), ssm_glu_w[l],
               ssm_glu_b[l], attn_out_norm[l], ssm_out_norm[l], w_out[l])
        h, = _ffn(h, norm_ffn2[l], ffn2_w_gate[l], ffn2_w_up[l], ffn2_w_down[l],
                  final_norm, final_norm=(l == depth - 1), mix=mix)
    return h.reshape(b, seq, d).astype(x.dtype)
```

```python
import functools
import math

import jax
import jax.numpy as jnp
from jax import lax
from jax.experimental import pallas as pl
from jax.experimental.pallas import tpu as pltpu

F32 = jnp.float32
BF16 = jnp.bfloat16

D_MODEL = 1024
ATTN_HEADS = 8
ATTN_KV_HEADS = 2
Q_PER_KV = ATTN_HEADS // ATTN_KV_HEADS
HEAD_DIM = 64
ATTN_WIDTH = ATTN_HEADS * HEAD_DIM
KV_WIDTH = ATTN_KV_HEADS * HEAD_DIM
WINDOW = 128
BLOCK = 128
SSM_CH = 16
SSM_WIDTH = D_MODEL - ATTN_WIDTH
SSM_GROUPS = SSM_WIDTH // SSM_CH
SSM_STATE = 64
IN_WIDTH = ATTN_WIDTH + 2 * KV_WIDTH + SSM_WIDTH
D_FF = 2816
EPS = 1e-6
NEG_INF = -1e30
LAMBDA_RE_MAX = -1e-4
LOG2_E = math.log2(math.e)
QUERY_SCALE = HEAD_DIM ** -0.5 * LOG2_E

LANES = 128
SUBLANES = 8
VMEM_LIMIT_BYTES = 56 * 1024 * 1024

FFN_TOKEN_TILE = 512
FFN_SUBTILES = 2
FF_TILE = 256
ATTN_BLOCKS_PER_STEP = 16
ATTN_HEADS_PER_PV = 2
SSM_CHUNK = 16
CHUNK_WIDTH = SSM_CHUNK * SSM_CH
GROUPS_PER_BLOCK = LANES // SSM_CH
SSM_INTERLEAVE = 8
SSM_RELAYOUT_UNROLL = 4
SSM_DMA_SLICES = 8
OPS_GROUPS_PER_STEP = 8

NT_DIMS = (((1,), (1,)), ((), ()))
TN_DIMS = (((0,), (0,)), ((), ()))


def _rms(x):
    return x * lax.rsqrt(jnp.mean(x * x, axis=-1, keepdims=True) + EPS)


def _mixed_update(rows, attn_ref, ssm_ref, gw_ref, gb_ref, ga_ref, gs_ref, wo_ref):
    y = jax.nn.gelu(ssm_ref[rows, :])
    z = jnp.dot(y.astype(BF16), gw_ref[...].astype(BF16), preferred_element_type=F32) + gb_ref[...]
    s = y * jax.nn.sigmoid(z)
    sn = _rms(s) * gs_ref[...]
    an = _rms(attn_ref[rows, :].astype(F32)) * ga_ref[...]
    mixed = jnp.concatenate([an, sn], axis=-1).astype(BF16)
    return jnp.dot(mixed, wo_ref[...].astype(BF16), preferred_element_type=F32)


def _mixer_inputs(y, rows, gain_ref, w_ref, q_ref, k_ref, v_ref, u_ref):
    hn = (_rms(y) * gain_ref[...]).astype(BF16)
    proj = jnp.dot(hn, w_ref[...].astype(BF16), preferred_element_type=F32)
    q_ref[rows, :] = (proj[:, :ATTN_WIDTH] * QUERY_SCALE).astype(BF16)
    k_ref[rows, :] = proj[:, ATTN_WIDTH:ATTN_WIDTH + KV_WIDTH].astype(BF16)
    v_ref[rows, :] = proj[:, ATTN_WIDTH + KV_WIDTH:ATTN_WIDTH + 2 * KV_WIDTH].astype(BF16)
    u_ref[rows, :] = proj[:, ATTN_WIDTH + 2 * KV_WIDTH:]


def _ffn_body(read_x, gain_ref, fgain_ref, wg_hbm, wu_hbm, wd_hbm, o_ref,
              wg_ref, wu_ref, wd_ref, act_ref, gu_stage, d_stage, sem, *, final_norm,
              epilogue=None):
    nj = D_FF // FF_TILE

    def weight_copies(j, slot):
        span = pl.ds(j * FF_TILE, FF_TILE)
        return (pltpu.make_async_copy(wg_hbm.at[:, span], gu_stage.at[0, slot], sem.at[0, slot]),
                pltpu.make_async_copy(wu_hbm.at[:, span], gu_stage.at[1, slot], sem.at[1, slot]),
                pltpu.make_async_copy(wd_hbm.at[span, :], d_stage.at[slot], sem.at[2, slot]))

    def step(stage_weights):
        if stage_weights:
            for copy in weight_copies(0, 0):
                copy.start()
        sub = o_ref.shape[0] // FFN_SUBTILES
        row_groups = [slice(h * sub, (h + 1) * sub) for h in range(FFN_SUBTILES)]
        xs = [read_x(rows) for rows in row_groups]
        hns = [(_rms(x) * gain_ref[...]).astype(BF16) for x in xs]
        for j in range(nj):
            cols = slice(j * FF_TILE, (j + 1) * FF_TILE)
            if stage_weights:
                slot = j % 2
                if j + 1 < nj:
                    for copy in weight_copies(j + 1, 1 - slot):
                        copy.start()
                for copy in weight_copies(j, slot):
                    copy.wait()
                wg_ref[:, cols] = gu_stage[0, slot].astype(BF16)
                wu_ref[:, cols] = gu_stage[1, slot].astype(BF16)
                wd_ref[cols, :] = d_stage[slot].astype(BF16)
            for rows, hn in zip(row_groups, hns):
                g = jnp.dot(hn, wg_ref[:, cols], preferred_element_type=F32)
                u = jnp.dot(hn, wu_ref[:, cols], preferred_element_type=F32)
                act_ref[rows, cols] = (g * jax.nn.sigmoid(g) * u).astype(BF16)
        for rows, x in zip(row_groups, xs):
            y = x + 0.5 * jnp.dot(act_ref[rows, :], wd_ref[...], preferred_element_type=F32)
            if final_norm:
                y = _rms(y) * fgain_ref[...]
            o_ref[rows, :] = y
            if epilogue is not None:
                epilogue(y, rows)

    pl.when(pl.program_id(0) == 0)(functools.partial(step, True))
    pl.when(pl.program_id(0) != 0)(functools.partial(step, False))


def _ffn_proj_kernel(x_ref, mgain_ref, win_ref, gain_ref, fgain_ref, wg_hbm, wu_hbm, wd_hbm,
                     o_ref, q_ref, k_ref, v_ref, u_ref, *scratch, final_norm):
    emit = lambda y, rows: _mixer_inputs(y, rows, mgain_ref, win_ref, q_ref, k_ref, v_ref, u_ref)
    _ffn_body(lambda rows: x_ref[rows, :], gain_ref, fgain_ref, wg_hbm, wu_hbm, wd_hbm, o_ref,
              *scratch, final_norm=final_norm, epilogue=emit)


def _mix_ffn_kernel(x_ref, attn_ref, ssm_ref, gw_ref, gb_ref, ga_ref, gs_ref, wo_ref, *ffn_refs,
                    final_norm):
    read_x = lambda rows: x_ref[rows, :] + _mixed_update(rows, attn_ref, ssm_ref, gw_ref, gb_ref,
                                                         ga_ref, gs_ref, wo_ref)
    _ffn_body(read_x, *ffn_refs, final_norm=final_norm)


def _ffn(x, gain, w_gate, w_up, w_down, final_gain, final_norm, mix=None, proj=None):
    assert (mix is None) != (proj is None)
    t = x.shape[0]
    tile = FFN_TOKEN_TILE
    vec = lambda width: pl.BlockSpec((1, width), lambda i: (0, 0))
    resident = lambda r, c: pl.BlockSpec((r, c), lambda i: (0, 0), pipeline_mode=pl.Buffered(1))
    row = lambda width: pl.BlockSpec((tile, width), lambda i: (i, 0))
    hbm = pl.BlockSpec(memory_space=pl.ANY)
    in_specs, args = [row(D_MODEL)], [x]
    out_specs, out_shape = [row(D_MODEL)], [jax.ShapeDtypeStruct((t, D_MODEL), F32)]
    if mix is not None:
        attn, ssm_pre, glu_w, glu_b, attn_gain, ssm_gain, w_out = mix
        body = _mix_ffn_kernel
        in_specs += [row(ATTN_WIDTH), row(SSM_WIDTH),
                     resident(SSM_WIDTH, SSM_WIDTH), vec(SSM_WIDTH), vec(ATTN_WIDTH),
                     vec(SSM_WIDTH), resident(D_MODEL, D_MODEL)]
        args += [attn, ssm_pre, glu_w, glu_b.reshape(1, -1).astype(F32),
                 attn_gain.reshape(1, -1).astype(F32), ssm_gain.reshape(1, -1).astype(F32), w_out]
    if proj is not None:
        mixer_gain, w_in = proj
        body = _ffn_proj_kernel
        in_specs += [vec(D_MODEL), resident(D_MODEL, IN_WIDTH)]
        args += [mixer_gain.reshape(1, D_MODEL), w_in]
        for width, dtype in ((ATTN_WIDTH, BF16), (KV_WIDTH, BF16), (KV_WIDTH, BF16),
                             (SSM_WIDTH, F32)):
            out_specs.append(row(width))
            out_shape.append(jax.ShapeDtypeStruct((t, width), dtype))
    in_specs += [vec(D_MODEL), vec(D_MODEL), hbm, hbm, hbm]
    args += [gain.reshape(1, D_MODEL), final_gain.reshape(1, D_MODEL), w_gate, w_up, w_down]
    return pl.pallas_call(
        functools.partial(body, final_norm=final_norm),
        grid=(t // tile,),
        in_specs=in_specs,
        out_specs=out_specs,
        out_shape=out_shape,
        scratch_shapes=[pltpu.VMEM((D_MODEL, D_FF), BF16),
                        pltpu.VMEM((D_MODEL, D_FF), BF16),
                        pltpu.VMEM((D_FF, D_MODEL), BF16),
                        pltpu.VMEM((tile, D_FF), BF16),
                        pltpu.VMEM((2, 2, D_MODEL, FF_TILE), F32),
                        pltpu.VMEM((2, FF_TILE, D_MODEL), F32),
                        pltpu.SemaphoreType.DMA((3, 2))],
        compiler_params=pltpu.CompilerParams(
            dimension_semantics=("arbitrary",), vmem_limit_bytes=VMEM_LIMIT_BYTES),
    )(*args)


def _attn_kernel(sink_ref, q_ref, *refs, edges):
    if edges:
        kp_ref, kc_ref, kn_ref, vp_ref, vc_ref, vn_ref, o_ref, bias_ref, k_ref, v_ref, s_ref = refs
    else:
        kc_ref, vc_ref, o_ref, bias_ref, k_ref, v_ref, s_ref = refs
    n = pl.program_id(1)
    last = pl.num_programs(1) - 1
    nblk = ATTN_BLOCKS_PER_STEP

    @pl.when((pl.program_id(0) == 0) & (n == 0))
    def _():
        kj = lax.broadcasted_iota(jnp.int32, (3 * BLOCK, BLOCK), 0)
        qi = lax.broadcasted_iota(jnp.int32, (3 * BLOCK, BLOCK), 1)
        rel = jnp.abs(kj - BLOCK - qi)
        dist = rel.astype(F32)
        inside = rel <= WINDOW
        has_prev = kj >= BLOCK
        has_next = kj < 2 * BLOCK
        for variant, ok in enumerate((inside & has_prev, inside, inside & has_next)):
            for h in range(ATTN_HEADS):
                slope = float(2.0 ** (-8.0 * (h + 1) / ATTN_HEADS))
                bias_ref[variant, h] = jnp.where(ok, (-slope * LOG2_E) * dist, NEG_INF)
        if not edges:
            k_ref[...] = jnp.zeros(k_ref.shape, BF16)
            v_ref[...] = jnp.zeros(v_ref.shape, BF16)
        for kh in range(ATTN_KV_HEADS):
            v_ref[:, (2 * kh + 1) * HEAD_DIM:(2 * kh + 2) * HEAD_DIM] = jnp.ones(
                (v_ref.shape[0], HEAD_DIM), BF16)

    spans = [(slice(BLOCK, (nblk + 1) * BLOCK), kc_ref, vc_ref)]
    if edges:
        spans += [(slice(0, BLOCK), kp_ref, vp_ref),
                  (slice((nblk + 1) * BLOCK, (nblk + 2) * BLOCK), kn_ref, vn_ref)]
    for rows, k_in, v_in in spans:
        k_ref[rows, :] = k_in[...]
        for kh in range(ATTN_KV_HEADS):
            v_ref[rows, 2 * kh * HEAD_DIM:(2 * kh + 1) * HEAD_DIM] = (
                v_in[:, kh * HEAD_DIM:(kh + 1) * HEAD_DIM])

    def scores(j, kh):
        kcat = k_ref[j * BLOCK:(j + 3) * BLOCK, kh * HEAD_DIM:(kh + 1) * HEAD_DIM]
        heads = [kh * Q_PER_KV + g for g in range(Q_PER_KV)]
        qs = jnp.concatenate(
            [q_ref[j * BLOCK:(j + 1) * BLOCK, h * HEAD_DIM:(h + 1) * HEAD_DIM] for h in heads],
            axis=0)
        s_ref[j, kh] = lax.dot_general(kcat, qs, NT_DIMS,
                                       preferred_element_type=F32)

    for kh in range(ATTN_KV_HEADS):
        scores(0, kh)
    for j in range(nblk):
        variant = 1
        if j == 0:
            variant = jnp.where(n == 0, 0, variant)
        if j == nblk - 1:
            variant = jnp.where(n == last, 2, variant)
        outs = []
        for kh in range(ATTN_KV_HEADS):
            if j + 1 < nblk:
                scores(j + 1, kh)
            v_ones = v_ref[j * BLOCK:(j + 3) * BLOCK, 2 * kh * HEAD_DIM:(2 * kh + 2) * HEAD_DIM]
            probs, sink_terms = [], []
            for g in range(Q_PER_KV):
                h = kh * Q_PER_KV + g
                s = s_ref[j, kh, :, g * BLOCK:(g + 1) * BLOCK] + bias_ref[variant, h]
                sink = sink_ref[h] * LOG2_E
                m = jnp.maximum(jnp.max(s, axis=0, keepdims=True), sink)
                probs.append(jnp.exp2(s - m).astype(BF16))
                sink_terms.append(jnp.exp2(sink - m))
            hp = ATTN_HEADS_PER_PV
            for g in range(0, Q_PER_KV, hp):
                pv = lax.dot_general(v_ones, jnp.concatenate(probs[g:g + hp], axis=1), TN_DIMS,
                                     preferred_element_type=F32)
                for t in range(hp):
                    cols = slice(t * BLOCK, (t + 1) * BLOCK)
                    den = pv[HEAD_DIM:HEAD_DIM + 1, cols] + sink_terms[g + t]
                    outs.append(pv[:HEAD_DIM, cols] / den)
        o_ref[j * BLOCK:(j + 1) * BLOCK, :] = jnp.concatenate(outs, axis=0).T.astype(o_ref.dtype)


def _attention(q, k, v, sinks):
    b, seq, _ = q.shape
    nblk = ATTN_BLOCKS_PER_STEP
    steps = seq // (nblk * BLOCK)
    nb = seq // BLOCK
    assert seq % (nblk * BLOCK) == 0 and nb >= 2
    edge = lambda f: pl.BlockSpec((None, BLOCK, KV_WIDTH), f)
    body = pl.BlockSpec((None, nblk * BLOCK, KV_WIDTH), lambda bi, n: (bi, n, 0))
    prev = lambda bi, n: (bi, jnp.maximum(n * nblk - 1, 0), 0)
    nxt = lambda bi, n: (bi, jnp.minimum((n + 1) * nblk, nb - 1), 0)
    rows = pl.BlockSpec((None, nblk * BLOCK, ATTN_WIDTH), lambda bi, n: (bi, n, 0))
    edges = steps > 1
    kv_specs = [edge(prev), body, edge(nxt)] if edges else [body]
    kv_args = lambda a: (a, a, a) if edges else (a,)
    return pl.pallas_call(
        functools.partial(_attn_kernel, edges=edges),
        grid=(b, steps),
        in_specs=[pl.BlockSpec(memory_space=pltpu.SMEM), rows] + kv_specs + kv_specs,
        out_specs=rows,
        out_shape=jax.ShapeDtypeStruct((b, seq, ATTN_WIDTH), BF16),
        scratch_shapes=[pltpu.VMEM((3, ATTN_HEADS, 3 * BLOCK, BLOCK), F32),
                        pltpu.VMEM(((nblk + 2) * BLOCK, KV_WIDTH), BF16),
                        pltpu.VMEM(((nblk + 2) * BLOCK, 2 * KV_WIDTH), BF16),
                        pltpu.VMEM((nblk, ATTN_KV_HEADS, 3 * BLOCK, Q_PER_KV * BLOCK), F32)],
        compiler_params=pltpu.CompilerParams(
            dimension_semantics=("arbitrary", "arbitrary"), vmem_limit_bytes=VMEM_LIMIT_BYTES),
    )(sinks.astype(F32), q, *kv_args(k), *kv_args(v))


def _dot_nt_split(a, b):
    a_hi, b_hi = a.astype(BF16), b.astype(BF16)
    a_lo = (a - a_hi.astype(F32)).astype(BF16)
    b_lo = (b - b_hi.astype(F32)).astype(BF16)
    dot = functools.partial(lax.dot_general, dimension_numbers=NT_DIMS,
                            preferred_element_type=F32)
    return dot(a_hi, b_hi) + dot(a_hi, b_lo) + dot(a_lo, b_hi)


def _ssm_ops_kernel(ldt_ref, lre_ref, lim_ref, d_ref, btr_ref, bti_ref, cre_ref, cim_ref,
                    ein_ref, toep_ref, eout_ref, aq_ref):
    q, hc, p = SSM_CHUNK, SSM_CH, SSM_STATE
    gs = lre_ref.shape[1]
    g_base = pl.program_id(0) * gs
    both_dirs = lambda ref, gi: jnp.concatenate([ref[0, gi], ref[1, gi]], axis=1)
    fwd = lax.broadcasted_iota(jnp.int32, (1, 2 * p), 1) < p
    zero_row = jnp.zeros((1, 2 * p), F32)
    row_id = lax.broadcasted_iota(jnp.int32, (CHUNK_WIDTH, CHUNK_WIDTH), 0)
    col_id = lax.broadcasted_iota(jnp.int32, (CHUNK_WIDTH, CHUNK_WIDTH), 1)

    def table(select, n):
        picks = [select(m) for m in range(n)]
        re = jnp.concatenate([jnp.broadcast_to(r, (hc, 2 * p)) for r, _ in picks], axis=0)
        im = jnp.concatenate([jnp.broadcast_to(i, (hc, 2 * p)) for _, i in picks], axis=0)
        return re, im

    def tile_rows(x, n):
        return jnp.concatenate([x] * n, axis=0)

    for gi in range(gs):
        lr = jnp.minimum(jnp.concatenate([lre_ref[0, gi:gi + 1, :], lre_ref[1, gi:gi + 1, :]],
                                         axis=1), LAMBDA_RE_MAX)
        li = jnp.concatenate([lim_ref[0, gi:gi + 1, :], lim_ref[1, gi:gi + 1, :]], axis=1)
        dt = jnp.exp(jnp.where(fwd, ldt_ref[0, g_base + gi], ldt_ref[1, g_base + gi]))
        mag = jnp.exp(lr * dt)
        a_r = mag * jnp.cos(li * dt)
        a_i = mag * jnp.sin(li * dt)
        den = lr * lr + li * li
        coef_r = ((a_r - 1.0) * lr + a_i * li) / den
        coef_i = (a_i * lr - (a_r - 1.0) * li) / den
        b_r, b_i = both_dirs(btr_ref, gi), both_dirs(bti_ref, gi)
        c_r, c_i = both_dirs(cre_ref, gi), both_dirs(cim_ref, gi)
        bb_r = coef_r * b_r - coef_i * b_i
        bb_i = coef_r * b_i + coef_i * b_r

        pw = [(jnp.ones((1, 2 * p), F32), zero_row)]
        for _ in range(q):
            r, i = pw[-1]
            pw.append((r * a_r - i * a_i, r * a_i + i * a_r))

        def both(f_idx, b_idx):
            fr, fi = pw[f_idx] if f_idx is not None else (zero_row, zero_row)
            br, bi = pw[b_idx] if b_idx is not None else (zero_row, zero_row)
            return jnp.where(fwd, fr, br), jnp.where(fwd, fi, bi)

        p_r, p_i = table(lambda i: both(q - 1 - i, i), q)
        tb_r, tb_i = tile_rows(bb_r, q), tile_rows(bb_i, q)
        ein = jnp.concatenate([tb_r * p_r - tb_i * p_i, tb_r * p_i + tb_i * p_r], axis=1)
        ein_ref[gi] = ein.astype(BF16)

        p_r, p_i = table(lambda j: both(j + 1, q - j), q)
        tc_r, tc_i = tile_rows(c_r, q), tile_rows(c_i, q)
        eout = jnp.concatenate([tc_r * p_r - tc_i * p_i, -(tc_r * p_i + tc_i * p_r)], axis=1)
        eout_ref[gi] = eout.astype(BF16)

        def lag(m):
            return both(m - (q - 1) if q - 1 <= m <= 2 * q - 2 else None,
                        (q - 1) - m if m <= q - 1 else None)

        p_r, p_i = table(lag, 2 * q)
        tc_r, tc_i = tile_rows(c_r, 2 * q), tile_rows(c_i, 2 * q)
        cpt = jnp.concatenate([tc_r * p_r - tc_i * p_i, tc_r * p_i + tc_i * p_r], axis=1)
        bcat = jnp.concatenate([bb_r, -bb_i], axis=1)
        kern = _dot_nt_split(bcat, cpt)
        toep = jnp.concatenate(
            [kern[:, hc * (q - 1 - i):hc * (q - 1 - i) + CHUNK_WIDTH] for i in range(q)], axis=0)
        skip = jnp.concatenate([d_ref[gi:gi + 1, :]] * q, axis=1)
        toep_ref[gi] = (toep + jnp.where(row_id == col_id, skip, 0.0)).astype(BF16)

        aq_ref[gi] = jnp.concatenate(
            [pw[q][0], pw[q][1], jnp.zeros((SUBLANES - 2, 2 * p), F32)], axis=0)


def _ssm_operators(lam_re, lam_im, log_dt, b_re, b_im, c_re, c_im, d_skip):
    g, p, hc = SSM_GROUPS, SSM_STATE, SSM_CH
    f32 = lambda a: a.astype(F32)
    channel_major = lambda b: f32(b).transpose(0, 1, 3, 2)
    gs = OPS_GROUPS_PER_STEP
    mat = pl.BlockSpec((gs, CHUNK_WIDTH, CHUNK_WIDTH), lambda s: (s, 0, 0))
    mat_shape = jax.ShapeDtypeStruct((g, CHUNK_WIDTH, CHUNK_WIDTH), BF16)
    per_dir = lambda *tail: pl.BlockSpec((2, gs) + tail, lambda s: (0, s) + (0,) * len(tail))
    return pl.pallas_call(
        _ssm_ops_kernel,
        grid=(g // gs,),
        in_specs=[pl.BlockSpec(memory_space=pltpu.SMEM), per_dir(p), per_dir(p),
                  pl.BlockSpec((gs, hc), lambda s: (s, 0)),
                  per_dir(hc, p), per_dir(hc, p), per_dir(hc, p), per_dir(hc, p)],
        out_specs=[mat, mat, mat, pl.BlockSpec((gs, SUBLANES, 2 * p), lambda s: (s, 0, 0))],
        out_shape=[mat_shape, mat_shape, mat_shape,
                   jax.ShapeDtypeStruct((g, SUBLANES, 2 * p), F32)],
        compiler_params=pltpu.CompilerParams(
            dimension_semantics=("parallel",), vmem_limit_bytes=VMEM_LIMIT_BYTES),
    )(f32(log_dt), f32(lam_re), f32(lam_im), f32(d_skip), channel_major(b_re), channel_major(b_im),
      f32(c_re), f32(c_im))


def _lane_roll(x, shift):
    shift %= LANES
    return jnp.concatenate([x[:, LANES - shift:], x[:, :LANES - shift]], axis=1)


def _block_transpose(v, lane_block):
    return _block_transposes([v], lane_block)[0]


def _block_transposes(groups, lane_block):
    n = len(groups[0])
    skewed = [[v[i] if i == 0 else _lane_roll(v[i], SSM_CH * i) for i in range(n)] for v in groups]
    picked = []
    for sk in skewed:
        rows = []
        for g in range(n):
            p = sk[(0 - g) % n]
            for j in range(1, n):
                p = jnp.where(lane_block == j, sk[(j - g) % n], p)
            rows.append(p)
        picked.append(rows)
    return [[p[g] if g == 0 else _lane_roll(p[g], -SSM_CH * g) for g in range(n)] for p in picked]


def _ssm_kernel(u_hbm, ein_ref, toep_ref, eout_ref, aq_ref, y_hbm,
                io_ref, ug_ref, s_ref, x_ref, in_sem, out_sem, *, batch, seq, pitch):
    q, half = SSM_CHUNK, SSM_STATE
    n_chunks = seq // q
    gpb = GROUPS_PER_BLOCK
    slab = 2 * batch
    lane_block = lax.broadcasted_iota(jnp.int32, (slab, LANES), 1) // SSM_CH
    step_id = pl.program_id(0)
    last_step = pl.num_programs(0) - 1
    slot = lax.rem(step_id, 2)

    n_slices = SSM_DMA_SLICES
    slice_rows = seq // n_slices
    slice_trips = n_chunks // (2 * n_slices)

    def column_copies(block, slot_, to_vmem, slices=range(SSM_DMA_SLICES)):
        lanes = pl.ds(pl.multiple_of(block * LANES, LANES), LANES)
        copies = []
        for k in slices:
            for b in range(batch):
                hbm = (u_hbm if to_vmem else y_hbm).at[b, pl.ds(k * slice_rows, slice_rows), lanes]
                vmem = io_ref.at[slot_, pl.ds(b * pitch + k * slice_rows, slice_rows), :]
                if to_vmem:
                    copies.append(pltpu.make_async_copy(hbm, vmem, in_sem.at[slot_, k, b]))
                else:
                    copies.append(pltpu.make_async_copy(vmem, hbm, out_sem.at[slot_, k, b]))
        return copies

    def start(copies):
        for copy in copies:
            copy.start()

    def wait(copies):
        for copy in copies:
            copy.wait()

    @pl.when(step_id == 0)
    def _():
        start(column_copies(0, 0, True))

    @pl.when(step_id < last_step)
    def _():
        @pl.when(step_id >= 1)
        def _():
            wait(column_copies(step_id - 1, 1 - slot, False))
        start(column_copies(step_id + 1, 1 - slot, True))

    buf = io_ref.at[slot]

    def gather(cp, carry):
        rows = pl.ds(pl.multiple_of(cp * slab, slab), slab)
        parts = []
        for part in range(q // gpb):
            t0 = cp * 2 * q + part * gpb
            parts.append([jnp.concatenate([buf[pl.ds(t0 + i, batch, stride=pitch), :],
                                           buf[pl.ds(t0 + q + i, batch, stride=pitch), :]],
                                          axis=0).astype(BF16) for i in range(gpb)])
        for part, w in enumerate(_block_transposes(parts, lane_block)):
            for g in range(gpb):
                ug_ref[g, rows, part * LANES:(part + 1) * LANES] = w[g]
        return carry

    for k in range(n_slices):
        wait(column_copies(step_id, slot, True, slices=(k,)))
        lax.fori_loop(k * slice_trips, (k + 1) * slice_trips, gather, 0,
                      unroll=SSM_RELAYOUT_UNROLL)

    fwd_lane = lax.broadcasted_iota(jnp.int32, (batch, 2 * half), 1) < half
    zeros = jnp.zeros((batch, half), F32)
    last_rows = pl.ds((n_chunks - 1) * batch, batch)
    ni = SSM_INTERLEAVE
    for g0 in range(0, gpb, ni):
        for gi in range(ni):
            s_ref[gi] = jnp.dot(ug_ref[g0 + gi], ein_ref[g0 + gi], preferred_element_type=F32)
            x_ref[gi, 0:batch, 0:half] = zeros
            x_ref[gi, 0:batch, 2 * half:3 * half] = zeros
            x_ref[gi, last_rows, half:2 * half] = zeros
            x_ref[gi, last_rows, 3 * half:4 * half] = zeros

        def step(k, carry):
            rf = pl.multiple_of(k * batch, batch)
            rb = pl.multiple_of((n_chunks - 1 - k) * batch, batch)
            new = []
            for gi in range(ni):
                xr, xi = carry[gi]
                sre = jnp.where(fwd_lane, s_ref[gi, pl.ds(rf, batch), 0:2 * half],
                                s_ref[gi, pl.ds(rb, batch), 0:2 * half])
                sim = jnp.where(fwd_lane, s_ref[gi, pl.ds(rf, batch), 2 * half:4 * half],
                                s_ref[gi, pl.ds(rb, batch), 2 * half:4 * half])
                ar = aq_ref[g0 + gi, 0:1, :]
                ai = aq_ref[g0 + gi, 1:2, :]
                nr = ar * xr - ai * xi + sre
                nim = ar * xi + ai * xr + sim
                x_ref[gi, pl.ds(rf + batch, batch), 0:half] = nr[:, 0:half]
                x_ref[gi, pl.ds(rf + batch, batch), 2 * half:3 * half] = nim[:, 0:half]
                x_ref[gi, pl.ds(rb - batch, batch), half:2 * half] = nr[:, half:2 * half]
                x_ref[gi, pl.ds(rb - batch, batch), 3 * half:4 * half] = nim[:, half:2 * half]
                new.append((nr, nim))
            return tuple(new)

        init = tuple((jnp.zeros((batch, 2 * half), F32), jnp.zeros((batch, 2 * half), F32))
                     for _ in range(ni))
        lax.fori_loop(0, n_chunks - 1, step, init, unroll=True)

        for gi in range(ni):
            g = g0 + gi
            y = jnp.dot(ug_ref[g], toep_ref[g], preferred_element_type=F32)
            y += lax.dot_general(x_ref[gi].astype(BF16), eout_ref[g], NT_DIMS,
                                 preferred_element_type=F32)
            ug_ref[g] = y.astype(BF16)

    def scatter(cp, carry):
        rows = pl.ds(pl.multiple_of(cp * slab, slab), slab)
        parts = [[ug_ref[g, rows, part * LANES:(part + 1) * LANES] for g in range(gpb)]
                 for part in range(q // gpb)]
        for part, w in enumerate(_block_transposes(parts, lane_block)):
            t0 = cp * 2 * q + part * gpb
            for j in range(gpb):
                wj = w[j].astype(F32)
                buf[pl.ds(t0 + j, batch, stride=pitch), :] = wj[0:batch]
                buf[pl.ds(t0 + q + j, batch, stride=pitch), :] = wj[batch:slab]
        return carry

    for k in range(n_slices):
        lax.fori_loop(k * slice_trips, (k + 1) * slice_trips, scatter, 0,
                      unroll=SSM_RELAYOUT_UNROLL)
        start(column_copies(step_id, slot, False, slices=(k,)))

    @pl.when(step_id == last_step)
    def _():
        @pl.when(step_id >= 1)
        def _():
            wait(column_copies(step_id - 1, 1 - slot, False))
        wait(column_copies(step_id, slot, False))


def _padded_seq(seq):
    return seq + SUBLANES if seq % (2 * SUBLANES) == 0 else seq


def _ssm(u, lam_re, lam_im, log_dt, b_re, b_im, c_re, c_im, d_skip):
    batch, seq, _ = u.shape
    pitch = _padded_seq(seq)
    rows = batch * seq // SSM_CHUNK
    ein, toep, eout, a_q = _ssm_operators(lam_re, lam_im, log_dt, b_re, b_im, c_re, c_im, d_skip)
    gpb = GROUPS_PER_BLOCK
    mat = pl.BlockSpec((gpb, CHUNK_WIDTH, CHUNK_WIDTH), lambda s: (s, 0, 0))
    hbm = pl.BlockSpec(memory_space=pl.ANY)
    return pl.pallas_call(
        functools.partial(_ssm_kernel, batch=batch, seq=seq, pitch=pitch),
        grid=(SSM_GROUPS // gpb,),
        in_specs=[hbm, mat, mat, mat,
                  pl.BlockSpec((gpb, SUBLANES, 2 * SSM_STATE), lambda s: (s, 0, 0))],
        out_specs=hbm,
        out_shape=jax.ShapeDtypeStruct((batch, seq, SSM_WIDTH), F32),
        scratch_shapes=[pltpu.VMEM((2, batch * pitch, LANES), F32),
                        pltpu.VMEM((gpb, rows, CHUNK_WIDTH), BF16),
                        pltpu.VMEM((SSM_INTERLEAVE, rows, 4 * SSM_STATE), F32),
                        pltpu.VMEM((SSM_INTERLEAVE, rows, 4 * SSM_STATE), F32),
                        pltpu.SemaphoreType.DMA((2, SSM_DMA_SLICES, batch)),
                        pltpu.SemaphoreType.DMA((2, SSM_DMA_SLICES, batch))],
        compiler_params=pltpu.CompilerParams(
            dimension_semantics=("arbitrary",), vmem_limit_bytes=VMEM_LIMIT_BYTES),
    )(u, ein, toep, eout, a_q)


def kernel(x, norm_ffn1, ffn1_w_gate, ffn1_w_up, ffn1_w_down, norm_mix, w_in, attn_sinks,
           ssm_lambda_re, ssm_lambda_im, ssm_log_dt, ssm_b_re, ssm_b_im, ssm_c_re, ssm_c_im,
           ssm_d, ssm_glu_w, ssm_glu_b, attn_out_norm, ssm_out_norm, w_out,
           norm_ffn2, ffn2_w_gate, ffn2_w_up, ffn2_w_down, final_norm):
    b, seq, d = x.shape
    depth = norm_ffn1.shape[0]
    assert d == D_MODEL and seq % BLOCK == 0 and (b * seq) % FFN_TOKEN_TILE == 0
    assert b == SUBLANES and seq % (2 * SSM_CHUNK * SSM_RELAYOUT_UNROLL * SSM_DMA_SLICES) == 0
    h = x.reshape(b * seq, d).astype(F32)
    for l in range(depth):
        h, q, k, v, u = _ffn(h, norm_ffn1[l], ffn1_w_gate[l], ffn1_w_up[l], ffn1_w_down[l],
                             final_norm, final_norm=False, proj=(norm_mix[l], w_in[l]))
        attn = _attention(q.reshape(b, seq, -1), k.reshape(b, seq, -1), v.reshape(b, seq, -1),
                          attn_sinks[l])
        ssm_pre = _ssm(u.reshape(b, seq, -1), ssm_lambda_re[l], ssm_lambda_im[l], ssm_log_dt[l],
                       ssm_b_re[l], ssm_b_im[l], ssm_c_re[l], ssm_c_im[l], ssm_d[l])
        mix = (attn.reshape(b * seq, -1), ssm_pre.reshape(b * seq, -1), ssm_glu_w[l],
               ssm_glu_b[l], attn_out_norm[l], ssm_out_norm[l], w_out[l])
        h, = _ffn(h, norm_ffn2[l], ffn2_w_gate[l], ffn2_w_up[l], ffn2_w_down[l],
                  final_norm, final_norm=(l == depth - 1), mix=mix)
    return h.reshape(b, seq, d).astype(x.dtype)
```
